```python
import jax, jax.numpy as jnp
from jax import lax
import numpy as np

D_MODEL = 1024
BATCH = 8
SEQ = 4096
DEPTH = 1

CHUNK = 64
D_MIX = D_MODEL
D_CONV = D_MIX // 2
CONV_HEADS = 8
CONV_WIDTH = 3
D_POOL = D_MIX - D_CONV
POOL_WINDOWS = (2, 4, 8, 16)
POOL_GROUPS = len(POOL_WINDOWS)
POOL_GC = D_POOL // POOL_GROUPS
D_FF = 2816
EPS = 1e-6

kernel_name = "hybrid_conv_pool_macaron_block"


def rms_norm(x, g):
    xf = x.astype(jnp.float32)
    y = xf * lax.rsqrt(jnp.mean(xf * xf, axis=-1, keepdims=True) + EPS)
    return (y * g.astype(jnp.float32)).astype(x.dtype)


def swiglu(h, w_gate, w_up, w_down):
    return (jax.nn.silu(h @ w_gate) * (h @ w_up)) @ w_down


def causal_depthwise_conv(z, w):
    s = z.shape[1]
    zp = jnp.pad(z, ((0, 0), (CONV_WIDTH - 1, 0), (0, 0)))
    return sum(w[k] * zp[:, k:k + s] for k in range(CONV_WIDTH))


def multiscale_pool_minus_self(u):
    s = u.shape[1]
    uf = u.astype(jnp.float32)
    c = jnp.cumsum(uf, axis=1)
    t1 = jnp.arange(1, s + 1, dtype=jnp.float32)
    outs = []
    for g, w in enumerate(POOL_WINDOWS):
        cg = c[:, :, g]
        shifted = jnp.pad(cg, ((0, 0), (w, 0), (0, 0)))[:, :s]
        count = jnp.minimum(t1, float(w))[None, :, None]
        outs.append((cg - shifted) / count - uf[:, :, g])
    return jnp.stack(outs, axis=2).astype(u.dtype)


def _fwd_setup_inputs(seed: int = 0) -> dict:
    key = jax.random.key(seed)
    ks = jax.random.split(key, 20)
    f32 = jnp.float32

    def nrm(k, shape, fan_in):
        return jax.random.normal(k, shape, f32) * (fan_in ** -0.5)

    def gain(k, shape):
        return 1.0 + 0.02 * jax.random.normal(k, shape, f32)

    L = DEPTH
    return {
        "x": jax.random.normal(ks[0], (BATCH, SEQ, D_MODEL), f32),
        "norm_ffn1": gain(ks[1], (L, D_MODEL)),
        "ffn1_w_gate": nrm(ks[2], (L, D_MODEL, D_FF), D_MODEL),
        "ffn1_w_up": nrm(ks[3], (L, D_MODEL, D_FF), D_MODEL),
        "ffn1_w_down": nrm(ks[4], (L, D_FF, D_MODEL), D_FF),
        "norm_mix": gain(ks[5], (L, D_MODEL)),
        "w_in": nrm(ks[6], (L, D_MODEL, 3 * D_CONV + D_POOL), D_MODEL),
        "conv_w": nrm(ks[7], (L, CONV_WIDTH, D_CONV), CONV_WIDTH),
        "pool_w": nrm(ks[8], (L, POOL_GROUPS, POOL_GC, POOL_GC), POOL_GC),
        "pool_scale": gain(ks[9], (L, D_POOL)),
        "w_out": nrm(ks[10], (L, D_MIX, D_MODEL), D_MIX),
        "norm_ffn2": gain(ks[11], (L, D_MODEL)),
        "ffn2_w_gate": nrm(ks[12], (L, D_MODEL, D_FF), D_MODEL),
        "ffn2_w_up": nrm(ks[13], (L, D_MODEL, D_FF), D_MODEL),
        "ffn2_w_down": nrm(ks[14], (L, D_FF, D_MODEL), D_FF),
        "norm_final": gain(ks[15], (D_MODEL,)),
    }


def _fwd_reference(x, norm_ffn1, ffn1_w_gate, ffn1_w_up, ffn1_w_down, norm_mix, w_in, conv_w,
              pool_w, pool_scale, w_out, norm_ffn2, ffn2_w_gate, ffn2_w_up, ffn2_w_down,
              norm_final):
    b, s, _ = x.shape
    for l in range(DEPTH):
        x = x + 0.5 * swiglu(rms_norm(x, norm_ffn1[l]), ffn1_w_gate[l], ffn1_w_up[l], ffn1_w_down[l])

        h = rms_norm(x, norm_mix[l])
        proj = h @ w_in[l]
        v = proj[..., :D_CONV]
        gate_b = proj[..., D_CONV:2 * D_CONV]
        gate_c = proj[..., 2 * D_CONV:3 * D_CONV]
        u = proj[..., 3 * D_CONV:]

        y_a = gate_b * causal_depthwise_conv(gate_c * v, conv_w[l])

        ug = u.reshape(b, s, POOL_GROUPS, POOL_GC)
        pooled = multiscale_pool_minus_self(ug)
        y_b = jnp.einsum("bsgc,gcd->bsgd", pooled, pool_w[l]).reshape(b, s, D_POOL) * pool_scale[l]

        x = x + jnp.concatenate([y_a, y_b], axis=-1) @ w_out[l]

        x = x + 0.5 * swiglu(rms_norm(x, norm_ffn2[l]), ffn2_w_gate[l], ffn2_w_up[l], ffn2_w_down[l])
    return rms_norm(x, norm_final)


import jax as _jax
import jax.numpy as _jnp

TWIN_FORMAT = 'train_step'
FWD_PARAMS = ['x', 'norm_ffn1', 'ffn1_w_gate', 'ffn1_w_up', 'ffn1_w_down', 'norm_mix', 'w_in', 'conv_w', 'pool_w', 'pool_scale', 'w_out', 'norm_ffn2', 'ffn2_w_gate', 'ffn2_w_up', 'ffn2_w_down', 'norm_final']
TWIN_WEIGHTS = ['norm_ffn1', 'ffn1_w_gate', 'ffn1_w_up', 'ffn1_w_down', 'norm_mix', 'w_in', 'conv_w', 'pool_w', 'pool_scale', 'w_out', 'norm_ffn2', 'ffn2_w_gate', 'ffn2_w_up', 'ffn2_w_down', 'norm_final']
TWIN_DIFF_INPUT = 'x'
TWIN_INPUTS = ['x', 'norm_ffn1', 'ffn1_w_gate', 'ffn1_w_up', 'ffn1_w_down', 'norm_mix', 'w_in', 'conv_w', 'pool_w', 'pool_scale', 'w_out', 'norm_ffn2', 'ffn2_w_gate', 'ffn2_w_up', 'ffn2_w_down', 'norm_final', 'loss_target', 'm_norm_ffn1', 'm_ffn1_w_gate', 'm_ffn1_w_up', 'm_ffn1_w_down', 'm_norm_mix', 'm_w_in', 'm_conv_w', 'm_pool_w', 'm_pool_scale', 'm_w_out', 'm_norm_ffn2', 'm_ffn2_w_gate', 'm_ffn2_w_up', 'm_ffn2_w_down', 'm_norm_final', 'v_norm_ffn1', 'v_ffn1_w_gate', 'v_ffn1_w_up', 'v_ffn1_w_down', 'v_norm_mix', 'v_w_in', 'v_conv_w', 'v_pool_w', 'v_pool_scale', 'v_w_out', 'v_norm_ffn2', 'v_ffn2_w_gate', 'v_ffn2_w_up', 'v_ffn2_w_down', 'v_norm_final']
TWIN_OUTPUTS = ['loss', 'grad_x', 'grad_norm_ffn1', 'grad_ffn1_w_gate', 'grad_ffn1_w_up', 'grad_ffn1_w_down', 'grad_norm_mix', 'grad_w_in', 'grad_conv_w', 'grad_pool_w', 'grad_pool_scale', 'grad_w_out', 'grad_norm_ffn2', 'grad_ffn2_w_gate', 'grad_ffn2_w_up', 'grad_ffn2_w_down', 'grad_norm_final', 'delta_norm_ffn1', 'delta_ffn1_w_gate', 'delta_ffn1_w_up', 'delta_ffn1_w_down', 'delta_norm_mix', 'delta_w_in', 'delta_conv_w', 'delta_pool_w', 'delta_pool_scale', 'delta_w_out', 'delta_norm_ffn2', 'delta_ffn2_w_gate', 'delta_ffn2_w_up', 'delta_ffn2_w_down', 'delta_norm_final', 'new_m_norm_ffn1', 'new_m_ffn1_w_gate', 'new_m_ffn1_w_up', 'new_m_ffn1_w_down', 'new_m_norm_mix', 'new_m_w_in', 'new_m_conv_w', 'new_m_pool_w', 'new_m_pool_scale', 'new_m_w_out', 'new_m_norm_ffn2', 'new_m_ffn2_w_gate', 'new_m_ffn2_w_up', 'new_m_ffn2_w_down', 'new_m_norm_final', 'new_v_norm_ffn1', 'new_v_ffn1_w_gate', 'new_v_ffn1_w_up', 'new_v_ffn1_w_down', 'new_v_norm_mix', 'new_v_w_in', 'new_v_conv_w', 'new_v_pool_w', 'new_v_pool_scale', 'new_v_w_out', 'new_v_norm_ffn2', 'new_v_ffn2_w_gate', 'new_v_ffn2_w_up', 'new_v_ffn2_w_down', 'new_v_norm_final']
TWIN_LEAF_KINDS = {'loss': 'loss', 'grad_x': 'grad_x', 'grad_norm_ffn1': 'grad_w', 'grad_ffn1_w_gate': 'grad_w', 'grad_ffn1_w_up': 'grad_w', 'grad_ffn1_w_down': 'grad_w', 'grad_norm_mix': 'grad_w', 'grad_w_in': 'grad_w', 'grad_conv_w': 'grad_w', 'grad_pool_w': 'grad_w', 'grad_pool_scale': 'grad_w', 'grad_w_out': 'grad_w', 'grad_norm_ffn2': 'grad_w', 'grad_ffn2_w_gate': 'grad_w', 'grad_ffn2_w_up': 'grad_w', 'grad_ffn2_w_down': 'grad_w', 'grad_norm_final': 'grad_w', 'delta_norm_ffn1': 'delta_w', 'delta_ffn1_w_gate': 'delta_w', 'delta_ffn1_w_up': 'delta_w', 'delta_ffn1_w_down': 'delta_w', 'delta_norm_mix': 'delta_w', 'delta_w_in': 'delta_w', 'delta_conv_w': 'delta_w', 'delta_pool_w': 'delta_w', 'delta_pool_scale': 'delta_w', 'delta_w_out': 'delta_w', 'delta_norm_ffn2': 'delta_w', 'delta_ffn2_w_gate': 'delta_w', 'delta_ffn2_w_up': 'delta_w', 'delta_ffn2_w_down': 'delta_w', 'delta_norm_final': 'delta_w', 'new_m_norm_ffn1': 'new_m', 'new_m_ffn1_w_gate': 'new_m', 'new_m_ffn1_w_up': 'new_m', 'new_m_ffn1_w_down': 'new_m', 'new_m_norm_mix': 'new_m', 'new_m_w_in': 'new_m', 'new_m_conv_w': 'new_m', 'new_m_pool_w': 'new_m', 'new_m_pool_scale': 'new_m', 'new_m_w_out': 'new_m', 'new_m_norm_ffn2': 'new_m', 'new_m_ffn2_w_gate': 'new_m', 'new_m_ffn2_w_up': 'new_m', 'new_m_ffn2_w_down': 'new_m', 'new_m_norm_final': 'new_m', 'new_v_norm_ffn1': 'new_v', 'new_v_ffn1_w_gate': 'new_v', 'new_v_ffn1_w_up': 'new_v', 'new_v_ffn1_w_down': 'new_v', 'new_v_norm_mix': 'new_v', 'new_v_w_in': 'new_v', 'new_v_conv_w': 'new_v', 'new_v_pool_w': 'new_v', 'new_v_pool_scale': 'new_v', 'new_v_w_out': 'new_v', 'new_v_norm_ffn2': 'new_v', 'new_v_ffn2_w_gate': 'new_v', 'new_v_ffn2_w_up': 'new_v', 'new_v_ffn2_w_down': 'new_v', 'new_v_norm_final': 'new_v'}


def _forward(args):
    return _fwd_reference(*[args[k] for k in FWD_PARAMS])


def _output_shape():
    def fwd():
        inp = _fwd_setup_inputs(0)
        return _fwd_reference(*[inp[k] for k in FWD_PARAMS])
    out = _jax.eval_shape(fwd)
    return out.shape, out.dtype

N_MICROBATCH = 1
ADAM_LR = 0.001
ADAM_B1 = 0.9
ADAM_B2 = 0.999
ADAM_EPS = 1e-08
ADAM_WD = 0.01
ADAM_STEP = 10
PER_EXAMPLE_BATCH_AXIS = {'x': 0, 'loss_target': 0}
SHARED_INPUTS = []
_WEIGHT_DTYPES = {'norm_ffn1': _jnp.float32, 'ffn1_w_gate': _jnp.float32, 'ffn1_w_up': _jnp.float32, 'ffn1_w_down': _jnp.float32, 'norm_mix': _jnp.float32, 'w_in': _jnp.float32, 'conv_w': _jnp.float32, 'pool_w': _jnp.float32, 'pool_scale': _jnp.float32, 'w_out': _jnp.float32, 'norm_ffn2': _jnp.float32, 'ffn2_w_gate': _jnp.float32, 'ffn2_w_up': _jnp.float32, 'ffn2_w_down': _jnp.float32, 'norm_final': _jnp.float32}
MOMENT_SCALE = {'norm_ffn1': 9.264786e-02, 'ffn1_w_gate': 3.949821e-02, 'ffn1_w_up': 3.819385e-02, 'ffn1_w_down': 6.338899e-02, 'norm_mix': 1.766414e-01, 'w_in': 1.245374e-01, 'conv_w': 1.301845e-01, 'pool_w': 1.149522e-01, 'pool_scale': 1.318941e-01, 'w_out': 1.222070e-01, 'norm_ffn2': 5.290302e-02, 'ffn2_w_gate': 2.306568e-02, 'ffn2_w_up': 2.233306e-02, 'ffn2_w_down': 3.704414e-02, 'norm_final': 3.201442e+01}


def _to_microbatches(a, axis):
    t = _jnp.moveaxis(a, axis, 0)
    t = t.reshape((N_MICROBATCH, t.shape[0] // N_MICROBATCH) + t.shape[1:])
    return _jnp.moveaxis(t, 1, axis + 1)


def setup_inputs(seed: int = 0) -> dict:
    inp = _fwd_setup_inputs(seed)
    key = _jax.random.fold_in(_jax.random.key(seed), 7919)
    shape, _ = _output_shape()
    out = dict(inp)
    out["loss_target"] = _jax.random.normal(_jax.random.fold_in(key, 0), shape, _jnp.float32)
    for i, name in enumerate(TWIN_WEIGHTS):
        w = inp[name].astype(_jnp.float32)
        if MOMENT_SCALE is None:
            s = _jnp.sqrt(_jnp.mean(_jnp.square(w)) + 1e-30)
        else:
            s = MOMENT_SCALE[name]
        km, kv = _jax.random.split(_jax.random.fold_in(key, i + 1))
        out[name] = w
        out["m_" + name] = s * _jax.random.normal(km, w.shape, _jnp.float32)
        out["v_" + name] = (s * s) * _jax.random.uniform(kv, w.shape, _jnp.float32, 0.5, 1.5)
    if N_MICROBATCH > 1:
        for name, axis in PER_EXAMPLE_BATCH_AXIS.items():
            out[name] = _to_microbatches(out[name], axis)
    return {'x': out['x'], 'norm_ffn1': out['norm_ffn1'], 'ffn1_w_gate': out['ffn1_w_gate'], 'ffn1_w_up': out['ffn1_w_up'], 'ffn1_w_down': out['ffn1_w_down'], 'norm_mix': out['norm_mix'], 'w_in': out['w_in'], 'conv_w': out['conv_w'], 'pool_w': out['pool_w'], 'pool_scale': out['pool_scale'], 'w_out': out['w_out'], 'norm_ffn2': out['norm_ffn2'], 'ffn2_w_gate': out['ffn2_w_gate'], 'ffn2_w_up': out['ffn2_w_up'], 'ffn2_w_down': out['ffn2_w_down'], 'norm_final': out['norm_final'], 'loss_target': out['loss_target'], 'm_norm_ffn1': out['m_norm_ffn1'], 'm_ffn1_w_gate': out['m_ffn1_w_gate'], 'm_ffn1_w_up': out['m_ffn1_w_up'], 'm_ffn1_w_down': out['m_ffn1_w_down'], 'm_norm_mix': out['m_norm_mix'], 'm_w_in': out['m_w_in'], 'm_conv_w': out['m_conv_w'], 'm_pool_w': out['m_pool_w'], 'm_pool_scale': out['m_pool_scale'], 'm_w_out': out['m_w_out'], 'm_norm_ffn2': out['m_norm_ffn2'], 'm_ffn2_w_gate': out['m_ffn2_w_gate'], 'm_ffn2_w_up': out['m_ffn2_w_up'], 'm_ffn2_w_down': out['m_ffn2_w_down'], 'm_norm_final': out['m_norm_final'], 'v_norm_ffn1': out['v_norm_ffn1'], 'v_ffn1_w_gate': out['v_ffn1_w_gate'], 'v_ffn1_w_up': out['v_ffn1_w_up'], 'v_ffn1_w_down': out['v_ffn1_w_down'], 'v_norm_mix': out['v_norm_mix'], 'v_w_in': out['v_w_in'], 'v_conv_w': out['v_conv_w'], 'v_pool_w': out['v_pool_w'], 'v_pool_scale': out['v_pool_scale'], 'v_w_out': out['v_w_out'], 'v_norm_ffn2': out['v_norm_ffn2'], 'v_ffn2_w_gate': out['v_ffn2_w_gate'], 'v_ffn2_w_up': out['v_ffn2_w_up'], 'v_ffn2_w_down': out['v_ffn2_w_down'], 'v_norm_final': out['v_norm_final']}


def _loss(weights, diff, rest, loss_target):
    with _jax.named_scope("forward"):
        args = {**rest, TWIN_DIFF_INPUT: diff, **{k: w.astype(_WEIGHT_DTYPES[k]) for k, w in weights.items()}}
        y = _forward(args)
    with _jax.named_scope("loss_head"):
        err = _jnp.square(y.astype(_jnp.float32) - loss_target)
        return 0.5 * _jnp.sum(_jnp.mean(err, axis=-1)) if err.ndim else 0.5 * err


def _adamw(w, g, m, v):
    m = ADAM_B1 * m + (1.0 - ADAM_B1) * g
    v = ADAM_B2 * v + (1.0 - ADAM_B2) * _jnp.square(g)
    m_hat = m / (1.0 - ADAM_B1 ** ADAM_STEP)
    v_hat = v / (1.0 - ADAM_B2 ** ADAM_STEP)
    delta = -ADAM_LR * (m_hat / (_jnp.sqrt(v_hat) + ADAM_EPS) + ADAM_WD * w)
    return delta, m, v


def reference(x, norm_ffn1, ffn1_w_gate, ffn1_w_up, ffn1_w_down, norm_mix, w_in, conv_w, pool_w, pool_scale, w_out, norm_ffn2, ffn2_w_gate, ffn2_w_up, ffn2_w_down, norm_final, loss_target, m_norm_ffn1, m_ffn1_w_gate, m_ffn1_w_up, m_ffn1_w_down, m_norm_mix, m_w_in, m_conv_w, m_pool_w, m_pool_scale, m_w_out, m_norm_ffn2, m_ffn2_w_gate, m_ffn2_w_up, m_ffn2_w_down, m_norm_final, v_norm_ffn1, v_ffn1_w_gate, v_ffn1_w_up, v_ffn1_w_down, v_norm_mix, v_w_in, v_conv_w, v_pool_w, v_pool_scale, v_w_out, v_norm_ffn2, v_ffn2_w_gate, v_ffn2_w_up, v_ffn2_w_down, v_norm_final):
    given = dict(x=x, norm_ffn1=norm_ffn1, ffn1_w_gate=ffn1_w_gate, ffn1_w_up=ffn1_w_up, ffn1_w_down=ffn1_w_down, norm_mix=norm_mix, w_in=w_in, conv_w=conv_w, pool_w=pool_w, pool_scale=pool_scale, w_out=w_out, norm_ffn2=norm_ffn2, ffn2_w_gate=ffn2_w_gate, ffn2_w_up=ffn2_w_up, ffn2_w_down=ffn2_w_down, norm_final=norm_final, loss_target=loss_target, m_norm_ffn1=m_norm_ffn1, m_ffn1_w_gate=m_ffn1_w_gate, m_ffn1_w_up=m_ffn1_w_up, m_ffn1_w_down=m_ffn1_w_down, m_norm_mix=m_norm_mix, m_w_in=m_w_in, m_conv_w=m_conv_w, m_pool_w=m_pool_w, m_pool_scale=m_pool_scale, m_w_out=m_w_out, m_norm_ffn2=m_norm_ffn2, m_ffn2_w_gate=m_ffn2_w_gate, m_ffn2_w_up=m_ffn2_w_up, m_ffn2_w_down=m_ffn2_w_down, m_norm_final=m_norm_final, v_norm_ffn1=v_norm_ffn1, v_ffn1_w_gate=v_ffn1_w_gate, v_ffn1_w_up=v_ffn1_w_up, v_ffn1_w_down=v_ffn1_w_down, v_norm_mix=v_norm_mix, v_w_in=v_w_in, v_conv_w=v_conv_w, v_pool_w=v_pool_w, v_pool_scale=v_pool_scale, v_w_out=v_w_out, v_norm_ffn2=v_norm_ffn2, v_ffn2_w_gate=v_ffn2_w_gate, v_ffn2_w_up=v_ffn2_w_up, v_ffn2_w_down=v_ffn2_w_down, v_norm_final=v_norm_final)
    weights = {n: given[n] for n in TWIN_WEIGHTS}
    shared = {n: given[n] for n in SHARED_INPUTS}
    per_example = {n: given[n] for n in ['x']}
    grad_fn = _jax.value_and_grad(_loss, argnums=(0, 1))

    def one_microbatch(ex, loss_target):
        ex = dict(ex)
        diff = ex.pop(TWIN_DIFF_INPUT)
        return grad_fn(weights, diff, {**shared, **ex}, loss_target)

    if N_MICROBATCH == 1:
        loss, (grad_w, grad_x) = one_microbatch(per_example, given["loss_target"])
    else:
        def body(carry, xs):
            loss_sum, grad_sum = carry
            l_k, (gw_k, gx_k) = one_microbatch(xs[0], xs[1])
            with _jax.named_scope("update"):
                return (loss_sum + l_k, _jax.tree.map(_jnp.add, grad_sum, gw_k)), gx_k

        init = (_jnp.zeros((), _jnp.float32), _jax.tree.map(_jnp.zeros_like, weights))
        (loss, grad_w), grad_x = _jax.lax.scan(body, init, (per_example, given["loss_target"]))
    with _jax.named_scope("update"):
        delta_w, new_m, new_v = {}, {}, {}
        for n in TWIN_WEIGHTS:
            delta_w[n], new_m[n], new_v[n] = _adamw(weights[n], grad_w[n], given["m_" + n], given["v_" + n])
    return (loss, grad_x, *[grad_w[n] for n in TWIN_WEIGHTS], *[delta_w[n] for n in TWIN_WEIGHTS],
            *[new_m[n] for n in TWIN_WEIGHTS], *[new_v[n] for n in TWIN_WEIGHTS])
```

```python
import functools

import jax
import jax.numpy as jnp
from jax import lax
from jax.experimental import pallas as pl
from jax.experimental.pallas import tpu as pltpu

F32 = jnp.float32
LOW = jnp.bfloat16

N_DEV = 8
EPS = 1e-6
D_CONV = 512
POOL_WINDOWS = (2, 4, 8, 16)
POOL_GC = 128
HALO = 16
W_IN_SHARD = 256

ADAM_LR = 0.001
ADAM_B1 = 0.9
ADAM_B2 = 0.999
ADAM_EPS = 1e-08
ADAM_WD = 0.01
ADAM_STEP = 10

VMEM_LIMIT_BYTES = 56 * 1024 * 1024
TOK_TILE = 1024
MIX_TOK_TILE = 256
WGRAD_TOK_TILE = 2048
FF_TILE_CANDIDATES = (256, 128)
WGRAD_ROW_CANDIDATES = (256, 128)


def _params(*sem):
    return pltpu.CompilerParams(dimension_semantics=sem, vmem_limit_bytes=VMEM_LIMIT_BYTES)


def _pick(n, candidates):
    for c in candidates:
        if n % c == 0:
            return c
    raise ValueError(f"no tile in {candidates} divides {n}")


def _dot(a, b):
    return lax.dot_general(a, b, (((1,), (0,)), ((), ())), preferred_element_type=F32)


def _dot_nt(a, b):
    return lax.dot_general(a, b, (((1,), (1,)), ((), ())), preferred_element_type=F32)


def _dot_tn(a, b):
    return lax.dot_general(a, b, (((0,), (0,)), ((), ())), preferred_element_type=F32)


def _rms_scale(x):
    return lax.rsqrt(jnp.mean(x * x, axis=-1, keepdims=True) + EPS)


def _rms_bwd(dy, x, g):
    r = _rms_scale(x)
    xhat = x * r
    gdy = dy * g
    dx = r * (gdy - xhat * jnp.mean(gdy * xhat, axis=-1, keepdims=True))
    return dx, jnp.sum(dy * xhat, axis=0, keepdims=True)


def _exchange(arrays, sliced, name):
    n = len(arrays)
    out_shape = []
    for arr, sl in zip(arrays, sliced):
        shape = arr.shape if sl else (N_DEV,) + arr.shape
        out_shape.append(jax.ShapeDtypeStruct(shape, arr.dtype))

    def body(*refs):
        ins, outs = refs[:n], refs[n:2 * n]
        send_sems, recv_sems, local_sems = refs[2 * n:]
        mx, my, mc = lax.axis_index("x"), lax.axis_index("y"), lax.axis_index("c")
        me = 4 * mx + 2 * my + mc

        def peer(m):
            px = lax.rem(mx + ((m >> 2) & 1), 2)
            py = lax.rem(my + ((m >> 1) & 1), 2)
            pc = lax.rem(mc + (m & 1), 2)
            return (px, py, pc), 4 * px + 2 * py + pc

        def remote(a, m):
            pid, pflat = peer(m)
            return pltpu.make_async_remote_copy(
                src_ref=ins[a].at[pflat] if sliced[a] else ins[a],
                dst_ref=outs[a].at[me],
                send_sem=send_sems.at[a, m - 1],
                recv_sem=recv_sems.at[a, m - 1],
                device_id=pid,
                device_id_type=pl.DeviceIdType.MESH,
            )

        def arrival(a, m):
            pid, pflat = peer(m)
            return pltpu.make_async_remote_copy(
                src_ref=ins[a].at[pflat] if sliced[a] else ins[a],
                dst_ref=outs[a].at[pflat],
                send_sem=send_sems.at[a, m - 1],
                recv_sem=recv_sems.at[a, m - 1],
                device_id=pid,
                device_id_type=pl.DeviceIdType.MESH,
            )

        def local(a):
            return pltpu.make_async_copy(ins[a].at[me] if sliced[a] else ins[a], outs[a].at[me], local_sems.at[a])

        for a in range(n):
            local(a).start()
        for m in range(1, N_DEV):
            for a in range(n):
                remote(a, m).start()
        for m in range(1, N_DEV):
            for a in range(n):
                arrival(a, m).wait_recv()
        for m in range(1, N_DEV):
            for a in range(n):
                remote(a, m).wait_send()
        for a in range(n):
            local(a).wait()

    any_spec = pl.BlockSpec(memory_space=pl.ANY)
    return pl.pallas_call(
        body,
        name=name,
        out_shape=out_shape,
        in_specs=[any_spec] * n,
        out_specs=[any_spec] * n,
        scratch_shapes=[
            pltpu.SemaphoreType.DMA((n, N_DEV - 1)),
            pltpu.SemaphoreType.DMA((n, N_DEV - 1)),
            pltpu.SemaphoreType.DMA((n,)),
        ],
    )(*arrays)


def _ffn_fwd(x, g, wgt, wut, wd, name):
    s_len, d = x.shape
    f_len = wd.shape[0]
    ts = min(TOK_TILE, s_len)
    fc = _pick(f_len, FF_TILE_CANDIDATES)

    def body(x_ref, g_ref, wg_ref, wu_ref, wd_ref, xo_ref, h_ref, a_ref, b_ref):
        @pl.when(pl.program_id(1) == 0)
        def _():
            xv = x_ref[...]
            h_ref[...] = (xv * _rms_scale(xv) * g_ref[...]).astype(LOW)
            xo_ref[...] = xv

        hb = h_ref[...]
        a = _dot_nt(hb, wg_ref[...])
        b = _dot_nt(hb, wu_ref[...])
        s = a * jax.nn.sigmoid(a) * b
        xo_ref[...] += 0.5 * _dot(s.astype(LOW), wd_ref[...])
        a_ref[...] = a.astype(LOW)
        b_ref[...] = b.astype(LOW)

    tok = pl.BlockSpec((ts, d), lambda t, f: (t, 0))
    wspec = pl.BlockSpec((fc, d), lambda t, f: (f, 0))
    hid = pl.BlockSpec((ts, fc), lambda t, f: (t, f))
    return pl.pallas_call(
        body,
        name=name,
        grid=(s_len // ts, f_len // fc),
        in_specs=[tok, pl.BlockSpec((1, d), lambda t, f: (0, 0)), wspec, wspec, wspec],
        out_specs=[tok, tok, hid, hid],
        out_shape=[
            jax.ShapeDtypeStruct((s_len, d), F32),
            jax.ShapeDtypeStruct((s_len, d), LOW),
            jax.ShapeDtypeStruct((s_len, f_len), LOW),
            jax.ShapeDtypeStruct((s_len, f_len), LOW),
        ],
        compiler_params=_params("arbitrary", "arbitrary"),
    )(x, g, wgt, wut, wd)


def _ffn_dgrad(dxo, x, g, a, b, wgt, wut, wd, name):
    s_len, d = x.shape
    f_len = wd.shape[0]
    ts = min(TOK_TILE, s_len)
    fc = _pick(f_len, FF_TILE_CANDIDATES)
    n_f = f_len // fc

    def body(dxo_ref, x_ref, g_ref, a_ref, b_ref, wg_ref, wu_ref, wd_ref,
             dxi_ref, dg_ref, da_ref, db_ref, s_ref, df_ref):
        t, f = pl.program_id(0), pl.program_id(1)

        @pl.when((t == 0) & (f == 0))
        def _():
            dg_ref[...] = jnp.zeros_like(dg_ref)

        @pl.when(f == 0)
        def _():
            df_ref[...] = (0.5 * dxo_ref[...]).astype(LOW)
            dxi_ref[...] = jnp.zeros_like(dxi_ref)

        ds = _dot_nt(df_ref[...], wd_ref[...])
        av = a_ref[...].astype(F32)
        bv = b_ref[...].astype(F32)
        sig = jax.nn.sigmoid(av)
        silu = av * sig
        s_ref[...] = (silu * bv).astype(LOW)
        da = (ds * bv * (sig * (1.0 + av * (1.0 - sig)))).astype(LOW)
        db = (ds * silu).astype(LOW)
        da_ref[...] = da
        db_ref[...] = db
        dxi_ref[...] += _dot(da, wg_ref[...]) + _dot(db, wu_ref[...])

        @pl.when(f == n_f - 1)
        def _():
            dx, dgp = _rms_bwd(dxi_ref[...], x_ref[...], g_ref[...])
            dxi_ref[...] = dxo_ref[...] + dx
            dg_ref[...] += dgp

    tok = pl.BlockSpec((ts, d), lambda t, f: (t, 0))
    vec = pl.BlockSpec((1, d), lambda t, f: (0, 0))
    wspec = pl.BlockSpec((fc, d), lambda t, f: (f, 0))
    hid = pl.BlockSpec((ts, fc), lambda t, f: (t, f))
    return pl.pallas_call(
        body,
        name=name,
        grid=(s_len // ts, n_f),
        in_specs=[tok, tok, vec, hid, hid, wspec, wspec, wspec],
        out_specs=[tok, vec, hid, hid, hid, tok],
        out_shape=[
            jax.ShapeDtypeStruct((s_len, d), F32),
            jax.ShapeDtypeStruct((1, d), F32),
            jax.ShapeDtypeStruct((s_len, f_len), LOW),
            jax.ShapeDtypeStruct((s_len, f_len), LOW),
            jax.ShapeDtypeStruct((s_len, f_len), LOW),
            jax.ShapeDtypeStruct((s_len, d), LOW),
        ],
        compiler_params=_params("arbitrary", "arbitrary"),
    )(dxo, x, g, a, b, wgt, wut, wd)


def _ffn_wgrad(s, da, db, df, h, name):
    s_len, f_len = s.shape
    d = h.shape[1]
    tm = _pick(f_len, WGRAD_ROW_CANDIDATES)
    tk = min(WGRAD_TOK_TILE, s_len)
    n_k = s_len // tk

    def body(s_ref, da_ref, db_ref, df_ref, h_ref, dwd_ref, dwg_ref, dwu_ref, acc_d, acc_g, acc_u):
        k = pl.program_id(1)

        @pl.when(k == 0)
        def _():
            acc_d[...] = jnp.zeros_like(acc_d)
            acc_g[...] = jnp.zeros_like(acc_g)
            acc_u[...] = jnp.zeros_like(acc_u)

        hv = h_ref[...]
        acc_d[...] += _dot_tn(s_ref[...], df_ref[...])
        acc_g[...] += _dot_tn(da_ref[...], hv)
        acc_u[...] += _dot_tn(db_ref[...], hv)

        @pl.when(k == n_k - 1)
        def _():
            dwd_ref[...] = acc_d[...].astype(LOW)
            dwg_ref[...] = acc_g[...].astype(LOW)
            dwu_ref[...] = acc_u[...].astype(LOW)

    hid = pl.BlockSpec((tk, tm), lambda i, k: (k, i))
    tok = pl.BlockSpec((tk, d), lambda i, k: (k, 0))
    out = pl.BlockSpec((tm, d), lambda i, k: (i, 0))
    return pl.pallas_call(
        body,
        name=name,
        grid=(f_len // tm, n_k),
        in_specs=[hid, hid, hid, tok, tok],
        out_specs=[out, out, out],
        out_shape=[jax.ShapeDtypeStruct((f_len, d), LOW)] * 3,
        scratch_shapes=[pltpu.VMEM((tm, d), F32)] * 3,
        compiler_params=_params("arbitrary", "arbitrary"),
    )(s, da, db, df, h)


def _wgrad_tn(xm, ym, tn, stacked, name):
    s_len, m = xm.shape
    n = ym.shape[1]
    tk = min(WGRAD_TOK_TILE, s_len)
    n_k = s_len // tk

    def body(x_ref, y_ref, o_ref, acc):
        k = pl.program_id(1)

        @pl.when(k == 0)
        def _():
            acc[...] = jnp.zeros_like(acc)

        acc[...] += _dot_tn(x_ref[...].astype(LOW), y_ref[...].astype(LOW))

        @pl.when(k == n_k - 1)
        def _():
            o_ref[...] = acc[...].astype(LOW)

    if stacked:
        out_spec = pl.BlockSpec((None, m, tn), lambda j, k: (j, 0, 0))
        out_shape = jax.ShapeDtypeStruct((n // tn, m, tn), LOW)
    else:
        out_spec = pl.BlockSpec((m, tn), lambda j, k: (0, j))
        out_shape = jax.ShapeDtypeStruct((m, n), LOW)
    return pl.pallas_call(
        body,
        name=name,
        grid=(n // tn, n_k),
        in_specs=[pl.BlockSpec((tk, m), lambda j, k: (k, 0)), pl.BlockSpec((tk, tn), lambda j, k: (k, j))],
        out_specs=out_spec,
        out_shape=out_shape,
        scratch_shapes=[pltpu.VMEM((m, tn), F32)],
        compiler_params=_params("arbitrary", "arbitrary"),
    )(xm, ym)


def _mix_parts(ext_ref, cw, ts, row0):
    dc = D_CONV

    def back(off, c0, c1):
        return ext_ref[HALO - off:HALO - off + ts, c0:c1]

    v, gate_b, gate_c = back(0, 0, dc), back(0, dc, 2 * dc), back(0, 2 * dc, 3 * dc)
    z0 = gate_c * v
    z1 = back(1, 2 * dc, 3 * dc) * back(1, 0, dc)
    z2 = back(2, 2 * dc, 3 * dc) * back(2, 0, dc)
    conv = cw[2:3, :] * z0 + cw[1:2, :] * z1 + cw[0:1, :] * z2
    rows = row0 + lax.broadcasted_iota(jnp.int32, (ts, 1), 0)
    pooled, inv_count = [], []
    for grp, w in enumerate(POOL_WINDOWS):
        c0 = 3 * dc + POOL_GC * grp
        u = back(0, c0, c0 + POOL_GC)
        acc = u
        for j in range(1, w):
            acc = acc + back(j, c0, c0 + POOL_GC)
        inv = 1.0 / jnp.minimum(rows + 1, w).astype(F32)
        pooled.append(acc * inv - u)
        inv_count.append(inv)
    return v, gate_b, gate_c, z0, z1, z2, conv, pooled, inv_count


def _mix_fwd(x, g, w_in, conv_w, pool_w, pool_scale, w_out, name):
    s_len, d = x.shape
    n_blk, _, wcols = w_in.shape
    p_len = n_blk * wcols
    d_mix = w_out.shape[0]
    ts = min(MIX_TOK_TILE, s_len)
    dc = D_CONV

    def body(x_ref, g_ref, win_ref, cw_ref, pw_ref, ps_ref, wout_ref, x2_ref, h_ref, proj_ref, ext_ref, cat_ref):
        t = pl.program_id(0)

        @pl.when(t == 0)
        def _():
            ext_ref[0:HALO, :] = jnp.zeros((HALO, p_len), F32)

        xv = x_ref[...]
        hb = (xv * _rms_scale(xv) * g_ref[...]).astype(LOW)
        h_ref[...] = hb
        for k in range(n_blk):
            ext_ref[HALO:HALO + ts, wcols * k:wcols * (k + 1)] = _dot(hb, win_ref[k])
        proj_ref[...] = ext_ref[HALO:HALO + ts, :]

        _, gate_b, _, _, _, _, conv, pooled, _ = _mix_parts(ext_ref, cw_ref[...], ts, t * ts)
        cat_ref[:, 0:dc] = (gate_b * conv).astype(LOW)
        for grp in range(len(POOL_WINDOWS)):
            c0 = POOL_GC * grp
            lin = _dot(pooled[grp].astype(LOW), pw_ref[grp])
            cat_ref[:, dc + c0:dc + c0 + POOL_GC] = (lin * ps_ref[:, c0:c0 + POOL_GC]).astype(LOW)
        x2_ref[...] = xv + _dot(cat_ref[...], wout_ref[...])
        ext_ref[0:HALO, :] = ext_ref[ts:ts + HALO, :]

    tok = pl.BlockSpec((ts, d), lambda t: (t, 0))

    def whole(arr):
        return pl.BlockSpec(arr.shape, lambda t: (0,) * arr.ndim)

    return pl.pallas_call(
        body,
        name=name,
        grid=(s_len // ts,),
        in_specs=[tok, whole(g), whole(w_in), whole(conv_w), whole(pool_w), whole(pool_scale), whole(w_out)],
        out_specs=[tok, tok, pl.BlockSpec((ts, p_len), lambda t: (t, 0))],
        out_shape=[
            jax.ShapeDtypeStruct((s_len, d), F32),
            jax.ShapeDtypeStruct((s_len, d), LOW),
            jax.ShapeDtypeStruct((s_len, p_len), F32),
        ],
        scratch_shapes=[pltpu.VMEM((ts + HALO, p_len), F32), pltpu.VMEM((ts, d_mix), LOW)],
        compiler_params=_params("arbitrary"),
    )(x, g, w_in, conv_w, pool_w, pool_scale, w_out)


def _mix_bwd(dx2, x, proj, g, w_in, conv_w, pool_w, pool_scale, w_out, name):
    s_len, d = x.shape
    n_blk, _, wcols = w_in.shape
    p_len = n_blk * wcols
    d_mix = w_out.shape[0]
    ts = min(MIX_TOK_TILE, s_len)
    n_t = s_len // ts
    dc = D_CONV
    n_grp = len(POOL_WINDOWS)

    def body(dx2_ref, x_ref, proj_ref, halo_ref, g_ref, win_ref, cw_ref, pw_ref, ps_ref, wout_ref,
             dx_ref, dproj_ref, cat_ref, dg_ref, dcw_ref, dpw_ref, dps_ref, ext_ref, fut_ref):
        i = pl.program_id(0)
        t = n_t - 1 - i

        @pl.when(i == 0)
        def _():
            dg_ref[...] = jnp.zeros_like(dg_ref)
            dcw_ref[...] = jnp.zeros_like(dcw_ref)
            dpw_ref[...] = jnp.zeros_like(dpw_ref)
            dps_ref[...] = jnp.zeros_like(dps_ref)
            fut_ref[ts:ts + HALO, :] = jnp.zeros((HALO, d_mix), F32)

        ext_ref[HALO:HALO + ts, :] = proj_ref[...]

        @pl.when(t == 0)
        def _():
            ext_ref[0:HALO, :] = jnp.zeros((HALO, p_len), F32)

        @pl.when(t > 0)
        def _():
            ext_ref[0:HALO, :] = halo_ref[...]

        cw = cw_ref[...]
        v, gate_b, gate_c, z0, z1, z2, conv, pooled, inv_count = _mix_parts(ext_ref, cw, ts, t * ts)
        dx2 = dx2_ref[...]
        dcat = _dot_nt(dx2.astype(LOW), wout_ref[...])

        dy_a = dcat[:, 0:dc]
        dconv = dy_a * gate_b
        fut_ref[0:ts, 0:dc] = dconv
        cat_ref[:, 0:dc] = (gate_b * conv).astype(LOW)
        dproj_ref[:, dc:2 * dc] = (dy_a * conv).astype(LOW)
        dcw_ref[2:3, :] += jnp.sum(dconv * z0, axis=0, keepdims=True)
        dcw_ref[1:2, :] += jnp.sum(dconv * z1, axis=0, keepdims=True)
        dcw_ref[0:1, :] += jnp.sum(dconv * z2, axis=0, keepdims=True)

        dpool = []
        for grp in range(n_grp):
            c0 = POOL_GC * grp
            pooled_b = pooled[grp].astype(LOW)
            lin = _dot(pooled_b, pw_ref[grp])
            dy_b = dcat[:, dc + c0:dc + c0 + POOL_GC]
            scale = ps_ref[:, c0:c0 + POOL_GC]
            cat_ref[:, dc + c0:dc + c0 + POOL_GC] = (lin * scale).astype(LOW)
            dps_ref[:, c0:c0 + POOL_GC] += jnp.sum(dy_b * lin, axis=0, keepdims=True)
            dlin = (dy_b * scale).astype(LOW)
            dpw_ref[grp] += _dot_tn(pooled_b, dlin)
            dpool.append(_dot_nt(dlin, pw_ref[grp]))
            fut_ref[0:ts, dc + c0:dc + c0 + POOL_GC] = dpool[grp] * inv_count[grp]

        def ahead(off, c0, c1):
            return fut_ref[off:off + ts, c0:c1]

        dz = cw[2:3, :] * ahead(0, 0, dc) + cw[1:2, :] * ahead(1, 0, dc) + cw[0:1, :] * ahead(2, 0, dc)
        dproj_ref[:, 0:dc] = (dz * gate_c).astype(LOW)
        dproj_ref[:, 2 * dc:3 * dc] = (dz * v).astype(LOW)
        for grp, w in enumerate(POOL_WINDOWS):
            c0 = dc + POOL_GC * grp
            acc = ahead(0, c0, c0 + POOL_GC)
            for j in range(1, w):
                acc = acc + ahead(j, c0, c0 + POOL_GC)
            dproj_ref[:, 2 * dc + c0:2 * dc + c0 + POOL_GC] = (acc - dpool[grp]).astype(LOW)

        dh = _dot_nt(dproj_ref[:, 0:wcols], win_ref[0])
        for k in range(1, n_blk):
            dh += _dot_nt(dproj_ref[:, wcols * k:wcols * (k + 1)], win_ref[k])
        dx, dgp = _rms_bwd(dh, x_ref[...], g_ref[...])
        dx_ref[...] = dx2 + dx
        dg_ref[...] += dgp
        fut_ref[ts:ts + HALO, :] = fut_ref[0:HALO, :]

    tok = pl.BlockSpec((ts, d), lambda i: (n_t - 1 - i, 0))
    halo = pl.BlockSpec((HALO, p_len), lambda i: (jnp.maximum((n_t - 1 - i) * (ts // HALO) - 1, 0), 0))

    def whole(arr):
        return pl.BlockSpec(arr.shape, lambda i: (0,) * arr.ndim)

    return pl.pallas_call(
        body,
        name=name,
        grid=(n_t,),
        in_specs=[tok, tok, pl.BlockSpec((ts, p_len), lambda i: (n_t - 1 - i, 0)), halo,
                  whole(g), whole(w_in), whole(conv_w), whole(pool_w), whole(pool_scale), whole(w_out)],
        out_specs=[tok, pl.BlockSpec((ts, p_len), lambda i: (n_t - 1 - i, 0)),
                   pl.BlockSpec((ts, d_mix), lambda i: (n_t - 1 - i, 0)),
                   whole(g), whole(conv_w), whole(pool_w), whole(pool_scale)],
        out_shape=[
            jax.ShapeDtypeStruct((s_len, d), F32),
            jax.ShapeDtypeStruct((s_len, p_len), LOW),
            jax.ShapeDtypeStruct((s_len, d_mix), LOW),
            jax.ShapeDtypeStruct(g.shape, F32),
            jax.ShapeDtypeStruct(conv_w.shape, F32),
            jax.ShapeDtypeStruct(pool_w.shape, F32),
            jax.ShapeDtypeStruct(pool_scale.shape, F32),
        ],
        scratch_shapes=[pltpu.VMEM((ts + HALO, p_len), F32), pltpu.VMEM((ts + HALO, d_mix), F32)],
        compiler_params=_params("arbitrary"),
    )(dx2, x, proj, proj, g, w_in, conv_w, pool_w, pool_scale, w_out)


def _loss_head(x, g, target, name):
    s_len, d = x.shape
    ts = min(TOK_TILE, s_len)

    def body(x_ref, g_ref, tgt_ref, loss_ref, dx_ref, dg_ref):
        @pl.when(pl.program_id(0) == 0)
        def _():
            loss_ref[...] = jnp.zeros_like(loss_ref)
            dg_ref[...] = jnp.zeros_like(dg_ref)

        xv, gv = x_ref[...], g_ref[...]
        err = xv * _rms_scale(xv) * gv - tgt_ref[...]
        loss_ref[...] += 0.5 * jnp.sum(jnp.mean(err * err, axis=-1, keepdims=True), axis=0, keepdims=True)
        dx, dgp = _rms_bwd(err * (1.0 / d), xv, gv)
        dx_ref[...] = dx
        dg_ref[...] += dgp

    tok = pl.BlockSpec((ts, d), lambda t: (t, 0))
    vec = pl.BlockSpec((1, d), lambda t: (0, 0))
    return pl.pallas_call(
        body,
        name=name,
        grid=(s_len // ts,),
        in_specs=[tok, vec, tok],
        out_specs=[pl.BlockSpec((1, 128), lambda t: (0, 0)), tok, vec],
        out_shape=[
            jax.ShapeDtypeStruct((1, 128), F32),
            jax.ShapeDtypeStruct((s_len, d), F32),
            jax.ShapeDtypeStruct((1, d), F32),
        ],
        compiler_params=_params("arbitrary"),
    )(x, g, target)


def _row_tile(rows, cols, stack_bytes):
    budget = 20 * 1024 * 1024
    per_row = cols * (4 * 7 + stack_bytes)
    for tr in (rows, 512, 256, 176, 128, 64, 32, 16, 8):
        if rows % tr == 0 and tr % 8 == 0 and tr * per_row * 2 <= budget:
            return tr
    return rows


def _sum_stack(stack, name):
    n, r, c = stack.shape
    tr = _row_tile(r, c, n * stack.dtype.itemsize)

    def body(s_ref, o_ref):
        acc = s_ref[0].astype(F32)
        for k in range(1, n):
            acc = acc + s_ref[k].astype(F32)
        o_ref[...] = acc

    return pl.pallas_call(
        body,
        name=name,
        grid=(r // tr,),
        in_specs=[pl.BlockSpec((n, tr, c), lambda i: (0, i, 0))],
        out_specs=pl.BlockSpec((tr, c), lambda i: (i, 0)),
        out_shape=jax.ShapeDtypeStruct((r, c), F32),
        compiler_params=_params("arbitrary"),
    )(stack)


def _adamw(stack, w, m, v, name):
    n, r, c = stack.shape
    tr = _row_tile(r, c, n * stack.dtype.itemsize)
    c1 = 1.0 - ADAM_B1 ** ADAM_STEP
    c2 = 1.0 - ADAM_B2 ** ADAM_STEP

    def body(s_ref, w_ref, m_ref, v_ref, g_ref, d_ref, mo_ref, vo_ref):
        gv = s_ref[0].astype(F32)
        for k in range(1, n):
            gv = gv + s_ref[k].astype(F32)
        mn = ADAM_B1 * m_ref[...] + (1.0 - ADAM_B1) * gv
        vn = ADAM_B2 * v_ref[...] + (1.0 - ADAM_B2) * (gv * gv)
        g_ref[...] = gv
        mo_ref[...] = mn
        vo_ref[...] = vn
        d_ref[...] = -ADAM_LR * ((mn / c1) / (jnp.sqrt(vn / c2) + ADAM_EPS) + ADAM_WD * w_ref[...])

    blk = pl.BlockSpec((tr, c), lambda i: (i, 0))
    return pl.pallas_call(
        body,
        name=name,
        grid=(r // tr,),
        in_specs=[pl.BlockSpec((n, tr, c), lambda i: (0, i, 0)), blk, blk, blk],
        out_specs=[blk] * 4,
        out_shape=[jax.ShapeDtypeStruct((r, c), F32)] * 4,
        compiler_params=_params("arbitrary"),
    )(stack, w, m, v)


def _to_sheet(parts):
    sheets, spans = [], []
    row = 0
    for p in parts:
        flat = p.reshape(-1).astype(F32)
        rows = -(-flat.shape[0] // 1024) * 8
        flat = jnp.pad(flat, (0, rows * 128 - flat.shape[0]))
        sheets.append(flat.reshape(rows, 128))
        spans.append((row, p.size, p.shape))
        row += rows
    return jnp.concatenate(sheets, axis=0), spans


def _from_sheet(sheet, spans):
    out = []
    for row, size, shape in spans:
        rows = -(-size // 1024) * 8
        out.append(sheet[row:row + rows].reshape(-1)[:size].reshape(shape))
    return out


def kernel(x, norm_ffn1, ffn1_w_gate, ffn1_w_up, ffn1_w_down, norm_mix, w_in, conv_w, pool_w, pool_scale, w_out, norm_ffn2, ffn2_w_gate, ffn2_w_up, ffn2_w_down, norm_final, loss_target, m_norm_ffn1, m_ffn1_w_gate, m_ffn1_w_up, m_ffn1_w_down, m_norm_mix, m_w_in, m_conv_w, m_pool_w, m_pool_scale, m_w_out, m_norm_ffn2, m_ffn2_w_gate, m_ffn2_w_up, m_ffn2_w_down, m_norm_final, v_norm_ffn1, v_ffn1_w_gate, v_ffn1_w_up, v_ffn1_w_down, v_norm_mix, v_w_in, v_conv_w, v_pool_w, v_pool_scale, v_w_out, v_norm_ffn2, v_ffn2_w_gate, v_ffn2_w_up, v_ffn2_w_down, v_norm_final):
    me = 4 * lax.axis_index("x") + 2 * lax.axis_index("y") + lax.axis_index("c")
    xs, tgt = x[0], loss_target[0]
    s_len, d = xs.shape
    f_shard = ffn1_w_down.shape[1]
    f_len = N_DEV * f_shard
    conv_shard = conv_w.shape[2]

    def low_t(wt):
        return wt[0].T.astype(LOW)

    conv_tile = jnp.zeros((8, 128), F32).at[0:conv_w.shape[1], 0:conv_shard].set(conv_w[0])
    shards = [low_t(ffn1_w_gate), low_t(ffn1_w_up), ffn1_w_down[0].astype(LOW),
              w_in[0].astype(LOW), w_out[0].astype(LOW),
              low_t(ffn2_w_gate), low_t(ffn2_w_up), ffn2_w_down[0].astype(LOW), conv_tile]
    full = _exchange(shards, [False] * len(shards), "gather_weights")
    wg1, wu1, wd1 = (full[k].reshape(f_len, d) for k in (0, 1, 2))
    w_in_full = full[3]
    w_out_full = full[4].reshape(-1, d)
    wg2, wu2, wd2 = (full[k].reshape(f_len, d) for k in (5, 6, 7))
    conv_full = jnp.concatenate([full[8][k, 0:conv_w.shape[1], 0:conv_shard] for k in range(N_DEV)], axis=1)
    pool_w_low = pool_w[0].astype(LOW)
    pool_scale2 = pool_scale

    x1, h1, a1, b1 = _ffn_fwd(xs, norm_ffn1, wg1, wu1, wd1, "ffn1_fwd")
    x2, h2, proj = _mix_fwd(x1, norm_mix, w_in_full, conv_full, pool_w_low, pool_scale2, w_out_full, "mix_fwd")
    x3, h3, a2, b2 = _ffn_fwd(x2, norm_ffn2, wg2, wu2, wd2, "ffn2_fwd")
    loss_row, dx3, dg_final = _loss_head(x3, norm_final.reshape(1, d), tgt, "loss_head")

    dx2, dg_ffn2, da, db, sv, df = _ffn_dgrad(dx3, x2, norm_ffn2, a2, b2, wg2, wu2, wd2, "ffn2_dgrad")
    dwd2, dwg2, dwu2 = _ffn_wgrad(sv, da, db, df, h3, "ffn2_wgrad")
    dx1, dproj, cat, dg_mix, dconv, dpool_w, dpool_scale = _mix_bwd(
        dx2, x1, proj, norm_mix, w_in_full, conv_full, pool_w_low, pool_scale2, w_out_full, "mix_bwd")
    dw_in = _wgrad_tn(h2, dproj, W_IN_SHARD, True, "w_in_wgrad")
    dw_out = _wgrad_tn(cat, dx2, d, False, "w_out_wgrad")
    dx0, dg_ffn1, da, db, sv, df = _ffn_dgrad(dx1, xs, norm_ffn1, a1, b1, wg1, wu1, wd1, "ffn1_dgrad")
    dwd1, dwg1, dwu1 = _ffn_wgrad(sv, da, db, df, h1, "ffn1_wgrad")

    small_parts = [dg_ffn1, dg_mix, dg_ffn2, dg_final, dconv, dpool_w, dpool_scale]
    small_sheet, spans = _to_sheet(small_parts)

    def by_dev(gw):
        return gw.reshape(N_DEV, -1, d)

    partials = [by_dev(dwg1), by_dev(dwu1), by_dev(dwd1), dw_in, by_dev(dw_out),
                by_dev(dwg2), by_dev(dwu2), by_dev(dwd2), small_sheet]
    got = _exchange(partials, [True] * 8 + [False], "scatter_grads")

    outs = {}

    def update(name, stack, w, m, v):
        outs[name] = _adamw(stack, w[0], m[0], v[0], "adamw_" + name)

    def update_t(name, stack, w, m, v):
        g = _sum_stack(stack, "sum_" + name).T
        outs[name] = _adamw(g[None], w[0], m[0], v[0], "adamw_" + name)

    update_t("ffn1_w_gate", got[0], ffn1_w_gate, m_ffn1_w_gate, v_ffn1_w_gate)
    update_t("ffn1_w_up", got[1], ffn1_w_up, m_ffn1_w_up, v_ffn1_w_up)
    update("ffn1_w_down", got[2], ffn1_w_down, m_ffn1_w_down, v_ffn1_w_down)
    update("w_in", got[3], w_in, m_w_in, v_w_in)
    update("w_out", got[4], w_out, m_w_out, v_w_out)
    update_t("ffn2_w_gate", got[5], ffn2_w_gate, m_ffn2_w_gate, v_ffn2_w_gate)
    update_t("ffn2_w_up", got[6], ffn2_w_up, m_ffn2_w_up, v_ffn2_w_up)
    update("ffn2_w_down", got[7], ffn2_w_down, m_ffn2_w_down, v_ffn2_w_down)

    g_small = _from_sheet(_sum_stack(got[8], "sum_small"), spans)
    g_conv = lax.dynamic_slice_in_dim(g_small[4], me * conv_shard, conv_shard, axis=1)
    small_names = ["norm_ffn1", "norm_mix", "norm_ffn2", "norm_final", "conv_w", "pool_w", "pool_scale"]
    small_g = [g_small[0], g_small[1], g_small[2], g_small[3].reshape(norm_final.shape), g_conv[None],
               g_small[5][None], g_small[6]]
    small_w = [norm_ffn1, norm_mix, norm_ffn2, norm_final, conv_w, pool_w, pool_scale]
    small_m = [m_norm_ffn1, m_norm_mix, m_norm_ffn2, m_norm_final, m_conv_w, m_pool_w, m_pool_scale]
    small_v = [v_norm_ffn1, v_norm_mix, v_norm_ffn2, v_norm_final, v_conv_w, v_pool_w, v_pool_scale]
    g_sheet, spans_u = _to_sheet(small_g)
    w_sheet, _ = _to_sheet(small_w)
    m_sheet, _ = _to_sheet(small_m)
    v_sheet, _ = _to_sheet(small_v)
    upd = _adamw(g_sheet[None], w_sheet, m_sheet, v_sheet, "adamw_small")
    small_out = [_from_sheet(u, spans_u) for u in upd]
    for k, nm in enumerate(small_names):
        outs[nm] = tuple(small_out[j][k] for j in range(4))

    loss = lax.psum(loss_row[0, 0], ("x", "y", "c"))
    order = ["norm_ffn1", "ffn1_w_gate", "ffn1_w_up", "ffn1_w_down", "norm_mix", "w_in", "conv_w", "pool_w",
             "pool_scale", "w_out", "norm_ffn2", "ffn2_w_gate", "ffn2_w_up", "ffn2_w_down", "norm_final"]
    big = {"ffn1_w_gate", "ffn1_w_up", "ffn1_w_down", "w_in", "w_out", "ffn2_w_gate", "ffn2_w_up", "ffn2_w_down"}

    def leaf(nm, j):
        val = outs[nm][j]
        return val[None] if nm in big else val

    return (loss, dx0[None],
            *[leaf(nm, 0) for nm in order], *[leaf(nm, 1) for nm in order],
            *[leaf(nm, 2) for nm in order], *[leaf(nm, 3) for nm in order])
```

```python
import jax
import jax.numpy as jnp
from jax import lax
from jax.experimental import pallas as pl
from jax.experimental.pallas import tpu as pltpu

F32 = jnp.float32
LOW = jnp.bfloat16

N_DEV = 8
EPS = 1e-6
D_CONV = 512
POOL_WINDOWS = (2, 4, 8, 16)
POOL_GC = 128
HALO = 16
W_IN_SHARD = 256

ADAM_LR = 0.001
ADAM_B1 = 0.9
ADAM_B2 = 0.999
ADAM_EPS = 1e-08
ADAM_WD = 0.01
ADAM_STEP = 10

VMEM_LIMIT_BYTES = 56 * 1024 * 1024
TOK_TILE = 1024
MIX_TOK_TILE = 256
WGRAD_TOK_TILE = 2048
FF_TILE_CANDIDATES = (256, 128)
WGRAD_ROW_CANDIDATES = (256, 128)


def _params(*sem):
    return pltpu.CompilerParams(dimension_semantics=sem, vmem_limit_bytes=VMEM_LIMIT_BYTES)


def _pick(n, candidates):
    for c in candidates:
        if n % c == 0:
            return c
    raise ValueError(f"no tile in {candidates} divides {n}")


def _dot(a, b):
    return lax.dot_general(a, b, (((1,), (0,)), ((), ())), preferred_element_type=F32)


def _dot_nt(a, b):
    return lax.dot_general(a, b, (((1,), (1,)), ((), ())), preferred_element_type=F32)


def _dot_tn(a, b):
    return lax.dot_general(a, b, (((0,), (0,)), ((), ())), preferred_element_type=F32)


def _rms_scale(x):
    return lax.rsqrt(jnp.mean(x * x, axis=-1, keepdims=True) + EPS)


def _rms_bwd(dy, x, g):
    r = _rms_scale(x)
    xhat = x * r
    gdy = dy * g
    dx = r * (gdy - xhat * jnp.mean(gdy * xhat, axis=-1, keepdims=True))
    return dx, jnp.sum(dy * xhat, axis=0, keepdims=True)


class _Exchange:
    def __init__(self, arrays, sliced):
        self.arrays, self.sliced, self.n = list(arrays), list(sliced), len(arrays)
        self.out_shape = [jax.ShapeDtypeStruct(arr.shape if sl else (N_DEV,) + arr.shape, arr.dtype)
                          for arr, sl in zip(arrays, sliced)]
        self.specs = [pl.BlockSpec(memory_space=pl.ANY)] * self.n
        self.scratch_shapes = [pltpu.SemaphoreType.DMA((self.n, N_DEV - 1)),
                               pltpu.SemaphoreType.DMA((self.n, N_DEV - 1)),
                               pltpu.SemaphoreType.DMA((self.n,))]

    def _copies(self, ins, outs, sems):
        send_sems, recv_sems, local_sems = sems
        sliced = self.sliced
        mx, my, mc = lax.axis_index("x"), lax.axis_index("y"), lax.axis_index("c")
        me = 4 * mx + 2 * my + mc

        def peer(m):
            px = lax.rem(mx + ((m >> 2) & 1), 2)
            py = lax.rem(my + ((m >> 1) & 1), 2)
            pc = lax.rem(mc + (m & 1), 2)
            return (px, py, pc), 4 * px + 2 * py + pc

        def remote(a, m, arriving):
            pid, pflat = peer(m)
            return pltpu.make_async_remote_copy(
                src_ref=ins[a].at[pflat] if sliced[a] else ins[a],
                dst_ref=outs[a].at[pflat if arriving else me],
                send_sem=send_sems.at[a, m - 1],
                recv_sem=recv_sems.at[a, m - 1],
                device_id=pid,
                device_id_type=pl.DeviceIdType.MESH,
            )

        def local(a):
            return pltpu.make_async_copy(ins[a].at[me] if sliced[a] else ins[a], outs[a].at[me], local_sems.at[a])

        return remote, local

    def start(self, ins, outs, sems):
        remote, local = self._copies(ins, outs, sems)
        for a in range(self.n):
            local(a).start()
        for m in range(1, N_DEV):
            for a in range(self.n):
                remote(a, m, False).start()

    def wait(self, ins, outs, sems):
        remote, local = self._copies(ins, outs, sems)
        for m in range(1, N_DEV):
            for a in range(self.n):
                remote(a, m, True).wait_recv()
        for m in range(1, N_DEV):
            for a in range(self.n):
                remote(a, m, False).wait_send()
        for a in range(self.n):
            local(a).wait()


def _exchange(arrays, sliced, name):
    ex = _Exchange(arrays, sliced)

    def body(*refs):
        ins, outs, sems = refs[:ex.n], refs[ex.n:2 * ex.n], refs[2 * ex.n:]
        ex.start(ins, outs, sems)
        ex.wait(ins, outs, sems)

    return pl.pallas_call(body, name=name, out_shape=ex.out_shape, in_specs=ex.specs, out_specs=ex.specs,
                          scratch_shapes=ex.scratch_shapes)(*arrays)


def _call(body, *, name, grid, in_specs, out_specs, out_shape, args, scratch_shapes=(), exchange=None):
    params = _params(*(("arbitrary",) * len(grid)))
    if exchange is None:
        return pl.pallas_call(body, name=name, grid=grid, in_specs=in_specs, out_specs=out_specs, out_shape=out_shape,
                              scratch_shapes=list(scratch_shapes), compiler_params=params)(*args)
    ex = exchange
    n_in, n_out, n_scr = len(in_specs), len(out_specs), len(scratch_shapes)
    n_steps = 1
    for g in grid:
        n_steps *= g

    def hosted(*refs):
        ins, refs = refs[:n_in], refs[n_in:]
        ex_ins, refs = refs[:ex.n], refs[ex.n:]
        outs, refs = refs[:n_out], refs[n_out:]
        ex_outs, refs = refs[:ex.n], refs[ex.n:]
        scr, sems = refs[:n_scr], refs[n_scr:]
        step = pl.program_id(0)
        for ax in range(1, len(grid)):
            step = step * grid[ax] + pl.program_id(ax)

        @pl.when(step == 0)
        def _():
            ex.start(ex_ins, ex_outs, sems)

        body(*ins, *outs, *scr)

        @pl.when(step == n_steps - 1)
        def _():
            ex.wait(ex_ins, ex_outs, sems)

    return pl.pallas_call(
        hosted, name=name, grid=grid,
        in_specs=list(in_specs) + ex.specs, out_specs=list(out_specs) + ex.specs,
        out_shape=list(out_shape) + ex.out_shape,
        scratch_shapes=list(scratch_shapes) + ex.scratch_shapes,
        compiler_params=params)(*args, *ex.arrays)


def _ffn_fwd(x, g, wgt, wut, wd, name, exchange=None):
    s_len, d = x.shape
    f_len = wd.shape[0]
    ts = min(TOK_TILE, s_len)
    fc = _pick(f_len, FF_TILE_CANDIDATES)

    def body(x_ref, g_ref, wg_ref, wu_ref, wd_ref, xo_ref, h_ref, a_ref, b_ref):
        @pl.when(pl.program_id(1) == 0)
        def _():
            xv = x_ref[...]
            h_ref[...] = (xv * _rms_scale(xv) * g_ref[...]).astype(LOW)
            xo_ref[...] = xv

        hb = h_ref[...]
        a = _dot_nt(hb, wg_ref[...])
        b = _dot_nt(hb, wu_ref[...])
        s = a * jax.nn.sigmoid(a) * b
        xo_ref[...] += 0.5 * _dot(s.astype(LOW), wd_ref[...])
        a_ref[...] = a.astype(LOW)
        b_ref[...] = b.astype(LOW)

    tok = pl.BlockSpec((ts, d), lambda t, f: (t, 0))
    wspec = pl.BlockSpec((fc, d), lambda t, f: (f, 0))
    hid = pl.BlockSpec((ts, fc), lambda t, f: (t, f))
    return _call(
        body,
        name=name,
        grid=(s_len // ts, f_len // fc),
        in_specs=[tok, pl.BlockSpec((1, d), lambda t, f: (0, 0)), wspec, wspec, wspec],
        out_specs=[tok, tok, hid, hid],
        out_shape=[
            jax.ShapeDtypeStruct((s_len, d), F32),
            jax.ShapeDtypeStruct((s_len, d), LOW),
            jax.ShapeDtypeStruct((s_len, f_len), LOW),
            jax.ShapeDtypeStruct((s_len, f_len), LOW),
        ],
        args=(x, g, wgt, wut, wd),
        exchange=exchange,
    )


def _ffn_dgrad(dxo, x, g, a, b, wgt, wut, wd, name, exchange=None):
    s_len, d = x.shape
    f_len = wd.shape[0]
    ts = min(TOK_TILE, s_len)
    fc = _pick(f_len, FF_TILE_CANDIDATES)
    n_f = f_len // fc

    def body(dxo_ref, x_ref, g_ref, a_ref, b_ref, wg_ref, wu_ref, wd_ref,
             dxi_ref, dg_ref, da_ref, db_ref, s_ref, df_ref):
        t, f = pl.program_id(0), pl.program_id(1)

        @pl.when((t == 0) & (f == 0))
        def _():
            dg_ref[...] = jnp.zeros_like(dg_ref)

        @pl.when(f == 0)
        def _():
            df_ref[...] = (0.5 * dxo_ref[...]).astype(LOW)
            dxi_ref[...] = jnp.zeros_like(dxi_ref)

        ds = _dot_nt(df_ref[...], wd_ref[...])
        av = a_ref[...].astype(F32)
        bv = b_ref[...].astype(F32)
        sig = jax.nn.sigmoid(av)
        silu = av * sig
        s_ref[...] = (silu * bv).astype(LOW)
        da = (ds * bv * (sig * (1.0 + av * (1.0 - sig)))).astype(LOW)
        db = (ds * silu).astype(LOW)
        da_ref[...] = da
        db_ref[...] = db
        dxi_ref[...] += _dot(da, wg_ref[...]) + _dot(db, wu_ref[...])

        @pl.when(f == n_f - 1)
        def _():
            dx, dgp = _rms_bwd(dxi_ref[...], x_ref[...], g_ref[...])
            dxi_ref[...] = dxo_ref[...] + dx
            dg_ref[...] += dgp

    tok = pl.BlockSpec((ts, d), lambda t, f: (t, 0))
    vec = pl.BlockSpec((1, d), lambda t, f: (0, 0))
    wspec = pl.BlockSpec((fc, d), lambda t, f: (f, 0))
    hid = pl.BlockSpec((ts, fc), lambda t, f: (t, f))
    return _call(
        body,
        name=name,
        grid=(s_len // ts, n_f),
        in_specs=[tok, tok, vec, hid, hid, wspec, wspec, wspec],
        out_specs=[tok, vec, hid, hid, hid, tok],
        out_shape=[
            jax.ShapeDtypeStruct((s_len, d), F32),
            jax.ShapeDtypeStruct((1, d), F32),
            jax.ShapeDtypeStruct((s_len, f_len), LOW),
            jax.ShapeDtypeStruct((s_len, f_len), LOW),
            jax.ShapeDtypeStruct((s_len, f_len), LOW),
            jax.ShapeDtypeStruct((s_len, d), LOW),
        ],
        args=(dxo, x, g, a, b, wgt, wut, wd),
        exchange=exchange,
    )


def _ffn_wgrad(s, da, db, df, h, name, exchange=None):
    s_len, f_len = s.shape
    d = h.shape[1]
    tm = _pick(f_len, WGRAD_ROW_CANDIDATES)
    tk = min(WGRAD_TOK_TILE, s_len)
    n_k = s_len // tk

    def body(s_ref, da_ref, db_ref, df_ref, h_ref, dwd_ref, dwg_ref, dwu_ref, acc_d, acc_g, acc_u):
        k = pl.program_id(1)

        @pl.when(k == 0)
        def _():
            acc_d[...] = jnp.zeros_like(acc_d)
            acc_g[...] = jnp.zeros_like(acc_g)
            acc_u[...] = jnp.zeros_like(acc_u)

        hv = h_ref[...]
        acc_d[...] += _dot_tn(s_ref[...], df_ref[...])
        acc_g[...] += _dot_tn(da_ref[...], hv)
        acc_u[...] += _dot_tn(db_ref[...], hv)

        @pl.when(k == n_k - 1)
        def _():
            dwd_ref[...] = acc_d[...].astype(LOW)
            dwg_ref[...] = acc_g[...].astype(LOW)
            dwu_ref[...] = acc_u[...].astype(LOW)

    hid = pl.BlockSpec((tk, tm), lambda i, k: (k, i))
    tok = pl.BlockSpec((tk, d), lambda i, k: (k, 0))
    out = pl.BlockSpec((tm, d), lambda i, k: (i, 0))
    return _call(
        body,
        name=name,
        grid=(f_len // tm, n_k),
        in_specs=[hid, hid, hid, tok, tok],
        out_specs=[out, out, out],
        out_shape=[jax.ShapeDtypeStruct((f_len, d), LOW)] * 3,
        scratch_shapes=[pltpu.VMEM((tm, d), F32)] * 3,
        args=(s, da, db, df, h),
        exchange=exchange,
    )


def _wgrad_tn(xm, ym, tn, stacked, name):
    s_len, m = xm.shape
    n = ym.shape[1]
    tk = min(WGRAD_TOK_TILE, s_len)
    n_k = s_len // tk

    def body(x_ref, y_ref, o_ref, acc):
        k = pl.program_id(1)

        @pl.when(k == 0)
        def _():
            acc[...] = jnp.zeros_like(acc)

        acc[...] += _dot_tn(x_ref[...].astype(LOW), y_ref[...].astype(LOW))

        @pl.when(k == n_k - 1)
        def _():
            o_ref[...] = acc[...].astype(LOW)

    if stacked:
        out_spec = pl.BlockSpec((None, m, tn), lambda j, k: (j, 0, 0))
        out_shape = jax.ShapeDtypeStruct((n // tn, m, tn), LOW)
    else:
        out_spec = pl.BlockSpec((m, tn), lambda j, k: (0, j))
        out_shape = jax.ShapeDtypeStruct((m, n), LOW)
    return pl.pallas_call(
        body,
        name=name,
        grid=(n // tn, n_k),
        in_specs=[pl.BlockSpec((tk, m), lambda j, k: (k, 0)), pl.BlockSpec((tk, tn), lambda j, k: (k, j))],
        out_specs=out_spec,
        out_shape=out_shape,
        scratch_shapes=[pltpu.VMEM((m, tn), F32)],
        compiler_params=_params("arbitrary", "arbitrary"),
    )(xm, ym)


def _mix_parts(ext_ref, cw, ts, row0):
    dc = D_CONV

    def back(off, c0, c1):
        return ext_ref[HALO - off:HALO - off + ts, c0:c1]

    v, gate_b, gate_c = back(0, 0, dc), back(0, dc, 2 * dc), back(0, 2 * dc, 3 * dc)
    z0 = gate_c * v
    z1 = back(1, 2 * dc, 3 * dc) * back(1, 0, dc)
    z2 = back(2, 2 * dc, 3 * dc) * back(2, 0, dc)
    conv = cw[2:3, :] * z0 + cw[1:2, :] * z1 + cw[0:1, :] * z2
    rows = row0 + lax.broadcasted_iota(jnp.int32, (ts, 1), 0)
    pooled, inv_count = [], []
    for grp, w in enumerate(POOL_WINDOWS):
        c0 = 3 * dc + POOL_GC * grp
        u = back(0, c0, c0 + POOL_GC)
        acc = u
        for j in range(1, w):
            acc = acc + back(j, c0, c0 + POOL_GC)
        inv = 1.0 / jnp.minimum(rows + 1, w).astype(F32)
        pooled.append(acc * inv - u)
        inv_count.append(inv)
    return v, gate_b, gate_c, z0, z1, z2, conv, pooled, inv_count


def _mix_fwd(x, g, w_in, conv_w, pool_w, pool_scale, w_out, name, exchange=None):
    s_len, d = x.shape
    n_blk, _, wcols = w_in.shape
    p_len = n_blk * wcols
    d_mix = w_out.shape[0]
    ts = min(MIX_TOK_TILE, s_len)
    dc = D_CONV

    def body(x_ref, g_ref, win_ref, cw_ref, pw_ref, ps_ref, wout_ref, x2_ref, h_ref, proj_ref, ext_ref, cat_ref):
        t = pl.program_id(0)

        @pl.when(t == 0)
        def _():
            ext_ref[0:HALO, :] = jnp.zeros((HALO, p_len), F32)

        xv = x_ref[...]
        hb = (xv * _rms_scale(xv) * g_ref[...]).astype(LOW)
        h_ref[...] = hb
        for k in range(n_blk):
            ext_ref[HALO:HALO + ts, wcols * k:wcols * (k + 1)] = _dot(hb, win_ref[k])
        proj_ref[...] = ext_ref[HALO:HALO + ts, :]

        _, gate_b, _, _, _, _, conv, pooled, _ = _mix_parts(ext_ref, cw_ref[...], ts, t * ts)
        cat_ref[:, 0:dc] = (gate_b * conv).astype(LOW)
        for grp in range(len(POOL_WINDOWS)):
            c0 = POOL_GC * grp
            lin = _dot(pooled[grp].astype(LOW), pw_ref[grp])
            cat_ref[:, dc + c0:dc + c0 + POOL_GC] = (lin * ps_ref[:, c0:c0 + POOL_GC]).astype(LOW)
        x2_ref[...] = xv + _dot(cat_ref[...], wout_ref[...])
        ext_ref[0:HALO, :] = ext_ref[ts:ts + HALO, :]

    tok = pl.BlockSpec((ts, d), lambda t: (t, 0))

    def whole(arr):
        return pl.BlockSpec(arr.shape, lambda t: (0,) * arr.ndim)

    return _call(
        body,
        name=name,
        grid=(s_len // ts,),
        in_specs=[tok, whole(g), whole(w_in), whole(conv_w), whole(pool_w), whole(pool_scale), whole(w_out)],
        out_specs=[tok, tok, pl.BlockSpec((ts, p_len), lambda t: (t, 0))],
        out_shape=[
            jax.ShapeDtypeStruct((s_len, d), F32),
            jax.ShapeDtypeStruct((s_len, d), LOW),
            jax.ShapeDtypeStruct((s_len, p_len), F32),
        ],
        scratch_shapes=[pltpu.VMEM((ts + HALO, p_len), F32), pltpu.VMEM((ts, d_mix), LOW)],
        args=(x, g, w_in, conv_w, pool_w, pool_scale, w_out),
        exchange=exchange,
    )


def _mix_bwd(dx2, x, proj, g, w_in, conv_w, pool_w, pool_scale, w_out, name):
    s_len, d = x.shape
    n_blk, _, wcols = w_in.shape
    p_len = n_blk * wcols
    d_mix = w_out.shape[0]
    ts = min(MIX_TOK_TILE, s_len)
    n_t = s_len // ts
    dc = D_CONV
    n_grp = len(POOL_WINDOWS)

    def body(dx2_ref, x_ref, proj_ref, halo_ref, g_ref, win_ref, cw_ref, pw_ref, ps_ref, wout_ref,
             dx_ref, dproj_ref, cat_ref, dg_ref, dcw_ref, dpw_ref, dps_ref, ext_ref, fut_ref):
        i = pl.program_id(0)
        t = n_t - 1 - i

        @pl.when(i == 0)
        def _():
            dg_ref[...] = jnp.zeros_like(dg_ref)
            dcw_ref[...] = jnp.zeros_like(dcw_ref)
            dpw_ref[...] = jnp.zeros_like(dpw_ref)
            dps_ref[...] = jnp.zeros_like(dps_ref)
            fut_ref[ts:ts + HALO, :] = jnp.zeros((HALO, d_mix), F32)

        ext_ref[HALO:HALO + ts, :] = proj_ref[...]

        @pl.when(t == 0)
        def _():
            ext_ref[0:HALO, :] = jnp.zeros((HALO, p_len), F32)

        @pl.when(t > 0)
        def _():
            ext_ref[0:HALO, :] = halo_ref[...]

        cw = cw_ref[...]
        v, gate_b, gate_c, z0, z1, z2, conv, pooled, inv_count = _mix_parts(ext_ref, cw, ts, t * ts)
        dx2 = dx2_ref[...]
        dcat = _dot_nt(dx2.astype(LOW), wout_ref[...])

        dy_a = dcat[:, 0:dc]
        dconv = dy_a * gate_b
        fut_ref[0:ts, 0:dc] = dconv
        cat_ref[:, 0:dc] = (gate_b * conv).astype(LOW)
        dproj_ref[:, dc:2 * dc] = (dy_a * conv).astype(LOW)
        dcw_ref[2:3, :] += jnp.sum(dconv * z0, axis=0, keepdims=True)
        dcw_ref[1:2, :] += jnp.sum(dconv * z1, axis=0, keepdims=True)
        dcw_ref[0:1, :] += jnp.sum(dconv * z2, axis=0, keepdims=True)

        dpool = []
        for grp in range(n_grp):
            c0 = POOL_GC * grp
            pooled_b = pooled[grp].astype(LOW)
            lin = _dot(pooled_b, pw_ref[grp])
            dy_b = dcat[:, dc + c0:dc + c0 + POOL_GC]
            scale = ps_ref[:, c0:c0 + POOL_GC]
            cat_ref[:, dc + c0:dc + c0 + POOL_GC] = (lin * scale).astype(LOW)
            dps_ref[:, c0:c0 + POOL_GC] += jnp.sum(dy_b * lin, axis=0, keepdims=True)
            dlin = (dy_b * scale).astype(LOW)
            dpw_ref[grp] += _dot_tn(pooled_b, dlin)
            dpool.append(_dot_nt(dlin, pw_ref[grp]))
            fut_ref[0:ts, dc + c0:dc + c0 + POOL_GC] = dpool[grp] * inv_count[grp]

        def ahead(off, c0, c1):
            return fut_ref[off:off + ts, c0:c1]

        dz = cw[2:3, :] * ahead(0, 0, dc) + cw[1:2, :] * ahead(1, 0, dc) + cw[0:1, :] * ahead(2, 0, dc)
        dproj_ref[:, 0:dc] = (dz * gate_c).astype(LOW)
        dproj_ref[:, 2 * dc:3 * dc] = (dz * v).astype(LOW)
        for grp, w in enumerate(POOL_WINDOWS):
            c0 = dc + POOL_GC * grp
            acc = ahead(0, c0, c0 + POOL_GC)
            for j in range(1, w):
                acc = acc + ahead(j, c0, c0 + POOL_GC)
            dproj_ref[:, 2 * dc + c0:2 * dc + c0 + POOL_GC] = (acc - dpool[grp]).astype(LOW)

        dh = _dot_nt(dproj_ref[:, 0:wcols], win_ref[0])
        for k in range(1, n_blk):
            dh += _dot_nt(dproj_ref[:, wcols * k:wcols * (k + 1)], win_ref[k])
        dx, dgp = _rms_bwd(dh, x_ref[...], g_ref[...])
        dx_ref[...] = dx2 + dx
        dg_ref[...] += dgp
        fut_ref[ts:ts + HALO, :] = fut_ref[0:HALO, :]

    tok = pl.BlockSpec((ts, d), lambda i: (n_t - 1 - i, 0))
    halo = pl.BlockSpec((HALO, p_len), lambda i: (jnp.maximum((n_t - 1 - i) * (ts // HALO) - 1, 0), 0))

    def whole(arr):
        return pl.BlockSpec(arr.shape, lambda i: (0,) * arr.ndim)

    return pl.pallas_call(
        body,
        name=name,
        grid=(n_t,),
        in_specs=[tok, tok, pl.BlockSpec((ts, p_len), lambda i: (n_t - 1 - i, 0)), halo,
                  whole(g), whole(w_in), whole(conv_w), whole(pool_w), whole(pool_scale), whole(w_out)],
        out_specs=[tok, pl.BlockSpec((ts, p_len), lambda i: (n_t - 1 - i, 0)),
                   pl.BlockSpec((ts, d_mix), lambda i: (n_t - 1 - i, 0)),
                   whole(g), whole(conv_w), whole(pool_w), whole(pool_scale)],
        out_shape=[
            jax.ShapeDtypeStruct((s_len, d), F32),
            jax.ShapeDtypeStruct((s_len, p_len), LOW),
            jax.ShapeDtypeStruct((s_len, d_mix), LOW),
            jax.ShapeDtypeStruct(g.shape, F32),
            jax.ShapeDtypeStruct(conv_w.shape, F32),
            jax.ShapeDtypeStruct(pool_w.shape, F32),
            jax.ShapeDtypeStruct(pool_scale.shape, F32),
        ],
        scratch_shapes=[pltpu.VMEM((ts + HALO, p_len), F32), pltpu.VMEM((ts + HALO, d_mix), F32)],
        compiler_params=_params("arbitrary"),
    )(dx2, x, proj, proj, g, w_in, conv_w, pool_w, pool_scale, w_out)


def _loss_head(x, g, target, name):
    s_len, d = x.shape
    ts = min(TOK_TILE, s_len)

    def body(x_ref, g_ref, tgt_ref, loss_ref, dx_ref, dg_ref):
        @pl.when(pl.program_id(0) == 0)
        def _():
            loss_ref[...] = jnp.zeros_like(loss_ref)
            dg_ref[...] = jnp.zeros_like(dg_ref)

        xv, gv = x_ref[...], g_ref[...]
        err = xv * _rms_scale(xv) * gv - tgt_ref[...]
        loss_ref[...] += 0.5 * jnp.sum(jnp.mean(err * err, axis=-1, keepdims=True), axis=0, keepdims=True)
        dx, dgp = _rms_bwd(err * (1.0 / d), xv, gv)
        dx_ref[...] = dx
        dg_ref[...] += dgp

    tok = pl.BlockSpec((ts, d), lambda t: (t, 0))
    vec = pl.BlockSpec((1, d), lambda t: (0, 0))
    return pl.pallas_call(
        body,
        name=name,
        grid=(s_len // ts,),
        in_specs=[tok, vec, tok],
        out_specs=[pl.BlockSpec((1, 128), lambda t: (0, 0)), tok, vec],
        out_shape=[
            jax.ShapeDtypeStruct((1, 128), F32),
            jax.ShapeDtypeStruct((s_len, d), F32),
            jax.ShapeDtypeStruct((1, d), F32),
        ],
        compiler_params=_params("arbitrary"),
    )(x, g, target)


def _row_tile(rows, cols, stack_bytes):
    budget = 20 * 1024 * 1024
    per_row = cols * (4 * 7 + stack_bytes)
    for tr in (rows, 512, 256, 176, 128, 64, 32, 16, 8):
        if rows % tr == 0 and tr % 8 == 0 and tr * per_row * 2 <= budget:
            return tr
    return rows


def _sum_stack(stack, name):
    n, r, c = stack.shape
    tr = _row_tile(r, c, n * stack.dtype.itemsize)

    def body(s_ref, o_ref):
        acc = s_ref[0].astype(F32)
        for k in range(1, n):
            acc = acc + s_ref[k].astype(F32)
        o_ref[...] = acc

    return pl.pallas_call(
        body,
        name=name,
        grid=(r // tr,),
        in_specs=[pl.BlockSpec((n, tr, c), lambda i: (0, i, 0))],
        out_specs=pl.BlockSpec((tr, c), lambda i: (i, 0)),
        out_shape=jax.ShapeDtypeStruct((r, c), F32),
        compiler_params=_params("arbitrary"),
    )(stack)


def _adamw(stack, w, m, v, name):
    n, r, c = stack.shape
    tr = _row_tile(r, c, n * stack.dtype.itemsize)
    c1 = 1.0 - ADAM_B1 ** ADAM_STEP
    c2 = 1.0 - ADAM_B2 ** ADAM_STEP

    def body(s_ref, w_ref, m_ref, v_ref, g_ref, d_ref, mo_ref, vo_ref):
        gv = s_ref[0].astype(F32)
        for k in range(1, n):
            gv = gv + s_ref[k].astype(F32)
        mn = ADAM_B1 * m_ref[...] + (1.0 - ADAM_B1) * gv
        vn = ADAM_B2 * v_ref[...] + (1.0 - ADAM_B2) * (gv * gv)
        g_ref[...] = gv
        mo_ref[...] = mn
        vo_ref[...] = vn
        d_ref[...] = -ADAM_LR * ((mn / c1) / (jnp.sqrt(vn / c2) + ADAM_EPS) + ADAM_WD * w_ref[...])

    blk = pl.BlockSpec((tr, c), lambda i: (i, 0))
    return pl.pallas_call(
        body,
        name=name,
        grid=(r // tr,),
        in_specs=[pl.BlockSpec((n, tr, c), lambda i: (0, i, 0)), blk, blk, blk],
        out_specs=[blk] * 4,
        out_shape=[jax.ShapeDtypeStruct((r, c), F32)] * 4,
        compiler_params=_params("arbitrary"),
    )(stack, w, m, v)


def _to_sheet(parts):
    sheets, spans = [], []
    row = 0
    for p in parts:
        flat = p.reshape(-1).astype(F32)
        rows = -(-flat.shape[0] // 1024) * 8
        flat = jnp.pad(flat, (0, rows * 128 - flat.shape[0]))
        sheets.append(flat.reshape(rows, 128))
        spans.append((row, p.size, p.shape))
        row += rows
    return jnp.concatenate(sheets, axis=0), spans


def _from_sheet(sheet, spans):
    out = []
    for row, size, shape in spans:
        rows = -(-size // 1024) * 8
        out.append(sheet[row:row + rows].reshape(-1)[:size].reshape(shape))
    return out


def kernel(x, norm_ffn1, ffn1_w_gate, ffn1_w_up, ffn1_w_down, norm_mix, w_in, conv_w, pool_w, pool_scale, w_out, norm_ffn2, ffn2_w_gate, ffn2_w_up, ffn2_w_down, norm_final, loss_target, m_norm_ffn1, m_ffn1_w_gate, m_ffn1_w_up, m_ffn1_w_down, m_norm_mix, m_w_in, m_conv_w, m_pool_w, m_pool_scale, m_w_out, m_norm_ffn2, m_ffn2_w_gate, m_ffn2_w_up, m_ffn2_w_down, m_norm_final, v_norm_ffn1, v_ffn1_w_gate, v_ffn1_w_up, v_ffn1_w_down, v_norm_mix, v_w_in, v_conv_w, v_pool_w, v_pool_scale, v_w_out, v_norm_ffn2, v_ffn2_w_gate, v_ffn2_w_up, v_ffn2_w_down, v_norm_final):
    me = 4 * lax.axis_index("x") + 2 * lax.axis_index("y") + lax.axis_index("c")
    xs, tgt = x[0], loss_target[0]
    s_len, d = xs.shape
    f_shard = ffn1_w_down.shape[1]
    f_len = N_DEV * f_shard
    conv_shard = conv_w.shape[2]

    def low_t(wt):
        return wt[0].T.astype(LOW)

    def by_dev(gw):
        return gw.reshape(N_DEV, -1, d)

    conv_tile = jnp.zeros((8, 128), F32).at[0:conv_w.shape[1], 0:conv_shard].set(conv_w[0])
    full = _exchange([low_t(ffn1_w_gate), low_t(ffn1_w_up), ffn1_w_down[0].astype(LOW)], [False] * 3, "gather_ffn1")
    wg1, wu1, wd1 = (w.reshape(f_len, d) for w in full)
    pool_w_low = pool_w[0].astype(LOW)

    later = _Exchange([w_in[0].astype(LOW), w_out[0].astype(LOW), conv_tile, ffn2_w_down[0].astype(LOW)], [False] * 4)
    x1, h1, a1, b1, w_in_full, w_out_full, conv_tiles, wd2 = _ffn_fwd(
        xs, norm_ffn1, wg1, wu1, wd1, "ffn1_fwd", exchange=later)
    w_out_full, wd2 = w_out_full.reshape(-1, d), wd2.reshape(f_len, d)
    conv_full = jnp.concatenate([conv_tiles[k, 0:conv_w.shape[1], 0:conv_shard] for k in range(N_DEV)], axis=1)
    later = _Exchange([low_t(ffn2_w_gate), low_t(ffn2_w_up)], [False] * 2)
    x2, h2, proj, wg2, wu2 = _mix_fwd(
        x1, norm_mix, w_in_full, conv_full, pool_w_low, pool_scale, w_out_full, "mix_fwd", exchange=later)
    wg2, wu2 = wg2.reshape(f_len, d), wu2.reshape(f_len, d)
    x3, h3, a2, b2 = _ffn_fwd(x2, norm_ffn2, wg2, wu2, wd2, "ffn2_fwd")
    loss_row, dx3, dg_final = _loss_head(x3, norm_final.reshape(1, d), tgt, "loss_head")

    dx2, dg_ffn2, da, db, sv, df = _ffn_dgrad(dx3, x2, norm_ffn2, a2, b2, wg2, wu2, wd2, "ffn2_dgrad")
    dwd2, dwg2, dwu2 = _ffn_wgrad(sv, da, db, df, h3, "ffn2_wgrad")
    dx1, dproj, cat, dg_mix, dconv, dpool_w, dpool_scale = _mix_bwd(
        dx2, x1, proj, norm_mix, w_in_full, conv_full, pool_w_low, pool_scale, w_out_full, "mix_bwd")
    dw_in = _wgrad_tn(h2, dproj, W_IN_SHARD, True, "w_in_wgrad")
    dw_out = _wgrad_tn(cat, dx2, d, False, "w_out_wgrad")
    earlier = _Exchange([by_dev(dwg2), by_dev(dwu2), by_dev(dwd2)], [True] * 3)
    dx0, dg_ffn1, da, db, sv, df, got_g2, got_u2, got_d2 = _ffn_dgrad(
        dx1, xs, norm_ffn1, a1, b1, wg1, wu1, wd1, "ffn1_dgrad", exchange=earlier)
    small_parts = [dg_ffn1, dg_mix, dg_ffn2, dg_final, dconv, dpool_w, dpool_scale]
    small_sheet, spans = _to_sheet(small_parts)
    earlier = _Exchange([dw_in, by_dev(dw_out), small_sheet], [True, True, False])
    dwd1, dwg1, dwu1, got_in, got_out, got_small = _ffn_wgrad(sv, da, db, df, h1, "ffn1_wgrad", exchange=earlier)
    got_g1, got_u1, got_d1 = _exchange([by_dev(dwg1), by_dev(dwu1), by_dev(dwd1)], [True] * 3, "scatter_ffn1")
    got = [got_g1, got_u1, got_d1, got_in, got_out, got_g2, got_u2, got_d2, got_small]

    outs = {}

    def update(name, stack, w, m, v):
        outs[name] = _adamw(stack, w[0], m[0], v[0], "adamw_" + name)

    def update_t(name, stack, w, m, v):
        g = _sum_stack(stack, "sum_" + name).T
        outs[name] = _adamw(g[None], w[0], m[0], v[0], "adamw_" + name)

    update_t("ffn1_w_gate", got[0], ffn1_w_gate, m_ffn1_w_gate, v_ffn1_w_gate)
    update_t("ffn1_w_up", got[1], ffn1_w_up, m_ffn1_w_up, v_ffn1_w_up)
    update("ffn1_w_down", got[2], ffn1_w_down, m_ffn1_w_down, v_ffn1_w_down)
    update("w_in", got[3], w_in, m_w_in, v_w_in)
    update("w_out", got[4], w_out, m_w_out, v_w_out)
    update_t("ffn2_w_gate", got[5], ffn2_w_gate, m_ffn2_w_gate, v_ffn2_w_gate)
    update_t("ffn2_w_up", got[6], ffn2_w_up, m_ffn2_w_up, v_ffn2_w_up)
    update("ffn2_w_down", got[7], ffn2_w_down, m_ffn2_w_down, v_ffn2_w_down)

    g_small = _from_sheet(_sum_stack(got[8], "sum_small"), spans)
    g_conv = lax.dynamic_slice_in_dim(g_small[4], me * conv_shard, conv_shard, axis=1)
    small_names = ["norm_ffn1", "norm_mix", "norm_ffn2", "norm_final", "conv_w", "pool_w", "pool_scale"]
    small_g = [g_small[0], g_small[1], g_small[2], g_small[3].reshape(norm_final.shape), g_conv[None],
               g_small[5][None], g_small[6]]
    small_w = [norm_ffn1, norm_mix, norm_ffn2, norm_final, conv_w, pool_w, pool_scale]
    small_m = [m_norm_ffn1, m_norm_mix, m_norm_ffn2, m_norm_final, m_conv_w, m_pool_w, m_pool_scale]
    small_v = [v_norm_ffn1, v_norm_mix, v_norm_ffn2, v_norm_final, v_conv_w, v_pool_w, v_pool_scale]
    g_sheet, spans_u = _to_sheet(small_g)
    w_sheet, _ = _to_sheet(small_w)
    m_sheet, _ = _to_sheet(small_m)
    v_sheet, _ = _to_sheet(small_v)
    upd = _adamw(g_sheet[None], w_sheet, m_sheet, v_sheet, "adamw_small")
    small_out = [_from_sheet(u, spans_u) for u in upd]
    for k, nm in enumerate(small_names):
        outs[nm] = tuple(small_out[j][k] for j in range(4))

    loss = lax.psum(loss_row[0, 0], ("x", "y", "c"))
    order = ["norm_ffn1", "ffn1_w_gate", "ffn1_w_up", "ffn1_w_down", "norm_mix", "w_in", "conv_w", "pool_w",
             "pool_scale", "w_out", "norm_ffn2", "ffn2_w_gate", "ffn2_w_up", "ffn2_w_down", "norm_final"]
    big = {"ffn1_w_gate", "ffn1_w_up", "ffn1_w_down", "w_in", "w_out", "ffn2_w_gate", "ffn2_w_up", "ffn2_w_down"}

    def leaf(nm, j):
        val = outs[nm][j]
        return val[None] if nm in big else val

    return (loss, dx0[None],
            *[leaf(nm, 0) for nm in order], *[leaf(nm, 1) for nm in order],
            *[leaf(nm, 2) for nm in order], *[leaf(nm, 3) for nm in order])
```

```python
import jax
import jax.numpy as jnp
from jax import lax
from jax.experimental import pallas as pl
from jax.experimental.pallas import tpu as pltpu

F32 = jnp.float32
LOW = jnp.bfloat16

N_DEV = 8
EPS = 1e-6
D_CONV = 512
POOL_WINDOWS = (2, 4, 8, 16)
POOL_GC = 128
HALO = 16
W_IN_SHARD = 256

ADAM_LR = 0.001
ADAM_B1 = 0.9
ADAM_B2 = 0.999
ADAM_EPS = 1e-08
ADAM_WD = 0.01
ADAM_STEP = 10

VMEM_LIMIT_BYTES = 56 * 1024 * 1024
TOK_TILE = 1024
MIX_TOK_TILE = 256
WGRAD_TOK_TILE = 2048
FF_TILE_CANDIDATES = (256, 128)
WGRAD_ROW_CANDIDATES = (256, 128)


def _params(*sem):
    return pltpu.CompilerParams(dimension_semantics=sem, vmem_limit_bytes=VMEM_LIMIT_BYTES)


def _pick(n, candidates):
    for c in candidates:
        if n % c == 0:
            return c
    raise ValueError(f"no tile in {candidates} divides {n}")


def _dot(a, b):
    return lax.dot_general(a, b, (((1,), (0,)), ((), ())), preferred_element_type=F32)


def _dot_nt(a, b):
    return lax.dot_general(a, b, (((1,), (1,)), ((), ())), preferred_element_type=F32)


def _dot_tn(a, b):
    return lax.dot_general(a, b, (((0,), (0,)), ((), ())), preferred_element_type=F32)


def _rms_scale(x):
    return lax.rsqrt(jnp.mean(x * x, axis=-1, keepdims=True) + EPS)


def _rms_bwd(dy, x, g):
    r = _rms_scale(x)
    xhat = x * r
    gdy = dy * g
    dx = r * (gdy - xhat * jnp.mean(gdy * xhat, axis=-1, keepdims=True))
    return dx, jnp.sum(dy * xhat, axis=0, keepdims=True)


class _Exchange:
    CHIPS = (2, 4, 6)

    def __init__(self, arrays, sliced, relay_at=None):
        assert relay_at is None or not any(sliced)
        self.relay_at = relay_at
        self.arrays, self.sliced, self.n = list(arrays), list(sliced), len(arrays)
        self.out_shape = [jax.ShapeDtypeStruct(arr.shape if sl else (N_DEV,) + arr.shape, arr.dtype)
                          for arr, sl in zip(arrays, sliced)]
        self.specs = [pl.BlockSpec(memory_space=pl.ANY)] * self.n
        self.scratch_shapes = [pltpu.SemaphoreType.DMA((self.n, N_DEV - 1)),
                               pltpu.SemaphoreType.DMA((self.n, N_DEV - 1)),
                               pltpu.SemaphoreType.DMA((self.n,))]

    def _copies(self, ins, outs, sems):
        send_sems, recv_sems, local_sems = sems
        sliced = self.sliced
        mx, my, mc = lax.axis_index("x"), lax.axis_index("y"), lax.axis_index("c")
        me = 4 * mx + 2 * my + mc

        def peer(m):
            px = lax.rem(mx + ((m >> 2) & 1), 2)
            py = lax.rem(my + ((m >> 1) & 1), 2)
            pc = lax.rem(mc + (m & 1), 2)
            return (px, py, pc), 4 * px + 2 * py + pc

        def remote(a, m, arriving):
            pid, pflat = peer(m)
            return pltpu.make_async_remote_copy(
                src_ref=ins[a].at[pflat] if sliced[a] else ins[a],
                dst_ref=outs[a].at[pflat if arriving else me],
                send_sem=send_sems.at[a, m - 1],
                recv_sem=recv_sems.at[a, m - 1],
                device_id=pid,
                device_id_type=pl.DeviceIdType.MESH,
            )

        def local(a):
            return pltpu.make_async_copy(ins[a].at[me] if sliced[a] else ins[a], outs[a].at[me], local_sems.at[a])

        def passed_on(a, m):
            _, origin = peer(m)
            sibling, _ = peer(1)
            return pltpu.make_async_remote_copy(
                src_ref=outs[a].at[origin],
                dst_ref=outs[a].at[origin],
                send_sem=send_sems.at[a, m],
                recv_sem=recv_sems.at[a, m],
                device_id=sibling,
                device_id_type=pl.DeviceIdType.MESH,
            )

        return remote, local, passed_on

    def start(self, ins, outs, sems):
        remote, local, _ = self._copies(ins, outs, sems)
        for a in range(self.n):
            local(a).start()
        for m in ((1,) + self.CHIPS if self.relay_at is not None else range(1, N_DEV)):
            for a in range(self.n):
                remote(a, m, False).start()

    def relay(self, ins, outs, sems):
        remote, _, passed_on = self._copies(ins, outs, sems)
        for m in self.CHIPS:
            for a in range(self.n):
                remote(a, m, True).wait_recv()
                passed_on(a, m).start()

    def wait(self, ins, outs, sems):
        remote, local, passed_on = self._copies(ins, outs, sems)
        if self.relay_at is None:
            for m in range(1, N_DEV):
                for a in range(self.n):
                    remote(a, m, True).wait_recv()
            for m in range(1, N_DEV):
                for a in range(self.n):
                    remote(a, m, False).wait_send()
        else:
            for m in (1,) + tuple(c + 1 for c in self.CHIPS):
                for a in range(self.n):
                    remote(a, m, True).wait_recv()
            for m in (1,) + self.CHIPS:
                for a in range(self.n):
                    remote(a, m, False).wait_send()
            for m in self.CHIPS:
                for a in range(self.n):
                    passed_on(a, m).wait_send()
        for a in range(self.n):
            local(a).wait()


def _exchange(arrays, sliced, name, relay=False):
    ex = _Exchange(arrays, sliced, relay_at=0 if relay else None)

    def body(*refs):
        ins, outs, sems = refs[:ex.n], refs[ex.n:2 * ex.n], refs[2 * ex.n:]
        ex.start(ins, outs, sems)
        if relay:
            ex.relay(ins, outs, sems)
        ex.wait(ins, outs, sems)

    return pl.pallas_call(body, name=name, out_shape=ex.out_shape, in_specs=ex.specs, out_specs=ex.specs,
                          scratch_shapes=ex.scratch_shapes)(*arrays)


def _call(body, *, name, grid, in_specs, out_specs, out_shape, args, scratch_shapes=(), exchange=None):
    params = _params(*(("arbitrary",) * len(grid)))
    if exchange is None:
        return pl.pallas_call(body, name=name, grid=grid, in_specs=in_specs, out_specs=out_specs, out_shape=out_shape,
                              scratch_shapes=list(scratch_shapes), compiler_params=params)(*args)
    ex = exchange
    n_in, n_out, n_scr = len(in_specs), len(out_specs), len(scratch_shapes)
    n_steps = 1
    for g in grid:
        n_steps *= g

    def hosted(*refs):
        ins, refs = refs[:n_in], refs[n_in:]
        ex_ins, refs = refs[:ex.n], refs[ex.n:]
        outs, refs = refs[:n_out], refs[n_out:]
        ex_outs, refs = refs[:ex.n], refs[ex.n:]
        scr, sems = refs[:n_scr], refs[n_scr:]
        step = pl.program_id(0)
        for ax in range(1, len(grid)):
            step = step * grid[ax] + pl.program_id(ax)

        @pl.when(step == 0)
        def _():
            ex.start(ex_ins, ex_outs, sems)

        body(*ins, *outs, *scr)

        if ex.relay_at is not None:
            @pl.when(step == min(int(ex.relay_at * n_steps), n_steps - 1))
            def _():
                ex.relay(ex_ins, ex_outs, sems)

        @pl.when(step == n_steps - 1)
        def _():
            ex.wait(ex_ins, ex_outs, sems)

    return pl.pallas_call(
        hosted, name=name, grid=grid,
        in_specs=list(in_specs) + ex.specs, out_specs=list(out_specs) + ex.specs,
        out_shape=list(out_shape) + ex.out_shape,
        scratch_shapes=list(scratch_shapes) + ex.scratch_shapes,
        compiler_params=params)(*args, *ex.arrays)


def _ffn_fwd(x, g, wgt, wut, wd, name, exchange=None):
    s_len, d = x.shape
    f_len = wd.shape[0]
    ts = min(TOK_TILE, s_len)
    fc = _pick(f_len, FF_TILE_CANDIDATES)

    def body(x_ref, g_ref, wg_ref, wu_ref, wd_ref, xo_ref, h_ref, a_ref, b_ref):
        @pl.when(pl.program_id(1) == 0)
        def _():
            xv = x_ref[...]
            h_ref[...] = (xv * _rms_scale(xv) * g_ref[...]).astype(LOW)
            xo_ref[...] = xv

        hb = h_ref[...]
        a = _dot_nt(hb, wg_ref[...])
        b = _dot_nt(hb, wu_ref[...])
        s = a * jax.nn.sigmoid(a) * b
        xo_ref[...] += 0.5 * _dot(s.astype(LOW), wd_ref[...])
        a_ref[...] = a.astype(LOW)
        b_ref[...] = b.astype(LOW)

    tok = pl.BlockSpec((ts, d), lambda t, f: (t, 0))
    wspec = pl.BlockSpec((fc, d), lambda t, f: (f, 0))
    hid = pl.BlockSpec((ts, fc), lambda t, f: (t, f))
    return _call(
        body,
        name=name,
        grid=(s_len // ts, f_len // fc),
        in_specs=[tok, pl.BlockSpec((1, d), lambda t, f: (0, 0)), wspec, wspec, wspec],
        out_specs=[tok, tok, hid, hid],
        out_shape=[
            jax.ShapeDtypeStruct((s_len, d), F32),
            jax.ShapeDtypeStruct((s_len, d), LOW),
            jax.ShapeDtypeStruct((s_len, f_len), LOW),
            jax.ShapeDtypeStruct((s_len, f_len), LOW),
        ],
        args=(x, g, wgt, wut, wd),
        exchange=exchange,
    )


def _ffn_dgrad(dxo, x, g, a, b, wgt, wut, wd, name, exchange=None):
    s_len, d = x.shape
    f_len = wd.shape[0]
    ts = min(TOK_TILE, s_len)
    fc = _pick(f_len, FF_TILE_CANDIDATES)
    n_f = f_len // fc

    def body(dxo_ref, x_ref, g_ref, a_ref, b_ref, wg_ref, wu_ref, wd_ref,
             dxi_ref, dg_ref, da_ref, db_ref, s_ref, df_ref):
        t, f = pl.program_id(0), pl.program_id(1)

        @pl.when((t == 0) & (f == 0))
        def _():
            dg_ref[...] = jnp.zeros_like(dg_ref)

        @pl.when(f == 0)
        def _():
            df_ref[...] = (0.5 * dxo_ref[...]).astype(LOW)
            dxi_ref[...] = jnp.zeros_like(dxi_ref)

        ds = _dot_nt(df_ref[...], wd_ref[...])
        av = a_ref[...].astype(F32)
        bv = b_ref[...].astype(F32)
        sig = jax.nn.sigmoid(av)
        silu = av * sig
        s_ref[...] = (silu * bv).astype(LOW)
        da = (ds * bv * (sig * (1.0 + av * (1.0 - sig)))).astype(LOW)
        db = (ds * silu).astype(LOW)
        da_ref[...] = da
        db_ref[...] = db
        dxi_ref[...] += _dot(da, wg_ref[...]) + _dot(db, wu_ref[...])

        @pl.when(f == n_f - 1)
        def _():
            dx, dgp = _rms_bwd(dxi_ref[...], x_ref[...], g_ref[...])
            dxi_ref[...] = dxo_ref[...] + dx
            dg_ref[...] += dgp

    tok = pl.BlockSpec((ts, d), lambda t, f: (t, 0))
    vec = pl.BlockSpec((1, d), lambda t, f: (0, 0))
    wspec = pl.BlockSpec((fc, d), lambda t, f: (f, 0))
    hid = pl.BlockSpec((ts, fc), lambda t, f: (t, f))
    return _call(
        body,
        name=name,
        grid=(s_len // ts, n_f),
        in_specs=[tok, tok, vec, hid, hid, wspec, wspec, wspec],
        out_specs=[tok, vec, hid, hid, hid, tok],
        out_shape=[
            jax.ShapeDtypeStruct((s_len, d), F32),
            jax.ShapeDtypeStruct((1, d), F32),
            jax.ShapeDtypeStruct((s_len, f_len), LOW),
            jax.ShapeDtypeStruct((s_len, f_len), LOW),
            jax.ShapeDtypeStruct((s_len, f_len), LOW),
            jax.ShapeDtypeStruct((s_len, d), LOW),
        ],
        args=(dxo, x, g, a, b, wgt, wut, wd),
        exchange=exchange,
    )


def _ffn_wgrad(s, da, db, df, h, name, exchange=None):
    s_len, f_len = s.shape
    d = h.shape[1]
    tm = _pick(f_len, WGRAD_ROW_CANDIDATES)
    tk = min(WGRAD_TOK_TILE, s_len)
    n_k = s_len // tk

    def body(s_ref, da_ref, db_ref, df_ref, h_ref, dwd_ref, dwg_ref, dwu_ref, acc_d, acc_g, acc_u):
        k = pl.program_id(1)

        @pl.when(k == 0)
        def _():
            acc_d[...] = jnp.zeros_like(acc_d)
            acc_g[...] = jnp.zeros_like(acc_g)
            acc_u[...] = jnp.zeros_like(acc_u)

        hv = h_ref[...]
        acc_d[...] += _dot_tn(s_ref[...], df_ref[...])
        acc_g[...] += _dot_tn(da_ref[...], hv)
        acc_u[...] += _dot_tn(db_ref[...], hv)

        @pl.when(k == n_k - 1)
        def _():
            dwd_ref[...] = acc_d[...].astype(LOW)
            dwg_ref[...] = acc_g[...].astype(LOW)
            dwu_ref[...] = acc_u[...].astype(LOW)

    hid = pl.BlockSpec((tk, tm), lambda i, k: (k, i))
    tok = pl.BlockSpec((tk, d), lambda i, k: (k, 0))
    out = pl.BlockSpec((tm, d), lambda i, k: (i, 0))
    return _call(
        body,
        name=name,
        grid=(f_len // tm, n_k),
        in_specs=[hid, hid, hid, tok, tok],
        out_specs=[out, out, out],
        out_shape=[jax.ShapeDtypeStruct((f_len, d), LOW)] * 3,
        scratch_shapes=[pltpu.VMEM((tm, d), F32)] * 3,
        args=(s, da, db, df, h),
        exchange=exchange,
    )


def _wgrad_tn(xm, ym, tn, stacked, name):
    s_len, m = xm.shape
    n = ym.shape[1]
    tk = min(WGRAD_TOK_TILE, s_len)
    n_k = s_len // tk

    def body(x_ref, y_ref, o_ref, acc):
        k = pl.program_id(1)

        @pl.when(k == 0)
        def _():
            acc[...] = jnp.zeros_like(acc)

        acc[...] += _dot_tn(x_ref[...].astype(LOW), y_ref[...].astype(LOW))

        @pl.when(k == n_k - 1)
        def _():
            o_ref[...] = acc[...].astype(LOW)

    if stacked:
        out_spec = pl.BlockSpec((None, m, tn), lambda j, k: (j, 0, 0))
        out_shape = jax.ShapeDtypeStruct((n // tn, m, tn), LOW)
    else:
        out_spec = pl.BlockSpec((m, tn), lambda j, k: (0, j))
        out_shape = jax.ShapeDtypeStruct((m, n), LOW)
    return pl.pallas_call(
        body,
        name=name,
        grid=(n // tn, n_k),
        in_specs=[pl.BlockSpec((tk, m), lambda j, k: (k, 0)), pl.BlockSpec((tk, tn), lambda j, k: (k, j))],
        out_specs=out_spec,
        out_shape=out_shape,
        scratch_shapes=[pltpu.VMEM((m, tn), F32)],
        compiler_params=_params("arbitrary", "arbitrary"),
    )(xm, ym)


def _mix_parts(ext_ref, cw, ts, row0):
    dc = D_CONV

    def back(off, c0, c1):
        return ext_ref[HALO - off:HALO - off + ts, c0:c1]

    v, gate_b, gate_c = back(0, 0, dc), back(0, dc, 2 * dc), back(0, 2 * dc, 3 * dc)
    z0 = gate_c * v
    z1 = back(1, 2 * dc, 3 * dc) * back(1, 0, dc)
    z2 = back(2, 2 * dc, 3 * dc) * back(2, 0, dc)
    conv = cw[2:3, :] * z0 + cw[1:2, :] * z1 + cw[0:1, :] * z2
    rows = row0 + lax.broadcasted_iota(jnp.int32, (ts, 1), 0)
    pooled, inv_count = [], []
    for grp, w in enumerate(POOL_WINDOWS):
        c0 = 3 * dc + POOL_GC * grp
        u = back(0, c0, c0 + POOL_GC)
        acc = u
        for j in range(1, w):
            acc = acc + back(j, c0, c0 + POOL_GC)
        inv = 1.0 / jnp.minimum(rows + 1, w).astype(F32)
        pooled.append(acc * inv - u)
        inv_count.append(inv)
    return v, gate_b, gate_c, z0, z1, z2, conv, pooled, inv_count


def _mix_fwd(x, g, w_in, conv_w, pool_w, pool_scale, w_out, name, exchange=None):
    s_len, d = x.shape
    n_blk, _, wcols = w_in.shape
    p_len = n_blk * wcols
    d_mix = w_out.shape[0]
    ts = min(MIX_TOK_TILE, s_len)
    dc = D_CONV

    def body(x_ref, g_ref, win_ref, cw_ref, pw_ref, ps_ref, wout_ref, x2_ref, h_ref, proj_ref, ext_ref, cat_ref):
        t = pl.program_id(0)

        @pl.when(t == 0)
        def _():
            ext_ref[0:HALO, :] = jnp.zeros((HALO, p_len), F32)

        xv = x_ref[...]
        hb = (xv * _rms_scale(xv) * g_ref[...]).astype(LOW)
        h_ref[...] = hb
        for k in range(n_blk):
            ext_ref[HALO:HALO + ts, wcols * k:wcols * (k + 1)] = _dot(hb, win_ref[k])
        proj_ref[...] = ext_ref[HALO:HALO + ts, :]

        _, gate_b, _, _, _, _, conv, pooled, _ = _mix_parts(ext_ref, cw_ref[...], ts, t * ts)
        cat_ref[:, 0:dc] = (gate_b * conv).astype(LOW)
        for grp in range(len(POOL_WINDOWS)):
            c0 = POOL_GC * grp
            lin = _dot(pooled[grp].astype(LOW), pw_ref[grp])
            cat_ref[:, dc + c0:dc + c0 + POOL_GC] = (lin * ps_ref[:, c0:c0 + POOL_GC]).astype(LOW)
        x2_ref[...] = xv + _dot(cat_ref[...], wout_ref[...])
        ext_ref[0:HALO, :] = ext_ref[ts:ts + HALO, :]

    tok = pl.BlockSpec((ts, d), lambda t: (t, 0))

    def whole(arr):
        return pl.BlockSpec(arr.shape, lambda t: (0,) * arr.ndim)

    return _call(
        body,
        name=name,
        grid=(s_len // ts,),
        in_specs=[tok, whole(g), whole(w_in), whole(conv_w), whole(pool_w), whole(pool_scale), whole(w_out)],
        out_specs=[tok, tok, pl.BlockSpec((ts, p_len), lambda t: (t, 0))],
        out_shape=[
            jax.ShapeDtypeStruct((s_len, d), F32),
            jax.ShapeDtypeStruct((s_len, d), LOW),
            jax.ShapeDtypeStruct((s_len, p_len), F32),
        ],
        scratch_shapes=[pltpu.VMEM((ts + HALO, p_len), F32), pltpu.VMEM((ts, d_mix), LOW)],
        args=(x, g, w_in, conv_w, pool_w, pool_scale, w_out),
        exchange=exchange,
    )


def _mix_bwd(dx2, x, proj, g, w_in, conv_w, pool_w, pool_scale, w_out, name):
    s_len, d = x.shape
    n_blk, _, wcols = w_in.shape
    p_len = n_blk * wcols
    d_mix = w_out.shape[0]
    ts = min(MIX_TOK_TILE, s_len)
    n_t = s_len // ts
    dc = D_CONV
    n_grp = len(POOL_WINDOWS)

    def body(dx2_ref, x_ref, proj_ref, halo_ref, g_ref, win_ref, cw_ref, pw_ref, ps_ref, wout_ref,
             dx_ref, dproj_ref, cat_ref, dg_ref, dcw_ref, dpw_ref, dps_ref, ext_ref, fut_ref):
        i = pl.program_id(0)
        t = n_t - 1 - i

        @pl.when(i == 0)
        def _():
            dg_ref[...] = jnp.zeros_like(dg_ref)
            dcw_ref[...] = jnp.zeros_like(dcw_ref)
            dpw_ref[...] = jnp.zeros_like(dpw_ref)
            dps_ref[...] = jnp.zeros_like(dps_ref)
            fut_ref[ts:ts + HALO, :] = jnp.zeros((HALO, d_mix), F32)

        ext_ref[HALO:HALO + ts, :] = proj_ref[...]

        @pl.when(t == 0)
        def _():
            ext_ref[0:HALO, :] = jnp.zeros((HALO, p_len), F32)

        @pl.when(t > 0)
        def _():
            ext_ref[0:HALO, :] = halo_ref[...]

        cw = cw_ref[...]
        v, gate_b, gate_c, z0, z1, z2, conv, pooled, inv_count = _mix_parts(ext_ref, cw, ts, t * ts)
        dx2 = dx2_ref[...]
        dcat = _dot_nt(dx2.astype(LOW), wout_ref[...])

        dy_a = dcat[:, 0:dc]
        dconv = dy_a * gate_b
        fut_ref[0:ts, 0:dc] = dconv
        cat_ref[:, 0:dc] = (gate_b * conv).astype(LOW)
        dproj_ref[:, dc:2 * dc] = (dy_a * conv).astype(LOW)
        dcw_ref[2:3, :] += jnp.sum(dconv * z0, axis=0, keepdims=True)
        dcw_ref[1:2, :] += jnp.sum(dconv * z1, axis=0, keepdims=True)
        dcw_ref[0:1, :] += jnp.sum(dconv * z2, axis=0, keepdims=True)

        dpool = []
        for grp in range(n_grp):
            c0 = POOL_GC * grp
            pooled_b = pooled[grp].astype(LOW)
            lin = _dot(pooled_b, pw_ref[grp])
            dy_b = dcat[:, dc + c0:dc + c0 + POOL_GC]
            scale = ps_ref[:, c0:c0 + POOL_GC]
            cat_ref[:, dc + c0:dc + c0 + POOL_GC] = (lin * scale).astype(LOW)
            dps_ref[:, c0:c0 + POOL_GC] += jnp.sum(dy_b * lin, axis=0, keepdims=True)
            dlin = (dy_b * scale).astype(LOW)
            dpw_ref[grp] += _dot_tn(pooled_b, dlin)
            dpool.append(_dot_nt(dlin, pw_ref[grp]))
            fut_ref[0:ts, dc + c0:dc + c0 + POOL_GC] = dpool[grp] * inv_count[grp]

        def ahead(off, c0, c1):
            return fut_ref[off:off + ts, c0:c1]

        dz = cw[2:3, :] * ahead(0, 0, dc) + cw[1:2, :] * ahead(1, 0, dc) + cw[0:1, :] * ahead(2, 0, dc)
        dproj_ref[:, 0:dc] = (dz * gate_c).astype(LOW)
        dproj_ref[:, 2 * dc:3 * dc] = (dz * v).astype(LOW)
        for grp, w in enumerate(POOL_WINDOWS):
            c0 = dc + POOL_GC * grp
            acc = ahead(0, c0, c0 + POOL_GC)
            for j in range(1, w):
                acc = acc + ahead(j, c0, c0 + POOL_GC)
            dproj_ref[:, 2 * dc + c0:2 * dc + c0 + POOL_GC] = (acc - dpool[grp]).astype(LOW)

        dh = _dot_nt(dproj_ref[:, 0:wcols], win_ref[0])
        for k in range(1, n_blk):
            dh += _dot_nt(dproj_ref[:, wcols * k:wcols * (k + 1)], win_ref[k])
        dx, dgp = _rms_bwd(dh, x_ref[...], g_ref[...])
        dx_ref[...] = dx2 + dx
        dg_ref[...] += dgp
        fut_ref[ts:ts + HALO, :] = fut_ref[0:HALO, :]

    tok = pl.BlockSpec((ts, d), lambda i: (n_t - 1 - i, 0))
    halo = pl.BlockSpec((HALO, p_len), lambda i: (jnp.maximum((n_t - 1 - i) * (ts // HALO) - 1, 0), 0))

    def whole(arr):
        return pl.BlockSpec(arr.shape, lambda i: (0,) * arr.ndim)

    return pl.pallas_call(
        body,
        name=name,
        grid=(n_t,),
        in_specs=[tok, tok, pl.BlockSpec((ts, p_len), lambda i: (n_t - 1 - i, 0)), halo,
                  whole(g), whole(w_in), whole(conv_w), whole(pool_w), whole(pool_scale), whole(w_out)],
        out_specs=[tok, pl.BlockSpec((ts, p_len), lambda i: (n_t - 1 - i, 0)),
                   pl.BlockSpec((ts, d_mix), lambda i: (n_t - 1 - i, 0)),
                   whole(g), whole(conv_w), whole(pool_w), whole(pool_scale)],
        out_shape=[
            jax.ShapeDtypeStruct((s_len, d), F32),
            jax.ShapeDtypeStruct((s_len, p_len), LOW),
            jax.ShapeDtypeStruct((s_len, d_mix), LOW),
            jax.ShapeDtypeStruct(g.shape, F32),
            jax.ShapeDtypeStruct(conv_w.shape, F32),
            jax.ShapeDtypeStruct(pool_w.shape, F32),
            jax.ShapeDtypeStruct(pool_scale.shape, F32),
        ],
        scratch_shapes=[pltpu.VMEM((ts + HALO, p_len), F32), pltpu.VMEM((ts + HALO, d_mix), F32)],
        compiler_params=_params("arbitrary"),
    )(dx2, x, proj, proj, g, w_in, conv_w, pool_w, pool_scale, w_out)


def _loss_head(x, g, target, name):
    s_len, d = x.shape
    ts = min(TOK_TILE, s_len)

    def body(x_ref, g_ref, tgt_ref, loss_ref, dx_ref, dg_ref):
        @pl.when(pl.program_id(0) == 0)
        def _():
            loss_ref[...] = jnp.zeros_like(loss_ref)
            dg_ref[...] = jnp.zeros_like(dg_ref)

        xv, gv = x_ref[...], g_ref[...]
        err = xv * _rms_scale(xv) * gv - tgt_ref[...]
        loss_ref[...] += 0.5 * jnp.sum(jnp.mean(err * err, axis=-1, keepdims=True), axis=0, keepdims=True)
        dx, dgp = _rms_bwd(err * (1.0 / d), xv, gv)
        dx_ref[...] = dx
        dg_ref[...] += dgp

    tok = pl.BlockSpec((ts, d), lambda t: (t, 0))
    vec = pl.BlockSpec((1, d), lambda t: (0, 0))
    return pl.pallas_call(
        body,
        name=name,
        grid=(s_len // ts,),
        in_specs=[tok, vec, tok],
        out_specs=[pl.BlockSpec((1, 128), lambda t: (0, 0)), tok, vec],
        out_shape=[
            jax.ShapeDtypeStruct((1, 128), F32),
            jax.ShapeDtypeStruct((s_len, d), F32),
            jax.ShapeDtypeStruct((1, d), F32),
        ],
        compiler_params=_params("arbitrary"),
    )(x, g, target)


def _row_tile(rows, cols, stack_bytes):
    budget = 20 * 1024 * 1024
    per_row = cols * (4 * 7 + stack_bytes)
    for tr in (rows, 512, 256, 176, 128, 64, 32, 16, 8):
        if rows % tr == 0 and tr % 8 == 0 and tr * per_row * 2 <= budget:
            return tr
    return rows


def _sum_stack(stack, name):
    n, r, c = stack.shape
    tr = _row_tile(r, c, n * stack.dtype.itemsize)

    def body(s_ref, o_ref):
        acc = s_ref[0].astype(F32)
        for k in range(1, n):
            acc = acc + s_ref[k].astype(F32)
        o_ref[...] = acc

    return pl.pallas_call(
        body,
        name=name,
        grid=(r // tr,),
        in_specs=[pl.BlockSpec((n, tr, c), lambda i: (0, i, 0))],
        out_specs=pl.BlockSpec((tr, c), lambda i: (i, 0)),
        out_shape=jax.ShapeDtypeStruct((r, c), F32),
        compiler_params=_params("arbitrary"),
    )(stack)


def _adamw(stack, w, m, v, name):
    n, r, c = stack.shape
    tr = _row_tile(r, c, n * stack.dtype.itemsize)
    c1 = 1.0 - ADAM_B1 ** ADAM_STEP
    c2 = 1.0 - ADAM_B2 ** ADAM_STEP

    def body(s_ref, w_ref, m_ref, v_ref, g_ref, d_ref, mo_ref, vo_ref):
        gv = s_ref[0].astype(F32)
        for k in range(1, n):
            gv = gv + s_ref[k].astype(F32)
        mn = ADAM_B1 * m_ref[...] + (1.0 - ADAM_B1) * gv
        vn = ADAM_B2 * v_ref[...] + (1.0 - ADAM_B2) * (gv * gv)
        g_ref[...] = gv
        mo_ref[...] = mn
        vo_ref[...] = vn
        d_ref[...] = -ADAM_LR * ((mn / c1) / (jnp.sqrt(vn / c2) + ADAM_EPS) + ADAM_WD * w_ref[...])

    blk = pl.BlockSpec((tr, c), lambda i: (i, 0))
    return pl.pallas_call(
        body,
        name=name,
        grid=(r // tr,),
        in_specs=[pl.BlockSpec((n, tr, c), lambda i: (0, i, 0)), blk, blk, blk],
        out_specs=[blk] * 4,
        out_shape=[jax.ShapeDtypeStruct((r, c), F32)] * 4,
        compiler_params=_params("arbitrary"),
    )(stack, w, m, v)


def _to_sheet(parts):
    sheets, spans = [], []
    row = 0
    for p in parts:
        flat = p.reshape(-1).astype(F32)
        rows = -(-flat.shape[0] // 1024) * 8
        flat = jnp.pad(flat, (0, rows * 128 - flat.shape[0]))
        sheets.append(flat.reshape(rows, 128))
        spans.append((row, p.size, p.shape))
        row += rows
    return jnp.concatenate(sheets, axis=0), spans


def _from_sheet(sheet, spans):
    out = []
    for row, size, shape in spans:
        rows = -(-size // 1024) * 8
        out.append(sheet[row:row + rows].reshape(-1)[:size].reshape(shape))
    return out


def kernel(x, norm_ffn1, ffn1_w_gate, ffn1_w_up, ffn1_w_down, norm_mix, w_in, conv_w, pool_w, pool_scale, w_out, norm_ffn2, ffn2_w_gate, ffn2_w_up, ffn2_w_down, norm_final, loss_target, m_norm_ffn1, m_ffn1_w_gate, m_ffn1_w_up, m_ffn1_w_down, m_norm_mix, m_w_in, m_conv_w, m_pool_w, m_pool_scale, m_w_out, m_norm_ffn2, m_ffn2_w_gate, m_ffn2_w_up, m_ffn2_w_down, m_norm_final, v_norm_ffn1, v_ffn1_w_gate, v_ffn1_w_up, v_ffn1_w_down, v_norm_mix, v_w_in, v_conv_w, v_pool_w, v_pool_scale, v_w_out, v_norm_ffn2, v_ffn2_w_gate, v_ffn2_w_up, v_ffn2_w_down, v_norm_final):
    me = 4 * lax.axis_index("x") + 2 * lax.axis_index("y") + lax.axis_index("c")
    xs, tgt = x[0], loss_target[0]
    s_len, d = xs.shape
    f_shard = ffn1_w_down.shape[1]
    f_len = N_DEV * f_shard
    conv_shard = conv_w.shape[2]

    def low_t(wt):
        return wt[0].T.astype(LOW)

    def by_dev(gw):
        return gw.reshape(N_DEV, -1, d)

    conv_tile = jnp.zeros((8, 128), F32).at[0:conv_w.shape[1], 0:conv_shard].set(conv_w[0])
    full = _exchange([low_t(ffn1_w_gate), low_t(ffn1_w_up), ffn1_w_down[0].astype(LOW)], [False] * 3, "gather_ffn1",
                     relay=True)
    wg1, wu1, wd1 = (w.reshape(f_len, d) for w in full)
    pool_w_low = pool_w[0].astype(LOW)

    later = _Exchange([w_in[0].astype(LOW), w_out[0].astype(LOW), conv_tile, ffn2_w_down[0].astype(LOW),
                       low_t(ffn2_w_up)], [False] * 5, relay_at=0.8)
    x1, h1, a1, b1, w_in_full, w_out_full, conv_tiles, wd2, wu2 = _ffn_fwd(
        xs, norm_ffn1, wg1, wu1, wd1, "ffn1_fwd", exchange=later)
    w_out_full, wd2, wu2 = w_out_full.reshape(-1, d), wd2.reshape(f_len, d), wu2.reshape(f_len, d)
    conv_full = jnp.concatenate([conv_tiles[k, 0:conv_w.shape[1], 0:conv_shard] for k in range(N_DEV)], axis=1)
    later = _Exchange([low_t(ffn2_w_gate)], [False], relay_at=0.7)
    x2, h2, proj, wg2 = _mix_fwd(
        x1, norm_mix, w_in_full, conv_full, pool_w_low, pool_scale, w_out_full, "mix_fwd", exchange=later)
    wg2 = wg2.reshape(f_len, d)
    x3, h3, a2, b2 = _ffn_fwd(x2, norm_ffn2, wg2, wu2, wd2, "ffn2_fwd")
    loss_row, dx3, dg_final = _loss_head(x3, norm_final.reshape(1, d), tgt, "loss_head")

    dx2, dg_ffn2, da, db, sv, df = _ffn_dgrad(dx3, x2, norm_ffn2, a2, b2, wg2, wu2, wd2, "ffn2_dgrad")
    dwd2, dwg2, dwu2 = _ffn_wgrad(sv, da, db, df, h3, "ffn2_wgrad")
    dx1, dproj, cat, dg_mix, dconv, dpool_w, dpool_scale = _mix_bwd(
        dx2, x1, proj, norm_mix, w_in_full, conv_full, pool_w_low, pool_scale, w_out_full, "mix_bwd")
    dw_in = _wgrad_tn(h2, dproj, W_IN_SHARD, True, "w_in_wgrad")
    dw_out = _wgrad_tn(cat, dx2, d, False, "w_out_wgrad")
    earlier = _Exchange([by_dev(dwg2), by_dev(dwu2), by_dev(dwd2)], [True] * 3)
    dx0, dg_ffn1, da, db, sv, df, got_g2, got_u2, got_d2 = _ffn_dgrad(
        dx1, xs, norm_ffn1, a1, b1, wg1, wu1, wd1, "ffn1_dgrad", exchange=earlier)
    small_parts = [dg_ffn1, dg_mix, dg_ffn2, dg_final, dconv, dpool_w, dpool_scale]
    small_sheet, spans = _to_sheet(small_parts)
    earlier = _Exchange([dw_in, by_dev(dw_out), small_sheet], [True, True, False])
    dwd1, dwg1, dwu1, got_in, got_out, got_small = _ffn_wgrad(sv, da, db, df, h1, "ffn1_wgrad", exchange=earlier)
    got_g1, got_u1, got_d1 = _exchange([by_dev(dwg1), by_dev(dwu1), by_dev(dwd1)], [True] * 3, "scatter_ffn1")
    got = [got_g1, got_u1, got_d1, got_in, got_out, got_g2, got_u2, got_d2, got_small]

    outs = {}

    def update(name, stack, w, m, v):
        outs[name] = _adamw(stack, w[0], m[0], v[0], "adamw_" + name)

    def update_t(name, stack, w, m, v):
        g = _sum_stack(stack, "sum_" + name).T
        outs[name] = _adamw(g[None], w[0], m[0], v[0], "adamw_" + name)

    update_t("ffn1_w_gate", got[0], ffn1_w_gate, m_ffn1_w_gate, v_ffn1_w_gate)
    update_t("ffn1_w_up", got[1], ffn1_w_up, m_ffn1_w_up, v_ffn1_w_up)
    update("ffn1_w_down", got[2], ffn1_w_down, m_ffn1_w_down, v_ffn1_w_down)
    update("w_in", got[3], w_in, m_w_in, v_w_in)
    update("w_out", got[4], w_out, m_w_out, v_w_out)
    update_t("ffn2_w_gate", got[5], ffn2_w_gate, m_ffn2_w_gate, v_ffn2_w_gate)
    update_t("ffn2_w_up", got[6], ffn2_w_up, m_ffn2_w_up, v_ffn2_w_up)
    update("ffn2_w_down", got[7], ffn2_w_down, m_ffn2_w_down, v_ffn2_w_down)

    g_small = _from_sheet(_sum_stack(got[8], "sum_small"), spans)
    g_conv = lax.dynamic_slice_in_dim(g_small[4], me * conv_shard, conv_shard, axis=1)
    small_names = ["norm_ffn1", "norm_mix", "norm_ffn2", "norm_final", "conv_w", "pool_w", "pool_scale"]
    small_g = [g_small[0], g_small[1], g_small[2], g_small[3].reshape(norm_final.shape), g_conv[None],
               g_small[5][None], g_small[6]]
    small_w = [norm_ffn1, norm_mix, norm_ffn2, norm_final, conv_w, pool_w, pool_scale]
    small_m = [m_norm_ffn1, m_norm_mix, m_norm_ffn2, m_norm_final, m_conv_w, m_pool_w, m_pool_scale]
    small_v = [v_norm_ffn1, v_norm_mix, v_norm_ffn2, v_norm_final, v_conv_w, v_pool_w, v_pool_scale]
    g_sheet, spans_u = _to_sheet(small_g)
    w_sheet, _ = _to_sheet(small_w)
    m_sheet, _ = _to_sheet(small_m)
    v_sheet, _ = _to_sheet(small_v)
    upd = _adamw(g_sheet[None], w_sheet, m_sheet, v_sheet, "adamw_small")
    small_out = [_from_sheet(u, spans_u) for u in upd]
    for k, nm in enumerate(small_names):
        outs[nm] = tuple(small_out[j][k] for j in range(4))

    loss = lax.psum(loss_row[0, 0], ("x", "y", "c"))
    order = ["norm_ffn1", "ffn1_w_gate", "ffn1_w_up", "ffn1_w_down", "norm_mix", "w_in", "conv_w", "pool_w",
             "pool_scale", "w_out", "norm_ffn2", "ffn2_w_gate", "ffn2_w_up", "ffn2_w_down", "norm_final"]
    big = {"ffn1_w_gate", "ffn1_w_up", "ffn1_w_down", "w_in", "w_out", "ffn2_w_gate", "ffn2_w_up", "ffn2_w_down"}

    def leaf(nm, j):
        val = outs[nm][j]
        return val[None] if nm in big else val

    return (loss, dx0[None],
            *[leaf(nm, 0) for nm in order], *[leaf(nm, 1) for nm in order],
            *[leaf(nm, 2) for nm in order], *[leaf(nm, 3) for nm in order])
```

```python
import jax
import jax.numpy as jnp
from jax import lax
from jax.experimental import pallas as pl
from jax.experimental.pallas import tpu as pltpu

F32 = jnp.float32
LOW = jnp.bfloat16

N_DEV = 8
EPS = 1e-6
D_CONV = 512
POOL_WINDOWS = (2, 4, 8, 16)
POOL_GC = 128
HALO = 16
W_IN_SHARD = 256

ADAM_LR = 0.001
ADAM_B1 = 0.9
ADAM_B2 = 0.999
ADAM_EPS = 1e-08
ADAM_WD = 0.01
ADAM_STEP = 10

VMEM_LIMIT_BYTES = 56 * 1024 * 1024
TOK_TILE = 1024
MIX_TOK_TILE = 256
WGRAD_TOK_TILE = 2048
FF_TILE_CANDIDATES = (256, 128)
WGRAD_ROW_CANDIDATES = (256, 128)


def _params(*sem):
    return pltpu.CompilerParams(dimension_semantics=sem, vmem_limit_bytes=VMEM_LIMIT_BYTES)


def _pick(n, candidates):
    for c in candidates:
        if n % c == 0:
            return c
    raise ValueError(f"no tile in {candidates} divides {n}")


def _dot(a, b):
    return lax.dot_general(a, b, (((1,), (0,)), ((), ())), preferred_element_type=F32)


def _dot_nt(a, b):
    return lax.dot_general(a, b, (((1,), (1,)), ((), ())), preferred_element_type=F32)


def _dot_tn(a, b):
    return lax.dot_general(a, b, (((0,), (0,)), ((), ())), preferred_element_type=F32)


def _rms_scale(x):
    return lax.rsqrt(jnp.mean(x * x, axis=-1, keepdims=True) + EPS)


def _rms_bwd(dy, x, g):
    r = _rms_scale(x)
    xhat = x * r
    gdy = dy * g
    dx = r * (gdy - xhat * jnp.mean(gdy * xhat, axis=-1, keepdims=True))
    return dx, jnp.sum(dy * xhat, axis=0, keepdims=True)


class _Exchange:
    CHIPS = (2, 4, 6)

    def __init__(self, arrays, sliced, relay_at=None, among_chips=False):
        assert relay_at is None or not any(sliced)
        assert not among_chips or (all(sliced) and relay_at is None)
        self.relay_at, self.among_chips = relay_at, among_chips
        self.peers = self.CHIPS if among_chips else ((1,) + self.CHIPS if relay_at is not None else tuple(range(1, N_DEV)))
        self.arrays, self.sliced, self.n = list(arrays), list(sliced), len(arrays)
        self.out_shape = [jax.ShapeDtypeStruct(arr.shape if sl else (N_DEV,) + arr.shape, arr.dtype)
                          for arr, sl in zip(arrays, sliced)]
        self.specs = [pl.BlockSpec(memory_space=pl.ANY)] * self.n
        self.scratch_shapes = [pltpu.SemaphoreType.DMA((self.n, N_DEV - 1)),
                               pltpu.SemaphoreType.DMA((self.n, N_DEV - 1)),
                               pltpu.SemaphoreType.DMA((self.n,))]

    def _copies(self, ins, outs, sems):
        send_sems, recv_sems, local_sems = sems
        sliced = self.sliced
        mx, my, mc = lax.axis_index("x"), lax.axis_index("y"), lax.axis_index("c")
        me = 2 * mx + my if self.among_chips else 4 * mx + 2 * my + mc

        def peer(m):
            px = lax.rem(mx + ((m >> 2) & 1), 2)
            py = lax.rem(my + ((m >> 1) & 1), 2)
            pc = lax.rem(mc + (m & 1), 2)
            return (px, py, pc), (2 * px + py if self.among_chips else 4 * px + 2 * py + pc)

        def remote(a, m, arriving):
            pid, pflat = peer(m)
            return pltpu.make_async_remote_copy(
                src_ref=ins[a].at[pflat] if sliced[a] else ins[a],
                dst_ref=outs[a].at[pflat if arriving else me],
                send_sem=send_sems.at[a, m - 1],
                recv_sem=recv_sems.at[a, m - 1],
                device_id=pid,
                device_id_type=pl.DeviceIdType.MESH,
            )

        def local(a):
            return pltpu.make_async_copy(ins[a].at[me] if sliced[a] else ins[a], outs[a].at[me], local_sems.at[a])

        def passed_on(a, m):
            _, origin = peer(m)
            sibling, _ = peer(1)
            return pltpu.make_async_remote_copy(
                src_ref=outs[a].at[origin],
                dst_ref=outs[a].at[origin],
                send_sem=send_sems.at[a, m],
                recv_sem=recv_sems.at[a, m],
                device_id=sibling,
                device_id_type=pl.DeviceIdType.MESH,
            )

        return remote, local, passed_on

    def start(self, ins, outs, sems):
        remote, local, _ = self._copies(ins, outs, sems)
        for a in range(self.n):
            local(a).start()
        for m in self.peers:
            for a in range(self.n):
                remote(a, m, False).start()

    def relay(self, ins, outs, sems):
        remote, _, passed_on = self._copies(ins, outs, sems)
        for m in self.CHIPS:
            for a in range(self.n):
                remote(a, m, True).wait_recv()
                passed_on(a, m).start()

    def wait(self, ins, outs, sems):
        remote, local, passed_on = self._copies(ins, outs, sems)
        if self.relay_at is None:
            for m in self.peers:
                for a in range(self.n):
                    remote(a, m, True).wait_recv()
            for m in self.peers:
                for a in range(self.n):
                    remote(a, m, False).wait_send()
        else:
            for m in (1,) + tuple(c + 1 for c in self.CHIPS):
                for a in range(self.n):
                    remote(a, m, True).wait_recv()
            for m in self.peers:
                for a in range(self.n):
                    remote(a, m, False).wait_send()
            for m in self.CHIPS:
                for a in range(self.n):
                    passed_on(a, m).wait_send()
        for a in range(self.n):
            local(a).wait()


def _pair_sum(stacks, name):
    n = len(stacks)
    n_chip = N_DEV // 2
    half = [(n_chip,) + st.shape[1:] for st in stacks]

    def body(*refs):
        ins, outs, mine, theirs = refs[:n], refs[n:2 * n], refs[2 * n:3 * n], refs[3 * n:4 * n]
        local_sems, send_sems, recv_sems = refs[4 * n:]
        mx, my, mc = lax.axis_index("x"), lax.axis_index("y"), lax.axis_index("c")

        def own(a, k):
            return pltpu.make_async_copy(ins[a].at[2 * k + mc], mine[a].at[k], local_sems.at[a, k])

        def swap(a, k):
            return pltpu.make_async_remote_copy(
                src_ref=ins[a].at[2 * k + (1 - mc)], dst_ref=theirs[a].at[k],
                send_sem=send_sems.at[a, k], recv_sem=recv_sems.at[a, k],
                device_id=(mx, my, 1 - mc), device_id_type=pl.DeviceIdType.MESH)

        for k in range(n_chip):
            for a in range(n):
                own(a, k).start()
                swap(a, k).start()
        for k in range(n_chip):
            for a in range(n):
                own(a, k).wait()
                swap(a, k).wait()
                outs[a][k] = (mine[a][k].astype(F32) + theirs[a][k].astype(F32)).astype(LOW)

    return pl.pallas_call(
        body, name=name,
        out_shape=[jax.ShapeDtypeStruct(h, LOW) for h in half],
        in_specs=[pl.BlockSpec(memory_space=pl.ANY)] * n,
        out_specs=[pl.BlockSpec(memory_space=pltpu.VMEM)] * n,
        scratch_shapes=([pltpu.VMEM(h, st.dtype) for h, st in zip(half, stacks)] * 2
                        + [pltpu.SemaphoreType.DMA((n, n_chip))] * 3),
        compiler_params=pltpu.CompilerParams(vmem_limit_bytes=VMEM_LIMIT_BYTES),
    )(*stacks)


def _exchange(arrays, sliced, name, relay=False):
    ex = _Exchange(arrays, sliced, relay_at=0 if relay else None)

    def body(*refs):
        ins, outs, sems = refs[:ex.n], refs[ex.n:2 * ex.n], refs[2 * ex.n:]
        ex.start(ins, outs, sems)
        if relay:
            ex.relay(ins, outs, sems)
        ex.wait(ins, outs, sems)

    return pl.pallas_call(body, name=name, out_shape=ex.out_shape, in_specs=ex.specs, out_specs=ex.specs,
                          scratch_shapes=ex.scratch_shapes)(*arrays)


def _call(body, *, name, grid, in_specs, out_specs, out_shape, args, scratch_shapes=(), exchange=None):
    params = _params(*(("arbitrary",) * len(grid)))
    if exchange is None:
        return pl.pallas_call(body, name=name, grid=grid, in_specs=in_specs, out_specs=out_specs, out_shape=out_shape,
                              scratch_shapes=list(scratch_shapes), compiler_params=params)(*args)
    ex = exchange
    n_in, n_out, n_scr = len(in_specs), len(out_specs), len(scratch_shapes)
    n_steps = 1
    for g in grid:
        n_steps *= g

    def hosted(*refs):
        ins, refs = refs[:n_in], refs[n_in:]
        ex_ins, refs = refs[:ex.n], refs[ex.n:]
        outs, refs = refs[:n_out], refs[n_out:]
        ex_outs, refs = refs[:ex.n], refs[ex.n:]
        scr, sems = refs[:n_scr], refs[n_scr:]
        step = pl.program_id(0)
        for ax in range(1, len(grid)):
            step = step * grid[ax] + pl.program_id(ax)

        @pl.when(step == 0)
        def _():
            ex.start(ex_ins, ex_outs, sems)

        body(*ins, *outs, *scr)

        if ex.relay_at is not None:
            @pl.when(step == min(int(ex.relay_at * n_steps), n_steps - 1))
            def _():
                ex.relay(ex_ins, ex_outs, sems)

        @pl.when(step == n_steps - 1)
        def _():
            ex.wait(ex_ins, ex_outs, sems)

    return pl.pallas_call(
        hosted, name=name, grid=grid,
        in_specs=list(in_specs) + ex.specs, out_specs=list(out_specs) + ex.specs,
        out_shape=list(out_shape) + ex.out_shape,
        scratch_shapes=list(scratch_shapes) + ex.scratch_shapes,
        compiler_params=params)(*args, *ex.arrays)


def _ffn_fwd(x, g, wgt, wut, wd, name, exchange=None):
    s_len, d = x.shape
    f_len = wd.shape[0]
    ts = min(TOK_TILE, s_len)
    fc = _pick(f_len, FF_TILE_CANDIDATES)

    def body(x_ref, g_ref, wg_ref, wu_ref, wd_ref, xo_ref, h_ref, a_ref, b_ref):
        @pl.when(pl.program_id(1) == 0)
        def _():
            xv = x_ref[...]
            h_ref[...] = (xv * _rms_scale(xv) * g_ref[...]).astype(LOW)
            xo_ref[...] = xv

        hb = h_ref[...]
        a = _dot_nt(hb, wg_ref[...])
        b = _dot_nt(hb, wu_ref[...])
        s = a * jax.nn.sigmoid(a) * b
        xo_ref[...] += 0.5 * _dot(s.astype(LOW), wd_ref[...])
        a_ref[...] = a.astype(LOW)
        b_ref[...] = b.astype(LOW)

    tok = pl.BlockSpec((ts, d), lambda t, f: (t, 0))
    wspec = pl.BlockSpec((fc, d), lambda t, f: (f, 0))
    hid = pl.BlockSpec((ts, fc), lambda t, f: (t, f))
    return _call(
        body,
        name=name,
        grid=(s_len // ts, f_len // fc),
        in_specs=[tok, pl.BlockSpec((1, d), lambda t, f: (0, 0)), wspec, wspec, wspec],
        out_specs=[tok, tok, hid, hid],
        out_shape=[
            jax.ShapeDtypeStruct((s_len, d), F32),
            jax.ShapeDtypeStruct((s_len, d), LOW),
            jax.ShapeDtypeStruct((s_len, f_len), LOW),
            jax.ShapeDtypeStruct((s_len, f_len), LOW),
        ],
        args=(x, g, wgt, wut, wd),
        exchange=exchange,
    )


def _ffn_dgrad(dxo, x, g, a, b, wgt, wut, wd, name, exchange=None):
    s_len, d = x.shape
    f_len = wd.shape[0]
    ts = min(TOK_TILE, s_len)
    fc = _pick(f_len, FF_TILE_CANDIDATES)
    n_f = f_len // fc

    def body(dxo_ref, x_ref, g_ref, a_ref, b_ref, wg_ref, wu_ref, wd_ref,
             dxi_ref, dg_ref, da_ref, db_ref, s_ref, df_ref):
        t, f = pl.program_id(0), pl.program_id(1)

        @pl.when((t == 0) & (f == 0))
        def _():
            dg_ref[...] = jnp.zeros_like(dg_ref)

        @pl.when(f == 0)
        def _():
            df_ref[...] = (0.5 * dxo_ref[...]).astype(LOW)
            dxi_ref[...] = jnp.zeros_like(dxi_ref)

        ds = _dot_nt(df_ref[...], wd_ref[...])
        av = a_ref[...].astype(F32)
        bv = b_ref[...].astype(F32)
        sig = jax.nn.sigmoid(av)
        silu = av * sig
        s_ref[...] = (silu * bv).astype(LOW)
        da = (ds * bv * (sig * (1.0 + av * (1.0 - sig)))).astype(LOW)
        db = (ds * silu).astype(LOW)
        da_ref[...] = da
        db_ref[...] = db
        dxi_ref[...] += _dot(da, wg_ref[...]) + _dot(db, wu_ref[...])

        @pl.when(f == n_f - 1)
        def _():
            dx, dgp = _rms_bwd(dxi_ref[...], x_ref[...], g_ref[...])
            dxi_ref[...] = dxo_ref[...] + dx
            dg_ref[...] += dgp

    tok = pl.BlockSpec((ts, d), lambda t, f: (t, 0))
    vec = pl.BlockSpec((1, d), lambda t, f: (0, 0))
    wspec = pl.BlockSpec((fc, d), lambda t, f: (f, 0))
    hid = pl.BlockSpec((ts, fc), lambda t, f: (t, f))
    return _call(
        body,
        name=name,
        grid=(s_len // ts, n_f),
        in_specs=[tok, tok, vec, hid, hid, wspec, wspec, wspec],
        out_specs=[tok, vec, hid, hid, hid, tok],
        out_shape=[
            jax.ShapeDtypeStruct((s_len, d), F32),
            jax.ShapeDtypeStruct((1, d), F32),
            jax.ShapeDtypeStruct((s_len, f_len), LOW),
            jax.ShapeDtypeStruct((s_len, f_len), LOW),
            jax.ShapeDtypeStruct((s_len, f_len), LOW),
            jax.ShapeDtypeStruct((s_len, d), LOW),
        ],
        args=(dxo, x, g, a, b, wgt, wut, wd),
        exchange=exchange,
    )


def _ffn_dact(dxo, a, b, wd, name, exchange=None):
    s_len, d = dxo.shape
    f_len = wd.shape[0]
    ts = min(TOK_TILE, s_len)
    fc = _pick(f_len, FF_TILE_CANDIDATES)

    def body(dxo_ref, a_ref, b_ref, wd_ref, da_ref, db_ref, s_ref, df_ref):
        @pl.when(pl.program_id(1) == 0)
        def _():
            df_ref[...] = (0.5 * dxo_ref[...]).astype(LOW)

        ds = _dot_nt(df_ref[...], wd_ref[...])
        av = a_ref[...].astype(F32)
        bv = b_ref[...].astype(F32)
        sig = jax.nn.sigmoid(av)
        silu = av * sig
        s_ref[...] = (silu * bv).astype(LOW)
        da_ref[...] = (ds * bv * (sig * (1.0 + av * (1.0 - sig)))).astype(LOW)
        db_ref[...] = (ds * silu).astype(LOW)

    tok = pl.BlockSpec((ts, d), lambda t, f: (t, 0))
    hid = pl.BlockSpec((ts, fc), lambda t, f: (t, f))
    return _call(
        body,
        name=name,
        grid=(s_len // ts, f_len // fc),
        in_specs=[tok, hid, hid, pl.BlockSpec((fc, d), lambda t, f: (f, 0))],
        out_specs=[hid, hid, hid, tok],
        out_shape=[jax.ShapeDtypeStruct((s_len, f_len), LOW)] * 3 + [jax.ShapeDtypeStruct((s_len, d), LOW)],
        args=(dxo, a, b, wd),
        exchange=exchange,
    )


def _ffn_dx(dxo, x, g, da, db, wgt, wut, name, exchange=None):
    s_len, d = x.shape
    f_len = wgt.shape[0]
    ts = min(TOK_TILE, s_len)
    fc = _pick(f_len, FF_TILE_CANDIDATES)
    n_f = f_len // fc

    def body(dxo_ref, x_ref, g_ref, da_ref, db_ref, wg_ref, wu_ref, dxi_ref, dg_ref):
        t, f = pl.program_id(0), pl.program_id(1)

        @pl.when((t == 0) & (f == 0))
        def _():
            dg_ref[...] = jnp.zeros_like(dg_ref)

        @pl.when(f == 0)
        def _():
            dxi_ref[...] = jnp.zeros_like(dxi_ref)

        dxi_ref[...] += _dot(da_ref[...], wg_ref[...]) + _dot(db_ref[...], wu_ref[...])

        @pl.when(f == n_f - 1)
        def _():
            dx, dgp = _rms_bwd(dxi_ref[...], x_ref[...], g_ref[...])
            dxi_ref[...] = dxo_ref[...] + dx
            dg_ref[...] += dgp

    tok = pl.BlockSpec((ts, d), lambda t, f: (t, 0))
    vec = pl.BlockSpec((1, d), lambda t, f: (0, 0))
    wspec = pl.BlockSpec((fc, d), lambda t, f: (f, 0))
    hid = pl.BlockSpec((ts, fc), lambda t, f: (t, f))
    return _call(
        body,
        name=name,
        grid=(s_len // ts, n_f),
        in_specs=[tok, tok, vec, hid, hid, wspec, wspec],
        out_specs=[tok, vec],
        out_shape=[jax.ShapeDtypeStruct((s_len, d), F32), jax.ShapeDtypeStruct((1, d), F32)],
        args=(dxo, x, g, da, db, wgt, wut),
        exchange=exchange,
    )


def _ffn_wgrad(s, da, db, df, h, name, exchange=None):
    s_len, f_len = s.shape
    d = h.shape[1]
    tm = _pick(f_len, WGRAD_ROW_CANDIDATES)
    tk = min(WGRAD_TOK_TILE, s_len)
    n_k = s_len // tk

    def body(s_ref, da_ref, db_ref, df_ref, h_ref, dwd_ref, dwg_ref, dwu_ref, acc_d, acc_g, acc_u):
        k = pl.program_id(1)

        @pl.when(k == 0)
        def _():
            acc_d[...] = jnp.zeros_like(acc_d)
            acc_g[...] = jnp.zeros_like(acc_g)
            acc_u[...] = jnp.zeros_like(acc_u)

        hv = h_ref[...]
        acc_d[...] += _dot_tn(s_ref[...], df_ref[...])
        acc_g[...] += _dot_tn(da_ref[...], hv)
        acc_u[...] += _dot_tn(db_ref[...], hv)

        @pl.when(k == n_k - 1)
        def _():
            dwd_ref[...] = acc_d[...].astype(LOW)
            dwg_ref[...] = acc_g[...].astype(LOW)
            dwu_ref[...] = acc_u[...].astype(LOW)

    hid = pl.BlockSpec((tk, tm), lambda i, k: (k, i))
    tok = pl.BlockSpec((tk, d), lambda i, k: (k, 0))
    out = pl.BlockSpec((tm, d), lambda i, k: (i, 0))
    return _call(
        body,
        name=name,
        grid=(f_len // tm, n_k),
        in_specs=[hid, hid, hid, tok, tok],
        out_specs=[out, out, out],
        out_shape=[jax.ShapeDtypeStruct((f_len, d), LOW)] * 3,
        scratch_shapes=[pltpu.VMEM((tm, d), F32)] * 3,
        args=(s, da, db, df, h),
        exchange=exchange,
    )


def _wgrad_tn(xm, ym, tn, stacked, name):
    s_len, m = xm.shape
    n = ym.shape[1]
    tk = min(WGRAD_TOK_TILE, s_len)
    n_k = s_len // tk

    def body(x_ref, y_ref, o_ref, acc):
        k = pl.program_id(1)

        @pl.when(k == 0)
        def _():
            acc[...] = jnp.zeros_like(acc)

        acc[...] += _dot_tn(x_ref[...].astype(LOW), y_ref[...].astype(LOW))

        @pl.when(k == n_k - 1)
        def _():
            o_ref[...] = acc[...].astype(LOW)

    if stacked:
        out_spec = pl.BlockSpec((None, m, tn), lambda j, k: (j, 0, 0))
        out_shape = jax.ShapeDtypeStruct((n // tn, m, tn), LOW)
    else:
        out_spec = pl.BlockSpec((m, tn), lambda j, k: (0, j))
        out_shape = jax.ShapeDtypeStruct((m, n), LOW)
    return pl.pallas_call(
        body,
        name=name,
        grid=(n // tn, n_k),
        in_specs=[pl.BlockSpec((tk, m), lambda j, k: (k, 0)), pl.BlockSpec((tk, tn), lambda j, k: (k, j))],
        out_specs=out_spec,
        out_shape=out_shape,
        scratch_shapes=[pltpu.VMEM((m, tn), F32)],
        compiler_params=_params("arbitrary", "arbitrary"),
    )(xm, ym)


def _mix_parts(ext_ref, cw, ts, row0):
    dc = D_CONV

    def back(off, c0, c1):
        return ext_ref[HALO - off:HALO - off + ts, c0:c1]

    v, gate_b, gate_c = back(0, 0, dc), back(0, dc, 2 * dc), back(0, 2 * dc, 3 * dc)
    z0 = gate_c * v
    z1 = back(1, 2 * dc, 3 * dc) * back(1, 0, dc)
    z2 = back(2, 2 * dc, 3 * dc) * back(2, 0, dc)
    conv = cw[2:3, :] * z0 + cw[1:2, :] * z1 + cw[0:1, :] * z2
    rows = row0 + lax.broadcasted_iota(jnp.int32, (ts, 1), 0)
    pooled, inv_count = [], []
    for grp, w in enumerate(POOL_WINDOWS):
        c0 = 3 * dc + POOL_GC * grp
        u = back(0, c0, c0 + POOL_GC)
        acc = u
        for j in range(1, w):
            acc = acc + back(j, c0, c0 + POOL_GC)
        inv = 1.0 / jnp.minimum(rows + 1, w).astype(F32)
        pooled.append(acc * inv - u)
        inv_count.append(inv)
    return v, gate_b, gate_c, z0, z1, z2, conv, pooled, inv_count


def _mix_fwd(x, g, w_in, conv_w, pool_w, pool_scale, w_out, name, exchange=None):
    s_len, d = x.shape
    n_blk, _, wcols = w_in.shape
    p_len = n_blk * wcols
    d_mix = w_out.shape[0]
    ts = min(MIX_TOK_TILE, s_len)
    dc = D_CONV

    def body(x_ref, g_ref, win_ref, cw_ref, pw_ref, ps_ref, wout_ref, x2_ref, h_ref, proj_ref, ext_ref, cat_ref):
        t = pl.program_id(0)

        @pl.when(t == 0)
        def _():
            ext_ref[0:HALO, :] = jnp.zeros((HALO, p_len), F32)

        xv = x_ref[...]
        hb = (xv * _rms_scale(xv) * g_ref[...]).astype(LOW)
        h_ref[...] = hb
        for k in range(n_blk):
            ext_ref[HALO:HALO + ts, wcols * k:wcols * (k + 1)] = _dot(hb, win_ref[k])
        proj_ref[...] = ext_ref[HALO:HALO + ts, :]

        _, gate_b, _, _, _, _, conv, pooled, _ = _mix_parts(ext_ref, cw_ref[...], ts, t * ts)
        cat_ref[:, 0:dc] = (gate_b * conv).astype(LOW)
        for grp in range(len(POOL_WINDOWS)):
            c0 = POOL_GC * grp
            lin = _dot(pooled[grp].astype(LOW), pw_ref[grp])
            cat_ref[:, dc + c0:dc + c0 + POOL_GC] = (lin * ps_ref[:, c0:c0 + POOL_GC]).astype(LOW)
        x2_ref[...] = xv + _dot(cat_ref[...], wout_ref[...])
        ext_ref[0:HALO, :] = ext_ref[ts:ts + HALO, :]

    tok = pl.BlockSpec((ts, d), lambda t: (t, 0))

    def whole(arr):
        return pl.BlockSpec(arr.shape, lambda t: (0,) * arr.ndim)

    return _call(
        body,
        name=name,
        grid=(s_len // ts,),
        in_specs=[tok, whole(g), whole(w_in), whole(conv_w), whole(pool_w), whole(pool_scale), whole(w_out)],
        out_specs=[tok, tok, pl.BlockSpec((ts, p_len), lambda t: (t, 0))],
        out_shape=[
            jax.ShapeDtypeStruct((s_len, d), F32),
            jax.ShapeDtypeStruct((s_len, d), LOW),
            jax.ShapeDtypeStruct((s_len, p_len), F32),
        ],
        scratch_shapes=[pltpu.VMEM((ts + HALO, p_len), F32), pltpu.VMEM((ts, d_mix), LOW)],
        args=(x, g, w_in, conv_w, pool_w, pool_scale, w_out),
        exchange=exchange,
    )


def _mix_bwd(dx2, x, proj, g, w_in, conv_w, pool_w, pool_scale, w_out, name):
    s_len, d = x.shape
    n_blk, _, wcols = w_in.shape
    p_len = n_blk * wcols
    d_mix = w_out.shape[0]
    ts = min(MIX_TOK_TILE, s_len)
    n_t = s_len // ts
    dc = D_CONV
    n_grp = len(POOL_WINDOWS)

    def body(dx2_ref, x_ref, proj_ref, halo_ref, g_ref, win_ref, cw_ref, pw_ref, ps_ref, wout_ref,
             dx_ref, dproj_ref, cat_ref, dg_ref, dcw_ref, dpw_ref, dps_ref, ext_ref, fut_ref):
        i = pl.program_id(0)
        t = n_t - 1 - i

        @pl.when(i == 0)
        def _():
            dg_ref[...] = jnp.zeros_like(dg_ref)
            dcw_ref[...] = jnp.zeros_like(dcw_ref)
            dpw_ref[...] = jnp.zeros_like(dpw_ref)
            dps_ref[...] = jnp.zeros_like(dps_ref)
            fut_ref[ts:ts + HALO, :] = jnp.zeros((HALO, d_mix), F32)

        ext_ref[HALO:HALO + ts, :] = proj_ref[...]

        @pl.when(t == 0)
        def _():
            ext_ref[0:HALO, :] = jnp.zeros((HALO, p_len), F32)

        @pl.when(t > 0)
        def _():
            ext_ref[0:HALO, :] = halo_ref[...]

        cw = cw_ref[...]
        v, gate_b, gate_c, z0, z1, z2, conv, pooled, inv_count = _mix_parts(ext_ref, cw, ts, t * ts)
        dx2 = dx2_ref[...]
        dcat = _dot_nt(dx2.astype(LOW), wout_ref[...])

        dy_a = dcat[:, 0:dc]
        dconv = dy_a * gate_b
        fut_ref[0:ts, 0:dc] = dconv
        cat_ref[:, 0:dc] = (gate_b * conv).astype(LOW)
        dproj_ref[:, dc:2 * dc] = (dy_a * conv).astype(LOW)
        dcw_ref[2:3, :] += jnp.sum(dconv * z0, axis=0, keepdims=True)
        dcw_ref[1:2, :] += jnp.sum(dconv * z1, axis=0, keepdims=True)
        dcw_ref[0:1, :] += jnp.sum(dconv * z2, axis=0, keepdims=True)

        dpool = []
        for grp in range(n_grp):
            c0 = POOL_GC * grp
            pooled_b = pooled[grp].astype(LOW)
            lin = _dot(pooled_b, pw_ref[grp])
            dy_b = dcat[:, dc + c0:dc + c0 + POOL_GC]
            scale = ps_ref[:, c0:c0 + POOL_GC]
            cat_ref[:, dc + c0:dc + c0 + POOL_GC] = (lin * scale).astype(LOW)
            dps_ref[:, c0:c0 + POOL_GC] += jnp.sum(dy_b * lin, axis=0, keepdims=True)
            dlin = (dy_b * scale).astype(LOW)
            dpw_ref[grp] += _dot_tn(pooled_b, dlin)
            dpool.append(_dot_nt(dlin, pw_ref[grp]))
            fut_ref[0:ts, dc + c0:dc + c0 + POOL_GC] = dpool[grp] * inv_count[grp]

        def ahead(off, c0, c1):
            return fut_ref[off:off + ts, c0:c1]

        dz = cw[2:3, :] * ahead(0, 0, dc) + cw[1:2, :] * ahead(1, 0, dc) + cw[0:1, :] * ahead(2, 0, dc)
        dproj_ref[:, 0:dc] = (dz * gate_c).astype(LOW)
        dproj_ref[:, 2 * dc:3 * dc] = (dz * v).astype(LOW)
        for grp, w in enumerate(POOL_WINDOWS):
            c0 = dc + POOL_GC * grp
            acc = ahead(0, c0, c0 + POOL_GC)
            for j in range(1, w):
                acc = acc + ahead(j, c0, c0 + POOL_GC)
            dproj_ref[:, 2 * dc + c0:2 * dc + c0 + POOL_GC] = (acc - dpool[grp]).astype(LOW)

        dh = _dot_nt(dproj_ref[:, 0:wcols], win_ref[0])
        for k in range(1, n_blk):
            dh += _dot_nt(dproj_ref[:, wcols * k:wcols * (k + 1)], win_ref[k])
        dx, dgp = _rms_bwd(dh, x_ref[...], g_ref[...])
        dx_ref[...] = dx2 + dx
        dg_ref[...] += dgp
        fut_ref[ts:ts + HALO, :] = fut_ref[0:HALO, :]

    tok = pl.BlockSpec((ts, d), lambda i: (n_t - 1 - i, 0))
    halo = pl.BlockSpec((HALO, p_len), lambda i: (jnp.maximum((n_t - 1 - i) * (ts // HALO) - 1, 0), 0))

    def whole(arr):
        return pl.BlockSpec(arr.shape, lambda i: (0,) * arr.ndim)

    return pl.pallas_call(
        body,
        name=name,
        grid=(n_t,),
        in_specs=[tok, tok, pl.BlockSpec((ts, p_len), lambda i: (n_t - 1 - i, 0)), halo,
                  whole(g), whole(w_in), whole(conv_w), whole(pool_w), whole(pool_scale), whole(w_out)],
        out_specs=[tok, pl.BlockSpec((ts, p_len), lambda i: (n_t - 1 - i, 0)),
                   pl.BlockSpec((ts, d_mix), lambda i: (n_t - 1 - i, 0)),
                   whole(g), whole(conv_w), whole(pool_w), whole(pool_scale)],
        out_shape=[
            jax.ShapeDtypeStruct((s_len, d), F32),
            jax.ShapeDtypeStruct((s_len, p_len), LOW),
            jax.ShapeDtypeStruct((s_len, d_mix), LOW),
            jax.ShapeDtypeStruct(g.shape, F32),
            jax.ShapeDtypeStruct(conv_w.shape, F32),
            jax.ShapeDtypeStruct(pool_w.shape, F32),
            jax.ShapeDtypeStruct(pool_scale.shape, F32),
        ],
        scratch_shapes=[pltpu.VMEM((ts + HALO, p_len), F32), pltpu.VMEM((ts + HALO, d_mix), F32)],
        compiler_params=_params("arbitrary"),
    )(dx2, x, proj, proj, g, w_in, conv_w, pool_w, pool_scale, w_out)


def _loss_head(x, g, target, name):
    s_len, d = x.shape
    ts = min(TOK_TILE, s_len)

    def body(x_ref, g_ref, tgt_ref, loss_ref, dx_ref, dg_ref):
        @pl.when(pl.program_id(0) == 0)
        def _():
            loss_ref[...] = jnp.zeros_like(loss_ref)
            dg_ref[...] = jnp.zeros_like(dg_ref)

        xv, gv = x_ref[...], g_ref[...]
        err = xv * _rms_scale(xv) * gv - tgt_ref[...]
        loss_ref[...] += 0.5 * jnp.sum(jnp.mean(err * err, axis=-1, keepdims=True), axis=0, keepdims=True)
        dx, dgp = _rms_bwd(err * (1.0 / d), xv, gv)
        dx_ref[...] = dx
        dg_ref[...] += dgp

    tok = pl.BlockSpec((ts, d), lambda t: (t, 0))
    vec = pl.BlockSpec((1, d), lambda t: (0, 0))
    return pl.pallas_call(
        body,
        name=name,
        grid=(s_len // ts,),
        in_specs=[tok, vec, tok],
        out_specs=[pl.BlockSpec((1, 128), lambda t: (0, 0)), tok, vec],
        out_shape=[
            jax.ShapeDtypeStruct((1, 128), F32),
            jax.ShapeDtypeStruct((s_len, d), F32),
            jax.ShapeDtypeStruct((1, d), F32),
        ],
        compiler_params=_params("arbitrary"),
    )(x, g, target)


def _row_tile(rows, cols, stack_bytes):
    budget = 20 * 1024 * 1024
    per_row = cols * (4 * 7 + stack_bytes)
    for tr in (rows, 512, 256, 176, 128, 64, 32, 16, 8):
        if rows % tr == 0 and tr % 8 == 0 and tr * per_row * 2 <= budget:
            return tr
    return rows


def _sum_stack(stack, name):
    n, r, c = stack.shape
    tr = _row_tile(r, c, n * stack.dtype.itemsize)

    def body(s_ref, o_ref):
        acc = s_ref[0].astype(F32)
        for k in range(1, n):
            acc = acc + s_ref[k].astype(F32)
        o_ref[...] = acc

    return pl.pallas_call(
        body,
        name=name,
        grid=(r // tr,),
        in_specs=[pl.BlockSpec((n, tr, c), lambda i: (0, i, 0))],
        out_specs=pl.BlockSpec((tr, c), lambda i: (i, 0)),
        out_shape=jax.ShapeDtypeStruct((r, c), F32),
        compiler_params=_params("arbitrary"),
    )(stack)


def _adamw(stack, w, m, v, name):
    n, r, c = stack.shape
    tr = _row_tile(r, c, n * stack.dtype.itemsize)
    c1 = 1.0 - ADAM_B1 ** ADAM_STEP
    c2 = 1.0 - ADAM_B2 ** ADAM_STEP

    def body(s_ref, w_ref, m_ref, v_ref, g_ref, d_ref, mo_ref, vo_ref):
        gv = s_ref[0].astype(F32)
        for k in range(1, n):
            gv = gv + s_ref[k].astype(F32)
        mn = ADAM_B1 * m_ref[...] + (1.0 - ADAM_B1) * gv
        vn = ADAM_B2 * v_ref[...] + (1.0 - ADAM_B2) * (gv * gv)
        g_ref[...] = gv
        mo_ref[...] = mn
        vo_ref[...] = vn
        d_ref[...] = -ADAM_LR * ((mn / c1) / (jnp.sqrt(vn / c2) + ADAM_EPS) + ADAM_WD * w_ref[...])

    blk = pl.BlockSpec((tr, c), lambda i: (i, 0))
    return pl.pallas_call(
        body,
        name=name,
        grid=(r // tr,),
        in_specs=[pl.BlockSpec((n, tr, c), lambda i: (0, i, 0)), blk, blk, blk],
        out_specs=[blk] * 4,
        out_shape=[jax.ShapeDtypeStruct((r, c), F32)] * 4,
        compiler_params=_params("arbitrary"),
    )(stack, w, m, v)


def _to_sheet(parts):
    sheets, spans = [], []
    row = 0
    for p in parts:
        flat = p.reshape(-1).astype(F32)
        rows = -(-flat.shape[0] // 1024) * 8
        flat = jnp.pad(flat, (0, rows * 128 - flat.shape[0]))
        sheets.append(flat.reshape(rows, 128))
        spans.append((row, p.size, p.shape))
        row += rows
    return jnp.concatenate(sheets, axis=0), spans


def _from_sheet(sheet, spans):
    out = []
    for row, size, shape in spans:
        rows = -(-size // 1024) * 8
        out.append(sheet[row:row + rows].reshape(-1)[:size].reshape(shape))
    return out


def kernel(x, norm_ffn1, ffn1_w_gate, ffn1_w_up, ffn1_w_down, norm_mix, w_in, conv_w, pool_w, pool_scale, w_out, norm_ffn2, ffn2_w_gate, ffn2_w_up, ffn2_w_down, norm_final, loss_target, m_norm_ffn1, m_ffn1_w_gate, m_ffn1_w_up, m_ffn1_w_down, m_norm_mix, m_w_in, m_conv_w, m_pool_w, m_pool_scale, m_w_out, m_norm_ffn2, m_ffn2_w_gate, m_ffn2_w_up, m_ffn2_w_down, m_norm_final, v_norm_ffn1, v_ffn1_w_gate, v_ffn1_w_up, v_ffn1_w_down, v_norm_mix, v_w_in, v_conv_w, v_pool_w, v_pool_scale, v_w_out, v_norm_ffn2, v_ffn2_w_gate, v_ffn2_w_up, v_ffn2_w_down, v_norm_final):
    me = 4 * lax.axis_index("x") + 2 * lax.axis_index("y") + lax.axis_index("c")
    xs, tgt = x[0], loss_target[0]
    s_len, d = xs.shape
    f_shard = ffn1_w_down.shape[1]
    f_len = N_DEV * f_shard
    conv_shard = conv_w.shape[2]

    def low_t(wt):
        return wt[0].T.astype(LOW)

    def by_dev(gw):
        return gw.reshape(N_DEV, -1, d)

    conv_tile = jnp.zeros((8, 128), F32).at[0:conv_w.shape[1], 0:conv_shard].set(conv_w[0])
    full = _exchange([low_t(ffn1_w_gate), low_t(ffn1_w_up), ffn1_w_down[0].astype(LOW)], [False] * 3, "gather_ffn1",
                     relay=True)
    wg1, wu1, wd1 = (w.reshape(f_len, d) for w in full)
    pool_w_low = pool_w[0].astype(LOW)

    later = _Exchange([w_in[0].astype(LOW), w_out[0].astype(LOW), conv_tile, ffn2_w_down[0].astype(LOW),
                       low_t(ffn2_w_up)], [False] * 5, relay_at=0.8)
    x1, h1, a1, b1, w_in_full, w_out_full, conv_tiles, wd2, wu2 = _ffn_fwd(
        xs, norm_ffn1, wg1, wu1, wd1, "ffn1_fwd", exchange=later)
    w_out_full, wd2, wu2 = w_out_full.reshape(-1, d), wd2.reshape(f_len, d), wu2.reshape(f_len, d)
    conv_full = jnp.concatenate([conv_tiles[k, 0:conv_w.shape[1], 0:conv_shard] for k in range(N_DEV)], axis=1)
    later = _Exchange([low_t(ffn2_w_gate)], [False], relay_at=0.7)
    x2, h2, proj, wg2 = _mix_fwd(
        x1, norm_mix, w_in_full, conv_full, pool_w_low, pool_scale, w_out_full, "mix_fwd", exchange=later)
    wg2 = wg2.reshape(f_len, d)
    x3, h3, a2, b2 = _ffn_fwd(x2, norm_ffn2, wg2, wu2, wd2, "ffn2_fwd")
    loss_row, dx3, dg_final = _loss_head(x3, norm_final.reshape(1, d), tgt, "loss_head")

    dx2, dg_ffn2, da, db, sv, df = _ffn_dgrad(dx3, x2, norm_ffn2, a2, b2, wg2, wu2, wd2, "ffn2_dgrad")
    dwd2, dwg2, dwu2 = _ffn_wgrad(sv, da, db, df, h3, "ffn2_wgrad")
    pair2 = _pair_sum([by_dev(dwg2), by_dev(dwu2), by_dev(dwd2)], "pair_sum_ffn2")
    dx1, dproj, cat, dg_mix, dconv, dpool_w, dpool_scale = _mix_bwd(
        dx2, x1, proj, norm_mix, w_in_full, conv_full, pool_w_low, pool_scale, w_out_full, "mix_bwd")
    dw_in = _wgrad_tn(h2, dproj, W_IN_SHARD, True, "w_in_wgrad")
    dw_out = _wgrad_tn(cat, dx2, d, False, "w_out_wgrad")
    small_parts = [dg_mix, dg_ffn2, dg_final, dconv, dpool_w, dpool_scale]
    small_sheet, spans = _to_sheet(small_parts)
    earlier = _Exchange([dw_in, by_dev(dw_out), small_sheet], [True, True, False])
    da, db, sv, df, got_in, got_out, got_small = _ffn_dact(dx1, a1, b1, wd1, "ffn1_dact", exchange=earlier)
    earlier = _Exchange(pair2, [True] * 3, among_chips=True)
    dwd1, dwg1, dwu1, got_g2, got_u2, got_d2 = _ffn_wgrad(sv, da, db, df, h1, "ffn1_wgrad", exchange=earlier)
    pair1 = _pair_sum([by_dev(dwg1), by_dev(dwu1), by_dev(dwd1)], "pair_sum_ffn1")
    earlier = _Exchange(pair1, [True] * 3, among_chips=True)
    dx0, dg_ffn1, got_g1, got_u1, got_d1 = _ffn_dx(
        dx1, xs, norm_ffn1, da, db, wg1, wu1, "ffn1_dx", exchange=earlier)
    (got_n1,) = _exchange([dg_ffn1.reshape(8, 128)], [False], "gather_dnorm_ffn1")
    got = [got_g1, got_u1, got_d1, got_in, got_out, got_g2, got_u2, got_d2, got_small]

    outs = {}

    def update(name, stack, w, m, v):
        outs[name] = _adamw(stack, w[0], m[0], v[0], "adamw_" + name)

    def update_t(name, stack, w, m, v):
        g = _sum_stack(stack, "sum_" + name).T
        outs[name] = _adamw(g[None], w[0], m[0], v[0], "adamw_" + name)

    update_t("ffn1_w_gate", got[0], ffn1_w_gate, m_ffn1_w_gate, v_ffn1_w_gate)
    update_t("ffn1_w_up", got[1], ffn1_w_up, m_ffn1_w_up, v_ffn1_w_up)
    update("ffn1_w_down", got[2], ffn1_w_down, m_ffn1_w_down, v_ffn1_w_down)
    update("w_in", got[3], w_in, m_w_in, v_w_in)
    update("w_out", got[4], w_out, m_w_out, v_w_out)
    update_t("ffn2_w_gate", got[5], ffn2_w_gate, m_ffn2_w_gate, v_ffn2_w_gate)
    update_t("ffn2_w_up", got[6], ffn2_w_up, m_ffn2_w_up, v_ffn2_w_up)
    update("ffn2_w_down", got[7], ffn2_w_down, m_ffn2_w_down, v_ffn2_w_down)

    g_small = _from_sheet(_sum_stack(got[8], "sum_small"), spans)
    g_norm_ffn1 = _sum_stack(got_n1, "sum_dnorm_ffn1").reshape(norm_ffn1.shape)
    g_conv = lax.dynamic_slice_in_dim(g_small[3], me * conv_shard, conv_shard, axis=1)
    small_names = ["norm_ffn1", "norm_mix", "norm_ffn2", "norm_final", "conv_w", "pool_w", "pool_scale"]
    small_g = [g_norm_ffn1, g_small[0], g_small[1], g_small[2].reshape(norm_final.shape), g_conv[None],
               g_small[4][None], g_small[5]]
    small_w = [norm_ffn1, norm_mix, norm_ffn2, norm_final, conv_w, pool_w, pool_scale]
    small_m = [m_norm_ffn1, m_norm_mix, m_norm_ffn2, m_norm_final, m_conv_w, m_pool_w, m_pool_scale]
    small_v = [v_norm_ffn1, v_norm_mix, v_norm_ffn2, v_norm_final, v_conv_w, v_pool_w, v_pool_scale]
    g_sheet, spans_u = _to_sheet(small_g)
    w_sheet, _ = _to_sheet(small_w)
    m_sheet, _ = _to_sheet(small_m)
    v_sheet, _ = _to_sheet(small_v)
    upd = _adamw(g_sheet[None], w_sheet, m_sheet, v_sheet, "adamw_small")
    small_out = [_from_sheet(u, spans_u) for u in upd]
    for k, nm in enumerate(small_names):
        outs[nm] = tuple(small_out[j][k] for j in range(4))

    loss = lax.psum(loss_row[0, 0], ("x", "y", "c"))
    order = ["norm_ffn1", "ffn1_w_gate", "ffn1_w_up", "ffn1_w_down", "norm_mix", "w_in", "conv_w", "pool_w",
             "pool_scale", "w_out", "norm_ffn2", "ffn2_w_gate", "ffn2_w_up", "ffn2_w_down", "norm_final"]
    big = {"ffn1_w_gate", "ffn1_w_up", "ffn1_w_down", "w_in", "w_out", "ffn2_w_gate", "ffn2_w_up", "ffn2_w_down"}

    def leaf(nm, j):
        val = outs[nm][j]
        return val[None] if nm in big else val

    return (loss, dx0[None],
            *[leaf(nm, 0) for nm in order], *[leaf(nm, 1) for nm in order],
            *[leaf(nm, 2) for nm in order], *[leaf(nm, 3) for nm in order])
```

```python
import jax
import jax.numpy as jnp
from jax import lax
from jax.experimental import pallas as pl
from jax.experimental.pallas import tpu as pltpu

F32 = jnp.float32
LOW = jnp.bfloat16

N_DEV = 8
EPS = 1e-6
D_CONV = 512
POOL_WINDOWS = (2, 4, 8, 16)
POOL_GC = 128
HALO = 16
W_IN_SHARD = 256

ADAM_LR = 0.001
ADAM_B1 = 0.9
ADAM_B2 = 0.999
ADAM_EPS = 1e-08
ADAM_WD = 0.01
ADAM_STEP = 10

VMEM_LIMIT_BYTES = 56 * 1024 * 1024
TOK_TILE = 512
MIX_TOK_TILE = 256
WGRAD_TOK_TILE = 2048
FF_TILE_CANDIDATES = (1408, 256, 128)
WGRAD_ROW_CANDIDATES = (256, 128)


def _params(*sem):
    return pltpu.CompilerParams(dimension_semantics=sem, vmem_limit_bytes=VMEM_LIMIT_BYTES)


def _resident(arr):
    return pl.BlockSpec(arr.shape, lambda *_: (0,) * arr.ndim, pipeline_mode=pl.Buffered(1))


def _pick(n, candidates):
    for c in candidates:
        if n % c == 0:
            return c
    raise ValueError(f"no tile in {candidates} divides {n}")


def _dot(a, b):
    return lax.dot_general(a, b, (((1,), (0,)), ((), ())), preferred_element_type=F32)


def _dot_nt(a, b):
    return lax.dot_general(a, b, (((1,), (1,)), ((), ())), preferred_element_type=F32)


def _dot_tn(a, b):
    return lax.dot_general(a, b, (((0,), (0,)), ((), ())), preferred_element_type=F32)


def _rms_scale(x):
    return lax.rsqrt(jnp.mean(x * x, axis=-1, keepdims=True) + EPS)


def _rms_bwd(dy, x, g):
    r = _rms_scale(x)
    xhat = x * r
    gdy = dy * g
    dx = r * (gdy - xhat * jnp.mean(gdy * xhat, axis=-1, keepdims=True))
    return dx, jnp.sum(dy * xhat, axis=0, keepdims=True)


class _Exchange:
    CHIPS = (2, 4, 6)

    def __init__(self, arrays, sliced, relay_at=None, among_chips=False):
        assert relay_at is None or not any(sliced)
        assert not among_chips or (all(sliced) and relay_at is None)
        self.relay_at, self.among_chips = relay_at, among_chips
        self.peers = self.CHIPS if among_chips else ((1,) + self.CHIPS if relay_at is not None else tuple(range(1, N_DEV)))
        self.arrays, self.sliced, self.n = list(arrays), list(sliced), len(arrays)
        self.out_shape = [jax.ShapeDtypeStruct(arr.shape if sl else (N_DEV,) + arr.shape, arr.dtype)
                          for arr, sl in zip(arrays, sliced)]
        self.specs = [pl.BlockSpec(memory_space=pl.ANY)] * self.n
        self.scratch_shapes = [pltpu.SemaphoreType.DMA((self.n, N_DEV - 1)),
                               pltpu.SemaphoreType.DMA((self.n, N_DEV - 1)),
                               pltpu.SemaphoreType.DMA((self.n,))]

    def _copies(self, ins, outs, sems):
        send_sems, recv_sems, local_sems = sems
        sliced = self.sliced
        mx, my, mc = lax.axis_index("x"), lax.axis_index("y"), lax.axis_index("c")
        me = 2 * mx + my if self.among_chips else 4 * mx + 2 * my + mc

        def peer(m):
            px = lax.rem(mx + ((m >> 2) & 1), 2)
            py = lax.rem(my + ((m >> 1) & 1), 2)
            pc = lax.rem(mc + (m & 1), 2)
            return (px, py, pc), (2 * px + py if self.among_chips else 4 * px + 2 * py + pc)

        def remote(a, m, arriving):
            pid, pflat = peer(m)
            return pltpu.make_async_remote_copy(
                src_ref=ins[a].at[pflat] if sliced[a] else ins[a],
                dst_ref=outs[a].at[pflat if arriving else me],
                send_sem=send_sems.at[a, m - 1],
                recv_sem=recv_sems.at[a, m - 1],
                device_id=pid,
                device_id_type=pl.DeviceIdType.MESH,
            )

        def local(a):
            return pltpu.make_async_copy(ins[a].at[me] if sliced[a] else ins[a], outs[a].at[me], local_sems.at[a])

        def passed_on(a, m):
            _, origin = peer(m)
            sibling, _ = peer(1)
            return pltpu.make_async_remote_copy(
                src_ref=outs[a].at[origin],
                dst_ref=outs[a].at[origin],
                send_sem=send_sems.at[a, m],
                recv_sem=recv_sems.at[a, m],
                device_id=sibling,
                device_id_type=pl.DeviceIdType.MESH,
            )

        return remote, local, passed_on

    def start(self, ins, outs, sems):
        remote, local, _ = self._copies(ins, outs, sems)
        for a in range(self.n):
            local(a).start()
        for m in self.peers:
            for a in range(self.n):
                remote(a, m, False).start()

    def relay(self, ins, outs, sems):
        remote, _, passed_on = self._copies(ins, outs, sems)
        for m in self.CHIPS:
            for a in range(self.n):
                remote(a, m, True).wait_recv()
                passed_on(a, m).start()

    def wait(self, ins, outs, sems):
        remote, local, passed_on = self._copies(ins, outs, sems)
        if self.relay_at is None:
            for m in self.peers:
                for a in range(self.n):
                    remote(a, m, True).wait_recv()
            for m in self.peers:
                for a in range(self.n):
                    remote(a, m, False).wait_send()
        else:
            for m in (1,) + tuple(c + 1 for c in self.CHIPS):
                for a in range(self.n):
                    remote(a, m, True).wait_recv()
            for m in self.peers:
                for a in range(self.n):
                    remote(a, m, False).wait_send()
            for m in self.CHIPS:
                for a in range(self.n):
                    passed_on(a, m).wait_send()
        for a in range(self.n):
            local(a).wait()


def _pair_sum(stacks, name):
    n = len(stacks)
    n_chip = N_DEV // 2
    half = [(n_chip,) + st.shape[1:] for st in stacks]

    def body(*refs):
        ins, outs, mine, theirs = refs[:n], refs[n:2 * n], refs[2 * n:3 * n], refs[3 * n:4 * n]
        local_sems, send_sems, recv_sems = refs[4 * n:]
        mx, my, mc = lax.axis_index("x"), lax.axis_index("y"), lax.axis_index("c")

        def own(a, k):
            return pltpu.make_async_copy(ins[a].at[2 * k + mc], mine[a].at[k], local_sems.at[a, k])

        def swap(a, k):
            return pltpu.make_async_remote_copy(
                src_ref=ins[a].at[2 * k + (1 - mc)], dst_ref=theirs[a].at[k],
                send_sem=send_sems.at[a, k], recv_sem=recv_sems.at[a, k],
                device_id=(mx, my, 1 - mc), device_id_type=pl.DeviceIdType.MESH)

        for k in range(n_chip):
            for a in range(n):
                own(a, k).start()
                swap(a, k).start()
        for k in range(n_chip):
            for a in range(n):
                own(a, k).wait()
                swap(a, k).wait()
                outs[a][k] = (mine[a][k].astype(F32) + theirs[a][k].astype(F32)).astype(LOW)

    return pl.pallas_call(
        body, name=name,
        out_shape=[jax.ShapeDtypeStruct(h, LOW) for h in half],
        in_specs=[pl.BlockSpec(memory_space=pl.ANY)] * n,
        out_specs=[pl.BlockSpec(memory_space=pltpu.VMEM)] * n,
        scratch_shapes=([pltpu.VMEM(h, st.dtype) for h, st in zip(half, stacks)] * 2
                        + [pltpu.SemaphoreType.DMA((n, n_chip))] * 3),
        compiler_params=pltpu.CompilerParams(vmem_limit_bytes=VMEM_LIMIT_BYTES),
    )(*stacks)


def _exchange(arrays, sliced, name, relay=False):
    ex = _Exchange(arrays, sliced, relay_at=0 if relay else None)

    def body(*refs):
        ins, outs, sems = refs[:ex.n], refs[ex.n:2 * ex.n], refs[2 * ex.n:]
        ex.start(ins, outs, sems)
        if relay:
            ex.relay(ins, outs, sems)
        ex.wait(ins, outs, sems)

    return pl.pallas_call(body, name=name, out_shape=ex.out_shape, in_specs=ex.specs, out_specs=ex.specs,
                          scratch_shapes=ex.scratch_shapes)(*arrays)


def _call(body, *, name, grid, in_specs, out_specs, out_shape, args, scratch_shapes=(), exchange=None):
    params = _params(*(("arbitrary",) * len(grid)))
    if exchange is None:
        return pl.pallas_call(body, name=name, grid=grid, in_specs=in_specs, out_specs=out_specs, out_shape=out_shape,
                              scratch_shapes=list(scratch_shapes), compiler_params=params)(*args)
    ex = exchange
    n_in, n_out, n_scr = len(in_specs), len(out_specs), len(scratch_shapes)
    n_steps = 1
    for g in grid:
        n_steps *= g

    def hosted(*refs):
        ins, refs = refs[:n_in], refs[n_in:]
        ex_ins, refs = refs[:ex.n], refs[ex.n:]
        outs, refs = refs[:n_out], refs[n_out:]
        ex_outs, refs = refs[:ex.n], refs[ex.n:]
        scr, sems = refs[:n_scr], refs[n_scr:]
        step = pl.program_id(0)
        for ax in range(1, len(grid)):
            step = step * grid[ax] + pl.program_id(ax)

        @pl.when(step == 0)
        def _():
            ex.start(ex_ins, ex_outs, sems)

        body(*ins, *outs, *scr)

        if ex.relay_at is not None:
            @pl.when(step == min(int(ex.relay_at * n_steps), n_steps - 1))
            def _():
                ex.relay(ex_ins, ex_outs, sems)

        @pl.when(step == n_steps - 1)
        def _():
            ex.wait(ex_ins, ex_outs, sems)

    return pl.pallas_call(
        hosted, name=name, grid=grid,
        in_specs=list(in_specs) + ex.specs, out_specs=list(out_specs) + ex.specs,
        out_shape=list(out_shape) + ex.out_shape,
        scratch_shapes=list(scratch_shapes) + ex.scratch_shapes,
        compiler_params=params)(*args, *ex.arrays)


def _ffn_fwd(x, g, wgt, wut, wd, name, exchange=None):
    s_len, d = x.shape
    f_len = wd.shape[0]
    ts = min(TOK_TILE, s_len)
    fc = _pick(f_len, FF_TILE_CANDIDATES)

    def body(x_ref, g_ref, wg_ref, wu_ref, wd_ref, xo_ref, h_ref, a_ref, b_ref, s_ref):
        @pl.when(pl.program_id(1) == 0)
        def _():
            xv = x_ref[...]
            h_ref[...] = (xv * _rms_scale(xv) * g_ref[...]).astype(LOW)
            xo_ref[...] = xv

        hb = h_ref[...]
        for c0, c1 in _slabs(fc):
            a = _dot_nt(hb, wg_ref[c0:c1, :])
            b = _dot_nt(hb, wu_ref[c0:c1, :])
            s_ref[:, c0:c1] = (a * jax.nn.sigmoid(a) * b).astype(LOW)
            a_ref[:, c0:c1] = a.astype(LOW)
            b_ref[:, c0:c1] = b.astype(LOW)
        xo_ref[...] += 0.5 * _dot(s_ref[...], wd_ref[...])

    tok = pl.BlockSpec((ts, d), lambda t, f: (t, 0))
    wspec = pl.BlockSpec((fc, d), lambda t, f: (f, 0))
    hid = pl.BlockSpec((ts, fc), lambda t, f: (t, f))
    return _call(
        body,
        name=name,
        grid=(s_len // ts, f_len // fc),
        in_specs=[tok, pl.BlockSpec((1, d), lambda t, f: (0, 0)), wspec, wspec, wspec],
        out_specs=[tok, tok, hid, hid],
        out_shape=[
            jax.ShapeDtypeStruct((s_len, d), F32),
            jax.ShapeDtypeStruct((s_len, d), LOW),
            jax.ShapeDtypeStruct((s_len, f_len), LOW),
            jax.ShapeDtypeStruct((s_len, f_len), LOW),
        ],
        scratch_shapes=[pltpu.VMEM((ts, fc), LOW)],
        args=(x, g, wgt, wut, wd),
        exchange=exchange,
    )


def _slabs(width, slab=256):
    return [(c0, min(c0 + slab, width)) for c0 in range(0, width, slab)]


def _ffn_dact(dxo, a, b, wd, name, exchange=None):
    s_len, d = dxo.shape
    f_len = wd.shape[0]
    ts = min(TOK_TILE, s_len)
    fc = _pick(f_len, FF_TILE_CANDIDATES)

    def body(dxo_ref, a_ref, b_ref, wd_ref, da_ref, db_ref, s_ref, df_ref):
        @pl.when(pl.program_id(1) == 0)
        def _():
            df_ref[...] = (0.5 * dxo_ref[...]).astype(LOW)

        dfb = df_ref[...]
        for c0, c1 in _slabs(fc):
            ds = _dot_nt(dfb, wd_ref[c0:c1, :])
            av = a_ref[:, c0:c1].astype(F32)
            bv = b_ref[:, c0:c1].astype(F32)
            sig = jax.nn.sigmoid(av)
            silu = av * sig
            s_ref[:, c0:c1] = (silu * bv).astype(LOW)
            da_ref[:, c0:c1] = (ds * bv * (sig * (1.0 + av * (1.0 - sig)))).astype(LOW)
            db_ref[:, c0:c1] = (ds * silu).astype(LOW)

    tok = pl.BlockSpec((ts, d), lambda t, f: (t, 0))
    hid = pl.BlockSpec((ts, fc), lambda t, f: (t, f))
    return _call(
        body,
        name=name,
        grid=(s_len // ts, f_len // fc),
        in_specs=[tok, hid, hid, pl.BlockSpec((fc, d), lambda t, f: (f, 0))],
        out_specs=[hid, hid, hid, tok],
        out_shape=[jax.ShapeDtypeStruct((s_len, f_len), LOW)] * 3 + [jax.ShapeDtypeStruct((s_len, d), LOW)],
        args=(dxo, a, b, wd),
        exchange=exchange,
    )


def _ffn_dx(dxo, x, g, da, db, wgt, wut, name, exchange=None):
    s_len, d = x.shape
    f_len = wgt.shape[0]
    ts = min(TOK_TILE, s_len)
    fc = _pick(f_len, FF_TILE_CANDIDATES)
    n_f = f_len // fc

    def body(dxo_ref, x_ref, g_ref, da_ref, db_ref, wg_ref, wu_ref, dxi_ref, dg_ref):
        t, f = pl.program_id(0), pl.program_id(1)

        @pl.when((t == 0) & (f == 0))
        def _():
            dg_ref[...] = jnp.zeros_like(dg_ref)

        @pl.when(f == 0)
        def _():
            dxi_ref[...] = jnp.zeros_like(dxi_ref)

        dxi_ref[...] += _dot(da_ref[...], wg_ref[...]) + _dot(db_ref[...], wu_ref[...])

        @pl.when(f == n_f - 1)
        def _():
            dx, dgp = _rms_bwd(dxi_ref[...], x_ref[...], g_ref[...])
            dxi_ref[...] = dxo_ref[...] + dx
            dg_ref[...] += dgp

    tok = pl.BlockSpec((ts, d), lambda t, f: (t, 0))
    vec = pl.BlockSpec((1, d), lambda t, f: (0, 0))
    wspec = pl.BlockSpec((fc, d), lambda t, f: (f, 0))
    hid = pl.BlockSpec((ts, fc), lambda t, f: (t, f))
    return _call(
        body,
        name=name,
        grid=(s_len // ts, n_f),
        in_specs=[tok, tok, vec, hid, hid, wspec, wspec],
        out_specs=[tok, vec],
        out_shape=[jax.ShapeDtypeStruct((s_len, d), F32), jax.ShapeDtypeStruct((1, d), F32)],
        args=(dxo, x, g, da, db, wgt, wut),
        exchange=exchange,
    )


def _ffn_wgrad(s, da, db, df, h, name, exchange=None):
    s_len, f_len = s.shape
    d = h.shape[1]
    tm = _pick(f_len, WGRAD_ROW_CANDIDATES)
    tk = min(WGRAD_TOK_TILE, s_len)
    n_k = s_len // tk

    def body(s_ref, da_ref, db_ref, df_ref, h_ref, dwd_ref, dwg_ref, dwu_ref, acc_d, acc_g, acc_u):
        k = pl.program_id(1)

        @pl.when(k == 0)
        def _():
            acc_d[...] = jnp.zeros_like(acc_d)
            acc_g[...] = jnp.zeros_like(acc_g)
            acc_u[...] = jnp.zeros_like(acc_u)

        hv = h_ref[...]
        acc_d[...] += _dot_tn(s_ref[...], df_ref[...])
        acc_g[...] += _dot_tn(da_ref[...], hv)
        acc_u[...] += _dot_tn(db_ref[...], hv)

        @pl.when(k == n_k - 1)
        def _():
            dwd_ref[...] = acc_d[...].astype(LOW)
            dwg_ref[...] = acc_g[...].astype(LOW)
            dwu_ref[...] = acc_u[...].astype(LOW)

    hid = pl.BlockSpec((tk, tm), lambda i, k: (k, i))
    tok = pl.BlockSpec((tk, d), lambda i, k: (k, 0))
    out = pl.BlockSpec((tm, d), lambda i, k: (i, 0))
    return _call(
        body,
        name=name,
        grid=(f_len // tm, n_k),
        in_specs=[hid, hid, hid, tok, tok],
        out_specs=[out, out, out],
        out_shape=[jax.ShapeDtypeStruct((f_len, d), LOW)] * 3,
        scratch_shapes=[pltpu.VMEM((tm, d), F32)] * 3,
        args=(s, da, db, df, h),
        exchange=exchange,
    )


def _wgrad_tn(xm, ym, tn, stacked, name):
    s_len, m = xm.shape
    n = ym.shape[1]
    tk = min(WGRAD_TOK_TILE, s_len)
    n_k = s_len // tk

    def body(x_ref, y_ref, o_ref, acc):
        k = pl.program_id(1)

        @pl.when(k == 0)
        def _():
            acc[...] = jnp.zeros_like(acc)

        acc[...] += _dot_tn(x_ref[...].astype(LOW), y_ref[...].astype(LOW))

        @pl.when(k == n_k - 1)
        def _():
            o_ref[...] = acc[...].astype(LOW)

    if stacked:
        out_spec = pl.BlockSpec((None, m, tn), lambda j, k: (j, 0, 0))
        out_shape = jax.ShapeDtypeStruct((n // tn, m, tn), LOW)
    else:
        out_spec = pl.BlockSpec((m, tn), lambda j, k: (0, j))
        out_shape = jax.ShapeDtypeStruct((m, n), LOW)
    return pl.pallas_call(
        body,
        name=name,
        grid=(n // tn, n_k),
        in_specs=[pl.BlockSpec((tk, m), lambda j, k: (k, 0)), pl.BlockSpec((tk, tn), lambda j, k: (k, j))],
        out_specs=out_spec,
        out_shape=out_shape,
        scratch_shapes=[pltpu.VMEM((m, tn), F32)],
        compiler_params=_params("arbitrary", "arbitrary"),
    )(xm, ym)


def _mix_parts(ext_ref, cw, ts, row0):
    dc = D_CONV

    def back(off, c0, c1):
        return ext_ref[HALO - off:HALO - off + ts, c0:c1]

    v, gate_b, gate_c = back(0, 0, dc), back(0, dc, 2 * dc), back(0, 2 * dc, 3 * dc)
    z0 = gate_c * v
    z1 = back(1, 2 * dc, 3 * dc) * back(1, 0, dc)
    z2 = back(2, 2 * dc, 3 * dc) * back(2, 0, dc)
    conv = cw[2:3, :] * z0 + cw[1:2, :] * z1 + cw[0:1, :] * z2
    rows = row0 + lax.broadcasted_iota(jnp.int32, (ts, 1), 0)
    pooled, inv_count = [], []
    for grp, w in enumerate(POOL_WINDOWS):
        c0 = 3 * dc + POOL_GC * grp
        u = back(0, c0, c0 + POOL_GC)
        acc = u
        for j in range(1, w):
            acc = acc + back(j, c0, c0 + POOL_GC)
        inv = 1.0 / jnp.minimum(rows + 1, w).astype(F32)
        pooled.append(acc * inv - u)
        inv_count.append(inv)
    return v, gate_b, gate_c, z0, z1, z2, conv, pooled, inv_count


def _mix_fwd(x, g, w_in, conv_w, pool_w, pool_scale, w_out, name, exchange=None):
    s_len, d = x.shape
    n_blk, _, wcols = w_in.shape
    p_len = n_blk * wcols
    d_mix = w_out.shape[0]
    ts = min(MIX_TOK_TILE, s_len)
    dc = D_CONV

    def body(x_ref, g_ref, win_ref, cw_ref, pw_ref, ps_ref, wout_ref, x2_ref, h_ref, proj_ref, ext_ref, cat_ref):
        t = pl.program_id(0)

        @pl.when(t == 0)
        def _():
            ext_ref[0:HALO, :] = jnp.zeros((HALO, p_len), F32)

        xv = x_ref[...]
        hb = (xv * _rms_scale(xv) * g_ref[...]).astype(LOW)
        h_ref[...] = hb
        for k in range(n_blk):
            ext_ref[HALO:HALO + ts, wcols * k:wcols * (k + 1)] = _dot(hb, win_ref[k])
        proj_ref[...] = ext_ref[HALO:HALO + ts, :]

        _, gate_b, _, _, _, _, conv, pooled, _ = _mix_parts(ext_ref, cw_ref[...], ts, t * ts)
        cat_ref[:, 0:dc] = (gate_b * conv).astype(LOW)
        for grp in range(len(POOL_WINDOWS)):
            c0 = POOL_GC * grp
            lin = _dot(pooled[grp].astype(LOW), pw_ref[grp])
            cat_ref[:, dc + c0:dc + c0 + POOL_GC] = (lin * ps_ref[:, c0:c0 + POOL_GC]).astype(LOW)
        x2_ref[...] = xv + _dot(cat_ref[...], wout_ref[...])
        ext_ref[0:HALO, :] = ext_ref[ts:ts + HALO, :]

    tok = pl.BlockSpec((ts, d), lambda t: (t, 0))

    def whole(arr):
        return pl.BlockSpec(arr.shape, lambda t: (0,) * arr.ndim)

    return _call(
        body,
        name=name,
        grid=(s_len // ts,),
        in_specs=[tok, whole(g), _resident(w_in), whole(conv_w), whole(pool_w), whole(pool_scale), _resident(w_out)],
        out_specs=[tok, tok, pl.BlockSpec((ts, p_len), lambda t: (t, 0))],
        out_shape=[
            jax.ShapeDtypeStruct((s_len, d), F32),
            jax.ShapeDtypeStruct((s_len, d), LOW),
            jax.ShapeDtypeStruct((s_len, p_len), F32),
        ],
        scratch_shapes=[pltpu.VMEM((ts + HALO, p_len), F32), pltpu.VMEM((ts, d_mix), LOW)],
        args=(x, g, w_in, conv_w, pool_w, pool_scale, w_out),
        exchange=exchange,
    )


def _mix_bwd(dx2, x, proj, g, w_in, conv_w, pool_w, pool_scale, w_out, name):
    s_len, d = x.shape
    n_blk, _, wcols = w_in.shape
    p_len = n_blk * wcols
    d_mix = w_out.shape[0]
    ts = min(MIX_TOK_TILE, s_len)
    n_t = s_len // ts
    dc = D_CONV
    n_grp = len(POOL_WINDOWS)

    def body(dx2_ref, x_ref, proj_ref, halo_ref, g_ref, win_ref, cw_ref, pw_ref, ps_ref, wout_ref,
             dx_ref, dproj_ref, cat_ref, dg_ref, dcw_ref, dpw_ref, dps_ref, ext_ref, fut_ref):
        i = pl.program_id(0)
        t = n_t - 1 - i

        @pl.when(i == 0)
        def _():
            dg_ref[...] = jnp.zeros_like(dg_ref)
            dcw_ref[...] = jnp.zeros_like(dcw_ref)
            dpw_ref[...] = jnp.zeros_like(dpw_ref)
            dps_ref[...] = jnp.zeros_like(dps_ref)
            fut_ref[ts:ts + HALO, :] = jnp.zeros((HALO, d_mix), F32)

        ext_ref[HALO:HALO + ts, :] = proj_ref[...]

        @pl.when(t == 0)
        def _():
            ext_ref[0:HALO, :] = jnp.zeros((HALO, p_len), F32)

        @pl.when(t > 0)
        def _():
            ext_ref[0:HALO, :] = halo_ref[...]

        cw = cw_ref[...]
        v, gate_b, gate_c, z0, z1, z2, conv, pooled, inv_count = _mix_parts(ext_ref, cw, ts, t * ts)
        dx2 = dx2_ref[...]
        dcat = _dot_nt(dx2.astype(LOW), wout_ref[...])

        dy_a = dcat[:, 0:dc]
        dconv = dy_a * gate_b
        fut_ref[0:ts, 0:dc] = dconv
        cat_ref[:, 0:dc] = (gate_b * conv).astype(LOW)
        dproj_ref[:, dc:2 * dc] = (dy_a * conv).astype(LOW)
        dcw_ref[2:3, :] += jnp.sum(dconv * z0, axis=0, keepdims=True)
        dcw_ref[1:2, :] += jnp.sum(dconv * z1, axis=0, keepdims=True)
        dcw_ref[0:1, :] += jnp.sum(dconv * z2, axis=0, keepdims=True)

        dpool = []
        for grp in range(n_grp):
            c0 = POOL_GC * grp
            pooled_b = pooled[grp].astype(LOW)
            lin = _dot(pooled_b, pw_ref[grp])
            dy_b = dcat[:, dc + c0:dc + c0 + POOL_GC]
            scale = ps_ref[:, c0:c0 + POOL_GC]
            cat_ref[:, dc + c0:dc + c0 + POOL_GC] = (lin * scale).astype(LOW)
            dps_ref[:, c0:c0 + POOL_GC] += jnp.sum(dy_b * lin, axis=0, keepdims=True)
            dlin = (dy_b * scale).astype(LOW)
            dpw_ref[grp] += _dot_tn(pooled_b, dlin)
            dpool.append(_dot_nt(dlin, pw_ref[grp]))
            fut_ref[0:ts, dc + c0:dc + c0 + POOL_GC] = dpool[grp] * inv_count[grp]

        def ahead(off, c0, c1):
            return fut_ref[off:off + ts, c0:c1]

        dz = cw[2:3, :] * ahead(0, 0, dc) + cw[1:2, :] * ahead(1, 0, dc) + cw[0:1, :] * ahead(2, 0, dc)
        dproj_ref[:, 0:dc] = (dz * gate_c).astype(LOW)
        dproj_ref[:, 2 * dc:3 * dc] = (dz * v).astype(LOW)
        for grp, w in enumerate(POOL_WINDOWS):
            c0 = dc + POOL_GC * grp
            acc = ahead(0, c0, c0 + POOL_GC)
            for j in range(1, w):
                acc = acc + ahead(j, c0, c0 + POOL_GC)
            dproj_ref[:, 2 * dc + c0:2 * dc + c0 + POOL_GC] = (acc - dpool[grp]).astype(LOW)

        dh = _dot_nt(dproj_ref[:, 0:wcols], win_ref[0])
        for k in range(1, n_blk):
            dh += _dot_nt(dproj_ref[:, wcols * k:wcols * (k + 1)], win_ref[k])
        dx, dgp = _rms_bwd(dh, x_ref[...], g_ref[...])
        dx_ref[...] = dx2 + dx
        dg_ref[...] += dgp
        fut_ref[ts:ts + HALO, :] = fut_ref[0:HALO, :]

    tok = pl.BlockSpec((ts, d), lambda i: (n_t - 1 - i, 0))
    halo = pl.BlockSpec((HALO, p_len), lambda i: (jnp.maximum((n_t - 1 - i) * (ts // HALO) - 1, 0), 0))

    def whole(arr):
        return pl.BlockSpec(arr.shape, lambda i: (0,) * arr.ndim)

    return pl.pallas_call(
        body,
        name=name,
        grid=(n_t,),
        in_specs=[tok, tok, pl.BlockSpec((ts, p_len), lambda i: (n_t - 1 - i, 0)), halo,
                  whole(g), _resident(w_in), whole(conv_w), whole(pool_w), whole(pool_scale), _resident(w_out)],
        out_specs=[tok, pl.BlockSpec((ts, p_len), lambda i: (n_t - 1 - i, 0)),
                   pl.BlockSpec((ts, d_mix), lambda i: (n_t - 1 - i, 0)),
                   whole(g), whole(conv_w), whole(pool_w), whole(pool_scale)],
        out_shape=[
            jax.ShapeDtypeStruct((s_len, d), F32),
            jax.ShapeDtypeStruct((s_len, p_len), LOW),
            jax.ShapeDtypeStruct((s_len, d_mix), LOW),
            jax.ShapeDtypeStruct(g.shape, F32),
            jax.ShapeDtypeStruct(conv_w.shape, F32),
            jax.ShapeDtypeStruct(pool_w.shape, F32),
            jax.ShapeDtypeStruct(pool_scale.shape, F32),
        ],
        scratch_shapes=[pltpu.VMEM((ts + HALO, p_len), F32), pltpu.VMEM((ts + HALO, d_mix), F32)],
        compiler_params=_params("arbitrary"),
    )(dx2, x, proj, proj, g, w_in, conv_w, pool_w, pool_scale, w_out)


def _loss_head(x, g, target, name):
    s_len, d = x.shape
    ts = min(TOK_TILE, s_len)

    def body(x_ref, g_ref, tgt_ref, loss_ref, dx_ref, dg_ref):
        @pl.when(pl.program_id(0) == 0)
        def _():
            loss_ref[...] = jnp.zeros_like(loss_ref)
            dg_ref[...] = jnp.zeros_like(dg_ref)

        xv, gv = x_ref[...], g_ref[...]
        err = xv * _rms_scale(xv) * gv - tgt_ref[...]
        loss_ref[...] += 0.5 * jnp.sum(jnp.mean(err * err, axis=-1, keepdims=True), axis=0, keepdims=True)
        dx, dgp = _rms_bwd(err * (1.0 / d), xv, gv)
        dx_ref[...] = dx
        dg_ref[...] += dgp

    tok = pl.BlockSpec((ts, d), lambda t: (t, 0))
    vec = pl.BlockSpec((1, d), lambda t: (0, 0))
    return pl.pallas_call(
        body,
        name=name,
        grid=(s_len // ts,),
        in_specs=[tok, vec, tok],
        out_specs=[pl.BlockSpec((1, 128), lambda t: (0, 0)), tok, vec],
        out_shape=[
            jax.ShapeDtypeStruct((1, 128), F32),
            jax.ShapeDtypeStruct((s_len, d), F32),
            jax.ShapeDtypeStruct((1, d), F32),
        ],
        compiler_params=_params("arbitrary"),
    )(x, g, target)


def _row_tile(rows, cols, stack_bytes):
    budget = 20 * 1024 * 1024
    per_row = cols * (4 * 7 + stack_bytes)
    for tr in (rows, 512, 256, 176, 128, 64, 32, 16, 8):
        if rows % tr == 0 and tr % 8 == 0 and tr * per_row * 2 <= budget:
            return tr
    return rows


def _sum_stack(stack, name):
    n, r, c = stack.shape
    tr = _row_tile(r, c, n * stack.dtype.itemsize)

    def body(s_ref, o_ref):
        acc = s_ref[0].astype(F32)
        for k in range(1, n):
            acc = acc + s_ref[k].astype(F32)
        o_ref[...] = acc

    return pl.pallas_call(
        body,
        name=name,
        grid=(r // tr,),
        in_specs=[pl.BlockSpec((n, tr, c), lambda i: (0, i, 0))],
        out_specs=pl.BlockSpec((tr, c), lambda i: (i, 0)),
        out_shape=jax.ShapeDtypeStruct((r, c), F32),
        compiler_params=_params("arbitrary"),
    )(stack)


def _adamw(stack, w, m, v, name):
    n, r, c = stack.shape
    tr = _row_tile(r, c, n * stack.dtype.itemsize)
    c1 = 1.0 - ADAM_B1 ** ADAM_STEP
    c2 = 1.0 - ADAM_B2 ** ADAM_STEP

    def body(s_ref, w_ref, m_ref, v_ref, g_ref, d_ref, mo_ref, vo_ref):
        gv = s_ref[0].astype(F32)
        for k in range(1, n):
            gv = gv + s_ref[k].astype(F32)
        mn = ADAM_B1 * m_ref[...] + (1.0 - ADAM_B1) * gv
        vn = ADAM_B2 * v_ref[...] + (1.0 - ADAM_B2) * (gv * gv)
        g_ref[...] = gv
        mo_ref[...] = mn
        vo_ref[...] = vn
        d_ref[...] = -ADAM_LR * ((mn / c1) / (jnp.sqrt(vn / c2) + ADAM_EPS) + ADAM_WD * w_ref[...])

    blk = pl.BlockSpec((tr, c), lambda i: (i, 0))
    return pl.pallas_call(
        body,
        name=name,
        grid=(r // tr,),
        in_specs=[pl.BlockSpec((n, tr, c), lambda i: (0, i, 0)), blk, blk, blk],
        out_specs=[blk] * 4,
        out_shape=[jax.ShapeDtypeStruct((r, c), F32)] * 4,
        compiler_params=_params("arbitrary"),
    )(stack, w, m, v)


def _to_sheet(parts):
    sheets, spans = [], []
    row = 0
    for p in parts:
        flat = p.reshape(-1).astype(F32)
        rows = -(-flat.shape[0] // 1024) * 8
        flat = jnp.pad(flat, (0, rows * 128 - flat.shape[0]))
        sheets.append(flat.reshape(rows, 128))
        spans.append((row, p.size, p.shape))
        row += rows
    return jnp.concatenate(sheets, axis=0), spans


def _from_sheet(sheet, spans):
    out = []
    for row, size, shape in spans:
        rows = -(-size // 1024) * 8
        out.append(sheet[row:row + rows].reshape(-1)[:size].reshape(shape))
    return out


def kernel(x, norm_ffn1, ffn1_w_gate, ffn1_w_up, ffn1_w_down, norm_mix, w_in, conv_w, pool_w, pool_scale, w_out, norm_ffn2, ffn2_w_gate, ffn2_w_up, ffn2_w_down, norm_final, loss_target, m_norm_ffn1, m_ffn1_w_gate, m_ffn1_w_up, m_ffn1_w_down, m_norm_mix, m_w_in, m_conv_w, m_pool_w, m_pool_scale, m_w_out, m_norm_ffn2, m_ffn2_w_gate, m_ffn2_w_up, m_ffn2_w_down, m_norm_final, v_norm_ffn1, v_ffn1_w_gate, v_ffn1_w_up, v_ffn1_w_down, v_norm_mix, v_w_in, v_conv_w, v_pool_w, v_pool_scale, v_w_out, v_norm_ffn2, v_ffn2_w_gate, v_ffn2_w_up, v_ffn2_w_down, v_norm_final):
    me = 4 * lax.axis_index("x") + 2 * lax.axis_index("y") + lax.axis_index("c")
    xs, tgt = x[0], loss_target[0]
    s_len, d = xs.shape
    f_shard = ffn1_w_down.shape[1]
    f_len = N_DEV * f_shard
    conv_shard = conv_w.shape[2]

    def low_t(wt):
        return wt[0].T.astype(LOW)

    def by_dev(gw):
        return gw.reshape(N_DEV, -1, d)

    conv_tile = jnp.zeros((8, 128), F32).at[0:conv_w.shape[1], 0:conv_shard].set(conv_w[0])
    full = _exchange([low_t(ffn1_w_gate), low_t(ffn1_w_up), ffn1_w_down[0].astype(LOW)], [False] * 3, "gather_ffn1",
                     relay=True)
    wg1, wu1, wd1 = (w.reshape(f_len, d) for w in full)
    pool_w_low = pool_w[0].astype(LOW)

    later = _Exchange([w_in[0].astype(LOW), w_out[0].astype(LOW), conv_tile, ffn2_w_down[0].astype(LOW),
                       low_t(ffn2_w_up)], [False] * 5, relay_at=0.8)
    x1, h1, a1, b1, w_in_full, w_out_full, conv_tiles, wd2, wu2 = _ffn_fwd(
        xs, norm_ffn1, wg1, wu1, wd1, "ffn1_fwd", exchange=later)
    w_out_full, wd2, wu2 = w_out_full.reshape(-1, d), wd2.reshape(f_len, d), wu2.reshape(f_len, d)
    conv_full = jnp.concatenate([conv_tiles[k, 0:conv_w.shape[1], 0:conv_shard] for k in range(N_DEV)], axis=1)
    later = _Exchange([low_t(ffn2_w_gate)], [False], relay_at=0.7)
    x2, h2, proj, wg2 = _mix_fwd(
        x1, norm_mix, w_in_full, conv_full, pool_w_low, pool_scale, w_out_full, "mix_fwd", exchange=later)
    wg2 = wg2.reshape(f_len, d)
    x3, h3, a2, b2 = _ffn_fwd(x2, norm_ffn2, wg2, wu2, wd2, "ffn2_fwd")
    loss_row, dx3, dg_final = _loss_head(x3, norm_final.reshape(1, d), tgt, "loss_head")

    da, db, sv, df = _ffn_dact(dx3, a2, b2, wd2, "ffn2_dact")
    dx2, dg_ffn2 = _ffn_dx(dx3, x2, norm_ffn2, da, db, wg2, wu2, "ffn2_dx")
    dwd2, dwg2, dwu2 = _ffn_wgrad(sv, da, db, df, h3, "ffn2_wgrad")
    pair2 = _pair_sum([by_dev(dwg2), by_dev(dwu2), by_dev(dwd2)], "pair_sum_ffn2")
    dx1, dproj, cat, dg_mix, dconv, dpool_w, dpool_scale = _mix_bwd(
        dx2, x1, proj, norm_mix, w_in_full, conv_full, pool_w_low, pool_scale, w_out_full, "mix_bwd")
    dw_in = _wgrad_tn(h2, dproj, W_IN_SHARD, True, "w_in_wgrad")
    dw_out = _wgrad_tn(cat, dx2, d, False, "w_out_wgrad")
    small_parts = [dg_mix, dg_ffn2, dg_final, dconv, dpool_w, dpool_scale]
    small_sheet, spans = _to_sheet(small_parts)
    earlier = _Exchange([dw_in, by_dev(dw_out), small_sheet], [True, True, False])
    da, db, sv, df, got_in, got_out, got_small = _ffn_dact(dx1, a1, b1, wd1, "ffn1_dact", exchange=earlier)
    earlier = _Exchange(pair2, [True] * 3, among_chips=True)
    dwd1, dwg1, dwu1, got_g2, got_u2, got_d2 = _ffn_wgrad(sv, da, db, df, h1, "ffn1_wgrad", exchange=earlier)
    pair1 = _pair_sum([by_dev(dwg1), by_dev(dwu1), by_dev(dwd1)], "pair_sum_ffn1")
    earlier = _Exchange(pair1, [True] * 3, among_chips=True)
    dx0, dg_ffn1, got_g1, got_u1, got_d1 = _ffn_dx(
        dx1, xs, norm_ffn1, da, db, wg1, wu1, "ffn1_dx", exchange=earlier)
    (got_n1,) = _exchange([dg_ffn1.reshape(8, 128)], [False], "gather_dnorm_ffn1")
    got = [got_g1, got_u1, got_d1, got_in, got_out, got_g2, got_u2, got_d2, got_small]

    outs = {}

    def update(name, stack, w, m, v):
        outs[name] = _adamw(stack, w[0], m[0], v[0], "adamw_" + name)

    def update_t(name, stack, w, m, v):
        res = _adamw(stack, w[0].T, m[0].T, v[0].T, "adamw_" + name)
        outs[name] = tuple(r.T for r in res)

    update_t("ffn1_w_gate", got[0], ffn1_w_gate, m_ffn1_w_gate, v_ffn1_w_gate)
    update_t("ffn1_w_up", got[1], ffn1_w_up, m_ffn1_w_up, v_ffn1_w_up)
    update("ffn1_w_down", got[2], ffn1_w_down, m_ffn1_w_down, v_ffn1_w_down)
    update("w_in", got[3], w_in, m_w_in, v_w_in)
    update("w_out", got[4], w_out, m_w_out, v_w_out)
    update_t("ffn2_w_gate", got[5], ffn2_w_gate, m_ffn2_w_gate, v_ffn2_w_gate)
    update_t("ffn2_w_up", got[6], ffn2_w_up, m_ffn2_w_up, v_ffn2_w_up)
    update("ffn2_w_down", got[7], ffn2_w_down, m_ffn2_w_down, v_ffn2_w_down)

    g_small = _from_sheet(_sum_stack(got[8], "sum_small"), spans)
    g_norm_ffn1 = _sum_stack(got_n1, "sum_dnorm_ffn1").reshape(norm_ffn1.shape)
    g_conv = lax.dynamic_slice_in_dim(g_small[3], me * conv_shard, conv_shard, axis=1)
    small_names = ["norm_ffn1", "norm_mix", "norm_ffn2", "norm_final", "conv_w", "pool_w", "pool_scale"]
    small_g = [g_norm_ffn1, g_small[0], g_small[1], g_small[2].reshape(norm_final.shape), g_conv[None],
               g_small[4][None], g_small[5]]
    small_w = [norm_ffn1, norm_mix, norm_ffn2, norm_final, conv_w, pool_w, pool_scale]
    small_m = [m_norm_ffn1, m_norm_mix, m_norm_ffn2, m_norm_final, m_conv_w, m_pool_w, m_pool_scale]
    small_v = [v_norm_ffn1, v_norm_mix, v_norm_ffn2, v_norm_final, v_conv_w, v_pool_w, v_pool_scale]
    g_sheet, spans_u = _to_sheet(small_g)
    w_sheet, _ = _to_sheet(small_w)
    m_sheet, _ = _to_sheet(small_m)
    v_sheet, _ = _to_sheet(small_v)
    upd = _adamw(g_sheet[None], w_sheet, m_sheet, v_sheet, "adamw_small")
    small_out = [_from_sheet(u, spans_u) for u in upd]
    for k, nm in enumerate(small_names):
        outs[nm] = tuple(small_out[j][k] for j in range(4))

    loss = lax.psum(loss_row[0, 0], ("x", "y", "c"))
    order = ["norm_ffn1", "ffn1_w_gate", "ffn1_w_up", "ffn1_w_down", "norm_mix", "w_in", "conv_w", "pool_w",
             "pool_scale", "w_out", "norm_ffn2", "ffn2_w_gate", "ffn2_w_up", "ffn2_w_down", "norm_final"]
    big = {"ffn1_w_gate", "ffn1_w_up", "ffn1_w_down", "w_in", "w_out", "ffn2_w_gate", "ffn2_w_up", "ffn2_w_down"}

    def leaf(nm, j):
        val = outs[nm][j]
        return val[None] if nm in big else val

    return (loss, dx0[None],
            *[leaf(nm, 0) for nm in order], *[leaf(nm, 1) for nm in order],
            *[leaf(nm, 2) for nm in order], *[leaf(nm, 3) for nm in order])
```

```python
import jax
import jax.numpy as jnp
from jax import lax
from jax.experimental import pallas as pl
from jax.experimental.pallas import tpu as pltpu

F32 = jnp.float32
LOW = jnp.bfloat16

N_DEV = 8
EPS = 1e-6
D_CONV = 512
POOL_WINDOWS = (2, 4, 8, 16)
POOL_GC = 128
HALO = 16
W_IN_SHARD = 256

ADAM_LR = 0.001
ADAM_B1 = 0.9
ADAM_B2 = 0.999
ADAM_EPS = 1e-08
ADAM_WD = 0.01
ADAM_STEP = 10

VMEM_LIMIT_BYTES = 56 * 1024 * 1024
TOK_TILE = 512
MIX_TOK_TILE = 256
WGRAD_TOK_TILE = 2048
FF_TILE_CANDIDATES = (1408, 256, 128)
WGRAD_ROW_CANDIDATES = (256, 128)
BWD_ROW_SLAB = 512


def _params(*sem):
    return pltpu.CompilerParams(dimension_semantics=sem, vmem_limit_bytes=VMEM_LIMIT_BYTES)


def _resident(arr):
    return pl.BlockSpec(arr.shape, lambda *_: (0,) * arr.ndim, pipeline_mode=pl.Buffered(1))


def _pick(n, candidates):
    for c in candidates:
        if n % c == 0:
            return c
    raise ValueError(f"no tile in {candidates} divides {n}")


def _dot(a, b):
    return lax.dot_general(a, b, (((1,), (0,)), ((), ())), preferred_element_type=F32)


def _dot_nt(a, b):
    return lax.dot_general(a, b, (((1,), (1,)), ((), ())), preferred_element_type=F32)


def _dot_tn(a, b):
    return lax.dot_general(a, b, (((0,), (0,)), ((), ())), preferred_element_type=F32)


def _rms_scale(x):
    return lax.rsqrt(jnp.mean(x * x, axis=-1, keepdims=True) + EPS)


def _rms_bwd(dy, x, g):
    r = _rms_scale(x)
    xhat = x * r
    gdy = dy * g
    dx = r * (gdy - xhat * jnp.mean(gdy * xhat, axis=-1, keepdims=True))
    return dx, jnp.sum(dy * xhat, axis=0, keepdims=True)


class _Exchange:
    CHIPS = (2, 4, 6)

    def __init__(self, arrays, sliced, relay_at=None, among_chips=False):
        assert relay_at is None or not any(sliced)
        assert not among_chips or (all(sliced) and relay_at is None)
        self.relay_at, self.among_chips = relay_at, among_chips
        self.peers = self.CHIPS if among_chips else ((1,) + self.CHIPS if relay_at is not None else tuple(range(1, N_DEV)))
        self.arrays, self.sliced, self.n = list(arrays), list(sliced), len(arrays)
        self.out_shape = [jax.ShapeDtypeStruct(arr.shape if sl else (N_DEV,) + arr.shape, arr.dtype)
                          for arr, sl in zip(arrays, sliced)]
        self.specs = [pl.BlockSpec(memory_space=pl.ANY)] * self.n
        self.scratch_shapes = [pltpu.SemaphoreType.DMA((self.n, N_DEV - 1)),
                               pltpu.SemaphoreType.DMA((self.n, N_DEV - 1)),
                               pltpu.SemaphoreType.DMA((self.n,))]

    def _copies(self, ins, outs, sems):
        send_sems, recv_sems, local_sems = sems
        sliced = self.sliced
        mx, my, mc = lax.axis_index("x"), lax.axis_index("y"), lax.axis_index("c")
        me = 2 * mx + my if self.among_chips else 4 * mx + 2 * my + mc

        def peer(m):
            px = lax.rem(mx + ((m >> 2) & 1), 2)
            py = lax.rem(my + ((m >> 1) & 1), 2)
            pc = lax.rem(mc + (m & 1), 2)
            return (px, py, pc), (2 * px + py if self.among_chips else 4 * px + 2 * py + pc)

        def remote(a, m, arriving):
            pid, pflat = peer(m)
            return pltpu.make_async_remote_copy(
                src_ref=ins[a].at[pflat] if sliced[a] else ins[a],
                dst_ref=outs[a].at[pflat if arriving else me],
                send_sem=send_sems.at[a, m - 1],
                recv_sem=recv_sems.at[a, m - 1],
                device_id=pid,
                device_id_type=pl.DeviceIdType.MESH,
            )

        def local(a):
            return pltpu.make_async_copy(ins[a].at[me] if sliced[a] else ins[a], outs[a].at[me], local_sems.at[a])

        def passed_on(a, m):
            _, origin = peer(m)
            sibling, _ = peer(1)
            return pltpu.make_async_remote_copy(
                src_ref=outs[a].at[origin],
                dst_ref=outs[a].at[origin],
                send_sem=send_sems.at[a, m],
                recv_sem=recv_sems.at[a, m],
                device_id=sibling,
                device_id_type=pl.DeviceIdType.MESH,
            )

        return remote, local, passed_on

    def start(self, ins, outs, sems):
        remote, local, _ = self._copies(ins, outs, sems)
        for a in range(self.n):
            local(a).start()
        for m in self.peers:
            for a in range(self.n):
                remote(a, m, False).start()

    def relay(self, ins, outs, sems):
        remote, _, passed_on = self._copies(ins, outs, sems)
        for m in self.CHIPS:
            for a in range(self.n):
                remote(a, m, True).wait_recv()
                passed_on(a, m).start()

    def wait(self, ins, outs, sems):
        remote, local, passed_on = self._copies(ins, outs, sems)
        if self.relay_at is None:
            for m in self.peers:
                for a in range(self.n):
                    remote(a, m, True).wait_recv()
            for m in self.peers:
                for a in range(self.n):
                    remote(a, m, False).wait_send()
        else:
            for m in (1,) + tuple(c + 1 for c in self.CHIPS):
                for a in range(self.n):
                    remote(a, m, True).wait_recv()
            for m in self.peers:
                for a in range(self.n):
                    remote(a, m, False).wait_send()
            for m in self.CHIPS:
                for a in range(self.n):
                    passed_on(a, m).wait_send()
        for a in range(self.n):
            local(a).wait()


def _pair_sum(stacks, name):
    n = len(stacks)
    n_chip = N_DEV // 2
    half = [(n_chip,) + st.shape[1:] for st in stacks]

    def body(*refs):
        ins, outs, mine, theirs = refs[:n], refs[n:2 * n], refs[2 * n:3 * n], refs[3 * n:4 * n]
        local_sems, send_sems, recv_sems = refs[4 * n:]
        mx, my, mc = lax.axis_index("x"), lax.axis_index("y"), lax.axis_index("c")

        def own(a, k):
            return pltpu.make_async_copy(ins[a].at[2 * k + mc], mine[a].at[k], local_sems.at[a, k])

        def swap(a, k):
            return pltpu.make_async_remote_copy(
                src_ref=ins[a].at[2 * k + (1 - mc)], dst_ref=theirs[a].at[k],
                send_sem=send_sems.at[a, k], recv_sem=recv_sems.at[a, k],
                device_id=(mx, my, 1 - mc), device_id_type=pl.DeviceIdType.MESH)

        for k in range(n_chip):
            for a in range(n):
                own(a, k).start()
                swap(a, k).start()
        for k in range(n_chip):
            for a in range(n):
                own(a, k).wait()
                swap(a, k).wait()
                outs[a][k] = (mine[a][k].astype(F32) + theirs[a][k].astype(F32)).astype(LOW)

    return pl.pallas_call(
        body, name=name,
        out_shape=[jax.ShapeDtypeStruct(h, LOW) for h in half],
        in_specs=[pl.BlockSpec(memory_space=pl.ANY)] * n,
        out_specs=[pl.BlockSpec(memory_space=pltpu.VMEM)] * n,
        scratch_shapes=([pltpu.VMEM(h, st.dtype) for h, st in zip(half, stacks)] * 2
                        + [pltpu.SemaphoreType.DMA((n, n_chip))] * 3),
        compiler_params=pltpu.CompilerParams(vmem_limit_bytes=VMEM_LIMIT_BYTES),
    )(*stacks)


def _exchange(arrays, sliced, name, relay=False):
    ex = _Exchange(arrays, sliced, relay_at=0 if relay else None)

    def body(*refs):
        ins, outs, sems = refs[:ex.n], refs[ex.n:2 * ex.n], refs[2 * ex.n:]
        ex.start(ins, outs, sems)
        if relay:
            ex.relay(ins, outs, sems)
        ex.wait(ins, outs, sems)

    return pl.pallas_call(body, name=name, out_shape=ex.out_shape, in_specs=ex.specs, out_specs=ex.specs,
                          scratch_shapes=ex.scratch_shapes)(*arrays)


def _call(body, *, name, grid, in_specs, out_specs, out_shape, args, scratch_shapes=(), exchange=None):
    params = _params(*(("arbitrary",) * len(grid)))
    if exchange is None:
        return pl.pallas_call(body, name=name, grid=grid, in_specs=in_specs, out_specs=out_specs, out_shape=out_shape,
                              scratch_shapes=list(scratch_shapes), compiler_params=params)(*args)
    exs = list(exchange) if isinstance(exchange, (list, tuple)) else [exchange]
    n_in, n_out, n_scr = len(in_specs), len(out_specs), len(scratch_shapes)
    n_ex = sum(ex.n for ex in exs)
    n_steps = 1
    for g in grid:
        n_steps *= g

    def hosted(*refs):
        ins, refs = refs[:n_in], refs[n_in:]
        ex_ins, refs = refs[:n_ex], refs[n_ex:]
        outs, refs = refs[:n_out], refs[n_out:]
        ex_outs, refs = refs[:n_ex], refs[n_ex:]
        scr, sems = refs[:n_scr], refs[n_scr:]
        parts, at = [], 0
        for j, ex in enumerate(exs):
            parts.append((ex_ins[at:at + ex.n], ex_outs[at:at + ex.n], sems[3 * j:3 * j + 3]))
            at += ex.n
        step = pl.program_id(0)
        for ax in range(1, len(grid)):
            step = step * grid[ax] + pl.program_id(ax)

        @pl.when(step == 0)
        def _():
            for ex, part in zip(exs, parts):
                ex.start(*part)

        body(*ins, *outs, *scr)

        for ex, part in zip(exs, parts):
            if ex.relay_at is not None:
                @pl.when(step == min(int(ex.relay_at * n_steps), n_steps - 1))
                def _():
                    ex.relay(*part)

        @pl.when(step == n_steps - 1)
        def _():
            for ex, part in zip(exs, parts):
                ex.wait(*part)

    return pl.pallas_call(
        hosted, name=name, grid=grid,
        in_specs=list(in_specs) + [sp for ex in exs for sp in ex.specs],
        out_specs=list(out_specs) + [sp for ex in exs for sp in ex.specs],
        out_shape=list(out_shape) + [sh for ex in exs for sh in ex.out_shape],
        scratch_shapes=list(scratch_shapes) + [sc for ex in exs for sc in ex.scratch_shapes],
        compiler_params=params)(*args, *[arr for ex in exs for arr in ex.arrays])


def _ffn_fwd(x, g, wgt, wut, wd, name, exchange=None):
    s_len, d = x.shape
    f_len = wd.shape[0]
    ts = min(TOK_TILE, s_len)
    fc = _pick(f_len, FF_TILE_CANDIDATES)

    def body(x_ref, g_ref, wg_ref, wu_ref, wd_ref, xo_ref, h_ref, a_ref, b_ref, s_ref):
        @pl.when(pl.program_id(1) == 0)
        def _():
            xv = x_ref[...]
            h_ref[...] = (xv * _rms_scale(xv) * g_ref[...]).astype(LOW)
            xo_ref[...] = xv

        hb = h_ref[...]
        for c0, c1 in _slabs(fc):
            a = _dot_nt(hb, wg_ref[c0:c1, :])
            b = _dot_nt(hb, wu_ref[c0:c1, :])
            s_ref[:, c0:c1] = (a * jax.nn.sigmoid(a) * b).astype(LOW)
            a_ref[:, c0:c1] = a.astype(LOW)
            b_ref[:, c0:c1] = b.astype(LOW)
        xo_ref[...] += 0.5 * _dot(s_ref[...], wd_ref[...])

    tok = pl.BlockSpec((ts, d), lambda t, f: (t, 0))
    wspec = pl.BlockSpec((fc, d), lambda t, f: (f, 0))
    hid = pl.BlockSpec((ts, fc), lambda t, f: (t, f))
    return _call(
        body,
        name=name,
        grid=(s_len // ts, f_len // fc),
        in_specs=[tok, pl.BlockSpec((1, d), lambda t, f: (0, 0)), wspec, wspec, wspec],
        out_specs=[tok, tok, hid, hid],
        out_shape=[
            jax.ShapeDtypeStruct((s_len, d), F32),
            jax.ShapeDtypeStruct((s_len, d), LOW),
            jax.ShapeDtypeStruct((s_len, f_len), LOW),
            jax.ShapeDtypeStruct((s_len, f_len), LOW),
        ],
        scratch_shapes=[pltpu.VMEM((ts, fc), LOW)],
        args=(x, g, wgt, wut, wd),
        exchange=exchange,
    )


def _slabs(width, slab=256):
    return [(c0, min(c0 + slab, width)) for c0 in range(0, width, slab)]


def _ffn_bwdw(df, a, b, h, wd, name, exchange=None):
    s_len, d = df.shape
    f_len = wd.shape[0]
    tm = _pick(f_len, WGRAD_ROW_CANDIDATES)
    tk = min(WGRAD_TOK_TILE, s_len)
    n_k = s_len // tk

    def body(df_ref, a_ref, b_ref, h_ref, wd_ref, da_ref, db_ref, dwd_ref, dwg_ref, dwu_ref,
             s_ref, acc_d, acc_g, acc_u):
        k = pl.program_id(1)

        @pl.when(k == 0)
        def _():
            acc_d[...] = jnp.zeros_like(acc_d)
            acc_g[...] = jnp.zeros_like(acc_g)
            acc_u[...] = jnp.zeros_like(acc_u)

        wdv = wd_ref[...]
        for r0, r1 in _slabs(tk, BWD_ROW_SLAB):
            ds = _dot_nt(df_ref[r0:r1, :], wdv)
            av = a_ref[r0:r1, :].astype(F32)
            bv = b_ref[r0:r1, :].astype(F32)
            sig = jax.nn.sigmoid(av)
            silu = av * sig
            s_ref[r0:r1, :] = (silu * bv).astype(LOW)
            da_ref[r0:r1, :] = (ds * bv * (sig * (1.0 + av * (1.0 - sig)))).astype(LOW)
            db_ref[r0:r1, :] = (ds * silu).astype(LOW)
        hv = h_ref[...]
        acc_d[...] += _dot_tn(s_ref[...], df_ref[...])
        acc_g[...] += _dot_tn(da_ref[...], hv)
        acc_u[...] += _dot_tn(db_ref[...], hv)

        @pl.when(k == n_k - 1)
        def _():
            dwd_ref[...] = acc_d[...].astype(LOW)
            dwg_ref[...] = acc_g[...].astype(LOW)
            dwu_ref[...] = acc_u[...].astype(LOW)

    hid = pl.BlockSpec((tk, tm), lambda i, k: (k, i))
    tok = pl.BlockSpec((tk, d), lambda i, k: (k, 0))
    wrow = pl.BlockSpec((tm, d), lambda i, k: (i, 0))
    return _call(
        body,
        name=name,
        grid=(f_len // tm, n_k),
        in_specs=[tok, hid, hid, tok, wrow],
        out_specs=[hid, hid, wrow, wrow, wrow],
        out_shape=[jax.ShapeDtypeStruct((s_len, f_len), LOW)] * 2 + [jax.ShapeDtypeStruct((f_len, d), LOW)] * 3,
        scratch_shapes=[pltpu.VMEM((tk, tm), LOW)] + [pltpu.VMEM((tm, d), F32)] * 3,
        args=(df, a, b, h, wd),
        exchange=exchange,
    )


def _ffn_dx(dxo, x, g, da, db, wgt, wut, name, exchange=None):
    s_len, d = x.shape
    f_len = wgt.shape[0]
    ts = min(TOK_TILE, s_len)
    fc = _pick(f_len, FF_TILE_CANDIDATES)
    n_f = f_len // fc

    def body(dxo_ref, x_ref, g_ref, da_ref, db_ref, wg_ref, wu_ref, dxi_ref, dg_ref):
        t, f = pl.program_id(0), pl.program_id(1)

        @pl.when((t == 0) & (f == 0))
        def _():
            dg_ref[...] = jnp.zeros_like(dg_ref)

        @pl.when(f == 0)
        def _():
            dxi_ref[...] = jnp.zeros_like(dxi_ref)

        dxi_ref[...] += _dot(da_ref[...], wg_ref[...]) + _dot(db_ref[...], wu_ref[...])

        @pl.when(f == n_f - 1)
        def _():
            dx, dgp = _rms_bwd(dxi_ref[...], x_ref[...], g_ref[...])
            dxi_ref[...] = dxo_ref[...] + dx
            dg_ref[...] += dgp

    tok = pl.BlockSpec((ts, d), lambda t, f: (t, 0))
    vec = pl.BlockSpec((1, d), lambda t, f: (0, 0))
    wspec = pl.BlockSpec((fc, d), lambda t, f: (f, 0))
    hid = pl.BlockSpec((ts, fc), lambda t, f: (t, f))
    return _call(
        body,
        name=name,
        grid=(s_len // ts, n_f),
        in_specs=[tok, tok, vec, hid, hid, wspec, wspec],
        out_specs=[tok, vec],
        out_shape=[jax.ShapeDtypeStruct((s_len, d), F32), jax.ShapeDtypeStruct((1, d), F32)],
        args=(dxo, x, g, da, db, wgt, wut),
        exchange=exchange,
    )


def _wgrad_tn(xm, ym, tn, stacked, name):
    s_len, m = xm.shape
    n = ym.shape[1]
    tk = min(WGRAD_TOK_TILE, s_len)
    n_k = s_len // tk

    def body(x_ref, y_ref, o_ref, acc):
        k = pl.program_id(1)

        @pl.when(k == 0)
        def _():
            acc[...] = jnp.zeros_like(acc)

        acc[...] += _dot_tn(x_ref[...].astype(LOW), y_ref[...].astype(LOW))

        @pl.when(k == n_k - 1)
        def _():
            o_ref[...] = acc[...].astype(LOW)

    if stacked:
        out_spec = pl.BlockSpec((None, m, tn), lambda j, k: (j, 0, 0))
        out_shape = jax.ShapeDtypeStruct((n // tn, m, tn), LOW)
    else:
        out_spec = pl.BlockSpec((m, tn), lambda j, k: (0, j))
        out_shape = jax.ShapeDtypeStruct((m, n), LOW)
    return pl.pallas_call(
        body,
        name=name,
        grid=(n // tn, n_k),
        in_specs=[pl.BlockSpec((tk, m), lambda j, k: (k, 0)), pl.BlockSpec((tk, tn), lambda j, k: (k, j))],
        out_specs=out_spec,
        out_shape=out_shape,
        scratch_shapes=[pltpu.VMEM((m, tn), F32)],
        compiler_params=_params("arbitrary", "arbitrary"),
    )(xm, ym)


def _mix_parts(ext_ref, cw, ts, row0):
    dc = D_CONV

    def back(off, c0, c1):
        return ext_ref[HALO - off:HALO - off + ts, c0:c1]

    v, gate_b, gate_c = back(0, 0, dc), back(0, dc, 2 * dc), back(0, 2 * dc, 3 * dc)
    z0 = gate_c * v
    z1 = back(1, 2 * dc, 3 * dc) * back(1, 0, dc)
    z2 = back(2, 2 * dc, 3 * dc) * back(2, 0, dc)
    conv = cw[2:3, :] * z0 + cw[1:2, :] * z1 + cw[0:1, :] * z2
    rows = row0 + lax.broadcasted_iota(jnp.int32, (ts, 1), 0)
    pooled, inv_count = [], []
    for grp, w in enumerate(POOL_WINDOWS):
        c0 = 3 * dc + POOL_GC * grp
        u = back(0, c0, c0 + POOL_GC)
        acc = u
        for j in range(1, w):
            acc = acc + back(j, c0, c0 + POOL_GC)
        inv = 1.0 / jnp.minimum(rows + 1, w).astype(F32)
        pooled.append(acc * inv - u)
        inv_count.append(inv)
    return v, gate_b, gate_c, z0, z1, z2, conv, pooled, inv_count


def _mix_fwd(x, g, w_in, conv_w, pool_w, pool_scale, w_out, name, exchange=None):
    s_len, d = x.shape
    n_blk, _, wcols = w_in.shape
    p_len = n_blk * wcols
    d_mix = w_out.shape[0]
    ts = min(MIX_TOK_TILE, s_len)
    dc = D_CONV

    def body(x_ref, g_ref, win_ref, cw_ref, pw_ref, ps_ref, wout_ref, x2_ref, h_ref, proj_ref, ext_ref, cat_ref):
        t = pl.program_id(0)

        @pl.when(t == 0)
        def _():
            ext_ref[0:HALO, :] = jnp.zeros((HALO, p_len), F32)

        xv = x_ref[...]
        hb = (xv * _rms_scale(xv) * g_ref[...]).astype(LOW)
        h_ref[...] = hb
        for k in range(n_blk):
            ext_ref[HALO:HALO + ts, wcols * k:wcols * (k + 1)] = _dot(hb, win_ref[k])
        proj_ref[...] = ext_ref[HALO:HALO + ts, :]

        _, gate_b, _, _, _, _, conv, pooled, _ = _mix_parts(ext_ref, cw_ref[...], ts, t * ts)
        cat_ref[:, 0:dc] = (gate_b * conv).astype(LOW)
        for grp in range(len(POOL_WINDOWS)):
            c0 = POOL_GC * grp
            lin = _dot(pooled[grp].astype(LOW), pw_ref[grp])
            cat_ref[:, dc + c0:dc + c0 + POOL_GC] = (lin * ps_ref[:, c0:c0 + POOL_GC]).astype(LOW)
        x2_ref[...] = xv + _dot(cat_ref[...], wout_ref[...])
        ext_ref[0:HALO, :] = ext_ref[ts:ts + HALO, :]

    tok = pl.BlockSpec((ts, d), lambda t: (t, 0))

    def whole(arr):
        return pl.BlockSpec(arr.shape, lambda t: (0,) * arr.ndim)

    return _call(
        body,
        name=name,
        grid=(s_len // ts,),
        in_specs=[tok, whole(g), _resident(w_in), whole(conv_w), whole(pool_w), whole(pool_scale), _resident(w_out)],
        out_specs=[tok, tok, pl.BlockSpec((ts, p_len), lambda t: (t, 0))],
        out_shape=[
            jax.ShapeDtypeStruct((s_len, d), F32),
            jax.ShapeDtypeStruct((s_len, d), LOW),
            jax.ShapeDtypeStruct((s_len, p_len), F32),
        ],
        scratch_shapes=[pltpu.VMEM((ts + HALO, p_len), F32), pltpu.VMEM((ts, d_mix), LOW)],
        args=(x, g, w_in, conv_w, pool_w, pool_scale, w_out),
        exchange=exchange,
    )


def _mix_bwd(dx2, x, proj, g, w_in, conv_w, pool_w, pool_scale, w_out, name):
    s_len, d = x.shape
    n_blk, _, wcols = w_in.shape
    p_len = n_blk * wcols
    d_mix = w_out.shape[0]
    ts = min(MIX_TOK_TILE, s_len)
    n_t = s_len // ts
    dc = D_CONV
    n_grp = len(POOL_WINDOWS)

    def body(dx2_ref, x_ref, proj_ref, halo_ref, g_ref, win_ref, cw_ref, pw_ref, ps_ref, wout_ref,
             dx_ref, dproj_ref, cat_ref, dg_ref, dcw_ref, dpw_ref, dps_ref, df_ref, ext_ref, fut_ref):
        i = pl.program_id(0)
        t = n_t - 1 - i

        @pl.when(i == 0)
        def _():
            dg_ref[...] = jnp.zeros_like(dg_ref)
            dcw_ref[...] = jnp.zeros_like(dcw_ref)
            dpw_ref[...] = jnp.zeros_like(dpw_ref)
            dps_ref[...] = jnp.zeros_like(dps_ref)
            fut_ref[ts:ts + HALO, :] = jnp.zeros((HALO, d_mix), F32)

        ext_ref[HALO:HALO + ts, :] = proj_ref[...]

        @pl.when(t == 0)
        def _():
            ext_ref[0:HALO, :] = jnp.zeros((HALO, p_len), F32)

        @pl.when(t > 0)
        def _():
            ext_ref[0:HALO, :] = halo_ref[...]

        cw = cw_ref[...]
        v, gate_b, gate_c, z0, z1, z2, conv, pooled, inv_count = _mix_parts(ext_ref, cw, ts, t * ts)
        dx2 = dx2_ref[...]
        dcat = _dot_nt(dx2.astype(LOW), wout_ref[...])

        dy_a = dcat[:, 0:dc]
        dconv = dy_a * gate_b
        fut_ref[0:ts, 0:dc] = dconv
        cat_ref[:, 0:dc] = (gate_b * conv).astype(LOW)
        dproj_ref[:, dc:2 * dc] = (dy_a * conv).astype(LOW)
        dcw_ref[2:3, :] += jnp.sum(dconv * z0, axis=0, keepdims=True)
        dcw_ref[1:2, :] += jnp.sum(dconv * z1, axis=0, keepdims=True)
        dcw_ref[0:1, :] += jnp.sum(dconv * z2, axis=0, keepdims=True)

        dpool = []
        for grp in range(n_grp):
            c0 = POOL_GC * grp
            pooled_b = pooled[grp].astype(LOW)
            lin = _dot(pooled_b, pw_ref[grp])
            dy_b = dcat[:, dc + c0:dc + c0 + POOL_GC]
            scale = ps_ref[:, c0:c0 + POOL_GC]
            cat_ref[:, dc + c0:dc + c0 + POOL_GC] = (lin * scale).astype(LOW)
            dps_ref[:, c0:c0 + POOL_GC] += jnp.sum(dy_b * lin, axis=0, keepdims=True)
            dlin = (dy_b * scale).astype(LOW)
            dpw_ref[grp] += _dot_tn(pooled_b, dlin)
            dpool.append(_dot_nt(dlin, pw_ref[grp]))
            fut_ref[0:ts, dc + c0:dc + c0 + POOL_GC] = dpool[grp] * inv_count[grp]

        def ahead(off, c0, c1):
            return fut_ref[off:off + ts, c0:c1]

        dz = cw[2:3, :] * ahead(0, 0, dc) + cw[1:2, :] * ahead(1, 0, dc) + cw[0:1, :] * ahead(2, 0, dc)
        dproj_ref[:, 0:dc] = (dz * gate_c).astype(LOW)
        dproj_ref[:, 2 * dc:3 * dc] = (dz * v).astype(LOW)
        for grp, w in enumerate(POOL_WINDOWS):
            c0 = dc + POOL_GC * grp
            acc = ahead(0, c0, c0 + POOL_GC)
            for j in range(1, w):
                acc = acc + ahead(j, c0, c0 + POOL_GC)
            dproj_ref[:, 2 * dc + c0:2 * dc + c0 + POOL_GC] = (acc - dpool[grp]).astype(LOW)

        dh = _dot_nt(dproj_ref[:, 0:wcols], win_ref[0])
        for k in range(1, n_blk):
            dh += _dot_nt(dproj_ref[:, wcols * k:wcols * (k + 1)], win_ref[k])
        dx, dgp = _rms_bwd(dh, x_ref[...], g_ref[...])
        dx = dx2 + dx
        dx_ref[...] = dx
        df_ref[...] = (0.5 * dx).astype(LOW)
        dg_ref[...] += dgp
        fut_ref[ts:ts + HALO, :] = fut_ref[0:HALO, :]

    tok = pl.BlockSpec((ts, d), lambda i: (n_t - 1 - i, 0))
    halo = pl.BlockSpec((HALO, p_len), lambda i: (jnp.maximum((n_t - 1 - i) * (ts // HALO) - 1, 0), 0))

    def whole(arr):
        return pl.BlockSpec(arr.shape, lambda i: (0,) * arr.ndim)

    return pl.pallas_call(
        body,
        name=name,
        grid=(n_t,),
        in_specs=[tok, tok, pl.BlockSpec((ts, p_len), lambda i: (n_t - 1 - i, 0)), halo,
                  whole(g), _resident(w_in), whole(conv_w), whole(pool_w), whole(pool_scale), _resident(w_out)],
        out_specs=[tok, pl.BlockSpec((ts, p_len), lambda i: (n_t - 1 - i, 0)),
                   pl.BlockSpec((ts, d_mix), lambda i: (n_t - 1 - i, 0)),
                   whole(g), whole(conv_w), whole(pool_w), whole(pool_scale), tok],
        out_shape=[
            jax.ShapeDtypeStruct((s_len, d), F32),
            jax.ShapeDtypeStruct((s_len, p_len), LOW),
            jax.ShapeDtypeStruct((s_len, d_mix), LOW),
            jax.ShapeDtypeStruct(g.shape, F32),
            jax.ShapeDtypeStruct(conv_w.shape, F32),
            jax.ShapeDtypeStruct(pool_w.shape, F32),
            jax.ShapeDtypeStruct(pool_scale.shape, F32),
            jax.ShapeDtypeStruct((s_len, d), LOW),
        ],
        scratch_shapes=[pltpu.VMEM((ts + HALO, p_len), F32), pltpu.VMEM((ts + HALO, d_mix), F32)],
        compiler_params=_params("arbitrary"),
    )(dx2, x, proj, proj, g, w_in, conv_w, pool_w, pool_scale, w_out)


def _loss_head(x, g, target, name):
    s_len, d = x.shape
    ts = min(TOK_TILE, s_len)

    def body(x_ref, g_ref, tgt_ref, loss_ref, dx_ref, dg_ref, df_ref):
        @pl.when(pl.program_id(0) == 0)
        def _():
            loss_ref[...] = jnp.zeros_like(loss_ref)
            dg_ref[...] = jnp.zeros_like(dg_ref)

        xv, gv = x_ref[...], g_ref[...]
        err = xv * _rms_scale(xv) * gv - tgt_ref[...]
        loss_ref[...] += 0.5 * jnp.sum(jnp.mean(err * err, axis=-1, keepdims=True), axis=0, keepdims=True)
        dx, dgp = _rms_bwd(err * (1.0 / d), xv, gv)
        dx_ref[...] = dx
        df_ref[...] = (0.5 * dx).astype(LOW)
        dg_ref[...] += dgp

    tok = pl.BlockSpec((ts, d), lambda t: (t, 0))
    vec = pl.BlockSpec((1, d), lambda t: (0, 0))
    return pl.pallas_call(
        body,
        name=name,
        grid=(s_len // ts,),
        in_specs=[tok, vec, tok],
        out_specs=[pl.BlockSpec((1, 128), lambda t: (0, 0)), tok, vec, tok],
        out_shape=[
            jax.ShapeDtypeStruct((1, 128), F32),
            jax.ShapeDtypeStruct((s_len, d), F32),
            jax.ShapeDtypeStruct((1, d), F32),
            jax.ShapeDtypeStruct((s_len, d), LOW),
        ],
        compiler_params=_params("arbitrary"),
    )(x, g, target)


def _row_tile(rows, cols, stack_bytes):
    budget = 20 * 1024 * 1024
    per_row = cols * (4 * 7 + stack_bytes)
    for tr in (rows, 512, 256, 176, 128, 64, 32, 16, 8):
        if rows % tr == 0 and tr % 8 == 0 and tr * per_row * 2 <= budget:
            return tr
    return rows


def _sum_stack(stack, name):
    n, r, c = stack.shape
    tr = _row_tile(r, c, n * stack.dtype.itemsize)

    def body(s_ref, o_ref):
        acc = s_ref[0].astype(F32)
        for k in range(1, n):
            acc = acc + s_ref[k].astype(F32)
        o_ref[...] = acc

    return pl.pallas_call(
        body,
        name=name,
        grid=(r // tr,),
        in_specs=[pl.BlockSpec((n, tr, c), lambda i: (0, i, 0))],
        out_specs=pl.BlockSpec((tr, c), lambda i: (i, 0)),
        out_shape=jax.ShapeDtypeStruct((r, c), F32),
        compiler_params=_params("arbitrary"),
    )(stack)


def _adamw(stack, w, m, v, name):
    n, r, c = stack.shape
    tr = _row_tile(r, c, n * stack.dtype.itemsize)
    c1 = 1.0 - ADAM_B1 ** ADAM_STEP
    c2 = 1.0 - ADAM_B2 ** ADAM_STEP

    def body(s_ref, w_ref, m_ref, v_ref, g_ref, d_ref, mo_ref, vo_ref):
        gv = s_ref[0].astype(F32)
        for k in range(1, n):
            gv = gv + s_ref[k].astype(F32)
        mn = ADAM_B1 * m_ref[...] + (1.0 - ADAM_B1) * gv
        vn = ADAM_B2 * v_ref[...] + (1.0 - ADAM_B2) * (gv * gv)
        g_ref[...] = gv
        mo_ref[...] = mn
        vo_ref[...] = vn
        d_ref[...] = -ADAM_LR * ((mn / c1) / (jnp.sqrt(vn / c2) + ADAM_EPS) + ADAM_WD * w_ref[...])

    blk = pl.BlockSpec((tr, c), lambda i: (i, 0))
    return pl.pallas_call(
        body,
        name=name,
        grid=(r // tr,),
        in_specs=[pl.BlockSpec((n, tr, c), lambda i: (0, i, 0)), blk, blk, blk],
        out_specs=[blk] * 4,
        out_shape=[jax.ShapeDtypeStruct((r, c), F32)] * 4,
        compiler_params=_params("arbitrary"),
    )(stack, w, m, v)


def _to_sheet(parts):
    sheets, spans = [], []
    row = 0
    for p in parts:
        flat = p.reshape(-1).astype(F32)
        rows = -(-flat.shape[0] // 1024) * 8
        flat = jnp.pad(flat, (0, rows * 128 - flat.shape[0]))
        sheets.append(flat.reshape(rows, 128))
        spans.append((row, p.size, p.shape))
        row += rows
    return jnp.concatenate(sheets, axis=0), spans


def _from_sheet(sheet, spans):
    out = []
    for row, size, shape in spans:
        rows = -(-size // 1024) * 8
        out.append(sheet[row:row + rows].reshape(-1)[:size].reshape(shape))
    return out


def kernel(x, norm_ffn1, ffn1_w_gate, ffn1_w_up, ffn1_w_down, norm_mix, w_in, conv_w, pool_w, pool_scale, w_out, norm_ffn2, ffn2_w_gate, ffn2_w_up, ffn2_w_down, norm_final, loss_target, m_norm_ffn1, m_ffn1_w_gate, m_ffn1_w_up, m_ffn1_w_down, m_norm_mix, m_w_in, m_conv_w, m_pool_w, m_pool_scale, m_w_out, m_norm_ffn2, m_ffn2_w_gate, m_ffn2_w_up, m_ffn2_w_down, m_norm_final, v_norm_ffn1, v_ffn1_w_gate, v_ffn1_w_up, v_ffn1_w_down, v_norm_mix, v_w_in, v_conv_w, v_pool_w, v_pool_scale, v_w_out, v_norm_ffn2, v_ffn2_w_gate, v_ffn2_w_up, v_ffn2_w_down, v_norm_final):
    me = 4 * lax.axis_index("x") + 2 * lax.axis_index("y") + lax.axis_index("c")
    xs, tgt = x[0], loss_target[0]
    s_len, d = xs.shape
    f_shard = ffn1_w_down.shape[1]
    f_len = N_DEV * f_shard
    conv_shard = conv_w.shape[2]

    def low_t(wt):
        return wt[0].T.astype(LOW)

    def by_dev(gw):
        return gw.reshape(N_DEV, -1, d)

    conv_tile = jnp.zeros((8, 128), F32).at[0:conv_w.shape[1], 0:conv_shard].set(conv_w[0])
    full = _exchange([low_t(ffn1_w_gate), low_t(ffn1_w_up), ffn1_w_down[0].astype(LOW)], [False] * 3, "gather_ffn1",
                     relay=True)
    wg1, wu1, wd1 = (w.reshape(f_len, d) for w in full)
    pool_w_low = pool_w[0].astype(LOW)

    later = _Exchange([w_in[0].astype(LOW), w_out[0].astype(LOW), conv_tile, ffn2_w_down[0].astype(LOW),
                       low_t(ffn2_w_up)], [False] * 5, relay_at=0.8)
    x1, h1, a1, b1, w_in_full, w_out_full, conv_tiles, wd2, wu2 = _ffn_fwd(
        xs, norm_ffn1, wg1, wu1, wd1, "ffn1_fwd", exchange=later)
    w_out_full, wd2, wu2 = w_out_full.reshape(-1, d), wd2.reshape(f_len, d), wu2.reshape(f_len, d)
    conv_full = jnp.concatenate([conv_tiles[k, 0:conv_w.shape[1], 0:conv_shard] for k in range(N_DEV)], axis=1)
    later = _Exchange([low_t(ffn2_w_gate)], [False], relay_at=0.7)
    x2, h2, proj, wg2 = _mix_fwd(
        x1, norm_mix, w_in_full, conv_full, pool_w_low, pool_scale, w_out_full, "mix_fwd", exchange=later)
    wg2 = wg2.reshape(f_len, d)
    x3, h3, a2, b2 = _ffn_fwd(x2, norm_ffn2, wg2, wu2, wd2, "ffn2_fwd")
    loss_row, dx3, dg_final, df3 = _loss_head(x3, norm_final.reshape(1, d), tgt, "loss_head")

    da, db, dwd2, dwg2, dwu2 = _ffn_bwdw(df3, a2, b2, h3, wd2, "ffn2_bwdw")
    dx2, dg_ffn2 = _ffn_dx(dx3, x2, norm_ffn2, da, db, wg2, wu2, "ffn2_dx")
    pair2 = _pair_sum([by_dev(dwg2), by_dev(dwu2), by_dev(dwd2)], "pair_sum_ffn2")
    dx1, dproj, cat, dg_mix, dconv, dpool_w, dpool_scale, df1 = _mix_bwd(
        dx2, x1, proj, norm_mix, w_in_full, conv_full, pool_w_low, pool_scale, w_out_full, "mix_bwd")
    dw_in = _wgrad_tn(h2, dproj, W_IN_SHARD, True, "w_in_wgrad")
    dw_out = _wgrad_tn(cat, dx2, d, False, "w_out_wgrad")
    small_parts = [dg_mix, dg_ffn2, dg_final, dconv, dpool_w, dpool_scale]
    small_sheet, spans = _to_sheet(small_parts)
    earlier = [_Exchange([dw_in, by_dev(dw_out), small_sheet], [True, True, False]),
               _Exchange(pair2, [True] * 3, among_chips=True)]
    da, db, dwd1, dwg1, dwu1, got_in, got_out, got_small, got_g2, got_u2, got_d2 = _ffn_bwdw(
        df1, a1, b1, h1, wd1, "ffn1_bwdw", exchange=earlier)
    pair1 = _pair_sum([by_dev(dwg1), by_dev(dwu1), by_dev(dwd1)], "pair_sum_ffn1")
    earlier = _Exchange(pair1, [True] * 3, among_chips=True)
    dx0, dg_ffn1, got_g1, got_u1, got_d1 = _ffn_dx(
        dx1, xs, norm_ffn1, da, db, wg1, wu1, "ffn1_dx", exchange=earlier)
    (got_n1,) = _exchange([dg_ffn1.reshape(8, 128)], [False], "gather_dnorm_ffn1")
    got = [got_g1, got_u1, got_d1, got_in, got_out, got_g2, got_u2, got_d2, got_small]

    outs = {}

    def update(name, stack, w, m, v):
        outs[name] = _adamw(stack, w[0], m[0], v[0], "adamw_" + name)

    def update_t(name, stack, w, m, v):
        res = _adamw(stack, w[0].T, m[0].T, v[0].T, "adamw_" + name)
        outs[name] = tuple(r.T for r in res)

    update_t("ffn1_w_gate", got[0], ffn1_w_gate, m_ffn1_w_gate, v_ffn1_w_gate)
    update_t("ffn1_w_up", got[1], ffn1_w_up, m_ffn1_w_up, v_ffn1_w_up)
    update("ffn1_w_down", got[2], ffn1_w_down, m_ffn1_w_down, v_ffn1_w_down)
    update("w_in", got[3], w_in, m_w_in, v_w_in)
    update("w_out", got[4], w_out, m_w_out, v_w_out)
    update_t("ffn2_w_gate", got[5], ffn2_w_gate, m_ffn2_w_gate, v_ffn2_w_gate)
    update_t("ffn2_w_up", got[6], ffn2_w_up, m_ffn2_w_up, v_ffn2_w_up)
    update("ffn2_w_down", got[7], ffn2_w_down, m_ffn2_w_down, v_ffn2_w_down)

    g_small = _from_sheet(_sum_stack(got[8], "sum_small"), spans)
    g_norm_ffn1 = _sum_stack(got_n1, "sum_dnorm_ffn1").reshape(norm_ffn1.shape)
    g_conv = lax.dynamic_slice_in_dim(g_small[3], me * conv_shard, conv_shard, axis=1)
    small_names = ["norm_ffn1", "norm_mix", "norm_ffn2", "norm_final", "conv_w", "pool_w", "pool_scale"]
    small_g = [g_norm_ffn1, g_small[0], g_small[1], g_small[2].reshape(norm_final.shape), g_conv[None],
               g_small[4][None], g_small[5]]
    small_w = [norm_ffn1, norm_mix, norm_ffn2, norm_final, conv_w, pool_w, pool_scale]
    small_m = [m_norm_ffn1, m_norm_mix, m_norm_ffn2, m_norm_final, m_conv_w, m_pool_w, m_pool_scale]
    small_v = [v_norm_ffn1, v_norm_mix, v_norm_ffn2, v_norm_final, v_conv_w, v_pool_w, v_pool_scale]
    g_sheet, spans_u = _to_sheet(small_g)
    w_sheet, _ = _to_sheet(small_w)
    m_sheet, _ = _to_sheet(small_m)
    v_sheet, _ = _to_sheet(small_v)
    upd = _adamw(g_sheet[None], w_sheet, m_sheet, v_sheet, "adamw_small")
    small_out = [_from_sheet(u, spans_u) for u in upd]
    for k, nm in enumerate(small_names):
        outs[nm] = tuple(small_out[j][k] for j in range(4))

    loss = lax.psum(loss_row[0, 0], ("x", "y", "c"))
    order = ["norm_ffn1", "ffn1_w_gate", "ffn1_w_up", "ffn1_w_down", "norm_mix", "w_in", "conv_w", "pool_w",
             "pool_scale", "w_out", "norm_ffn2", "ffn2_w_gate", "ffn2_w_up", "ffn2_w_down", "norm_final"]
    big = {"ffn1_w_gate", "ffn1_w_up", "ffn1_w_down", "w_in", "w_out", "ffn2_w_gate", "ffn2_w_up", "ffn2_w_down"}

    def leaf(nm, j):
        val = outs[nm][j]
        return val[None] if nm in big else val

    return (loss, dx0[None],
            *[leaf(nm, 0) for nm in order], *[leaf(nm, 1) for nm in order],
            *[leaf(nm, 2) for nm in order], *[leaf(nm, 3) for nm in order])
```

```python
import jax
import jax.numpy as jnp
from jax import lax
from jax.experimental import pallas as pl
from jax.experimental.pallas import tpu as pltpu

F32 = jnp.float32
LOW = jnp.bfloat16

N_DEV = 8
EPS = 1e-6
D_CONV = 512
POOL_WINDOWS = (2, 4, 8, 16)
POOL_GC = 128
HALO = 16
W_IN_SHARD = 256

ADAM_LR = 0.001
ADAM_B1 = 0.9
ADAM_B2 = 0.999
ADAM_EPS = 1e-08
ADAM_WD = 0.01
ADAM_STEP = 10

VMEM_LIMIT_BYTES = 56 * 1024 * 1024
TOK_TILE = 512
MIX_TOK_TILE = 256
WGRAD_TOK_TILE = 2048
FF_TILE_CANDIDATES = (1408, 256, 128)
WGRAD_ROW_CANDIDATES = (256, 128)
BWD_ROW_SLAB = 512


def _params(*sem):
    return pltpu.CompilerParams(dimension_semantics=sem, vmem_limit_bytes=VMEM_LIMIT_BYTES)


def _resident(arr):
    return pl.BlockSpec(arr.shape, lambda *_: (0,) * arr.ndim, pipeline_mode=pl.Buffered(1))


def _pick(n, candidates):
    for c in candidates:
        if n % c == 0:
            return c
    raise ValueError(f"no tile in {candidates} divides {n}")


def _dot(a, b):
    return lax.dot_general(a, b, (((1,), (0,)), ((), ())), preferred_element_type=F32)


def _dot_nt(a, b):
    return lax.dot_general(a, b, (((1,), (1,)), ((), ())), preferred_element_type=F32)


def _dot_tn(a, b):
    return lax.dot_general(a, b, (((0,), (0,)), ((), ())), preferred_element_type=F32)


def _rms_scale(x):
    return lax.rsqrt(jnp.mean(x * x, axis=-1, keepdims=True) + EPS)


def _rms_bwd(dy, x, g):
    r = _rms_scale(x)
    xhat = x * r
    gdy = dy * g
    dx = r * (gdy - xhat * jnp.mean(gdy * xhat, axis=-1, keepdims=True))
    return dx, jnp.sum(dy * xhat, axis=0, keepdims=True)


class _Exchange:
    CHIPS = (2, 4, 6)

    def __init__(self, arrays, sliced, relay_at=None, among_chips=False):
        assert relay_at is None or not any(sliced)
        assert not among_chips or (all(sliced) and relay_at is None)
        self.relay_at, self.among_chips = relay_at, among_chips
        self.peers = self.CHIPS if among_chips else ((1,) + self.CHIPS if relay_at is not None else tuple(range(1, N_DEV)))
        self.arrays, self.sliced, self.n = list(arrays), list(sliced), len(arrays)
        self.out_shape = [jax.ShapeDtypeStruct(arr.shape if sl else (N_DEV,) + arr.shape, arr.dtype)
                          for arr, sl in zip(arrays, sliced)]
        self.specs = [pl.BlockSpec(memory_space=pl.ANY)] * self.n
        self.scratch_shapes = [pltpu.SemaphoreType.DMA((self.n, N_DEV - 1)),
                               pltpu.SemaphoreType.DMA((self.n, N_DEV - 1)),
                               pltpu.SemaphoreType.DMA((self.n,))]

    def _copies(self, ins, outs, sems):
        send_sems, recv_sems, local_sems = sems
        sliced = self.sliced
        mx, my, mc = lax.axis_index("x"), lax.axis_index("y"), lax.axis_index("c")
        me = 2 * mx + my if self.among_chips else 4 * mx + 2 * my + mc

        def peer(m):
            px = lax.rem(mx + ((m >> 2) & 1), 2)
            py = lax.rem(my + ((m >> 1) & 1), 2)
            pc = lax.rem(mc + (m & 1), 2)
            return (px, py, pc), (2 * px + py if self.among_chips else 4 * px + 2 * py + pc)

        def remote(a, m, arriving):
            pid, pflat = peer(m)
            return pltpu.make_async_remote_copy(
                src_ref=ins[a].at[pflat] if sliced[a] else ins[a],
                dst_ref=outs[a].at[pflat if arriving else me],
                send_sem=send_sems.at[a, m - 1],
                recv_sem=recv_sems.at[a, m - 1],
                device_id=pid,
                device_id_type=pl.DeviceIdType.MESH,
            )

        def local(a):
            return pltpu.make_async_copy(ins[a].at[me] if sliced[a] else ins[a], outs[a].at[me], local_sems.at[a])

        def passed_on(a, m):
            _, origin = peer(m)
            sibling, _ = peer(1)
            return pltpu.make_async_remote_copy(
                src_ref=outs[a].at[origin],
                dst_ref=outs[a].at[origin],
                send_sem=send_sems.at[a, m],
                recv_sem=recv_sems.at[a, m],
                device_id=sibling,
                device_id_type=pl.DeviceIdType.MESH,
            )

        return remote, local, passed_on

    def start(self, ins, outs, sems):
        remote, local, _ = self._copies(ins, outs, sems)
        for a in range(self.n):
            local(a).start()
        for m in self.peers:
            for a in range(self.n):
                remote(a, m, False).start()

    def relay(self, ins, outs, sems):
        remote, _, passed_on = self._copies(ins, outs, sems)
        for m in self.CHIPS:
            for a in range(self.n):
                remote(a, m, True).wait_recv()
                passed_on(a, m).start()

    def wait(self, ins, outs, sems):
        remote, local, passed_on = self._copies(ins, outs, sems)
        if self.relay_at is None:
            for m in self.peers:
                for a in range(self.n):
                    remote(a, m, True).wait_recv()
            for m in self.peers:
                for a in range(self.n):
                    remote(a, m, False).wait_send()
        else:
            for m in (1,) + tuple(c + 1 for c in self.CHIPS):
                for a in range(self.n):
                    remote(a, m, True).wait_recv()
            for m in self.peers:
                for a in range(self.n):
                    remote(a, m, False).wait_send()
            for m in self.CHIPS:
                for a in range(self.n):
                    passed_on(a, m).wait_send()
        for a in range(self.n):
            local(a).wait()


def _pair_sum(stacks, name):
    n = len(stacks)
    n_chip = N_DEV // 2
    half = [(n_chip,) + st.shape[1:] for st in stacks]

    def body(*refs):
        ins, outs, mine, theirs = refs[:n], refs[n:2 * n], refs[2 * n:3 * n], refs[3 * n:4 * n]
        local_sems, send_sems, recv_sems = refs[4 * n:]
        mx, my, mc = lax.axis_index("x"), lax.axis_index("y"), lax.axis_index("c")

        def own(a, k):
            return pltpu.make_async_copy(ins[a].at[2 * k + mc], mine[a].at[k], local_sems.at[a, k])

        def swap(a, k):
            return pltpu.make_async_remote_copy(
                src_ref=ins[a].at[2 * k + (1 - mc)], dst_ref=theirs[a].at[k],
                send_sem=send_sems.at[a, k], recv_sem=recv_sems.at[a, k],
                device_id=(mx, my, 1 - mc), device_id_type=pl.DeviceIdType.MESH)

        for k in range(n_chip):
            for a in range(n):
                own(a, k).start()
                swap(a, k).start()
        for k in range(n_chip):
            for a in range(n):
                own(a, k).wait()
                swap(a, k).wait()
                outs[a][k] = (mine[a][k].astype(F32) + theirs[a][k].astype(F32)).astype(LOW)

    return pl.pallas_call(
        body, name=name,
        out_shape=[jax.ShapeDtypeStruct(h, LOW) for h in half],
        in_specs=[pl.BlockSpec(memory_space=pl.ANY)] * n,
        out_specs=[pl.BlockSpec(memory_space=pltpu.VMEM)] * n,
        scratch_shapes=([pltpu.VMEM(h, st.dtype) for h, st in zip(half, stacks)] * 2
                        + [pltpu.SemaphoreType.DMA((n, n_chip))] * 3),
        compiler_params=pltpu.CompilerParams(vmem_limit_bytes=VMEM_LIMIT_BYTES),
    )(*stacks)


def _exchange(arrays, sliced, name, relay=False):
    ex = _Exchange(arrays, sliced, relay_at=0 if relay else None)

    def body(*refs):
        ins, outs, sems = refs[:ex.n], refs[ex.n:2 * ex.n], refs[2 * ex.n:]
        ex.start(ins, outs, sems)
        if relay:
            ex.relay(ins, outs, sems)
        ex.wait(ins, outs, sems)

    return pl.pallas_call(body, name=name, out_shape=ex.out_shape, in_specs=ex.specs, out_specs=ex.specs,
                          scratch_shapes=ex.scratch_shapes)(*arrays)


def _call(body, *, name, grid, in_specs, out_specs, out_shape, args, scratch_shapes=(), exchange=None):
    params = _params(*(("arbitrary",) * len(grid)))
    if exchange is None:
        return pl.pallas_call(body, name=name, grid=grid, in_specs=in_specs, out_specs=out_specs, out_shape=out_shape,
                              scratch_shapes=list(scratch_shapes), compiler_params=params)(*args)
    exs = list(exchange) if isinstance(exchange, (list, tuple)) else [exchange]
    n_in, n_out, n_scr = len(in_specs), len(out_specs), len(scratch_shapes)
    n_ex = sum(ex.n for ex in exs)
    n_steps = 1
    for g in grid:
        n_steps *= g

    def hosted(*refs):
        ins, refs = refs[:n_in], refs[n_in:]
        ex_ins, refs = refs[:n_ex], refs[n_ex:]
        outs, refs = refs[:n_out], refs[n_out:]
        ex_outs, refs = refs[:n_ex], refs[n_ex:]
        scr, sems = refs[:n_scr], refs[n_scr:]
        parts, at = [], 0
        for j, ex in enumerate(exs):
            parts.append((ex_ins[at:at + ex.n], ex_outs[at:at + ex.n], sems[3 * j:3 * j + 3]))
            at += ex.n
        step = pl.program_id(0)
        for ax in range(1, len(grid)):
            step = step * grid[ax] + pl.program_id(ax)

        @pl.when(step == 0)
        def _():
            for ex, part in zip(exs, parts):
                ex.start(*part)

        body(*ins, *outs, *scr)

        for ex, part in zip(exs, parts):
            if ex.relay_at is not None:
                @pl.when(step == min(int(ex.relay_at * n_steps), n_steps - 1))
                def _():
                    ex.relay(*part)

        @pl.when(step == n_steps - 1)
        def _():
            for ex, part in zip(exs, parts):
                ex.wait(*part)

    return pl.pallas_call(
        hosted, name=name, grid=grid,
        in_specs=list(in_specs) + [sp for ex in exs for sp in ex.specs],
        out_specs=list(out_specs) + [sp for ex in exs for sp in ex.specs],
        out_shape=list(out_shape) + [sh for ex in exs for sh in ex.out_shape],
        scratch_shapes=list(scratch_shapes) + [sc for ex in exs for sc in ex.scratch_shapes],
        compiler_params=params)(*args, *[arr for ex in exs for arr in ex.arrays])


def _ffn_fwd(x, g, wgt, wut, wd, name, exchange=None):
    s_len, d = x.shape
    f_len = wd.shape[0]
    ts = min(TOK_TILE, s_len)
    fc = _pick(f_len, FF_TILE_CANDIDATES)

    def body(x_ref, g_ref, wg_ref, wu_ref, wd_ref, xo_ref, h_ref, a_ref, b_ref, s_ref):
        @pl.when(pl.program_id(1) == 0)
        def _():
            xv = x_ref[...]
            h_ref[...] = (xv * _rms_scale(xv) * g_ref[...]).astype(LOW)
            xo_ref[...] = xv

        hb = h_ref[...]
        for c0, c1 in _slabs(fc):
            a = _dot_nt(hb, wg_ref[c0:c1, :])
            b = _dot_nt(hb, wu_ref[c0:c1, :])
            s_ref[:, c0:c1] = (a * jax.nn.sigmoid(a) * b).astype(LOW)
            a_ref[:, c0:c1] = a.astype(LOW)
            b_ref[:, c0:c1] = b.astype(LOW)
        xo_ref[...] += 0.5 * _dot(s_ref[...], wd_ref[...])

    tok = pl.BlockSpec((ts, d), lambda t, f: (t, 0))
    wspec = pl.BlockSpec((fc, d), lambda t, f: (f, 0))
    hid = pl.BlockSpec((ts, fc), lambda t, f: (t, f))
    return _call(
        body,
        name=name,
        grid=(s_len // ts, f_len // fc),
        in_specs=[tok, pl.BlockSpec((1, d), lambda t, f: (0, 0)), wspec, wspec, wspec],
        out_specs=[tok, tok, hid, hid],
        out_shape=[
            jax.ShapeDtypeStruct((s_len, d), F32),
            jax.ShapeDtypeStruct((s_len, d), LOW),
            jax.ShapeDtypeStruct((s_len, f_len), LOW),
            jax.ShapeDtypeStruct((s_len, f_len), LOW),
        ],
        scratch_shapes=[pltpu.VMEM((ts, fc), LOW)],
        args=(x, g, wgt, wut, wd),
        exchange=exchange,
    )


def _slabs(width, slab=256):
    return [(c0, min(c0 + slab, width)) for c0 in range(0, width, slab)]


def _ffn_bwdw(df, a, b, h, wd, name, exchange=None):
    s_len, d = df.shape
    f_len = wd.shape[0]
    tm = _pick(f_len, WGRAD_ROW_CANDIDATES)
    tk = min(WGRAD_TOK_TILE, s_len)
    n_k = s_len // tk

    def body(df_ref, a_ref, b_ref, h_ref, wd_ref, da_ref, db_ref, dwd_ref, dwg_ref, dwu_ref,
             s_ref, acc_d, acc_g, acc_u):
        k = pl.program_id(1)

        @pl.when(k == 0)
        def _():
            acc_d[...] = jnp.zeros_like(acc_d)
            acc_g[...] = jnp.zeros_like(acc_g)
            acc_u[...] = jnp.zeros_like(acc_u)

        wdv = wd_ref[...]
        for r0, r1 in _slabs(tk, BWD_ROW_SLAB):
            ds = _dot_nt(df_ref[r0:r1, :], wdv)
            av = a_ref[r0:r1, :].astype(F32)
            bv = b_ref[r0:r1, :].astype(F32)
            sig = jax.nn.sigmoid(av)
            silu = av * sig
            s_ref[r0:r1, :] = (silu * bv).astype(LOW)
            da_ref[r0:r1, :] = (ds * bv * (sig * (1.0 + av * (1.0 - sig)))).astype(LOW)
            db_ref[r0:r1, :] = (ds * silu).astype(LOW)
        hv = h_ref[...]
        acc_d[...] += _dot_tn(s_ref[...], df_ref[...])
        acc_g[...] += _dot_tn(da_ref[...], hv)
        acc_u[...] += _dot_tn(db_ref[...], hv)

        @pl.when(k == n_k - 1)
        def _():
            dwd_ref[...] = acc_d[...].astype(LOW)
            dwg_ref[...] = acc_g[...].astype(LOW)
            dwu_ref[...] = acc_u[...].astype(LOW)

    hid = pl.BlockSpec((tk, tm), lambda i, k: (k, i))
    tok = pl.BlockSpec((tk, d), lambda i, k: (k, 0))
    wrow = pl.BlockSpec((tm, d), lambda i, k: (i, 0))
    return _call(
        body,
        name=name,
        grid=(f_len // tm, n_k),
        in_specs=[tok, hid, hid, tok, wrow],
        out_specs=[hid, hid, wrow, wrow, wrow],
        out_shape=[jax.ShapeDtypeStruct((s_len, f_len), LOW)] * 2 + [jax.ShapeDtypeStruct((f_len, d), LOW)] * 3,
        scratch_shapes=[pltpu.VMEM((tk, tm), LOW)] + [pltpu.VMEM((tm, d), F32)] * 3,
        args=(df, a, b, h, wd),
        exchange=exchange,
    )


def _ffn_dx(dxo, x, g, da, db, wgt, wut, name, exchange=None):
    s_len, d = x.shape
    f_len = wgt.shape[0]
    ts = min(TOK_TILE, s_len)
    fc = _pick(f_len, FF_TILE_CANDIDATES)
    n_f = f_len // fc

    def body(dxo_ref, x_ref, g_ref, da_ref, db_ref, wg_ref, wu_ref, dxi_ref, dg_ref):
        t, f = pl.program_id(0), pl.program_id(1)

        @pl.when((t == 0) & (f == 0))
        def _():
            dg_ref[...] = jnp.zeros_like(dg_ref)

        @pl.when(f == 0)
        def _():
            dxi_ref[...] = jnp.zeros_like(dxi_ref)

        dxi_ref[...] += _dot(da_ref[...], wg_ref[...]) + _dot(db_ref[...], wu_ref[...])

        @pl.when(f == n_f - 1)
        def _():
            dx, dgp = _rms_bwd(dxi_ref[...], x_ref[...], g_ref[...])
            dxi_ref[...] = dxo_ref[...] + dx
            dg_ref[...] += dgp

    tok = pl.BlockSpec((ts, d), lambda t, f: (t, 0))
    vec = pl.BlockSpec((1, d), lambda t, f: (0, 0))
    wspec = pl.BlockSpec((fc, d), lambda t, f: (f, 0))
    hid = pl.BlockSpec((ts, fc), lambda t, f: (t, f))
    return _call(
        body,
        name=name,
        grid=(s_len // ts, n_f),
        in_specs=[tok, tok, vec, hid, hid, wspec, wspec],
        out_specs=[tok, vec],
        out_shape=[jax.ShapeDtypeStruct((s_len, d), F32), jax.ShapeDtypeStruct((1, d), F32)],
        args=(dxo, x, g, da, db, wgt, wut),
        exchange=exchange,
    )


def _wgrad_tn(xm, ym, tn, stacked, name):
    s_len, m = xm.shape
    n = ym.shape[1]
    tk = min(WGRAD_TOK_TILE, s_len)
    n_k = s_len // tk

    def body(x_ref, y_ref, o_ref, acc):
        k = pl.program_id(1)

        @pl.when(k == 0)
        def _():
            acc[...] = jnp.zeros_like(acc)

        acc[...] += _dot_tn(x_ref[...].astype(LOW), y_ref[...].astype(LOW))

        @pl.when(k == n_k - 1)
        def _():
            o_ref[...] = acc[...].astype(LOW)

    if stacked:
        out_spec = pl.BlockSpec((None, m, tn), lambda j, k: (j, 0, 0))
        out_shape = jax.ShapeDtypeStruct((n // tn, m, tn), LOW)
    else:
        out_spec = pl.BlockSpec((m, tn), lambda j, k: (0, j))
        out_shape = jax.ShapeDtypeStruct((m, n), LOW)
    return pl.pallas_call(
        body,
        name=name,
        grid=(n // tn, n_k),
        in_specs=[pl.BlockSpec((tk, m), lambda j, k: (k, 0)), pl.BlockSpec((tk, tn), lambda j, k: (k, j))],
        out_specs=out_spec,
        out_shape=out_shape,
        scratch_shapes=[pltpu.VMEM((m, tn), F32)],
        compiler_params=_params("arbitrary", "arbitrary"),
    )(xm, ym)


def _mix_parts(ext_ref, cw, ts, row0):
    dc = D_CONV

    def back(off, c0, c1):
        return ext_ref[HALO - off:HALO - off + ts, c0:c1]

    v, gate_b, gate_c = back(0, 0, dc), back(0, dc, 2 * dc), back(0, 2 * dc, 3 * dc)
    z0 = gate_c * v
    z1 = back(1, 2 * dc, 3 * dc) * back(1, 0, dc)
    z2 = back(2, 2 * dc, 3 * dc) * back(2, 0, dc)
    conv = cw[2:3, :] * z0 + cw[1:2, :] * z1 + cw[0:1, :] * z2
    rows = row0 + lax.broadcasted_iota(jnp.int32, (ts, 1), 0)
    pooled, inv_count = [], []
    for grp, w in enumerate(POOL_WINDOWS):
        c0 = 3 * dc + POOL_GC * grp
        u = back(0, c0, c0 + POOL_GC)
        acc = u
        for j in range(1, w):
            acc = acc + back(j, c0, c0 + POOL_GC)
        inv = 1.0 / jnp.minimum(rows + 1, w).astype(F32)
        pooled.append(acc * inv - u)
        inv_count.append(inv)
    return v, gate_b, gate_c, z0, z1, z2, conv, pooled, inv_count


def _mix_fwd(x, g, w_in, conv_w, pool_w, pool_scale, w_out, name, exchange=None):
    s_len, d = x.shape
    n_blk, _, wcols = w_in.shape
    p_len = n_blk * wcols
    d_mix = w_out.shape[0]
    ts = min(MIX_TOK_TILE, s_len)
    dc = D_CONV

    def body(x_ref, g_ref, win_ref, cw_ref, pw_ref, ps_ref, wout_ref, x2_ref, h_ref, proj_ref, ext_ref, cat_ref):
        t = pl.program_id(0)

        @pl.when(t == 0)
        def _():
            ext_ref[0:HALO, :] = jnp.zeros((HALO, p_len), F32)

        xv = x_ref[...]
        hb = (xv * _rms_scale(xv) * g_ref[...]).astype(LOW)
        h_ref[...] = hb
        for k in range(n_blk):
            ext_ref[HALO:HALO + ts, wcols * k:wcols * (k + 1)] = _dot(hb, win_ref[k])
        proj_ref[...] = ext_ref[HALO:HALO + ts, :]

        _, gate_b, _, _, _, _, conv, pooled, _ = _mix_parts(ext_ref, cw_ref[...], ts, t * ts)
        cat_ref[:, 0:dc] = (gate_b * conv).astype(LOW)
        for grp in range(len(POOL_WINDOWS)):
            c0 = POOL_GC * grp
            lin = _dot(pooled[grp].astype(LOW), pw_ref[grp])
            cat_ref[:, dc + c0:dc + c0 + POOL_GC] = (lin * ps_ref[:, c0:c0 + POOL_GC]).astype(LOW)
        x2_ref[...] = xv + _dot(cat_ref[...], wout_ref[...])
        ext_ref[0:HALO, :] = ext_ref[ts:ts + HALO, :]

    tok = pl.BlockSpec((ts, d), lambda t: (t, 0))

    def whole(arr):
        return pl.BlockSpec(arr.shape, lambda t: (0,) * arr.ndim)

    return _call(
        body,
        name=name,
        grid=(s_len // ts,),
        in_specs=[tok, whole(g), _resident(w_in), whole(conv_w), whole(pool_w), whole(pool_scale), _resident(w_out)],
        out_specs=[tok, tok, pl.BlockSpec((ts, p_len), lambda t: (t, 0))],
        out_shape=[
            jax.ShapeDtypeStruct((s_len, d), F32),
            jax.ShapeDtypeStruct((s_len, d), LOW),
            jax.ShapeDtypeStruct((s_len, p_len), F32),
        ],
        scratch_shapes=[pltpu.VMEM((ts + HALO, p_len), F32), pltpu.VMEM((ts, d_mix), LOW)],
        args=(x, g, w_in, conv_w, pool_w, pool_scale, w_out),
        exchange=exchange,
    )


def _mix_bwd(dx2, x, proj, g, w_in, conv_w, pool_w, pool_scale, w_out, name, exchange=None):
    s_len, d = x.shape
    n_blk, _, wcols = w_in.shape
    p_len = n_blk * wcols
    d_mix = w_out.shape[0]
    ts = min(MIX_TOK_TILE, s_len)
    n_t = s_len // ts
    dc = D_CONV
    n_grp = len(POOL_WINDOWS)

    def body(dx2_ref, x_ref, proj_ref, halo_ref, g_ref, win_ref, cw_ref, pw_ref, ps_ref, wout_ref,
             dx_ref, dproj_ref, cat_ref, dg_ref, dcw_ref, dpw_ref, dps_ref, df_ref, ext_ref, fut_ref):
        i = pl.program_id(0)
        t = n_t - 1 - i

        @pl.when(i == 0)
        def _():
            dg_ref[...] = jnp.zeros_like(dg_ref)
            dcw_ref[...] = jnp.zeros_like(dcw_ref)
            dpw_ref[...] = jnp.zeros_like(dpw_ref)
            dps_ref[...] = jnp.zeros_like(dps_ref)
            fut_ref[ts:ts + HALO, :] = jnp.zeros((HALO, d_mix), F32)

        ext_ref[HALO:HALO + ts, :] = proj_ref[...]

        @pl.when(t == 0)
        def _():
            ext_ref[0:HALO, :] = jnp.zeros((HALO, p_len), F32)

        @pl.when(t > 0)
        def _():
            ext_ref[0:HALO, :] = halo_ref[...]

        cw = cw_ref[...]
        v, gate_b, gate_c, z0, z1, z2, conv, pooled, inv_count = _mix_parts(ext_ref, cw, ts, t * ts)
        dx2 = dx2_ref[...]
        dcat = _dot_nt(dx2.astype(LOW), wout_ref[...])

        dy_a = dcat[:, 0:dc]
        dconv = dy_a * gate_b
        fut_ref[0:ts, 0:dc] = dconv
        cat_ref[:, 0:dc] = (gate_b * conv).astype(LOW)
        dproj_ref[:, dc:2 * dc] = (dy_a * conv).astype(LOW)
        dcw_ref[2:3, :] += jnp.sum(dconv * z0, axis=0, keepdims=True)
        dcw_ref[1:2, :] += jnp.sum(dconv * z1, axis=0, keepdims=True)
        dcw_ref[0:1, :] += jnp.sum(dconv * z2, axis=0, keepdims=True)

        dpool = []
        for grp in range(n_grp):
            c0 = POOL_GC * grp
            pooled_b = pooled[grp].astype(LOW)
            lin = _dot(pooled_b, pw_ref[grp])
            dy_b = dcat[:, dc + c0:dc + c0 + POOL_GC]
            scale = ps_ref[:, c0:c0 + POOL_GC]
            cat_ref[:, dc + c0:dc + c0 + POOL_GC] = (lin * scale).astype(LOW)
            dps_ref[:, c0:c0 + POOL_GC] += jnp.sum(dy_b * lin, axis=0, keepdims=True)
            dlin = (dy_b * scale).astype(LOW)
            dpw_ref[grp] += _dot_tn(pooled_b, dlin)
            dpool.append(_dot_nt(dlin, pw_ref[grp]))
            fut_ref[0:ts, dc + c0:dc + c0 + POOL_GC] = dpool[grp] * inv_count[grp]

        def ahead(off, c0, c1):
            return fut_ref[off:off + ts, c0:c1]

        dz = cw[2:3, :] * ahead(0, 0, dc) + cw[1:2, :] * ahead(1, 0, dc) + cw[0:1, :] * ahead(2, 0, dc)
        dproj_ref[:, 0:dc] = (dz * gate_c).astype(LOW)
        dproj_ref[:, 2 * dc:3 * dc] = (dz * v).astype(LOW)
        for grp, w in enumerate(POOL_WINDOWS):
            c0 = dc + POOL_GC * grp
            acc = ahead(0, c0, c0 + POOL_GC)
            for j in range(1, w):
                acc = acc + ahead(j, c0, c0 + POOL_GC)
            dproj_ref[:, 2 * dc + c0:2 * dc + c0 + POOL_GC] = (acc - dpool[grp]).astype(LOW)

        dh = _dot_nt(dproj_ref[:, 0:wcols], win_ref[0])
        for k in range(1, n_blk):
            dh += _dot_nt(dproj_ref[:, wcols * k:wcols * (k + 1)], win_ref[k])
        dx, dgp = _rms_bwd(dh, x_ref[...], g_ref[...])
        dx = dx2 + dx
        dx_ref[...] = dx
        df_ref[...] = (0.5 * dx).astype(LOW)
        dg_ref[...] += dgp
        fut_ref[ts:ts + HALO, :] = fut_ref[0:HALO, :]

    tok = pl.BlockSpec((ts, d), lambda i: (n_t - 1 - i, 0))
    halo = pl.BlockSpec((HALO, p_len), lambda i: (jnp.maximum((n_t - 1 - i) * (ts // HALO) - 1, 0), 0))

    def whole(arr):
        return pl.BlockSpec(arr.shape, lambda i: (0,) * arr.ndim)

    return _call(
        body,
        name=name,
        grid=(n_t,),
        in_specs=[tok, tok, pl.BlockSpec((ts, p_len), lambda i: (n_t - 1 - i, 0)), halo,
                  whole(g), _resident(w_in), whole(conv_w), whole(pool_w), whole(pool_scale), _resident(w_out)],
        out_specs=[tok, pl.BlockSpec((ts, p_len), lambda i: (n_t - 1 - i, 0)),
                   pl.BlockSpec((ts, d_mix), lambda i: (n_t - 1 - i, 0)),
                   whole(g), whole(conv_w), whole(pool_w), whole(pool_scale), tok],
        out_shape=[
            jax.ShapeDtypeStruct((s_len, d), F32),
            jax.ShapeDtypeStruct((s_len, p_len), LOW),
            jax.ShapeDtypeStruct((s_len, d_mix), LOW),
            jax.ShapeDtypeStruct(g.shape, F32),
            jax.ShapeDtypeStruct(conv_w.shape, F32),
            jax.ShapeDtypeStruct(pool_w.shape, F32),
            jax.ShapeDtypeStruct(pool_scale.shape, F32),
            jax.ShapeDtypeStruct((s_len, d), LOW),
        ],
        scratch_shapes=[pltpu.VMEM((ts + HALO, p_len), F32), pltpu.VMEM((ts + HALO, d_mix), F32)],
        args=(dx2, x, proj, proj, g, w_in, conv_w, pool_w, pool_scale, w_out),
        exchange=exchange,
    )


def _loss_head(x, g, target, name):
    s_len, d = x.shape
    ts = min(TOK_TILE, s_len)

    def body(x_ref, g_ref, tgt_ref, loss_ref, dx_ref, dg_ref, df_ref):
        @pl.when(pl.program_id(0) == 0)
        def _():
            loss_ref[...] = jnp.zeros_like(loss_ref)
            dg_ref[...] = jnp.zeros_like(dg_ref)

        xv, gv = x_ref[...], g_ref[...]
        err = xv * _rms_scale(xv) * gv - tgt_ref[...]
        loss_ref[...] += 0.5 * jnp.sum(jnp.mean(err * err, axis=-1, keepdims=True), axis=0, keepdims=True)
        dx, dgp = _rms_bwd(err * (1.0 / d), xv, gv)
        dx_ref[...] = dx
        df_ref[...] = (0.5 * dx).astype(LOW)
        dg_ref[...] += dgp

    tok = pl.BlockSpec((ts, d), lambda t: (t, 0))
    vec = pl.BlockSpec((1, d), lambda t: (0, 0))
    return pl.pallas_call(
        body,
        name=name,
        grid=(s_len // ts,),
        in_specs=[tok, vec, tok],
        out_specs=[pl.BlockSpec((1, 128), lambda t: (0, 0)), tok, vec, tok],
        out_shape=[
            jax.ShapeDtypeStruct((1, 128), F32),
            jax.ShapeDtypeStruct((s_len, d), F32),
            jax.ShapeDtypeStruct((1, d), F32),
            jax.ShapeDtypeStruct((s_len, d), LOW),
        ],
        compiler_params=_params("arbitrary"),
    )(x, g, target)


def _row_tile(rows, cols, stack_bytes):
    budget = 20 * 1024 * 1024
    per_row = cols * (4 * 7 + stack_bytes)
    for tr in (rows, 512, 256, 176, 128, 64, 32, 16, 8):
        if rows % tr == 0 and tr % 8 == 0 and tr * per_row * 2 <= budget:
            return tr
    return rows


def _sum_stack(stack, name):
    n, r, c = stack.shape
    tr = _row_tile(r, c, n * stack.dtype.itemsize)

    def body(s_ref, o_ref):
        acc = s_ref[0].astype(F32)
        for k in range(1, n):
            acc = acc + s_ref[k].astype(F32)
        o_ref[...] = acc

    return pl.pallas_call(
        body,
        name=name,
        grid=(r // tr,),
        in_specs=[pl.BlockSpec((n, tr, c), lambda i: (0, i, 0))],
        out_specs=pl.BlockSpec((tr, c), lambda i: (i, 0)),
        out_shape=jax.ShapeDtypeStruct((r, c), F32),
        compiler_params=_params("arbitrary"),
    )(stack)


def _adamw(stack, w, m, v, name):
    n, r, c = stack.shape
    tr = _row_tile(r, c, n * stack.dtype.itemsize)
    c1 = 1.0 - ADAM_B1 ** ADAM_STEP
    c2 = 1.0 - ADAM_B2 ** ADAM_STEP

    def body(s_ref, w_ref, m_ref, v_ref, g_ref, d_ref, mo_ref, vo_ref):
        gv = s_ref[0].astype(F32)
        for k in range(1, n):
            gv = gv + s_ref[k].astype(F32)
        mn = ADAM_B1 * m_ref[...] + (1.0 - ADAM_B1) * gv
        vn = ADAM_B2 * v_ref[...] + (1.0 - ADAM_B2) * (gv * gv)
        g_ref[...] = gv
        mo_ref[...] = mn
        vo_ref[...] = vn
        d_ref[...] = -ADAM_LR * ((mn / c1) / (jnp.sqrt(vn / c2) + ADAM_EPS) + ADAM_WD * w_ref[...])

    blk = pl.BlockSpec((tr, c), lambda i: (i, 0))
    return pl.pallas_call(
        body,
        name=name,
        grid=(r // tr,),
        in_specs=[pl.BlockSpec((n, tr, c), lambda i: (0, i, 0)), blk, blk, blk],
        out_specs=[blk] * 4,
        out_shape=[jax.ShapeDtypeStruct((r, c), F32)] * 4,
        compiler_params=_params("arbitrary"),
    )(stack, w, m, v)


def _to_sheet(parts):
    sheets, spans = [], []
    row = 0
    for p in parts:
        flat = p.reshape(-1).astype(F32)
        rows = -(-flat.shape[0] // 1024) * 8
        flat = jnp.pad(flat, (0, rows * 128 - flat.shape[0]))
        sheets.append(flat.reshape(rows, 128))
        spans.append((row, p.size, p.shape))
        row += rows
    return jnp.concatenate(sheets, axis=0), spans


def _from_sheet(sheet, spans):
    out = []
    for row, size, shape in spans:
        rows = -(-size // 1024) * 8
        out.append(sheet[row:row + rows].reshape(-1)[:size].reshape(shape))
    return out


def kernel(x, norm_ffn1, ffn1_w_gate, ffn1_w_up, ffn1_w_down, norm_mix, w_in, conv_w, pool_w, pool_scale, w_out, norm_ffn2, ffn2_w_gate, ffn2_w_up, ffn2_w_down, norm_final, loss_target, m_norm_ffn1, m_ffn1_w_gate, m_ffn1_w_up, m_ffn1_w_down, m_norm_mix, m_w_in, m_conv_w, m_pool_w, m_pool_scale, m_w_out, m_norm_ffn2, m_ffn2_w_gate, m_ffn2_w_up, m_ffn2_w_down, m_norm_final, v_norm_ffn1, v_ffn1_w_gate, v_ffn1_w_up, v_ffn1_w_down, v_norm_mix, v_w_in, v_conv_w, v_pool_w, v_pool_scale, v_w_out, v_norm_ffn2, v_ffn2_w_gate, v_ffn2_w_up, v_ffn2_w_down, v_norm_final):
    me = 4 * lax.axis_index("x") + 2 * lax.axis_index("y") + lax.axis_index("c")
    xs, tgt = x[0], loss_target[0]
    s_len, d = xs.shape
    f_shard = ffn1_w_down.shape[1]
    f_len = N_DEV * f_shard
    conv_shard = conv_w.shape[2]

    def low_t(wt):
        return wt[0].T.astype(LOW)

    def by_dev(gw):
        return gw.reshape(N_DEV, -1, d)

    conv_tile = jnp.zeros((8, 128), F32).at[0:conv_w.shape[1], 0:conv_shard].set(conv_w[0])
    full = _exchange([low_t(ffn1_w_gate), low_t(ffn1_w_up), ffn1_w_down[0].astype(LOW)], [False] * 3, "gather_ffn1",
                     relay=True)
    wg1, wu1, wd1 = (w.reshape(f_len, d) for w in full)
    pool_w_low = pool_w[0].astype(LOW)

    later = _Exchange([w_in[0].astype(LOW), w_out[0].astype(LOW), conv_tile, ffn2_w_down[0].astype(LOW)],
                      [False] * 4, relay_at=0.7)
    x1, h1, a1, b1, w_in_full, w_out_full, conv_tiles, wd2 = _ffn_fwd(
        xs, norm_ffn1, wg1, wu1, wd1, "ffn1_fwd", exchange=later)
    w_out_full, wd2 = w_out_full.reshape(-1, d), wd2.reshape(f_len, d)
    conv_full = jnp.concatenate([conv_tiles[k, 0:conv_w.shape[1], 0:conv_shard] for k in range(N_DEV)], axis=1)
    later = _Exchange([low_t(ffn2_w_gate), low_t(ffn2_w_up)], [False] * 2, relay_at=0.85)
    x2, h2, proj, wg2, wu2 = _mix_fwd(
        x1, norm_mix, w_in_full, conv_full, pool_w_low, pool_scale, w_out_full, "mix_fwd", exchange=later)
    wg2, wu2 = wg2.reshape(f_len, d), wu2.reshape(f_len, d)
    x3, h3, a2, b2 = _ffn_fwd(x2, norm_ffn2, wg2, wu2, wd2, "ffn2_fwd")
    loss_row, dx3, dg_final, df3 = _loss_head(x3, norm_final.reshape(1, d), tgt, "loss_head")

    da, db, dwd2, dwg2, dwu2 = _ffn_bwdw(df3, a2, b2, h3, wd2, "ffn2_bwdw")
    pair2 = _pair_sum([by_dev(dwg2), by_dev(dwu2), by_dev(dwd2)], "pair_sum_ffn2")
    earlier = _Exchange(pair2[:2], [True] * 2, among_chips=True)
    dx2, dg_ffn2, got_g2, got_u2 = _ffn_dx(dx3, x2, norm_ffn2, da, db, wg2, wu2, "ffn2_dx", exchange=earlier)
    earlier = _Exchange(pair2[2:], [True], among_chips=True)
    dx1, dproj, cat, dg_mix, dconv, dpool_w, dpool_scale, df1, got_d2 = _mix_bwd(
        dx2, x1, proj, norm_mix, w_in_full, conv_full, pool_w_low, pool_scale, w_out_full, "mix_bwd",
        exchange=earlier)
    dw_in = _wgrad_tn(h2, dproj, W_IN_SHARD, True, "w_in_wgrad")
    dw_out = _wgrad_tn(cat, dx2, d, False, "w_out_wgrad")
    small_parts = [dg_mix, dg_ffn2, dg_final, dconv, dpool_w, dpool_scale]
    small_sheet, spans = _to_sheet(small_parts)
    earlier = _Exchange([dw_in, by_dev(dw_out), small_sheet], [True, True, False])
    da, db, dwd1, dwg1, dwu1, got_in, got_out, got_small = _ffn_bwdw(
        df1, a1, b1, h1, wd1, "ffn1_bwdw", exchange=earlier)
    pair1 = _pair_sum([by_dev(dwg1), by_dev(dwu1), by_dev(dwd1)], "pair_sum_ffn1")
    earlier = _Exchange(pair1, [True] * 3, among_chips=True)
    dx0, dg_ffn1, got_g1, got_u1, got_d1 = _ffn_dx(
        dx1, xs, norm_ffn1, da, db, wg1, wu1, "ffn1_dx", exchange=earlier)
    (got_n1,) = _exchange([dg_ffn1.reshape(8, 128)], [False], "gather_dnorm_ffn1")
    got = [got_g1, got_u1, got_d1, got_in, got_out, got_g2, got_u2, got_d2, got_small]

    outs = {}

    def update(name, stack, w, m, v):
        outs[name] = _adamw(stack, w[0], m[0], v[0], "adamw_" + name)

    def update_t(name, stack, w, m, v):
        res = _adamw(stack, w[0].T, m[0].T, v[0].T, "adamw_" + name)
        outs[name] = tuple(r.T for r in res)

    update_t("ffn1_w_gate", got[0], ffn1_w_gate, m_ffn1_w_gate, v_ffn1_w_gate)
    update_t("ffn1_w_up", got[1], ffn1_w_up, m_ffn1_w_up, v_ffn1_w_up)
    update("ffn1_w_down", got[2], ffn1_w_down, m_ffn1_w_down, v_ffn1_w_down)
    update("w_in", got[3], w_in, m_w_in, v_w_in)
    update("w_out", got[4], w_out, m_w_out, v_w_out)
    update_t("ffn2_w_gate", got[5], ffn2_w_gate, m_ffn2_w_gate, v_ffn2_w_gate)
    update_t("ffn2_w_up", got[6], ffn2_w_up, m_ffn2_w_up, v_ffn2_w_up)
    update("ffn2_w_down", got[7], ffn2_w_down, m_ffn2_w_down, v_ffn2_w_down)

    g_small = _from_sheet(_sum_stack(got[8], "sum_small"), spans)
    g_norm_ffn1 = _sum_stack(got_n1, "sum_dnorm_ffn1").reshape(norm_ffn1.shape)
    g_conv = lax.dynamic_slice_in_dim(g_small[3], me * conv_shard, conv_shard, axis=1)
    small_names = ["norm_ffn1", "norm_mix", "norm_ffn2", "norm_final", "conv_w", "pool_w", "pool_scale"]
    small_g = [g_norm_ffn1, g_small[0], g_small[1], g_small[2].reshape(norm_final.shape), g_conv[None],
               g_small[4][None], g_small[5]]
    small_w = [norm_ffn1, norm_mix, norm_ffn2, norm_final, conv_w, pool_w, pool_scale]
    small_m = [m_norm_ffn1, m_norm_mix, m_norm_ffn2, m_norm_final, m_conv_w, m_pool_w, m_pool_scale]
    small_v = [v_norm_ffn1, v_norm_mix, v_norm_ffn2, v_norm_final, v_conv_w, v_pool_w, v_pool_scale]
    g_sheet, spans_u = _to_sheet(small_g)
    w_sheet, _ = _to_sheet(small_w)
    m_sheet, _ = _to_sheet(small_m)
    v_sheet, _ = _to_sheet(small_v)
    upd = _adamw(g_sheet[None], w_sheet, m_sheet, v_sheet, "adamw_small")
    small_out = [_from_sheet(u, spans_u) for u in upd]
    for k, nm in enumerate(small_names):
        outs[nm] = tuple(small_out[j][k] for j in range(4))

    loss = lax.psum(loss_row[0, 0], ("x", "y", "c"))
    order = ["norm_ffn1", "ffn1_w_gate", "ffn1_w_up", "ffn1_w_down", "norm_mix", "w_in", "conv_w", "pool_w",
             "pool_scale", "w_out", "norm_ffn2", "ffn2_w_gate", "ffn2_w_up", "ffn2_w_down", "norm_final"]
    big = {"ffn1_w_gate", "ffn1_w_up", "ffn1_w_down", "w_in", "w_out", "ffn2_w_gate", "ffn2_w_up", "ffn2_w_down"}

    def leaf(nm, j):
        val = outs[nm][j]
        return val[None] if nm in big else val

    return (loss, dx0[None],
            *[leaf(nm, 0) for nm in order], *[leaf(nm, 1) for nm in order],
            *[leaf(nm, 2) for nm in order], *[leaf(nm, 3) for nm in order])
```

```python
import jax
import jax.numpy as jnp
from jax import lax
from jax.experimental import pallas as pl
from jax.experimental.pallas import tpu as pltpu

F32 = jnp.float32
LOW = jnp.bfloat16

N_DEV = 8
EPS = 1e-6
D_CONV = 512
POOL_WINDOWS = (2, 4, 8, 16)
POOL_GC = 128
HALO = 16
W_IN_SHARD = 256

ADAM_LR = 0.001
ADAM_B1 = 0.9
ADAM_B2 = 0.999
ADAM_EPS = 1e-08
ADAM_WD = 0.01
ADAM_STEP = 10

VMEM_LIMIT_BYTES = 56 * 1024 * 1024
TOK_TILE = 512
MIX_TOK_TILE = 256
WGRAD_TOK_TILE = 4096
FF_TILE_CANDIDATES = (2816, 1408, 256, 128)
WGRAD_ROW_CANDIDATES = (256, 128)
BWD_ROW_SLAB = 256


def _params(*sem):
    return pltpu.CompilerParams(dimension_semantics=sem, vmem_limit_bytes=VMEM_LIMIT_BYTES)


def _resident(arr):
    return pl.BlockSpec(arr.shape, lambda *_: (0,) * arr.ndim, pipeline_mode=pl.Buffered(1))


def _pick(n, candidates):
    for c in candidates:
        if n % c == 0:
            return c
    raise ValueError(f"no tile in {candidates} divides {n}")


def _dot(a, b):
    return lax.dot_general(a, b, (((1,), (0,)), ((), ())), preferred_element_type=F32)


def _dot_nt(a, b):
    return lax.dot_general(a, b, (((1,), (1,)), ((), ())), preferred_element_type=F32)


def _dot_tn(a, b):
    return lax.dot_general(a, b, (((0,), (0,)), ((), ())), preferred_element_type=F32)


def _rms_scale(x):
    return lax.rsqrt(jnp.mean(x * x, axis=-1, keepdims=True) + EPS)


def _rms_bwd(dy, x, g):
    r = _rms_scale(x)
    xhat = x * r
    gdy = dy * g
    dx = r * (gdy - xhat * jnp.mean(gdy * xhat, axis=-1, keepdims=True))
    return dx, jnp.sum(dy * xhat, axis=0, keepdims=True)


class _Exchange:
    CHIPS = (2, 4, 6)

    def __init__(self, arrays, sliced, relay_at=None, among_chips=False):
        assert relay_at is None or not any(sliced)
        assert not among_chips or (all(sliced) and relay_at is None)
        self.relay_at, self.among_chips = relay_at, among_chips
        self.peers = self.CHIPS if among_chips else ((1,) + self.CHIPS if relay_at is not None else tuple(range(1, N_DEV)))
        self.arrays, self.sliced, self.n = list(arrays), list(sliced), len(arrays)
        self.out_shape = [jax.ShapeDtypeStruct(arr.shape if sl else (N_DEV,) + arr.shape, arr.dtype)
                          for arr, sl in zip(arrays, sliced)]
        self.specs = [pl.BlockSpec(memory_space=pl.ANY)] * self.n
        self.scratch_shapes = [pltpu.SemaphoreType.DMA((self.n, N_DEV - 1)),
                               pltpu.SemaphoreType.DMA((self.n, N_DEV - 1)),
                               pltpu.SemaphoreType.DMA((self.n,))]

    def _copies(self, ins, outs, sems):
        send_sems, recv_sems, local_sems = sems
        sliced = self.sliced
        mx, my, mc = lax.axis_index("x"), lax.axis_index("y"), lax.axis_index("c")
        me = 2 * mx + my if self.among_chips else 4 * mx + 2 * my + mc

        def peer(m):
            px = lax.rem(mx + ((m >> 2) & 1), 2)
            py = lax.rem(my + ((m >> 1) & 1), 2)
            pc = lax.rem(mc + (m & 1), 2)
            return (px, py, pc), (2 * px + py if self.among_chips else 4 * px + 2 * py + pc)

        def remote(a, m, arriving):
            pid, pflat = peer(m)
            return pltpu.make_async_remote_copy(
                src_ref=ins[a].at[pflat] if sliced[a] else ins[a],
                dst_ref=outs[a].at[pflat if arriving else me],
                send_sem=send_sems.at[a, m - 1],
                recv_sem=recv_sems.at[a, m - 1],
                device_id=pid,
                device_id_type=pl.DeviceIdType.MESH,
            )

        def local(a):
            return pltpu.make_async_copy(ins[a].at[me] if sliced[a] else ins[a], outs[a].at[me], local_sems.at[a])

        def passed_on(a, m):
            _, origin = peer(m)
            sibling, _ = peer(1)
            return pltpu.make_async_remote_copy(
                src_ref=outs[a].at[origin],
                dst_ref=outs[a].at[origin],
                send_sem=send_sems.at[a, m],
                recv_sem=recv_sems.at[a, m],
                device_id=sibling,
                device_id_type=pl.DeviceIdType.MESH,
            )

        return remote, local, passed_on

    def start(self, ins, outs, sems):
        remote, local, _ = self._copies(ins, outs, sems)
        for a in range(self.n):
            local(a).start()
        for m in self.peers:
            for a in range(self.n):
                remote(a, m, False).start()

    def relay(self, ins, outs, sems):
        remote, _, passed_on = self._copies(ins, outs, sems)
        for m in self.CHIPS:
            for a in range(self.n):
                remote(a, m, True).wait_recv()
                passed_on(a, m).start()

    def wait(self, ins, outs, sems):
        remote, local, passed_on = self._copies(ins, outs, sems)
        if self.relay_at is None:
            for m in self.peers:
                for a in range(self.n):
                    remote(a, m, True).wait_recv()
            for m in self.peers:
                for a in range(self.n):
                    remote(a, m, False).wait_send()
        else:
            for m in (1,) + tuple(c + 1 for c in self.CHIPS):
                for a in range(self.n):
                    remote(a, m, True).wait_recv()
            for m in self.peers:
                for a in range(self.n):
                    remote(a, m, False).wait_send()
            for m in self.CHIPS:
                for a in range(self.n):
                    passed_on(a, m).wait_send()
        for a in range(self.n):
            local(a).wait()


def _pair_sum(stacks, name):
    n = len(stacks)
    n_chip = N_DEV // 2
    half = [(n_chip,) + st.shape[1:] for st in stacks]

    def body(*refs):
        ins, outs, mine, theirs = refs[:n], refs[n:2 * n], refs[2 * n:3 * n], refs[3 * n:4 * n]
        local_sems, send_sems, recv_sems = refs[4 * n:]
        mx, my, mc = lax.axis_index("x"), lax.axis_index("y"), lax.axis_index("c")

        def own(a, k):
            return pltpu.make_async_copy(ins[a].at[2 * k + mc], mine[a].at[k], local_sems.at[a, k])

        def swap(a, k):
            return pltpu.make_async_remote_copy(
                src_ref=ins[a].at[2 * k + (1 - mc)], dst_ref=theirs[a].at[k],
                send_sem=send_sems.at[a, k], recv_sem=recv_sems.at[a, k],
                device_id=(mx, my, 1 - mc), device_id_type=pl.DeviceIdType.MESH)

        for k in range(n_chip):
            for a in range(n):
                own(a, k).start()
                swap(a, k).start()
        for k in range(n_chip):
            for a in range(n):
                own(a, k).wait()
                swap(a, k).wait()
                outs[a][k] = (mine[a][k].astype(F32) + theirs[a][k].astype(F32)).astype(LOW)

    return pl.pallas_call(
        body, name=name,
        out_shape=[jax.ShapeDtypeStruct(h, LOW) for h in half],
        in_specs=[pl.BlockSpec(memory_space=pl.ANY)] * n,
        out_specs=[pl.BlockSpec(memory_space=pltpu.VMEM)] * n,
        scratch_shapes=([pltpu.VMEM(h, st.dtype) for h, st in zip(half, stacks)] * 2
                        + [pltpu.SemaphoreType.DMA((n, n_chip))] * 3),
        compiler_params=pltpu.CompilerParams(vmem_limit_bytes=VMEM_LIMIT_BYTES),
    )(*stacks)


def _exchange(arrays, sliced, name, relay=False):
    ex = _Exchange(arrays, sliced, relay_at=0 if relay else None)

    def body(*refs):
        ins, outs, sems = refs[:ex.n], refs[ex.n:2 * ex.n], refs[2 * ex.n:]
        ex.start(ins, outs, sems)
        if relay:
            ex.relay(ins, outs, sems)
        ex.wait(ins, outs, sems)

    return pl.pallas_call(body, name=name, out_shape=ex.out_shape, in_specs=ex.specs, out_specs=ex.specs,
                          scratch_shapes=ex.scratch_shapes)(*arrays)


def _call(body, *, name, grid, in_specs, out_specs, out_shape, args, scratch_shapes=(), exchange=None):
    params = _params(*(("arbitrary",) * len(grid)))
    if exchange is None:
        return pl.pallas_call(body, name=name, grid=grid, in_specs=in_specs, out_specs=out_specs, out_shape=out_shape,
                              scratch_shapes=list(scratch_shapes), compiler_params=params)(*args)
    exs = list(exchange) if isinstance(exchange, (list, tuple)) else [exchange]
    n_in, n_out, n_scr = len(in_specs), len(out_specs), len(scratch_shapes)
    n_ex = sum(ex.n for ex in exs)
    n_steps = 1
    for g in grid:
        n_steps *= g

    def hosted(*refs):
        ins, refs = refs[:n_in], refs[n_in:]
        ex_ins, refs = refs[:n_ex], refs[n_ex:]
        outs, refs = refs[:n_out], refs[n_out:]
        ex_outs, refs = refs[:n_ex], refs[n_ex:]
        scr, sems = refs[:n_scr], refs[n_scr:]
        parts, at = [], 0
        for j, ex in enumerate(exs):
            parts.append((ex_ins[at:at + ex.n], ex_outs[at:at + ex.n], sems[3 * j:3 * j + 3]))
            at += ex.n
        step = pl.program_id(0)
        for ax in range(1, len(grid)):
            step = step * grid[ax] + pl.program_id(ax)

        @pl.when(step == 0)
        def _():
            for ex, part in zip(exs, parts):
                ex.start(*part)

        body(*ins, *outs, *scr)

        for ex, part in zip(exs, parts):
            if ex.relay_at is not None:
                @pl.when(step == min(int(ex.relay_at * n_steps), n_steps - 1))
                def _():
                    ex.relay(*part)

        @pl.when(step == n_steps - 1)
        def _():
            for ex, part in zip(exs, parts):
                ex.wait(*part)

    return pl.pallas_call(
        hosted, name=name, grid=grid,
        in_specs=list(in_specs) + [sp for ex in exs for sp in ex.specs],
        out_specs=list(out_specs) + [sp for ex in exs for sp in ex.specs],
        out_shape=list(out_shape) + [sh for ex in exs for sh in ex.out_shape],
        scratch_shapes=list(scratch_shapes) + [sc for ex in exs for sc in ex.scratch_shapes],
        compiler_params=params)(*args, *[arr for ex in exs for arr in ex.arrays])


def _ffn_fwd(x, g, wgt, wut, wd, name, exchange=None):
    s_len, d = x.shape
    f_len = wd.shape[0]
    ts = min(TOK_TILE, s_len)
    fc = _pick(f_len, FF_TILE_CANDIDATES)

    def body(x_ref, g_ref, wg_ref, wu_ref, wd_ref, xo_ref, h_ref, a_ref, b_ref, s_ref):
        @pl.when(pl.program_id(1) == 0)
        def _():
            xv = x_ref[...]
            h_ref[...] = (xv * _rms_scale(xv) * g_ref[...]).astype(LOW)
            xo_ref[...] = xv

        hb = h_ref[...]
        for c0, c1 in _slabs(fc):
            a = _dot_nt(hb, wg_ref[c0:c1, :])
            b = _dot_nt(hb, wu_ref[c0:c1, :])
            s_ref[:, c0:c1] = (a * jax.nn.sigmoid(a) * b).astype(LOW)
            a_ref[:, c0:c1] = a.astype(LOW)
            b_ref[:, c0:c1] = b.astype(LOW)
        xo_ref[...] += 0.5 * _dot(s_ref[...], wd_ref[...])

    tok = pl.BlockSpec((ts, d), lambda t, f: (t, 0))
    wspec = _resident(wd) if fc == f_len else pl.BlockSpec((fc, d), lambda t, f: (f, 0))
    hid = pl.BlockSpec((ts, fc), lambda t, f: (t, f))
    return _call(
        body,
        name=name,
        grid=(s_len // ts, f_len // fc),
        in_specs=[tok, pl.BlockSpec((1, d), lambda t, f: (0, 0)), wspec, wspec, wspec],
        out_specs=[tok, tok, hid, hid],
        out_shape=[
            jax.ShapeDtypeStruct((s_len, d), F32),
            jax.ShapeDtypeStruct((s_len, d), LOW),
            jax.ShapeDtypeStruct((s_len, f_len), LOW),
            jax.ShapeDtypeStruct((s_len, f_len), LOW),
        ],
        scratch_shapes=[pltpu.VMEM((ts, fc), LOW)],
        args=(x, g, wgt, wut, wd),
        exchange=exchange,
    )


def _slabs(width, slab=256):
    return [(c0, min(c0 + slab, width)) for c0 in range(0, width, slab)]


def _ffn_bwdw(df, a, b, h, wd, name, exchange=None):
    s_len, d = df.shape
    f_len = wd.shape[0]
    tm = _pick(f_len, WGRAD_ROW_CANDIDATES)
    tk = min(WGRAD_TOK_TILE, s_len)
    n_k = s_len // tk

    def body(df_ref, a_ref, b_ref, h_ref, wd_ref, da_ref, db_ref, dwd_ref, dwg_ref, dwu_ref,
             s_ref, acc_d, acc_g, acc_u):
        k = pl.program_id(1)

        @pl.when(k == 0)
        def _():
            acc_d[...] = jnp.zeros_like(acc_d)
            acc_g[...] = jnp.zeros_like(acc_g)
            acc_u[...] = jnp.zeros_like(acc_u)

        wdv = wd_ref[...]
        for r0, r1 in _slabs(tk, BWD_ROW_SLAB):
            ds = _dot_nt(df_ref[r0:r1, :], wdv)
            av = a_ref[r0:r1, :].astype(F32)
            bv = b_ref[r0:r1, :].astype(F32)
            sig = jax.nn.sigmoid(av)
            silu = av * sig
            s_ref[r0:r1, :] = (silu * bv).astype(LOW)
            da_ref[r0:r1, :] = (ds * bv * (sig * (1.0 + av * (1.0 - sig)))).astype(LOW)
            db_ref[r0:r1, :] = (ds * silu).astype(LOW)
        hv = h_ref[...]
        acc_d[...] += _dot_tn(s_ref[...], df_ref[...])
        acc_g[...] += _dot_tn(da_ref[...], hv)
        acc_u[...] += _dot_tn(db_ref[...], hv)

        @pl.when(k == n_k - 1)
        def _():
            dwd_ref[...] = acc_d[...].astype(LOW)
            dwg_ref[...] = acc_g[...].astype(LOW)
            dwu_ref[...] = acc_u[...].astype(LOW)

    hid = pl.BlockSpec((tk, tm), lambda i, k: (k, i))
    tok = pl.BlockSpec((tk, d), lambda i, k: (k, 0))
    wrow = pl.BlockSpec((tm, d), lambda i, k: (i, 0))
    return _call(
        body,
        name=name,
        grid=(f_len // tm, n_k),
        in_specs=[tok, hid, hid, tok, wrow],
        out_specs=[hid, hid, wrow, wrow, wrow],
        out_shape=[jax.ShapeDtypeStruct((s_len, f_len), LOW)] * 2 + [jax.ShapeDtypeStruct((f_len, d), LOW)] * 3,
        scratch_shapes=[pltpu.VMEM((tk, tm), LOW)] + [pltpu.VMEM((tm, d), F32)] * 3,
        args=(df, a, b, h, wd),
        exchange=exchange,
    )


def _ffn_dx(dxo, x, g, da, db, wgt, wut, name, exchange=None):
    s_len, d = x.shape
    f_len = wgt.shape[0]
    ts = min(TOK_TILE, s_len)
    fc = _pick(f_len, FF_TILE_CANDIDATES)
    n_f = f_len // fc

    def body(dxo_ref, x_ref, g_ref, da_ref, db_ref, wg_ref, wu_ref, dxi_ref, dg_ref):
        t, f = pl.program_id(0), pl.program_id(1)

        @pl.when((t == 0) & (f == 0))
        def _():
            dg_ref[...] = jnp.zeros_like(dg_ref)

        @pl.when(f == 0)
        def _():
            dxi_ref[...] = jnp.zeros_like(dxi_ref)

        dxi_ref[...] += _dot(da_ref[...], wg_ref[...]) + _dot(db_ref[...], wu_ref[...])

        @pl.when(f == n_f - 1)
        def _():
            dx, dgp = _rms_bwd(dxi_ref[...], x_ref[...], g_ref[...])
            dxi_ref[...] = dxo_ref[...] + dx
            dg_ref[...] += dgp

    tok = pl.BlockSpec((ts, d), lambda t, f: (t, 0))
    vec = pl.BlockSpec((1, d), lambda t, f: (0, 0))
    wspec = _resident(wgt) if fc == f_len else pl.BlockSpec((fc, d), lambda t, f: (f, 0))
    hid = pl.BlockSpec((ts, fc), lambda t, f: (t, f))
    return _call(
        body,
        name=name,
        grid=(s_len // ts, n_f),
        in_specs=[tok, tok, vec, hid, hid, wspec, wspec],
        out_specs=[tok, vec],
        out_shape=[jax.ShapeDtypeStruct((s_len, d), F32), jax.ShapeDtypeStruct((1, d), F32)],
        args=(dxo, x, g, da, db, wgt, wut),
        exchange=exchange,
    )


def _wgrad_tn(xm, ym, tn, stacked, name):
    s_len, m = xm.shape
    n = ym.shape[1]
    tk = min(WGRAD_TOK_TILE, s_len)
    n_k = s_len // tk

    def body(x_ref, y_ref, o_ref, acc):
        k = pl.program_id(1)

        @pl.when(k == 0)
        def _():
            acc[...] = jnp.zeros_like(acc)

        acc[...] += _dot_tn(x_ref[...].astype(LOW), y_ref[...].astype(LOW))

        @pl.when(k == n_k - 1)
        def _():
            o_ref[...] = acc[...].astype(LOW)

    if stacked:
        out_spec = pl.BlockSpec((None, m, tn), lambda j, k: (j, 0, 0))
        out_shape = jax.ShapeDtypeStruct((n // tn, m, tn), LOW)
    else:
        out_spec = pl.BlockSpec((m, tn), lambda j, k: (0, j))
        out_shape = jax.ShapeDtypeStruct((m, n), LOW)
    return pl.pallas_call(
        body,
        name=name,
        grid=(n // tn, n_k),
        in_specs=[pl.BlockSpec((tk, m), lambda j, k: (k, 0)), pl.BlockSpec((tk, tn), lambda j, k: (k, j))],
        out_specs=out_spec,
        out_shape=out_shape,
        scratch_shapes=[pltpu.VMEM((m, tn), F32)],
        compiler_params=_params("arbitrary", "arbitrary"),
    )(xm, ym)


def _mix_parts(ext_ref, cw, ts, row0):
    dc = D_CONV

    def back(off, c0, c1):
        return ext_ref[HALO - off:HALO - off + ts, c0:c1]

    v, gate_b, gate_c = back(0, 0, dc), back(0, dc, 2 * dc), back(0, 2 * dc, 3 * dc)
    z0 = gate_c * v
    z1 = back(1, 2 * dc, 3 * dc) * back(1, 0, dc)
    z2 = back(2, 2 * dc, 3 * dc) * back(2, 0, dc)
    conv = cw[2:3, :] * z0 + cw[1:2, :] * z1 + cw[0:1, :] * z2
    rows = row0 + lax.broadcasted_iota(jnp.int32, (ts, 1), 0)
    pooled, inv_count = [], []
    for grp, w in enumerate(POOL_WINDOWS):
        c0 = 3 * dc + POOL_GC * grp
        u = back(0, c0, c0 + POOL_GC)
        acc = u
        for j in range(1, w):
            acc = acc + back(j, c0, c0 + POOL_GC)
        inv = 1.0 / jnp.minimum(rows + 1, w).astype(F32)
        pooled.append(acc * inv - u)
        inv_count.append(inv)
    return v, gate_b, gate_c, z0, z1, z2, conv, pooled, inv_count


def _mix_fwd(x, g, w_in, conv_w, pool_w, pool_scale, w_out, name, exchange=None):
    s_len, d = x.shape
    n_blk, _, wcols = w_in.shape
    p_len = n_blk * wcols
    d_mix = w_out.shape[0]
    ts = min(MIX_TOK_TILE, s_len)
    dc = D_CONV

    def body(x_ref, g_ref, win_ref, cw_ref, pw_ref, ps_ref, wout_ref, x2_ref, h_ref, proj_ref, ext_ref, cat_ref):
        t = pl.program_id(0)

        @pl.when(t == 0)
        def _():
            ext_ref[0:HALO, :] = jnp.zeros((HALO, p_len), F32)

        xv = x_ref[...]
        hb = (xv * _rms_scale(xv) * g_ref[...]).astype(LOW)
        h_ref[...] = hb
        for k in range(n_blk):
            ext_ref[HALO:HALO + ts, wcols * k:wcols * (k + 1)] = _dot(hb, win_ref[k])
        proj_ref[...] = ext_ref[HALO:HALO + ts, :]

        _, gate_b, _, _, _, _, conv, pooled, _ = _mix_parts(ext_ref, cw_ref[...], ts, t * ts)
        cat_ref[:, 0:dc] = (gate_b * conv).astype(LOW)
        for grp in range(len(POOL_WINDOWS)):
            c0 = POOL_GC * grp
            lin = _dot(pooled[grp].astype(LOW), pw_ref[grp])
            cat_ref[:, dc + c0:dc + c0 + POOL_GC] = (lin * ps_ref[:, c0:c0 + POOL_GC]).astype(LOW)
        x2_ref[...] = xv + _dot(cat_ref[...], wout_ref[...])
        ext_ref[0:HALO, :] = ext_ref[ts:ts + HALO, :]

    tok = pl.BlockSpec((ts, d), lambda t: (t, 0))

    def whole(arr):
        return pl.BlockSpec(arr.shape, lambda t: (0,) * arr.ndim)

    return _call(
        body,
        name=name,
        grid=(s_len // ts,),
        in_specs=[tok, whole(g), _resident(w_in), whole(conv_w), whole(pool_w), whole(pool_scale), _resident(w_out)],
        out_specs=[tok, tok, pl.BlockSpec((ts, p_len), lambda t: (t, 0))],
        out_shape=[
            jax.ShapeDtypeStruct((s_len, d), F32),
            jax.ShapeDtypeStruct((s_len, d), LOW),
            jax.ShapeDtypeStruct((s_len, p_len), F32),
        ],
        scratch_shapes=[pltpu.VMEM((ts + HALO, p_len), F32), pltpu.VMEM((ts, d_mix), LOW)],
        args=(x, g, w_in, conv_w, pool_w, pool_scale, w_out),
        exchange=exchange,
    )


def _mix_bwd(dx2, x, proj, g, w_in, conv_w, pool_w, pool_scale, w_out, name, exchange=None):
    s_len, d = x.shape
    n_blk, _, wcols = w_in.shape
    p_len = n_blk * wcols
    d_mix = w_out.shape[0]
    ts = min(MIX_TOK_TILE, s_len)
    n_t = s_len // ts
    dc = D_CONV
    n_grp = len(POOL_WINDOWS)

    def body(dx2_ref, x_ref, proj_ref, halo_ref, g_ref, win_ref, cw_ref, pw_ref, ps_ref, wout_ref,
             dx_ref, dproj_ref, cat_ref, dg_ref, dcw_ref, dpw_ref, dps_ref, df_ref, ext_ref, fut_ref):
        i = pl.program_id(0)
        t = n_t - 1 - i

        @pl.when(i == 0)
        def _():
            dg_ref[...] = jnp.zeros_like(dg_ref)
            dcw_ref[...] = jnp.zeros_like(dcw_ref)
            dpw_ref[...] = jnp.zeros_like(dpw_ref)
            dps_ref[...] = jnp.zeros_like(dps_ref)
            fut_ref[ts:ts + HALO, :] = jnp.zeros((HALO, d_mix), F32)

        ext_ref[HALO:HALO + ts, :] = proj_ref[...]

        @pl.when(t == 0)
        def _():
            ext_ref[0:HALO, :] = jnp.zeros((HALO, p_len), F32)

        @pl.when(t > 0)
        def _():
            ext_ref[0:HALO, :] = halo_ref[...]

        cw = cw_ref[...]
        v, gate_b, gate_c, z0, z1, z2, conv, pooled, inv_count = _mix_parts(ext_ref, cw, ts, t * ts)
        dx2 = dx2_ref[...]
        dcat = _dot_nt(dx2.astype(LOW), wout_ref[...])

        dy_a = dcat[:, 0:dc]
        dconv = dy_a * gate_b
        fut_ref[0:ts, 0:dc] = dconv
        cat_ref[:, 0:dc] = (gate_b * conv).astype(LOW)
        dproj_ref[:, dc:2 * dc] = (dy_a * conv).astype(LOW)
        dcw_ref[2:3, :] += jnp.sum(dconv * z0, axis=0, keepdims=True)
        dcw_ref[1:2, :] += jnp.sum(dconv * z1, axis=0, keepdims=True)
        dcw_ref[0:1, :] += jnp.sum(dconv * z2, axis=0, keepdims=True)

        dpool = []
        for grp in range(n_grp):
            c0 = POOL_GC * grp
            pooled_b = pooled[grp].astype(LOW)
            lin = _dot(pooled_b, pw_ref[grp])
            dy_b = dcat[:, dc + c0:dc + c0 + POOL_GC]
            scale = ps_ref[:, c0:c0 + POOL_GC]
            cat_ref[:, dc + c0:dc + c0 + POOL_GC] = (lin * scale).astype(LOW)
            dps_ref[:, c0:c0 + POOL_GC] += jnp.sum(dy_b * lin, axis=0, keepdims=True)
            dlin = (dy_b * scale).astype(LOW)
            dpw_ref[grp] += _dot_tn(pooled_b, dlin)
            dpool.append(_dot_nt(dlin, pw_ref[grp]))
            fut_ref[0:ts, dc + c0:dc + c0 + POOL_GC] = dpool[grp] * inv_count[grp]

        def ahead(off, c0, c1):
            return fut_ref[off:off + ts, c0:c1]

        dz = cw[2:3, :] * ahead(0, 0, dc) + cw[1:2, :] * ahead(1, 0, dc) + cw[0:1, :] * ahead(2, 0, dc)
        dproj_ref[:, 0:dc] = (dz * gate_c).astype(LOW)
        dproj_ref[:, 2 * dc:3 * dc] = (dz * v).astype(LOW)
        for grp, w in enumerate(POOL_WINDOWS):
            c0 = dc + POOL_GC * grp
            acc = ahead(0, c0, c0 + POOL_GC)
            for j in range(1, w):
                acc = acc + ahead(j, c0, c0 + POOL_GC)
            dproj_ref[:, 2 * dc + c0:2 * dc + c0 + POOL_GC] = (acc - dpool[grp]).astype(LOW)

        dh = _dot_nt(dproj_ref[:, 0:wcols], win_ref[0])
        for k in range(1, n_blk):
            dh += _dot_nt(dproj_ref[:, wcols * k:wcols * (k + 1)], win_ref[k])
        dx, dgp = _rms_bwd(dh, x_ref[...], g_ref[...])
        dx = dx2 + dx
        dx_ref[...] = dx
        df_ref[...] = (0.5 * dx).astype(LOW)
        dg_ref[...] += dgp
        fut_ref[ts:ts + HALO, :] = fut_ref[0:HALO, :]

    tok = pl.BlockSpec((ts, d), lambda i: (n_t - 1 - i, 0))
    halo = pl.BlockSpec((HALO, p_len), lambda i: (jnp.maximum((n_t - 1 - i) * (ts // HALO) - 1, 0), 0))

    def whole(arr):
        return pl.BlockSpec(arr.shape, lambda i: (0,) * arr.ndim)

    return _call(
        body,
        name=name,
        grid=(n_t,),
        in_specs=[tok, tok, pl.BlockSpec((ts, p_len), lambda i: (n_t - 1 - i, 0)), halo,
                  whole(g), _resident(w_in), whole(conv_w), whole(pool_w), whole(pool_scale), _resident(w_out)],
        out_specs=[tok, pl.BlockSpec((ts, p_len), lambda i: (n_t - 1 - i, 0)),
                   pl.BlockSpec((ts, d_mix), lambda i: (n_t - 1 - i, 0)),
                   whole(g), whole(conv_w), whole(pool_w), whole(pool_scale), tok],
        out_shape=[
            jax.ShapeDtypeStruct((s_len, d), F32),
            jax.ShapeDtypeStruct((s_len, p_len), LOW),
            jax.ShapeDtypeStruct((s_len, d_mix), LOW),
            jax.ShapeDtypeStruct(g.shape, F32),
            jax.ShapeDtypeStruct(conv_w.shape, F32),
            jax.ShapeDtypeStruct(pool_w.shape, F32),
            jax.ShapeDtypeStruct(pool_scale.shape, F32),
            jax.ShapeDtypeStruct((s_len, d), LOW),
        ],
        scratch_shapes=[pltpu.VMEM((ts + HALO, p_len), F32), pltpu.VMEM((ts + HALO, d_mix), F32)],
        args=(dx2, x, proj, proj, g, w_in, conv_w, pool_w, pool_scale, w_out),
        exchange=exchange,
    )


def _loss_head(x, g, target, name):
    s_len, d = x.shape
    ts = min(TOK_TILE, s_len)

    def body(x_ref, g_ref, tgt_ref, loss_ref, dx_ref, dg_ref, df_ref):
        @pl.when(pl.program_id(0) == 0)
        def _():
            loss_ref[...] = jnp.zeros_like(loss_ref)
            dg_ref[...] = jnp.zeros_like(dg_ref)

        xv, gv = x_ref[...], g_ref[...]
        err = xv * _rms_scale(xv) * gv - tgt_ref[...]
        loss_ref[...] += 0.5 * jnp.sum(jnp.mean(err * err, axis=-1, keepdims=True), axis=0, keepdims=True)
        dx, dgp = _rms_bwd(err * (1.0 / d), xv, gv)
        dx_ref[...] = dx
        df_ref[...] = (0.5 * dx).astype(LOW)
        dg_ref[...] += dgp

    tok = pl.BlockSpec((ts, d), lambda t: (t, 0))
    vec = pl.BlockSpec((1, d), lambda t: (0, 0))
    return pl.pallas_call(
        body,
        name=name,
        grid=(s_len // ts,),
        in_specs=[tok, vec, tok],
        out_specs=[pl.BlockSpec((1, 128), lambda t: (0, 0)), tok, vec, tok],
        out_shape=[
            jax.ShapeDtypeStruct((1, 128), F32),
            jax.ShapeDtypeStruct((s_len, d), F32),
            jax.ShapeDtypeStruct((1, d), F32),
            jax.ShapeDtypeStruct((s_len, d), LOW),
        ],
        compiler_params=_params("arbitrary"),
    )(x, g, target)


def _row_tile(rows, cols, stack_bytes):
    budget = 20 * 1024 * 1024
    per_row = cols * (4 * 7 + stack_bytes)
    for tr in (rows, 512, 256, 176, 128, 64, 32, 16, 8):
        if rows % tr == 0 and tr % 8 == 0 and tr * per_row * 2 <= budget:
            return tr
    return rows


def _sum_stack(stack, name):
    n, r, c = stack.shape
    tr = _row_tile(r, c, n * stack.dtype.itemsize)

    def body(s_ref, o_ref):
        acc = s_ref[0].astype(F32)
        for k in range(1, n):
            acc = acc + s_ref[k].astype(F32)
        o_ref[...] = acc

    return pl.pallas_call(
        body,
        name=name,
        grid=(r // tr,),
        in_specs=[pl.BlockSpec((n, tr, c), lambda i: (0, i, 0))],
        out_specs=pl.BlockSpec((tr, c), lambda i: (i, 0)),
        out_shape=jax.ShapeDtypeStruct((r, c), F32),
        compiler_params=_params("arbitrary"),
    )(stack)


def _adamw(stack, w, m, v, name):
    n, r, c = stack.shape
    tr = _row_tile(r, c, n * stack.dtype.itemsize)
    c1 = 1.0 - ADAM_B1 ** ADAM_STEP
    c2 = 1.0 - ADAM_B2 ** ADAM_STEP

    def body(s_ref, w_ref, m_ref, v_ref, g_ref, d_ref, mo_ref, vo_ref):
        gv = s_ref[0].astype(F32)
        for k in range(1, n):
            gv = gv + s_ref[k].astype(F32)
        mn = ADAM_B1 * m_ref[...] + (1.0 - ADAM_B1) * gv
        vn = ADAM_B2 * v_ref[...] + (1.0 - ADAM_B2) * (gv * gv)
        g_ref[...] = gv
        mo_ref[...] = mn
        vo_ref[...] = vn
        d_ref[...] = -ADAM_LR * ((mn / c1) / (jnp.sqrt(vn / c2) + ADAM_EPS) + ADAM_WD * w_ref[...])

    blk = pl.BlockSpec((tr, c), lambda i: (i, 0))
    return pl.pallas_call(
        body,
        name=name,
        grid=(r // tr,),
        in_specs=[pl.BlockSpec((n, tr, c), lambda i: (0, i, 0)), blk, blk, blk],
        out_specs=[blk] * 4,
        out_shape=[jax.ShapeDtypeStruct((r, c), F32)] * 4,
        compiler_params=_params("arbitrary"),
    )(stack, w, m, v)


def _to_sheet(parts):
    sheets, spans = [], []
    row = 0
    for p in parts:
        flat = p.reshape(-1).astype(F32)
        rows = -(-flat.shape[0] // 1024) * 8
        flat = jnp.pad(flat, (0, rows * 128 - flat.shape[0]))
        sheets.append(flat.reshape(rows, 128))
        spans.append((row, p.size, p.shape))
        row += rows
    return jnp.concatenate(sheets, axis=0), spans


def _from_sheet(sheet, spans):
    out = []
    for row, size, shape in spans:
        rows = -(-size // 1024) * 8
        out.append(sheet[row:row + rows].reshape(-1)[:size].reshape(shape))
    return out


def kernel(x, norm_ffn1, ffn1_w_gate, ffn1_w_up, ffn1_w_down, norm_mix, w_in, conv_w, pool_w, pool_scale, w_out, norm_ffn2, ffn2_w_gate, ffn2_w_up, ffn2_w_down, norm_final, loss_target, m_norm_ffn1, m_ffn1_w_gate, m_ffn1_w_up, m_ffn1_w_down, m_norm_mix, m_w_in, m_conv_w, m_pool_w, m_pool_scale, m_w_out, m_norm_ffn2, m_ffn2_w_gate, m_ffn2_w_up, m_ffn2_w_down, m_norm_final, v_norm_ffn1, v_ffn1_w_gate, v_ffn1_w_up, v_ffn1_w_down, v_norm_mix, v_w_in, v_conv_w, v_pool_w, v_pool_scale, v_w_out, v_norm_ffn2, v_ffn2_w_gate, v_ffn2_w_up, v_ffn2_w_down, v_norm_final):
    me = 4 * lax.axis_index("x") + 2 * lax.axis_index("y") + lax.axis_index("c")
    xs, tgt = x[0], loss_target[0]
    s_len, d = xs.shape
    f_shard = ffn1_w_down.shape[1]
    f_len = N_DEV * f_shard
    conv_shard = conv_w.shape[2]

    def low_t(wt):
        return wt[0].T.astype(LOW)

    def by_dev(gw):
        return gw.reshape(N_DEV, -1, d)

    conv_tile = jnp.zeros((8, 128), F32).at[0:conv_w.shape[1], 0:conv_shard].set(conv_w[0])
    full = _exchange([low_t(ffn1_w_gate), low_t(ffn1_w_up), ffn1_w_down[0].astype(LOW)], [False] * 3, "gather_ffn1",
                     relay=True)
    wg1, wu1, wd1 = (w.reshape(f_len, d) for w in full)
    pool_w_low = pool_w[0].astype(LOW)

    later = _Exchange([w_in[0].astype(LOW), w_out[0].astype(LOW), conv_tile, ffn2_w_down[0].astype(LOW)],
                      [False] * 4, relay_at=0.7)
    x1, h1, a1, b1, w_in_full, w_out_full, conv_tiles, wd2 = _ffn_fwd(
        xs, norm_ffn1, wg1, wu1, wd1, "ffn1_fwd", exchange=later)
    w_out_full, wd2 = w_out_full.reshape(-1, d), wd2.reshape(f_len, d)
    conv_full = jnp.concatenate([conv_tiles[k, 0:conv_w.shape[1], 0:conv_shard] for k in range(N_DEV)], axis=1)
    later = _Exchange([low_t(ffn2_w_gate), low_t(ffn2_w_up)], [False] * 2, relay_at=0.85)
    x2, h2, proj, wg2, wu2 = _mix_fwd(
        x1, norm_mix, w_in_full, conv_full, pool_w_low, pool_scale, w_out_full, "mix_fwd", exchange=later)
    wg2, wu2 = wg2.reshape(f_len, d), wu2.reshape(f_len, d)
    x3, h3, a2, b2 = _ffn_fwd(x2, norm_ffn2, wg2, wu2, wd2, "ffn2_fwd")
    loss_row, dx3, dg_final, df3 = _loss_head(x3, norm_final.reshape(1, d), tgt, "loss_head")

    da, db, dwd2, dwg2, dwu2 = _ffn_bwdw(df3, a2, b2, h3, wd2, "ffn2_bwdw")
    pair2 = _pair_sum([by_dev(dwg2), by_dev(dwu2), by_dev(dwd2)], "pair_sum_ffn2")
    earlier = _Exchange(pair2[:2], [True] * 2, among_chips=True)
    dx2, dg_ffn2, got_g2, got_u2 = _ffn_dx(dx3, x2, norm_ffn2, da, db, wg2, wu2, "ffn2_dx", exchange=earlier)
    earlier = _Exchange(pair2[2:], [True], among_chips=True)
    dx1, dproj, cat, dg_mix, dconv, dpool_w, dpool_scale, df1, got_d2 = _mix_bwd(
        dx2, x1, proj, norm_mix, w_in_full, conv_full, pool_w_low, pool_scale, w_out_full, "mix_bwd",
        exchange=earlier)
    dw_in = _wgrad_tn(h2, dproj, W_IN_SHARD, True, "w_in_wgrad")
    dw_out = _wgrad_tn(cat, dx2, d, False, "w_out_wgrad")
    small_parts = [dg_mix, dg_ffn2, dg_final, dconv, dpool_w, dpool_scale]
    small_sheet, spans = _to_sheet(small_parts)
    earlier = _Exchange([dw_in, by_dev(dw_out), small_sheet], [True, True, False])
    da, db, dwd1, dwg1, dwu1, got_in, got_out, got_small = _ffn_bwdw(
        df1, a1, b1, h1, wd1, "ffn1_bwdw", exchange=earlier)
    pair1 = _pair_sum([by_dev(dwg1), by_dev(dwu1), by_dev(dwd1)], "pair_sum_ffn1")
    earlier = _Exchange(pair1, [True] * 3, among_chips=True)
    dx0, dg_ffn1, got_g1, got_u1, got_d1 = _ffn_dx(
        dx1, xs, norm_ffn1, da, db, wg1, wu1, "ffn1_dx", exchange=earlier)
    (got_n1,) = _exchange([dg_ffn1.reshape(8, 128)], [False], "gather_dnorm_ffn1")
    got = [got_g1, got_u1, got_d1, got_in, got_out, got_g2, got_u2, got_d2, got_small]

    outs = {}

    def update(name, stack, w, m, v):
        outs[name] = _adamw(stack, w[0], m[0], v[0], "adamw_" + name)

    def update_t(name, stack, w, m, v):
        res = _adamw(stack, w[0].T, m[0].T, v[0].T, "adamw_" + name)
        outs[name] = tuple(r.T for r in res)

    update_t("ffn1_w_gate", got[0], ffn1_w_gate, m_ffn1_w_gate, v_ffn1_w_gate)
    update_t("ffn1_w_up", got[1], ffn1_w_up, m_ffn1_w_up, v_ffn1_w_up)
    update("ffn1_w_down", got[2], ffn1_w_down, m_ffn1_w_down, v_ffn1_w_down)
    update("w_in", got[3], w_in, m_w_in, v_w_in)
    update("w_out", got[4], w_out, m_w_out, v_w_out)
    update_t("ffn2_w_gate", got[5], ffn2_w_gate, m_ffn2_w_gate, v_ffn2_w_gate)
    update_t("ffn2_w_up", got[6], ffn2_w_up, m_ffn2_w_up, v_ffn2_w_up)
    update("ffn2_w_down", got[7], ffn2_w_down, m_ffn2_w_down, v_ffn2_w_down)

    g_small = _from_sheet(_sum_stack(got[8], "sum_small"), spans)
    g_norm_ffn1 = _sum_stack(got_n1, "sum_dnorm_ffn1").reshape(norm_ffn1.shape)
    g_conv = lax.dynamic_slice_in_dim(g_small[3], me * conv_shard, conv_shard, axis=1)
    small_names = ["norm_ffn1", "norm_mix", "norm_ffn2", "norm_final", "conv_w", "pool_w", "pool_scale"]
    small_g = [g_norm_ffn1, g_small[0], g_small[1], g_small[2].reshape(norm_final.shape), g_conv[None],
               g_small[4][None], g_small[5]]
    small_w = [norm_ffn1, norm_mix, norm_ffn2, norm_final, conv_w, pool_w, pool_scale]
    small_m = [m_norm_ffn1, m_norm_mix, m_norm_ffn2, m_norm_final, m_conv_w, m_pool_w, m_pool_scale]
    small_v = [v_norm_ffn1, v_norm_mix, v_norm_ffn2, v_norm_final, v_conv_w, v_pool_w, v_pool_scale]
    g_sheet, spans_u = _to_sheet(small_g)
    w_sheet, _ = _to_sheet(small_w)
    m_sheet, _ = _to_sheet(small_m)
    v_sheet, _ = _to_sheet(small_v)
    upd = _adamw(g_sheet[None], w_sheet, m_sheet, v_sheet, "adamw_small")
    small_out = [_from_sheet(u, spans_u) for u in upd]
    for k, nm in enumerate(small_names):
        outs[nm] = tuple(small_out[j][k] for j in range(4))

    loss = lax.psum(loss_row[0, 0], ("x", "y", "c"))
    order = ["norm_ffn1", "ffn1_w_gate", "ffn1_w_up", "ffn1_w_down", "norm_mix", "w_in", "conv_w", "pool_w",
             "pool_scale", "w_out", "norm_ffn2", "ffn2_w_gate", "ffn2_w_up", "ffn2_w_down", "norm_final"]
    big = {"ffn1_w_gate", "ffn1_w_up", "ffn1_w_down", "w_in", "w_out", "ffn2_w_gate", "ffn2_w_up", "ffn2_w_down"}

    def leaf(nm, j):
        val = outs[nm][j]
        return val[None] if nm in big else val

    return (loss, dx0[None],
            *[leaf(nm, 0) for nm in order], *[leaf(nm, 1) for nm in order],
            *[leaf(nm, 2) for nm in order], *[leaf(nm, 3) for nm in order])
```

```python
import math

import jax
import jax.numpy as jnp
from jax import lax
from jax.experimental import pallas as pl
from jax.experimental.pallas import tpu as pltpu

F32 = jnp.float32
LOW = jnp.bfloat16

N_DEV = 8
EPS = 1e-6
D_CONV = 512
POOL_WINDOWS = (2, 4, 8, 16)
POOL_GC = 128
HALO = 16
W_IN_SHARD = 256

ADAM_LR = 0.001
ADAM_B1 = 0.9
ADAM_B2 = 0.999
ADAM_EPS = 1e-08
ADAM_WD = 0.01
ADAM_STEP = 10

VMEM_LIMIT_BYTES = 56 * 1024 * 1024
TOK_TILE = 512
MIX_TOK_TILE = 256
WGRAD_TOK_TILE = 4096
WGRAD_ROW_CANDIDATES = (256, 128)
BWD_ROW_SLAB = 256


def _params(*sem):
    return pltpu.CompilerParams(dimension_semantics=sem, vmem_limit_bytes=VMEM_LIMIT_BYTES)


def _resident(arr):
    return pl.BlockSpec(arr.shape, lambda *_: (0,) * arr.ndim, pipeline_mode=pl.Buffered(1))


def _pick(n, candidates):
    for c in candidates:
        if n % c == 0:
            return c
    raise ValueError(f"no tile in {candidates} divides {n}")


def _dot(a, b):
    return lax.dot_general(a, b, (((1,), (0,)), ((), ())), preferred_element_type=F32)


def _dot_nt(a, b):
    return lax.dot_general(a, b, (((1,), (1,)), ((), ())), preferred_element_type=F32)


def _dot_tn(a, b):
    return lax.dot_general(a, b, (((0,), (0,)), ((), ())), preferred_element_type=F32)


def _rms_scale(x):
    return lax.rsqrt(jnp.mean(x * x, axis=-1, keepdims=True) + EPS)


def _rms_bwd(dy, x, g):
    r = _rms_scale(x)
    xhat = x * r
    gdy = dy * g
    dx = r * (gdy - xhat * jnp.mean(gdy * xhat, axis=-1, keepdims=True))
    return dx, jnp.sum(dy * xhat, axis=0, keepdims=True)


class _Exchange:
    CHIPS = (2, 4, 6)

    def __init__(self, arrays, sliced, relay_at=None, among_chips=False):
        assert relay_at is None or not any(sliced)
        assert not among_chips or (all(sliced) and relay_at is None)
        self.relay_at, self.among_chips = relay_at, among_chips
        self.peers = self.CHIPS if among_chips else ((1,) + self.CHIPS if relay_at is not None else tuple(range(1, N_DEV)))
        self.arrays, self.sliced, self.n = list(arrays), list(sliced), len(arrays)
        self.out_shape = [jax.ShapeDtypeStruct(arr.shape if sl else (N_DEV,) + arr.shape, arr.dtype)
                          for arr, sl in zip(arrays, sliced)]
        self.specs = [pl.BlockSpec(memory_space=pl.ANY)] * self.n
        self.scratch_shapes = [pltpu.SemaphoreType.DMA((self.n, N_DEV - 1)),
                               pltpu.SemaphoreType.DMA((self.n, N_DEV - 1)),
                               pltpu.SemaphoreType.DMA((self.n,))]

    def _copies(self, ins, outs, sems):
        send_sems, recv_sems, local_sems = sems
        sliced = self.sliced
        mx, my, mc = lax.axis_index("x"), lax.axis_index("y"), lax.axis_index("c")
        me = 2 * mx + my if self.among_chips else 4 * mx + 2 * my + mc

        def peer(m):
            px = lax.rem(mx + ((m >> 2) & 1), 2)
            py = lax.rem(my + ((m >> 1) & 1), 2)
            pc = lax.rem(mc + (m & 1), 2)
            return (px, py, pc), (2 * px + py if self.among_chips else 4 * px + 2 * py + pc)

        def remote(a, m, arriving):
            pid, pflat = peer(m)
            return pltpu.make_async_remote_copy(
                src_ref=ins[a].at[pflat] if sliced[a] else ins[a],
                dst_ref=outs[a].at[pflat if arriving else me],
                send_sem=send_sems.at[a, m - 1],
                recv_sem=recv_sems.at[a, m - 1],
                device_id=pid,
                device_id_type=pl.DeviceIdType.MESH,
            )

        def local(a):
            return pltpu.make_async_copy(ins[a].at[me] if sliced[a] else ins[a], outs[a].at[me], local_sems.at[a])

        def passed_on(a, m):
            _, origin = peer(m)
            sibling, _ = peer(1)
            return pltpu.make_async_remote_copy(
                src_ref=outs[a].at[origin],
                dst_ref=outs[a].at[origin],
                send_sem=send_sems.at[a, m],
                recv_sem=recv_sems.at[a, m],
                device_id=sibling,
                device_id_type=pl.DeviceIdType.MESH,
            )

        return remote, local, passed_on

    def start(self, ins, outs, sems):
        remote, local, _ = self._copies(ins, outs, sems)
        for a in range(self.n):
            local(a).start()
        for m in self.peers:
            for a in range(self.n):
                remote(a, m, False).start()

    def relay(self, ins, outs, sems):
        remote, _, passed_on = self._copies(ins, outs, sems)
        for m in self.CHIPS:
            for a in range(self.n):
                remote(a, m, True).wait_recv()
                passed_on(a, m).start()

    def wait(self, ins, outs, sems):
        remote, local, passed_on = self._copies(ins, outs, sems)
        if self.relay_at is None:
            for m in self.peers:
                for a in range(self.n):
                    remote(a, m, True).wait_recv()
            for m in self.peers:
                for a in range(self.n):
                    remote(a, m, False).wait_send()
        else:
            for m in (1,) + tuple(c + 1 for c in self.CHIPS):
                for a in range(self.n):
                    remote(a, m, True).wait_recv()
            for m in self.peers:
                for a in range(self.n):
                    remote(a, m, False).wait_send()
            for m in self.CHIPS:
                for a in range(self.n):
                    passed_on(a, m).wait_send()
        for a in range(self.n):
            local(a).wait()


def _pair_sum(stacks, name):
    n = len(stacks)
    n_chip = N_DEV // 2
    half = [(n_chip,) + st.shape[1:] for st in stacks]

    def body(*refs):
        ins, outs, mine, theirs = refs[:n], refs[n:2 * n], refs[2 * n:3 * n], refs[3 * n:4 * n]
        local_sems, send_sems, recv_sems = refs[4 * n:]
        mx, my, mc = lax.axis_index("x"), lax.axis_index("y"), lax.axis_index("c")

        def own(a, k):
            return pltpu.make_async_copy(ins[a].at[2 * k + mc], mine[a].at[k], local_sems.at[a, k])

        def swap(a, k):
            return pltpu.make_async_remote_copy(
                src_ref=ins[a].at[2 * k + (1 - mc)], dst_ref=theirs[a].at[k],
                send_sem=send_sems.at[a, k], recv_sem=recv_sems.at[a, k],
                device_id=(mx, my, 1 - mc), device_id_type=pl.DeviceIdType.MESH)

        for k in range(n_chip):
            for a in range(n):
                own(a, k).start()
                swap(a, k).start()
        for k in range(n_chip):
            for a in range(n):
                own(a, k).wait()
                swap(a, k).wait()
                outs[a][k] = (mine[a][k].astype(F32) + theirs[a][k].astype(F32)).astype(LOW)

    return pl.pallas_call(
        body, name=name,
        out_shape=[jax.ShapeDtypeStruct(h, LOW) for h in half],
        in_specs=[pl.BlockSpec(memory_space=pl.ANY)] * n,
        out_specs=[pl.BlockSpec(memory_space=pltpu.VMEM)] * n,
        scratch_shapes=([pltpu.VMEM(h, st.dtype) for h, st in zip(half, stacks)] * 2
                        + [pltpu.SemaphoreType.DMA((n, n_chip))] * 3),
        compiler_params=pltpu.CompilerParams(vmem_limit_bytes=VMEM_LIMIT_BYTES),
    )(*stacks)


def _exchange(arrays, sliced, name, relay=False):
    ex = _Exchange(arrays, sliced, relay_at=0 if relay else None)

    def body(*refs):
        ins, outs, sems = refs[:ex.n], refs[ex.n:2 * ex.n], refs[2 * ex.n:]
        ex.start(ins, outs, sems)
        if relay:
            ex.relay(ins, outs, sems)
        ex.wait(ins, outs, sems)

    return pl.pallas_call(body, name=name, out_shape=ex.out_shape, in_specs=ex.specs, out_specs=ex.specs,
                          scratch_shapes=ex.scratch_shapes)(*arrays)


def _call(body, *, name, grid, in_specs, out_specs, out_shape, args, scratch_shapes=(), exchange=None):
    params = _params(*(("arbitrary",) * len(grid)))
    if exchange is None:
        return pl.pallas_call(body, name=name, grid=grid, in_specs=in_specs, out_specs=out_specs, out_shape=out_shape,
                              scratch_shapes=list(scratch_shapes), compiler_params=params)(*args)
    exs = list(exchange) if isinstance(exchange, (list, tuple)) else [exchange]
    n_in, n_out, n_scr = len(in_specs), len(out_specs), len(scratch_shapes)
    n_ex = sum(ex.n for ex in exs)
    n_steps = 1
    for g in grid:
        n_steps *= g

    def hosted(*refs):
        ins, refs = refs[:n_in], refs[n_in:]
        ex_ins, refs = refs[:n_ex], refs[n_ex:]
        outs, refs = refs[:n_out], refs[n_out:]
        ex_outs, refs = refs[:n_ex], refs[n_ex:]
        scr, sems = refs[:n_scr], refs[n_scr:]
        parts, at = [], 0
        for j, ex in enumerate(exs):
            parts.append((ex_ins[at:at + ex.n], ex_outs[at:at + ex.n], sems[3 * j:3 * j + 3]))
            at += ex.n
        step = pl.program_id(0)
        for ax in range(1, len(grid)):
            step = step * grid[ax] + pl.program_id(ax)

        @pl.when(step == 0)
        def _():
            for ex, part in zip(exs, parts):
                ex.start(*part)

        body(*ins, *outs, *scr)

        for ex, part in zip(exs, parts):
            if ex.relay_at is not None:
                @pl.when(step == min(int(ex.relay_at * n_steps), n_steps - 1))
                def _():
                    ex.relay(*part)

        @pl.when(step == n_steps - 1)
        def _():
            for ex, part in zip(exs, parts):
                ex.wait(*part)

    return pl.pallas_call(
        hosted, name=name, grid=grid,
        in_specs=list(in_specs) + [sp for ex in exs for sp in ex.specs],
        out_specs=list(out_specs) + [sp for ex in exs for sp in ex.specs],
        out_shape=list(out_shape) + [sh for ex in exs for sh in ex.out_shape],
        scratch_shapes=list(scratch_shapes) + [sc for ex in exs for sc in ex.scratch_shapes],
        compiler_params=params)(*args, *[arr for ex in exs for arr in ex.arrays])


def _ffn_fwd(x, g, h, wgt, wut, wd, name, exchange=None):
    s_len, d = x.shape
    fc = wd.shape[0]
    ts = min(TOK_TILE, s_len)
    first = h is None

    def body(*refs):
        x_ref, gh_ref, wg_ref, wu_ref, wd_ref, xo_ref = refs[:6]
        a_ref, b_ref, s_ref = refs[-3:]
        xv = x_ref[...]
        if first:
            hb = (xv * _rms_scale(xv) * gh_ref[...]).astype(LOW)
            refs[6][...] = hb
        else:
            hb = gh_ref[...]
        for c0, c1 in _slabs(fc):
            a = _dot_nt(hb, wg_ref[c0:c1, :])
            b = _dot_nt(hb, wu_ref[c0:c1, :])
            s_ref[:, c0:c1] = (a * jax.nn.sigmoid(a) * b).astype(LOW)
            a_ref[:, c0:c1] = a.astype(LOW)
            b_ref[:, c0:c1] = b.astype(LOW)
        xo_ref[...] = xv + 0.5 * _dot(s_ref[...], wd_ref[...])

    tok = pl.BlockSpec((ts, d), lambda t: (t, 0))
    hid = pl.BlockSpec((ts, fc), lambda t: (t, 0))
    tok_out = jax.ShapeDtypeStruct((s_len, d), F32)
    h_out = jax.ShapeDtypeStruct((s_len, d), LOW)
    hid_out = jax.ShapeDtypeStruct((s_len, fc), LOW)
    return _call(
        body,
        name=name,
        grid=(s_len // ts,),
        in_specs=[tok, pl.BlockSpec((1, d), lambda t: (0, 0)) if first else tok,
                  _resident(wgt), _resident(wut), _resident(wd)],
        out_specs=[tok] + ([tok] if first else []) + [hid, hid],
        out_shape=[tok_out] + ([h_out] if first else []) + [hid_out, hid_out],
        scratch_shapes=[pltpu.VMEM((ts, fc), LOW)],
        args=(x, g if first else h, wgt, wut, wd),
        exchange=exchange,
    )


def _slabs(width, slab=256):
    return [(c0, min(c0 + slab, width)) for c0 in range(0, width, slab)]


def _ffn_bwdw(df, a, b, h, wd, name, exchange=None):
    s_len, d = df.shape
    f_len = wd.shape[0]
    tm = _pick(f_len, WGRAD_ROW_CANDIDATES)
    tk = min(WGRAD_TOK_TILE, s_len)
    n_k = s_len // tk

    def body(df_ref, a_ref, b_ref, h_ref, wd_ref, da_ref, db_ref, dwd_ref, dwg_ref, dwu_ref,
             s_ref, acc_d, acc_g, acc_u):
        k = pl.program_id(1)

        @pl.when(k == 0)
        def _():
            acc_d[...] = jnp.zeros_like(acc_d)
            acc_g[...] = jnp.zeros_like(acc_g)
            acc_u[...] = jnp.zeros_like(acc_u)

        wdv = wd_ref[...]
        for r0, r1 in _slabs(tk, BWD_ROW_SLAB):
            ds = _dot_nt(df_ref[r0:r1, :], wdv)
            av = a_ref[r0:r1, :].astype(F32)
            bv = b_ref[r0:r1, :].astype(F32)
            sig = jax.nn.sigmoid(av)
            silu = av * sig
            s_ref[r0:r1, :] = (silu * bv).astype(LOW)
            da_ref[r0:r1, :] = (ds * bv * (sig * (1.0 + av * (1.0 - sig)))).astype(LOW)
            db_ref[r0:r1, :] = (ds * silu).astype(LOW)
        hv = h_ref[...]
        acc_d[...] += _dot_tn(s_ref[...], df_ref[...])
        acc_g[...] += _dot_tn(da_ref[...], hv)
        acc_u[...] += _dot_tn(db_ref[...], hv)

        @pl.when(k == n_k - 1)
        def _():
            dwd_ref[...] = acc_d[...].astype(LOW)
            dwg_ref[...] = acc_g[...].astype(LOW)
            dwu_ref[...] = acc_u[...].astype(LOW)

    hid = pl.BlockSpec((tk, tm), lambda i, k: (k, i))
    tok = pl.BlockSpec((tk, d), lambda i, k: (k, 0))
    wrow = pl.BlockSpec((tm, d), lambda i, k: (i, 0))
    return _call(
        body,
        name=name,
        grid=(f_len // tm, n_k),
        in_specs=[tok, hid, hid, tok, wrow],
        out_specs=[hid, hid, wrow, wrow, wrow],
        out_shape=[jax.ShapeDtypeStruct((s_len, f_len), LOW)] * 2 + [jax.ShapeDtypeStruct((f_len, d), LOW)] * 3,
        scratch_shapes=[pltpu.VMEM((tk, tm), LOW)] + [pltpu.VMEM((tm, d), F32)] * 3,
        args=(df, a, b, h, wd),
        exchange=exchange,
    )


def _ffn_dx(dxo, x, g, parts, name, exchange=None):
    s_len, d = x.shape
    ts = min(TOK_TILE, s_len)
    n_p = len(parts)

    def body(dxo_ref, x_ref, g_ref, *refs):
        dxi_ref, dg_ref = refs[4 * n_p:]

        @pl.when(pl.program_id(0) == 0)
        def _():
            dg_ref[...] = jnp.zeros_like(dg_ref)

        dh = None
        for p in range(n_p):
            da_ref, db_ref, wg_ref, wu_ref = refs[4 * p:4 * p + 4]
            part = _dot(da_ref[...], wg_ref[...]) + _dot(db_ref[...], wu_ref[...])
            dh = part if dh is None else dh + part
        dx, dgp = _rms_bwd(dh, x_ref[...], g_ref[...])
        dxi_ref[...] = dxo_ref[...] + dx
        dg_ref[...] += dgp

    tok = pl.BlockSpec((ts, d), lambda t: (t, 0))
    vec = pl.BlockSpec((1, d), lambda t: (0, 0))
    part_specs, part_args = [], []
    for da, db, wgt, wut in parts:
        hid = pl.BlockSpec((ts, da.shape[1]), lambda t: (t, 0))
        part_specs += [hid, hid, _resident(wgt), _resident(wut)]
        part_args += [da, db, wgt, wut]
    return _call(
        body,
        name=name,
        grid=(s_len // ts,),
        in_specs=[tok, tok, vec] + part_specs,
        out_specs=[tok, vec],
        out_shape=[jax.ShapeDtypeStruct((s_len, d), F32), jax.ShapeDtypeStruct((1, d), F32)],
        args=(dxo, x, g, *part_args),
        exchange=exchange,
    )


def _wgrad_tn(xm, ym, tn, stacked, name):
    s_len, m = xm.shape
    n = ym.shape[1]
    tk = min(WGRAD_TOK_TILE, s_len)
    n_k = s_len // tk

    def body(x_ref, y_ref, o_ref, acc):
        k = pl.program_id(1)

        @pl.when(k == 0)
        def _():
            acc[...] = jnp.zeros_like(acc)

        acc[...] += _dot_tn(x_ref[...].astype(LOW), y_ref[...].astype(LOW))

        @pl.when(k == n_k - 1)
        def _():
            o_ref[...] = acc[...].astype(LOW)

    if stacked:
        out_spec = pl.BlockSpec((None, m, tn), lambda j, k: (j, 0, 0))
        out_shape = jax.ShapeDtypeStruct((n // tn, m, tn), LOW)
    else:
        out_spec = pl.BlockSpec((m, tn), lambda j, k: (0, j))
        out_shape = jax.ShapeDtypeStruct((m, n), LOW)
    return pl.pallas_call(
        body,
        name=name,
        grid=(n // tn, n_k),
        in_specs=[pl.BlockSpec((tk, m), lambda j, k: (k, 0)), pl.BlockSpec((tk, tn), lambda j, k: (k, j))],
        out_specs=out_spec,
        out_shape=out_shape,
        scratch_shapes=[pltpu.VMEM((m, tn), F32)],
        compiler_params=_params("arbitrary", "arbitrary"),
    )(xm, ym)


def _mix_parts(ext_ref, cw, ts, row0):
    dc = D_CONV

    def back(off, c0, c1):
        return ext_ref[HALO - off:HALO - off + ts, c0:c1]

    v, gate_b, gate_c = back(0, 0, dc), back(0, dc, 2 * dc), back(0, 2 * dc, 3 * dc)
    z0 = gate_c * v
    z1 = back(1, 2 * dc, 3 * dc) * back(1, 0, dc)
    z2 = back(2, 2 * dc, 3 * dc) * back(2, 0, dc)
    conv = cw[2:3, :] * z0 + cw[1:2, :] * z1 + cw[0:1, :] * z2
    rows = row0 + lax.broadcasted_iota(jnp.int32, (ts, 1), 0)
    pooled, inv_count = [], []
    for grp, w in enumerate(POOL_WINDOWS):
        c0 = 3 * dc + POOL_GC * grp
        u = back(0, c0, c0 + POOL_GC)
        acc = u
        for j in range(1, w):
            acc = acc + back(j, c0, c0 + POOL_GC)
        inv = 1.0 / jnp.minimum(rows + 1, w).astype(F32)
        pooled.append(acc * inv - u)
        inv_count.append(inv)
    return v, gate_b, gate_c, z0, z1, z2, conv, pooled, inv_count


def _mix_fwd(x, g, w_in, conv_w, pool_w, pool_scale, w_out, name, exchange=None):
    s_len, d = x.shape
    n_blk, _, wcols = w_in.shape
    p_len = n_blk * wcols
    d_mix = w_out.shape[0]
    ts = min(MIX_TOK_TILE, s_len)
    dc = D_CONV

    def body(x_ref, g_ref, win_ref, cw_ref, pw_ref, ps_ref, wout_ref, x2_ref, h_ref, proj_ref, ext_ref, cat_ref):
        t = pl.program_id(0)

        @pl.when(t == 0)
        def _():
            ext_ref[0:HALO, :] = jnp.zeros((HALO, p_len), F32)

        xv = x_ref[...]
        hb = (xv * _rms_scale(xv) * g_ref[...]).astype(LOW)
        h_ref[...] = hb
        for k in range(n_blk):
            ext_ref[HALO:HALO + ts, wcols * k:wcols * (k + 1)] = _dot(hb, win_ref[k])
        proj_ref[...] = ext_ref[HALO:HALO + ts, :]

        _, gate_b, _, _, _, _, conv, pooled, _ = _mix_parts(ext_ref, cw_ref[...], ts, t * ts)
        cat_ref[:, 0:dc] = (gate_b * conv).astype(LOW)
        for grp in range(len(POOL_WINDOWS)):
            c0 = POOL_GC * grp
            lin = _dot(pooled[grp].astype(LOW), pw_ref[grp])
            cat_ref[:, dc + c0:dc + c0 + POOL_GC] = (lin * ps_ref[:, c0:c0 + POOL_GC]).astype(LOW)
        x2_ref[...] = xv + _dot(cat_ref[...], wout_ref[...])
        ext_ref[0:HALO, :] = ext_ref[ts:ts + HALO, :]

    tok = pl.BlockSpec((ts, d), lambda t: (t, 0))

    def whole(arr):
        return pl.BlockSpec(arr.shape, lambda t: (0,) * arr.ndim)

    return _call(
        body,
        name=name,
        grid=(s_len // ts,),
        in_specs=[tok, whole(g), _resident(w_in), whole(conv_w), whole(pool_w), whole(pool_scale), _resident(w_out)],
        out_specs=[tok, tok, pl.BlockSpec((ts, p_len), lambda t: (t, 0))],
        out_shape=[
            jax.ShapeDtypeStruct((s_len, d), F32),
            jax.ShapeDtypeStruct((s_len, d), LOW),
            jax.ShapeDtypeStruct((s_len, p_len), F32),
        ],
        scratch_shapes=[pltpu.VMEM((ts + HALO, p_len), F32), pltpu.VMEM((ts, d_mix), LOW)],
        args=(x, g, w_in, conv_w, pool_w, pool_scale, w_out),
        exchange=exchange,
    )


def _mix_bwd(dx2, x, proj, g, w_in, conv_w, pool_w, pool_scale, w_out, name, exchange=None):
    s_len, d = x.shape
    n_blk, _, wcols = w_in.shape
    p_len = n_blk * wcols
    d_mix = w_out.shape[0]
    ts = min(MIX_TOK_TILE, s_len)
    n_t = s_len // ts
    dc = D_CONV
    n_grp = len(POOL_WINDOWS)

    def body(dx2_ref, x_ref, proj_ref, halo_ref, g_ref, win_ref, cw_ref, pw_ref, ps_ref, wout_ref,
             dx_ref, dproj_ref, cat_ref, dg_ref, dcw_ref, dpw_ref, dps_ref, df_ref, ext_ref, fut_ref):
        i = pl.program_id(0)
        t = n_t - 1 - i

        @pl.when(i == 0)
        def _():
            dg_ref[...] = jnp.zeros_like(dg_ref)
            dcw_ref[...] = jnp.zeros_like(dcw_ref)
            dpw_ref[...] = jnp.zeros_like(dpw_ref)
            dps_ref[...] = jnp.zeros_like(dps_ref)
            fut_ref[ts:ts + HALO, :] = jnp.zeros((HALO, d_mix), F32)

        ext_ref[HALO:HALO + ts, :] = proj_ref[...]

        @pl.when(t == 0)
        def _():
            ext_ref[0:HALO, :] = jnp.zeros((HALO, p_len), F32)

        @pl.when(t > 0)
        def _():
            ext_ref[0:HALO, :] = halo_ref[...]

        cw = cw_ref[...]
        v, gate_b, gate_c, z0, z1, z2, conv, pooled, inv_count = _mix_parts(ext_ref, cw, ts, t * ts)
        dx2 = dx2_ref[...]
        dcat = _dot_nt(dx2.astype(LOW), wout_ref[...])

        dy_a = dcat[:, 0:dc]
        dconv = dy_a * gate_b
        fut_ref[0:ts, 0:dc] = dconv
        cat_ref[:, 0:dc] = (gate_b * conv).astype(LOW)
        dproj_ref[:, dc:2 * dc] = (dy_a * conv).astype(LOW)
        dcw_ref[2:3, :] += jnp.sum(dconv * z0, axis=0, keepdims=True)
        dcw_ref[1:2, :] += jnp.sum(dconv * z1, axis=0, keepdims=True)
        dcw_ref[0:1, :] += jnp.sum(dconv * z2, axis=0, keepdims=True)

        dpool = []
        for grp in range(n_grp):
            c0 = POOL_GC * grp
            pooled_b = pooled[grp].astype(LOW)
            lin = _dot(pooled_b, pw_ref[grp])
            dy_b = dcat[:, dc + c0:dc + c0 + POOL_GC]
            scale = ps_ref[:, c0:c0 + POOL_GC]
            cat_ref[:, dc + c0:dc + c0 + POOL_GC] = (lin * scale).astype(LOW)
            dps_ref[:, c0:c0 + POOL_GC] += jnp.sum(dy_b * lin, axis=0, keepdims=True)
            dlin = (dy_b * scale).astype(LOW)
            dpw_ref[grp] += _dot_tn(pooled_b, dlin)
            dpool.append(_dot_nt(dlin, pw_ref[grp]))
            fut_ref[0:ts, dc + c0:dc + c0 + POOL_GC] = dpool[grp] * inv_count[grp]

        def ahead(off, c0, c1):
            return fut_ref[off:off + ts, c0:c1]

        dz = cw[2:3, :] * ahead(0, 0, dc) + cw[1:2, :] * ahead(1, 0, dc) + cw[0:1, :] * ahead(2, 0, dc)
        dproj_ref[:, 0:dc] = (dz * gate_c).astype(LOW)
        dproj_ref[:, 2 * dc:3 * dc] = (dz * v).astype(LOW)
        for grp, w in enumerate(POOL_WINDOWS):
            c0 = dc + POOL_GC * grp
            acc = ahead(0, c0, c0 + POOL_GC)
            for j in range(1, w):
                acc = acc + ahead(j, c0, c0 + POOL_GC)
            dproj_ref[:, 2 * dc + c0:2 * dc + c0 + POOL_GC] = (acc - dpool[grp]).astype(LOW)

        dh = _dot_nt(dproj_ref[:, 0:wcols], win_ref[0])
        for k in range(1, n_blk):
            dh += _dot_nt(dproj_ref[:, wcols * k:wcols * (k + 1)], win_ref[k])
        dx, dgp = _rms_bwd(dh, x_ref[...], g_ref[...])
        dx = dx2 + dx
        dx_ref[...] = dx
        df_ref[...] = (0.5 * dx).astype(LOW)
        dg_ref[...] += dgp
        fut_ref[ts:ts + HALO, :] = fut_ref[0:HALO, :]

    tok = pl.BlockSpec((ts, d), lambda i: (n_t - 1 - i, 0))
    halo = pl.BlockSpec((HALO, p_len), lambda i: (jnp.maximum((n_t - 1 - i) * (ts // HALO) - 1, 0), 0))

    def whole(arr):
        return pl.BlockSpec(arr.shape, lambda i: (0,) * arr.ndim)

    return _call(
        body,
        name=name,
        grid=(n_t,),
        in_specs=[tok, tok, pl.BlockSpec((ts, p_len), lambda i: (n_t - 1 - i, 0)), halo,
                  whole(g), _resident(w_in), whole(conv_w), whole(pool_w), whole(pool_scale), _resident(w_out)],
        out_specs=[tok, pl.BlockSpec((ts, p_len), lambda i: (n_t - 1 - i, 0)),
                   pl.BlockSpec((ts, d_mix), lambda i: (n_t - 1 - i, 0)),
                   whole(g), whole(conv_w), whole(pool_w), whole(pool_scale), tok],
        out_shape=[
            jax.ShapeDtypeStruct((s_len, d), F32),
            jax.ShapeDtypeStruct((s_len, p_len), LOW),
            jax.ShapeDtypeStruct((s_len, d_mix), LOW),
            jax.ShapeDtypeStruct(g.shape, F32),
            jax.ShapeDtypeStruct(conv_w.shape, F32),
            jax.ShapeDtypeStruct(pool_w.shape, F32),
            jax.ShapeDtypeStruct(pool_scale.shape, F32),
            jax.ShapeDtypeStruct((s_len, d), LOW),
        ],
        scratch_shapes=[pltpu.VMEM((ts + HALO, p_len), F32), pltpu.VMEM((ts + HALO, d_mix), F32)],
        args=(dx2, x, proj, proj, g, w_in, conv_w, pool_w, pool_scale, w_out),
        exchange=exchange,
    )


def _loss_head(x, g, target, name):
    s_len, d = x.shape
    ts = min(TOK_TILE, s_len)

    def body(x_ref, g_ref, tgt_ref, loss_ref, dx_ref, dg_ref, df_ref):
        @pl.when(pl.program_id(0) == 0)
        def _():
            loss_ref[...] = jnp.zeros_like(loss_ref)
            dg_ref[...] = jnp.zeros_like(dg_ref)

        xv, gv = x_ref[...], g_ref[...]
        err = xv * _rms_scale(xv) * gv - tgt_ref[...]
        loss_ref[...] += 0.5 * jnp.sum(jnp.mean(err * err, axis=-1, keepdims=True), axis=0, keepdims=True)
        dx, dgp = _rms_bwd(err * (1.0 / d), xv, gv)
        dx_ref[...] = dx
        df_ref[...] = (0.5 * dx).astype(LOW)
        dg_ref[...] += dgp

    tok = pl.BlockSpec((ts, d), lambda t: (t, 0))
    vec = pl.BlockSpec((1, d), lambda t: (0, 0))
    return pl.pallas_call(
        body,
        name=name,
        grid=(s_len // ts,),
        in_specs=[tok, vec, tok],
        out_specs=[pl.BlockSpec((1, 128), lambda t: (0, 0)), tok, vec, tok],
        out_shape=[
            jax.ShapeDtypeStruct((1, 128), F32),
            jax.ShapeDtypeStruct((s_len, d), F32),
            jax.ShapeDtypeStruct((1, d), F32),
            jax.ShapeDtypeStruct((s_len, d), LOW),
        ],
        compiler_params=_params("arbitrary"),
    )(x, g, target)


def _row_tile(rows, cols, stack_bytes):
    budget = 20 * 1024 * 1024
    per_row = cols * (4 * 7 + stack_bytes)
    for tr in (rows, 512, 256, 176, 128, 64, 32, 16, 8):
        if rows % tr == 0 and tr % 8 == 0 and tr * per_row * 2 <= budget:
            return tr
    return rows


def _sum_stack(stack, name):
    n, r, c = stack.shape
    tr = _row_tile(r, c, n * stack.dtype.itemsize)

    def body(s_ref, o_ref):
        acc = s_ref[0].astype(F32)
        for k in range(1, n):
            acc = acc + s_ref[k].astype(F32)
        o_ref[...] = acc

    return pl.pallas_call(
        body,
        name=name,
        grid=(r // tr,),
        in_specs=[pl.BlockSpec((n, tr, c), lambda i: (0, i, 0))],
        out_specs=pl.BlockSpec((tr, c), lambda i: (i, 0)),
        out_shape=jax.ShapeDtypeStruct((r, c), F32),
        compiler_params=_params("arbitrary"),
    )(stack)


def _adamw(stacks, w, m, v, name):
    stacks = list(stacks) if isinstance(stacks, (list, tuple)) else [stacks]
    r, c = w.shape
    n = stacks[0].shape[0]
    part_rows = [st.shape[1] for st in stacks]
    assert sum(part_rows) == r and all(st.shape[0] == n for st in stacks)
    common = part_rows[0]
    for pr in part_rows[1:]:
        common = math.gcd(common, pr)
    tr = _row_tile(common, c, n * stacks[0].dtype.itemsize)
    first_blk = [sum(part_rows[:j]) // tr for j in range(len(stacks))]
    c1 = 1.0 - ADAM_B1 ** ADAM_STEP
    c2 = 1.0 - ADAM_B2 ** ADAM_STEP

    def body(*refs):
        s_refs = refs[:len(stacks)]
        w_ref, m_ref, v_ref, g_ref, d_ref, mo_ref, vo_ref = refs[len(stacks):]

        def total(s_ref):
            acc = s_ref[0].astype(F32)
            for k in range(1, n):
                acc = acc + s_ref[k].astype(F32)
            return acc

        gv = total(s_refs[0])
        for j in range(1, len(stacks)):
            gv = jnp.where(pl.program_id(0) >= first_blk[j], total(s_refs[j]), gv)
        mn = ADAM_B1 * m_ref[...] + (1.0 - ADAM_B1) * gv
        vn = ADAM_B2 * v_ref[...] + (1.0 - ADAM_B2) * (gv * gv)
        g_ref[...] = gv
        mo_ref[...] = mn
        vo_ref[...] = vn
        d_ref[...] = -ADAM_LR * ((mn / c1) / (jnp.sqrt(vn / c2) + ADAM_EPS) + ADAM_WD * w_ref[...])

    def stack_spec(j):
        last = part_rows[j] // tr - 1
        return pl.BlockSpec((n, tr, c), lambda i: (0, jnp.clip(i - first_blk[j], 0, last), 0))

    blk = pl.BlockSpec((tr, c), lambda i: (i, 0))
    return pl.pallas_call(
        body,
        name=name,
        grid=(r // tr,),
        in_specs=[stack_spec(j) for j in range(len(stacks))] + [blk, blk, blk],
        out_specs=[blk] * 4,
        out_shape=[jax.ShapeDtypeStruct((r, c), F32)] * 4,
        compiler_params=_params("arbitrary"),
    )(*stacks, w, m, v)


def _to_sheet(parts):
    sheets, spans = [], []
    row = 0
    for p in parts:
        flat = p.reshape(-1).astype(F32)
        rows = -(-flat.shape[0] // 1024) * 8
        flat = jnp.pad(flat, (0, rows * 128 - flat.shape[0]))
        sheets.append(flat.reshape(rows, 128))
        spans.append((row, p.size, p.shape))
        row += rows
    return jnp.concatenate(sheets, axis=0), spans


def _from_sheet(sheet, spans):
    out = []
    for row, size, shape in spans:
        rows = -(-size // 1024) * 8
        out.append(sheet[row:row + rows].reshape(-1)[:size].reshape(shape))
    return out


def kernel(x, norm_ffn1, ffn1_w_gate, ffn1_w_up, ffn1_w_down, norm_mix, w_in, conv_w, pool_w, pool_scale, w_out, norm_ffn2, ffn2_w_gate, ffn2_w_up, ffn2_w_down, norm_final, loss_target, m_norm_ffn1, m_ffn1_w_gate, m_ffn1_w_up, m_ffn1_w_down, m_norm_mix, m_w_in, m_conv_w, m_pool_w, m_pool_scale, m_w_out, m_norm_ffn2, m_ffn2_w_gate, m_ffn2_w_up, m_ffn2_w_down, m_norm_final, v_norm_ffn1, v_ffn1_w_gate, v_ffn1_w_up, v_ffn1_w_down, v_norm_mix, v_w_in, v_conv_w, v_pool_w, v_pool_scale, v_w_out, v_norm_ffn2, v_ffn2_w_gate, v_ffn2_w_up, v_ffn2_w_down, v_norm_final):
    me = 4 * lax.axis_index("x") + 2 * lax.axis_index("y") + lax.axis_index("c")
    xs, tgt = x[0], loss_target[0]
    s_len, d = xs.shape
    f_shard = ffn1_w_down.shape[1]
    conv_shard = conv_w.shape[2]

    def low_t(wt):
        return wt[0].T.astype(LOW)

    def by_dev(gw):
        return gw.reshape(N_DEV, -1, d)

    conv_tile = jnp.zeros((8, 128), F32).at[0:conv_w.shape[1], 0:conv_shard].set(conv_w[0])
    pool_w_low = pool_w[0].astype(LOW)

    rows_a = -(-f_shard // 64) * 32

    def parts_of(w_gate, w_up, w_down):
        shards = [low_t(w_gate), low_t(w_up), w_down[0].astype(LOW)]
        return [s[:rows_a] for s in shards], [s[rows_a:] for s in shards]

    def rows_flat(stacks):
        return [st.reshape(-1, d) for st in stacks]

    def gather(shards, relay_at):
        return _Exchange(shards, [False] * len(shards), relay_at=relay_at)

    def scatter(pairs):
        return _Exchange(pairs, [True] * len(pairs), among_chips=True)

    w1a_shards, w1b_shards = parts_of(ffn1_w_gate, ffn1_w_up, ffn1_w_down)
    w2a_shards, w2b_shards = parts_of(ffn2_w_gate, ffn2_w_up, ffn2_w_down)

    wg1a, wu1a, wd1a = rows_flat(_exchange(w1a_shards, [False] * 3, "gather_ffn1_a", relay=True))
    xa, h1, a1a, b1a, *w1b = _ffn_fwd(xs, norm_ffn1, None, wg1a, wu1a, wd1a, "ffn1_fwd_a",
                                      exchange=gather(w1b_shards, 0.7))
    wg1b, wu1b, wd1b = rows_flat(w1b)
    x1, a1b, b1b, w_in_full, w_out_full, conv_tiles = _ffn_fwd(
        xa, None, h1, wg1b, wu1b, wd1b, "ffn1_fwd_b",
        exchange=gather([w_in[0].astype(LOW), w_out[0].astype(LOW), conv_tile], 0.7))
    w_out_full = w_out_full.reshape(-1, d)
    conv_full = jnp.concatenate([conv_tiles[k, 0:conv_w.shape[1], 0:conv_shard] for k in range(N_DEV)], axis=1)
    x2, h2, proj, *w2a = _mix_fwd(x1, norm_mix, w_in_full, conv_full, pool_w_low, pool_scale, w_out_full, "mix_fwd",
                                  exchange=gather(w2a_shards, 0.8))
    wg2a, wu2a, wd2a = rows_flat(w2a)
    xb, h3, a2a, b2a, *w2b = _ffn_fwd(x2, norm_ffn2, None, wg2a, wu2a, wd2a, "ffn2_fwd_a",
                                      exchange=gather(w2b_shards, 0.7))
    wg2b, wu2b, wd2b = rows_flat(w2b)
    x3, a2b, b2b = _ffn_fwd(xb, None, h3, wg2b, wu2b, wd2b, "ffn2_fwd_b")
    loss_row, dx3, dg_final, df3 = _loss_head(x3, norm_final.reshape(1, d), tgt, "loss_head")

    da2a, db2a, dwd, dwg, dwu = _ffn_bwdw(df3, a2a, b2a, h3, wd2a, "ffn2_bwdw_a")
    pairs = _pair_sum([by_dev(dwg), by_dev(dwu), by_dev(dwd)], "pair_sum_ffn2_a")
    da2b, db2b, dwd, dwg, dwu, *got_2a = _ffn_bwdw(df3, a2b, b2b, h3, wd2b, "ffn2_bwdw_b", exchange=scatter(pairs))
    pairs = _pair_sum([by_dev(dwg), by_dev(dwu), by_dev(dwd)], "pair_sum_ffn2_b")
    dx2, dg_ffn2, *got_2b = _ffn_dx(dx3, x2, norm_ffn2, [(da2a, db2a, wg2a, wu2a), (da2b, db2b, wg2b, wu2b)],
                                    "ffn2_dx", exchange=scatter(pairs))
    dx1, dproj, cat, dg_mix, dconv, dpool_w, dpool_scale, df1 = _mix_bwd(
        dx2, x1, proj, norm_mix, w_in_full, conv_full, pool_w_low, pool_scale, w_out_full, "mix_bwd")
    dw_in = _wgrad_tn(h2, dproj, W_IN_SHARD, True, "w_in_wgrad")
    dw_out = _wgrad_tn(cat, dx2, d, False, "w_out_wgrad")
    small_parts = [dg_mix, dg_ffn2, dg_final, dconv, dpool_w, dpool_scale]
    small_sheet, spans = _to_sheet(small_parts)
    da1a, db1a, dwd, dwg, dwu, got_in, got_out, got_small = _ffn_bwdw(
        df1, a1a, b1a, h1, wd1a, "ffn1_bwdw_a",
        exchange=_Exchange([dw_in, by_dev(dw_out), small_sheet], [True, True, False]))
    pairs = _pair_sum([by_dev(dwg), by_dev(dwu), by_dev(dwd)], "pair_sum_ffn1_a")
    da1b, db1b, dwd, dwg, dwu, *got_1a = _ffn_bwdw(df1, a1b, b1b, h1, wd1b, "ffn1_bwdw_b", exchange=scatter(pairs))
    pairs = _pair_sum([by_dev(dwg), by_dev(dwu), by_dev(dwd)], "pair_sum_ffn1_b")
    dx0, dg_ffn1, *got_1b = _ffn_dx(dx1, xs, norm_ffn1, [(da1a, db1a, wg1a, wu1a), (da1b, db1b, wg1b, wu1b)],
                                    "ffn1_dx", exchange=scatter(pairs))
    (got_n1,) = _exchange([dg_ffn1.reshape(8, 128)], [False], "gather_dnorm_ffn1")

    outs = {}

    def update(name, stacks, w, m, v):
        outs[name] = _adamw(stacks, w[0], m[0], v[0], "adamw_" + name)

    def update_t(name, stacks, w, m, v):
        res = _adamw(stacks, w[0].T, m[0].T, v[0].T, "adamw_" + name)
        outs[name] = tuple(r.T for r in res)

    update_t("ffn1_w_gate", [got_1a[0], got_1b[0]], ffn1_w_gate, m_ffn1_w_gate, v_ffn1_w_gate)
    update_t("ffn1_w_up", [got_1a[1], got_1b[1]], ffn1_w_up, m_ffn1_w_up, v_ffn1_w_up)
    update("ffn1_w_down", [got_1a[2], got_1b[2]], ffn1_w_down, m_ffn1_w_down, v_ffn1_w_down)
    update("w_in", got_in, w_in, m_w_in, v_w_in)
    update("w_out", got_out, w_out, m_w_out, v_w_out)
    update_t("ffn2_w_gate", [got_2a[0], got_2b[0]], ffn2_w_gate, m_ffn2_w_gate, v_ffn2_w_gate)
    update_t("ffn2_w_up", [got_2a[1], got_2b[1]], ffn2_w_up, m_ffn2_w_up, v_ffn2_w_up)
    update("ffn2_w_down", [got_2a[2], got_2b[2]], ffn2_w_down, m_ffn2_w_down, v_ffn2_w_down)

    g_small = _from_sheet(_sum_stack(got_small, "sum_small"), spans)
    g_norm_ffn1 = _sum_stack(got_n1, "sum_dnorm_ffn1").reshape(norm_ffn1.shape)
    g_conv = lax.dynamic_slice_in_dim(g_small[3], me * conv_shard, conv_shard, axis=1)
    small_names = ["norm_ffn1", "norm_mix", "norm_ffn2", "norm_final", "conv_w", "pool_w", "pool_scale"]
    small_g = [g_norm_ffn1, g_small[0], g_small[1], g_small[2].reshape(norm_final.shape), g_conv[None],
               g_small[4][None], g_small[5]]
    small_w = [norm_ffn1, norm_mix, norm_ffn2, norm_final, conv_w, pool_w, pool_scale]
    small_m = [m_norm_ffn1, m_norm_mix, m_norm_ffn2, m_norm_final, m_conv_w, m_pool_w, m_pool_scale]
    small_v = [v_norm_ffn1, v_norm_mix, v_norm_ffn2, v_norm_final, v_conv_w, v_pool_w, v_pool_scale]
    g_sheet, spans_u = _to_sheet(small_g)
    w_sheet, _ = _to_sheet(small_w)
    m_sheet, _ = _to_sheet(small_m)
    v_sheet, _ = _to_sheet(small_v)
    upd = _adamw(g_sheet[None], w_sheet, m_sheet, v_sheet, "adamw_small")
    small_out = [_from_sheet(u, spans_u) for u in upd]
    for k, nm in enumerate(small_names):
        outs[nm] = tuple(small_out[j][k] for j in range(4))

    loss = lax.psum(loss_row[0, 0], ("x", "y", "c"))
    order = ["norm_ffn1", "ffn1_w_gate", "ffn1_w_up", "ffn1_w_down", "norm_mix", "w_in", "conv_w", "pool_w",
             "pool_scale", "w_out", "norm_ffn2", "ffn2_w_gate", "ffn2_w_up", "ffn2_w_down", "norm_final"]
    big = {"ffn1_w_gate", "ffn1_w_up", "ffn1_w_down", "w_in", "w_out", "ffn2_w_gate", "ffn2_w_up", "ffn2_w_down"}

    def leaf(nm, j):
        val = outs[nm][j]
        return val[None] if nm in big else val

    return (loss, dx0[None],
            *[leaf(nm, 0) for nm in order], *[leaf(nm, 1) for nm in order],
            *[leaf(nm, 2) for nm in order], *[leaf(nm, 3) for nm in order])
```

```python
import jax
import jax.numpy as jnp
from jax import lax
from jax.experimental import pallas as pl
from jax.experimental.pallas import tpu as pltpu

F32 = jnp.float32
LOW = jnp.bfloat16

N_DEV = 8
EPS = 1e-6
D_CONV = 512
POOL_WINDOWS = (2, 4, 8, 16)
POOL_GC = 128
HALO = 16
W_IN_SHARD = 256

ADAM_LR = 0.001
ADAM_B1 = 0.9
ADAM_B2 = 0.999
ADAM_EPS = 1e-08
ADAM_WD = 0.01
ADAM_STEP = 10

VMEM_LIMIT_BYTES = 56 * 1024 * 1024
TOK_TILE = 512
MIX_TOK_TILE = 256
WGRAD_TOK_TILE = 4096
WGRAD_ROW_CANDIDATES = (256, 128)
BWD_ROW_SLAB = 256


def _params(*sem):
    return pltpu.CompilerParams(dimension_semantics=sem, vmem_limit_bytes=VMEM_LIMIT_BYTES)


def _resident(arr):
    return pl.BlockSpec(arr.shape, lambda *_: (0,) * arr.ndim, pipeline_mode=pl.Buffered(1))


def _pick(n, candidates):
    for c in candidates:
        if n % c == 0:
            return c
    raise ValueError(f"no tile in {candidates} divides {n}")


def _dot(a, b):
    return lax.dot_general(a, b, (((1,), (0,)), ((), ())), preferred_element_type=F32)


def _dot_nt(a, b):
    return lax.dot_general(a, b, (((1,), (1,)), ((), ())), preferred_element_type=F32)


def _dot_tn(a, b):
    return lax.dot_general(a, b, (((0,), (0,)), ((), ())), preferred_element_type=F32)


def _rms_scale(x):
    return lax.rsqrt(jnp.mean(x * x, axis=-1, keepdims=True) + EPS)


def _rms_bwd(dy, x, g):
    r = _rms_scale(x)
    xhat = x * r
    gdy = dy * g
    dx = r * (gdy - xhat * jnp.mean(gdy * xhat, axis=-1, keepdims=True))
    return dx, jnp.sum(dy * xhat, axis=0, keepdims=True)


class _Exchange:
    CHIPS = (2, 4, 6)

    def __init__(self, arrays, sliced, relay_at=None, among_chips=False):
        assert relay_at is None or not any(sliced)
        assert not among_chips or (all(sliced) and relay_at is None)
        self.relay_at, self.among_chips = relay_at, among_chips
        self.peers = self.CHIPS if among_chips else ((1,) + self.CHIPS if relay_at is not None else tuple(range(1, N_DEV)))
        self.arrays, self.sliced, self.n = list(arrays), list(sliced), len(arrays)
        self.out_shape = [jax.ShapeDtypeStruct(arr.shape if sl else (N_DEV,) + arr.shape, arr.dtype)
                          for arr, sl in zip(arrays, sliced)]
        self.specs = [pl.BlockSpec(memory_space=pl.ANY)] * self.n
        self.scratch_shapes = [pltpu.SemaphoreType.DMA((self.n, N_DEV - 1)),
                               pltpu.SemaphoreType.DMA((self.n, N_DEV - 1)),
                               pltpu.SemaphoreType.DMA((self.n,))]

    def _copies(self, ins, outs, sems):
        send_sems, recv_sems, local_sems = sems
        sliced = self.sliced
        mx, my, mc = lax.axis_index("x"), lax.axis_index("y"), lax.axis_index("c")
        me = 2 * mx + my if self.among_chips else 4 * mx + 2 * my + mc

        def peer(m):
            px = lax.rem(mx + ((m >> 2) & 1), 2)
            py = lax.rem(my + ((m >> 1) & 1), 2)
            pc = lax.rem(mc + (m & 1), 2)
            return (px, py, pc), (2 * px + py if self.among_chips else 4 * px + 2 * py + pc)

        def remote(a, m, arriving):
            pid, pflat = peer(m)
            return pltpu.make_async_remote_copy(
                src_ref=ins[a].at[pflat] if sliced[a] else ins[a],
                dst_ref=outs[a].at[pflat if arriving else me],
                send_sem=send_sems.at[a, m - 1],
                recv_sem=recv_sems.at[a, m - 1],
                device_id=pid,
                device_id_type=pl.DeviceIdType.MESH,
            )

        def local(a):
            return pltpu.make_async_copy(ins[a].at[me] if sliced[a] else ins[a], outs[a].at[me], local_sems.at[a])

        def passed_on(a, m):
            _, origin = peer(m)
            sibling, _ = peer(1)
            return pltpu.make_async_remote_copy(
                src_ref=outs[a].at[origin],
                dst_ref=outs[a].at[origin],
                send_sem=send_sems.at[a, m],
                recv_sem=recv_sems.at[a, m],
                device_id=sibling,
                device_id_type=pl.DeviceIdType.MESH,
            )

        return remote, local, passed_on

    def start(self, ins, outs, sems):
        remote, local, _ = self._copies(ins, outs, sems)
        for a in range(self.n):
            local(a).start()
        for m in self.peers:
            for a in range(self.n):
                remote(a, m, False).start()

    def relay(self, ins, outs, sems):
        remote, _, passed_on = self._copies(ins, outs, sems)
        for m in self.CHIPS:
            for a in range(self.n):
                remote(a, m, True).wait_recv()
                passed_on(a, m).start()

    def wait(self, ins, outs, sems):
        remote, local, passed_on = self._copies(ins, outs, sems)
        if self.relay_at is None:
            for m in self.peers:
                for a in range(self.n):
                    remote(a, m, True).wait_recv()
            for m in self.peers:
                for a in range(self.n):
                    remote(a, m, False).wait_send()
        else:
            for m in (1,) + tuple(c + 1 for c in self.CHIPS):
                for a in range(self.n):
                    remote(a, m, True).wait_recv()
            for m in self.peers:
                for a in range(self.n):
                    remote(a, m, False).wait_send()
            for m in self.CHIPS:
                for a in range(self.n):
                    passed_on(a, m).wait_send()
        for a in range(self.n):
            local(a).wait()


def _pair_sum(stacks, name):
    n = len(stacks)
    n_chip = N_DEV // 2
    half = [(n_chip,) + st.shape[1:] for st in stacks]

    def body(*refs):
        ins, outs, mine, theirs = refs[:n], refs[n:2 * n], refs[2 * n:3 * n], refs[3 * n:4 * n]
        local_sems, send_sems, recv_sems = refs[4 * n:]
        mx, my, mc = lax.axis_index("x"), lax.axis_index("y"), lax.axis_index("c")

        def own(a, k):
            return pltpu.make_async_copy(ins[a].at[2 * k + mc], mine[a].at[k], local_sems.at[a, k])

        def swap(a, k):
            return pltpu.make_async_remote_copy(
                src_ref=ins[a].at[2 * k + (1 - mc)], dst_ref=theirs[a].at[k],
                send_sem=send_sems.at[a, k], recv_sem=recv_sems.at[a, k],
                device_id=(mx, my, 1 - mc), device_id_type=pl.DeviceIdType.MESH)

        for k in range(n_chip):
            for a in range(n):
                own(a, k).start()
                swap(a, k).start()
        for k in range(n_chip):
            for a in range(n):
                own(a, k).wait()
                swap(a, k).wait()
                outs[a][k] = (mine[a][k].astype(F32) + theirs[a][k].astype(F32)).astype(LOW)

    return pl.pallas_call(
        body, name=name,
        out_shape=[jax.ShapeDtypeStruct(h, LOW) for h in half],
        in_specs=[pl.BlockSpec(memory_space=pl.ANY)] * n,
        out_specs=[pl.BlockSpec(memory_space=pltpu.VMEM)] * n,
        scratch_shapes=([pltpu.VMEM(h, st.dtype) for h, st in zip(half, stacks)] * 2
                        + [pltpu.SemaphoreType.DMA((n, n_chip))] * 3),
        compiler_params=pltpu.CompilerParams(vmem_limit_bytes=VMEM_LIMIT_BYTES),
    )(*stacks)


def _exchange(arrays, sliced, name, relay=False):
    ex = _Exchange(arrays, sliced, relay_at=0 if relay else None)

    def body(*refs):
        ins, outs, sems = refs[:ex.n], refs[ex.n:2 * ex.n], refs[2 * ex.n:]
        ex.start(ins, outs, sems)
        if relay:
            ex.relay(ins, outs, sems)
        ex.wait(ins, outs, sems)

    return pl.pallas_call(body, name=name, out_shape=ex.out_shape, in_specs=ex.specs, out_specs=ex.specs,
                          scratch_shapes=ex.scratch_shapes)(*arrays)


def _call(body, *, name, grid, in_specs, out_specs, out_shape, args, scratch_shapes=(), exchange=None):
    params = _params(*(("arbitrary",) * len(grid)))
    if exchange is None:
        return pl.pallas_call(body, name=name, grid=grid, in_specs=in_specs, out_specs=out_specs, out_shape=out_shape,
                              scratch_shapes=list(scratch_shapes), compiler_params=params)(*args)
    exs = list(exchange) if isinstance(exchange, (list, tuple)) else [exchange]
    n_in, n_out, n_scr = len(in_specs), len(out_specs), len(scratch_shapes)
    n_ex = sum(ex.n for ex in exs)
    n_steps = 1
    for g in grid:
        n_steps *= g

    def hosted(*refs):
        ins, refs = refs[:n_in], refs[n_in:]
        ex_ins, refs = refs[:n_ex], refs[n_ex:]
        outs, refs = refs[:n_out], refs[n_out:]
        ex_outs, refs = refs[:n_ex], refs[n_ex:]
        scr, sems = refs[:n_scr], refs[n_scr:]
        parts, at = [], 0
        for j, ex in enumerate(exs):
            parts.append((ex_ins[at:at + ex.n], ex_outs[at:at + ex.n], sems[3 * j:3 * j + 3]))
            at += ex.n
        step = pl.program_id(0)
        for ax in range(1, len(grid)):
            step = step * grid[ax] + pl.program_id(ax)

        @pl.when(step == 0)
        def _():
            for ex, part in zip(exs, parts):
                ex.start(*part)

        body(*ins, *outs, *scr)

        for ex, part in zip(exs, parts):
            if ex.relay_at is not None:
                @pl.when(step == min(int(ex.relay_at * n_steps), n_steps - 1))
                def _():
                    ex.relay(*part)

        @pl.when(step == n_steps - 1)
        def _():
            for ex, part in zip(exs, parts):
                ex.wait(*part)

    return pl.pallas_call(
        hosted, name=name, grid=grid,
        in_specs=list(in_specs) + [sp for ex in exs for sp in ex.specs],
        out_specs=list(out_specs) + [sp for ex in exs for sp in ex.specs],
        out_shape=list(out_shape) + [sh for ex in exs for sh in ex.out_shape],
        scratch_shapes=list(scratch_shapes) + [sc for ex in exs for sc in ex.scratch_shapes],
        compiler_params=params)(*args, *[arr for ex in exs for arr in ex.arrays])


def _ffn_fwd(x, g, h, wgt, wut, wd, name, exchange=None):
    s_len, d = x.shape
    fc = wd.shape[0]
    ts = min(TOK_TILE, s_len)
    first = h is None

    def body(*refs):
        x_ref, gh_ref, wg_ref, wu_ref, wd_ref, xo_ref = refs[:6]
        a_ref, b_ref, s_ref = refs[-3:]
        xv = x_ref[...]
        if first:
            hb = (xv * _rms_scale(xv) * gh_ref[...]).astype(LOW)
            refs[6][...] = hb
        else:
            hb = gh_ref[...]
        for c0, c1 in _slabs(fc):
            a = _dot_nt(hb, wg_ref[c0:c1, :])
            b = _dot_nt(hb, wu_ref[c0:c1, :])
            s_ref[:, c0:c1] = (a * jax.nn.sigmoid(a) * b).astype(LOW)
            a_ref[:, c0:c1] = a.astype(LOW)
            b_ref[:, c0:c1] = b.astype(LOW)
        xo_ref[...] = xv + 0.5 * _dot(s_ref[...], wd_ref[...])

    tok = pl.BlockSpec((ts, d), lambda t: (t, 0))
    hid = pl.BlockSpec((ts, fc), lambda t: (t, 0))
    tok_out = jax.ShapeDtypeStruct((s_len, d), F32)
    h_out = jax.ShapeDtypeStruct((s_len, d), LOW)
    hid_out = jax.ShapeDtypeStruct((s_len, fc), LOW)
    return _call(
        body,
        name=name,
        grid=(s_len // ts,),
        in_specs=[tok, pl.BlockSpec((1, d), lambda t: (0, 0)) if first else tok,
                  _resident(wgt), _resident(wut), _resident(wd)],
        out_specs=[tok] + ([tok] if first else []) + [hid, hid],
        out_shape=[tok_out] + ([h_out] if first else []) + [hid_out, hid_out],
        scratch_shapes=[pltpu.VMEM((ts, fc), LOW)],
        args=(x, g if first else h, wgt, wut, wd),
        exchange=exchange,
    )


def _slabs(width, slab=256):
    return [(c0, min(c0 + slab, width)) for c0 in range(0, width, slab)]


def _ffn_bwdw(df, a, b, h, wd, name, exchange=None):
    s_len, d = df.shape
    f_len = wd.shape[0]
    tm = _pick(f_len, WGRAD_ROW_CANDIDATES)
    tk = min(WGRAD_TOK_TILE, s_len)
    n_k = s_len // tk

    def body(df_ref, a_ref, b_ref, h_ref, wd_ref, da_ref, db_ref, dwd_ref, dwg_ref, dwu_ref,
             s_ref, acc_d, acc_g, acc_u):
        k = pl.program_id(1)

        @pl.when(k == 0)
        def _():
            acc_d[...] = jnp.zeros_like(acc_d)
            acc_g[...] = jnp.zeros_like(acc_g)
            acc_u[...] = jnp.zeros_like(acc_u)

        wdv = wd_ref[...]
        for r0, r1 in _slabs(tk, BWD_ROW_SLAB):
            ds = _dot_nt(df_ref[r0:r1, :], wdv)
            av = a_ref[r0:r1, :].astype(F32)
            bv = b_ref[r0:r1, :].astype(F32)
            sig = jax.nn.sigmoid(av)
            silu = av * sig
            s_ref[r0:r1, :] = (silu * bv).astype(LOW)
            da_ref[r0:r1, :] = (ds * bv * (sig * (1.0 + av * (1.0 - sig)))).astype(LOW)
            db_ref[r0:r1, :] = (ds * silu).astype(LOW)
        hv = h_ref[...]
        acc_d[...] += _dot_tn(s_ref[...], df_ref[...])
        acc_g[...] += _dot_tn(da_ref[...], hv)
        acc_u[...] += _dot_tn(db_ref[...], hv)

        @pl.when(k == n_k - 1)
        def _():
            dwd_ref[...] = acc_d[...].astype(LOW)
            dwg_ref[...] = acc_g[...].astype(LOW)
            dwu_ref[...] = acc_u[...].astype(LOW)

    hid = pl.BlockSpec((tk, tm), lambda i, k: (k, i))
    tok = pl.BlockSpec((tk, d), lambda i, k: (k, 0))
    wrow = pl.BlockSpec((tm, d), lambda i, k: (i, 0))
    return _call(
        body,
        name=name,
        grid=(f_len // tm, n_k),
        in_specs=[tok, hid, hid, tok, wrow],
        out_specs=[hid, hid, wrow, wrow, wrow],
        out_shape=[jax.ShapeDtypeStruct((s_len, f_len), LOW)] * 2 + [jax.ShapeDtypeStruct((f_len, d), LOW)] * 3,
        scratch_shapes=[pltpu.VMEM((tk, tm), LOW)] + [pltpu.VMEM((tm, d), F32)] * 3,
        args=(df, a, b, h, wd),
        exchange=exchange,
    )


def _ffn_dx(dxo, x, g, parts, name, exchange=None):
    s_len, d = x.shape
    ts = min(TOK_TILE, s_len)
    n_p = len(parts)

    def body(dxo_ref, x_ref, g_ref, *refs):
        dxi_ref, dg_ref = refs[4 * n_p:]

        @pl.when(pl.program_id(0) == 0)
        def _():
            dg_ref[...] = jnp.zeros_like(dg_ref)

        dh = None
        for p in range(n_p):
            da_ref, db_ref, wg_ref, wu_ref = refs[4 * p:4 * p + 4]
            part = _dot(da_ref[...], wg_ref[...]) + _dot(db_ref[...], wu_ref[...])
            dh = part if dh is None else dh + part
        dx, dgp = _rms_bwd(dh, x_ref[...], g_ref[...])
        dxi_ref[...] = dxo_ref[...] + dx
        dg_ref[...] += dgp

    tok = pl.BlockSpec((ts, d), lambda t: (t, 0))
    vec = pl.BlockSpec((1, d), lambda t: (0, 0))
    part_specs, part_args = [], []
    for da, db, wgt, wut in parts:
        hid = pl.BlockSpec((ts, da.shape[1]), lambda t: (t, 0))
        part_specs += [hid, hid, _resident(wgt), _resident(wut)]
        part_args += [da, db, wgt, wut]
    return _call(
        body,
        name=name,
        grid=(s_len // ts,),
        in_specs=[tok, tok, vec] + part_specs,
        out_specs=[tok, vec],
        out_shape=[jax.ShapeDtypeStruct((s_len, d), F32), jax.ShapeDtypeStruct((1, d), F32)],
        args=(dxo, x, g, *part_args),
        exchange=exchange,
    )


def _wgrad_tn(xm, ym, tn, stacked, name, exchange=None):
    s_len, m = xm.shape
    n = ym.shape[1]
    tk = min(WGRAD_TOK_TILE, s_len)
    n_k = s_len // tk

    def body(x_ref, y_ref, o_ref, acc):
        k = pl.program_id(1)

        @pl.when(k == 0)
        def _():
            acc[...] = jnp.zeros_like(acc)

        acc[...] += _dot_tn(x_ref[...].astype(LOW), y_ref[...].astype(LOW))

        @pl.when(k == n_k - 1)
        def _():
            o_ref[...] = acc[...].astype(LOW)

    if stacked:
        out_spec = pl.BlockSpec((None, m, tn), lambda j, k: (j, 0, 0))
        out_shape = jax.ShapeDtypeStruct((n // tn, m, tn), LOW)
    else:
        out_spec = pl.BlockSpec((m, tn), lambda j, k: (0, j))
        out_shape = jax.ShapeDtypeStruct((m, n), LOW)
    return _call(
        body,
        name=name,
        grid=(n // tn, n_k),
        in_specs=[pl.BlockSpec((tk, m), lambda j, k: (k, 0)), pl.BlockSpec((tk, tn), lambda j, k: (k, j))],
        out_specs=[out_spec],
        out_shape=[out_shape],
        scratch_shapes=[pltpu.VMEM((m, tn), F32)],
        args=(xm, ym),
        exchange=exchange,
    )


def _mix_parts(ext_ref, cw, ts, row0):
    dc = D_CONV

    def back(off, c0, c1):
        return ext_ref[HALO - off:HALO - off + ts, c0:c1]

    v, gate_b, gate_c = back(0, 0, dc), back(0, dc, 2 * dc), back(0, 2 * dc, 3 * dc)
    z0 = gate_c * v
    z1 = back(1, 2 * dc, 3 * dc) * back(1, 0, dc)
    z2 = back(2, 2 * dc, 3 * dc) * back(2, 0, dc)
    conv = cw[2:3, :] * z0 + cw[1:2, :] * z1 + cw[0:1, :] * z2
    rows = row0 + lax.broadcasted_iota(jnp.int32, (ts, 1), 0)
    pooled, inv_count = [], []
    for grp, w in enumerate(POOL_WINDOWS):
        c0 = 3 * dc + POOL_GC * grp
        u = back(0, c0, c0 + POOL_GC)
        acc = u
        for j in range(1, w):
            acc = acc + back(j, c0, c0 + POOL_GC)
        inv = 1.0 / jnp.minimum(rows + 1, w).astype(F32)
        pooled.append(acc * inv - u)
        inv_count.append(inv)
    return v, gate_b, gate_c, z0, z1, z2, conv, pooled, inv_count


def _mix_fwd(x, g, w_in, conv_w, pool_w, pool_scale, w_out, name, exchange=None):
    s_len, d = x.shape
    n_blk, _, wcols = w_in.shape
    p_len = n_blk * wcols
    d_mix = w_out.shape[0]
    ts = min(MIX_TOK_TILE, s_len)
    dc = D_CONV

    def body(x_ref, g_ref, win_ref, cw_ref, pw_ref, ps_ref, wout_ref, x2_ref, h_ref, proj_ref, ext_ref, cat_ref):
        t = pl.program_id(0)

        @pl.when(t == 0)
        def _():
            ext_ref[0:HALO, :] = jnp.zeros((HALO, p_len), F32)

        xv = x_ref[...]
        hb = (xv * _rms_scale(xv) * g_ref[...]).astype(LOW)
        h_ref[...] = hb
        for k in range(n_blk):
            ext_ref[HALO:HALO + ts, wcols * k:wcols * (k + 1)] = _dot(hb, win_ref[k])
        proj_ref[...] = ext_ref[HALO:HALO + ts, :]

        _, gate_b, _, _, _, _, conv, pooled, _ = _mix_parts(ext_ref, cw_ref[...], ts, t * ts)
        cat_ref[:, 0:dc] = (gate_b * conv).astype(LOW)
        for grp in range(len(POOL_WINDOWS)):
            c0 = POOL_GC * grp
            lin = _dot(pooled[grp].astype(LOW), pw_ref[grp])
            cat_ref[:, dc + c0:dc + c0 + POOL_GC] = (lin * ps_ref[:, c0:c0 + POOL_GC]).astype(LOW)
        x2_ref[...] = xv + _dot(cat_ref[...], wout_ref[...])
        ext_ref[0:HALO, :] = ext_ref[ts:ts + HALO, :]

    tok = pl.BlockSpec((ts, d), lambda t: (t, 0))

    def whole(arr):
        return pl.BlockSpec(arr.shape, lambda t: (0,) * arr.ndim)

    return _call(
        body,
        name=name,
        grid=(s_len // ts,),
        in_specs=[tok, whole(g), _resident(w_in), whole(conv_w), whole(pool_w), whole(pool_scale), _resident(w_out)],
        out_specs=[tok, tok, pl.BlockSpec((ts, p_len), lambda t: (t, 0))],
        out_shape=[
            jax.ShapeDtypeStruct((s_len, d), F32),
            jax.ShapeDtypeStruct((s_len, d), LOW),
            jax.ShapeDtypeStruct((s_len, p_len), F32),
        ],
        scratch_shapes=[pltpu.VMEM((ts + HALO, p_len), F32), pltpu.VMEM((ts, d_mix), LOW)],
        args=(x, g, w_in, conv_w, pool_w, pool_scale, w_out),
        exchange=exchange,
    )


def _mix_bwd(dx2, x, proj, g, w_in, conv_w, pool_w, pool_scale, w_out, name, exchange=None):
    s_len, d = x.shape
    n_blk, _, wcols = w_in.shape
    p_len = n_blk * wcols
    d_mix = w_out.shape[0]
    ts = min(MIX_TOK_TILE, s_len)
    n_t = s_len // ts
    dc = D_CONV
    n_grp = len(POOL_WINDOWS)

    def body(dx2_ref, x_ref, proj_ref, halo_ref, g_ref, win_ref, cw_ref, pw_ref, ps_ref, wout_ref,
             dx_ref, dproj_ref, cat_ref, dg_ref, dcw_ref, dpw_ref, dps_ref, df_ref, ext_ref, fut_ref):
        i = pl.program_id(0)
        t = n_t - 1 - i

        @pl.when(i == 0)
        def _():
            dg_ref[...] = jnp.zeros_like(dg_ref)
            dcw_ref[...] = jnp.zeros_like(dcw_ref)
            dpw_ref[...] = jnp.zeros_like(dpw_ref)
            dps_ref[...] = jnp.zeros_like(dps_ref)
            fut_ref[ts:ts + HALO, :] = jnp.zeros((HALO, d_mix), F32)

        ext_ref[HALO:HALO + ts, :] = proj_ref[...]

        @pl.when(t == 0)
        def _():
            ext_ref[0:HALO, :] = jnp.zeros((HALO, p_len), F32)

        @pl.when(t > 0)
        def _():
            ext_ref[0:HALO, :] = halo_ref[...]

        cw = cw_ref[...]
        v, gate_b, gate_c, z0, z1, z2, conv, pooled, inv_count = _mix_parts(ext_ref, cw, ts, t * ts)
        dx2 = dx2_ref[...]
        dcat = _dot_nt(dx2.astype(LOW), wout_ref[...])

        dy_a = dcat[:, 0:dc]
        dconv = dy_a * gate_b
        fut_ref[0:ts, 0:dc] = dconv
        cat_ref[:, 0:dc] = (gate_b * conv).astype(LOW)
        dproj_ref[:, dc:2 * dc] = (dy_a * conv).astype(LOW)
        dcw_ref[2:3, :] += jnp.sum(dconv * z0, axis=0, keepdims=True)
        dcw_ref[1:2, :] += jnp.sum(dconv * z1, axis=0, keepdims=True)
        dcw_ref[0:1, :] += jnp.sum(dconv * z2, axis=0, keepdims=True)

        dpool = []
        for grp in range(n_grp):
            c0 = POOL_GC * grp
            pooled_b = pooled[grp].astype(LOW)
            lin = _dot(pooled_b, pw_ref[grp])
            dy_b = dcat[:, dc + c0:dc + c0 + POOL_GC]
            scale = ps_ref[:, c0:c0 + POOL_GC]
            cat_ref[:, dc + c0:dc + c0 + POOL_GC] = (lin * scale).astype(LOW)
            dps_ref[:, c0:c0 + POOL_GC] += jnp.sum(dy_b * lin, axis=0, keepdims=True)
            dlin = (dy_b * scale).astype(LOW)
            dpw_ref[grp] += _dot_tn(pooled_b, dlin)
            dpool.append(_dot_nt(dlin, pw_ref[grp]))
            fut_ref[0:ts, dc + c0:dc + c0 + POOL_GC] = dpool[grp] * inv_count[grp]

        def ahead(off, c0, c1):
            return fut_ref[off:off + ts, c0:c1]

        dz = cw[2:3, :] * ahead(0, 0, dc) + cw[1:2, :] * ahead(1, 0, dc) + cw[0:1, :] * ahead(2, 0, dc)
        dproj_ref[:, 0:dc] = (dz * gate_c).astype(LOW)
        dproj_ref[:, 2 * dc:3 * dc] = (dz * v).astype(LOW)
        for grp, w in enumerate(POOL_WINDOWS):
            c0 = dc + POOL_GC * grp
            acc = ahead(0, c0, c0 + POOL_GC)
            for j in range(1, w):
                acc = acc + ahead(j, c0, c0 + POOL_GC)
            dproj_ref[:, 2 * dc + c0:2 * dc + c0 + POOL_GC] = (acc - dpool[grp]).astype(LOW)

        dh = _dot_nt(dproj_ref[:, 0:wcols], win_ref[0])
        for k in range(1, n_blk):
            dh += _dot_nt(dproj_ref[:, wcols * k:wcols * (k + 1)], win_ref[k])
        dx, dgp = _rms_bwd(dh, x_ref[...], g_ref[...])
        dx = dx2 + dx
        dx_ref[...] = dx
        df_ref[...] = (0.5 * dx).astype(LOW)
        dg_ref[...] += dgp
        fut_ref[ts:ts + HALO, :] = fut_ref[0:HALO, :]

    tok = pl.BlockSpec((ts, d), lambda i: (n_t - 1 - i, 0))
    halo = pl.BlockSpec((HALO, p_len), lambda i: (jnp.maximum((n_t - 1 - i) * (ts // HALO) - 1, 0), 0))

    def whole(arr):
        return pl.BlockSpec(arr.shape, lambda i: (0,) * arr.ndim)

    return _call(
        body,
        name=name,
        grid=(n_t,),
        in_specs=[tok, tok, pl.BlockSpec((ts, p_len), lambda i: (n_t - 1 - i, 0)), halo,
                  whole(g), _resident(w_in), whole(conv_w), whole(pool_w), whole(pool_scale), _resident(w_out)],
        out_specs=[tok, pl.BlockSpec((ts, p_len), lambda i: (n_t - 1 - i, 0)),
                   pl.BlockSpec((ts, d_mix), lambda i: (n_t - 1 - i, 0)),
                   whole(g), whole(conv_w), whole(pool_w), whole(pool_scale), tok],
        out_shape=[
            jax.ShapeDtypeStruct((s_len, d), F32),
            jax.ShapeDtypeStruct((s_len, p_len), LOW),
            jax.ShapeDtypeStruct((s_len, d_mix), LOW),
            jax.ShapeDtypeStruct(g.shape, F32),
            jax.ShapeDtypeStruct(conv_w.shape, F32),
            jax.ShapeDtypeStruct(pool_w.shape, F32),
            jax.ShapeDtypeStruct(pool_scale.shape, F32),
            jax.ShapeDtypeStruct((s_len, d), LOW),
        ],
        scratch_shapes=[pltpu.VMEM((ts + HALO, p_len), F32), pltpu.VMEM((ts + HALO, d_mix), F32)],
        args=(dx2, x, proj, proj, g, w_in, conv_w, pool_w, pool_scale, w_out),
        exchange=exchange,
    )


def _loss_head(x, g, target, name):
    s_len, d = x.shape
    ts = min(TOK_TILE, s_len)

    def body(x_ref, g_ref, tgt_ref, loss_ref, dx_ref, dg_ref, df_ref):
        @pl.when(pl.program_id(0) == 0)
        def _():
            loss_ref[...] = jnp.zeros_like(loss_ref)
            dg_ref[...] = jnp.zeros_like(dg_ref)

        xv, gv = x_ref[...], g_ref[...]
        err = xv * _rms_scale(xv) * gv - tgt_ref[...]
        loss_ref[...] += 0.5 * jnp.sum(jnp.mean(err * err, axis=-1, keepdims=True), axis=0, keepdims=True)
        dx, dgp = _rms_bwd(err * (1.0 / d), xv, gv)
        dx_ref[...] = dx
        df_ref[...] = (0.5 * dx).astype(LOW)
        dg_ref[...] += dgp

    tok = pl.BlockSpec((ts, d), lambda t: (t, 0))
    vec = pl.BlockSpec((1, d), lambda t: (0, 0))
    return pl.pallas_call(
        body,
        name=name,
        grid=(s_len // ts,),
        in_specs=[tok, vec, tok],
        out_specs=[pl.BlockSpec((1, 128), lambda t: (0, 0)), tok, vec, tok],
        out_shape=[
            jax.ShapeDtypeStruct((1, 128), F32),
            jax.ShapeDtypeStruct((s_len, d), F32),
            jax.ShapeDtypeStruct((1, d), F32),
            jax.ShapeDtypeStruct((s_len, d), LOW),
        ],
        compiler_params=_params("arbitrary"),
    )(x, g, target)


def _row_tile(rows, cols, stack_bytes):
    budget = 20 * 1024 * 1024
    per_row = cols * (4 * 7 + stack_bytes)
    for tr in (rows, 512, 256, 176, 128, 64, 32, 16, 8):
        if rows % tr == 0 and tr % 8 == 0 and tr * per_row * 2 <= budget:
            return tr
    return rows


def _sum_stack(stack, name):
    n, r, c = stack.shape
    tr = _row_tile(r, c, n * stack.dtype.itemsize)

    def body(s_ref, o_ref):
        acc = s_ref[0].astype(F32)
        for k in range(1, n):
            acc = acc + s_ref[k].astype(F32)
        o_ref[...] = acc

    return pl.pallas_call(
        body,
        name=name,
        grid=(r // tr,),
        in_specs=[pl.BlockSpec((n, tr, c), lambda i: (0, i, 0))],
        out_specs=pl.BlockSpec((tr, c), lambda i: (i, 0)),
        out_shape=jax.ShapeDtypeStruct((r, c), F32),
        compiler_params=_params("arbitrary"),
    )(stack)


def _adamw(stacks, w, m, v, name):
    stacks = list(stacks) if isinstance(stacks, (list, tuple)) else [stacks]
    r, c = w.shape
    n = stacks[0].shape[0]
    part_rows = [st.shape[1] for st in stacks]
    assert sum(part_rows) == r and all(st.shape[0] == n for st in stacks)
    first_row = [sum(part_rows[:j]) for j in range(len(stacks))]
    tc = next(t for t in (512, 256, 128) if c % t == 0)
    c1 = 1.0 - ADAM_B1 ** ADAM_STEP
    c2 = 1.0 - ADAM_B2 ** ADAM_STEP

    def body(*refs):
        s_refs = refs[:len(stacks)]
        w_ref, m_ref, v_ref, g_ref, d_ref, mo_ref, vo_ref = refs[len(stacks):]
        for s_ref, r0, nr in zip(s_refs, first_row, part_rows):
            gv = s_ref[0].astype(F32)
            for k in range(1, n):
                gv = gv + s_ref[k].astype(F32)
            mn = ADAM_B1 * m_ref[r0:r0 + nr, :] + (1.0 - ADAM_B1) * gv
            vn = ADAM_B2 * v_ref[r0:r0 + nr, :] + (1.0 - ADAM_B2) * (gv * gv)
            g_ref[r0:r0 + nr, :] = gv
            mo_ref[r0:r0 + nr, :] = mn
            vo_ref[r0:r0 + nr, :] = vn
            d_ref[r0:r0 + nr, :] = -ADAM_LR * ((mn / c1) / (jnp.sqrt(vn / c2) + ADAM_EPS)
                                               + ADAM_WD * w_ref[r0:r0 + nr, :])

    blk = pl.BlockSpec((r, tc), lambda i: (0, i))
    return pl.pallas_call(
        body,
        name=name,
        grid=(c // tc,),
        in_specs=[pl.BlockSpec((n, nr, tc), lambda i: (0, 0, i)) for nr in part_rows] + [blk, blk, blk],
        out_specs=[blk] * 4,
        out_shape=[jax.ShapeDtypeStruct((r, c), F32)] * 4,
        compiler_params=_params("arbitrary"),
    )(*stacks, w, m, v)


def _to_sheet(parts):
    sheets, spans = [], []
    row = 0
    for p in parts:
        flat = p.reshape(-1).astype(F32)
        rows = -(-flat.shape[0] // 1024) * 8
        flat = jnp.pad(flat, (0, rows * 128 - flat.shape[0]))
        sheets.append(flat.reshape(rows, 128))
        spans.append((row, p.size, p.shape))
        row += rows
    return jnp.concatenate(sheets, axis=0), spans


def _from_sheet(sheet, spans):
    out = []
    for row, size, shape in spans:
        rows = -(-size // 1024) * 8
        out.append(sheet[row:row + rows].reshape(-1)[:size].reshape(shape))
    return out


def kernel(x, norm_ffn1, ffn1_w_gate, ffn1_w_up, ffn1_w_down, norm_mix, w_in, conv_w, pool_w, pool_scale, w_out, norm_ffn2, ffn2_w_gate, ffn2_w_up, ffn2_w_down, norm_final, loss_target, m_norm_ffn1, m_ffn1_w_gate, m_ffn1_w_up, m_ffn1_w_down, m_norm_mix, m_w_in, m_conv_w, m_pool_w, m_pool_scale, m_w_out, m_norm_ffn2, m_ffn2_w_gate, m_ffn2_w_up, m_ffn2_w_down, m_norm_final, v_norm_ffn1, v_ffn1_w_gate, v_ffn1_w_up, v_ffn1_w_down, v_norm_mix, v_w_in, v_conv_w, v_pool_w, v_pool_scale, v_w_out, v_norm_ffn2, v_ffn2_w_gate, v_ffn2_w_up, v_ffn2_w_down, v_norm_final):
    me = 4 * lax.axis_index("x") + 2 * lax.axis_index("y") + lax.axis_index("c")
    xs, tgt = x[0], loss_target[0]
    s_len, d = xs.shape
    f_shard = ffn1_w_down.shape[1]
    conv_shard = conv_w.shape[2]

    def low_t(wt):
        return wt[0].T.astype(LOW)

    def by_dev(gw):
        return gw.reshape(N_DEV, -1, d)

    conv_tile = jnp.zeros((8, 128), F32).at[0:conv_w.shape[1], 0:conv_shard].set(conv_w[0])
    pool_w_low = pool_w[0].astype(LOW)

    rows_a = -(-f_shard // 64) * 32

    def parts_of(w_gate, w_up, w_down):
        shards = [low_t(w_gate), low_t(w_up), w_down[0].astype(LOW)]
        return [s[:rows_a] for s in shards], [s[rows_a:] for s in shards]

    def rows_flat(stacks):
        return [st.reshape(-1, d) for st in stacks]

    def gather(shards):
        return _Exchange(shards, [False] * len(shards), relay_at=1.0)

    def scatter(pairs):
        return _Exchange(pairs, [True] * len(pairs), among_chips=True)

    w1a_shards, w1b_shards = parts_of(ffn1_w_gate, ffn1_w_up, ffn1_w_down)
    w2a_shards, w2b_shards = parts_of(ffn2_w_gate, ffn2_w_up, ffn2_w_down)

    wg1a, wu1a, wd1a = rows_flat(_exchange(w1a_shards, [False] * 3, "gather_ffn1_a", relay=True))
    xa, h1, a1a, b1a, *w1b = _ffn_fwd(xs, norm_ffn1, None, wg1a, wu1a, wd1a, "ffn1_fwd_a",
                                      exchange=gather(w1b_shards))
    wg1b, wu1b, wd1b = rows_flat(w1b)
    x1, a1b, b1b, w_in_full, w_out_full, conv_tiles = _ffn_fwd(
        xa, None, h1, wg1b, wu1b, wd1b, "ffn1_fwd_b",
        exchange=gather([w_in[0].astype(LOW), w_out[0].astype(LOW), conv_tile]))
    w_out_full = w_out_full.reshape(-1, d)
    conv_full = jnp.concatenate([conv_tiles[k, 0:conv_w.shape[1], 0:conv_shard] for k in range(N_DEV)], axis=1)
    x2, h2, proj, *w2a = _mix_fwd(x1, norm_mix, w_in_full, conv_full, pool_w_low, pool_scale, w_out_full, "mix_fwd",
                                  exchange=gather(w2a_shards))
    wg2a, wu2a, wd2a = rows_flat(w2a)
    xb, h3, a2a, b2a, *w2b = _ffn_fwd(x2, norm_ffn2, None, wg2a, wu2a, wd2a, "ffn2_fwd_a",
                                      exchange=gather(w2b_shards))
    wg2b, wu2b, wd2b = rows_flat(w2b)
    x3, a2b, b2b = _ffn_fwd(xb, None, h3, wg2b, wu2b, wd2b, "ffn2_fwd_b")
    loss_row, dx3, dg_final, df3 = _loss_head(x3, norm_final.reshape(1, d), tgt, "loss_head")

    da2a, db2a, dwd, dwg, dwu = _ffn_bwdw(df3, a2a, b2a, h3, wd2a, "ffn2_bwdw_a")
    pairs = _pair_sum([by_dev(dwg), by_dev(dwu), by_dev(dwd)], "pair_sum_ffn2_a")
    da2b, db2b, dwd, dwg, dwu, *got_2a = _ffn_bwdw(df3, a2b, b2b, h3, wd2b, "ffn2_bwdw_b", exchange=scatter(pairs))
    pairs = _pair_sum([by_dev(dwg), by_dev(dwu), by_dev(dwd)], "pair_sum_ffn2_b")
    dx2, dg_ffn2, *got_2b = _ffn_dx(dx3, x2, norm_ffn2, [(da2a, db2a, wg2a, wu2a), (da2b, db2b, wg2b, wu2b)],
                                    "ffn2_dx", exchange=scatter(pairs))
    dx1, dproj, cat, dg_mix, dconv, dpool_w, dpool_scale, df1 = _mix_bwd(
        dx2, x1, proj, norm_mix, w_in_full, conv_full, pool_w_low, pool_scale, w_out_full, "mix_bwd")
    small_parts = [dg_mix, dg_ffn2, dg_final, dconv, dpool_w, dpool_scale]
    small_sheet, spans = _to_sheet(small_parts)
    dw_in, got_small = _wgrad_tn(h2, dproj, W_IN_SHARD, True, "w_in_wgrad",
                                 exchange=_Exchange([small_sheet], [False]))
    (dw_out,) = _wgrad_tn(cat, dx2, d, False, "w_out_wgrad")
    pairs = _pair_sum([dw_in, by_dev(dw_out)], "pair_sum_mix")
    da1a, db1a, dwd, dwg, dwu, got_in, got_out = _ffn_bwdw(
        df1, a1a, b1a, h1, wd1a, "ffn1_bwdw_a", exchange=scatter(pairs))
    pairs = _pair_sum([by_dev(dwg), by_dev(dwu), by_dev(dwd)], "pair_sum_ffn1_a")
    da1b, db1b, dwd, dwg, dwu, *got_1a = _ffn_bwdw(df1, a1b, b1b, h1, wd1b, "ffn1_bwdw_b", exchange=scatter(pairs))
    pairs = _pair_sum([by_dev(dwg), by_dev(dwu), by_dev(dwd)], "pair_sum_ffn1_b")
    dx0, dg_ffn1, *got_1b = _ffn_dx(dx1, xs, norm_ffn1, [(da1a, db1a, wg1a, wu1a), (da1b, db1b, wg1b, wu1b)],
                                    "ffn1_dx", exchange=scatter(pairs))
    (got_n1,) = _exchange([dg_ffn1.reshape(8, 128)], [False], "gather_dnorm_ffn1")

    outs = {}

    def update(name, stacks, w, m, v):
        outs[name] = _adamw(stacks, w[0], m[0], v[0], "adamw_" + name)

    def update_t(name, stacks, w, m, v):
        res = _adamw(stacks, w[0].T, m[0].T, v[0].T, "adamw_" + name)
        outs[name] = tuple(r.T for r in res)

    update_t("ffn1_w_gate", [got_1a[0], got_1b[0]], ffn1_w_gate, m_ffn1_w_gate, v_ffn1_w_gate)
    update_t("ffn1_w_up", [got_1a[1], got_1b[1]], ffn1_w_up, m_ffn1_w_up, v_ffn1_w_up)
    update("ffn1_w_down", [got_1a[2], got_1b[2]], ffn1_w_down, m_ffn1_w_down, v_ffn1_w_down)
    update("w_in", got_in, w_in, m_w_in, v_w_in)
    update("w_out", got_out, w_out, m_w_out, v_w_out)
    update_t("ffn2_w_gate", [got_2a[0], got_2b[0]], ffn2_w_gate, m_ffn2_w_gate, v_ffn2_w_gate)
    update_t("ffn2_w_up", [got_2a[1], got_2b[1]], ffn2_w_up, m_ffn2_w_up, v_ffn2_w_up)
    update("ffn2_w_down", [got_2a[2], got_2b[2]], ffn2_w_down, m_ffn2_w_down, v_ffn2_w_down)

    g_small = _from_sheet(_sum_stack(got_small, "sum_small"), spans)
    g_norm_ffn1 = _sum_stack(got_n1, "sum_dnorm_ffn1").reshape(norm_ffn1.shape)
    g_conv = lax.dynamic_slice_in_dim(g_small[3], me * conv_shard, conv_shard, axis=1)
    small_names = ["norm_ffn1", "norm_mix", "norm_ffn2", "norm_final", "conv_w", "pool_w", "pool_scale"]
    small_g = [g_norm_ffn1, g_small[0], g_small[1], g_small[2].reshape(norm_final.shape), g_conv[None],
               g_small[4][None], g_small[5]]
    small_w = [norm_ffn1, norm_mix, norm_ffn2, norm_final, conv_w, pool_w, pool_scale]
    small_m = [m_norm_ffn1, m_norm_mix, m_norm_ffn2, m_norm_final, m_conv_w, m_pool_w, m_pool_scale]
    small_v = [v_norm_ffn1, v_norm_mix, v_norm_ffn2, v_norm_final, v_conv_w, v_pool_w, v_pool_scale]
    g_sheet, spans_u = _to_sheet(small_g)
    w_sheet, _ = _to_sheet(small_w)
    m_sheet, _ = _to_sheet(small_m)
    v_sheet, _ = _to_sheet(small_v)
    upd = _adamw(g_sheet[None], w_sheet, m_sheet, v_sheet, "adamw_small")
    small_out = [_from_sheet(u, spans_u) for u in upd]
    for k, nm in enumerate(small_names):
        outs[nm] = tuple(small_out[j][k] for j in range(4))

    loss = lax.psum(loss_row[0, 0], ("x", "y", "c"))
    order = ["norm_ffn1", "ffn1_w_gate", "ffn1_w_up", "ffn1_w_down", "norm_mix", "w_in", "conv_w", "pool_w",
             "pool_scale", "w_out", "norm_ffn2", "ffn2_w_gate", "ffn2_w_up", "ffn2_w_down", "norm_final"]
    big = {"ffn1_w_gate", "ffn1_w_up", "ffn1_w_down", "w_in", "w_out", "ffn2_w_gate", "ffn2_w_up", "ffn2_w_down"}

    def leaf(nm, j):
        val = outs[nm][j]
        return val[None] if nm in big else val

    return (loss, dx0[None],
            *[leaf(nm, 0) for nm in order], *[leaf(nm, 1) for nm in order],
            *[leaf(nm, 2) for nm in order], *[leaf(nm, 3) for nm in order])
```

```python
import jax
import jax.numpy as jnp
from jax import lax
from jax.experimental import pallas as pl
from jax.experimental.pallas import tpu as pltpu

F32 = jnp.float32
LOW = jnp.bfloat16

N_DEV = 8
EPS = 1e-6
D_CONV = 512
POOL_WINDOWS = (2, 4, 8, 16)
POOL_GC = 128
HALO = 16
W_IN_SHARD = 256

ADAM_LR = 0.001
ADAM_B1 = 0.9
ADAM_B2 = 0.999
ADAM_EPS = 1e-08
ADAM_WD = 0.01
ADAM_STEP = 10

VMEM_LIMIT_BYTES = 56 * 1024 * 1024
TOK_TILE = 512
MIX_TOK_TILE = 256
WGRAD_TOK_TILE = 4096
WGRAD_ROW_CANDIDATES = (256, 128)
BWD_ROW_SLAB = 256


def _params(*sem):
    return pltpu.CompilerParams(dimension_semantics=sem, vmem_limit_bytes=VMEM_LIMIT_BYTES)


def _resident(arr):
    return pl.BlockSpec(arr.shape, lambda *_: (0,) * arr.ndim, pipeline_mode=pl.Buffered(1))


def _pick(n, candidates):
    for c in candidates:
        if n % c == 0:
            return c
    raise ValueError(f"no tile in {candidates} divides {n}")


def _dot(a, b):
    return lax.dot_general(a, b, (((1,), (0,)), ((), ())), preferred_element_type=F32)


def _dot_nt(a, b):
    return lax.dot_general(a, b, (((1,), (1,)), ((), ())), preferred_element_type=F32)


def _dot_tn(a, b):
    return lax.dot_general(a, b, (((0,), (0,)), ((), ())), preferred_element_type=F32)


def _rms_scale(x):
    return lax.rsqrt(jnp.mean(x * x, axis=-1, keepdims=True) + EPS)


def _rms_bwd(dy, x, g):
    r = _rms_scale(x)
    xhat = x * r
    gdy = dy * g
    dx = r * (gdy - xhat * jnp.mean(gdy * xhat, axis=-1, keepdims=True))
    return dx, jnp.sum(dy * xhat, axis=0, keepdims=True)


class _Exchange:
    CHIPS = (2, 4, 6)

    def __init__(self, arrays, sliced, relay_at=None, among_chips=False):
        assert relay_at is None or not any(sliced)
        assert not among_chips or (all(sliced) and relay_at is None)
        self.relay_at, self.among_chips = relay_at, among_chips
        self.peers = self.CHIPS if among_chips else ((1, 2, 4) if relay_at is not None else tuple(range(1, N_DEV)))
        self.arrays, self.sliced, self.n = list(arrays), list(sliced), len(arrays)
        self.out_shape = [jax.ShapeDtypeStruct(arr.shape if sl else (N_DEV,) + arr.shape, arr.dtype)
                          for arr, sl in zip(arrays, sliced)]
        self.specs = [pl.BlockSpec(memory_space=pl.ANY)] * self.n
        self.scratch_shapes = [pltpu.SemaphoreType.DMA((self.n, N_DEV)),
                               pltpu.SemaphoreType.DMA((self.n, N_DEV)),
                               pltpu.SemaphoreType.DMA((self.n,))]

    HALF_VIA = ((4, 2, 5), (2, 4, 7))

    def _halves(self, a):
        rows = self.arrays[a].shape[0]
        if rows % 32:
            return ((0, rows), None)
        return ((0, rows // 2), (rows // 2, rows // 2))

    def _copies(self, ins, outs, sems):
        send_sems, recv_sems, local_sems = sems
        sliced = self.sliced
        mx, my, mc = lax.axis_index("x"), lax.axis_index("y"), lax.axis_index("c")
        me = 2 * mx + my if self.among_chips else 4 * mx + 2 * my + mc

        def peer(m):
            px = lax.rem(mx + ((m >> 2) & 1), 2)
            py = lax.rem(my + ((m >> 1) & 1), 2)
            pc = lax.rem(mc + (m & 1), 2)
            return (px, py, pc), (2 * px + py if self.among_chips else 4 * px + 2 * py + pc)

        def remote(a, m, arriving):
            pid, pflat = peer(m)
            return pltpu.make_async_remote_copy(
                src_ref=ins[a].at[pflat] if sliced[a] else ins[a],
                dst_ref=outs[a].at[pflat if arriving else me],
                send_sem=send_sems.at[a, m - 1],
                recv_sem=recv_sems.at[a, m - 1],
                device_id=pid,
                device_id_type=pl.DeviceIdType.MESH,
            )

        def local(a):
            return pltpu.make_async_copy(ins[a].at[me] if sliced[a] else ins[a], outs[a].at[me], local_sems.at[a])

        def passed_on(a, m):
            _, origin = peer(m)
            sibling, _ = peer(1)
            return pltpu.make_async_remote_copy(
                src_ref=outs[a].at[origin],
                dst_ref=outs[a].at[origin],
                send_sem=send_sems.at[a, m],
                recv_sem=recv_sems.at[a, m],
                device_id=sibling,
                device_id_type=pl.DeviceIdType.MESH,
            )

        def half_on(a, h, arriving):
            via, to, column = self.HALF_VIA[h]
            r0, nr = self._halves(a)[h]
            _, origin = peer(6 if arriving else via)
            rows = outs[a].at[origin].at[pl.ds(r0, nr)]
            return pltpu.make_async_remote_copy(
                src_ref=rows, dst_ref=rows, send_sem=send_sems.at[a, column], recv_sem=recv_sems.at[a, column],
                device_id=peer(to)[0], device_id_type=pl.DeviceIdType.MESH)

        return remote, local, passed_on, half_on

    def start(self, ins, outs, sems):
        remote, local, _, _ = self._copies(ins, outs, sems)
        for a in range(self.n):
            local(a).start()
        for m in self.peers:
            for a in range(self.n):
                remote(a, m, False).start()

    def relay(self, ins, outs, sems):
        remote, _, passed_on, half_on = self._copies(ins, outs, sems)
        for h, (via, _, _) in enumerate(self.HALF_VIA):
            for a in range(self.n):
                remote(a, via, True).wait_recv()
                passed_on(a, via).start()
                if self._halves(a)[h] is not None:
                    half_on(a, h, False).start()

    def relay_last(self, ins, outs, sems):
        _, _, passed_on, half_on = self._copies(ins, outs, sems)
        for a in range(self.n):
            for h in range(2):
                if self._halves(a)[h] is not None:
                    half_on(a, h, True).wait_recv()
            passed_on(a, 6).start()

    def wait(self, ins, outs, sems):
        remote, local, passed_on, half_on = self._copies(ins, outs, sems)
        if self.relay_at is None:
            for m in self.peers:
                for a in range(self.n):
                    remote(a, m, True).wait_recv()
            for m in self.peers:
                for a in range(self.n):
                    remote(a, m, False).wait_send()
        else:
            for m in (1, 3, 5, 7):
                for a in range(self.n):
                    remote(a, m, True).wait_recv()
            for m in self.peers:
                for a in range(self.n):
                    remote(a, m, False).wait_send()
            for a in range(self.n):
                for m in self.CHIPS:
                    passed_on(a, m).wait_send()
                for h in range(2):
                    if self._halves(a)[h] is not None:
                        half_on(a, h, False).wait_send()
        for a in range(self.n):
            local(a).wait()


def _pair_sum(stacks, name):
    n = len(stacks)
    n_chip = N_DEV // 2
    half = [(n_chip,) + st.shape[1:] for st in stacks]

    def body(*refs):
        ins, outs, mine, theirs = refs[:n], refs[n:2 * n], refs[2 * n:3 * n], refs[3 * n:4 * n]
        local_sems, send_sems, recv_sems = refs[4 * n:]
        mx, my, mc = lax.axis_index("x"), lax.axis_index("y"), lax.axis_index("c")

        def own(a, k):
            return pltpu.make_async_copy(ins[a].at[2 * k + mc], mine[a].at[k], local_sems.at[a, k])

        def swap(a, k):
            return pltpu.make_async_remote_copy(
                src_ref=ins[a].at[2 * k + (1 - mc)], dst_ref=theirs[a].at[k],
                send_sem=send_sems.at[a, k], recv_sem=recv_sems.at[a, k],
                device_id=(mx, my, 1 - mc), device_id_type=pl.DeviceIdType.MESH)

        for k in range(n_chip):
            for a in range(n):
                own(a, k).start()
                swap(a, k).start()
        for k in range(n_chip):
            for a in range(n):
                own(a, k).wait()
                swap(a, k).wait()
                outs[a][k] = (mine[a][k].astype(F32) + theirs[a][k].astype(F32)).astype(LOW)

    return pl.pallas_call(
        body, name=name,
        out_shape=[jax.ShapeDtypeStruct(h, LOW) for h in half],
        in_specs=[pl.BlockSpec(memory_space=pl.ANY)] * n,
        out_specs=[pl.BlockSpec(memory_space=pltpu.VMEM)] * n,
        scratch_shapes=([pltpu.VMEM(h, st.dtype) for h, st in zip(half, stacks)] * 2
                        + [pltpu.SemaphoreType.DMA((n, n_chip))] * 3),
        compiler_params=pltpu.CompilerParams(vmem_limit_bytes=VMEM_LIMIT_BYTES),
    )(*stacks)


def _exchange(arrays, sliced, name, relay=False):
    ex = _Exchange(arrays, sliced, relay_at=0 if relay else None)

    def body(*refs):
        ins, outs, sems = refs[:ex.n], refs[ex.n:2 * ex.n], refs[2 * ex.n:]
        ex.start(ins, outs, sems)
        if relay:
            ex.relay(ins, outs, sems)
            ex.relay_last(ins, outs, sems)
        ex.wait(ins, outs, sems)

    return pl.pallas_call(body, name=name, out_shape=ex.out_shape, in_specs=ex.specs, out_specs=ex.specs,
                          scratch_shapes=ex.scratch_shapes)(*arrays)


def _call(body, *, name, grid, in_specs, out_specs, out_shape, args, scratch_shapes=(), exchange=None):
    params = _params(*(("arbitrary",) * len(grid)))
    if exchange is None:
        return pl.pallas_call(body, name=name, grid=grid, in_specs=in_specs, out_specs=out_specs, out_shape=out_shape,
                              scratch_shapes=list(scratch_shapes), compiler_params=params)(*args)
    exs = list(exchange) if isinstance(exchange, (list, tuple)) else [exchange]
    n_in, n_out, n_scr = len(in_specs), len(out_specs), len(scratch_shapes)
    n_ex = sum(ex.n for ex in exs)
    n_steps = 1
    for g in grid:
        n_steps *= g

    def hosted(*refs):
        ins, refs = refs[:n_in], refs[n_in:]
        ex_ins, refs = refs[:n_ex], refs[n_ex:]
        outs, refs = refs[:n_out], refs[n_out:]
        ex_outs, refs = refs[:n_ex], refs[n_ex:]
        scr, sems = refs[:n_scr], refs[n_scr:]
        parts, at = [], 0
        for j, ex in enumerate(exs):
            parts.append((ex_ins[at:at + ex.n], ex_outs[at:at + ex.n], sems[3 * j:3 * j + 3]))
            at += ex.n
        step = pl.program_id(0)
        for ax in range(1, len(grid)):
            step = step * grid[ax] + pl.program_id(ax)

        @pl.when(step == 0)
        def _():
            for ex, part in zip(exs, parts):
                ex.start(*part)

        body(*ins, *outs, *scr)

        for ex, part in zip(exs, parts):
            if ex.relay_at is not None:
                @pl.when(step == min(int(ex.relay_at * n_steps), n_steps - 1))
                def _():
                    ex.relay(*part)

        @pl.when(step == n_steps - 1)
        def _():
            for ex, part in zip(exs, parts):
                if ex.relay_at is not None:
                    ex.relay_last(*part)
                ex.wait(*part)

    return pl.pallas_call(
        hosted, name=name, grid=grid,
        in_specs=list(in_specs) + [sp for ex in exs for sp in ex.specs],
        out_specs=list(out_specs) + [sp for ex in exs for sp in ex.specs],
        out_shape=list(out_shape) + [sh for ex in exs for sh in ex.out_shape],
        scratch_shapes=list(scratch_shapes) + [sc for ex in exs for sc in ex.scratch_shapes],
        compiler_params=params)(*args, *[arr for ex in exs for arr in ex.arrays])


def _ffn_fwd(x, g, h, wgt, wut, wd, name, exchange=None):
    s_len, d = x.shape
    fc = wd.shape[0]
    ts = min(TOK_TILE, s_len)
    first = h is None

    def body(*refs):
        x_ref, gh_ref, wg_ref, wu_ref, wd_ref, xo_ref = refs[:6]
        a_ref, b_ref, s_ref = refs[-3:]
        xv = x_ref[...]
        if first:
            hb = (xv * _rms_scale(xv) * gh_ref[...]).astype(LOW)
            refs[6][...] = hb
        else:
            hb = gh_ref[...]
        for c0, c1 in _slabs(fc):
            a = _dot_nt(hb, wg_ref[c0:c1, :])
            b = _dot_nt(hb, wu_ref[c0:c1, :])
            s_ref[:, c0:c1] = (a * jax.nn.sigmoid(a) * b).astype(LOW)
            a_ref[:, c0:c1] = a.astype(LOW)
            b_ref[:, c0:c1] = b.astype(LOW)
        xo_ref[...] = xv + 0.5 * _dot(s_ref[...], wd_ref[...])

    tok = pl.BlockSpec((ts, d), lambda t: (t, 0))
    hid = pl.BlockSpec((ts, fc), lambda t: (t, 0))
    tok_out = jax.ShapeDtypeStruct((s_len, d), F32)
    h_out = jax.ShapeDtypeStruct((s_len, d), LOW)
    hid_out = jax.ShapeDtypeStruct((s_len, fc), LOW)
    return _call(
        body,
        name=name,
        grid=(s_len // ts,),
        in_specs=[tok, pl.BlockSpec((1, d), lambda t: (0, 0)) if first else tok,
                  _resident(wgt), _resident(wut), _resident(wd)],
        out_specs=[tok] + ([tok] if first else []) + [hid, hid],
        out_shape=[tok_out] + ([h_out] if first else []) + [hid_out, hid_out],
        scratch_shapes=[pltpu.VMEM((ts, fc), LOW)],
        args=(x, g if first else h, wgt, wut, wd),
        exchange=exchange,
    )


def _slabs(width, slab=256):
    return [(c0, min(c0 + slab, width)) for c0 in range(0, width, slab)]


def _ffn_bwdw(df, a, b, h, wd, name, exchange=None):
    s_len, d = df.shape
    f_len = wd.shape[0]
    tm = _pick(f_len, WGRAD_ROW_CANDIDATES)
    tk = min(WGRAD_TOK_TILE, s_len)
    n_k = s_len // tk

    def body(df_ref, a_ref, b_ref, h_ref, wd_ref, da_ref, db_ref, dwd_ref, dwg_ref, dwu_ref,
             s_ref, acc_d, acc_g, acc_u):
        k = pl.program_id(1)

        @pl.when(k == 0)
        def _():
            acc_d[...] = jnp.zeros_like(acc_d)
            acc_g[...] = jnp.zeros_like(acc_g)
            acc_u[...] = jnp.zeros_like(acc_u)

        wdv = wd_ref[...]
        for r0, r1 in _slabs(tk, BWD_ROW_SLAB):
            ds = _dot_nt(df_ref[r0:r1, :], wdv)
            av = a_ref[r0:r1, :].astype(F32)
            bv = b_ref[r0:r1, :].astype(F32)
            sig = jax.nn.sigmoid(av)
            silu = av * sig
            s_ref[r0:r1, :] = (silu * bv).astype(LOW)
            da_ref[r0:r1, :] = (ds * bv * (sig * (1.0 + av * (1.0 - sig)))).astype(LOW)
            db_ref[r0:r1, :] = (ds * silu).astype(LOW)
        hv = h_ref[...]
        acc_d[...] += _dot_tn(s_ref[...], df_ref[...])
        acc_g[...] += _dot_tn(da_ref[...], hv)
        acc_u[...] += _dot_tn(db_ref[...], hv)

        @pl.when(k == n_k - 1)
        def _():
            dwd_ref[...] = acc_d[...].astype(LOW)
            dwg_ref[...] = acc_g[...].astype(LOW)
            dwu_ref[...] = acc_u[...].astype(LOW)

    hid = pl.BlockSpec((tk, tm), lambda i, k: (k, i))
    tok = pl.BlockSpec((tk, d), lambda i, k: (k, 0))
    wrow = pl.BlockSpec((tm, d), lambda i, k: (i, 0))
    return _call(
        body,
        name=name,
        grid=(f_len // tm, n_k),
        in_specs=[tok, hid, hid, tok, wrow],
        out_specs=[hid, hid, wrow, wrow, wrow],
        out_shape=[jax.ShapeDtypeStruct((s_len, f_len), LOW)] * 2 + [jax.ShapeDtypeStruct((f_len, d), LOW)] * 3,
        scratch_shapes=[pltpu.VMEM((tk, tm), LOW)] + [pltpu.VMEM((tm, d), F32)] * 3,
        args=(df, a, b, h, wd),
        exchange=exchange,
    )


def _ffn_dx(dxo, x, g, parts, name, exchange=None):
    s_len, d = x.shape
    ts = min(TOK_TILE, s_len)
    n_p = len(parts)

    def body(dxo_ref, x_ref, g_ref, *refs):
        dxi_ref, dg_ref = refs[4 * n_p:]

        @pl.when(pl.program_id(0) == 0)
        def _():
            dg_ref[...] = jnp.zeros_like(dg_ref)

        dh = None
        for p in range(n_p):
            da_ref, db_ref, wg_ref, wu_ref = refs[4 * p:4 * p + 4]
            part = _dot(da_ref[...], wg_ref[...]) + _dot(db_ref[...], wu_ref[...])
            dh = part if dh is None else dh + part
        dx, dgp = _rms_bwd(dh, x_ref[...], g_ref[...])
        dxi_ref[...] = dxo_ref[...] + dx
        dg_ref[...] += dgp

    tok = pl.BlockSpec((ts, d), lambda t: (t, 0))
    vec = pl.BlockSpec((1, d), lambda t: (0, 0))
    part_specs, part_args = [], []
    for da, db, wgt, wut in parts:
        hid = pl.BlockSpec((ts, da.shape[1]), lambda t: (t, 0))
        part_specs += [hid, hid, _resident(wgt), _resident(wut)]
        part_args += [da, db, wgt, wut]
    return _call(
        body,
        name=name,
        grid=(s_len // ts,),
        in_specs=[tok, tok, vec] + part_specs,
        out_specs=[tok, vec],
        out_shape=[jax.ShapeDtypeStruct((s_len, d), F32), jax.ShapeDtypeStruct((1, d), F32)],
        args=(dxo, x, g, *part_args),
        exchange=exchange,
    )


def _wgrad_tn(xm, ym, tn, stacked, name, exchange=None):
    s_len, m = xm.shape
    n = ym.shape[1]
    tk = min(WGRAD_TOK_TILE, s_len)
    n_k = s_len // tk

    def body(x_ref, y_ref, o_ref, acc):
        k = pl.program_id(1)

        @pl.when(k == 0)
        def _():
            acc[...] = jnp.zeros_like(acc)

        acc[...] += _dot_tn(x_ref[...].astype(LOW), y_ref[...].astype(LOW))

        @pl.when(k == n_k - 1)
        def _():
            o_ref[...] = acc[...].astype(LOW)

    if stacked:
        out_spec = pl.BlockSpec((None, m, tn), lambda j, k: (j, 0, 0))
        out_shape = jax.ShapeDtypeStruct((n // tn, m, tn), LOW)
    else:
        out_spec = pl.BlockSpec((m, tn), lambda j, k: (0, j))
        out_shape = jax.ShapeDtypeStruct((m, n), LOW)
    return _call(
        body,
        name=name,
        grid=(n // tn, n_k),
        in_specs=[pl.BlockSpec((tk, m), lambda j, k: (k, 0)), pl.BlockSpec((tk, tn), lambda j, k: (k, j))],
        out_specs=[out_spec],
        out_shape=[out_shape],
        scratch_shapes=[pltpu.VMEM((m, tn), F32)],
        args=(xm, ym),
        exchange=exchange,
    )


def _mix_parts(ext_ref, cw, ts, row0):
    dc = D_CONV

    def back(off, c0, c1):
        return ext_ref[HALO - off:HALO - off + ts, c0:c1]

    v, gate_b, gate_c = back(0, 0, dc), back(0, dc, 2 * dc), back(0, 2 * dc, 3 * dc)
    z0 = gate_c * v
    z1 = back(1, 2 * dc, 3 * dc) * back(1, 0, dc)
    z2 = back(2, 2 * dc, 3 * dc) * back(2, 0, dc)
    conv = cw[2:3, :] * z0 + cw[1:2, :] * z1 + cw[0:1, :] * z2
    rows = row0 + lax.broadcasted_iota(jnp.int32, (ts, 1), 0)
    pooled, inv_count = [], []
    for grp, w in enumerate(POOL_WINDOWS):
        c0 = 3 * dc + POOL_GC * grp
        u = back(0, c0, c0 + POOL_GC)
        acc = u
        for j in range(1, w):
            acc = acc + back(j, c0, c0 + POOL_GC)
        inv = 1.0 / jnp.minimum(rows + 1, w).astype(F32)
        pooled.append(acc * inv - u)
        inv_count.append(inv)
    return v, gate_b, gate_c, z0, z1, z2, conv, pooled, inv_count


def _mix_fwd(x, g, w_in, conv_w, pool_w, pool_scale, w_out, name, exchange=None):
    s_len, d = x.shape
    n_blk, _, wcols = w_in.shape
    p_len = n_blk * wcols
    d_mix = w_out.shape[0]
    ts = min(MIX_TOK_TILE, s_len)
    dc = D_CONV

    def body(x_ref, g_ref, win_ref, cw_ref, pw_ref, ps_ref, wout_ref, x2_ref, h_ref, proj_ref, ext_ref, cat_ref):
        t = pl.program_id(0)

        @pl.when(t == 0)
        def _():
            ext_ref[0:HALO, :] = jnp.zeros((HALO, p_len), F32)

        xv = x_ref[...]
        hb = (xv * _rms_scale(xv) * g_ref[...]).astype(LOW)
        h_ref[...] = hb
        for k in range(n_blk):
            ext_ref[HALO:HALO + ts, wcols * k:wcols * (k + 1)] = _dot(hb, win_ref[k])
        proj_ref[...] = ext_ref[HALO:HALO + ts, :]

        _, gate_b, _, _, _, _, conv, pooled, _ = _mix_parts(ext_ref, cw_ref[...], ts, t * ts)
        cat_ref[:, 0:dc] = (gate_b * conv).astype(LOW)
        for grp in range(len(POOL_WINDOWS)):
            c0 = POOL_GC * grp
            lin = _dot(pooled[grp].astype(LOW), pw_ref[grp])
            cat_ref[:, dc + c0:dc + c0 + POOL_GC] = (lin * ps_ref[:, c0:c0 + POOL_GC]).astype(LOW)
        x2_ref[...] = xv + _dot(cat_ref[...], wout_ref[...])
        ext_ref[0:HALO, :] = ext_ref[ts:ts + HALO, :]

    tok = pl.BlockSpec((ts, d), lambda t: (t, 0))

    def whole(arr):
        return pl.BlockSpec(arr.shape, lambda t: (0,) * arr.ndim)

    return _call(
        body,
        name=name,
        grid=(s_len // ts,),
        in_specs=[tok, whole(g), _resident(w_in), whole(conv_w), whole(pool_w), whole(pool_scale), _resident(w_out)],
        out_specs=[tok, tok, pl.BlockSpec((ts, p_len), lambda t: (t, 0))],
        out_shape=[
            jax.ShapeDtypeStruct((s_len, d), F32),
            jax.ShapeDtypeStruct((s_len, d), LOW),
            jax.ShapeDtypeStruct((s_len, p_len), F32),
        ],
        scratch_shapes=[pltpu.VMEM((ts + HALO, p_len), F32), pltpu.VMEM((ts, d_mix), LOW)],
        args=(x, g, w_in, conv_w, pool_w, pool_scale, w_out),
        exchange=exchange,
    )


def _mix_bwd(dx2, x, proj, g, w_in, conv_w, pool_w, pool_scale, w_out, name, exchange=None):
    s_len, d = x.shape
    n_blk, _, wcols = w_in.shape
    p_len = n_blk * wcols
    d_mix = w_out.shape[0]
    ts = min(MIX_TOK_TILE, s_len)
    n_t = s_len // ts
    dc = D_CONV
    n_grp = len(POOL_WINDOWS)

    def body(dx2_ref, x_ref, proj_ref, halo_ref, g_ref, win_ref, cw_ref, pw_ref, ps_ref, wout_ref,
             dx_ref, dproj_ref, cat_ref, dg_ref, dcw_ref, dpw_ref, dps_ref, df_ref, ext_ref, fut_ref):
        i = pl.program_id(0)
        t = n_t - 1 - i

        @pl.when(i == 0)
        def _():
            dg_ref[...] = jnp.zeros_like(dg_ref)
            dcw_ref[...] = jnp.zeros_like(dcw_ref)
            dpw_ref[...] = jnp.zeros_like(dpw_ref)
            dps_ref[...] = jnp.zeros_like(dps_ref)
            fut_ref[ts:ts + HALO, :] = jnp.zeros((HALO, d_mix), F32)

        ext_ref[HALO:HALO + ts, :] = proj_ref[...]

        @pl.when(t == 0)
        def _():
            ext_ref[0:HALO, :] = jnp.zeros((HALO, p_len), F32)

        @pl.when(t > 0)
        def _():
            ext_ref[0:HALO, :] = halo_ref[...]

        cw = cw_ref[...]
        v, gate_b, gate_c, z0, z1, z2, conv, pooled, inv_count = _mix_parts(ext_ref, cw, ts, t * ts)
        dx2 = dx2_ref[...]
        dcat = _dot_nt(dx2.astype(LOW), wout_ref[...])

        dy_a = dcat[:, 0:dc]
        dconv = dy_a * gate_b
        fut_ref[0:ts, 0:dc] = dconv
        cat_ref[:, 0:dc] = (gate_b * conv).astype(LOW)
        dproj_ref[:, dc:2 * dc] = (dy_a * conv).astype(LOW)
        dcw_ref[2:3, :] += jnp.sum(dconv * z0, axis=0, keepdims=True)
        dcw_ref[1:2, :] += jnp.sum(dconv * z1, axis=0, keepdims=True)
        dcw_ref[0:1, :] += jnp.sum(dconv * z2, axis=0, keepdims=True)

        dpool = []
        for grp in range(n_grp):
            c0 = POOL_GC * grp
            pooled_b = pooled[grp].astype(LOW)
            lin = _dot(pooled_b, pw_ref[grp])
            dy_b = dcat[:, dc + c0:dc + c0 + POOL_GC]
            scale = ps_ref[:, c0:c0 + POOL_GC]
            cat_ref[:, dc + c0:dc + c0 + POOL_GC] = (lin * scale).astype(LOW)
            dps_ref[:, c0:c0 + POOL_GC] += jnp.sum(dy_b * lin, axis=0, keepdims=True)
            dlin = (dy_b * scale).astype(LOW)
            dpw_ref[grp] += _dot_tn(pooled_b, dlin)
            dpool.append(_dot_nt(dlin, pw_ref[grp]))
            fut_ref[0:ts, dc + c0:dc + c0 + POOL_GC] = dpool[grp] * inv_count[grp]

        def ahead(off, c0, c1):
            return fut_ref[off:off + ts, c0:c1]

        dz = cw[2:3, :] * ahead(0, 0, dc) + cw[1:2, :] * ahead(1, 0, dc) + cw[0:1, :] * ahead(2, 0, dc)
        dproj_ref[:, 0:dc] = (dz * gate_c).astype(LOW)
        dproj_ref[:, 2 * dc:3 * dc] = (dz * v).astype(LOW)
        for grp, w in enumerate(POOL_WINDOWS):
            c0 = dc + POOL_GC * grp
            acc = ahead(0, c0, c0 + POOL_GC)
            for j in range(1, w):
                acc = acc + ahead(j, c0, c0 + POOL_GC)
            dproj_ref[:, 2 * dc + c0:2 * dc + c0 + POOL_GC] = (acc - dpool[grp]).astype(LOW)

        dh = _dot_nt(dproj_ref[:, 0:wcols], win_ref[0])
        for k in range(1, n_blk):
            dh += _dot_nt(dproj_ref[:, wcols * k:wcols * (k + 1)], win_ref[k])
        dx, dgp = _rms_bwd(dh, x_ref[...], g_ref[...])
        dx = dx2 + dx
        dx_ref[...] = dx
        df_ref[...] = (0.5 * dx).astype(LOW)
        dg_ref[...] += dgp
        fut_ref[ts:ts + HALO, :] = fut_ref[0:HALO, :]

    tok = pl.BlockSpec((ts, d), lambda i: (n_t - 1 - i, 0))
    halo = pl.BlockSpec((HALO, p_len), lambda i: (jnp.maximum((n_t - 1 - i) * (ts // HALO) - 1, 0), 0))

    def whole(arr):
        return pl.BlockSpec(arr.shape, lambda i: (0,) * arr.ndim)

    return _call(
        body,
        name=name,
        grid=(n_t,),
        in_specs=[tok, tok, pl.BlockSpec((ts, p_len), lambda i: (n_t - 1 - i, 0)), halo,
                  whole(g), _resident(w_in), whole(conv_w), whole(pool_w), whole(pool_scale), _resident(w_out)],
        out_specs=[tok, pl.BlockSpec((ts, p_len), lambda i: (n_t - 1 - i, 0)),
                   pl.BlockSpec((ts, d_mix), lambda i: (n_t - 1 - i, 0)),
                   whole(g), whole(conv_w), whole(pool_w), whole(pool_scale), tok],
        out_shape=[
            jax.ShapeDtypeStruct((s_len, d), F32),
            jax.ShapeDtypeStruct((s_len, p_len), LOW),
            jax.ShapeDtypeStruct((s_len, d_mix), LOW),
            jax.ShapeDtypeStruct(g.shape, F32),
            jax.ShapeDtypeStruct(conv_w.shape, F32),
            jax.ShapeDtypeStruct(pool_w.shape, F32),
            jax.ShapeDtypeStruct(pool_scale.shape, F32),
            jax.ShapeDtypeStruct((s_len, d), LOW),
        ],
        scratch_shapes=[pltpu.VMEM((ts + HALO, p_len), F32), pltpu.VMEM((ts + HALO, d_mix), F32)],
        args=(dx2, x, proj, proj, g, w_in, conv_w, pool_w, pool_scale, w_out),
        exchange=exchange,
    )


def _loss_head(x, g, target, name):
    s_len, d = x.shape
    ts = min(TOK_TILE, s_len)

    def body(x_ref, g_ref, tgt_ref, loss_ref, dx_ref, dg_ref, df_ref):
        @pl.when(pl.program_id(0) == 0)
        def _():
            loss_ref[...] = jnp.zeros_like(loss_ref)
            dg_ref[...] = jnp.zeros_like(dg_ref)

        xv, gv = x_ref[...], g_ref[...]
        err = xv * _rms_scale(xv) * gv - tgt_ref[...]
        loss_ref[...] += 0.5 * jnp.sum(jnp.mean(err * err, axis=-1, keepdims=True), axis=0, keepdims=True)
        dx, dgp = _rms_bwd(err * (1.0 / d), xv, gv)
        dx_ref[...] = dx
        df_ref[...] = (0.5 * dx).astype(LOW)
        dg_ref[...] += dgp

    tok = pl.BlockSpec((ts, d), lambda t: (t, 0))
    vec = pl.BlockSpec((1, d), lambda t: (0, 0))
    return pl.pallas_call(
        body,
        name=name,
        grid=(s_len // ts,),
        in_specs=[tok, vec, tok],
        out_specs=[pl.BlockSpec((1, 128), lambda t: (0, 0)), tok, vec, tok],
        out_shape=[
            jax.ShapeDtypeStruct((1, 128), F32),
            jax.ShapeDtypeStruct((s_len, d), F32),
            jax.ShapeDtypeStruct((1, d), F32),
            jax.ShapeDtypeStruct((s_len, d), LOW),
        ],
        compiler_params=_params("arbitrary"),
    )(x, g, target)


def _row_tile(rows, cols, stack_bytes):
    budget = 20 * 1024 * 1024
    per_row = cols * (4 * 7 + stack_bytes)
    for tr in (rows, 512, 256, 176, 128, 64, 32, 16, 8):
        if rows % tr == 0 and tr % 8 == 0 and tr * per_row * 2 <= budget:
            return tr
    return rows


def _sum_stack(stack, name):
    n, r, c = stack.shape
    tr = _row_tile(r, c, n * stack.dtype.itemsize)

    def body(s_ref, o_ref):
        acc = s_ref[0].astype(F32)
        for k in range(1, n):
            acc = acc + s_ref[k].astype(F32)
        o_ref[...] = acc

    return pl.pallas_call(
        body,
        name=name,
        grid=(r // tr,),
        in_specs=[pl.BlockSpec((n, tr, c), lambda i: (0, i, 0))],
        out_specs=pl.BlockSpec((tr, c), lambda i: (i, 0)),
        out_shape=jax.ShapeDtypeStruct((r, c), F32),
        compiler_params=_params("arbitrary"),
    )(stack)


def _adamw(stacks, w, m, v, name):
    stacks = list(stacks) if isinstance(stacks, (list, tuple)) else [stacks]
    r, c = w.shape
    n = stacks[0].shape[0]
    part_rows = [st.shape[1] for st in stacks]
    assert sum(part_rows) == r and all(st.shape[0] == n for st in stacks)
    first_row = [sum(part_rows[:j]) for j in range(len(stacks))]
    tc = next(t for t in (512, 256, 128) if c % t == 0)
    c1 = 1.0 - ADAM_B1 ** ADAM_STEP
    c2 = 1.0 - ADAM_B2 ** ADAM_STEP

    def body(*refs):
        s_refs = refs[:len(stacks)]
        w_ref, m_ref, v_ref, g_ref, d_ref, mo_ref, vo_ref = refs[len(stacks):]
        for s_ref, r0, nr in zip(s_refs, first_row, part_rows):
            gv = s_ref[0].astype(F32)
            for k in range(1, n):
                gv = gv + s_ref[k].astype(F32)
            mn = ADAM_B1 * m_ref[r0:r0 + nr, :] + (1.0 - ADAM_B1) * gv
            vn = ADAM_B2 * v_ref[r0:r0 + nr, :] + (1.0 - ADAM_B2) * (gv * gv)
            g_ref[r0:r0 + nr, :] = gv
            mo_ref[r0:r0 + nr, :] = mn
            vo_ref[r0:r0 + nr, :] = vn
            d_ref[r0:r0 + nr, :] = -ADAM_LR * ((mn / c1) / (jnp.sqrt(vn / c2) + ADAM_EPS)
                                               + ADAM_WD * w_ref[r0:r0 + nr, :])

    blk = pl.BlockSpec((r, tc), lambda i: (0, i))
    return pl.pallas_call(
        body,
        name=name,
        grid=(c // tc,),
        in_specs=[pl.BlockSpec((n, nr, tc), lambda i: (0, 0, i)) for nr in part_rows] + [blk, blk, blk],
        out_specs=[blk] * 4,
        out_shape=[jax.ShapeDtypeStruct((r, c), F32)] * 4,
        compiler_params=_params("arbitrary"),
    )(*stacks, w, m, v)


def _to_sheet(parts):
    sheets, spans = [], []
    row = 0
    for p in parts:
        flat = p.reshape(-1).astype(F32)
        rows = -(-flat.shape[0] // 1024) * 8
        flat = jnp.pad(flat, (0, rows * 128 - flat.shape[0]))
        sheets.append(flat.reshape(rows, 128))
        spans.append((row, p.size, p.shape))
        row += rows
    return jnp.concatenate(sheets, axis=0), spans


def _from_sheet(sheet, spans):
    out = []
    for row, size, shape in spans:
        rows = -(-size // 1024) * 8
        out.append(sheet[row:row + rows].reshape(-1)[:size].reshape(shape))
    return out


def kernel(x, norm_ffn1, ffn1_w_gate, ffn1_w_up, ffn1_w_down, norm_mix, w_in, conv_w, pool_w, pool_scale, w_out, norm_ffn2, ffn2_w_gate, ffn2_w_up, ffn2_w_down, norm_final, loss_target, m_norm_ffn1, m_ffn1_w_gate, m_ffn1_w_up, m_ffn1_w_down, m_norm_mix, m_w_in, m_conv_w, m_pool_w, m_pool_scale, m_w_out, m_norm_ffn2, m_ffn2_w_gate, m_ffn2_w_up, m_ffn2_w_down, m_norm_final, v_norm_ffn1, v_ffn1_w_gate, v_ffn1_w_up, v_ffn1_w_down, v_norm_mix, v_w_in, v_conv_w, v_pool_w, v_pool_scale, v_w_out, v_norm_ffn2, v_ffn2_w_gate, v_ffn2_w_up, v_ffn2_w_down, v_norm_final):
    me = 4 * lax.axis_index("x") + 2 * lax.axis_index("y") + lax.axis_index("c")
    xs, tgt = x[0], loss_target[0]
    s_len, d = xs.shape
    f_shard = ffn1_w_down.shape[1]
    conv_shard = conv_w.shape[2]

    def low_t(wt):
        return wt[0].T.astype(LOW)

    def by_dev(gw):
        return gw.reshape(N_DEV, -1, d)

    conv_tile = jnp.zeros((8, 128), F32).at[0:conv_w.shape[1], 0:conv_shard].set(conv_w[0])
    pool_w_low = pool_w[0].astype(LOW)

    rows_a = -(-f_shard // 64) * 32

    def parts_of(w_gate, w_up, w_down):
        shards = [low_t(w_gate), low_t(w_up), w_down[0].astype(LOW)]
        return [s[:rows_a] for s in shards], [s[rows_a:] for s in shards]

    def rows_flat(stacks):
        return [st.reshape(-1, d) for st in stacks]

    def gather(shards):
        return _Exchange(shards, [False] * len(shards), relay_at=0.6)

    def scatter(pairs):
        return _Exchange(pairs, [True] * len(pairs), among_chips=True)

    w1a_shards, w1b_shards = parts_of(ffn1_w_gate, ffn1_w_up, ffn1_w_down)
    w2a_shards, w2b_shards = parts_of(ffn2_w_gate, ffn2_w_up, ffn2_w_down)

    wg1a, wu1a, wd1a = rows_flat(_exchange(w1a_shards, [False] * 3, "gather_ffn1_a", relay=True))
    xa, h1, a1a, b1a, *w1b = _ffn_fwd(xs, norm_ffn1, None, wg1a, wu1a, wd1a, "ffn1_fwd_a",
                                      exchange=gather(w1b_shards))
    wg1b, wu1b, wd1b = rows_flat(w1b)
    x1, a1b, b1b, w_in_full, w_out_full, conv_tiles = _ffn_fwd(
        xa, None, h1, wg1b, wu1b, wd1b, "ffn1_fwd_b",
        exchange=gather([w_in[0].astype(LOW), w_out[0].astype(LOW), conv_tile]))
    w_out_full = w_out_full.reshape(-1, d)
    conv_full = jnp.concatenate([conv_tiles[k, 0:conv_w.shape[1], 0:conv_shard] for k in range(N_DEV)], axis=1)
    x2, h2, proj, *w2a = _mix_fwd(x1, norm_mix, w_in_full, conv_full, pool_w_low, pool_scale, w_out_full, "mix_fwd",
                                  exchange=gather(w2a_shards))
    wg2a, wu2a, wd2a = rows_flat(w2a)
    xb, h3, a2a, b2a, *w2b = _ffn_fwd(x2, norm_ffn2, None, wg2a, wu2a, wd2a, "ffn2_fwd_a",
                                      exchange=gather(w2b_shards))
    wg2b, wu2b, wd2b = rows_flat(w2b)
    x3, a2b, b2b = _ffn_fwd(xb, None, h3, wg2b, wu2b, wd2b, "ffn2_fwd_b")
    loss_row, dx3, dg_final, df3 = _loss_head(x3, norm_final.reshape(1, d), tgt, "loss_head")

    da2a, db2a, dwd, dwg, dwu = _ffn_bwdw(df3, a2a, b2a, h3, wd2a, "ffn2_bwdw_a")
    pairs = _pair_sum([by_dev(dwg), by_dev(dwu), by_dev(dwd)], "pair_sum_ffn2_a")
    da2b, db2b, dwd, dwg, dwu, *got_2a = _ffn_bwdw(df3, a2b, b2b, h3, wd2b, "ffn2_bwdw_b", exchange=scatter(pairs))
    pairs = _pair_sum([by_dev(dwg), by_dev(dwu), by_dev(dwd)], "pair_sum_ffn2_b")
    dx2, dg_ffn2, *got_2b = _ffn_dx(dx3, x2, norm_ffn2, [(da2a, db2a, wg2a, wu2a), (da2b, db2b, wg2b, wu2b)],
                                    "ffn2_dx", exchange=scatter(pairs))
    dx1, dproj, cat, dg_mix, dconv, dpool_w, dpool_scale, df1 = _mix_bwd(
        dx2, x1, proj, norm_mix, w_in_full, conv_full, pool_w_low, pool_scale, w_out_full, "mix_bwd")
    small_parts = [dg_mix, dg_ffn2, dg_final, dconv, dpool_w, dpool_scale]
    small_sheet, spans = _to_sheet(small_parts)
    dw_in, got_small = _wgrad_tn(h2, dproj, W_IN_SHARD, True, "w_in_wgrad",
                                 exchange=_Exchange([small_sheet], [False]))
    (dw_out,) = _wgrad_tn(cat, dx2, d, False, "w_out_wgrad")
    pairs = _pair_sum([dw_in, by_dev(dw_out)], "pair_sum_mix")
    da1a, db1a, dwd, dwg, dwu, got_in, got_out = _ffn_bwdw(
        df1, a1a, b1a, h1, wd1a, "ffn1_bwdw_a", exchange=scatter(pairs))
    pairs = _pair_sum([by_dev(dwg), by_dev(dwu), by_dev(dwd)], "pair_sum_ffn1_a")
    da1b, db1b, dwd, dwg, dwu, *got_1a = _ffn_bwdw(df1, a1b, b1b, h1, wd1b, "ffn1_bwdw_b", exchange=scatter(pairs))
    pairs = _pair_sum([by_dev(dwg), by_dev(dwu), by_dev(dwd)], "pair_sum_ffn1_b")
    dx0, dg_ffn1, *got_1b = _ffn_dx(dx1, xs, norm_ffn1, [(da1a, db1a, wg1a, wu1a), (da1b, db1b, wg1b, wu1b)],
                                    "ffn1_dx", exchange=scatter(pairs))
    (got_n1,) = _exchange([dg_ffn1.reshape(8, 128)], [False], "gather_dnorm_ffn1")

    outs = {}

    def update(name, stacks, w, m, v):
        outs[name] = _adamw(stacks, w[0], m[0], v[0], "adamw_" + name)

    def update_t(name, stacks, w, m, v):
        res = _adamw(stacks, w[0].T, m[0].T, v[0].T, "adamw_" + name)
        outs[name] = tuple(r.T for r in res)

    update_t("ffn1_w_gate", [got_1a[0], got_1b[0]], ffn1_w_gate, m_ffn1_w_gate, v_ffn1_w_gate)
    update_t("ffn1_w_up", [got_1a[1], got_1b[1]], ffn1_w_up, m_ffn1_w_up, v_ffn1_w_up)
    update("ffn1_w_down", [got_1a[2], got_1b[2]], ffn1_w_down, m_ffn1_w_down, v_ffn1_w_down)
    update("w_in", got_in, w_in, m_w_in, v_w_in)
    update("w_out", got_out, w_out, m_w_out, v_w_out)
    update_t("ffn2_w_gate", [got_2a[0], got_2b[0]], ffn2_w_gate, m_ffn2_w_gate, v_ffn2_w_gate)
    update_t("ffn2_w_up", [got_2a[1], got_2b[1]], ffn2_w_up, m_ffn2_w_up, v_ffn2_w_up)
    update("ffn2_w_down", [got_2a[2], got_2b[2]], ffn2_w_down, m_ffn2_w_down, v_ffn2_w_down)

    g_small = _from_sheet(_sum_stack(got_small, "sum_small"), spans)
    g_norm_ffn1 = _sum_stack(got_n1, "sum_dnorm_ffn1").reshape(norm_ffn1.shape)
    g_conv = lax.dynamic_slice_in_dim(g_small[3], me * conv_shard, conv_shard, axis=1)
    small_names = ["norm_ffn1", "norm_mix", "norm_ffn2", "norm_final", "conv_w", "pool_w", "pool_scale"]
    small_g = [g_norm_ffn1, g_small[0], g_small[1], g_small[2].reshape(norm_final.shape), g_conv[None],
               g_small[4][None], g_small[5]]
    small_w = [norm_ffn1, norm_mix, norm_ffn2, norm_final, conv_w, pool_w, pool_scale]
    small_m = [m_norm_ffn1, m_norm_mix, m_norm_ffn2, m_norm_final, m_conv_w, m_pool_w, m_pool_scale]
    small_v = [v_norm_ffn1, v_norm_mix, v_norm_ffn2, v_norm_final, v_conv_w, v_pool_w, v_pool_scale]
    g_sheet, spans_u = _to_sheet(small_g)
    w_sheet, _ = _to_sheet(small_w)
    m_sheet, _ = _to_sheet(small_m)
    v_sheet, _ = _to_sheet(small_v)
    upd = _adamw(g_sheet[None], w_sheet, m_sheet, v_sheet, "adamw_small")
    small_out = [_from_sheet(u, spans_u) for u in upd]
    for k, nm in enumerate(small_names):
        outs[nm] = tuple(small_out[j][k] for j in range(4))

    loss = lax.psum(loss_row[0, 0], ("x", "y", "c"))
    order = ["norm_ffn1", "ffn1_w_gate", "ffn1_w_up", "ffn1_w_down", "norm_mix", "w_in", "conv_w", "pool_w",
             "pool_scale", "w_out", "norm_ffn2", "ffn2_w_gate", "ffn2_w_up", "ffn2_w_down", "norm_final"]
    big = {"ffn1_w_gate", "ffn1_w_up", "ffn1_w_down", "w_in", "w_out", "ffn2_w_gate", "ffn2_w_up", "ffn2_w_down"}

    def leaf(nm, j):
        val = outs[nm][j]
        return val[None] if nm in big else val

    return (loss, dx0[None],
            *[leaf(nm, 0) for nm in order], *[leaf(nm, 1) for nm in order],
            *[leaf(nm, 2) for nm in order], *[leaf(nm, 3) for nm in order])
```

```python
import jax
import jax.numpy as jnp
from jax import lax
from jax.experimental import pallas as pl
from jax.experimental.pallas import tpu as pltpu

F32 = jnp.float32
LOW = jnp.bfloat16

N_DEV = 8
EPS = 1e-6
D_CONV = 512
POOL_WINDOWS = (2, 4, 8, 16)
POOL_GC = 128
HALO = 16
W_IN_SHARD = 256

ADAM_LR = 0.001
ADAM_B1 = 0.9
ADAM_B2 = 0.999
ADAM_EPS = 1e-08
ADAM_WD = 0.01
ADAM_STEP = 10

VMEM_LIMIT_BYTES = 56 * 1024 * 1024
TOK_TILE = 512
MIX_TOK_TILE = 256
WGRAD_TOK_TILE = 4096
WGRAD_ROW_CANDIDATES = (256, 128)
RELAY_LAST_LATER = 0.25
BWD_ROW_SLAB = 256


def _params(*sem):
    return pltpu.CompilerParams(dimension_semantics=sem, vmem_limit_bytes=VMEM_LIMIT_BYTES)


def _resident(arr):
    return pl.BlockSpec(arr.shape, lambda *_: (0,) * arr.ndim, pipeline_mode=pl.Buffered(1))


def _pick(n, candidates):
    for c in candidates:
        if n % c == 0:
            return c
    raise ValueError(f"no tile in {candidates} divides {n}")


def _dot(a, b):
    return lax.dot_general(a, b, (((1,), (0,)), ((), ())), preferred_element_type=F32)


def _dot_nt(a, b):
    return lax.dot_general(a, b, (((1,), (1,)), ((), ())), preferred_element_type=F32)


def _dot_tn(a, b):
    return lax.dot_general(a, b, (((0,), (0,)), ((), ())), preferred_element_type=F32)


def _rms_scale(x):
    return lax.rsqrt(jnp.mean(x * x, axis=-1, keepdims=True) + EPS)


def _rms_bwd(dy, x, g):
    r = _rms_scale(x)
    xhat = x * r
    gdy = dy * g
    dx = r * (gdy - xhat * jnp.mean(gdy * xhat, axis=-1, keepdims=True))
    return dx, jnp.sum(dy * xhat, axis=0, keepdims=True)


class _Exchange:
    CHIPS = (2, 4, 6)

    def __init__(self, arrays, sliced, relay_at=None, among_chips=False):
        assert relay_at is None or not any(sliced)
        assert not among_chips or (all(sliced) and relay_at is None)
        self.relay_at, self.among_chips = relay_at, among_chips
        self.peers = self.CHIPS if among_chips else ((1, 2, 4) if relay_at is not None else tuple(range(1, N_DEV)))
        self.arrays, self.sliced, self.n = list(arrays), list(sliced), len(arrays)
        self.out_shape = [jax.ShapeDtypeStruct(arr.shape if sl else (N_DEV,) + arr.shape, arr.dtype)
                          for arr, sl in zip(arrays, sliced)]
        self.specs = [pl.BlockSpec(memory_space=pl.ANY)] * self.n
        self.scratch_shapes = [pltpu.SemaphoreType.DMA((self.n, N_DEV)),
                               pltpu.SemaphoreType.DMA((self.n, N_DEV)),
                               pltpu.SemaphoreType.DMA((self.n,))]

    HALF_VIA = ((4, 2, 5), (2, 4, 7))

    def _halves(self, a):
        rows = self.arrays[a].shape[0]
        if rows % 32:
            return ((0, rows), None)
        return ((0, rows // 2), (rows // 2, rows // 2))

    def _copies(self, ins, outs, sems):
        send_sems, recv_sems, local_sems = sems
        sliced = self.sliced
        mx, my, mc = lax.axis_index("x"), lax.axis_index("y"), lax.axis_index("c")
        me = 2 * mx + my if self.among_chips else 4 * mx + 2 * my + mc

        def peer(m):
            px = lax.rem(mx + ((m >> 2) & 1), 2)
            py = lax.rem(my + ((m >> 1) & 1), 2)
            pc = lax.rem(mc + (m & 1), 2)
            return (px, py, pc), (2 * px + py if self.among_chips else 4 * px + 2 * py + pc)

        def remote(a, m, arriving):
            pid, pflat = peer(m)
            return pltpu.make_async_remote_copy(
                src_ref=ins[a].at[pflat] if sliced[a] else ins[a],
                dst_ref=outs[a].at[pflat if arriving else me],
                send_sem=send_sems.at[a, m - 1],
                recv_sem=recv_sems.at[a, m - 1],
                device_id=pid,
                device_id_type=pl.DeviceIdType.MESH,
            )

        def local(a):
            return pltpu.make_async_copy(ins[a].at[me] if sliced[a] else ins[a], outs[a].at[me], local_sems.at[a])

        def passed_on(a, m):
            _, origin = peer(m)
            sibling, _ = peer(1)
            return pltpu.make_async_remote_copy(
                src_ref=outs[a].at[origin],
                dst_ref=outs[a].at[origin],
                send_sem=send_sems.at[a, m],
                recv_sem=recv_sems.at[a, m],
                device_id=sibling,
                device_id_type=pl.DeviceIdType.MESH,
            )

        def half_on(a, h, arriving):
            via, to, column = self.HALF_VIA[h]
            r0, nr = self._halves(a)[h]
            _, origin = peer(6 if arriving else via)
            rows = outs[a].at[origin].at[pl.ds(r0, nr)]
            return pltpu.make_async_remote_copy(
                src_ref=rows, dst_ref=rows, send_sem=send_sems.at[a, column], recv_sem=recv_sems.at[a, column],
                device_id=peer(to)[0], device_id_type=pl.DeviceIdType.MESH)

        return remote, local, passed_on, half_on

    def start(self, ins, outs, sems):
        remote, local, _, _ = self._copies(ins, outs, sems)
        for a in range(self.n):
            local(a).start()
        for m in self.peers:
            for a in range(self.n):
                remote(a, m, False).start()

    def relay(self, ins, outs, sems):
        remote, _, passed_on, half_on = self._copies(ins, outs, sems)
        for h, (via, _, _) in enumerate(self.HALF_VIA):
            for a in range(self.n):
                remote(a, via, True).wait_recv()
                passed_on(a, via).start()
                if self._halves(a)[h] is not None:
                    half_on(a, h, False).start()

    def relay_last(self, ins, outs, sems):
        _, _, passed_on, half_on = self._copies(ins, outs, sems)
        for a in range(self.n):
            for h in range(2):
                if self._halves(a)[h] is not None:
                    half_on(a, h, True).wait_recv()
            passed_on(a, 6).start()

    def wait(self, ins, outs, sems):
        remote, local, passed_on, half_on = self._copies(ins, outs, sems)
        if self.relay_at is None:
            for m in self.peers:
                for a in range(self.n):
                    remote(a, m, True).wait_recv()
            for m in self.peers:
                for a in range(self.n):
                    remote(a, m, False).wait_send()
        else:
            for m in (1, 3, 5, 7):
                for a in range(self.n):
                    remote(a, m, True).wait_recv()
            for m in self.peers:
                for a in range(self.n):
                    remote(a, m, False).wait_send()
            for a in range(self.n):
                for m in self.CHIPS:
                    passed_on(a, m).wait_send()
                for h in range(2):
                    if self._halves(a)[h] is not None:
                        half_on(a, h, False).wait_send()
        for a in range(self.n):
            local(a).wait()


def _pair_sum(stacks, name):
    n = len(stacks)
    n_chip = N_DEV // 2
    half = [(n_chip,) + st.shape[1:] for st in stacks]

    def body(*refs):
        ins, outs, mine, theirs = refs[:n], refs[n:2 * n], refs[2 * n:3 * n], refs[3 * n:4 * n]
        local_sems, send_sems, recv_sems = refs[4 * n:]
        mx, my, mc = lax.axis_index("x"), lax.axis_index("y"), lax.axis_index("c")

        def own(a, k):
            return pltpu.make_async_copy(ins[a].at[2 * k + mc], mine[a].at[k], local_sems.at[a, k])

        def swap(a, k):
            return pltpu.make_async_remote_copy(
                src_ref=ins[a].at[2 * k + (1 - mc)], dst_ref=theirs[a].at[k],
                send_sem=send_sems.at[a, k], recv_sem=recv_sems.at[a, k],
                device_id=(mx, my, 1 - mc), device_id_type=pl.DeviceIdType.MESH)

        for k in range(n_chip):
            for a in range(n):
                own(a, k).start()
                swap(a, k).start()
        for k in range(n_chip):
            for a in range(n):
                own(a, k).wait()
                swap(a, k).wait()
                outs[a][k] = (mine[a][k].astype(F32) + theirs[a][k].astype(F32)).astype(LOW)

    return pl.pallas_call(
        body, name=name,
        out_shape=[jax.ShapeDtypeStruct(h, LOW) for h in half],
        in_specs=[pl.BlockSpec(memory_space=pl.ANY)] * n,
        out_specs=[pl.BlockSpec(memory_space=pltpu.VMEM)] * n,
        scratch_shapes=([pltpu.VMEM(h, st.dtype) for h, st in zip(half, stacks)] * 2
                        + [pltpu.SemaphoreType.DMA((n, n_chip))] * 3),
        compiler_params=pltpu.CompilerParams(vmem_limit_bytes=VMEM_LIMIT_BYTES),
    )(*stacks)


def _exchange(arrays, sliced, name, relay=False):
    ex = _Exchange(arrays, sliced, relay_at=0 if relay else None)

    def body(*refs):
        ins, outs, sems = refs[:ex.n], refs[ex.n:2 * ex.n], refs[2 * ex.n:]
        ex.start(ins, outs, sems)
        if relay:
            ex.relay(ins, outs, sems)
            ex.relay_last(ins, outs, sems)
        ex.wait(ins, outs, sems)

    return pl.pallas_call(body, name=name, out_shape=ex.out_shape, in_specs=ex.specs, out_specs=ex.specs,
                          scratch_shapes=ex.scratch_shapes)(*arrays)


def _call(body, *, name, grid, in_specs, out_specs, out_shape, args, scratch_shapes=(), exchange=None):
    params = _params(*(("arbitrary",) * len(grid)))
    if exchange is None:
        return pl.pallas_call(body, name=name, grid=grid, in_specs=in_specs, out_specs=out_specs, out_shape=out_shape,
                              scratch_shapes=list(scratch_shapes), compiler_params=params)(*args)
    exs = list(exchange) if isinstance(exchange, (list, tuple)) else [exchange]
    n_in, n_out, n_scr = len(in_specs), len(out_specs), len(scratch_shapes)
    n_ex = sum(ex.n for ex in exs)
    n_steps = 1
    for g in grid:
        n_steps *= g

    def hosted(*refs):
        ins, refs = refs[:n_in], refs[n_in:]
        ex_ins, refs = refs[:n_ex], refs[n_ex:]
        outs, refs = refs[:n_out], refs[n_out:]
        ex_outs, refs = refs[:n_ex], refs[n_ex:]
        scr, sems = refs[:n_scr], refs[n_scr:]
        parts, at = [], 0
        for j, ex in enumerate(exs):
            parts.append((ex_ins[at:at + ex.n], ex_outs[at:at + ex.n], sems[3 * j:3 * j + 3]))
            at += ex.n
        step = pl.program_id(0)
        for ax in range(1, len(grid)):
            step = step * grid[ax] + pl.program_id(ax)

        @pl.when(step == 0)
        def _():
            for ex, part in zip(exs, parts):
                ex.start(*part)

        body(*ins, *outs, *scr)

        for ex, part in zip(exs, parts):
            if ex.relay_at is not None:
                @pl.when(step == min(int(ex.relay_at * n_steps), n_steps - 1))
                def _():
                    ex.relay(*part)

                @pl.when(step == min(int((ex.relay_at + RELAY_LAST_LATER) * n_steps), n_steps - 1))
                def _():
                    ex.relay_last(*part)

        @pl.when(step == n_steps - 1)
        def _():
            for ex, part in zip(exs, parts):
                ex.wait(*part)

    return pl.pallas_call(
        hosted, name=name, grid=grid,
        in_specs=list(in_specs) + [sp for ex in exs for sp in ex.specs],
        out_specs=list(out_specs) + [sp for ex in exs for sp in ex.specs],
        out_shape=list(out_shape) + [sh for ex in exs for sh in ex.out_shape],
        scratch_shapes=list(scratch_shapes) + [sc for ex in exs for sc in ex.scratch_shapes],
        compiler_params=params)(*args, *[arr for ex in exs for arr in ex.arrays])


def _ffn_fwd(x, g, h, wgt, wut, wd, name, exchange=None):
    s_len, d = x.shape
    fc = wd.shape[0]
    ts = min(TOK_TILE, s_len)
    first = h is None

    def body(*refs):
        x_ref, gh_ref, wg_ref, wu_ref, wd_ref, xo_ref = refs[:6]
        a_ref, b_ref, s_ref = refs[-3:]
        xv = x_ref[...]
        if first:
            hb = (xv * _rms_scale(xv) * gh_ref[...]).astype(LOW)
            refs[6][...] = hb
        else:
            hb = gh_ref[...]
        for c0, c1 in _slabs(fc):
            a = _dot_nt(hb, wg_ref[c0:c1, :])
            b = _dot_nt(hb, wu_ref[c0:c1, :])
            s_ref[:, c0:c1] = (a * jax.nn.sigmoid(a) * b).astype(LOW)
            a_ref[:, c0:c1] = a.astype(LOW)
            b_ref[:, c0:c1] = b.astype(LOW)
        xo_ref[...] = xv + 0.5 * _dot(s_ref[...], wd_ref[...])

    tok = pl.BlockSpec((ts, d), lambda t: (t, 0))
    hid = pl.BlockSpec((ts, fc), lambda t: (t, 0))
    tok_out = jax.ShapeDtypeStruct((s_len, d), F32)
    h_out = jax.ShapeDtypeStruct((s_len, d), LOW)
    hid_out = jax.ShapeDtypeStruct((s_len, fc), LOW)
    return _call(
        body,
        name=name,
        grid=(s_len // ts,),
        in_specs=[tok, pl.BlockSpec((1, d), lambda t: (0, 0)) if first else tok,
                  _resident(wgt), _resident(wut), _resident(wd)],
        out_specs=[tok] + ([tok] if first else []) + [hid, hid],
        out_shape=[tok_out] + ([h_out] if first else []) + [hid_out, hid_out],
        scratch_shapes=[pltpu.VMEM((ts, fc), LOW)],
        args=(x, g if first else h, wgt, wut, wd),
        exchange=exchange,
    )


def _slabs(width, slab=256):
    return [(c0, min(c0 + slab, width)) for c0 in range(0, width, slab)]


def _ffn_bwdw(df, a, b, h, wd, name, exchange=None):
    s_len, d = df.shape
    f_len = wd.shape[0]
    tm = _pick(f_len, WGRAD_ROW_CANDIDATES)
    tk = min(WGRAD_TOK_TILE, s_len)
    n_k = s_len // tk

    def body(df_ref, a_ref, b_ref, h_ref, wd_ref, da_ref, db_ref, dwd_ref, dwg_ref, dwu_ref,
             s_ref, acc_d, acc_g, acc_u):
        k = pl.program_id(1)

        @pl.when(k == 0)
        def _():
            acc_d[...] = jnp.zeros_like(acc_d)
            acc_g[...] = jnp.zeros_like(acc_g)
            acc_u[...] = jnp.zeros_like(acc_u)

        wdv = wd_ref[...]
        for r0, r1 in _slabs(tk, BWD_ROW_SLAB):
            ds = _dot_nt(df_ref[r0:r1, :], wdv)
            av = a_ref[r0:r1, :].astype(F32)
            bv = b_ref[r0:r1, :].astype(F32)
            sig = jax.nn.sigmoid(av)
            silu = av * sig
            s_ref[r0:r1, :] = (silu * bv).astype(LOW)
            da_ref[r0:r1, :] = (ds * bv * (sig * (1.0 + av * (1.0 - sig)))).astype(LOW)
            db_ref[r0:r1, :] = (ds * silu).astype(LOW)
        hv = h_ref[...]
        acc_d[...] += _dot_tn(s_ref[...], df_ref[...])
        acc_g[...] += _dot_tn(da_ref[...], hv)
        acc_u[...] += _dot_tn(db_ref[...], hv)

        @pl.when(k == n_k - 1)
        def _():
            dwd_ref[...] = acc_d[...].astype(LOW)
            dwg_ref[...] = acc_g[...].astype(LOW)
            dwu_ref[...] = acc_u[...].astype(LOW)

    hid = pl.BlockSpec((tk, tm), lambda i, k: (k, i))
    tok = pl.BlockSpec((tk, d), lambda i, k: (k, 0))
    wrow = pl.BlockSpec((tm, d), lambda i, k: (i, 0))
    return _call(
        body,
        name=name,
        grid=(f_len // tm, n_k),
        in_specs=[tok, hid, hid, tok, wrow],
        out_specs=[hid, hid, wrow, wrow, wrow],
        out_shape=[jax.ShapeDtypeStruct((s_len, f_len), LOW)] * 2 + [jax.ShapeDtypeStruct((f_len, d), LOW)] * 3,
        scratch_shapes=[pltpu.VMEM((tk, tm), LOW)] + [pltpu.VMEM((tm, d), F32)] * 3,
        args=(df, a, b, h, wd),
        exchange=exchange,
    )


def _ffn_dx(dxo, x, g, parts, name, exchange=None):
    s_len, d = x.shape
    ts = min(TOK_TILE, s_len)
    n_p = len(parts)

    def body(dxo_ref, x_ref, g_ref, *refs):
        dxi_ref, dg_ref = refs[4 * n_p:]

        @pl.when(pl.program_id(0) == 0)
        def _():
            dg_ref[...] = jnp.zeros_like(dg_ref)

        dh = None
        for p in range(n_p):
            da_ref, db_ref, wg_ref, wu_ref = refs[4 * p:4 * p + 4]
            part = _dot(da_ref[...], wg_ref[...]) + _dot(db_ref[...], wu_ref[...])
            dh = part if dh is None else dh + part
        dx, dgp = _rms_bwd(dh, x_ref[...], g_ref[...])
        dxi_ref[...] = dxo_ref[...] + dx
        dg_ref[...] += dgp

    tok = pl.BlockSpec((ts, d), lambda t: (t, 0))
    vec = pl.BlockSpec((1, d), lambda t: (0, 0))
    part_specs, part_args = [], []
    for da, db, wgt, wut in parts:
        hid = pl.BlockSpec((ts, da.shape[1]), lambda t: (t, 0))
        part_specs += [hid, hid, _resident(wgt), _resident(wut)]
        part_args += [da, db, wgt, wut]
    return _call(
        body,
        name=name,
        grid=(s_len // ts,),
        in_specs=[tok, tok, vec] + part_specs,
        out_specs=[tok, vec],
        out_shape=[jax.ShapeDtypeStruct((s_len, d), F32), jax.ShapeDtypeStruct((1, d), F32)],
        args=(dxo, x, g, *part_args),
        exchange=exchange,
    )


def _wgrad_tn(xm, ym, tn, stacked, name, exchange=None):
    s_len, m = xm.shape
    n = ym.shape[1]
    tk = min(WGRAD_TOK_TILE, s_len)
    n_k = s_len // tk

    def body(x_ref, y_ref, o_ref, acc):
        k = pl.program_id(1)

        @pl.when(k == 0)
        def _():
            acc[...] = jnp.zeros_like(acc)

        acc[...] += _dot_tn(x_ref[...].astype(LOW), y_ref[...].astype(LOW))

        @pl.when(k == n_k - 1)
        def _():
            o_ref[...] = acc[...].astype(LOW)

    if stacked:
        out_spec = pl.BlockSpec((None, m, tn), lambda j, k: (j, 0, 0))
        out_shape = jax.ShapeDtypeStruct((n // tn, m, tn), LOW)
    else:
        out_spec = pl.BlockSpec((m, tn), lambda j, k: (0, j))
        out_shape = jax.ShapeDtypeStruct((m, n), LOW)
    return _call(
        body,
        name=name,
        grid=(n // tn, n_k),
        in_specs=[pl.BlockSpec((tk, m), lambda j, k: (k, 0)), pl.BlockSpec((tk, tn), lambda j, k: (k, j))],
        out_specs=[out_spec],
        out_shape=[out_shape],
        scratch_shapes=[pltpu.VMEM((m, tn), F32)],
        args=(xm, ym),
        exchange=exchange,
    )


def _mix_parts(ext_ref, cw, ts, row0):
    dc = D_CONV

    def back(off, c0, c1):
        return ext_ref[HALO - off:HALO - off + ts, c0:c1]

    v, gate_b, gate_c = back(0, 0, dc), back(0, dc, 2 * dc), back(0, 2 * dc, 3 * dc)
    z0 = gate_c * v
    z1 = back(1, 2 * dc, 3 * dc) * back(1, 0, dc)
    z2 = back(2, 2 * dc, 3 * dc) * back(2, 0, dc)
    conv = cw[2:3, :] * z0 + cw[1:2, :] * z1 + cw[0:1, :] * z2
    rows = row0 + lax.broadcasted_iota(jnp.int32, (ts, 1), 0)
    pooled, inv_count = [], []
    for grp, w in enumerate(POOL_WINDOWS):
        c0 = 3 * dc + POOL_GC * grp
        u = back(0, c0, c0 + POOL_GC)
        acc = u
        for j in range(1, w):
            acc = acc + back(j, c0, c0 + POOL_GC)
        inv = 1.0 / jnp.minimum(rows + 1, w).astype(F32)
        pooled.append(acc * inv - u)
        inv_count.append(inv)
    return v, gate_b, gate_c, z0, z1, z2, conv, pooled, inv_count


def _mix_fwd(x, g, w_in, conv_w, pool_w, pool_scale, w_out, name, exchange=None):
    s_len, d = x.shape
    n_blk, _, wcols = w_in.shape
    p_len = n_blk * wcols
    d_mix = w_out.shape[0]
    ts = min(MIX_TOK_TILE, s_len)
    dc = D_CONV

    def body(x_ref, g_ref, win_ref, cw_ref, pw_ref, ps_ref, wout_ref, x2_ref, h_ref, proj_ref, ext_ref, cat_ref):
        t = pl.program_id(0)

        @pl.when(t == 0)
        def _():
            ext_ref[0:HALO, :] = jnp.zeros((HALO, p_len), F32)

        xv = x_ref[...]
        hb = (xv * _rms_scale(xv) * g_ref[...]).astype(LOW)
        h_ref[...] = hb
        for k in range(n_blk):
            ext_ref[HALO:HALO + ts, wcols * k:wcols * (k + 1)] = _dot(hb, win_ref[k])
        proj_ref[...] = ext_ref[HALO:HALO + ts, :]

        _, gate_b, _, _, _, _, conv, pooled, _ = _mix_parts(ext_ref, cw_ref[...], ts, t * ts)
        cat_ref[:, 0:dc] = (gate_b * conv).astype(LOW)
        for grp in range(len(POOL_WINDOWS)):
            c0 = POOL_GC * grp
            lin = _dot(pooled[grp].astype(LOW), pw_ref[grp])
            cat_ref[:, dc + c0:dc + c0 + POOL_GC] = (lin * ps_ref[:, c0:c0 + POOL_GC]).astype(LOW)
        x2_ref[...] = xv + _dot(cat_ref[...], wout_ref[...])
        ext_ref[0:HALO, :] = ext_ref[ts:ts + HALO, :]

    tok = pl.BlockSpec((ts, d), lambda t: (t, 0))

    def whole(arr):
        return pl.BlockSpec(arr.shape, lambda t: (0,) * arr.ndim)

    return _call(
        body,
        name=name,
        grid=(s_len // ts,),
        in_specs=[tok, whole(g), _resident(w_in), whole(conv_w), whole(pool_w), whole(pool_scale), _resident(w_out)],
        out_specs=[tok, tok, pl.BlockSpec((ts, p_len), lambda t: (t, 0))],
        out_shape=[
            jax.ShapeDtypeStruct((s_len, d), F32),
            jax.ShapeDtypeStruct((s_len, d), LOW),
            jax.ShapeDtypeStruct((s_len, p_len), F32),
        ],
        scratch_shapes=[pltpu.VMEM((ts + HALO, p_len), F32), pltpu.VMEM((ts, d_mix), LOW)],
        args=(x, g, w_in, conv_w, pool_w, pool_scale, w_out),
        exchange=exchange,
    )


def _mix_bwd(dx2, x, proj, g, w_in, conv_w, pool_w, pool_scale, w_out, name, exchange=None):
    s_len, d = x.shape
    n_blk, _, wcols = w_in.shape
    p_len = n_blk * wcols
    d_mix = w_out.shape[0]
    ts = min(MIX_TOK_TILE, s_len)
    n_t = s_len // ts
    dc = D_CONV
    n_grp = len(POOL_WINDOWS)

    def body(dx2_ref, x_ref, proj_ref, halo_ref, g_ref, win_ref, cw_ref, pw_ref, ps_ref, wout_ref,
             dx_ref, dproj_ref, cat_ref, dg_ref, dcw_ref, dpw_ref, dps_ref, df_ref, ext_ref, fut_ref):
        i = pl.program_id(0)
        t = n_t - 1 - i

        @pl.when(i == 0)
        def _():
            dg_ref[...] = jnp.zeros_like(dg_ref)
            dcw_ref[...] = jnp.zeros_like(dcw_ref)
            dpw_ref[...] = jnp.zeros_like(dpw_ref)
            dps_ref[...] = jnp.zeros_like(dps_ref)
            fut_ref[ts:ts + HALO, :] = jnp.zeros((HALO, d_mix), F32)

        ext_ref[HALO:HALO + ts, :] = proj_ref[...]

        @pl.when(t == 0)
        def _():
            ext_ref[0:HALO, :] = jnp.zeros((HALO, p_len), F32)

        @pl.when(t > 0)
        def _():
            ext_ref[0:HALO, :] = halo_ref[...]

        cw = cw_ref[...]
        v, gate_b, gate_c, z0, z1, z2, conv, pooled, inv_count = _mix_parts(ext_ref, cw, ts, t * ts)
        dx2 = dx2_ref[...]
        dcat = _dot_nt(dx2.astype(LOW), wout_ref[...])

        dy_a = dcat[:, 0:dc]
        dconv = dy_a * gate_b
        fut_ref[0:ts, 0:dc] = dconv
        cat_ref[:, 0:dc] = (gate_b * conv).astype(LOW)
        dproj_ref[:, dc:2 * dc] = (dy_a * conv).astype(LOW)
        dcw_ref[2:3, :] += jnp.sum(dconv * z0, axis=0, keepdims=True)
        dcw_ref[1:2, :] += jnp.sum(dconv * z1, axis=0, keepdims=True)
        dcw_ref[0:1, :] += jnp.sum(dconv * z2, axis=0, keepdims=True)

        dpool = []
        for grp in range(n_grp):
            c0 = POOL_GC * grp
            pooled_b = pooled[grp].astype(LOW)
            lin = _dot(pooled_b, pw_ref[grp])
            dy_b = dcat[:, dc + c0:dc + c0 + POOL_GC]
            scale = ps_ref[:, c0:c0 + POOL_GC]
            cat_ref[:, dc + c0:dc + c0 + POOL_GC] = (lin * scale).astype(LOW)
            dps_ref[:, c0:c0 + POOL_GC] += jnp.sum(dy_b * lin, axis=0, keepdims=True)
            dlin = (dy_b * scale).astype(LOW)
            dpw_ref[grp] += _dot_tn(pooled_b, dlin)
            dpool.append(_dot_nt(dlin, pw_ref[grp]))
            fut_ref[0:ts, dc + c0:dc + c0 + POOL_GC] = dpool[grp] * inv_count[grp]

        def ahead(off, c0, c1):
            return fut_ref[off:off + ts, c0:c1]

        dz = cw[2:3, :] * ahead(0, 0, dc) + cw[1:2, :] * ahead(1, 0, dc) + cw[0:1, :] * ahead(2, 0, dc)
        dproj_ref[:, 0:dc] = (dz * gate_c).astype(LOW)
        dproj_ref[:, 2 * dc:3 * dc] = (dz * v).astype(LOW)
        for grp, w in enumerate(POOL_WINDOWS):
            c0 = dc + POOL_GC * grp
            acc = ahead(0, c0, c0 + POOL_GC)
            for j in range(1, w):
                acc = acc + ahead(j, c0, c0 + POOL_GC)
            dproj_ref[:, 2 * dc + c0:2 * dc + c0 + POOL_GC] = (acc - dpool[grp]).astype(LOW)

        dh = _dot_nt(dproj_ref[:, 0:wcols], win_ref[0])
        for k in range(1, n_blk):
            dh += _dot_nt(dproj_ref[:, wcols * k:wcols * (k + 1)], win_ref[k])
        dx, dgp = _rms_bwd(dh, x_ref[...], g_ref[...])
        dx = dx2 + dx
        dx_ref[...] = dx
        df_ref[...] = (0.5 * dx).astype(LOW)
        dg_ref[...] += dgp
        fut_ref[ts:ts + HALO, :] = fut_ref[0:HALO, :]

    tok = pl.BlockSpec((ts, d), lambda i: (n_t - 1 - i, 0))
    halo = pl.BlockSpec((HALO, p_len), lambda i: (jnp.maximum((n_t - 1 - i) * (ts // HALO) - 1, 0), 0))

    def whole(arr):
        return pl.BlockSpec(arr.shape, lambda i: (0,) * arr.ndim)

    return _call(
        body,
        name=name,
        grid=(n_t,),
        in_specs=[tok, tok, pl.BlockSpec((ts, p_len), lambda i: (n_t - 1 - i, 0)), halo,
                  whole(g), _resident(w_in), whole(conv_w), whole(pool_w), whole(pool_scale), _resident(w_out)],
        out_specs=[tok, pl.BlockSpec((ts, p_len), lambda i: (n_t - 1 - i, 0)),
                   pl.BlockSpec((ts, d_mix), lambda i: (n_t - 1 - i, 0)),
                   whole(g), whole(conv_w), whole(pool_w), whole(pool_scale), tok],
        out_shape=[
            jax.ShapeDtypeStruct((s_len, d), F32),
            jax.ShapeDtypeStruct((s_len, p_len), LOW),
            jax.ShapeDtypeStruct((s_len, d_mix), LOW),
            jax.ShapeDtypeStruct(g.shape, F32),
            jax.ShapeDtypeStruct(conv_w.shape, F32),
            jax.ShapeDtypeStruct(pool_w.shape, F32),
            jax.ShapeDtypeStruct(pool_scale.shape, F32),
            jax.ShapeDtypeStruct((s_len, d), LOW),
        ],
        scratch_shapes=[pltpu.VMEM((ts + HALO, p_len), F32), pltpu.VMEM((ts + HALO, d_mix), F32)],
        args=(dx2, x, proj, proj, g, w_in, conv_w, pool_w, pool_scale, w_out),
        exchange=exchange,
    )


def _loss_head(x, g, target, name):
    s_len, d = x.shape
    ts = min(TOK_TILE, s_len)

    def body(x_ref, g_ref, tgt_ref, loss_ref, dx_ref, dg_ref, df_ref):
        @pl.when(pl.program_id(0) == 0)
        def _():
            loss_ref[...] = jnp.zeros_like(loss_ref)
            dg_ref[...] = jnp.zeros_like(dg_ref)

        xv, gv = x_ref[...], g_ref[...]
        err = xv * _rms_scale(xv) * gv - tgt_ref[...]
        loss_ref[...] += 0.5 * jnp.sum(jnp.mean(err * err, axis=-1, keepdims=True), axis=0, keepdims=True)
        dx, dgp = _rms_bwd(err * (1.0 / d), xv, gv)
        dx_ref[...] = dx
        df_ref[...] = (0.5 * dx).astype(LOW)
        dg_ref[...] += dgp

    tok = pl.BlockSpec((ts, d), lambda t: (t, 0))
    vec = pl.BlockSpec((1, d), lambda t: (0, 0))
    return pl.pallas_call(
        body,
        name=name,
        grid=(s_len // ts,),
        in_specs=[tok, vec, tok],
        out_specs=[pl.BlockSpec((1, 128), lambda t: (0, 0)), tok, vec, tok],
        out_shape=[
            jax.ShapeDtypeStruct((1, 128), F32),
            jax.ShapeDtypeStruct((s_len, d), F32),
            jax.ShapeDtypeStruct((1, d), F32),
            jax.ShapeDtypeStruct((s_len, d), LOW),
        ],
        compiler_params=_params("arbitrary"),
    )(x, g, target)


def _row_tile(rows, cols, stack_bytes):
    budget = 20 * 1024 * 1024
    per_row = cols * (4 * 7 + stack_bytes)
    for tr in (rows, 512, 256, 176, 128, 64, 32, 16, 8):
        if rows % tr == 0 and tr % 8 == 0 and tr * per_row * 2 <= budget:
            return tr
    return rows


def _sum_stack(stack, name):
    n, r, c = stack.shape
    tr = _row_tile(r, c, n * stack.dtype.itemsize)

    def body(s_ref, o_ref):
        acc = s_ref[0].astype(F32)
        for k in range(1, n):
            acc = acc + s_ref[k].astype(F32)
        o_ref[...] = acc

    return pl.pallas_call(
        body,
        name=name,
        grid=(r // tr,),
        in_specs=[pl.BlockSpec((n, tr, c), lambda i: (0, i, 0))],
        out_specs=pl.BlockSpec((tr, c), lambda i: (i, 0)),
        out_shape=jax.ShapeDtypeStruct((r, c), F32),
        compiler_params=_params("arbitrary"),
    )(stack)


def _adamw(stacks, w, m, v, name):
    stacks = list(stacks) if isinstance(stacks, (list, tuple)) else [stacks]
    r, c = w.shape
    n = stacks[0].shape[0]
    part_rows = [st.shape[1] for st in stacks]
    assert sum(part_rows) == r and all(st.shape[0] == n for st in stacks)
    first_row = [sum(part_rows[:j]) for j in range(len(stacks))]
    tc = next(t for t in (512, 256, 128) if c % t == 0)
    c1 = 1.0 - ADAM_B1 ** ADAM_STEP
    c2 = 1.0 - ADAM_B2 ** ADAM_STEP

    def body(*refs):
        s_refs = refs[:len(stacks)]
        w_ref, m_ref, v_ref, g_ref, d_ref, mo_ref, vo_ref = refs[len(stacks):]
        for s_ref, r0, nr in zip(s_refs, first_row, part_rows):
            gv = s_ref[0].astype(F32)
            for k in range(1, n):
                gv = gv + s_ref[k].astype(F32)
            mn = ADAM_B1 * m_ref[r0:r0 + nr, :] + (1.0 - ADAM_B1) * gv
            vn = ADAM_B2 * v_ref[r0:r0 + nr, :] + (1.0 - ADAM_B2) * (gv * gv)
            g_ref[r0:r0 + nr, :] = gv
            mo_ref[r0:r0 + nr, :] = mn
            vo_ref[r0:r0 + nr, :] = vn
            d_ref[r0:r0 + nr, :] = -ADAM_LR * ((mn / c1) / (jnp.sqrt(vn / c2) + ADAM_EPS)
                                               + ADAM_WD * w_ref[r0:r0 + nr, :])

    blk = pl.BlockSpec((r, tc), lambda i: (0, i))
    return pl.pallas_call(
        body,
        name=name,
        grid=(c // tc,),
        in_specs=[pl.BlockSpec((n, nr, tc), lambda i: (0, 0, i)) for nr in part_rows] + [blk, blk, blk],
        out_specs=[blk] * 4,
        out_shape=[jax.ShapeDtypeStruct((r, c), F32)] * 4,
        compiler_params=_params("arbitrary"),
    )(*stacks, w, m, v)


def _to_sheet(parts):
    sheets, spans = [], []
    row = 0
    for p in parts:
        flat = p.reshape(-1).astype(F32)
        rows = -(-flat.shape[0] // 1024) * 8
        flat = jnp.pad(flat, (0, rows * 128 - flat.shape[0]))
        sheets.append(flat.reshape(rows, 128))
        spans.append((row, p.size, p.shape))
        row += rows
    return jnp.concatenate(sheets, axis=0), spans


def _from_sheet(sheet, spans):
    out = []
    for row, size, shape in spans:
        rows = -(-size // 1024) * 8
        out.append(sheet[row:row + rows].reshape(-1)[:size].reshape(shape))
    return out


def kernel(x, norm_ffn1, ffn1_w_gate, ffn1_w_up, ffn1_w_down, norm_mix, w_in, conv_w, pool_w, pool_scale, w_out, norm_ffn2, ffn2_w_gate, ffn2_w_up, ffn2_w_down, norm_final, loss_target, m_norm_ffn1, m_ffn1_w_gate, m_ffn1_w_up, m_ffn1_w_down, m_norm_mix, m_w_in, m_conv_w, m_pool_w, m_pool_scale, m_w_out, m_norm_ffn2, m_ffn2_w_gate, m_ffn2_w_up, m_ffn2_w_down, m_norm_final, v_norm_ffn1, v_ffn1_w_gate, v_ffn1_w_up, v_ffn1_w_down, v_norm_mix, v_w_in, v_conv_w, v_pool_w, v_pool_scale, v_w_out, v_norm_ffn2, v_ffn2_w_gate, v_ffn2_w_up, v_ffn2_w_down, v_norm_final):
    me = 4 * lax.axis_index("x") + 2 * lax.axis_index("y") + lax.axis_index("c")
    xs, tgt = x[0], loss_target[0]
    s_len, d = xs.shape
    f_shard = ffn1_w_down.shape[1]
    conv_shard = conv_w.shape[2]

    def low_t(wt):
        return wt[0].T.astype(LOW)

    def by_dev(gw):
        return gw.reshape(N_DEV, -1, d)

    conv_tile = jnp.zeros((8, 128), F32).at[0:conv_w.shape[1], 0:conv_shard].set(conv_w[0])
    pool_w_low = pool_w[0].astype(LOW)

    rows_a = -(-f_shard // 64) * 32

    def parts_of(w_gate, w_up, w_down):
        shards = [low_t(w_gate), low_t(w_up), w_down[0].astype(LOW)]
        return [s[:rows_a] for s in shards], [s[rows_a:] for s in shards]

    def rows_flat(stacks):
        return [st.reshape(-1, d) for st in stacks]

    def gather(shards):
        return _Exchange(shards, [False] * len(shards), relay_at=0.6)

    def scatter(pairs):
        return _Exchange(pairs, [True] * len(pairs), among_chips=True)

    w1a_shards, w1b_shards = parts_of(ffn1_w_gate, ffn1_w_up, ffn1_w_down)
    w2a_shards, w2b_shards = parts_of(ffn2_w_gate, ffn2_w_up, ffn2_w_down)

    wg1a, wu1a, wd1a = rows_flat(_exchange(w1a_shards, [False] * 3, "gather_ffn1_a", relay=True))
    xa, h1, a1a, b1a, *w1b = _ffn_fwd(xs, norm_ffn1, None, wg1a, wu1a, wd1a, "ffn1_fwd_a",
                                      exchange=gather(w1b_shards))
    wg1b, wu1b, wd1b = rows_flat(w1b)
    x1, a1b, b1b, w_in_full, w_out_full, conv_tiles = _ffn_fwd(
        xa, None, h1, wg1b, wu1b, wd1b, "ffn1_fwd_b",
        exchange=gather([w_in[0].astype(LOW), w_out[0].astype(LOW), conv_tile]))
    w_out_full = w_out_full.reshape(-1, d)
    conv_full = jnp.concatenate([conv_tiles[k, 0:conv_w.shape[1], 0:conv_shard] for k in range(N_DEV)], axis=1)
    x2, h2, proj, *w2a = _mix_fwd(x1, norm_mix, w_in_full, conv_full, pool_w_low, pool_scale, w_out_full, "mix_fwd",
                                  exchange=gather(w2a_shards))
    wg2a, wu2a, wd2a = rows_flat(w2a)
    xb, h3, a2a, b2a, *w2b = _ffn_fwd(x2, norm_ffn2, None, wg2a, wu2a, wd2a, "ffn2_fwd_a",
                                      exchange=gather(w2b_shards))
    wg2b, wu2b, wd2b = rows_flat(w2b)
    x3, a2b, b2b = _ffn_fwd(xb, None, h3, wg2b, wu2b, wd2b, "ffn2_fwd_b")
    loss_row, dx3, dg_final, df3 = _loss_head(x3, norm_final.reshape(1, d), tgt, "loss_head")

    da2a, db2a, dwd_a, dwg_a, dwu_a = _ffn_bwdw(df3, a2a, b2a, h3, wd2a, "ffn2_bwdw_a")
    da2b, db2b, dwd_b, dwg_b, dwu_b = _ffn_bwdw(df3, a2b, b2b, h3, wd2b, "ffn2_bwdw_b")
    pairs = _pair_sum([by_dev(dwg_a), by_dev(dwu_a), by_dev(dwd_a), by_dev(dwg_b), by_dev(dwu_b), by_dev(dwd_b)],
                      "pair_sum_ffn2")
    dx2, dg_ffn2, *got_2a = _ffn_dx(dx3, x2, norm_ffn2, [(da2a, db2a, wg2a, wu2a), (da2b, db2b, wg2b, wu2b)],
                                    "ffn2_dx", exchange=scatter(pairs[:3]))
    dx1, dproj, cat, dg_mix, dconv, dpool_w, dpool_scale, df1, *got_2b = _mix_bwd(
        dx2, x1, proj, norm_mix, w_in_full, conv_full, pool_w_low, pool_scale, w_out_full, "mix_bwd",
        exchange=scatter(pairs[3:]))
    small_parts = [dg_mix, dg_ffn2, dg_final, dconv, dpool_w, dpool_scale, loss_row]
    small_sheet, spans = _to_sheet(small_parts)
    dw_in, got_small = _wgrad_tn(h2, dproj, W_IN_SHARD, True, "w_in_wgrad",
                                 exchange=_Exchange([small_sheet], [False]))
    (dw_out,) = _wgrad_tn(cat, dx2, d, False, "w_out_wgrad")
    pairs = _pair_sum([dw_in, by_dev(dw_out)], "pair_sum_mix")
    da1a, db1a, dwd, dwg, dwu, got_in, got_out = _ffn_bwdw(
        df1, a1a, b1a, h1, wd1a, "ffn1_bwdw_a", exchange=scatter(pairs))
    pairs = _pair_sum([by_dev(dwg), by_dev(dwu), by_dev(dwd)], "pair_sum_ffn1_a")
    da1b, db1b, dwd, dwg, dwu, *got_1a = _ffn_bwdw(df1, a1b, b1b, h1, wd1b, "ffn1_bwdw_b", exchange=scatter(pairs))
    pairs = _pair_sum([by_dev(dwg), by_dev(dwu), by_dev(dwd)], "pair_sum_ffn1_b")
    dx0, dg_ffn1, *got_1b = _ffn_dx(dx1, xs, norm_ffn1, [(da1a, db1a, wg1a, wu1a), (da1b, db1b, wg1b, wu1b)],
                                    "ffn1_dx", exchange=scatter(pairs))
    (got_n1,) = _exchange([dg_ffn1.reshape(8, 128)], [False], "gather_dnorm_ffn1")

    outs = {}

    def update(name, stacks, w, m, v):
        outs[name] = _adamw(stacks, w[0], m[0], v[0], "adamw_" + name)

    def update_t(name, stacks, w, m, v):
        res = _adamw(stacks, w[0].T, m[0].T, v[0].T, "adamw_" + name)
        outs[name] = tuple(r.T for r in res)

    update_t("ffn1_w_gate", [got_1a[0], got_1b[0]], ffn1_w_gate, m_ffn1_w_gate, v_ffn1_w_gate)
    update_t("ffn1_w_up", [got_1a[1], got_1b[1]], ffn1_w_up, m_ffn1_w_up, v_ffn1_w_up)
    update("ffn1_w_down", [got_1a[2], got_1b[2]], ffn1_w_down, m_ffn1_w_down, v_ffn1_w_down)
    update("w_in", got_in, w_in, m_w_in, v_w_in)
    update("w_out", got_out, w_out, m_w_out, v_w_out)
    update_t("ffn2_w_gate", [got_2a[0], got_2b[0]], ffn2_w_gate, m_ffn2_w_gate, v_ffn2_w_gate)
    update_t("ffn2_w_up", [got_2a[1], got_2b[1]], ffn2_w_up, m_ffn2_w_up, v_ffn2_w_up)
    update("ffn2_w_down", [got_2a[2], got_2b[2]], ffn2_w_down, m_ffn2_w_down, v_ffn2_w_down)

    g_small = _from_sheet(_sum_stack(got_small, "sum_small"), spans)
    g_norm_ffn1 = _sum_stack(got_n1, "sum_dnorm_ffn1").reshape(norm_ffn1.shape)
    g_conv = lax.dynamic_slice_in_dim(g_small[3], me * conv_shard, conv_shard, axis=1)
    small_names = ["norm_ffn1", "norm_mix", "norm_ffn2", "norm_final", "conv_w", "pool_w", "pool_scale"]
    small_g = [g_norm_ffn1, g_small[0], g_small[1], g_small[2].reshape(norm_final.shape), g_conv[None],
               g_small[4][None], g_small[5]]
    small_w = [norm_ffn1, norm_mix, norm_ffn2, norm_final, conv_w, pool_w, pool_scale]
    small_m = [m_norm_ffn1, m_norm_mix, m_norm_ffn2, m_norm_final, m_conv_w, m_pool_w, m_pool_scale]
    small_v = [v_norm_ffn1, v_norm_mix, v_norm_ffn2, v_norm_final, v_conv_w, v_pool_w, v_pool_scale]
    g_sheet, spans_u = _to_sheet(small_g)
    w_sheet, _ = _to_sheet(small_w)
    m_sheet, _ = _to_sheet(small_m)
    v_sheet, _ = _to_sheet(small_v)
    upd = _adamw(g_sheet[None], w_sheet, m_sheet, v_sheet, "adamw_small")
    small_out = [_from_sheet(u, spans_u) for u in upd]
    for k, nm in enumerate(small_names):
        outs[nm] = tuple(small_out[j][k] for j in range(4))

    loss = g_small[6][0, 0]
    order = ["norm_ffn1", "ffn1_w_gate", "ffn1_w_up", "ffn1_w_down", "norm_mix", "w_in", "conv_w", "pool_w",
             "pool_scale", "w_out", "norm_ffn2", "ffn2_w_gate", "ffn2_w_up", "ffn2_w_down", "norm_final"]
    big = {"ffn1_w_gate", "ffn1_w_up", "ffn1_w_down", "w_in", "w_out", "ffn2_w_gate", "ffn2_w_up", "ffn2_w_down"}

    def leaf(nm, j):
        val = outs[nm][j]
        return val[None] if nm in big else val

    return (loss, dx0[None],
            *[leaf(nm, 0) for nm in order], *[leaf(nm, 1) for nm in order],
            *[leaf(nm, 2) for nm in order], *[leaf(nm, 3) for nm in order])
```

```python
import jax
import jax.numpy as jnp
from jax import lax
from jax.experimental import pallas as pl
from jax.experimental.pallas import tpu as pltpu

F32 = jnp.float32
LOW = jnp.bfloat16

N_DEV = 8
EPS = 1e-6
D_CONV = 512
POOL_WINDOWS = (2, 4, 8, 16)
POOL_GC = 128
HALO = 16
W_IN_SHARD = 256

ADAM_LR = 0.001
ADAM_B1 = 0.9
ADAM_B2 = 0.999
ADAM_EPS = 1e-08
ADAM_WD = 0.01
ADAM_STEP = 10

VMEM_LIMIT_BYTES = 56 * 1024 * 1024
TOK_TILE = 512
MIX_TOK_TILE = 256
WGRAD_TOK_TILE = 4096
WGRAD_ROW_CANDIDATES = (256, 128)
RELAY_LAST_LATER = 0.25
BWD_ROW_SLAB = 2048


def _params(*sem):
    return pltpu.CompilerParams(dimension_semantics=sem, vmem_limit_bytes=VMEM_LIMIT_BYTES)


def _resident(arr):
    return pl.BlockSpec(arr.shape, lambda *_: (0,) * arr.ndim, pipeline_mode=pl.Buffered(1))


def _pick(n, candidates):
    for c in candidates:
        if n % c == 0:
            return c
    raise ValueError(f"no tile in {candidates} divides {n}")


def _dot(a, b):
    return lax.dot_general(a, b, (((1,), (0,)), ((), ())), preferred_element_type=F32)


def _dot_nt(a, b):
    return lax.dot_general(a, b, (((1,), (1,)), ((), ())), preferred_element_type=F32)


def _dot_tn(a, b):
    return lax.dot_general(a, b, (((0,), (0,)), ((), ())), preferred_element_type=F32)


def _rms_scale(x):
    return lax.rsqrt(jnp.mean(x * x, axis=-1, keepdims=True) + EPS)


def _rms_bwd(dy, x, g):
    r = _rms_scale(x)
    xhat = x * r
    gdy = dy * g
    dx = r * (gdy - xhat * jnp.mean(gdy * xhat, axis=-1, keepdims=True))
    return dx, jnp.sum(dy * xhat, axis=0, keepdims=True)


class _Exchange:
    CHIPS = (2, 4, 6)

    def __init__(self, arrays, sliced, relay_at=None, among_chips=False):
        assert relay_at is None or not any(sliced)
        assert not among_chips or (all(sliced) and relay_at is None)
        self.relay_at, self.among_chips = relay_at, among_chips
        self.peers = self.CHIPS if among_chips else ((1, 2, 4) if relay_at is not None else tuple(range(1, N_DEV)))
        self.arrays, self.sliced, self.n = list(arrays), list(sliced), len(arrays)
        self.out_shape = [jax.ShapeDtypeStruct(arr.shape if sl else (N_DEV,) + arr.shape, arr.dtype)
                          for arr, sl in zip(arrays, sliced)]
        self.specs = [pl.BlockSpec(memory_space=pl.ANY)] * self.n
        self.scratch_shapes = [pltpu.SemaphoreType.DMA((self.n, N_DEV)),
                               pltpu.SemaphoreType.DMA((self.n, N_DEV)),
                               pltpu.SemaphoreType.DMA((self.n,))]

    HALF_VIA = ((4, 2, 5), (2, 4, 7))

    def _halves(self, a):
        rows = self.arrays[a].shape[0]
        if rows % 32:
            return ((0, rows), None)
        return ((0, rows // 2), (rows // 2, rows // 2))

    def _copies(self, ins, outs, sems):
        send_sems, recv_sems, local_sems = sems
        sliced = self.sliced
        mx, my, mc = lax.axis_index("x"), lax.axis_index("y"), lax.axis_index("c")
        me = 2 * mx + my if self.among_chips else 4 * mx + 2 * my + mc

        def peer(m):
            px = lax.rem(mx + ((m >> 2) & 1), 2)
            py = lax.rem(my + ((m >> 1) & 1), 2)
            pc = lax.rem(mc + (m & 1), 2)
            return (px, py, pc), (2 * px + py if self.among_chips else 4 * px + 2 * py + pc)

        def remote(a, m, arriving):
            pid, pflat = peer(m)
            return pltpu.make_async_remote_copy(
                src_ref=ins[a].at[pflat] if sliced[a] else ins[a],
                dst_ref=outs[a].at[pflat if arriving else me],
                send_sem=send_sems.at[a, m - 1],
                recv_sem=recv_sems.at[a, m - 1],
                device_id=pid,
                device_id_type=pl.DeviceIdType.MESH,
            )

        def local(a):
            return pltpu.make_async_copy(ins[a].at[me] if sliced[a] else ins[a], outs[a].at[me], local_sems.at[a])

        def passed_on(a, m):
            _, origin = peer(m)
            sibling, _ = peer(1)
            return pltpu.make_async_remote_copy(
                src_ref=outs[a].at[origin],
                dst_ref=outs[a].at[origin],
                send_sem=send_sems.at[a, m],
                recv_sem=recv_sems.at[a, m],
                device_id=sibling,
                device_id_type=pl.DeviceIdType.MESH,
            )

        def half_on(a, h, arriving):
            via, to, column = self.HALF_VIA[h]
            r0, nr = self._halves(a)[h]
            _, origin = peer(6 if arriving else via)
            rows = outs[a].at[origin].at[pl.ds(r0, nr)]
            return pltpu.make_async_remote_copy(
                src_ref=rows, dst_ref=rows, send_sem=send_sems.at[a, column], recv_sem=recv_sems.at[a, column],
                device_id=peer(to)[0], device_id_type=pl.DeviceIdType.MESH)

        return remote, local, passed_on, half_on

    def start(self, ins, outs, sems):
        remote, local, _, _ = self._copies(ins, outs, sems)
        for a in range(self.n):
            local(a).start()
        for m in self.peers:
            for a in range(self.n):
                remote(a, m, False).start()

    def relay(self, ins, outs, sems):
        remote, _, passed_on, half_on = self._copies(ins, outs, sems)
        for h, (via, _, _) in enumerate(self.HALF_VIA):
            for a in range(self.n):
                remote(a, via, True).wait_recv()
                passed_on(a, via).start()
                if self._halves(a)[h] is not None:
                    half_on(a, h, False).start()

    def relay_last(self, ins, outs, sems):
        _, _, passed_on, half_on = self._copies(ins, outs, sems)
        for a in range(self.n):
            for h in range(2):
                if self._halves(a)[h] is not None:
                    half_on(a, h, True).wait_recv()
            passed_on(a, 6).start()

    def wait(self, ins, outs, sems):
        remote, local, passed_on, half_on = self._copies(ins, outs, sems)
        if self.relay_at is None:
            for m in self.peers:
                for a in range(self.n):
                    remote(a, m, True).wait_recv()
            for m in self.peers:
                for a in range(self.n):
                    remote(a, m, False).wait_send()
        else:
            for m in (1, 3, 5, 7):
                for a in range(self.n):
                    remote(a, m, True).wait_recv()
            for m in self.peers:
                for a in range(self.n):
                    remote(a, m, False).wait_send()
            for a in range(self.n):
                for m in self.CHIPS:
                    passed_on(a, m).wait_send()
                for h in range(2):
                    if self._halves(a)[h] is not None:
                        half_on(a, h, False).wait_send()
        for a in range(self.n):
            local(a).wait()


def _pair_sum(stacks, name):
    n = len(stacks)
    n_chip = N_DEV // 2
    half = [(n_chip,) + st.shape[1:] for st in stacks]

    def body(*refs):
        ins, outs, mine, theirs = refs[:n], refs[n:2 * n], refs[2 * n:3 * n], refs[3 * n:4 * n]
        local_sems, send_sems, recv_sems = refs[4 * n:]
        mx, my, mc = lax.axis_index("x"), lax.axis_index("y"), lax.axis_index("c")

        def own(a, k):
            return pltpu.make_async_copy(ins[a].at[2 * k + mc], mine[a].at[k], local_sems.at[a, k])

        def swap(a, k):
            return pltpu.make_async_remote_copy(
                src_ref=ins[a].at[2 * k + (1 - mc)], dst_ref=theirs[a].at[k],
                send_sem=send_sems.at[a, k], recv_sem=recv_sems.at[a, k],
                device_id=(mx, my, 1 - mc), device_id_type=pl.DeviceIdType.MESH)

        for k in range(n_chip):
            for a in range(n):
                own(a, k).start()
                swap(a, k).start()
        for k in range(n_chip):
            for a in range(n):
                own(a, k).wait()
                swap(a, k).wait()
                outs[a][k] = (mine[a][k].astype(F32) + theirs[a][k].astype(F32)).astype(LOW)

    return pl.pallas_call(
        body, name=name,
        out_shape=[jax.ShapeDtypeStruct(h, LOW) for h in half],
        in_specs=[pl.BlockSpec(memory_space=pl.ANY)] * n,
        out_specs=[pl.BlockSpec(memory_space=pltpu.VMEM)] * n,
        scratch_shapes=([pltpu.VMEM(h, st.dtype) for h, st in zip(half, stacks)] * 2
                        + [pltpu.SemaphoreType.DMA((n, n_chip))] * 3),
        compiler_params=pltpu.CompilerParams(vmem_limit_bytes=VMEM_LIMIT_BYTES),
    )(*stacks)


def _exchange(arrays, sliced, name, relay=False):
    ex = _Exchange(arrays, sliced, relay_at=0 if relay else None)

    def body(*refs):
        ins, outs, sems = refs[:ex.n], refs[ex.n:2 * ex.n], refs[2 * ex.n:]
        ex.start(ins, outs, sems)
        if relay:
            ex.relay(ins, outs, sems)
            ex.relay_last(ins, outs, sems)
        ex.wait(ins, outs, sems)

    return pl.pallas_call(body, name=name, out_shape=ex.out_shape, in_specs=ex.specs, out_specs=ex.specs,
                          scratch_shapes=ex.scratch_shapes)(*arrays)


def _call(body, *, name, grid, in_specs, out_specs, out_shape, args, scratch_shapes=(), exchange=None):
    params = _params(*(("arbitrary",) * len(grid)))
    if exchange is None:
        return pl.pallas_call(body, name=name, grid=grid, in_specs=in_specs, out_specs=out_specs, out_shape=out_shape,
                              scratch_shapes=list(scratch_shapes), compiler_params=params)(*args)
    exs = list(exchange) if isinstance(exchange, (list, tuple)) else [exchange]
    n_in, n_out, n_scr = len(in_specs), len(out_specs), len(scratch_shapes)
    n_ex = sum(ex.n for ex in exs)
    n_steps = 1
    for g in grid:
        n_steps *= g

    def hosted(*refs):
        ins, refs = refs[:n_in], refs[n_in:]
        ex_ins, refs = refs[:n_ex], refs[n_ex:]
        outs, refs = refs[:n_out], refs[n_out:]
        ex_outs, refs = refs[:n_ex], refs[n_ex:]
        scr, sems = refs[:n_scr], refs[n_scr:]
        parts, at = [], 0
        for j, ex in enumerate(exs):
            parts.append((ex_ins[at:at + ex.n], ex_outs[at:at + ex.n], sems[3 * j:3 * j + 3]))
            at += ex.n
        step = pl.program_id(0)
        for ax in range(1, len(grid)):
            step = step * grid[ax] + pl.program_id(ax)

        @pl.when(step == 0)
        def _():
            for ex, part in zip(exs, parts):
                ex.start(*part)

        body(*ins, *outs, *scr)

        for ex, part in zip(exs, parts):
            if ex.relay_at is not None:
                @pl.when(step == min(int(ex.relay_at * n_steps), n_steps - 1))
                def _():
                    ex.relay(*part)

                @pl.when(step == min(int((ex.relay_at + RELAY_LAST_LATER) * n_steps), n_steps - 1))
                def _():
                    ex.relay_last(*part)

        @pl.when(step == n_steps - 1)
        def _():
            for ex, part in zip(exs, parts):
                ex.wait(*part)

    return pl.pallas_call(
        hosted, name=name, grid=grid,
        in_specs=list(in_specs) + [sp for ex in exs for sp in ex.specs],
        out_specs=list(out_specs) + [sp for ex in exs for sp in ex.specs],
        out_shape=list(out_shape) + [sh for ex in exs for sh in ex.out_shape],
        scratch_shapes=list(scratch_shapes) + [sc for ex in exs for sc in ex.scratch_shapes],
        compiler_params=params)(*args, *[arr for ex in exs for arr in ex.arrays])


def _ffn_fwd(x, g, h, wgt, wut, wd, name, exchange=None):
    s_len, d = x.shape
    fc = wd.shape[0]
    ts = min(TOK_TILE, s_len)
    first = h is None

    def body(*refs):
        x_ref, gh_ref, wg_ref, wu_ref, wd_ref, xo_ref = refs[:6]
        a_ref, b_ref, s_ref = refs[-3:]
        xv = x_ref[...]
        if first:
            hb = (xv * _rms_scale(xv) * gh_ref[...]).astype(LOW)
            refs[6][...] = hb
        else:
            hb = gh_ref[...]
        for c0, c1 in _slabs(fc):
            a = _dot_nt(hb, wg_ref[c0:c1, :])
            b = _dot_nt(hb, wu_ref[c0:c1, :])
            s_ref[:, c0:c1] = (a * jax.nn.sigmoid(a) * b).astype(LOW)
            a_ref[:, c0:c1] = a.astype(LOW)
            b_ref[:, c0:c1] = b.astype(LOW)
        xo_ref[...] = xv + 0.5 * _dot(s_ref[...], wd_ref[...])

    tok = pl.BlockSpec((ts, d), lambda t: (t, 0))
    hid = pl.BlockSpec((ts, fc), lambda t: (t, 0))
    tok_out = jax.ShapeDtypeStruct((s_len, d), F32)
    h_out = jax.ShapeDtypeStruct((s_len, d), LOW)
    hid_out = jax.ShapeDtypeStruct((s_len, fc), LOW)
    return _call(
        body,
        name=name,
        grid=(s_len // ts,),
        in_specs=[tok, pl.BlockSpec((1, d), lambda t: (0, 0)) if first else tok,
                  _resident(wgt), _resident(wut), _resident(wd)],
        out_specs=[tok] + ([tok] if first else []) + [hid, hid],
        out_shape=[tok_out] + ([h_out] if first else []) + [hid_out, hid_out],
        scratch_shapes=[pltpu.VMEM((ts, fc), LOW)],
        args=(x, g if first else h, wgt, wut, wd),
        exchange=exchange,
    )


def _slabs(width, slab=256):
    return [(c0, min(c0 + slab, width)) for c0 in range(0, width, slab)]


def _ffn_bwdw(df, a, b, h, wd, name, exchange=None):
    s_len, d = df.shape
    f_len = wd.shape[0]
    tm = _pick(f_len, WGRAD_ROW_CANDIDATES)
    tk = min(WGRAD_TOK_TILE, s_len)
    n_k = s_len // tk

    def body(df_ref, a_ref, b_ref, h_ref, wd_ref, da_ref, db_ref, dwd_ref, dwg_ref, dwu_ref,
             s_ref, acc_d, acc_g, acc_u):
        k = pl.program_id(1)

        @pl.when(k == 0)
        def _():
            acc_d[...] = jnp.zeros_like(acc_d)
            acc_g[...] = jnp.zeros_like(acc_g)
            acc_u[...] = jnp.zeros_like(acc_u)

        wdv = wd_ref[...]
        for r0, r1 in _slabs(tk, BWD_ROW_SLAB):
            ds = _dot_nt(df_ref[r0:r1, :], wdv)
            av = a_ref[r0:r1, :].astype(F32)
            bv = b_ref[r0:r1, :].astype(F32)
            sig = jax.nn.sigmoid(av)
            silu = av * sig
            s_ref[r0:r1, :] = (silu * bv).astype(LOW)
            da_ref[r0:r1, :] = (ds * bv * (sig * (1.0 + av * (1.0 - sig)))).astype(LOW)
            db_ref[r0:r1, :] = (ds * silu).astype(LOW)
            hv = h_ref[r0:r1, :]
            acc_d[...] += _dot_tn(s_ref[r0:r1, :], df_ref[r0:r1, :])
            acc_g[...] += _dot_tn(da_ref[r0:r1, :], hv)
            acc_u[...] += _dot_tn(db_ref[r0:r1, :], hv)

        @pl.when(k == n_k - 1)
        def _():
            dwd_ref[...] = acc_d[...].astype(LOW)
            dwg_ref[...] = acc_g[...].astype(LOW)
            dwu_ref[...] = acc_u[...].astype(LOW)

    hid = pl.BlockSpec((tk, tm), lambda i, k: (k, i))
    tok = pl.BlockSpec((tk, d), lambda i, k: (k, 0))
    wrow = pl.BlockSpec((tm, d), lambda i, k: (i, 0))
    return _call(
        body,
        name=name,
        grid=(f_len // tm, n_k),
        in_specs=[tok, hid, hid, tok, wrow],
        out_specs=[hid, hid, wrow, wrow, wrow],
        out_shape=[jax.ShapeDtypeStruct((s_len, f_len), LOW)] * 2 + [jax.ShapeDtypeStruct((f_len, d), LOW)] * 3,
        scratch_shapes=[pltpu.VMEM((tk, tm), LOW)] + [pltpu.VMEM((tm, d), F32)] * 3,
        args=(df, a, b, h, wd),
        exchange=exchange,
    )


def _ffn_dx(dxo, x, g, parts, name, exchange=None):
    s_len, d = x.shape
    ts = min(TOK_TILE, s_len)
    n_p = len(parts)

    def body(dxo_ref, x_ref, g_ref, *refs):
        dxi_ref, dg_ref = refs[4 * n_p:]

        @pl.when(pl.program_id(0) == 0)
        def _():
            dg_ref[...] = jnp.zeros_like(dg_ref)

        dh = None
        for p in range(n_p):
            da_ref, db_ref, wg_ref, wu_ref = refs[4 * p:4 * p + 4]
            part = _dot(da_ref[...], wg_ref[...]) + _dot(db_ref[...], wu_ref[...])
            dh = part if dh is None else dh + part
        dx, dgp = _rms_bwd(dh, x_ref[...], g_ref[...])
        dxi_ref[...] = dxo_ref[...] + dx
        dg_ref[...] += dgp

    tok = pl.BlockSpec((ts, d), lambda t: (t, 0))
    vec = pl.BlockSpec((1, d), lambda t: (0, 0))
    part_specs, part_args = [], []
    for da, db, wgt, wut in parts:
        hid = pl.BlockSpec((ts, da.shape[1]), lambda t: (t, 0))
        part_specs += [hid, hid, _resident(wgt), _resident(wut)]
        part_args += [da, db, wgt, wut]
    return _call(
        body,
        name=name,
        grid=(s_len // ts,),
        in_specs=[tok, tok, vec] + part_specs,
        out_specs=[tok, vec],
        out_shape=[jax.ShapeDtypeStruct((s_len, d), F32), jax.ShapeDtypeStruct((1, d), F32)],
        args=(dxo, x, g, *part_args),
        exchange=exchange,
    )


def _wgrad_tn(xm, ym, tn, stacked, name, exchange=None):
    s_len, m = xm.shape
    n = ym.shape[1]
    tk = min(WGRAD_TOK_TILE, s_len)
    n_k = s_len // tk

    def body(x_ref, y_ref, o_ref, acc):
        k = pl.program_id(1)

        @pl.when(k == 0)
        def _():
            acc[...] = jnp.zeros_like(acc)

        acc[...] += _dot_tn(x_ref[...].astype(LOW), y_ref[...].astype(LOW))

        @pl.when(k == n_k - 1)
        def _():
            o_ref[...] = acc[...].astype(LOW)

    if stacked:
        out_spec = pl.BlockSpec((None, m, tn), lambda j, k: (j, 0, 0))
        out_shape = jax.ShapeDtypeStruct((n // tn, m, tn), LOW)
    else:
        out_spec = pl.BlockSpec((m, tn), lambda j, k: (0, j))
        out_shape = jax.ShapeDtypeStruct((m, n), LOW)
    return _call(
        body,
        name=name,
        grid=(n // tn, n_k),
        in_specs=[pl.BlockSpec((tk, m), lambda j, k: (k, 0)), pl.BlockSpec((tk, tn), lambda j, k: (k, j))],
        out_specs=[out_spec],
        out_shape=[out_shape],
        scratch_shapes=[pltpu.VMEM((m, tn), F32)],
        args=(xm, ym),
        exchange=exchange,
    )


def _mix_parts(ext_ref, cw, ts, row0):
    dc = D_CONV

    def back(off, c0, c1):
        return ext_ref[HALO - off:HALO - off + ts, c0:c1]

    v, gate_b, gate_c = back(0, 0, dc), back(0, dc, 2 * dc), back(0, 2 * dc, 3 * dc)
    z0 = gate_c * v
    z1 = back(1, 2 * dc, 3 * dc) * back(1, 0, dc)
    z2 = back(2, 2 * dc, 3 * dc) * back(2, 0, dc)
    conv = cw[2:3, :] * z0 + cw[1:2, :] * z1 + cw[0:1, :] * z2
    rows = row0 + lax.broadcasted_iota(jnp.int32, (ts, 1), 0)
    pooled, inv_count = [], []
    for grp, w in enumerate(POOL_WINDOWS):
        c0 = 3 * dc + POOL_GC * grp
        u = back(0, c0, c0 + POOL_GC)
        acc = u
        for j in range(1, w):
            acc = acc + back(j, c0, c0 + POOL_GC)
        inv = 1.0 / jnp.minimum(rows + 1, w).astype(F32)
        pooled.append(acc * inv - u)
        inv_count.append(inv)
    return v, gate_b, gate_c, z0, z1, z2, conv, pooled, inv_count


def _mix_fwd(x, g, w_in, conv_w, pool_w, pool_scale, w_out, name, exchange=None):
    s_len, d = x.shape
    n_blk, _, wcols = w_in.shape
    p_len = n_blk * wcols
    d_mix = w_out.shape[0]
    ts = min(MIX_TOK_TILE, s_len)
    dc = D_CONV

    def body(x_ref, g_ref, win_ref, cw_ref, pw_ref, ps_ref, wout_ref, x2_ref, h_ref, proj_ref, ext_ref, cat_ref):
        t = pl.program_id(0)

        @pl.when(t == 0)
        def _():
            ext_ref[0:HALO, :] = jnp.zeros((HALO, p_len), F32)

        xv = x_ref[...]
        hb = (xv * _rms_scale(xv) * g_ref[...]).astype(LOW)
        h_ref[...] = hb
        for k in range(n_blk):
            ext_ref[HALO:HALO + ts, wcols * k:wcols * (k + 1)] = _dot(hb, win_ref[k])
        proj_ref[...] = ext_ref[HALO:HALO + ts, :]

        _, gate_b, _, _, _, _, conv, pooled, _ = _mix_parts(ext_ref, cw_ref[...], ts, t * ts)
        cat_ref[:, 0:dc] = (gate_b * conv).astype(LOW)
        for grp in range(len(POOL_WINDOWS)):
            c0 = POOL_GC * grp
            lin = _dot(pooled[grp].astype(LOW), pw_ref[grp])
            cat_ref[:, dc + c0:dc + c0 + POOL_GC] = (lin * ps_ref[:, c0:c0 + POOL_GC]).astype(LOW)
        x2_ref[...] = xv + _dot(cat_ref[...], wout_ref[...])
        ext_ref[0:HALO, :] = ext_ref[ts:ts + HALO, :]

    tok = pl.BlockSpec((ts, d), lambda t: (t, 0))

    def whole(arr):
        return pl.BlockSpec(arr.shape, lambda t: (0,) * arr.ndim)

    return _call(
        body,
        name=name,
        grid=(s_len // ts,),
        in_specs=[tok, whole(g), _resident(w_in), whole(conv_w), whole(pool_w), whole(pool_scale), _resident(w_out)],
        out_specs=[tok, tok, pl.BlockSpec((ts, p_len), lambda t: (t, 0))],
        out_shape=[
            jax.ShapeDtypeStruct((s_len, d), F32),
            jax.ShapeDtypeStruct((s_len, d), LOW),
            jax.ShapeDtypeStruct((s_len, p_len), F32),
        ],
        scratch_shapes=[pltpu.VMEM((ts + HALO, p_len), F32), pltpu.VMEM((ts, d_mix), LOW)],
        args=(x, g, w_in, conv_w, pool_w, pool_scale, w_out),
        exchange=exchange,
    )


def _mix_bwd(dx2, x, proj, g, w_in, conv_w, pool_w, pool_scale, w_out, name, exchange=None):
    s_len, d = x.shape
    n_blk, _, wcols = w_in.shape
    p_len = n_blk * wcols
    d_mix = w_out.shape[0]
    ts = min(MIX_TOK_TILE, s_len)
    n_t = s_len // ts
    dc = D_CONV
    n_grp = len(POOL_WINDOWS)

    def body(dx2_ref, x_ref, proj_ref, halo_ref, g_ref, win_ref, cw_ref, pw_ref, ps_ref, wout_ref,
             dx_ref, dproj_ref, cat_ref, dg_ref, dcw_ref, dpw_ref, dps_ref, df_ref, ext_ref, fut_ref):
        i = pl.program_id(0)
        t = n_t - 1 - i

        @pl.when(i == 0)
        def _():
            dg_ref[...] = jnp.zeros_like(dg_ref)
            dcw_ref[...] = jnp.zeros_like(dcw_ref)
            dpw_ref[...] = jnp.zeros_like(dpw_ref)
            dps_ref[...] = jnp.zeros_like(dps_ref)
            fut_ref[ts:ts + HALO, :] = jnp.zeros((HALO, d_mix), F32)

        ext_ref[HALO:HALO + ts, :] = proj_ref[...]

        @pl.when(t == 0)
        def _():
            ext_ref[0:HALO, :] = jnp.zeros((HALO, p_len), F32)

        @pl.when(t > 0)
        def _():
            ext_ref[0:HALO, :] = halo_ref[...]

        cw = cw_ref[...]
        v, gate_b, gate_c, z0, z1, z2, conv, pooled, inv_count = _mix_parts(ext_ref, cw, ts, t * ts)
        dx2 = dx2_ref[...]
        dcat = _dot_nt(dx2.astype(LOW), wout_ref[...])

        dy_a = dcat[:, 0:dc]
        dconv = dy_a * gate_b
        fut_ref[0:ts, 0:dc] = dconv
        cat_ref[:, 0:dc] = (gate_b * conv).astype(LOW)
        dproj_ref[:, dc:2 * dc] = (dy_a * conv).astype(LOW)
        dcw_ref[2:3, :] += jnp.sum(dconv * z0, axis=0, keepdims=True)
        dcw_ref[1:2, :] += jnp.sum(dconv * z1, axis=0, keepdims=True)
        dcw_ref[0:1, :] += jnp.sum(dconv * z2, axis=0, keepdims=True)

        dpool = []
        for grp in range(n_grp):
            c0 = POOL_GC * grp
            pooled_b = pooled[grp].astype(LOW)
            lin = _dot(pooled_b, pw_ref[grp])
            dy_b = dcat[:, dc + c0:dc + c0 + POOL_GC]
            scale = ps_ref[:, c0:c0 + POOL_GC]
            cat_ref[:, dc + c0:dc + c0 + POOL_GC] = (lin * scale).astype(LOW)
            dps_ref[:, c0:c0 + POOL_GC] += jnp.sum(dy_b * lin, axis=0, keepdims=True)
            dlin = (dy_b * scale).astype(LOW)
            dpw_ref[grp] += _dot_tn(pooled_b, dlin)
            dpool.append(_dot_nt(dlin, pw_ref[grp]))
            fut_ref[0:ts, dc + c0:dc + c0 + POOL_GC] = dpool[grp] * inv_count[grp]

        def ahead(off, c0, c1):
            return fut_ref[off:off + ts, c0:c1]

        dz = cw[2:3, :] * ahead(0, 0, dc) + cw[1:2, :] * ahead(1, 0, dc) + cw[0:1, :] * ahead(2, 0, dc)
        dproj_ref[:, 0:dc] = (dz * gate_c).astype(LOW)
        dproj_ref[:, 2 * dc:3 * dc] = (dz * v).astype(LOW)
        for grp, w in enumerate(POOL_WINDOWS):
            c0 = dc + POOL_GC * grp
            acc = ahead(0, c0, c0 + POOL_GC)
            for j in range(1, w):
                acc = acc + ahead(j, c0, c0 + POOL_GC)
            dproj_ref[:, 2 * dc + c0:2 * dc + c0 + POOL_GC] = (acc - dpool[grp]).astype(LOW)

        dh = _dot_nt(dproj_ref[:, 0:wcols], win_ref[0])
        for k in range(1, n_blk):
            dh += _dot_nt(dproj_ref[:, wcols * k:wcols * (k + 1)], win_ref[k])
        dx, dgp = _rms_bwd(dh, x_ref[...], g_ref[...])
        dx = dx2 + dx
        dx_ref[...] = dx
        df_ref[...] = (0.5 * dx).astype(LOW)
        dg_ref[...] += dgp
        fut_ref[ts:ts + HALO, :] = fut_ref[0:HALO, :]

    tok = pl.BlockSpec((ts, d), lambda i: (n_t - 1 - i, 0))
    halo = pl.BlockSpec((HALO, p_len), lambda i: (jnp.maximum((n_t - 1 - i) * (ts // HALO) - 1, 0), 0))

    def whole(arr):
        return pl.BlockSpec(arr.shape, lambda i: (0,) * arr.ndim)

    return _call(
        body,
        name=name,
        grid=(n_t,),
        in_specs=[tok, tok, pl.BlockSpec((ts, p_len), lambda i: (n_t - 1 - i, 0)), halo,
                  whole(g), _resident(w_in), whole(conv_w), whole(pool_w), whole(pool_scale), _resident(w_out)],
        out_specs=[tok, pl.BlockSpec((ts, p_len), lambda i: (n_t - 1 - i, 0)),
                   pl.BlockSpec((ts, d_mix), lambda i: (n_t - 1 - i, 0)),
                   whole(g), whole(conv_w), whole(pool_w), whole(pool_scale), tok],
        out_shape=[
            jax.ShapeDtypeStruct((s_len, d), F32),
            jax.ShapeDtypeStruct((s_len, p_len), LOW),
            jax.ShapeDtypeStruct((s_len, d_mix), LOW),
            jax.ShapeDtypeStruct(g.shape, F32),
            jax.ShapeDtypeStruct(conv_w.shape, F32),
            jax.ShapeDtypeStruct(pool_w.shape, F32),
            jax.ShapeDtypeStruct(pool_scale.shape, F32),
            jax.ShapeDtypeStruct((s_len, d), LOW),
        ],
        scratch_shapes=[pltpu.VMEM((ts + HALO, p_len), F32), pltpu.VMEM((ts + HALO, d_mix), F32)],
        args=(dx2, x, proj, proj, g, w_in, conv_w, pool_w, pool_scale, w_out),
        exchange=exchange,
    )


def _loss_head(x, g, target, name):
    s_len, d = x.shape
    ts = min(TOK_TILE, s_len)

    def body(x_ref, g_ref, tgt_ref, loss_ref, dx_ref, dg_ref, df_ref):
        @pl.when(pl.program_id(0) == 0)
        def _():
            loss_ref[...] = jnp.zeros_like(loss_ref)
            dg_ref[...] = jnp.zeros_like(dg_ref)

        xv, gv = x_ref[...], g_ref[...]
        err = xv * _rms_scale(xv) * gv - tgt_ref[...]
        loss_ref[...] += 0.5 * jnp.sum(jnp.mean(err * err, axis=-1, keepdims=True), axis=0, keepdims=True)
        dx, dgp = _rms_bwd(err * (1.0 / d), xv, gv)
        dx_ref[...] = dx
        df_ref[...] = (0.5 * dx).astype(LOW)
        dg_ref[...] += dgp

    tok = pl.BlockSpec((ts, d), lambda t: (t, 0))
    vec = pl.BlockSpec((1, d), lambda t: (0, 0))
    return pl.pallas_call(
        body,
        name=name,
        grid=(s_len // ts,),
        in_specs=[tok, vec, tok],
        out_specs=[pl.BlockSpec((1, 128), lambda t: (0, 0)), tok, vec, tok],
        out_shape=[
            jax.ShapeDtypeStruct((1, 128), F32),
            jax.ShapeDtypeStruct((s_len, d), F32),
            jax.ShapeDtypeStruct((1, d), F32),
            jax.ShapeDtypeStruct((s_len, d), LOW),
        ],
        compiler_params=_params("arbitrary"),
    )(x, g, target)


def _row_tile(rows, cols, stack_bytes):
    budget = 20 * 1024 * 1024
    per_row = cols * (4 * 7 + stack_bytes)
    for tr in (rows, 512, 256, 176, 128, 64, 32, 16, 8):
        if rows % tr == 0 and tr % 8 == 0 and tr * per_row * 2 <= budget:
            return tr
    return rows


def _sum_stack(stack, name):
    n, r, c = stack.shape
    tr = _row_tile(r, c, n * stack.dtype.itemsize)

    def body(s_ref, o_ref):
        acc = s_ref[0].astype(F32)
        for k in range(1, n):
            acc = acc + s_ref[k].astype(F32)
        o_ref[...] = acc

    return pl.pallas_call(
        body,
        name=name,
        grid=(r // tr,),
        in_specs=[pl.BlockSpec((n, tr, c), lambda i: (0, i, 0))],
        out_specs=pl.BlockSpec((tr, c), lambda i: (i, 0)),
        out_shape=jax.ShapeDtypeStruct((r, c), F32),
        compiler_params=_params("arbitrary"),
    )(stack)


def _adamw(stacks, w, m, v, name):
    stacks = list(stacks) if isinstance(stacks, (list, tuple)) else [stacks]
    r, c = w.shape
    n = stacks[0].shape[0]
    part_rows = [st.shape[1] for st in stacks]
    assert sum(part_rows) == r and all(st.shape[0] == n for st in stacks)
    first_row = [sum(part_rows[:j]) for j in range(len(stacks))]
    tc = next(t for t in (512, 256, 128) if c % t == 0)
    c1 = 1.0 - ADAM_B1 ** ADAM_STEP
    c2 = 1.0 - ADAM_B2 ** ADAM_STEP

    def body(*refs):
        s_refs = refs[:len(stacks)]
        w_ref, m_ref, v_ref, g_ref, d_ref, mo_ref, vo_ref = refs[len(stacks):]
        for s_ref, r0, nr in zip(s_refs, first_row, part_rows):
            gv = s_ref[0].astype(F32)
            for k in range(1, n):
                gv = gv + s_ref[k].astype(F32)
            mn = ADAM_B1 * m_ref[r0:r0 + nr, :] + (1.0 - ADAM_B1) * gv
            vn = ADAM_B2 * v_ref[r0:r0 + nr, :] + (1.0 - ADAM_B2) * (gv * gv)
            g_ref[r0:r0 + nr, :] = gv
            mo_ref[r0:r0 + nr, :] = mn
            vo_ref[r0:r0 + nr, :] = vn
            d_ref[r0:r0 + nr, :] = -ADAM_LR * ((mn / c1) / (jnp.sqrt(vn / c2) + ADAM_EPS)
                                               + ADAM_WD * w_ref[r0:r0 + nr, :])

    blk = pl.BlockSpec((r, tc), lambda i: (0, i))
    return pl.pallas_call(
        body,
        name=name,
        grid=(c // tc,),
        in_specs=[pl.BlockSpec((n, nr, tc), lambda i: (0, 0, i)) for nr in part_rows] + [blk, blk, blk],
        out_specs=[blk] * 4,
        out_shape=[jax.ShapeDtypeStruct((r, c), F32)] * 4,
        compiler_params=_params("arbitrary"),
    )(*stacks, w, m, v)


def _to_sheet(parts):
    sheets, spans = [], []
    row = 0
    for p in parts:
        flat = p.reshape(-1).astype(F32)
        rows = -(-flat.shape[0] // 1024) * 8
        flat = jnp.pad(flat, (0, rows * 128 - flat.shape[0]))
        sheets.append(flat.reshape(rows, 128))
        spans.append((row, p.size, p.shape))
        row += rows
    return jnp.concatenate(sheets, axis=0), spans


def _from_sheet(sheet, spans):
    out = []
    for row, size, shape in spans:
        rows = -(-size // 1024) * 8
        out.append(sheet[row:row + rows].reshape(-1)[:size].reshape(shape))
    return out


def kernel(x, norm_ffn1, ffn1_w_gate, ffn1_w_up, ffn1_w_down, norm_mix, w_in, conv_w, pool_w, pool_scale, w_out, norm_ffn2, ffn2_w_gate, ffn2_w_up, ffn2_w_down, norm_final, loss_target, m_norm_ffn1, m_ffn1_w_gate, m_ffn1_w_up, m_ffn1_w_down, m_norm_mix, m_w_in, m_conv_w, m_pool_w, m_pool_scale, m_w_out, m_norm_ffn2, m_ffn2_w_gate, m_ffn2_w_up, m_ffn2_w_down, m_norm_final, v_norm_ffn1, v_ffn1_w_gate, v_ffn1_w_up, v_ffn1_w_down, v_norm_mix, v_w_in, v_conv_w, v_pool_w, v_pool_scale, v_w_out, v_norm_ffn2, v_ffn2_w_gate, v_ffn2_w_up, v_ffn2_w_down, v_norm_final):
    me = 4 * lax.axis_index("x") + 2 * lax.axis_index("y") + lax.axis_index("c")
    xs, tgt = x[0], loss_target[0]
    s_len, d = xs.shape
    f_shard = ffn1_w_down.shape[1]
    conv_shard = conv_w.shape[2]

    def low_t(wt):
        return wt[0].T.astype(LOW)

    def by_dev(gw):
        return gw.reshape(N_DEV, -1, d)

    conv_tile = jnp.zeros((8, 128), F32).at[0:conv_w.shape[1], 0:conv_shard].set(conv_w[0])
    pool_w_low = pool_w[0].astype(LOW)

    rows_a = -(-f_shard // 64) * 32

    def parts_of(w_gate, w_up, w_down):
        shards = [low_t(w_gate), low_t(w_up), w_down[0].astype(LOW)]
        return [s[:rows_a] for s in shards], [s[rows_a:] for s in shards]

    def rows_flat(stacks):
        return [st.reshape(-1, d) for st in stacks]

    def gather(shards):
        return _Exchange(shards, [False] * len(shards), relay_at=0.6)

    def scatter(pairs):
        return _Exchange(pairs, [True] * len(pairs), among_chips=True)

    w1a_shards, w1b_shards = parts_of(ffn1_w_gate, ffn1_w_up, ffn1_w_down)
    w2a_shards, w2b_shards = parts_of(ffn2_w_gate, ffn2_w_up, ffn2_w_down)

    wg1a, wu1a, wd1a = rows_flat(_exchange(w1a_shards, [False] * 3, "gather_ffn1_a", relay=True))
    xa, h1, a1a, b1a, *w1b = _ffn_fwd(xs, norm_ffn1, None, wg1a, wu1a, wd1a, "ffn1_fwd_a",
                                      exchange=gather(w1b_shards))
    wg1b, wu1b, wd1b = rows_flat(w1b)
    x1, a1b, b1b, w_in_full, w_out_full, conv_tiles = _ffn_fwd(
        xa, None, h1, wg1b, wu1b, wd1b, "ffn1_fwd_b",
        exchange=gather([w_in[0].astype(LOW), w_out[0].astype(LOW), conv_tile]))
    w_out_full = w_out_full.reshape(-1, d)
    conv_full = jnp.concatenate([conv_tiles[k, 0:conv_w.shape[1], 0:conv_shard] for k in range(N_DEV)], axis=1)
    x2, h2, proj, *w2a = _mix_fwd(x1, norm_mix, w_in_full, conv_full, pool_w_low, pool_scale, w_out_full, "mix_fwd",
                                  exchange=gather(w2a_shards))
    wg2a, wu2a, wd2a = rows_flat(w2a)
    xb, h3, a2a, b2a, *w2b = _ffn_fwd(x2, norm_ffn2, None, wg2a, wu2a, wd2a, "ffn2_fwd_a",
                                      exchange=gather(w2b_shards))
    wg2b, wu2b, wd2b = rows_flat(w2b)
    x3, a2b, b2b = _ffn_fwd(xb, None, h3, wg2b, wu2b, wd2b, "ffn2_fwd_b")
    loss_row, dx3, dg_final, df3 = _loss_head(x3, norm_final.reshape(1, d), tgt, "loss_head")

    da2a, db2a, dwd_a, dwg_a, dwu_a = _ffn_bwdw(df3, a2a, b2a, h3, wd2a, "ffn2_bwdw_a")
    da2b, db2b, dwd_b, dwg_b, dwu_b = _ffn_bwdw(df3, a2b, b2b, h3, wd2b, "ffn2_bwdw_b")
    pairs = _pair_sum([by_dev(dwg_a), by_dev(dwu_a), by_dev(dwd_a), by_dev(dwg_b), by_dev(dwu_b), by_dev(dwd_b)],
                      "pair_sum_ffn2")
    dx2, dg_ffn2, *got_2a = _ffn_dx(dx3, x2, norm_ffn2, [(da2a, db2a, wg2a, wu2a), (da2b, db2b, wg2b, wu2b)],
                                    "ffn2_dx", exchange=scatter(pairs[:3]))
    dx1, dproj, cat, dg_mix, dconv, dpool_w, dpool_scale, df1, *got_2b = _mix_bwd(
        dx2, x1, proj, norm_mix, w_in_full, conv_full, pool_w_low, pool_scale, w_out_full, "mix_bwd",
        exchange=scatter(pairs[3:]))
    small_parts = [dg_mix, dg_ffn2, dg_final, dconv, dpool_w, dpool_scale, loss_row]
    small_sheet, spans = _to_sheet(small_parts)
    dw_in, got_small = _wgrad_tn(h2, dproj, W_IN_SHARD, True, "w_in_wgrad",
                                 exchange=_Exchange([small_sheet], [False], relay_at=0.5))
    (dw_out,) = _wgrad_tn(cat, dx2, d, False, "w_out_wgrad")
    pairs = _pair_sum([dw_in, by_dev(dw_out)], "pair_sum_mix")
    da1a, db1a, dwd, dwg, dwu, got_in, got_out = _ffn_bwdw(
        df1, a1a, b1a, h1, wd1a, "ffn1_bwdw_a", exchange=scatter(pairs))
    pairs = _pair_sum([by_dev(dwg), by_dev(dwu), by_dev(dwd)], "pair_sum_ffn1_a")
    da1b, db1b, dwd, dwg, dwu, *got_1a = _ffn_bwdw(df1, a1b, b1b, h1, wd1b, "ffn1_bwdw_b", exchange=scatter(pairs))
    pairs = _pair_sum([by_dev(dwg), by_dev(dwu), by_dev(dwd)], "pair_sum_ffn1_b")
    dx0, dg_ffn1, *got_1b = _ffn_dx(dx1, xs, norm_ffn1, [(da1a, db1a, wg1a, wu1a), (da1b, db1b, wg1b, wu1b)],
                                    "ffn1_dx", exchange=scatter(pairs))
    (got_n1,) = _exchange([dg_ffn1.reshape(8, 128)], [False], "gather_dnorm_ffn1")

    outs = {}

    def update(name, stacks, w, m, v):
        outs[name] = _adamw(stacks, w[0], m[0], v[0], "adamw_" + name)

    def update_t(name, stacks, w, m, v):
        res = _adamw(stacks, w[0].T, m[0].T, v[0].T, "adamw_" + name)
        outs[name] = tuple(r.T for r in res)

    update_t("ffn1_w_gate", [got_1a[0], got_1b[0]], ffn1_w_gate, m_ffn1_w_gate, v_ffn1_w_gate)
    update_t("ffn1_w_up", [got_1a[1], got_1b[1]], ffn1_w_up, m_ffn1_w_up, v_ffn1_w_up)
    update("ffn1_w_down", [got_1a[2], got_1b[2]], ffn1_w_down, m_ffn1_w_down, v_ffn1_w_down)
    update("w_in", got_in, w_in, m_w_in, v_w_in)
    update("w_out", got_out, w_out, m_w_out, v_w_out)
    update_t("ffn2_w_gate", [got_2a[0], got_2b[0]], ffn2_w_gate, m_ffn2_w_gate, v_ffn2_w_gate)
    update_t("ffn2_w_up", [got_2a[1], got_2b[1]], ffn2_w_up, m_ffn2_w_up, v_ffn2_w_up)
    update("ffn2_w_down", [got_2a[2], got_2b[2]], ffn2_w_down, m_ffn2_w_down, v_ffn2_w_down)

    g_small = _from_sheet(_sum_stack(got_small, "sum_small"), spans)
    g_norm_ffn1 = _sum_stack(got_n1, "sum_dnorm_ffn1").reshape(norm_ffn1.shape)
    g_conv = lax.dynamic_slice_in_dim(g_small[3], me * conv_shard, conv_shard, axis=1)
    small_names = ["norm_ffn1", "norm_mix", "norm_ffn2", "norm_final", "conv_w", "pool_w", "pool_scale"]
    small_g = [g_norm_ffn1, g_small[0], g_small[1], g_small[2].reshape(norm_final.shape), g_conv[None],
               g_small[4][None], g_small[5]]
    small_w = [norm_ffn1, norm_mix, norm_ffn2, norm_final, conv_w, pool_w, pool_scale]
    small_m = [m_norm_ffn1, m_norm_mix, m_norm_ffn2, m_norm_final, m_conv_w, m_pool_w, m_pool_scale]
    small_v = [v_norm_ffn1, v_norm_mix, v_norm_ffn2, v_norm_final, v_conv_w, v_pool_w, v_pool_scale]
    g_sheet, spans_u = _to_sheet(small_g)
    w_sheet, _ = _to_sheet(small_w)
    m_sheet, _ = _to_sheet(small_m)
    v_sheet, _ = _to_sheet(small_v)
    upd = _adamw(g_sheet[None], w_sheet, m_sheet, v_sheet, "adamw_small")
    small_out = [_from_sheet(u, spans_u) for u in upd]
    for k, nm in enumerate(small_names):
        outs[nm] = tuple(small_out[j][k] for j in range(4))

    loss = g_small[6][0, 0]
    order = ["norm_ffn1", "ffn1_w_gate", "ffn1_w_up", "ffn1_w_down", "norm_mix", "w_in", "conv_w", "pool_w",
             "pool_scale", "w_out", "norm_ffn2", "ffn2_w_gate", "ffn2_w_up", "ffn2_w_down", "norm_final"]
    big = {"ffn1_w_gate", "ffn1_w_up", "ffn1_w_down", "w_in", "w_out", "ffn2_w_gate", "ffn2_w_up", "ffn2_w_down"}

    def leaf(nm, j):
        val = outs[nm][j]
        return val[None] if nm in big else val

    return (loss, dx0[None],
            *[leaf(nm, 0) for nm in order], *[leaf(nm, 1) for nm in order],
            *[leaf(nm, 2) for nm in order], *[leaf(nm, 3) for nm in order])
```

```python
import jax
import jax.numpy as jnp
from jax import lax
from jax.experimental import pallas as pl
from jax.experimental.pallas import tpu as pltpu

F32 = jnp.float32
LOW = jnp.bfloat16

N_DEV = 8
EPS = 1e-6
D_CONV = 512
POOL_WINDOWS = (2, 4, 8, 16)
POOL_GC = 128
HALO = 16
W_IN_SHARD = 256

ADAM_LR = 0.001
ADAM_B1 = 0.9
ADAM_B2 = 0.999
ADAM_EPS = 1e-08
ADAM_WD = 0.01
ADAM_STEP = 10

VMEM_LIMIT_BYTES = 56 * 1024 * 1024
TOK_TILE = 512
MIX_TOK_TILE = 512
MIX_ROW_SLAB = 256
WGRAD_TOK_TILE = 4096
WGRAD_ROW_CANDIDATES = (256, 128)
RELAY_LAST_LATER = 0.25
BWD_ROW_SLAB = 2048


def _params(*sem):
    return pltpu.CompilerParams(dimension_semantics=sem, vmem_limit_bytes=VMEM_LIMIT_BYTES)


def _resident(arr):
    return pl.BlockSpec(arr.shape, lambda *_: (0,) * arr.ndim, pipeline_mode=pl.Buffered(1))


def _pick(n, candidates):
    for c in candidates:
        if n % c == 0:
            return c
    raise ValueError(f"no tile in {candidates} divides {n}")


def _dot(a, b):
    return lax.dot_general(a, b, (((1,), (0,)), ((), ())), preferred_element_type=F32)


def _dot_nt(a, b):
    return lax.dot_general(a, b, (((1,), (1,)), ((), ())), preferred_element_type=F32)


def _dot_tn(a, b):
    return lax.dot_general(a, b, (((0,), (0,)), ((), ())), preferred_element_type=F32)


def _rms_scale(x):
    return lax.rsqrt(jnp.mean(x * x, axis=-1, keepdims=True) + EPS)


def _rms_bwd(dy, x, g):
    r = _rms_scale(x)
    xhat = x * r
    gdy = dy * g
    dx = r * (gdy - xhat * jnp.mean(gdy * xhat, axis=-1, keepdims=True))
    return dx, jnp.sum(dy * xhat, axis=0, keepdims=True)


class _Exchange:
    CHIPS = (2, 4, 6)

    def __init__(self, arrays, sliced, relay_at=None, among_chips=False):
        assert relay_at is None or not any(sliced)
        assert not among_chips or (all(sliced) and relay_at is None)
        self.relay_at, self.among_chips = relay_at, among_chips
        self.peers = self.CHIPS if among_chips else ((1, 2, 4) if relay_at is not None else tuple(range(1, N_DEV)))
        self.arrays, self.sliced, self.n = list(arrays), list(sliced), len(arrays)
        self.out_shape = [jax.ShapeDtypeStruct(arr.shape if sl else (N_DEV,) + arr.shape, arr.dtype)
                          for arr, sl in zip(arrays, sliced)]
        self.specs = [pl.BlockSpec(memory_space=pl.ANY)] * self.n
        self.scratch_shapes = [pltpu.SemaphoreType.DMA((self.n, N_DEV)),
                               pltpu.SemaphoreType.DMA((self.n, N_DEV)),
                               pltpu.SemaphoreType.DMA((self.n,))]

    HALF_VIA = ((4, 2, 5), (2, 4, 7))

    def _halves(self, a):
        rows = self.arrays[a].shape[0]
        if rows % 32:
            return ((0, rows), None)
        return ((0, rows // 2), (rows // 2, rows // 2))

    def _copies(self, ins, outs, sems):
        send_sems, recv_sems, local_sems = sems
        sliced = self.sliced
        mx, my, mc = lax.axis_index("x"), lax.axis_index("y"), lax.axis_index("c")
        me = 2 * mx + my if self.among_chips else 4 * mx + 2 * my + mc

        def peer(m):
            px = lax.rem(mx + ((m >> 2) & 1), 2)
            py = lax.rem(my + ((m >> 1) & 1), 2)
            pc = lax.rem(mc + (m & 1), 2)
            return (px, py, pc), (2 * px + py if self.among_chips else 4 * px + 2 * py + pc)

        def remote(a, m, arriving):
            pid, pflat = peer(m)
            return pltpu.make_async_remote_copy(
                src_ref=ins[a].at[pflat] if sliced[a] else ins[a],
                dst_ref=outs[a].at[pflat if arriving else me],
                send_sem=send_sems.at[a, m - 1],
                recv_sem=recv_sems.at[a, m - 1],
                device_id=pid,
                device_id_type=pl.DeviceIdType.MESH,
            )

        def local(a):
            return pltpu.make_async_copy(ins[a].at[me] if sliced[a] else ins[a], outs[a].at[me], local_sems.at[a])

        def passed_on(a, m):
            _, origin = peer(m)
            sibling, _ = peer(1)
            return pltpu.make_async_remote_copy(
                src_ref=outs[a].at[origin],
                dst_ref=outs[a].at[origin],
                send_sem=send_sems.at[a, m],
                recv_sem=recv_sems.at[a, m],
                device_id=sibling,
                device_id_type=pl.DeviceIdType.MESH,
            )

        def half_on(a, h, arriving):
            via, to, column = self.HALF_VIA[h]
            r0, nr = self._halves(a)[h]
            _, origin = peer(6 if arriving else via)
            rows = outs[a].at[origin].at[pl.ds(r0, nr)]
            return pltpu.make_async_remote_copy(
                src_ref=rows, dst_ref=rows, send_sem=send_sems.at[a, column], recv_sem=recv_sems.at[a, column],
                device_id=peer(to)[0], device_id_type=pl.DeviceIdType.MESH)

        return remote, local, passed_on, half_on

    def start(self, ins, outs, sems):
        remote, local, _, _ = self._copies(ins, outs, sems)
        for a in range(self.n):
            local(a).start()
        for m in self.peers:
            for a in range(self.n):
                remote(a, m, False).start()

    def relay(self, ins, outs, sems):
        remote, _, passed_on, half_on = self._copies(ins, outs, sems)
        for h, (via, _, _) in enumerate(self.HALF_VIA):
            for a in range(self.n):
                remote(a, via, True).wait_recv()
                passed_on(a, via).start()
                if self._halves(a)[h] is not None:
                    half_on(a, h, False).start()

    def relay_last(self, ins, outs, sems):
        _, _, passed_on, half_on = self._copies(ins, outs, sems)
        for a in range(self.n):
            for h in range(2):
                if self._halves(a)[h] is not None:
                    half_on(a, h, True).wait_recv()
            passed_on(a, 6).start()

    def wait(self, ins, outs, sems):
        remote, local, passed_on, half_on = self._copies(ins, outs, sems)
        if self.relay_at is None:
            for m in self.peers:
                for a in range(self.n):
                    remote(a, m, True).wait_recv()
            for m in self.peers:
                for a in range(self.n):
                    remote(a, m, False).wait_send()
        else:
            for m in (1, 3, 5, 7):
                for a in range(self.n):
                    remote(a, m, True).wait_recv()
            for m in self.peers:
                for a in range(self.n):
                    remote(a, m, False).wait_send()
            for a in range(self.n):
                for m in self.CHIPS:
                    passed_on(a, m).wait_send()
                for h in range(2):
                    if self._halves(a)[h] is not None:
                        half_on(a, h, False).wait_send()
        for a in range(self.n):
            local(a).wait()


def _pair_sum(stacks, name):
    n = len(stacks)
    n_chip = N_DEV // 2
    half = [(n_chip,) + st.shape[1:] for st in stacks]

    def body(*refs):
        ins, outs, mine, theirs = refs[:n], refs[n:2 * n], refs[2 * n:3 * n], refs[3 * n:4 * n]
        local_sems, send_sems, recv_sems = refs[4 * n:]
        mx, my, mc = lax.axis_index("x"), lax.axis_index("y"), lax.axis_index("c")

        def own(a, k):
            return pltpu.make_async_copy(ins[a].at[2 * k + mc], mine[a].at[k], local_sems.at[a, k])

        def swap(a, k):
            return pltpu.make_async_remote_copy(
                src_ref=ins[a].at[2 * k + (1 - mc)], dst_ref=theirs[a].at[k],
                send_sem=send_sems.at[a, k], recv_sem=recv_sems.at[a, k],
                device_id=(mx, my, 1 - mc), device_id_type=pl.DeviceIdType.MESH)

        for k in range(n_chip):
            for a in range(n):
                own(a, k).start()
                swap(a, k).start()
        for k in range(n_chip):
            for a in range(n):
                own(a, k).wait()
                swap(a, k).wait()
                outs[a][k] = (mine[a][k].astype(F32) + theirs[a][k].astype(F32)).astype(LOW)

    return pl.pallas_call(
        body, name=name,
        out_shape=[jax.ShapeDtypeStruct(h, LOW) for h in half],
        in_specs=[pl.BlockSpec(memory_space=pl.ANY)] * n,
        out_specs=[pl.BlockSpec(memory_space=pltpu.VMEM)] * n,
        scratch_shapes=([pltpu.VMEM(h, st.dtype) for h, st in zip(half, stacks)] * 2
                        + [pltpu.SemaphoreType.DMA((n, n_chip))] * 3),
        compiler_params=pltpu.CompilerParams(vmem_limit_bytes=VMEM_LIMIT_BYTES),
    )(*stacks)


def _exchange(arrays, sliced, name, relay=False):
    ex = _Exchange(arrays, sliced, relay_at=0 if relay else None)

    def body(*refs):
        ins, outs, sems = refs[:ex.n], refs[ex.n:2 * ex.n], refs[2 * ex.n:]
        ex.start(ins, outs, sems)
        if relay:
            ex.relay(ins, outs, sems)
            ex.relay_last(ins, outs, sems)
        ex.wait(ins, outs, sems)

    return pl.pallas_call(body, name=name, out_shape=ex.out_shape, in_specs=ex.specs, out_specs=ex.specs,
                          scratch_shapes=ex.scratch_shapes)(*arrays)


def _call(body, *, name, grid, in_specs, out_specs, out_shape, args, scratch_shapes=(), exchange=None):
    params = _params(*(("arbitrary",) * len(grid)))
    if exchange is None:
        return pl.pallas_call(body, name=name, grid=grid, in_specs=in_specs, out_specs=out_specs, out_shape=out_shape,
                              scratch_shapes=list(scratch_shapes), compiler_params=params)(*args)
    exs = list(exchange) if isinstance(exchange, (list, tuple)) else [exchange]
    n_in, n_out, n_scr = len(in_specs), len(out_specs), len(scratch_shapes)
    n_ex = sum(ex.n for ex in exs)
    n_steps = 1
    for g in grid:
        n_steps *= g

    def hosted(*refs):
        ins, refs = refs[:n_in], refs[n_in:]
        ex_ins, refs = refs[:n_ex], refs[n_ex:]
        outs, refs = refs[:n_out], refs[n_out:]
        ex_outs, refs = refs[:n_ex], refs[n_ex:]
        scr, sems = refs[:n_scr], refs[n_scr:]
        parts, at = [], 0
        for j, ex in enumerate(exs):
            parts.append((ex_ins[at:at + ex.n], ex_outs[at:at + ex.n], sems[3 * j:3 * j + 3]))
            at += ex.n
        step = pl.program_id(0)
        for ax in range(1, len(grid)):
            step = step * grid[ax] + pl.program_id(ax)

        @pl.when(step == 0)
        def _():
            for ex, part in zip(exs, parts):
                ex.start(*part)

        body(*ins, *outs, *scr)

        for ex, part in zip(exs, parts):
            if ex.relay_at is not None:
                @pl.when(step == min(int(ex.relay_at * n_steps), n_steps - 1))
                def _():
                    ex.relay(*part)

                @pl.when(step == min(int((ex.relay_at + RELAY_LAST_LATER) * n_steps), n_steps - 1))
                def _():
                    ex.relay_last(*part)

        @pl.when(step == n_steps - 1)
        def _():
            for ex, part in zip(exs, parts):
                ex.wait(*part)

    return pl.pallas_call(
        hosted, name=name, grid=grid,
        in_specs=list(in_specs) + [sp for ex in exs for sp in ex.specs],
        out_specs=list(out_specs) + [sp for ex in exs for sp in ex.specs],
        out_shape=list(out_shape) + [sh for ex in exs for sh in ex.out_shape],
        scratch_shapes=list(scratch_shapes) + [sc for ex in exs for sc in ex.scratch_shapes],
        compiler_params=params)(*args, *[arr for ex in exs for arr in ex.arrays])


def _ffn_fwd(x, g, h, wgt, wut, wd, name, exchange=None):
    s_len, d = x.shape
    fc = wd.shape[0]
    ts = min(TOK_TILE, s_len)
    first = h is None

    def body(*refs):
        x_ref, gh_ref, wg_ref, wu_ref, wd_ref, xo_ref = refs[:6]
        a_ref, b_ref, s_ref = refs[-3:]
        xv = x_ref[...]
        if first:
            hb = (xv * _rms_scale(xv) * gh_ref[...]).astype(LOW)
            refs[6][...] = hb
        else:
            hb = gh_ref[...]
        for c0, c1 in _slabs(fc):
            a = _dot_nt(hb, wg_ref[c0:c1, :])
            b = _dot_nt(hb, wu_ref[c0:c1, :])
            s_ref[:, c0:c1] = (a * jax.nn.sigmoid(a) * b).astype(LOW)
            a_ref[:, c0:c1] = a.astype(LOW)
            b_ref[:, c0:c1] = b.astype(LOW)
        xo_ref[...] = xv + 0.5 * _dot(s_ref[...], wd_ref[...])

    tok = pl.BlockSpec((ts, d), lambda t: (t, 0))
    hid = pl.BlockSpec((ts, fc), lambda t: (t, 0))
    tok_out = jax.ShapeDtypeStruct((s_len, d), F32)
    h_out = jax.ShapeDtypeStruct((s_len, d), LOW)
    hid_out = jax.ShapeDtypeStruct((s_len, fc), LOW)
    return _call(
        body,
        name=name,
        grid=(s_len // ts,),
        in_specs=[tok, pl.BlockSpec((1, d), lambda t: (0, 0)) if first else tok,
                  _resident(wgt), _resident(wut), _resident(wd)],
        out_specs=[tok] + ([tok] if first else []) + [hid, hid],
        out_shape=[tok_out] + ([h_out] if first else []) + [hid_out, hid_out],
        scratch_shapes=[pltpu.VMEM((ts, fc), LOW)],
        args=(x, g if first else h, wgt, wut, wd),
        exchange=exchange,
    )


def _slabs(width, slab=256):
    return [(c0, min(c0 + slab, width)) for c0 in range(0, width, slab)]


def _ffn_bwdw(df, a, b, h, wd, name, exchange=None):
    s_len, d = df.shape
    f_len = wd.shape[0]
    tm = _pick(f_len, WGRAD_ROW_CANDIDATES)
    tk = min(WGRAD_TOK_TILE, s_len)
    n_k = s_len // tk

    def body(df_ref, a_ref, b_ref, h_ref, wd_ref, da_ref, db_ref, dwd_ref, dwg_ref, dwu_ref,
             s_ref, acc_d, acc_g, acc_u):
        k = pl.program_id(1)

        @pl.when(k == 0)
        def _():
            acc_d[...] = jnp.zeros_like(acc_d)
            acc_g[...] = jnp.zeros_like(acc_g)
            acc_u[...] = jnp.zeros_like(acc_u)

        wdv = wd_ref[...]
        for r0, r1 in _slabs(tk, BWD_ROW_SLAB):
            ds = _dot_nt(df_ref[r0:r1, :], wdv)
            av = a_ref[r0:r1, :].astype(F32)
            bv = b_ref[r0:r1, :].astype(F32)
            sig = jax.nn.sigmoid(av)
            silu = av * sig
            s_ref[r0:r1, :] = (silu * bv).astype(LOW)
            da_ref[r0:r1, :] = (ds * bv * (sig * (1.0 + av * (1.0 - sig)))).astype(LOW)
            db_ref[r0:r1, :] = (ds * silu).astype(LOW)
            hv = h_ref[r0:r1, :]
            acc_d[...] += _dot_tn(s_ref[r0:r1, :], df_ref[r0:r1, :])
            acc_g[...] += _dot_tn(da_ref[r0:r1, :], hv)
            acc_u[...] += _dot_tn(db_ref[r0:r1, :], hv)

        @pl.when(k == n_k - 1)
        def _():
            dwd_ref[...] = acc_d[...].astype(LOW)
            dwg_ref[...] = acc_g[...].astype(LOW)
            dwu_ref[...] = acc_u[...].astype(LOW)

    hid = pl.BlockSpec((tk, tm), lambda i, k: (k, i))
    tok = pl.BlockSpec((tk, d), lambda i, k: (k, 0))
    wrow = pl.BlockSpec((tm, d), lambda i, k: (i, 0))
    return _call(
        body,
        name=name,
        grid=(f_len // tm, n_k),
        in_specs=[tok, hid, hid, tok, wrow],
        out_specs=[hid, hid, wrow, wrow, wrow],
        out_shape=[jax.ShapeDtypeStruct((s_len, f_len), LOW)] * 2 + [jax.ShapeDtypeStruct((f_len, d), LOW)] * 3,
        scratch_shapes=[pltpu.VMEM((tk, tm), LOW)] + [pltpu.VMEM((tm, d), F32)] * 3,
        args=(df, a, b, h, wd),
        exchange=exchange,
    )


def _ffn_dx(dxo, x, g, parts, name, exchange=None):
    s_len, d = x.shape
    ts = min(TOK_TILE, s_len)
    n_p = len(parts)

    def body(dxo_ref, x_ref, g_ref, *refs):
        dxi_ref, dg_ref = refs[4 * n_p:]

        @pl.when(pl.program_id(0) == 0)
        def _():
            dg_ref[...] = jnp.zeros_like(dg_ref)

        dh = None
        for p in range(n_p):
            da_ref, db_ref, wg_ref, wu_ref = refs[4 * p:4 * p + 4]
            part = _dot(da_ref[...], wg_ref[...]) + _dot(db_ref[...], wu_ref[...])
            dh = part if dh is None else dh + part
        dx, dgp = _rms_bwd(dh, x_ref[...], g_ref[...])
        dxi_ref[...] = dxo_ref[...] + dx
        dg_ref[...] += dgp

    tok = pl.BlockSpec((ts, d), lambda t: (t, 0))
    vec = pl.BlockSpec((1, d), lambda t: (0, 0))
    part_specs, part_args = [], []
    for da, db, wgt, wut in parts:
        hid = pl.BlockSpec((ts, da.shape[1]), lambda t: (t, 0))
        part_specs += [hid, hid, _resident(wgt), _resident(wut)]
        part_args += [da, db, wgt, wut]
    return _call(
        body,
        name=name,
        grid=(s_len // ts,),
        in_specs=[tok, tok, vec] + part_specs,
        out_specs=[tok, vec],
        out_shape=[jax.ShapeDtypeStruct((s_len, d), F32), jax.ShapeDtypeStruct((1, d), F32)],
        args=(dxo, x, g, *part_args),
        exchange=exchange,
    )


def _wgrad_tn(xm, ym, tn, stacked, name, exchange=None):
    s_len, m = xm.shape
    n = ym.shape[1]
    tk = min(WGRAD_TOK_TILE, s_len)
    n_k = s_len // tk

    def body(x_ref, y_ref, o_ref, acc):
        k = pl.program_id(1)

        @pl.when(k == 0)
        def _():
            acc[...] = jnp.zeros_like(acc)

        acc[...] += _dot_tn(x_ref[...].astype(LOW), y_ref[...].astype(LOW))

        @pl.when(k == n_k - 1)
        def _():
            o_ref[...] = acc[...].astype(LOW)

    if stacked:
        out_spec = pl.BlockSpec((None, m, tn), lambda j, k: (j, 0, 0))
        out_shape = jax.ShapeDtypeStruct((n // tn, m, tn), LOW)
    else:
        out_spec = pl.BlockSpec((m, tn), lambda j, k: (0, j))
        out_shape = jax.ShapeDtypeStruct((m, n), LOW)
    return _call(
        body,
        name=name,
        grid=(n // tn, n_k),
        in_specs=[pl.BlockSpec((tk, m), lambda j, k: (k, 0)), pl.BlockSpec((tk, tn), lambda j, k: (k, j))],
        out_specs=[out_spec],
        out_shape=[out_shape],
        scratch_shapes=[pltpu.VMEM((m, tn), F32)],
        args=(xm, ym),
        exchange=exchange,
    )


def _mix_parts(ext_ref, cw, ts, row0, r0=0):
    dc = D_CONV

    def back(off, c0, c1):
        return ext_ref[HALO + r0 - off:HALO + r0 - off + ts, c0:c1]

    v, gate_b, gate_c = back(0, 0, dc), back(0, dc, 2 * dc), back(0, 2 * dc, 3 * dc)
    z0 = gate_c * v
    z1 = back(1, 2 * dc, 3 * dc) * back(1, 0, dc)
    z2 = back(2, 2 * dc, 3 * dc) * back(2, 0, dc)
    conv = cw[2:3, :] * z0 + cw[1:2, :] * z1 + cw[0:1, :] * z2
    rows = row0 + lax.broadcasted_iota(jnp.int32, (ts, 1), 0)
    pooled, inv_count = [], []
    for grp, w in enumerate(POOL_WINDOWS):
        c0 = 3 * dc + POOL_GC * grp
        u = back(0, c0, c0 + POOL_GC)
        acc = u
        for j in range(1, w):
            acc = acc + back(j, c0, c0 + POOL_GC)
        inv = 1.0 / jnp.minimum(rows + 1, w).astype(F32)
        pooled.append(acc * inv - u)
        inv_count.append(inv)
    return v, gate_b, gate_c, z0, z1, z2, conv, pooled, inv_count


def _mix_fwd(x, g, w_in, conv_w, pool_w, pool_scale, w_out, name, exchange=None):
    s_len, d = x.shape
    n_blk, _, wcols = w_in.shape
    p_len = n_blk * wcols
    d_mix = w_out.shape[0]
    ts = min(MIX_TOK_TILE, s_len)
    dc = D_CONV

    def body(x_ref, g_ref, win_ref, cw_ref, pw_ref, ps_ref, wout_ref, x2_ref, h_ref, proj_ref, ext_ref, cat_ref):
        t = pl.program_id(0)

        @pl.when(t == 0)
        def _():
            ext_ref[0:HALO, :] = jnp.zeros((HALO, p_len), F32)

        for r0, r1 in _slabs(ts, MIX_ROW_SLAB):
            nr = r1 - r0
            xv = x_ref[r0:r1, :]
            hb = (xv * _rms_scale(xv) * g_ref[...]).astype(LOW)
            h_ref[r0:r1, :] = hb
            for k in range(n_blk):
                ext_ref[HALO + r0:HALO + r1, wcols * k:wcols * (k + 1)] = _dot(hb, win_ref[k])
            proj_ref[r0:r1, :] = ext_ref[HALO + r0:HALO + r1, :]

            _, gate_b, _, _, _, _, conv, pooled, _ = _mix_parts(ext_ref, cw_ref[...], nr, t * ts + r0, r0)
            cat_ref[r0:r1, 0:dc] = (gate_b * conv).astype(LOW)
            for grp in range(len(POOL_WINDOWS)):
                c0 = POOL_GC * grp
                lin = _dot(pooled[grp].astype(LOW), pw_ref[grp])
                cat_ref[r0:r1, dc + c0:dc + c0 + POOL_GC] = (lin * ps_ref[:, c0:c0 + POOL_GC]).astype(LOW)
            x2_ref[r0:r1, :] = xv + _dot(cat_ref[r0:r1, :], wout_ref[...])
        ext_ref[0:HALO, :] = ext_ref[ts:ts + HALO, :]

    tok = pl.BlockSpec((ts, d), lambda t: (t, 0))

    def whole(arr):
        return pl.BlockSpec(arr.shape, lambda t: (0,) * arr.ndim)

    return _call(
        body,
        name=name,
        grid=(s_len // ts,),
        in_specs=[tok, whole(g), _resident(w_in), whole(conv_w), whole(pool_w), whole(pool_scale), _resident(w_out)],
        out_specs=[tok, tok, pl.BlockSpec((ts, p_len), lambda t: (t, 0))],
        out_shape=[
            jax.ShapeDtypeStruct((s_len, d), F32),
            jax.ShapeDtypeStruct((s_len, d), LOW),
            jax.ShapeDtypeStruct((s_len, p_len), F32),
        ],
        scratch_shapes=[pltpu.VMEM((ts + HALO, p_len), F32), pltpu.VMEM((ts, d_mix), LOW)],
        args=(x, g, w_in, conv_w, pool_w, pool_scale, w_out),
        exchange=exchange,
    )


def _mix_bwd(dx2, x, proj, g, w_in, conv_w, pool_w, pool_scale, w_out, name, exchange=None):
    s_len, d = x.shape
    n_blk, _, wcols = w_in.shape
    p_len = n_blk * wcols
    d_mix = w_out.shape[0]
    ts = min(MIX_TOK_TILE, s_len)
    n_t = s_len // ts
    dc = D_CONV
    n_grp = len(POOL_WINDOWS)

    def body(dx2_ref, x_ref, proj_ref, halo_ref, g_ref, win_ref, cw_ref, pw_ref, ps_ref, wout_ref,
             dx_ref, dproj_ref, cat_ref, dg_ref, dcw_ref, dpw_ref, dps_ref, df_ref, ext_ref, fut_ref):
        i = pl.program_id(0)
        t = n_t - 1 - i

        @pl.when(i == 0)
        def _():
            dg_ref[...] = jnp.zeros_like(dg_ref)
            dcw_ref[...] = jnp.zeros_like(dcw_ref)
            dpw_ref[...] = jnp.zeros_like(dpw_ref)
            dps_ref[...] = jnp.zeros_like(dps_ref)
            fut_ref[ts:ts + HALO, :] = jnp.zeros((HALO, d_mix), F32)

        ext_ref[HALO:HALO + ts, :] = proj_ref[...]

        @pl.when(t == 0)
        def _():
            ext_ref[0:HALO, :] = jnp.zeros((HALO, p_len), F32)

        @pl.when(t > 0)
        def _():
            ext_ref[0:HALO, :] = halo_ref[...]

        cw = cw_ref[...]
        for r0, r1 in reversed(_slabs(ts, MIX_ROW_SLAB)):
            nr = r1 - r0
            v, gate_b, gate_c, z0, z1, z2, conv, pooled, inv_count = _mix_parts(ext_ref, cw, nr, t * ts + r0, r0)
            dx2 = dx2_ref[r0:r1, :]
            dcat = _dot_nt(dx2.astype(LOW), wout_ref[...])

            dy_a = dcat[:, 0:dc]
            dconv = dy_a * gate_b
            fut_ref[r0:r1, 0:dc] = dconv
            cat_ref[r0:r1, 0:dc] = (gate_b * conv).astype(LOW)
            dproj_ref[r0:r1, dc:2 * dc] = (dy_a * conv).astype(LOW)
            dcw_ref[2:3, :] += jnp.sum(dconv * z0, axis=0, keepdims=True)
            dcw_ref[1:2, :] += jnp.sum(dconv * z1, axis=0, keepdims=True)
            dcw_ref[0:1, :] += jnp.sum(dconv * z2, axis=0, keepdims=True)

            dpool = []
            for grp in range(n_grp):
                c0 = POOL_GC * grp
                pooled_b = pooled[grp].astype(LOW)
                lin = _dot(pooled_b, pw_ref[grp])
                dy_b = dcat[:, dc + c0:dc + c0 + POOL_GC]
                scale = ps_ref[:, c0:c0 + POOL_GC]
                cat_ref[r0:r1, dc + c0:dc + c0 + POOL_GC] = (lin * scale).astype(LOW)
                dps_ref[:, c0:c0 + POOL_GC] += jnp.sum(dy_b * lin, axis=0, keepdims=True)
                dlin = (dy_b * scale).astype(LOW)
                dpw_ref[grp] += _dot_tn(pooled_b, dlin)
                dpool.append(_dot_nt(dlin, pw_ref[grp]))
                fut_ref[r0:r1, dc + c0:dc + c0 + POOL_GC] = dpool[grp] * inv_count[grp]

            def ahead(off, c0, c1):
                return fut_ref[r0 + off:r1 + off, c0:c1]

            dz = cw[2:3, :] * ahead(0, 0, dc) + cw[1:2, :] * ahead(1, 0, dc) + cw[0:1, :] * ahead(2, 0, dc)
            dproj_ref[r0:r1, 0:dc] = (dz * gate_c).astype(LOW)
            dproj_ref[r0:r1, 2 * dc:3 * dc] = (dz * v).astype(LOW)
            for grp, w in enumerate(POOL_WINDOWS):
                c0 = dc + POOL_GC * grp
                acc = ahead(0, c0, c0 + POOL_GC)
                for j in range(1, w):
                    acc = acc + ahead(j, c0, c0 + POOL_GC)
                dproj_ref[r0:r1, 2 * dc + c0:2 * dc + c0 + POOL_GC] = (acc - dpool[grp]).astype(LOW)

            dh = _dot_nt(dproj_ref[r0:r1, 0:wcols], win_ref[0])
            for k in range(1, n_blk):
                dh += _dot_nt(dproj_ref[r0:r1, wcols * k:wcols * (k + 1)], win_ref[k])
            dx, dgp = _rms_bwd(dh, x_ref[r0:r1, :], g_ref[...])
            dx = dx2 + dx
            dx_ref[r0:r1, :] = dx
            df_ref[r0:r1, :] = (0.5 * dx).astype(LOW)
            dg_ref[...] += dgp
        fut_ref[ts:ts + HALO, :] = fut_ref[0:HALO, :]

    tok = pl.BlockSpec((ts, d), lambda i: (n_t - 1 - i, 0))
    halo = pl.BlockSpec((HALO, p_len), lambda i: (jnp.maximum((n_t - 1 - i) * (ts // HALO) - 1, 0), 0))

    def whole(arr):
        return pl.BlockSpec(arr.shape, lambda i: (0,) * arr.ndim)

    return _call(
        body,
        name=name,
        grid=(n_t,),
        in_specs=[tok, tok, pl.BlockSpec((ts, p_len), lambda i: (n_t - 1 - i, 0)), halo,
                  whole(g), _resident(w_in), whole(conv_w), whole(pool_w), whole(pool_scale), _resident(w_out)],
        out_specs=[tok, pl.BlockSpec((ts, p_len), lambda i: (n_t - 1 - i, 0)),
                   pl.BlockSpec((ts, d_mix), lambda i: (n_t - 1 - i, 0)),
                   whole(g), whole(conv_w), whole(pool_w), whole(pool_scale), tok],
        out_shape=[
            jax.ShapeDtypeStruct((s_len, d), F32),
            jax.ShapeDtypeStruct((s_len, p_len), LOW),
            jax.ShapeDtypeStruct((s_len, d_mix), LOW),
            jax.ShapeDtypeStruct(g.shape, F32),
            jax.ShapeDtypeStruct(conv_w.shape, F32),
            jax.ShapeDtypeStruct(pool_w.shape, F32),
            jax.ShapeDtypeStruct(pool_scale.shape, F32),
            jax.ShapeDtypeStruct((s_len, d), LOW),
        ],
        scratch_shapes=[pltpu.VMEM((ts + HALO, p_len), F32), pltpu.VMEM((ts + HALO, d_mix), F32)],
        args=(dx2, x, proj, proj, g, w_in, conv_w, pool_w, pool_scale, w_out),
        exchange=exchange,
    )


def _loss_head(x, g, target, name):
    s_len, d = x.shape
    ts = min(TOK_TILE, s_len)

    def body(x_ref, g_ref, tgt_ref, loss_ref, dx_ref, dg_ref, df_ref):
        @pl.when(pl.program_id(0) == 0)
        def _():
            loss_ref[...] = jnp.zeros_like(loss_ref)
            dg_ref[...] = jnp.zeros_like(dg_ref)

        xv, gv = x_ref[...], g_ref[...]
        err = xv * _rms_scale(xv) * gv - tgt_ref[...]
        loss_ref[...] += 0.5 * jnp.sum(jnp.mean(err * err, axis=-1, keepdims=True), axis=0, keepdims=True)
        dx, dgp = _rms_bwd(err * (1.0 / d), xv, gv)
        dx_ref[...] = dx
        df_ref[...] = (0.5 * dx).astype(LOW)
        dg_ref[...] += dgp

    tok = pl.BlockSpec((ts, d), lambda t: (t, 0))
    vec = pl.BlockSpec((1, d), lambda t: (0, 0))
    return pl.pallas_call(
        body,
        name=name,
        grid=(s_len // ts,),
        in_specs=[tok, vec, tok],
        out_specs=[pl.BlockSpec((1, 128), lambda t: (0, 0)), tok, vec, tok],
        out_shape=[
            jax.ShapeDtypeStruct((1, 128), F32),
            jax.ShapeDtypeStruct((s_len, d), F32),
            jax.ShapeDtypeStruct((1, d), F32),
            jax.ShapeDtypeStruct((s_len, d), LOW),
        ],
        compiler_params=_params("arbitrary"),
    )(x, g, target)


def _row_tile(rows, cols, stack_bytes):
    budget = 20 * 1024 * 1024
    per_row = cols * (4 * 7 + stack_bytes)
    for tr in (rows, 512, 256, 176, 128, 64, 32, 16, 8):
        if rows % tr == 0 and tr % 8 == 0 and tr * per_row * 2 <= budget:
            return tr
    return rows


def _sum_stack(stack, name):
    n, r, c = stack.shape
    tr = _row_tile(r, c, n * stack.dtype.itemsize)

    def body(s_ref, o_ref):
        acc = s_ref[0].astype(F32)
        for k in range(1, n):
            acc = acc + s_ref[k].astype(F32)
        o_ref[...] = acc

    return pl.pallas_call(
        body,
        name=name,
        grid=(r // tr,),
        in_specs=[pl.BlockSpec((n, tr, c), lambda i: (0, i, 0))],
        out_specs=pl.BlockSpec((tr, c), lambda i: (i, 0)),
        out_shape=jax.ShapeDtypeStruct((r, c), F32),
        compiler_params=_params("arbitrary"),
    )(stack)


def _adamw(stacks, w, m, v, name):
    stacks = list(stacks) if isinstance(stacks, (list, tuple)) else [stacks]
    r, c = w.shape
    n = stacks[0].shape[0]
    part_rows = [st.shape[1] for st in stacks]
    assert sum(part_rows) == r and all(st.shape[0] == n for st in stacks)
    first_row = [sum(part_rows[:j]) for j in range(len(stacks))]
    tc = next(t for t in (512, 256, 128) if c % t == 0)
    c1 = 1.0 - ADAM_B1 ** ADAM_STEP
    c2 = 1.0 - ADAM_B2 ** ADAM_STEP

    def body(*refs):
        s_refs = refs[:len(stacks)]
        w_ref, m_ref, v_ref, g_ref, d_ref, mo_ref, vo_ref = refs[len(stacks):]
        for s_ref, r0, nr in zip(s_refs, first_row, part_rows):
            gv = s_ref[0].astype(F32)
            for k in range(1, n):
                gv = gv + s_ref[k].astype(F32)
            mn = ADAM_B1 * m_ref[r0:r0 + nr, :] + (1.0 - ADAM_B1) * gv
            vn = ADAM_B2 * v_ref[r0:r0 + nr, :] + (1.0 - ADAM_B2) * (gv * gv)
            g_ref[r0:r0 + nr, :] = gv
            mo_ref[r0:r0 + nr, :] = mn
            vo_ref[r0:r0 + nr, :] = vn
            d_ref[r0:r0 + nr, :] = -ADAM_LR * ((mn / c1) / (jnp.sqrt(vn / c2) + ADAM_EPS)
                                               + ADAM_WD * w_ref[r0:r0 + nr, :])

    blk = pl.BlockSpec((r, tc), lambda i: (0, i))
    return pl.pallas_call(
        body,
        name=name,
        grid=(c // tc,),
        in_specs=[pl.BlockSpec((n, nr, tc), lambda i: (0, 0, i)) for nr in part_rows] + [blk, blk, blk],
        out_specs=[blk] * 4,
        out_shape=[jax.ShapeDtypeStruct((r, c), F32)] * 4,
        compiler_params=_params("arbitrary"),
    )(*stacks, w, m, v)


def _to_sheet(parts):
    sheets, spans = [], []
    row = 0
    for p in parts:
        flat = p.reshape(-1).astype(F32)
        rows = -(-flat.shape[0] // 1024) * 8
        flat = jnp.pad(flat, (0, rows * 128 - flat.shape[0]))
        sheets.append(flat.reshape(rows, 128))
        spans.append((row, p.size, p.shape))
        row += rows
    return jnp.concatenate(sheets, axis=0), spans


def _from_sheet(sheet, spans):
    out = []
    for row, size, shape in spans:
        rows = -(-size // 1024) * 8
        out.append(sheet[row:row + rows].reshape(-1)[:size].reshape(shape))
    return out


def kernel(x, norm_ffn1, ffn1_w_gate, ffn1_w_up, ffn1_w_down, norm_mix, w_in, conv_w, pool_w, pool_scale, w_out, norm_ffn2, ffn2_w_gate, ffn2_w_up, ffn2_w_down, norm_final, loss_target, m_norm_ffn1, m_ffn1_w_gate, m_ffn1_w_up, m_ffn1_w_down, m_norm_mix, m_w_in, m_conv_w, m_pool_w, m_pool_scale, m_w_out, m_norm_ffn2, m_ffn2_w_gate, m_ffn2_w_up, m_ffn2_w_down, m_norm_final, v_norm_ffn1, v_ffn1_w_gate, v_ffn1_w_up, v_ffn1_w_down, v_norm_mix, v_w_in, v_conv_w, v_pool_w, v_pool_scale, v_w_out, v_norm_ffn2, v_ffn2_w_gate, v_ffn2_w_up, v_ffn2_w_down, v_norm_final):
    me = 4 * lax.axis_index("x") + 2 * lax.axis_index("y") + lax.axis_index("c")
    xs, tgt = x[0], loss_target[0]
    s_len, d = xs.shape
    f_shard = ffn1_w_down.shape[1]
    conv_shard = conv_w.shape[2]

    def low_t(wt):
        return wt[0].T.astype(LOW)

    def by_dev(gw):
        return gw.reshape(N_DEV, -1, d)

    conv_tile = jnp.zeros((8, 128), F32).at[0:conv_w.shape[1], 0:conv_shard].set(conv_w[0])
    pool_w_low = pool_w[0].astype(LOW)

    rows_a = -(-f_shard // 64) * 32

    def parts_of(w_gate, w_up, w_down):
        shards = [low_t(w_gate), low_t(w_up), w_down[0].astype(LOW)]
        return [s[:rows_a] for s in shards], [s[rows_a:] for s in shards]

    def rows_flat(stacks):
        return [st.reshape(-1, d) for st in stacks]

    def gather(shards):
        return _Exchange(shards, [False] * len(shards), relay_at=0.6)

    def scatter(pairs):
        return _Exchange(pairs, [True] * len(pairs), among_chips=True)

    w1a_shards, w1b_shards = parts_of(ffn1_w_gate, ffn1_w_up, ffn1_w_down)
    w2a_shards, w2b_shards = parts_of(ffn2_w_gate, ffn2_w_up, ffn2_w_down)

    wg1a, wu1a, wd1a = rows_flat(_exchange(w1a_shards, [False] * 3, "gather_ffn1_a", relay=True))
    xa, h1, a1a, b1a, *w1b = _ffn_fwd(xs, norm_ffn1, None, wg1a, wu1a, wd1a, "ffn1_fwd_a",
                                      exchange=gather(w1b_shards))
    wg1b, wu1b, wd1b = rows_flat(w1b)
    x1, a1b, b1b, w_in_full, w_out_full, conv_tiles = _ffn_fwd(
        xa, None, h1, wg1b, wu1b, wd1b, "ffn1_fwd_b",
        exchange=gather([w_in[0].astype(LOW), w_out[0].astype(LOW), conv_tile]))
    w_out_full = w_out_full.reshape(-1, d)
    conv_full = jnp.concatenate([conv_tiles[k, 0:conv_w.shape[1], 0:conv_shard] for k in range(N_DEV)], axis=1)
    x2, h2, proj, *w2a = _mix_fwd(x1, norm_mix, w_in_full, conv_full, pool_w_low, pool_scale, w_out_full, "mix_fwd",
                                  exchange=gather(w2a_shards))
    wg2a, wu2a, wd2a = rows_flat(w2a)
    xb, h3, a2a, b2a, *w2b = _ffn_fwd(x2, norm_ffn2, None, wg2a, wu2a, wd2a, "ffn2_fwd_a",
                                      exchange=gather(w2b_shards))
    wg2b, wu2b, wd2b = rows_flat(w2b)
    x3, a2b, b2b = _ffn_fwd(xb, None, h3, wg2b, wu2b, wd2b, "ffn2_fwd_b")
    loss_row, dx3, dg_final, df3 = _loss_head(x3, norm_final.reshape(1, d), tgt, "loss_head")

    da2a, db2a, dwd_a, dwg_a, dwu_a = _ffn_bwdw(df3, a2a, b2a, h3, wd2a, "ffn2_bwdw_a")
    da2b, db2b, dwd_b, dwg_b, dwu_b = _ffn_bwdw(df3, a2b, b2b, h3, wd2b, "ffn2_bwdw_b")
    pairs = _pair_sum([by_dev(dwg_a), by_dev(dwu_a), by_dev(dwd_a), by_dev(dwg_b), by_dev(dwu_b), by_dev(dwd_b)],
                      "pair_sum_ffn2")
    dx2, dg_ffn2, *got_2a = _ffn_dx(dx3, x2, norm_ffn2, [(da2a, db2a, wg2a, wu2a), (da2b, db2b, wg2b, wu2b)],
                                    "ffn2_dx", exchange=scatter(pairs[:3]))
    dx1, dproj, cat, dg_mix, dconv, dpool_w, dpool_scale, df1, *got_2b = _mix_bwd(
        dx2, x1, proj, norm_mix, w_in_full, conv_full, pool_w_low, pool_scale, w_out_full, "mix_bwd",
        exchange=scatter(pairs[3:]))
    small_parts = [dg_mix, dg_ffn2, dg_final, dconv, dpool_w, dpool_scale, loss_row]
    small_sheet, spans = _to_sheet(small_parts)
    dw_in, got_small = _wgrad_tn(h2, dproj, W_IN_SHARD, True, "w_in_wgrad",
                                 exchange=_Exchange([small_sheet], [False], relay_at=0.5))
    (dw_out,) = _wgrad_tn(cat, dx2, d, False, "w_out_wgrad")
    pairs = _pair_sum([dw_in, by_dev(dw_out)], "pair_sum_mix")
    da1a, db1a, dwd, dwg, dwu, got_in, got_out = _ffn_bwdw(
        df1, a1a, b1a, h1, wd1a, "ffn1_bwdw_a", exchange=scatter(pairs))
    pairs = _pair_sum([by_dev(dwg), by_dev(dwu), by_dev(dwd)], "pair_sum_ffn1_a")
    da1b, db1b, dwd, dwg, dwu, *got_1a = _ffn_bwdw(df1, a1b, b1b, h1, wd1b, "ffn1_bwdw_b", exchange=scatter(pairs))
    pairs = _pair_sum([by_dev(dwg), by_dev(dwu), by_dev(dwd)], "pair_sum_ffn1_b")
    dx0, dg_ffn1, *got_1b = _ffn_dx(dx1, xs, norm_ffn1, [(da1a, db1a, wg1a, wu1a), (da1b, db1b, wg1b, wu1b)],
                                    "ffn1_dx", exchange=scatter(pairs))
    (got_n1,) = _exchange([dg_ffn1.reshape(8, 128)], [False], "gather_dnorm_ffn1")

    outs = {}

    def update(name, stacks, w, m, v):
        outs[name] = _adamw(stacks, w[0], m[0], v[0], "adamw_" + name)

    def update_t(name, stacks, w, m, v):
        res = _adamw(stacks, w[0].T, m[0].T, v[0].T, "adamw_" + name)
        outs[name] = tuple(r.T for r in res)

    update_t("ffn1_w_gate", [got_1a[0], got_1b[0]], ffn1_w_gate, m_ffn1_w_gate, v_ffn1_w_gate)
    update_t("ffn1_w_up", [got_1a[1], got_1b[1]], ffn1_w_up, m_ffn1_w_up, v_ffn1_w_up)
    update("ffn1_w_down", [got_1a[2], got_1b[2]], ffn1_w_down, m_ffn1_w_down, v_ffn1_w_down)
    update("w_in", got_in, w_in, m_w_in, v_w_in)
    update("w_out", got_out, w_out, m_w_out, v_w_out)
    update_t("ffn2_w_gate", [got_2a[0], got_2b[0]], ffn2_w_gate, m_ffn2_w_gate, v_ffn2_w_gate)
    update_t("ffn2_w_up", [got_2a[1], got_2b[1]], ffn2_w_up, m_ffn2_w_up, v_ffn2_w_up)
    update("ffn2_w_down", [got_2a[2], got_2b[2]], ffn2_w_down, m_ffn2_w_down, v_ffn2_w_down)

    g_small = _from_sheet(_sum_stack(got_small, "sum_small"), spans)
    g_norm_ffn1 = _sum_stack(got_n1, "sum_dnorm_ffn1").reshape(norm_ffn1.shape)
    g_conv = lax.dynamic_slice_in_dim(g_small[3], me * conv_shard, conv_shard, axis=1)
    small_names = ["norm_ffn1", "norm_mix", "norm_ffn2", "norm_final", "conv_w", "pool_w", "pool_scale"]
    small_g = [g_norm_ffn1, g_small[0], g_small[1], g_small[2].reshape(norm_final.shape), g_conv[None],
               g_small[4][None], g_small[5]]
    small_w = [norm_ffn1, norm_mix, norm_ffn2, norm_final, conv_w, pool_w, pool_scale]
    small_m = [m_norm_ffn1, m_norm_mix, m_norm_ffn2, m_norm_final, m_conv_w, m_pool_w, m_pool_scale]
    small_v = [v_norm_ffn1, v_norm_mix, v_norm_ffn2, v_norm_final, v_conv_w, v_pool_w, v_pool_scale]
    g_sheet, spans_u = _to_sheet(small_g)
    w_sheet, _ = _to_sheet(small_w)
    m_sheet, _ = _to_sheet(small_m)
    v_sheet, _ = _to_sheet(small_v)
    upd = _adamw(g_sheet[None], w_sheet, m_sheet, v_sheet, "adamw_small")
    small_out = [_from_sheet(u, spans_u) for u in upd]
    for k, nm in enumerate(small_names):
        outs[nm] = tuple(small_out[j][k] for j in range(4))

    loss = g_small[6][0, 0]
    order = ["norm_ffn1", "ffn1_w_gate", "ffn1_w_up", "ffn1_w_down", "norm_mix", "w_in", "conv_w", "pool_w",
             "pool_scale", "w_out", "norm_ffn2", "ffn2_w_gate", "ffn2_w_up", "ffn2_w_down", "norm_final"]
    big = {"ffn1_w_gate", "ffn1_w_up", "ffn1_w_down", "w_in", "w_out", "ffn2_w_gate", "ffn2_w_up", "ffn2_w_down"}

    def leaf(nm, j):
        val = outs[nm][j]
        return val[None] if nm in big else val

    return (loss, dx0[None],
            *[leaf(nm, 0) for nm in order], *[leaf(nm, 1) for nm in order],
            *[leaf(nm, 2) for nm in order], *[leaf(nm, 3) for nm in order])
```

```python
import jax
import jax.numpy as jnp
from jax import lax
from jax.experimental import pallas as pl
from jax.experimental.pallas import tpu as pltpu

F32 = jnp.float32
LOW = jnp.bfloat16

N_DEV = 8
EPS = 1e-6
D_CONV = 512
POOL_WINDOWS = (2, 4, 8, 16)
POOL_GC = 128
HALO = 16
W_IN_SHARD = 256

ADAM_LR = 0.001
ADAM_B1 = 0.9
ADAM_B2 = 0.999
ADAM_EPS = 1e-08
ADAM_WD = 0.01
ADAM_STEP = 10

VMEM_LIMIT_BYTES = 56 * 1024 * 1024
TOK_TILE = 512
MIX_TOK_TILE = 256
WGRAD_TOK_TILE = 4096
WGRAD_ROW_CANDIDATES = (256, 128)
RELAY_LAST_LATER = 0.25
BWD_ROW_SLAB = 2048


def _params(*sem):
    return pltpu.CompilerParams(dimension_semantics=sem, vmem_limit_bytes=VMEM_LIMIT_BYTES)


def _resident(arr):
    return pl.BlockSpec(arr.shape, lambda *_: (0,) * arr.ndim, pipeline_mode=pl.Buffered(1))


def _pick(n, candidates):
    for c in candidates:
        if n % c == 0:
            return c
    raise ValueError(f"no tile in {candidates} divides {n}")


def _dot(a, b):
    return lax.dot_general(a, b, (((1,), (0,)), ((), ())), preferred_element_type=F32)


def _dot_nt(a, b):
    return lax.dot_general(a, b, (((1,), (1,)), ((), ())), preferred_element_type=F32)


def _dot_tn(a, b):
    return lax.dot_general(a, b, (((0,), (0,)), ((), ())), preferred_element_type=F32)


def _rms_scale(x):
    return lax.rsqrt(jnp.mean(x * x, axis=-1, keepdims=True) + EPS)


def _rms_bwd(dy, x, g):
    r = _rms_scale(x)
    xhat = x * r
    gdy = dy * g
    dx = r * (gdy - xhat * jnp.mean(gdy * xhat, axis=-1, keepdims=True))
    return dx, jnp.sum(dy * xhat, axis=0, keepdims=True)


COLLECTIVE_PAIR, COLLECTIVE_CHIPS, COLLECTIVE_RELAY = 0, 1, 2


def _handshake(peer_numbers):
    mx, my, mc = lax.axis_index("x"), lax.axis_index("y"), lax.axis_index("c")
    barrier = pltpu.get_barrier_semaphore()
    for m in peer_numbers:
        peer = (lax.rem(mx + ((m >> 2) & 1), 2), lax.rem(my + ((m >> 1) & 1), 2), lax.rem(mc + (m & 1), 2))
        pl.semaphore_signal(barrier, inc=1, device_id=peer, device_id_type=pl.DeviceIdType.MESH)
    pl.semaphore_wait(barrier, len(peer_numbers))


class _Exchange:
    CHIPS = (2, 4, 6)

    def __init__(self, arrays, sliced, relay_at=None, among_chips=False):
        assert relay_at is None or not any(sliced)
        assert not among_chips or (all(sliced) and relay_at is None)
        self.relay_at, self.among_chips = relay_at, among_chips
        self.peers = self.CHIPS if among_chips else ((1, 2, 4) if relay_at is not None else tuple(range(1, N_DEV)))
        self.collective_id = COLLECTIVE_CHIPS if among_chips else (COLLECTIVE_RELAY if relay_at is not None else None)
        self.arrays, self.sliced, self.n = list(arrays), list(sliced), len(arrays)
        self.out_shape = [jax.ShapeDtypeStruct(arr.shape if sl else (N_DEV,) + arr.shape, arr.dtype)
                          for arr, sl in zip(arrays, sliced)]
        self.specs = [pl.BlockSpec(memory_space=pl.ANY)] * self.n
        self.scratch_shapes = [pltpu.SemaphoreType.DMA((self.n, N_DEV)),
                               pltpu.SemaphoreType.DMA((self.n, N_DEV)),
                               pltpu.SemaphoreType.DMA((self.n,))]

    HALF_VIA = ((4, 2, 5), (2, 4, 7))

    def _halves(self, a):
        rows = self.arrays[a].shape[0]
        if rows % 32:
            return ((0, rows), None)
        return ((0, rows // 2), (rows // 2, rows // 2))

    def _copies(self, ins, outs, sems):
        send_sems, recv_sems, local_sems = sems
        sliced = self.sliced
        mx, my, mc = lax.axis_index("x"), lax.axis_index("y"), lax.axis_index("c")
        me = 2 * mx + my if self.among_chips else 4 * mx + 2 * my + mc

        def peer(m):
            px = lax.rem(mx + ((m >> 2) & 1), 2)
            py = lax.rem(my + ((m >> 1) & 1), 2)
            pc = lax.rem(mc + (m & 1), 2)
            return (px, py, pc), (2 * px + py if self.among_chips else 4 * px + 2 * py + pc)

        def remote(a, m, arriving):
            pid, pflat = peer(m)
            return pltpu.make_async_remote_copy(
                src_ref=ins[a].at[pflat] if sliced[a] else ins[a],
                dst_ref=outs[a].at[pflat if arriving else me],
                send_sem=send_sems.at[a, m - 1],
                recv_sem=recv_sems.at[a, m - 1],
                device_id=pid,
                device_id_type=pl.DeviceIdType.MESH,
            )

        def local(a):
            return pltpu.make_async_copy(ins[a].at[me] if sliced[a] else ins[a], outs[a].at[me], local_sems.at[a])

        def passed_on(a, m):
            _, origin = peer(m)
            sibling, _ = peer(1)
            return pltpu.make_async_remote_copy(
                src_ref=outs[a].at[origin],
                dst_ref=outs[a].at[origin],
                send_sem=send_sems.at[a, m],
                recv_sem=recv_sems.at[a, m],
                device_id=sibling,
                device_id_type=pl.DeviceIdType.MESH,
            )

        def half_on(a, h, arriving):
            via, to, column = self.HALF_VIA[h]
            r0, nr = self._halves(a)[h]
            _, origin = peer(6 if arriving else via)
            rows = outs[a].at[origin].at[pl.ds(r0, nr)]
            return pltpu.make_async_remote_copy(
                src_ref=rows, dst_ref=rows, send_sem=send_sems.at[a, column], recv_sem=recv_sems.at[a, column],
                device_id=peer(to)[0], device_id_type=pl.DeviceIdType.MESH)

        return remote, local, passed_on, half_on

    def start(self, ins, outs, sems):
        remote, local, _, _ = self._copies(ins, outs, sems)
        if self.collective_id is not None:
            _handshake(self.peers)
        for a in range(self.n):
            local(a).start()
        for m in self.peers:
            for a in range(self.n):
                remote(a, m, False).start()

    def relay(self, ins, outs, sems):
        remote, _, passed_on, half_on = self._copies(ins, outs, sems)
        for h, (via, _, _) in enumerate(self.HALF_VIA):
            for a in range(self.n):
                remote(a, via, True).wait_recv()
                passed_on(a, via).start()
                if self._halves(a)[h] is not None:
                    half_on(a, h, False).start()

    def relay_last(self, ins, outs, sems):
        _, _, passed_on, half_on = self._copies(ins, outs, sems)
        for a in range(self.n):
            for h in range(2):
                if self._halves(a)[h] is not None:
                    half_on(a, h, True).wait_recv()
            passed_on(a, 6).start()

    def wait(self, ins, outs, sems):
        remote, local, passed_on, half_on = self._copies(ins, outs, sems)
        if self.relay_at is None:
            for m in self.peers:
                for a in range(self.n):
                    remote(a, m, True).wait_recv()
            for m in self.peers:
                for a in range(self.n):
                    remote(a, m, False).wait_send()
        else:
            for m in (1, 3, 5, 7):
                for a in range(self.n):
                    remote(a, m, True).wait_recv()
            for m in self.peers:
                for a in range(self.n):
                    remote(a, m, False).wait_send()
            for a in range(self.n):
                for m in self.CHIPS:
                    passed_on(a, m).wait_send()
                for h in range(2):
                    if self._halves(a)[h] is not None:
                        half_on(a, h, False).wait_send()
        for a in range(self.n):
            local(a).wait()


def _pair_sum(stacks, name):
    n = len(stacks)
    n_chip = N_DEV // 2
    half = [(n_chip,) + st.shape[1:] for st in stacks]

    def body(*refs):
        ins, outs, mine, theirs = refs[:n], refs[n:2 * n], refs[2 * n:3 * n], refs[3 * n:4 * n]
        local_sems, send_sems, recv_sems = refs[4 * n:]
        mx, my, mc = lax.axis_index("x"), lax.axis_index("y"), lax.axis_index("c")

        def own(a, k):
            return pltpu.make_async_copy(ins[a].at[2 * k + mc], mine[a].at[k], local_sems.at[a, k])

        def swap(a, k):
            return pltpu.make_async_remote_copy(
                src_ref=ins[a].at[2 * k + (1 - mc)], dst_ref=theirs[a].at[k],
                send_sem=send_sems.at[a, k], recv_sem=recv_sems.at[a, k],
                device_id=(mx, my, 1 - mc), device_id_type=pl.DeviceIdType.MESH)

        _handshake((1,))
        for k in range(n_chip):
            for a in range(n):
                own(a, k).start()
                swap(a, k).start()
        for k in range(n_chip):
            for a in range(n):
                own(a, k).wait()
                swap(a, k).wait()
                outs[a][k] = (mine[a][k].astype(F32) + theirs[a][k].astype(F32)).astype(LOW)

    return pl.pallas_call(
        body, name=name,
        out_shape=[jax.ShapeDtypeStruct(h, LOW) for h in half],
        in_specs=[pl.BlockSpec(memory_space=pl.ANY)] * n,
        out_specs=[pl.BlockSpec(memory_space=pltpu.VMEM)] * n,
        scratch_shapes=([pltpu.VMEM(h, st.dtype) for h, st in zip(half, stacks)] * 2
                        + [pltpu.SemaphoreType.DMA((n, n_chip))] * 3),
        compiler_params=pltpu.CompilerParams(vmem_limit_bytes=VMEM_LIMIT_BYTES, collective_id=COLLECTIVE_PAIR),
    )(*stacks)


def _exchange(arrays, sliced, name, relay=False):
    ex = _Exchange(arrays, sliced, relay_at=0 if relay else None)

    def body(*refs):
        ins, outs, sems = refs[:ex.n], refs[ex.n:2 * ex.n], refs[2 * ex.n:]
        ex.start(ins, outs, sems)
        if relay:
            ex.relay(ins, outs, sems)
            ex.relay_last(ins, outs, sems)
        ex.wait(ins, outs, sems)

    return pl.pallas_call(body, name=name, out_shape=ex.out_shape, in_specs=ex.specs, out_specs=ex.specs,
                          scratch_shapes=ex.scratch_shapes,
                          compiler_params=pltpu.CompilerParams(collective_id=ex.collective_id))(*arrays)


def _call(body, *, name, grid, in_specs, out_specs, out_shape, args, scratch_shapes=(), exchange=None):
    params = _params(*(("arbitrary",) * len(grid)))
    if exchange is None:
        return pl.pallas_call(body, name=name, grid=grid, in_specs=in_specs, out_specs=out_specs, out_shape=out_shape,
                              scratch_shapes=list(scratch_shapes), compiler_params=params)(*args)
    exs = list(exchange) if isinstance(exchange, (list, tuple)) else [exchange]
    assert len(exs) == 1 or all(ex.collective_id is None for ex in exs)
    params = pltpu.CompilerParams(dimension_semantics=("arbitrary",) * len(grid), vmem_limit_bytes=VMEM_LIMIT_BYTES,
                                  collective_id=exs[0].collective_id)
    n_in, n_out, n_scr = len(in_specs), len(out_specs), len(scratch_shapes)
    n_ex = sum(ex.n for ex in exs)
    n_steps = 1
    for g in grid:
        n_steps *= g

    def hosted(*refs):
        ins, refs = refs[:n_in], refs[n_in:]
        ex_ins, refs = refs[:n_ex], refs[n_ex:]
        outs, refs = refs[:n_out], refs[n_out:]
        ex_outs, refs = refs[:n_ex], refs[n_ex:]
        scr, sems = refs[:n_scr], refs[n_scr:]
        parts, at = [], 0
        for j, ex in enumerate(exs):
            parts.append((ex_ins[at:at + ex.n], ex_outs[at:at + ex.n], sems[3 * j:3 * j + 3]))
            at += ex.n
        step = pl.program_id(0)
        for ax in range(1, len(grid)):
            step = step * grid[ax] + pl.program_id(ax)

        @pl.when(step == 0)
        def _():
            for ex, part in zip(exs, parts):
                ex.start(*part)

        body(*ins, *outs, *scr)

        for ex, part in zip(exs, parts):
            if ex.relay_at is not None:
                @pl.when(step == min(int(ex.relay_at * n_steps), n_steps - 1))
                def _():
                    ex.relay(*part)

                @pl.when(step == min(int((ex.relay_at + RELAY_LAST_LATER) * n_steps), n_steps - 1))
                def _():
                    ex.relay_last(*part)

        @pl.when(step == n_steps - 1)
        def _():
            for ex, part in zip(exs, parts):
                ex.wait(*part)

    return pl.pallas_call(
        hosted, name=name, grid=grid,
        in_specs=list(in_specs) + [sp for ex in exs for sp in ex.specs],
        out_specs=list(out_specs) + [sp for ex in exs for sp in ex.specs],
        out_shape=list(out_shape) + [sh for ex in exs for sh in ex.out_shape],
        scratch_shapes=list(scratch_shapes) + [sc for ex in exs for sc in ex.scratch_shapes],
        compiler_params=params)(*args, *[arr for ex in exs for arr in ex.arrays])


def _ffn_fwd(x, g, h, wgt, wut, wd, name, exchange=None):
    s_len, d = x.shape
    fc = wd.shape[0]
    ts = min(TOK_TILE, s_len)
    first = h is None

    def body(*refs):
        x_ref, gh_ref, wg_ref, wu_ref, wd_ref, xo_ref = refs[:6]
        a_ref, b_ref, s_ref = refs[-3:]
        xv = x_ref[...]
        if first:
            hb = (xv * _rms_scale(xv) * gh_ref[...]).astype(LOW)
            refs[6][...] = hb
        else:
            hb = gh_ref[...]
        for c0, c1 in _slabs(fc):
            a = _dot_nt(hb, wg_ref[c0:c1, :])
            b = _dot_nt(hb, wu_ref[c0:c1, :])
            s_ref[:, c0:c1] = (a * jax.nn.sigmoid(a) * b).astype(LOW)
            a_ref[:, c0:c1] = a.astype(LOW)
            b_ref[:, c0:c1] = b.astype(LOW)
        xo_ref[...] = xv + 0.5 * _dot(s_ref[...], wd_ref[...])

    tok = pl.BlockSpec((ts, d), lambda t: (t, 0))
    hid = pl.BlockSpec((ts, fc), lambda t: (t, 0))
    tok_out = jax.ShapeDtypeStruct((s_len, d), F32)
    h_out = jax.ShapeDtypeStruct((s_len, d), LOW)
    hid_out = jax.ShapeDtypeStruct((s_len, fc), LOW)
    return _call(
        body,
        name=name,
        grid=(s_len // ts,),
        in_specs=[tok, pl.BlockSpec((1, d), lambda t: (0, 0)) if first else tok,
                  _resident(wgt), _resident(wut), _resident(wd)],
        out_specs=[tok] + ([tok] if first else []) + [hid, hid],
        out_shape=[tok_out] + ([h_out] if first else []) + [hid_out, hid_out],
        scratch_shapes=[pltpu.VMEM((ts, fc), LOW)],
        args=(x, g if first else h, wgt, wut, wd),
        exchange=exchange,
    )


def _slabs(width, slab=256):
    return [(c0, min(c0 + slab, width)) for c0 in range(0, width, slab)]


def _ffn_bwdw(df, a, b, h, wd, name, exchange=None):
    s_len, d = df.shape
    f_len = wd.shape[0]
    tm = _pick(f_len, WGRAD_ROW_CANDIDATES)
    tk = min(WGRAD_TOK_TILE, s_len)
    n_k = s_len // tk

    def body(df_ref, a_ref, b_ref, h_ref, wd_ref, da_ref, db_ref, dwd_ref, dwg_ref, dwu_ref,
             s_ref, acc_d, acc_g, acc_u):
        k = pl.program_id(1)

        @pl.when(k == 0)
        def _():
            acc_d[...] = jnp.zeros_like(acc_d)
            acc_g[...] = jnp.zeros_like(acc_g)
            acc_u[...] = jnp.zeros_like(acc_u)

        wdv = wd_ref[...]
        for r0, r1 in _slabs(tk, BWD_ROW_SLAB):
            ds = _dot_nt(df_ref[r0:r1, :], wdv)
            av = a_ref[r0:r1, :].astype(F32)
            bv = b_ref[r0:r1, :].astype(F32)
            sig = jax.nn.sigmoid(av)
            silu = av * sig
            s_ref[r0:r1, :] = (silu * bv).astype(LOW)
            da_ref[r0:r1, :] = (ds * bv * (sig * (1.0 + av * (1.0 - sig)))).astype(LOW)
            db_ref[r0:r1, :] = (ds * silu).astype(LOW)
            hv = h_ref[r0:r1, :]
            acc_d[...] += _dot_tn(s_ref[r0:r1, :], df_ref[r0:r1, :])
            acc_g[...] += _dot_tn(da_ref[r0:r1, :], hv)
            acc_u[...] += _dot_tn(db_ref[r0:r1, :], hv)

        @pl.when(k == n_k - 1)
        def _():
            dwd_ref[...] = acc_d[...].astype(LOW)
            dwg_ref[...] = acc_g[...].astype(LOW)
            dwu_ref[...] = acc_u[...].astype(LOW)

    hid = pl.BlockSpec((tk, tm), lambda i, k: (k, i))
    tok = pl.BlockSpec((tk, d), lambda i, k: (k, 0))
    wrow = pl.BlockSpec((tm, d), lambda i, k: (i, 0))
    return _call(
        body,
        name=name,
        grid=(f_len // tm, n_k),
        in_specs=[tok, hid, hid, tok, wrow],
        out_specs=[hid, hid, wrow, wrow, wrow],
        out_shape=[jax.ShapeDtypeStruct((s_len, f_len), LOW)] * 2 + [jax.ShapeDtypeStruct((f_len, d), LOW)] * 3,
        scratch_shapes=[pltpu.VMEM((tk, tm), LOW)] + [pltpu.VMEM((tm, d), F32)] * 3,
        args=(df, a, b, h, wd),
        exchange=exchange,
    )


def _ffn_dx(dxo, x, g, parts, name, exchange=None):
    s_len, d = x.shape
    ts = min(TOK_TILE, s_len)
    n_p = len(parts)

    def body(dxo_ref, x_ref, g_ref, *refs):
        dxi_ref, dg_ref = refs[4 * n_p:]

        @pl.when(pl.program_id(0) == 0)
        def _():
            dg_ref[...] = jnp.zeros_like(dg_ref)

        dh = None
        for p in range(n_p):
            da_ref, db_ref, wg_ref, wu_ref = refs[4 * p:4 * p + 4]
            part = _dot(da_ref[...], wg_ref[...]) + _dot(db_ref[...], wu_ref[...])
            dh = part if dh is None else dh + part
        dx, dgp = _rms_bwd(dh, x_ref[...], g_ref[...])
        dxi_ref[...] = dxo_ref[...] + dx
        dg_ref[...] += dgp

    tok = pl.BlockSpec((ts, d), lambda t: (t, 0))
    vec = pl.BlockSpec((1, d), lambda t: (0, 0))
    part_specs, part_args = [], []
    for da, db, wgt, wut in parts:
        hid = pl.BlockSpec((ts, da.shape[1]), lambda t: (t, 0))
        part_specs += [hid, hid, _resident(wgt), _resident(wut)]
        part_args += [da, db, wgt, wut]
    return _call(
        body,
        name=name,
        grid=(s_len // ts,),
        in_specs=[tok, tok, vec] + part_specs,
        out_specs=[tok, vec],
        out_shape=[jax.ShapeDtypeStruct((s_len, d), F32), jax.ShapeDtypeStruct((1, d), F32)],
        args=(dxo, x, g, *part_args),
        exchange=exchange,
    )


def _wgrad_tn(xm, ym, tn, stacked, name, exchange=None):
    s_len, m = xm.shape
    n = ym.shape[1]
    tk = min(WGRAD_TOK_TILE, s_len)
    n_k = s_len // tk

    def body(x_ref, y_ref, o_ref, acc):
        k = pl.program_id(1)

        @pl.when(k == 0)
        def _():
            acc[...] = jnp.zeros_like(acc)

        acc[...] += _dot_tn(x_ref[...].astype(LOW), y_ref[...].astype(LOW))

        @pl.when(k == n_k - 1)
        def _():
            o_ref[...] = acc[...].astype(LOW)

    if stacked:
        out_spec = pl.BlockSpec((None, m, tn), lambda j, k: (j, 0, 0))
        out_shape = jax.ShapeDtypeStruct((n // tn, m, tn), LOW)
    else:
        out_spec = pl.BlockSpec((m, tn), lambda j, k: (0, j))
        out_shape = jax.ShapeDtypeStruct((m, n), LOW)
    return _call(
        body,
        name=name,
        grid=(n // tn, n_k),
        in_specs=[pl.BlockSpec((tk, m), lambda j, k: (k, 0)), pl.BlockSpec((tk, tn), lambda j, k: (k, j))],
        out_specs=[out_spec],
        out_shape=[out_shape],
        scratch_shapes=[pltpu.VMEM((m, tn), F32)],
        args=(xm, ym),
        exchange=exchange,
    )


def _mix_parts(ext_ref, cw, ts, row0):
    dc = D_CONV

    def back(off, c0, c1):
        return ext_ref[HALO - off:HALO - off + ts, c0:c1]

    v, gate_b, gate_c = back(0, 0, dc), back(0, dc, 2 * dc), back(0, 2 * dc, 3 * dc)
    z0 = gate_c * v
    z1 = back(1, 2 * dc, 3 * dc) * back(1, 0, dc)
    z2 = back(2, 2 * dc, 3 * dc) * back(2, 0, dc)
    conv = cw[2:3, :] * z0 + cw[1:2, :] * z1 + cw[0:1, :] * z2
    rows = row0 + lax.broadcasted_iota(jnp.int32, (ts, 1), 0)
    pooled, inv_count = [], []
    for grp, w in enumerate(POOL_WINDOWS):
        c0 = 3 * dc + POOL_GC * grp
        u = back(0, c0, c0 + POOL_GC)
        acc = u
        for j in range(1, w):
            acc = acc + back(j, c0, c0 + POOL_GC)
        inv = 1.0 / jnp.minimum(rows + 1, w).astype(F32)
        pooled.append(acc * inv - u)
        inv_count.append(inv)
    return v, gate_b, gate_c, z0, z1, z2, conv, pooled, inv_count


def _mix_fwd(x, g, w_in, conv_w, pool_w, pool_scale, w_out, name, exchange=None):
    s_len, d = x.shape
    n_blk, _, wcols = w_in.shape
    p_len = n_blk * wcols
    d_mix = w_out.shape[0]
    ts = min(MIX_TOK_TILE, s_len)
    dc = D_CONV

    def body(x_ref, g_ref, win_ref, cw_ref, pw_ref, ps_ref, wout_ref, x2_ref, h_ref, proj_ref, ext_ref, cat_ref):
        t = pl.program_id(0)

        @pl.when(t == 0)
        def _():
            ext_ref[0:HALO, :] = jnp.zeros((HALO, p_len), F32)

        xv = x_ref[...]
        hb = (xv * _rms_scale(xv) * g_ref[...]).astype(LOW)
        h_ref[...] = hb
        for k in range(n_blk):
            ext_ref[HALO:HALO + ts, wcols * k:wcols * (k + 1)] = _dot(hb, win_ref[k])
        proj_ref[...] = ext_ref[HALO:HALO + ts, :]

        _, gate_b, _, _, _, _, conv, pooled, _ = _mix_parts(ext_ref, cw_ref[...], ts, t * ts)
        cat_ref[:, 0:dc] = (gate_b * conv).astype(LOW)
        for grp in range(len(POOL_WINDOWS)):
            c0 = POOL_GC * grp
            lin = _dot(pooled[grp].astype(LOW), pw_ref[grp])
            cat_ref[:, dc + c0:dc + c0 + POOL_GC] = (lin * ps_ref[:, c0:c0 + POOL_GC]).astype(LOW)
        x2_ref[...] = xv + _dot(cat_ref[...], wout_ref[...])
        ext_ref[0:HALO, :] = ext_ref[ts:ts + HALO, :]

    tok = pl.BlockSpec((ts, d), lambda t: (t, 0))

    def whole(arr):
        return pl.BlockSpec(arr.shape, lambda t: (0,) * arr.ndim)

    return _call(
        body,
        name=name,
        grid=(s_len // ts,),
        in_specs=[tok, whole(g), _resident(w_in), whole(conv_w), whole(pool_w), whole(pool_scale), _resident(w_out)],
        out_specs=[tok, tok, pl.BlockSpec((ts, p_len), lambda t: (t, 0))],
        out_shape=[
            jax.ShapeDtypeStruct((s_len, d), F32),
            jax.ShapeDtypeStruct((s_len, d), LOW),
            jax.ShapeDtypeStruct((s_len, p_len), F32),
        ],
        scratch_shapes=[pltpu.VMEM((ts + HALO, p_len), F32), pltpu.VMEM((ts, d_mix), LOW)],
        args=(x, g, w_in, conv_w, pool_w, pool_scale, w_out),
        exchange=exchange,
    )


def _mix_bwd(dx2, x, proj, g, w_in, conv_w, pool_w, pool_scale, w_out, name, exchange=None):
    s_len, d = x.shape
    n_blk, _, wcols = w_in.shape
    p_len = n_blk * wcols
    d_mix = w_out.shape[0]
    ts = min(MIX_TOK_TILE, s_len)
    n_t = s_len // ts
    dc = D_CONV
    n_grp = len(POOL_WINDOWS)

    def body(dx2_ref, x_ref, proj_ref, halo_ref, g_ref, win_ref, cw_ref, pw_ref, ps_ref, wout_ref,
             dx_ref, dproj_ref, cat_ref, dg_ref, dcw_ref, dpw_ref, dps_ref, df_ref, ext_ref, fut_ref):
        i = pl.program_id(0)
        t = n_t - 1 - i

        @pl.when(i == 0)
        def _():
            dg_ref[...] = jnp.zeros_like(dg_ref)
            dcw_ref[...] = jnp.zeros_like(dcw_ref)
            dpw_ref[...] = jnp.zeros_like(dpw_ref)
            dps_ref[...] = jnp.zeros_like(dps_ref)
            fut_ref[ts:ts + HALO, :] = jnp.zeros((HALO, d_mix), F32)

        ext_ref[HALO:HALO + ts, :] = proj_ref[...]

        @pl.when(t == 0)
        def _():
            ext_ref[0:HALO, :] = jnp.zeros((HALO, p_len), F32)

        @pl.when(t > 0)
        def _():
            ext_ref[0:HALO, :] = halo_ref[...]

        cw = cw_ref[...]
        v, gate_b, gate_c, z0, z1, z2, conv, pooled, inv_count = _mix_parts(ext_ref, cw, ts, t * ts)
        dx2 = dx2_ref[...]
        dcat = _dot_nt(dx2.astype(LOW), wout_ref[...])

        dy_a = dcat[:, 0:dc]
        dconv = dy_a * gate_b
        fut_ref[0:ts, 0:dc] = dconv
        cat_ref[:, 0:dc] = (gate_b * conv).astype(LOW)
        dproj_ref[:, dc:2 * dc] = (dy_a * conv).astype(LOW)
        dcw_ref[2:3, :] += jnp.sum(dconv * z0, axis=0, keepdims=True)
        dcw_ref[1:2, :] += jnp.sum(dconv * z1, axis=0, keepdims=True)
        dcw_ref[0:1, :] += jnp.sum(dconv * z2, axis=0, keepdims=True)

        dpool = []
        for grp in range(n_grp):
            c0 = POOL_GC * grp
            pooled_b = pooled[grp].astype(LOW)
            lin = _dot(pooled_b, pw_ref[grp])
            dy_b = dcat[:, dc + c0:dc + c0 + POOL_GC]
            scale = ps_ref[:, c0:c0 + POOL_GC]
            cat_ref[:, dc + c0:dc + c0 + POOL_GC] = (lin * scale).astype(LOW)
            dps_ref[:, c0:c0 + POOL_GC] += jnp.sum(dy_b * lin, axis=0, keepdims=True)
            dlin = (dy_b * scale).astype(LOW)
            dpw_ref[grp] += _dot_tn(pooled_b, dlin)
            dpool.append(_dot_nt(dlin, pw_ref[grp]))
            fut_ref[0:ts, dc + c0:dc + c0 + POOL_GC] = dpool[grp] * inv_count[grp]

        def ahead(off, c0, c1):
            return fut_ref[off:off + ts, c0:c1]

        dz = cw[2:3, :] * ahead(0, 0, dc) + cw[1:2, :] * ahead(1, 0, dc) + cw[0:1, :] * ahead(2, 0, dc)
        dproj_ref[:, 0:dc] = (dz * gate_c).astype(LOW)
        dproj_ref[:, 2 * dc:3 * dc] = (dz * v).astype(LOW)
        for grp, w in enumerate(POOL_WINDOWS):
            c0 = dc + POOL_GC * grp
            acc = ahead(0, c0, c0 + POOL_GC)
            for j in range(1, w):
                acc = acc + ahead(j, c0, c0 + POOL_GC)
            dproj_ref[:, 2 * dc + c0:2 * dc + c0 + POOL_GC] = (acc - dpool[grp]).astype(LOW)

        dh = _dot_nt(dproj_ref[:, 0:wcols], win_ref[0])
        for k in range(1, n_blk):
            dh += _dot_nt(dproj_ref[:, wcols * k:wcols * (k + 1)], win_ref[k])
        dx, dgp = _rms_bwd(dh, x_ref[...], g_ref[...])
        dx = dx2 + dx
        dx_ref[...] = dx
        df_ref[...] = (0.5 * dx).astype(LOW)
        dg_ref[...] += dgp
        fut_ref[ts:ts + HALO, :] = fut_ref[0:HALO, :]

    tok = pl.BlockSpec((ts, d), lambda i: (n_t - 1 - i, 0))
    halo = pl.BlockSpec((HALO, p_len), lambda i: (jnp.maximum((n_t - 1 - i) * (ts // HALO) - 1, 0), 0))

    def whole(arr):
        return pl.BlockSpec(arr.shape, lambda i: (0,) * arr.ndim)

    return _call(
        body,
        name=name,
        grid=(n_t,),
        in_specs=[tok, tok, pl.BlockSpec((ts, p_len), lambda i: (n_t - 1 - i, 0)), halo,
                  whole(g), _resident(w_in), whole(conv_w), whole(pool_w), whole(pool_scale), _resident(w_out)],
        out_specs=[tok, pl.BlockSpec((ts, p_len), lambda i: (n_t - 1 - i, 0)),
                   pl.BlockSpec((ts, d_mix), lambda i: (n_t - 1 - i, 0)),
                   whole(g), whole(conv_w), whole(pool_w), whole(pool_scale), tok],
        out_shape=[
            jax.ShapeDtypeStruct((s_len, d), F32),
            jax.ShapeDtypeStruct((s_len, p_len), LOW),
            jax.ShapeDtypeStruct((s_len, d_mix), LOW),
            jax.ShapeDtypeStruct(g.shape, F32),
            jax.ShapeDtypeStruct(conv_w.shape, F32),
            jax.ShapeDtypeStruct(pool_w.shape, F32),
            jax.ShapeDtypeStruct(pool_scale.shape, F32),
            jax.ShapeDtypeStruct((s_len, d), LOW),
        ],
        scratch_shapes=[pltpu.VMEM((ts + HALO, p_len), F32), pltpu.VMEM((ts + HALO, d_mix), F32)],
        args=(dx2, x, proj, proj, g, w_in, conv_w, pool_w, pool_scale, w_out),
        exchange=exchange,
    )


def _loss_head(x, g, target, name):
    s_len, d = x.shape
    ts = min(TOK_TILE, s_len)

    def body(x_ref, g_ref, tgt_ref, loss_ref, dx_ref, dg_ref, df_ref):
        @pl.when(pl.program_id(0) == 0)
        def _():
            loss_ref[...] = jnp.zeros_like(loss_ref)
            dg_ref[...] = jnp.zeros_like(dg_ref)

        xv, gv = x_ref[...], g_ref[...]
        err = xv * _rms_scale(xv) * gv - tgt_ref[...]
        loss_ref[...] += 0.5 * jnp.sum(jnp.mean(err * err, axis=-1, keepdims=True), axis=0, keepdims=True)
        dx, dgp = _rms_bwd(err * (1.0 / d), xv, gv)
        dx_ref[...] = dx
        df_ref[...] = (0.5 * dx).astype(LOW)
        dg_ref[...] += dgp

    tok = pl.BlockSpec((ts, d), lambda t: (t, 0))
    vec = pl.BlockSpec((1, d), lambda t: (0, 0))
    return pl.pallas_call(
        body,
        name=name,
        grid=(s_len // ts,),
        in_specs=[tok, vec, tok],
        out_specs=[pl.BlockSpec((1, 128), lambda t: (0, 0)), tok, vec, tok],
        out_shape=[
            jax.ShapeDtypeStruct((1, 128), F32),
            jax.ShapeDtypeStruct((s_len, d), F32),
            jax.ShapeDtypeStruct((1, d), F32),
            jax.ShapeDtypeStruct((s_len, d), LOW),
        ],
        compiler_params=_params("arbitrary"),
    )(x, g, target)


def _row_tile(rows, cols, stack_bytes):
    budget = 20 * 1024 * 1024
    per_row = cols * (4 * 7 + stack_bytes)
    for tr in (rows, 512, 256, 176, 128, 64, 32, 16, 8):
        if rows % tr == 0 and tr % 8 == 0 and tr * per_row * 2 <= budget:
            return tr
    return rows


def _sum_stack(stack, name):
    n, r, c = stack.shape
    tr = _row_tile(r, c, n * stack.dtype.itemsize)

    def body(s_ref, o_ref):
        acc = s_ref[0].astype(F32)
        for k in range(1, n):
            acc = acc + s_ref[k].astype(F32)
        o_ref[...] = acc

    return pl.pallas_call(
        body,
        name=name,
        grid=(r // tr,),
        in_specs=[pl.BlockSpec((n, tr, c), lambda i: (0, i, 0))],
        out_specs=pl.BlockSpec((tr, c), lambda i: (i, 0)),
        out_shape=jax.ShapeDtypeStruct((r, c), F32),
        compiler_params=_params("arbitrary"),
    )(stack)


def _adamw(stacks, w, m, v, name):
    stacks = list(stacks) if isinstance(stacks, (list, tuple)) else [stacks]
    r, c = w.shape
    n = stacks[0].shape[0]
    part_rows = [st.shape[1] for st in stacks]
    assert sum(part_rows) == r and all(st.shape[0] == n for st in stacks)
    first_row = [sum(part_rows[:j]) for j in range(len(stacks))]
    tc = next(t for t in (512, 256, 128) if c % t == 0)
    c1 = 1.0 - ADAM_B1 ** ADAM_STEP
    c2 = 1.0 - ADAM_B2 ** ADAM_STEP

    def body(*refs):
        s_refs = refs[:len(stacks)]
        w_ref, m_ref, v_ref, g_ref, d_ref, mo_ref, vo_ref = refs[len(stacks):]
        for s_ref, r0, nr in zip(s_refs, first_row, part_rows):
            gv = s_ref[0].astype(F32)
            for k in range(1, n):
                gv = gv + s_ref[k].astype(F32)
            mn = ADAM_B1 * m_ref[r0:r0 + nr, :] + (1.0 - ADAM_B1) * gv
            vn = ADAM_B2 * v_ref[r0:r0 + nr, :] + (1.0 - ADAM_B2) * (gv * gv)
            g_ref[r0:r0 + nr, :] = gv
            mo_ref[r0:r0 + nr, :] = mn
            vo_ref[r0:r0 + nr, :] = vn
            d_ref[r0:r0 + nr, :] = -ADAM_LR * ((mn / c1) / (jnp.sqrt(vn / c2) + ADAM_EPS)
                                               + ADAM_WD * w_ref[r0:r0 + nr, :])

    blk = pl.BlockSpec((r, tc), lambda i: (0, i))
    return pl.pallas_call(
        body,
        name=name,
        grid=(c // tc,),
        in_specs=[pl.BlockSpec((n, nr, tc), lambda i: (0, 0, i)) for nr in part_rows] + [blk, blk, blk],
        out_specs=[blk] * 4,
        out_shape=[jax.ShapeDtypeStruct((r, c), F32)] * 4,
        compiler_params=_params("arbitrary"),
    )(*stacks, w, m, v)


def _to_sheet(parts):
    sheets, spans = [], []
    row = 0
    for p in parts:
        flat = p.reshape(-1).astype(F32)
        rows = -(-flat.shape[0] // 1024) * 8
        flat = jnp.pad(flat, (0, rows * 128 - flat.shape[0]))
        sheets.append(flat.reshape(rows, 128))
        spans.append((row, p.size, p.shape))
        row += rows
    return jnp.concatenate(sheets, axis=0), spans


def _from_sheet(sheet, spans):
    out = []
    for row, size, shape in spans:
        rows = -(-size // 1024) * 8
        out.append(sheet[row:row + rows].reshape(-1)[:size].reshape(shape))
    return out


def kernel(x, norm_ffn1, ffn1_w_gate, ffn1_w_up, ffn1_w_down, norm_mix, w_in, conv_w, pool_w, pool_scale, w_out, norm_ffn2, ffn2_w_gate, ffn2_w_up, ffn2_w_down, norm_final, loss_target, m_norm_ffn1, m_ffn1_w_gate, m_ffn1_w_up, m_ffn1_w_down, m_norm_mix, m_w_in, m_conv_w, m_pool_w, m_pool_scale, m_w_out, m_norm_ffn2, m_ffn2_w_gate, m_ffn2_w_up, m_ffn2_w_down, m_norm_final, v_norm_ffn1, v_ffn1_w_gate, v_ffn1_w_up, v_ffn1_w_down, v_norm_mix, v_w_in, v_conv_w, v_pool_w, v_pool_scale, v_w_out, v_norm_ffn2, v_ffn2_w_gate, v_ffn2_w_up, v_ffn2_w_down, v_norm_final):
    me = 4 * lax.axis_index("x") + 2 * lax.axis_index("y") + lax.axis_index("c")
    xs, tgt = x[0], loss_target[0]
    s_len, d = xs.shape
    f_shard = ffn1_w_down.shape[1]
    conv_shard = conv_w.shape[2]

    def low_t(wt):
        return wt[0].T.astype(LOW)

    def by_dev(gw):
        return gw.reshape(N_DEV, -1, d)

    conv_tile = jnp.zeros((8, 128), F32).at[0:conv_w.shape[1], 0:conv_shard].set(conv_w[0])
    pool_w_low = pool_w[0].astype(LOW)

    rows_a = -(-f_shard // 64) * 32

    def parts_of(w_gate, w_up, w_down):
        shards = [low_t(w_gate), low_t(w_up), w_down[0].astype(LOW)]
        return [s[:rows_a] for s in shards], [s[rows_a:] for s in shards]

    def rows_flat(stacks):
        return [st.reshape(-1, d) for st in stacks]

    def gather(shards):
        return _Exchange(shards, [False] * len(shards), relay_at=0.6)

    def scatter(pairs):
        return _Exchange(pairs, [True] * len(pairs), among_chips=True)

    w1a_shards, w1b_shards = parts_of(ffn1_w_gate, ffn1_w_up, ffn1_w_down)
    w2a_shards, w2b_shards = parts_of(ffn2_w_gate, ffn2_w_up, ffn2_w_down)

    wg1a, wu1a, wd1a = rows_flat(_exchange(w1a_shards, [False] * 3, "gather_ffn1_a", relay=True))
    xa, h1, a1a, b1a, *w1b = _ffn_fwd(xs, norm_ffn1, None, wg1a, wu1a, wd1a, "ffn1_fwd_a",
                                      exchange=gather(w1b_shards))
    wg1b, wu1b, wd1b = rows_flat(w1b)
    x1, a1b, b1b, w_in_full, w_out_full, conv_tiles = _ffn_fwd(
        xa, None, h1, wg1b, wu1b, wd1b, "ffn1_fwd_b",
        exchange=gather([w_in[0].astype(LOW), w_out[0].astype(LOW), conv_tile]))
    w_out_full = w_out_full.reshape(-1, d)
    conv_full = jnp.concatenate([conv_tiles[k, 0:conv_w.shape[1], 0:conv_shard] for k in range(N_DEV)], axis=1)
    x2, h2, proj, *w2a = _mix_fwd(x1, norm_mix, w_in_full, conv_full, pool_w_low, pool_scale, w_out_full, "mix_fwd",
                                  exchange=gather(w2a_shards))
    wg2a, wu2a, wd2a = rows_flat(w2a)
    xb, h3, a2a, b2a, *w2b = _ffn_fwd(x2, norm_ffn2, None, wg2a, wu2a, wd2a, "ffn2_fwd_a",
                                      exchange=gather(w2b_shards))
    wg2b, wu2b, wd2b = rows_flat(w2b)
    x3, a2b, b2b = _ffn_fwd(xb, None, h3, wg2b, wu2b, wd2b, "ffn2_fwd_b")
    loss_row, dx3, dg_final, df3 = _loss_head(x3, norm_final.reshape(1, d), tgt, "loss_head")

    da2a, db2a, dwd_a, dwg_a, dwu_a = _ffn_bwdw(df3, a2a, b2a, h3, wd2a, "ffn2_bwdw_a")
    da2b, db2b, dwd_b, dwg_b, dwu_b = _ffn_bwdw(df3, a2b, b2b, h3, wd2b, "ffn2_bwdw_b")
    pairs = _pair_sum([by_dev(dwg_a), by_dev(dwu_a), by_dev(dwd_a), by_dev(dwg_b), by_dev(dwu_b), by_dev(dwd_b)],
                      "pair_sum_ffn2")
    dx2, dg_ffn2, *got_2a = _ffn_dx(dx3, x2, norm_ffn2, [(da2a, db2a, wg2a, wu2a), (da2b, db2b, wg2b, wu2b)],
                                    "ffn2_dx", exchange=scatter(pairs[:3]))
    dx1, dproj, cat, dg_mix, dconv, dpool_w, dpool_scale, df1, *got_2b = _mix_bwd(
        dx2, x1, proj, norm_mix, w_in_full, conv_full, pool_w_low, pool_scale, w_out_full, "mix_bwd",
        exchange=scatter(pairs[3:]))
    small_parts = [dg_mix, dg_ffn2, dg_final, dconv, dpool_w, dpool_scale, loss_row]
    small_sheet, spans = _to_sheet(small_parts)
    dw_in, got_small = _wgrad_tn(h2, dproj, W_IN_SHARD, True, "w_in_wgrad",
                                 exchange=_Exchange([small_sheet], [False], relay_at=0.5))
    (dw_out,) = _wgrad_tn(cat, dx2, d, False, "w_out_wgrad")
    pairs = _pair_sum([dw_in, by_dev(dw_out)], "pair_sum_mix")
    da1a, db1a, dwd, dwg, dwu, got_in, got_out = _ffn_bwdw(
        df1, a1a, b1a, h1, wd1a, "ffn1_bwdw_a", exchange=scatter(pairs))
    pairs = _pair_sum([by_dev(dwg), by_dev(dwu), by_dev(dwd)], "pair_sum_ffn1_a")
    da1b, db1b, dwd, dwg, dwu, *got_1a = _ffn_bwdw(df1, a1b, b1b, h1, wd1b, "ffn1_bwdw_b", exchange=scatter(pairs))
    pairs = _pair_sum([by_dev(dwg), by_dev(dwu), by_dev(dwd)], "pair_sum_ffn1_b")
    dx0, dg_ffn1, *got_1b = _ffn_dx(dx1, xs, norm_ffn1, [(da1a, db1a, wg1a, wu1a), (da1b, db1b, wg1b, wu1b)],
                                    "ffn1_dx", exchange=scatter(pairs))
    (got_n1,) = _exchange([dg_ffn1.reshape(8, 128)], [False], "gather_dnorm_ffn1")

    outs = {}

    def update(name, stacks, w, m, v):
        outs[name] = _adamw(stacks, w[0], m[0], v[0], "adamw_" + name)

    def update_t(name, stacks, w, m, v):
        res = _adamw(stacks, w[0].T, m[0].T, v[0].T, "adamw_" + name)
        outs[name] = tuple(r.T for r in res)

    update_t("ffn1_w_gate", [got_1a[0], got_1b[0]], ffn1_w_gate, m_ffn1_w_gate, v_ffn1_w_gate)
    update_t("ffn1_w_up", [got_1a[1], got_1b[1]], ffn1_w_up, m_ffn1_w_up, v_ffn1_w_up)
    update("ffn1_w_down", [got_1a[2], got_1b[2]], ffn1_w_down, m_ffn1_w_down, v_ffn1_w_down)
    update("w_in", got_in, w_in, m_w_in, v_w_in)
    update("w_out", got_out, w_out, m_w_out, v_w_out)
    update_t("ffn2_w_gate", [got_2a[0], got_2b[0]], ffn2_w_gate, m_ffn2_w_gate, v_ffn2_w_gate)
    update_t("ffn2_w_up", [got_2a[1], got_2b[1]], ffn2_w_up, m_ffn2_w_up, v_ffn2_w_up)
    update("ffn2_w_down", [got_2a[2], got_2b[2]], ffn2_w_down, m_ffn2_w_down, v_ffn2_w_down)

    g_small = _from_sheet(_sum_stack(got_small, "sum_small"), spans)
    g_norm_ffn1 = _sum_stack(got_n1, "sum_dnorm_ffn1").reshape(norm_ffn1.shape)
    g_conv = lax.dynamic_slice_in_dim(g_small[3], me * conv_shard, conv_shard, axis=1)
    small_names = ["norm_ffn1", "norm_mix", "norm_ffn2", "norm_final", "conv_w", "pool_w", "pool_scale"]
    small_g = [g_norm_ffn1, g_small[0], g_small[1], g_small[2].reshape(norm_final.shape), g_conv[None],
               g_small[4][None], g_small[5]]
    small_w = [norm_ffn1, norm_mix, norm_ffn2, norm_final, conv_w, pool_w, pool_scale]
    small_m = [m_norm_ffn1, m_norm_mix, m_norm_ffn2, m_norm_final, m_conv_w, m_pool_w, m_pool_scale]
    small_v = [v_norm_ffn1, v_norm_mix, v_norm_ffn2, v_norm_final, v_conv_w, v_pool_w, v_pool_scale]
    g_sheet, spans_u = _to_sheet(small_g)
    w_sheet, _ = _to_sheet(small_w)
    m_sheet, _ = _to_sheet(small_m)
    v_sheet, _ = _to_sheet(small_v)
    upd = _adamw(g_sheet[None], w_sheet, m_sheet, v_sheet, "adamw_small")
    small_out = [_from_sheet(u, spans_u) for u in upd]
    for k, nm in enumerate(small_names):
        outs[nm] = tuple(small_out[j][k] for j in range(4))

    loss = g_small[6][0, 0]
    order = ["norm_ffn1", "ffn1_w_gate", "ffn1_w_up", "ffn1_w_down", "norm_mix", "w_in", "conv_w", "pool_w",
             "pool_scale", "w_out", "norm_ffn2", "ffn2_w_gate", "ffn2_w_up", "ffn2_w_down", "norm_final"]
    big = {"ffn1_w_gate", "ffn1_w_up", "ffn1_w_down", "w_in", "w_out", "ffn2_w_gate", "ffn2_w_up", "ffn2_w_down"}

    def leaf(nm, j):
        val = outs[nm][j]
        return val[None] if nm in big else val

    return (loss, dx0[None],
            *[leaf(nm, 0) for nm in order], *[leaf(nm, 1) for nm in order],
            *[leaf(nm, 2) for nm in order], *[leaf(nm, 3) for nm in order])
```

```python
import jax
import jax.numpy as jnp
from jax import lax
from jax.experimental import pallas as pl
from jax.experimental.pallas import tpu as pltpu

F32 = jnp.float32
LOW = jnp.bfloat16

N_DEV = 8
EPS = 1e-6
D_CONV = 512
POOL_WINDOWS = (2, 4, 8, 16)
POOL_GC = 128
HALO = 16
W_IN_SHARD = 256

ADAM_LR = 0.001
ADAM_B1 = 0.9
ADAM_B2 = 0.999
ADAM_EPS = 1e-08
ADAM_WD = 0.01
ADAM_STEP = 10

VMEM_LIMIT_BYTES = 56 * 1024 * 1024
TOK_TILE = 512
MIX_TOK_TILE = 256
WGRAD_TOK_TILE = 4096
WGRAD_ROW_CANDIDATES = (256, 128)
RELAY_LAST_LATER = 0.25
BWD_ROW_SLAB = 2048


def _params(*sem):
    return pltpu.CompilerParams(dimension_semantics=sem, vmem_limit_bytes=VMEM_LIMIT_BYTES)


def _resident(arr):
    return pl.BlockSpec(arr.shape, lambda *_: (0,) * arr.ndim, pipeline_mode=pl.Buffered(1))


def _pick(n, candidates):
    for c in candidates:
        if n % c == 0:
            return c
    raise ValueError(f"no tile in {candidates} divides {n}")


def _dot(a, b):
    return lax.dot_general(a, b, (((1,), (0,)), ((), ())), preferred_element_type=F32)


def _dot_nt(a, b):
    return lax.dot_general(a, b, (((1,), (1,)), ((), ())), preferred_element_type=F32)


def _dot_tn(a, b):
    return lax.dot_general(a, b, (((0,), (0,)), ((), ())), preferred_element_type=F32)


def _rms_scale(x):
    return lax.rsqrt(jnp.mean(x * x, axis=-1, keepdims=True) + EPS)


def _rms_bwd(dy, x, g):
    r = _rms_scale(x)
    xhat = x * r
    gdy = dy * g
    dx = r * (gdy - xhat * jnp.mean(gdy * xhat, axis=-1, keepdims=True))
    return dx, jnp.sum(dy * xhat, axis=0, keepdims=True)


COLLECTIVE_PAIR, COLLECTIVE_CHIPS, COLLECTIVE_RELAY = 0, 1, 2


def _handshake(peer_numbers):
    mx, my, mc = lax.axis_index("x"), lax.axis_index("y"), lax.axis_index("c")
    barrier = pltpu.get_barrier_semaphore()
    for m in peer_numbers:
        peer = (lax.rem(mx + ((m >> 2) & 1), 2), lax.rem(my + ((m >> 1) & 1), 2), lax.rem(mc + (m & 1), 2))
        pl.semaphore_signal(barrier, inc=1, device_id=peer, device_id_type=pl.DeviceIdType.MESH)
    pl.semaphore_wait(barrier, len(peer_numbers))


class _Exchange:
    CHIPS = (2, 4, 6)

    def __init__(self, arrays, sliced, relay_at=None, among_chips=False):
        assert relay_at is None or not any(sliced)
        assert not among_chips or (all(sliced) and relay_at is None)
        self.relay_at, self.among_chips = relay_at, among_chips
        self.peers = self.CHIPS if among_chips else ((1, 2, 4) if relay_at is not None else tuple(range(1, N_DEV)))
        self.collective_id = COLLECTIVE_CHIPS if among_chips else (COLLECTIVE_RELAY if relay_at is not None else None)
        self.arrays, self.sliced, self.n = list(arrays), list(sliced), len(arrays)
        self.out_shape = [jax.ShapeDtypeStruct(arr.shape if sl else (N_DEV,) + arr.shape, arr.dtype)
                          for arr, sl in zip(arrays, sliced)]
        self.specs = [pl.BlockSpec(memory_space=pl.ANY)] * self.n
        self.scratch_shapes = [pltpu.SemaphoreType.DMA((self.n, N_DEV)),
                               pltpu.SemaphoreType.DMA((self.n, N_DEV)),
                               pltpu.SemaphoreType.DMA((self.n,))]

    HALF_VIA = ((4, 2, 5), (2, 4, 7))

    def _halves(self, a):
        rows = self.arrays[a].shape[0]
        if rows % 32:
            return ((0, rows), None)
        return ((0, rows // 2), (rows // 2, rows // 2))

    def _copies(self, ins, outs, sems):
        send_sems, recv_sems, local_sems = sems
        sliced = self.sliced
        mx, my, mc = lax.axis_index("x"), lax.axis_index("y"), lax.axis_index("c")
        me = 2 * mx + my if self.among_chips else 4 * mx + 2 * my + mc

        def peer(m):
            px = lax.rem(mx + ((m >> 2) & 1), 2)
            py = lax.rem(my + ((m >> 1) & 1), 2)
            pc = lax.rem(mc + (m & 1), 2)
            return (px, py, pc), (2 * px + py if self.among_chips else 4 * px + 2 * py + pc)

        def remote(a, m, arriving):
            pid, pflat = peer(m)
            return pltpu.make_async_remote_copy(
                src_ref=ins[a].at[pflat] if sliced[a] else ins[a],
                dst_ref=outs[a].at[pflat if arriving else me],
                send_sem=send_sems.at[a, m - 1],
                recv_sem=recv_sems.at[a, m - 1],
                device_id=pid,
                device_id_type=pl.DeviceIdType.MESH,
            )

        def local(a):
            return pltpu.make_async_copy(ins[a].at[me] if sliced[a] else ins[a], outs[a].at[me], local_sems.at[a])

        def passed_on(a, m):
            _, origin = peer(m)
            sibling, _ = peer(1)
            return pltpu.make_async_remote_copy(
                src_ref=outs[a].at[origin],
                dst_ref=outs[a].at[origin],
                send_sem=send_sems.at[a, m],
                recv_sem=recv_sems.at[a, m],
                device_id=sibling,
                device_id_type=pl.DeviceIdType.MESH,
            )

        def half_on(a, h, arriving):
            via, to, column = self.HALF_VIA[h]
            r0, nr = self._halves(a)[h]
            _, origin = peer(6 if arriving else via)
            rows = outs[a].at[origin].at[pl.ds(r0, nr)]
            return pltpu.make_async_remote_copy(
                src_ref=rows, dst_ref=rows, send_sem=send_sems.at[a, column], recv_sem=recv_sems.at[a, column],
                device_id=peer(to)[0], device_id_type=pl.DeviceIdType.MESH)

        return remote, local, passed_on, half_on

    def start(self, ins, outs, sems):
        remote, local, _, _ = self._copies(ins, outs, sems)
        if self.collective_id is not None:
            _handshake(self.peers)
        for a in range(self.n):
            local(a).start()
        for m in self.peers:
            for a in range(self.n):
                remote(a, m, False).start()

    def relay(self, ins, outs, sems):
        remote, _, passed_on, half_on = self._copies(ins, outs, sems)
        for h, (via, _, _) in enumerate(self.HALF_VIA):
            for a in range(self.n):
                remote(a, via, True).wait_recv()
                passed_on(a, via).start()
                if self._halves(a)[h] is not None:
                    half_on(a, h, False).start()

    def relay_last(self, ins, outs, sems):
        _, _, passed_on, half_on = self._copies(ins, outs, sems)
        for a in range(self.n):
            for h in range(2):
                if self._halves(a)[h] is not None:
                    half_on(a, h, True).wait_recv()
            passed_on(a, 6).start()

    def wait(self, ins, outs, sems):
        remote, local, passed_on, half_on = self._copies(ins, outs, sems)
        if self.relay_at is None:
            for m in self.peers:
                for a in range(self.n):
                    remote(a, m, True).wait_recv()
            for m in self.peers:
                for a in range(self.n):
                    remote(a, m, False).wait_send()
        else:
            for m in (1, 3, 5, 7):
                for a in range(self.n):
                    remote(a, m, True).wait_recv()
            for m in self.peers:
                for a in range(self.n):
                    remote(a, m, False).wait_send()
            for a in range(self.n):
                for m in self.CHIPS:
                    passed_on(a, m).wait_send()
                for h in range(2):
                    if self._halves(a)[h] is not None:
                        half_on(a, h, False).wait_send()
        for a in range(self.n):
            local(a).wait()


def _pair_sum(stacks, name):
    n = len(stacks)
    n_chip = N_DEV // 2
    half = [(n_chip,) + st.shape[1:] for st in stacks]

    def body(*refs):
        ins, outs, mine, theirs = refs[:n], refs[n:2 * n], refs[2 * n:3 * n], refs[3 * n:4 * n]
        local_sems, send_sems, recv_sems = refs[4 * n:]
        mx, my, mc = lax.axis_index("x"), lax.axis_index("y"), lax.axis_index("c")

        def own(a, k):
            return pltpu.make_async_copy(ins[a].at[2 * k + mc], mine[a].at[k], local_sems.at[a, k])

        def swap(a, k):
            return pltpu.make_async_remote_copy(
                src_ref=ins[a].at[2 * k + (1 - mc)], dst_ref=theirs[a].at[k],
                send_sem=send_sems.at[a, k], recv_sem=recv_sems.at[a, k],
                device_id=(mx, my, 1 - mc), device_id_type=pl.DeviceIdType.MESH)

        _handshake((1,))
        for k in range(n_chip):
            for a in range(n):
                own(a, k).start()
                swap(a, k).start()
        for k in range(n_chip):
            for a in range(n):
                own(a, k).wait()
                swap(a, k).wait()
                outs[a][k] = (mine[a][k].astype(F32) + theirs[a][k].astype(F32)).astype(LOW)

    return pl.pallas_call(
        body, name=name,
        out_shape=[jax.ShapeDtypeStruct(h, LOW) for h in half],
        in_specs=[pl.BlockSpec(memory_space=pl.ANY)] * n,
        out_specs=[pl.BlockSpec(memory_space=pltpu.VMEM)] * n,
        scratch_shapes=([pltpu.VMEM(h, st.dtype) for h, st in zip(half, stacks)] * 2
                        + [pltpu.SemaphoreType.DMA((n, n_chip))] * 3),
        compiler_params=pltpu.CompilerParams(vmem_limit_bytes=VMEM_LIMIT_BYTES, collective_id=COLLECTIVE_PAIR),
    )(*stacks)


def _exchange(arrays, sliced, name, relay=False):
    ex = _Exchange(arrays, sliced, relay_at=0 if relay else None)

    def body(*refs):
        ins, outs, sems = refs[:ex.n], refs[ex.n:2 * ex.n], refs[2 * ex.n:]
        ex.start(ins, outs, sems)
        if relay:
            ex.relay(ins, outs, sems)
            ex.relay_last(ins, outs, sems)
        ex.wait(ins, outs, sems)

    return pl.pallas_call(body, name=name, out_shape=ex.out_shape, in_specs=ex.specs, out_specs=ex.specs,
                          scratch_shapes=ex.scratch_shapes,
                          compiler_params=pltpu.CompilerParams(collective_id=ex.collective_id))(*arrays)


def _call(body, *, name, grid, in_specs, out_specs, out_shape, args, scratch_shapes=(), exchange=None):
    params = _params(*(("arbitrary",) * len(grid)))
    if exchange is None:
        return pl.pallas_call(body, name=name, grid=grid, in_specs=in_specs, out_specs=out_specs, out_shape=out_shape,
                              scratch_shapes=list(scratch_shapes), compiler_params=params)(*args)
    exs = list(exchange) if isinstance(exchange, (list, tuple)) else [exchange]
    assert len(exs) == 1 or all(ex.collective_id is None for ex in exs)
    params = pltpu.CompilerParams(dimension_semantics=("arbitrary",) * len(grid), vmem_limit_bytes=VMEM_LIMIT_BYTES,
                                  collective_id=exs[0].collective_id)
    n_in, n_out, n_scr = len(in_specs), len(out_specs), len(scratch_shapes)
    n_ex = sum(ex.n for ex in exs)
    n_steps = 1
    for g in grid:
        n_steps *= g

    def hosted(*refs):
        ins, refs = refs[:n_in], refs[n_in:]
        ex_ins, refs = refs[:n_ex], refs[n_ex:]
        outs, refs = refs[:n_out], refs[n_out:]
        ex_outs, refs = refs[:n_ex], refs[n_ex:]
        scr, sems = refs[:n_scr], refs[n_scr:]
        parts, at = [], 0
        for j, ex in enumerate(exs):
            parts.append((ex_ins[at:at + ex.n], ex_outs[at:at + ex.n], sems[3 * j:3 * j + 3]))
            at += ex.n
        step = pl.program_id(0)
        for ax in range(1, len(grid)):
            step = step * grid[ax] + pl.program_id(ax)

        @pl.when(step == 0)
        def _():
            for ex, part in zip(exs, parts):
                ex.start(*part)

        body(*ins, *outs, *scr)

        for ex, part in zip(exs, parts):
            if ex.relay_at is not None:
                @pl.when(step == min(int(ex.relay_at * n_steps), n_steps - 1))
                def _():
                    ex.relay(*part)

                @pl.when(step == min(int((ex.relay_at + RELAY_LAST_LATER) * n_steps), n_steps - 1))
                def _():
                    ex.relay_last(*part)

        @pl.when(step == n_steps - 1)
        def _():
            for ex, part in zip(exs, parts):
                ex.wait(*part)

    return pl.pallas_call(
        hosted, name=name, grid=grid,
        in_specs=list(in_specs) + [sp for ex in exs for sp in ex.specs],
        out_specs=list(out_specs) + [sp for ex in exs for sp in ex.specs],
        out_shape=list(out_shape) + [sh for ex in exs for sh in ex.out_shape],
        scratch_shapes=list(scratch_shapes) + [sc for ex in exs for sc in ex.scratch_shapes],
        compiler_params=params)(*args, *[arr for ex in exs for arr in ex.arrays])


def _ffn_fwd(x, g, h, wgt, wut, wd, name, exchange=None):
    s_len, d = x.shape
    fc = wd.shape[0]
    ts = min(TOK_TILE, s_len)
    first = h is None

    def body(*refs):
        x_ref, gh_ref, wg_ref, wu_ref, wd_ref, xo_ref = refs[:6]
        a_ref, b_ref, s_ref = refs[-3:]
        xv = x_ref[...]
        if first:
            hb = (xv * _rms_scale(xv) * gh_ref[...]).astype(LOW)
            refs[6][...] = hb
        else:
            hb = gh_ref[...]
        for c0, c1 in _slabs(fc):
            a = _dot_nt(hb, wg_ref[c0:c1, :])
            b = _dot_nt(hb, wu_ref[c0:c1, :])
            s_ref[:, c0:c1] = (a * jax.nn.sigmoid(a) * b).astype(LOW)
            a_ref[:, c0:c1] = a.astype(LOW)
            b_ref[:, c0:c1] = b.astype(LOW)
        xo_ref[...] = xv + 0.5 * _dot(s_ref[...], wd_ref[...])

    tok = pl.BlockSpec((ts, d), lambda t: (t, 0))
    hid = pl.BlockSpec((ts, fc), lambda t: (t, 0))
    tok_out = jax.ShapeDtypeStruct((s_len, d), F32)
    h_out = jax.ShapeDtypeStruct((s_len, d), LOW)
    hid_out = jax.ShapeDtypeStruct((s_len, fc), LOW)
    return _call(
        body,
        name=name,
        grid=(s_len // ts,),
        in_specs=[tok, pl.BlockSpec((1, d), lambda t: (0, 0)) if first else tok,
                  _resident(wgt), _resident(wut), _resident(wd)],
        out_specs=[tok] + ([tok] if first else []) + [hid, hid],
        out_shape=[tok_out] + ([h_out] if first else []) + [hid_out, hid_out],
        scratch_shapes=[pltpu.VMEM((ts, fc), LOW)],
        args=(x, g if first else h, wgt, wut, wd),
        exchange=exchange,
    )


def _slabs(width, slab=256):
    return [(c0, min(c0 + slab, width)) for c0 in range(0, width, slab)]


def _ffn_bwdw(df, a, b, h, wd, name, exchange=None):
    s_len, d = df.shape
    f_len = wd.shape[0]
    tm = _pick(f_len, WGRAD_ROW_CANDIDATES)
    tk = min(WGRAD_TOK_TILE, s_len)
    n_k = s_len // tk

    def body(df_ref, a_ref, b_ref, h_ref, wd_ref, da_ref, db_ref, dwd_ref, dwg_ref, dwu_ref,
             s_ref, acc_d, acc_g, acc_u):
        k = pl.program_id(1)

        @pl.when(k == 0)
        def _():
            acc_d[...] = jnp.zeros_like(acc_d)
            acc_g[...] = jnp.zeros_like(acc_g)
            acc_u[...] = jnp.zeros_like(acc_u)

        wdv = wd_ref[...]
        for r0, r1 in _slabs(tk, BWD_ROW_SLAB):
            ds = _dot_nt(df_ref[r0:r1, :], wdv)
            av = a_ref[r0:r1, :].astype(F32)
            bv = b_ref[r0:r1, :].astype(F32)
            sig = jax.nn.sigmoid(av)
            silu = av * sig
            s_ref[r0:r1, :] = (silu * bv).astype(LOW)
            da_ref[r0:r1, :] = (ds * bv * (sig * (1.0 + av * (1.0 - sig)))).astype(LOW)
            db_ref[r0:r1, :] = (ds * silu).astype(LOW)
            hv = h_ref[r0:r1, :]
            acc_d[...] += _dot_tn(s_ref[r0:r1, :], df_ref[r0:r1, :])
            acc_g[...] += _dot_tn(da_ref[r0:r1, :], hv)
            acc_u[...] += _dot_tn(db_ref[r0:r1, :], hv)

        @pl.when(k == n_k - 1)
        def _():
            dwd_ref[...] = acc_d[...].astype(LOW)
            dwg_ref[...] = acc_g[...].astype(LOW)
            dwu_ref[...] = acc_u[...].astype(LOW)

    hid = pl.BlockSpec((tk, tm), lambda i, k: (k, i))
    tok = pl.BlockSpec((tk, d), lambda i, k: (k, 0))
    wrow = pl.BlockSpec((tm, d), lambda i, k: (i, 0))
    return _call(
        body,
        name=name,
        grid=(f_len // tm, n_k),
        in_specs=[tok, hid, hid, tok, wrow],
        out_specs=[hid, hid, wrow, wrow, wrow],
        out_shape=[jax.ShapeDtypeStruct((s_len, f_len), LOW)] * 2 + [jax.ShapeDtypeStruct((f_len, d), LOW)] * 3,
        scratch_shapes=[pltpu.VMEM((tk, tm), LOW)] + [pltpu.VMEM((tm, d), F32)] * 3,
        args=(df, a, b, h, wd),
        exchange=exchange,
    )


def _ffn_dx(dxo, x, g, parts, name, exchange=None):
    s_len, d = x.shape
    ts = min(TOK_TILE, s_len)
    n_p = len(parts)

    def body(dxo_ref, x_ref, g_ref, *refs):
        dxi_ref, dg_ref = refs[4 * n_p:]

        @pl.when(pl.program_id(0) == 0)
        def _():
            dg_ref[...] = jnp.zeros_like(dg_ref)

        dh = None
        for p in range(n_p):
            da_ref, db_ref, wg_ref, wu_ref = refs[4 * p:4 * p + 4]
            part = _dot(da_ref[...], wg_ref[...]) + _dot(db_ref[...], wu_ref[...])
            dh = part if dh is None else dh + part
        dx, dgp = _rms_bwd(dh, x_ref[...], g_ref[...])
        dxi_ref[...] = dxo_ref[...] + dx
        dg_ref[...] += dgp

    tok = pl.BlockSpec((ts, d), lambda t: (t, 0))
    vec = pl.BlockSpec((1, d), lambda t: (0, 0))
    part_specs, part_args = [], []
    for da, db, wgt, wut in parts:
        hid = pl.BlockSpec((ts, da.shape[1]), lambda t: (t, 0))
        part_specs += [hid, hid, _resident(wgt), _resident(wut)]
        part_args += [da, db, wgt, wut]
    return _call(
        body,
        name=name,
        grid=(s_len // ts,),
        in_specs=[tok, tok, vec] + part_specs,
        out_specs=[tok, vec],
        out_shape=[jax.ShapeDtypeStruct((s_len, d), F32), jax.ShapeDtypeStruct((1, d), F32)],
        args=(dxo, x, g, *part_args),
        exchange=exchange,
    )


def _wgrad_tn(xm, ym, tn, stacked, name, exchange=None):
    s_len, m = xm.shape
    n = ym.shape[1]
    tk = min(WGRAD_TOK_TILE, s_len)
    n_k = s_len // tk

    def body(x_ref, y_ref, o_ref, acc):
        k = pl.program_id(1)

        @pl.when(k == 0)
        def _():
            acc[...] = jnp.zeros_like(acc)

        acc[...] += _dot_tn(x_ref[...].astype(LOW), y_ref[...].astype(LOW))

        @pl.when(k == n_k - 1)
        def _():
            o_ref[...] = acc[...].astype(LOW)

    if stacked:
        out_spec = pl.BlockSpec((None, m, tn), lambda j, k: (j, 0, 0))
        out_shape = jax.ShapeDtypeStruct((n // tn, m, tn), LOW)
    else:
        out_spec = pl.BlockSpec((m, tn), lambda j, k: (0, j))
        out_shape = jax.ShapeDtypeStruct((m, n), LOW)
    return _call(
        body,
        name=name,
        grid=(n // tn, n_k),
        in_specs=[pl.BlockSpec((tk, m), lambda j, k: (k, 0)), pl.BlockSpec((tk, tn), lambda j, k: (k, j))],
        out_specs=[out_spec],
        out_shape=[out_shape],
        scratch_shapes=[pltpu.VMEM((m, tn), F32)],
        args=(xm, ym),
        exchange=exchange,
    )


def _mix_parts(ext_ref, cw, ts, row0):
    dc = D_CONV

    def back(off, c0, c1):
        return ext_ref[HALO - off:HALO - off + ts, c0:c1]

    v, gate_b, gate_c = back(0, 0, dc), back(0, dc, 2 * dc), back(0, 2 * dc, 3 * dc)
    z0 = gate_c * v
    z1 = back(1, 2 * dc, 3 * dc) * back(1, 0, dc)
    z2 = back(2, 2 * dc, 3 * dc) * back(2, 0, dc)
    conv = cw[2:3, :] * z0 + cw[1:2, :] * z1 + cw[0:1, :] * z2
    rows = row0 + lax.broadcasted_iota(jnp.int32, (ts, 1), 0)
    pooled, inv_count = [], []
    for grp, w in enumerate(POOL_WINDOWS):
        c0 = 3 * dc + POOL_GC * grp
        u = back(0, c0, c0 + POOL_GC)
        acc = u
        for j in range(1, w):
            acc = acc + back(j, c0, c0 + POOL_GC)
        inv = 1.0 / jnp.minimum(rows + 1, w).astype(F32)
        pooled.append(acc * inv - u)
        inv_count.append(inv)
    return v, gate_b, gate_c, z0, z1, z2, conv, pooled, inv_count


def _mix_fwd(x, g, w_in, conv_w, pool_w, pool_scale, w_out, name, exchange=None):
    s_len, d = x.shape
    n_blk, _, wcols = w_in.shape
    p_len = n_blk * wcols
    d_mix = w_out.shape[0]
    ts = min(MIX_TOK_TILE, s_len)
    dc = D_CONV

    def body(x_ref, g_ref, win_ref, cw_ref, pw_ref, ps_ref, wout_ref, x2_ref, h_ref, proj_ref, ext_ref, cat_ref):
        t = pl.program_id(0)

        @pl.when(t == 0)
        def _():
            ext_ref[0:HALO, :] = jnp.zeros((HALO, p_len), F32)

        xv = x_ref[...]
        hb = (xv * _rms_scale(xv) * g_ref[...]).astype(LOW)
        h_ref[...] = hb
        for k in range(n_blk):
            ext_ref[HALO:HALO + ts, wcols * k:wcols * (k + 1)] = _dot(hb, win_ref[k])
        proj_ref[...] = ext_ref[HALO:HALO + ts, :]

        _, gate_b, _, _, _, _, conv, pooled, _ = _mix_parts(ext_ref, cw_ref[...], ts, t * ts)
        cat_ref[:, 0:dc] = (gate_b * conv).astype(LOW)
        for grp in range(len(POOL_WINDOWS)):
            c0 = POOL_GC * grp
            lin = _dot(pooled[grp].astype(LOW), pw_ref[grp])
            cat_ref[:, dc + c0:dc + c0 + POOL_GC] = (lin * ps_ref[:, c0:c0 + POOL_GC]).astype(LOW)
        x2_ref[...] = xv + _dot(cat_ref[...], wout_ref[...])
        ext_ref[0:HALO, :] = ext_ref[ts:ts + HALO, :]

    tok = pl.BlockSpec((ts, d), lambda t: (t, 0))

    def whole(arr):
        return pl.BlockSpec(arr.shape, lambda t: (0,) * arr.ndim)

    return _call(
        body,
        name=name,
        grid=(s_len // ts,),
        in_specs=[tok, whole(g), _resident(w_in), whole(conv_w), whole(pool_w), whole(pool_scale), _resident(w_out)],
        out_specs=[tok, tok, pl.BlockSpec((ts, p_len), lambda t: (t, 0))],
        out_shape=[
            jax.ShapeDtypeStruct((s_len, d), F32),
            jax.ShapeDtypeStruct((s_len, d), LOW),
            jax.ShapeDtypeStruct((s_len, p_len), F32),
        ],
        scratch_shapes=[pltpu.VMEM((ts + HALO, p_len), F32), pltpu.VMEM((ts, d_mix), LOW)],
        args=(x, g, w_in, conv_w, pool_w, pool_scale, w_out),
        exchange=exchange,
    )


def _mix_bwd(dx2, x, proj, g, w_in, conv_w, pool_w, pool_scale, w_out, name, exchange=None):
    s_len, d = x.shape
    n_blk, _, wcols = w_in.shape
    p_len = n_blk * wcols
    d_mix = w_out.shape[0]
    ts = min(MIX_TOK_TILE, s_len)
    n_t = s_len // ts
    dc = D_CONV
    n_grp = len(POOL_WINDOWS)

    def body(dx2_ref, x_ref, proj_ref, halo_ref, g_ref, win_ref, cw_ref, pw_ref, ps_ref, wout_ref,
             dx_ref, dproj_ref, cat_ref, dg_ref, dcw_ref, dpw_ref, dps_ref, df_ref, ext_ref, fut_ref):
        i = pl.program_id(0)
        t = n_t - 1 - i

        @pl.when(i == 0)
        def _():
            dg_ref[...] = jnp.zeros_like(dg_ref)
            dcw_ref[...] = jnp.zeros_like(dcw_ref)
            dpw_ref[...] = jnp.zeros_like(dpw_ref)
            dps_ref[...] = jnp.zeros_like(dps_ref)
            fut_ref[ts:ts + HALO, :] = jnp.zeros((HALO, d_mix), F32)

        ext_ref[HALO:HALO + ts, :] = proj_ref[...]

        @pl.when(t == 0)
        def _():
            ext_ref[0:HALO, :] = jnp.zeros((HALO, p_len), F32)

        @pl.when(t > 0)
        def _():
            ext_ref[0:HALO, :] = halo_ref[...]

        cw = cw_ref[...]
        v, gate_b, gate_c, z0, z1, z2, conv, pooled, inv_count = _mix_parts(ext_ref, cw, ts, t * ts)
        dx2 = dx2_ref[...]
        dcat = _dot_nt(dx2.astype(LOW), wout_ref[...])

        dy_a = dcat[:, 0:dc]
        dconv = dy_a * gate_b
        fut_ref[0:ts, 0:dc] = dconv
        cat_ref[:, 0:dc] = (gate_b * conv).astype(LOW)
        dproj_ref[:, dc:2 * dc] = (dy_a * conv).astype(LOW)
        dcw_ref[2:3, :] += jnp.sum(dconv * z0, axis=0, keepdims=True)
        dcw_ref[1:2, :] += jnp.sum(dconv * z1, axis=0, keepdims=True)
        dcw_ref[0:1, :] += jnp.sum(dconv * z2, axis=0, keepdims=True)

        dpool = []
        for grp in range(n_grp):
            c0 = POOL_GC * grp
            pooled_b = pooled[grp].astype(LOW)
            lin = _dot(pooled_b, pw_ref[grp])
            dy_b = dcat[:, dc + c0:dc + c0 + POOL_GC]
            scale = ps_ref[:, c0:c0 + POOL_GC]
            cat_ref[:, dc + c0:dc + c0 + POOL_GC] = (lin * scale).astype(LOW)
            dps_ref[:, c0:c0 + POOL_GC] += jnp.sum(dy_b * lin, axis=0, keepdims=True)
            dlin = (dy_b * scale).astype(LOW)
            dpw_ref[grp] += _dot_tn(pooled_b, dlin)
            dpool.append(_dot_nt(dlin, pw_ref[grp]))
            fut_ref[0:ts, dc + c0:dc + c0 + POOL_GC] = dpool[grp] * inv_count[grp]

        def ahead(off, c0, c1):
            return fut_ref[off:off + ts, c0:c1]

        dz = cw[2:3, :] * ahead(0, 0, dc) + cw[1:2, :] * ahead(1, 0, dc) + cw[0:1, :] * ahead(2, 0, dc)
        dproj_ref[:, 0:dc] = (dz * gate_c).astype(LOW)
        dproj_ref[:, 2 * dc:3 * dc] = (dz * v).astype(LOW)
        for grp, w in enumerate(POOL_WINDOWS):
            c0 = dc + POOL_GC * grp
            acc = ahead(0, c0, c0 + POOL_GC)
            for j in range(1, w):
                acc = acc + ahead(j, c0, c0 + POOL_GC)
            dproj_ref[:, 2 * dc + c0:2 * dc + c0 + POOL_GC] = (acc - dpool[grp]).astype(LOW)

        dh = _dot_nt(dproj_ref[:, 0:wcols], win_ref[0])
        for k in range(1, n_blk):
            dh += _dot_nt(dproj_ref[:, wcols * k:wcols * (k + 1)], win_ref[k])
        dx, dgp = _rms_bwd(dh, x_ref[...], g_ref[...])
        dx = dx2 + dx
        dx_ref[...] = dx
        df_ref[...] = (0.5 * dx).astype(LOW)
        dg_ref[...] += dgp
        fut_ref[ts:ts + HALO, :] = fut_ref[0:HALO, :]

    tok = pl.BlockSpec((ts, d), lambda i: (n_t - 1 - i, 0))
    halo = pl.BlockSpec((HALO, p_len), lambda i: (jnp.maximum((n_t - 1 - i) * (ts // HALO) - 1, 0), 0))

    def whole(arr):
        return pl.BlockSpec(arr.shape, lambda i: (0,) * arr.ndim)

    return _call(
        body,
        name=name,
        grid=(n_t,),
        in_specs=[tok, tok, pl.BlockSpec((ts, p_len), lambda i: (n_t - 1 - i, 0)), halo,
                  whole(g), _resident(w_in), whole(conv_w), whole(pool_w), whole(pool_scale), _resident(w_out)],
        out_specs=[tok, pl.BlockSpec((ts, p_len), lambda i: (n_t - 1 - i, 0)),
                   pl.BlockSpec((ts, d_mix), lambda i: (n_t - 1 - i, 0)),
                   whole(g), whole(conv_w), whole(pool_w), whole(pool_scale), tok],
        out_shape=[
            jax.ShapeDtypeStruct((s_len, d), F32),
            jax.ShapeDtypeStruct((s_len, p_len), LOW),
            jax.ShapeDtypeStruct((s_len, d_mix), LOW),
            jax.ShapeDtypeStruct(g.shape, F32),
            jax.ShapeDtypeStruct(conv_w.shape, F32),
            jax.ShapeDtypeStruct(pool_w.shape, F32),
            jax.ShapeDtypeStruct(pool_scale.shape, F32),
            jax.ShapeDtypeStruct((s_len, d), LOW),
        ],
        scratch_shapes=[pltpu.VMEM((ts + HALO, p_len), F32), pltpu.VMEM((ts + HALO, d_mix), F32)],
        args=(dx2, x, proj, proj, g, w_in, conv_w, pool_w, pool_scale, w_out),
        exchange=exchange,
    )


def _ffn_fwd_loss(x, h, wgt, wut, wd, g, target, name):
    s_len, d = x.shape
    fc = wd.shape[0]
    ts = min(TOK_TILE, s_len)

    def body(x_ref, h_ref, wg_ref, wu_ref, wd_ref, g_ref, tgt_ref,
             a_ref, b_ref, loss_ref, dx_ref, dg_ref, df_ref, s_ref):
        @pl.when(pl.program_id(0) == 0)
        def _():
            loss_ref[...] = jnp.zeros_like(loss_ref)
            dg_ref[...] = jnp.zeros_like(dg_ref)

        hb = h_ref[...]
        for c0, c1 in _slabs(fc):
            a = _dot_nt(hb, wg_ref[c0:c1, :])
            b = _dot_nt(hb, wu_ref[c0:c1, :])
            s_ref[:, c0:c1] = (a * jax.nn.sigmoid(a) * b).astype(LOW)
            a_ref[:, c0:c1] = a.astype(LOW)
            b_ref[:, c0:c1] = b.astype(LOW)
        xv = x_ref[...] + 0.5 * _dot(s_ref[...], wd_ref[...])
        gv = g_ref[...]
        err = xv * _rms_scale(xv) * gv - tgt_ref[...]
        loss_ref[...] += 0.5 * jnp.sum(jnp.mean(err * err, axis=-1, keepdims=True), axis=0, keepdims=True)
        dx, dgp = _rms_bwd(err * (1.0 / d), xv, gv)
        dx_ref[...] = dx
        df_ref[...] = (0.5 * dx).astype(LOW)
        dg_ref[...] += dgp

    tok = pl.BlockSpec((ts, d), lambda t: (t, 0))
    vec = pl.BlockSpec((1, d), lambda t: (0, 0))
    hid = pl.BlockSpec((ts, fc), lambda t: (t, 0))
    return pl.pallas_call(
        body,
        name=name,
        grid=(s_len // ts,),
        in_specs=[tok, tok, _resident(wgt), _resident(wut), _resident(wd), vec, tok],
        out_specs=[hid, hid, pl.BlockSpec((1, 128), lambda t: (0, 0)), tok, vec, tok],
        out_shape=[
            jax.ShapeDtypeStruct((s_len, fc), LOW),
            jax.ShapeDtypeStruct((s_len, fc), LOW),
            jax.ShapeDtypeStruct((1, 128), F32),
            jax.ShapeDtypeStruct((s_len, d), F32),
            jax.ShapeDtypeStruct((1, d), F32),
            jax.ShapeDtypeStruct((s_len, d), LOW),
        ],
        scratch_shapes=[pltpu.VMEM((ts, fc), LOW)],
        compiler_params=_params("arbitrary"),
    )(x, h, wgt, wut, wd, g, target)


def _row_tile(rows, cols, stack_bytes):
    budget = 20 * 1024 * 1024
    per_row = cols * (4 * 7 + stack_bytes)
    for tr in (rows, 512, 256, 176, 128, 64, 32, 16, 8):
        if rows % tr == 0 and tr % 8 == 0 and tr * per_row * 2 <= budget:
            return tr
    return rows


def _sum_stack(stack, name):
    n, r, c = stack.shape
    tr = _row_tile(r, c, n * stack.dtype.itemsize)

    def body(s_ref, o_ref):
        acc = s_ref[0].astype(F32)
        for k in range(1, n):
            acc = acc + s_ref[k].astype(F32)
        o_ref[...] = acc

    return pl.pallas_call(
        body,
        name=name,
        grid=(r // tr,),
        in_specs=[pl.BlockSpec((n, tr, c), lambda i: (0, i, 0))],
        out_specs=pl.BlockSpec((tr, c), lambda i: (i, 0)),
        out_shape=jax.ShapeDtypeStruct((r, c), F32),
        compiler_params=_params("arbitrary"),
    )(stack)


def _adamw(stacks, w, m, v, name):
    stacks = list(stacks) if isinstance(stacks, (list, tuple)) else [stacks]
    r, c = w.shape
    n = stacks[0].shape[0]
    part_rows = [st.shape[1] for st in stacks]
    assert sum(part_rows) == r and all(st.shape[0] == n for st in stacks)
    first_row = [sum(part_rows[:j]) for j in range(len(stacks))]
    tc = next(t for t in (512, 256, 128) if c % t == 0)
    c1 = 1.0 - ADAM_B1 ** ADAM_STEP
    c2 = 1.0 - ADAM_B2 ** ADAM_STEP

    def body(*refs):
        s_refs = refs[:len(stacks)]
        w_ref, m_ref, v_ref, g_ref, d_ref, mo_ref, vo_ref = refs[len(stacks):]
        for s_ref, r0, nr in zip(s_refs, first_row, part_rows):
            gv = s_ref[0].astype(F32)
            for k in range(1, n):
                gv = gv + s_ref[k].astype(F32)
            mn = ADAM_B1 * m_ref[r0:r0 + nr, :] + (1.0 - ADAM_B1) * gv
            vn = ADAM_B2 * v_ref[r0:r0 + nr, :] + (1.0 - ADAM_B2) * (gv * gv)
            g_ref[r0:r0 + nr, :] = gv
            mo_ref[r0:r0 + nr, :] = mn
            vo_ref[r0:r0 + nr, :] = vn
            d_ref[r0:r0 + nr, :] = -ADAM_LR * ((mn / c1) / (jnp.sqrt(vn / c2) + ADAM_EPS)
                                               + ADAM_WD * w_ref[r0:r0 + nr, :])

    blk = pl.BlockSpec((r, tc), lambda i: (0, i))
    return pl.pallas_call(
        body,
        name=name,
        grid=(c // tc,),
        in_specs=[pl.BlockSpec((n, nr, tc), lambda i: (0, 0, i)) for nr in part_rows] + [blk, blk, blk],
        out_specs=[blk] * 4,
        out_shape=[jax.ShapeDtypeStruct((r, c), F32)] * 4,
        compiler_params=_params("arbitrary"),
    )(*stacks, w, m, v)


def _to_sheet(parts):
    sheets, spans = [], []
    row = 0
    for p in parts:
        flat = p.reshape(-1).astype(F32)
        rows = -(-flat.shape[0] // 1024) * 8
        flat = jnp.pad(flat, (0, rows * 128 - flat.shape[0]))
        sheets.append(flat.reshape(rows, 128))
        spans.append((row, p.size, p.shape))
        row += rows
    return jnp.concatenate(sheets, axis=0), spans


def _from_sheet(sheet, spans):
    out = []
    for row, size, shape in spans:
        rows = -(-size // 1024) * 8
        out.append(sheet[row:row + rows].reshape(-1)[:size].reshape(shape))
    return out


def kernel(x, norm_ffn1, ffn1_w_gate, ffn1_w_up, ffn1_w_down, norm_mix, w_in, conv_w, pool_w, pool_scale, w_out, norm_ffn2, ffn2_w_gate, ffn2_w_up, ffn2_w_down, norm_final, loss_target, m_norm_ffn1, m_ffn1_w_gate, m_ffn1_w_up, m_ffn1_w_down, m_norm_mix, m_w_in, m_conv_w, m_pool_w, m_pool_scale, m_w_out, m_norm_ffn2, m_ffn2_w_gate, m_ffn2_w_up, m_ffn2_w_down, m_norm_final, v_norm_ffn1, v_ffn1_w_gate, v_ffn1_w_up, v_ffn1_w_down, v_norm_mix, v_w_in, v_conv_w, v_pool_w, v_pool_scale, v_w_out, v_norm_ffn2, v_ffn2_w_gate, v_ffn2_w_up, v_ffn2_w_down, v_norm_final):
    me = 4 * lax.axis_index("x") + 2 * lax.axis_index("y") + lax.axis_index("c")
    xs, tgt = x[0], loss_target[0]
    s_len, d = xs.shape
    f_shard = ffn1_w_down.shape[1]
    conv_shard = conv_w.shape[2]

    def low_t(wt):
        return wt[0].T.astype(LOW)

    def by_dev(gw):
        return gw.reshape(N_DEV, -1, d)

    conv_tile = jnp.zeros((8, 128), F32).at[0:conv_w.shape[1], 0:conv_shard].set(conv_w[0])
    pool_w_low = pool_w[0].astype(LOW)

    rows_a = -(-f_shard // 64) * 32

    def parts_of(w_gate, w_up, w_down):
        shards = [low_t(w_gate), low_t(w_up), w_down[0].astype(LOW)]
        return [s[:rows_a] for s in shards], [s[rows_a:] for s in shards]

    def rows_flat(stacks):
        return [st.reshape(-1, d) for st in stacks]

    def gather(shards):
        return _Exchange(shards, [False] * len(shards), relay_at=0.6)

    def scatter(pairs):
        return _Exchange(pairs, [True] * len(pairs), among_chips=True)

    w1a_shards, w1b_shards = parts_of(ffn1_w_gate, ffn1_w_up, ffn1_w_down)
    w2a_shards, w2b_shards = parts_of(ffn2_w_gate, ffn2_w_up, ffn2_w_down)

    wg1a, wu1a, wd1a = rows_flat(_exchange(w1a_shards, [False] * 3, "gather_ffn1_a", relay=True))
    xa, h1, a1a, b1a, *w1b = _ffn_fwd(xs, norm_ffn1, None, wg1a, wu1a, wd1a, "ffn1_fwd_a",
                                      exchange=gather(w1b_shards))
    wg1b, wu1b, wd1b = rows_flat(w1b)
    x1, a1b, b1b, w_in_full, w_out_full, conv_tiles = _ffn_fwd(
        xa, None, h1, wg1b, wu1b, wd1b, "ffn1_fwd_b",
        exchange=gather([w_in[0].astype(LOW), w_out[0].astype(LOW), conv_tile]))
    w_out_full = w_out_full.reshape(-1, d)
    conv_full = jnp.concatenate([conv_tiles[k, 0:conv_w.shape[1], 0:conv_shard] for k in range(N_DEV)], axis=1)
    x2, h2, proj, *w2a = _mix_fwd(x1, norm_mix, w_in_full, conv_full, pool_w_low, pool_scale, w_out_full, "mix_fwd",
                                  exchange=gather(w2a_shards))
    wg2a, wu2a, wd2a = rows_flat(w2a)
    xb, h3, a2a, b2a, *w2b = _ffn_fwd(x2, norm_ffn2, None, wg2a, wu2a, wd2a, "ffn2_fwd_a",
                                      exchange=gather(w2b_shards))
    wg2b, wu2b, wd2b = rows_flat(w2b)
    a2b, b2b, loss_row, dx3, dg_final, df3 = _ffn_fwd_loss(
        xb, h3, wg2b, wu2b, wd2b, norm_final.reshape(1, d), tgt, "ffn2_fwd_b_loss")

    da2a, db2a, dwd_a, dwg_a, dwu_a = _ffn_bwdw(df3, a2a, b2a, h3, wd2a, "ffn2_bwdw_a")
    da2b, db2b, dwd_b, dwg_b, dwu_b = _ffn_bwdw(df3, a2b, b2b, h3, wd2b, "ffn2_bwdw_b")
    pairs = _pair_sum([by_dev(dwg_a), by_dev(dwu_a), by_dev(dwd_a), by_dev(dwg_b), by_dev(dwu_b), by_dev(dwd_b)],
                      "pair_sum_ffn2")
    dx2, dg_ffn2, *got_2a = _ffn_dx(dx3, x2, norm_ffn2, [(da2a, db2a, wg2a, wu2a), (da2b, db2b, wg2b, wu2b)],
                                    "ffn2_dx", exchange=scatter(pairs[:3]))
    dx1, dproj, cat, dg_mix, dconv, dpool_w, dpool_scale, df1, *got_2b = _mix_bwd(
        dx2, x1, proj, norm_mix, w_in_full, conv_full, pool_w_low, pool_scale, w_out_full, "mix_bwd",
        exchange=scatter(pairs[3:]))
    small_parts = [dg_mix, dg_ffn2, dg_final, dconv, dpool_w, dpool_scale, loss_row]
    small_sheet, spans = _to_sheet(small_parts)
    dw_in, got_small = _wgrad_tn(h2, dproj, W_IN_SHARD, True, "w_in_wgrad",
                                 exchange=_Exchange([small_sheet], [False], relay_at=0.5))
    (dw_out,) = _wgrad_tn(cat, dx2, d, False, "w_out_wgrad")
    pairs = _pair_sum([dw_in, by_dev(dw_out)], "pair_sum_mix")
    da1a, db1a, dwd, dwg, dwu, got_in, got_out = _ffn_bwdw(
        df1, a1a, b1a, h1, wd1a, "ffn1_bwdw_a", exchange=scatter(pairs))
    pairs = _pair_sum([by_dev(dwg), by_dev(dwu), by_dev(dwd)], "pair_sum_ffn1_a")
    da1b, db1b, dwd, dwg, dwu, *got_1a = _ffn_bwdw(df1, a1b, b1b, h1, wd1b, "ffn1_bwdw_b", exchange=scatter(pairs))
    pairs = _pair_sum([by_dev(dwg), by_dev(dwu), by_dev(dwd)], "pair_sum_ffn1_b")
    dx0, dg_ffn1, *got_1b = _ffn_dx(dx1, xs, norm_ffn1, [(da1a, db1a, wg1a, wu1a), (da1b, db1b, wg1b, wu1b)],
                                    "ffn1_dx", exchange=scatter(pairs))
    (got_n1,) = _exchange([dg_ffn1.reshape(8, 128)], [False], "gather_dnorm_ffn1")

    outs = {}

    def update(name, stacks, w, m, v):
        outs[name] = _adamw(stacks, w[0], m[0], v[0], "adamw_" + name)

    def update_t(name, stacks, w, m, v):
        res = _adamw(stacks, w[0].T, m[0].T, v[0].T, "adamw_" + name)
        outs[name] = tuple(r.T for r in res)

    update_t("ffn1_w_gate", [got_1a[0], got_1b[0]], ffn1_w_gate, m_ffn1_w_gate, v_ffn1_w_gate)
    update_t("ffn1_w_up", [got_1a[1], got_1b[1]], ffn1_w_up, m_ffn1_w_up, v_ffn1_w_up)
    update("ffn1_w_down", [got_1a[2], got_1b[2]], ffn1_w_down, m_ffn1_w_down, v_ffn1_w_down)
    update("w_in", got_in, w_in, m_w_in, v_w_in)
    update("w_out", got_out, w_out, m_w_out, v_w_out)
    update_t("ffn2_w_gate", [got_2a[0], got_2b[0]], ffn2_w_gate, m_ffn2_w_gate, v_ffn2_w_gate)
    update_t("ffn2_w_up", [got_2a[1], got_2b[1]], ffn2_w_up, m_ffn2_w_up, v_ffn2_w_up)
    update("ffn2_w_down", [got_2a[2], got_2b[2]], ffn2_w_down, m_ffn2_w_down, v_ffn2_w_down)

    g_small = _from_sheet(_sum_stack(got_small, "sum_small"), spans)
    g_norm_ffn1 = _sum_stack(got_n1, "sum_dnorm_ffn1").reshape(norm_ffn1.shape)
    g_conv = lax.dynamic_slice_in_dim(g_small[3], me * conv_shard, conv_shard, axis=1)
    small_names = ["norm_ffn1", "norm_mix", "norm_ffn2", "norm_final", "conv_w", "pool_w", "pool_scale"]
    small_g = [g_norm_ffn1, g_small[0], g_small[1], g_small[2].reshape(norm_final.shape), g_conv[None],
               g_small[4][None], g_small[5]]
    small_w = [norm_ffn1, norm_mix, norm_ffn2, norm_final, conv_w, pool_w, pool_scale]
    small_m = [m_norm_ffn1, m_norm_mix, m_norm_ffn2, m_norm_final, m_conv_w, m_pool_w, m_pool_scale]
    small_v = [v_norm_ffn1, v_norm_mix, v_norm_ffn2, v_norm_final, v_conv_w, v_pool_w, v_pool_scale]
    g_sheet, spans_u = _to_sheet(small_g)
    w_sheet, _ = _to_sheet(small_w)
    m_sheet, _ = _to_sheet(small_m)
    v_sheet, _ = _to_sheet(small_v)
    upd = _adamw(g_sheet[None], w_sheet, m_sheet, v_sheet, "adamw_small")
    small_out = [_from_sheet(u, spans_u) for u in upd]
    for k, nm in enumerate(small_names):
        outs[nm] = tuple(small_out[j][k] for j in range(4))

    loss = g_small[6][0, 0]
    order = ["norm_ffn1", "ffn1_w_gate", "ffn1_w_up", "ffn1_w_down", "norm_mix", "w_in", "conv_w", "pool_w",
             "pool_scale", "w_out", "norm_ffn2", "ffn2_w_gate", "ffn2_w_up", "ffn2_w_down", "norm_final"]
    big = {"ffn1_w_gate", "ffn1_w_up", "ffn1_w_down", "w_in", "w_out", "ffn2_w_gate", "ffn2_w_up", "ffn2_w_down"}

    def leaf(nm, j):
        val = outs[nm][j]
        return val[None] if nm in big else val

    return (loss, dx0[None],
            *[leaf(nm, 0) for nm in order], *[leaf(nm, 1) for nm in order],
            *[leaf(nm, 2) for nm in order], *[leaf(nm, 3) for nm in order])
```

```python
import jax
import jax.numpy as jnp
from jax import lax
from jax.experimental import pallas as pl
from jax.experimental.pallas import tpu as pltpu

F32 = jnp.float32
LOW = jnp.bfloat16

N_DEV = 8
EPS = 1e-6
D_CONV = 512
POOL_WINDOWS = (2, 4, 8, 16)
POOL_GC = 128
HALO = 16
W_IN_SHARD = 256

ADAM_LR = 0.001
ADAM_B1 = 0.9
ADAM_B2 = 0.999
ADAM_EPS = 1e-08
ADAM_WD = 0.01
ADAM_STEP = 10

VMEM_LIMIT_BYTES = 56 * 1024 * 1024
TOK_TILE = 512
MIX_TOK_TILE = 256
WGRAD_TOK_TILE = 4096
WGRAD_ROW_CANDIDATES = (256, 128)
RELAY_LAST_LATER = 0.25
BWD_ROW_SLAB = 2048


def _params(*sem):
    return pltpu.CompilerParams(dimension_semantics=sem, vmem_limit_bytes=VMEM_LIMIT_BYTES)


def _resident(arr):
    return pl.BlockSpec(arr.shape, lambda *_: (0,) * arr.ndim, pipeline_mode=pl.Buffered(1))


def _pick(n, candidates):
    for c in candidates:
        if n % c == 0:
            return c
    raise ValueError(f"no tile in {candidates} divides {n}")


def _dot(a, b):
    return lax.dot_general(a, b, (((1,), (0,)), ((), ())), preferred_element_type=F32)


def _dot_nt(a, b):
    return lax.dot_general(a, b, (((1,), (1,)), ((), ())), preferred_element_type=F32)


def _dot_tn(a, b):
    return lax.dot_general(a, b, (((0,), (0,)), ((), ())), preferred_element_type=F32)


def _rms_scale(x):
    return lax.rsqrt(jnp.mean(x * x, axis=-1, keepdims=True) + EPS)


def _rms_bwd(dy, x, g):
    r = _rms_scale(x)
    xhat = x * r
    gdy = dy * g
    dx = r * (gdy - xhat * jnp.mean(gdy * xhat, axis=-1, keepdims=True))
    return dx, jnp.sum(dy * xhat, axis=0, keepdims=True)


COLLECTIVE_PAIR, COLLECTIVE_CHIPS, COLLECTIVE_RELAY = 0, 1, 2


def _handshake(peer_numbers):
    mx, my, mc = lax.axis_index("x"), lax.axis_index("y"), lax.axis_index("c")
    barrier = pltpu.get_barrier_semaphore()
    for m in peer_numbers:
        peer = (lax.rem(mx + ((m >> 2) & 1), 2), lax.rem(my + ((m >> 1) & 1), 2), lax.rem(mc + (m & 1), 2))
        pl.semaphore_signal(barrier, inc=1, device_id=peer, device_id_type=pl.DeviceIdType.MESH)
    pl.semaphore_wait(barrier, len(peer_numbers))


class _Exchange:
    CHIPS = (2, 4, 6)

    def __init__(self, arrays, sliced, relay_at=None, among_chips=False):
        assert relay_at is None or not any(sliced)
        assert not among_chips or (all(sliced) and relay_at is None)
        self.relay_at, self.among_chips = relay_at, among_chips
        self.peers = self.CHIPS if among_chips else ((1, 2, 4) if relay_at is not None else tuple(range(1, N_DEV)))
        self.collective_id = COLLECTIVE_CHIPS if among_chips else (COLLECTIVE_RELAY if relay_at is not None else None)
        self.rows = [arr[1:] if isinstance(arr, tuple) else None for arr in arrays]
        self.arrays = [arr[0] if isinstance(arr, tuple) else arr for arr in arrays]
        self.sliced, self.n = list(sliced), len(arrays)
        assert all(rg is None or not sl for rg, sl in zip(self.rows, sliced))
        self.block_shape = [arr.shape if rg is None else (rg[1],) + arr.shape[1:]
                            for arr, rg in zip(self.arrays, self.rows)]
        self.out_shape = [jax.ShapeDtypeStruct(shape if sl else (N_DEV,) + shape, arr.dtype)
                          for arr, shape, sl in zip(self.arrays, self.block_shape, sliced)]
        self.specs = [pl.BlockSpec(memory_space=pl.ANY)] * self.n
        self.scratch_shapes = [pltpu.SemaphoreType.DMA((self.n, N_DEV)),
                               pltpu.SemaphoreType.DMA((self.n, N_DEV)),
                               pltpu.SemaphoreType.DMA((self.n,))]

    HALF_VIA = ((4, 2, 5), (2, 4, 7))

    def _halves(self, a):
        rows = self.block_shape[a][0]
        if rows % 32:
            return ((0, rows), None)
        return ((0, rows // 2), (rows // 2, rows // 2))

    def _copies(self, ins, outs, sems):
        send_sems, recv_sems, local_sems = sems
        sliced = self.sliced
        mx, my, mc = lax.axis_index("x"), lax.axis_index("y"), lax.axis_index("c")
        me = 2 * mx + my if self.among_chips else 4 * mx + 2 * my + mc

        def peer(m):
            px = lax.rem(mx + ((m >> 2) & 1), 2)
            py = lax.rem(my + ((m >> 1) & 1), 2)
            pc = lax.rem(mc + (m & 1), 2)
            return (px, py, pc), (2 * px + py if self.among_chips else 4 * px + 2 * py + pc)

        def mine(a):
            return ins[a] if self.rows[a] is None else ins[a].at[pl.ds(*self.rows[a])]

        def remote(a, m, arriving):
            pid, pflat = peer(m)
            return pltpu.make_async_remote_copy(
                src_ref=ins[a].at[pflat] if sliced[a] else mine(a),
                dst_ref=outs[a].at[pflat if arriving else me],
                send_sem=send_sems.at[a, m - 1],
                recv_sem=recv_sems.at[a, m - 1],
                device_id=pid,
                device_id_type=pl.DeviceIdType.MESH,
            )

        def local(a):
            return pltpu.make_async_copy(ins[a].at[me] if sliced[a] else mine(a), outs[a].at[me], local_sems.at[a])

        def passed_on(a, m):
            _, origin = peer(m)
            sibling, _ = peer(1)
            return pltpu.make_async_remote_copy(
                src_ref=outs[a].at[origin],
                dst_ref=outs[a].at[origin],
                send_sem=send_sems.at[a, m],
                recv_sem=recv_sems.at[a, m],
                device_id=sibling,
                device_id_type=pl.DeviceIdType.MESH,
            )

        def half_on(a, h, arriving):
            via, to, column = self.HALF_VIA[h]
            r0, nr = self._halves(a)[h]
            _, origin = peer(6 if arriving else via)
            rows = outs[a].at[origin].at[pl.ds(r0, nr)]
            return pltpu.make_async_remote_copy(
                src_ref=rows, dst_ref=rows, send_sem=send_sems.at[a, column], recv_sem=recv_sems.at[a, column],
                device_id=peer(to)[0], device_id_type=pl.DeviceIdType.MESH)

        return remote, local, passed_on, half_on

    def start(self, ins, outs, sems):
        remote, local, _, _ = self._copies(ins, outs, sems)
        if self.collective_id is not None:
            _handshake(self.peers)
        for a in range(self.n):
            local(a).start()
        for m in self.peers:
            for a in range(self.n):
                remote(a, m, False).start()

    def relay(self, ins, outs, sems):
        remote, _, passed_on, half_on = self._copies(ins, outs, sems)
        for h, (via, _, _) in enumerate(self.HALF_VIA):
            for a in range(self.n):
                remote(a, via, True).wait_recv()
                passed_on(a, via).start()
                if self._halves(a)[h] is not None:
                    half_on(a, h, False).start()

    def relay_last(self, ins, outs, sems):
        _, _, passed_on, half_on = self._copies(ins, outs, sems)
        for a in range(self.n):
            for h in range(2):
                if self._halves(a)[h] is not None:
                    half_on(a, h, True).wait_recv()
            passed_on(a, 6).start()

    def wait(self, ins, outs, sems):
        remote, local, passed_on, half_on = self._copies(ins, outs, sems)
        if self.relay_at is None:
            for m in self.peers:
                for a in range(self.n):
                    remote(a, m, True).wait_recv()
            for m in self.peers:
                for a in range(self.n):
                    remote(a, m, False).wait_send()
        else:
            for m in (1, 3, 5, 7):
                for a in range(self.n):
                    remote(a, m, True).wait_recv()
            for m in self.peers:
                for a in range(self.n):
                    remote(a, m, False).wait_send()
            for a in range(self.n):
                for m in self.CHIPS:
                    passed_on(a, m).wait_send()
                for h in range(2):
                    if self._halves(a)[h] is not None:
                        half_on(a, h, False).wait_send()
        for a in range(self.n):
            local(a).wait()


def _pair_sum(stacks, name):
    n = len(stacks)
    n_chip = N_DEV // 2
    half = [(n_chip,) + st.shape[1:] for st in stacks]

    def body(*refs):
        ins, outs, mine, theirs = refs[:n], refs[n:2 * n], refs[2 * n:3 * n], refs[3 * n:4 * n]
        local_sems, send_sems, recv_sems = refs[4 * n:]
        mx, my, mc = lax.axis_index("x"), lax.axis_index("y"), lax.axis_index("c")

        def own(a, k):
            return pltpu.make_async_copy(ins[a].at[2 * k + mc], mine[a].at[k], local_sems.at[a, k])

        def swap(a, k):
            return pltpu.make_async_remote_copy(
                src_ref=ins[a].at[2 * k + (1 - mc)], dst_ref=theirs[a].at[k],
                send_sem=send_sems.at[a, k], recv_sem=recv_sems.at[a, k],
                device_id=(mx, my, 1 - mc), device_id_type=pl.DeviceIdType.MESH)

        _handshake((1,))
        for k in range(n_chip):
            for a in range(n):
                own(a, k).start()
                swap(a, k).start()
        for k in range(n_chip):
            for a in range(n):
                own(a, k).wait()
                swap(a, k).wait()
                outs[a][k] = (mine[a][k].astype(F32) + theirs[a][k].astype(F32)).astype(LOW)

    return pl.pallas_call(
        body, name=name,
        out_shape=[jax.ShapeDtypeStruct(h, LOW) for h in half],
        in_specs=[pl.BlockSpec(memory_space=pl.ANY)] * n,
        out_specs=[pl.BlockSpec(memory_space=pltpu.VMEM)] * n,
        scratch_shapes=([pltpu.VMEM(h, st.dtype) for h, st in zip(half, stacks)] * 2
                        + [pltpu.SemaphoreType.DMA((n, n_chip))] * 3),
        compiler_params=pltpu.CompilerParams(vmem_limit_bytes=VMEM_LIMIT_BYTES, collective_id=COLLECTIVE_PAIR),
    )(*stacks)


def _exchange(arrays, sliced, name, relay=False):
    ex = _Exchange(arrays, sliced, relay_at=0 if relay else None)

    def body(*refs):
        ins, outs, sems = refs[:ex.n], refs[ex.n:2 * ex.n], refs[2 * ex.n:]
        ex.start(ins, outs, sems)
        if relay:
            ex.relay(ins, outs, sems)
            ex.relay_last(ins, outs, sems)
        ex.wait(ins, outs, sems)

    return pl.pallas_call(body, name=name, out_shape=ex.out_shape, in_specs=ex.specs, out_specs=ex.specs,
                          scratch_shapes=ex.scratch_shapes,
                          compiler_params=pltpu.CompilerParams(collective_id=ex.collective_id))(*ex.arrays)


def _call(body, *, name, grid, in_specs, out_specs, out_shape, args, scratch_shapes=(), exchange=None):
    params = _params(*(("arbitrary",) * len(grid)))
    if exchange is None:
        return pl.pallas_call(body, name=name, grid=grid, in_specs=in_specs, out_specs=out_specs, out_shape=out_shape,
                              scratch_shapes=list(scratch_shapes), compiler_params=params)(*args)
    exs = list(exchange) if isinstance(exchange, (list, tuple)) else [exchange]
    assert len(exs) == 1 or all(ex.collective_id is None for ex in exs)
    params = pltpu.CompilerParams(dimension_semantics=("arbitrary",) * len(grid), vmem_limit_bytes=VMEM_LIMIT_BYTES,
                                  collective_id=exs[0].collective_id)
    n_in, n_out, n_scr = len(in_specs), len(out_specs), len(scratch_shapes)
    n_ex = sum(ex.n for ex in exs)
    n_steps = 1
    for g in grid:
        n_steps *= g

    def hosted(*refs):
        ins, refs = refs[:n_in], refs[n_in:]
        ex_ins, refs = refs[:n_ex], refs[n_ex:]
        outs, refs = refs[:n_out], refs[n_out:]
        ex_outs, refs = refs[:n_ex], refs[n_ex:]
        scr, sems = refs[:n_scr], refs[n_scr:]
        parts, at = [], 0
        for j, ex in enumerate(exs):
            parts.append((ex_ins[at:at + ex.n], ex_outs[at:at + ex.n], sems[3 * j:3 * j + 3]))
            at += ex.n
        step = pl.program_id(0)
        for ax in range(1, len(grid)):
            step = step * grid[ax] + pl.program_id(ax)

        @pl.when(step == 0)
        def _():
            for ex, part in zip(exs, parts):
                ex.start(*part)

        body(*ins, *outs, *scr)

        for ex, part in zip(exs, parts):
            if ex.relay_at is not None:
                @pl.when(step == min(int(ex.relay_at * n_steps), n_steps - 1))
                def _():
                    ex.relay(*part)

                @pl.when(step == min(int((ex.relay_at + RELAY_LAST_LATER) * n_steps), n_steps - 1))
                def _():
                    ex.relay_last(*part)

        @pl.when(step == n_steps - 1)
        def _():
            for ex, part in zip(exs, parts):
                ex.wait(*part)

    return pl.pallas_call(
        hosted, name=name, grid=grid,
        in_specs=list(in_specs) + [sp for ex in exs for sp in ex.specs],
        out_specs=list(out_specs) + [sp for ex in exs for sp in ex.specs],
        out_shape=list(out_shape) + [sh for ex in exs for sh in ex.out_shape],
        scratch_shapes=list(scratch_shapes) + [sc for ex in exs for sc in ex.scratch_shapes],
        compiler_params=params)(*args, *[arr for ex in exs for arr in ex.arrays])


def _ffn_fwd(x, g, h, wgt, wut, wd, name, exchange=None):
    s_len, d = x.shape
    fc = wd.shape[0]
    ts = min(TOK_TILE, s_len)
    first = h is None

    def body(*refs):
        x_ref, gh_ref, wg_ref, wu_ref, wd_ref, xo_ref = refs[:6]
        a_ref, b_ref, s_ref = refs[-3:]
        xv = x_ref[...]
        if first:
            hb = (xv * _rms_scale(xv) * gh_ref[...]).astype(LOW)
            refs[6][...] = hb
        else:
            hb = gh_ref[...]
        for c0, c1 in _slabs(fc):
            a = _dot_nt(hb, wg_ref[c0:c1, :])
            b = _dot_nt(hb, wu_ref[c0:c1, :])
            s_ref[:, c0:c1] = (a * jax.nn.sigmoid(a) * b).astype(LOW)
            a_ref[:, c0:c1] = a.astype(LOW)
            b_ref[:, c0:c1] = b.astype(LOW)
        xo_ref[...] = xv + 0.5 * _dot(s_ref[...], wd_ref[...])

    tok = pl.BlockSpec((ts, d), lambda t: (t, 0))
    hid = pl.BlockSpec((ts, fc), lambda t: (t, 0))
    tok_out = jax.ShapeDtypeStruct((s_len, d), F32)
    h_out = jax.ShapeDtypeStruct((s_len, d), LOW)
    hid_out = jax.ShapeDtypeStruct((s_len, fc), LOW)
    return _call(
        body,
        name=name,
        grid=(s_len // ts,),
        in_specs=[tok, pl.BlockSpec((1, d), lambda t: (0, 0)) if first else tok,
                  _resident(wgt), _resident(wut), _resident(wd)],
        out_specs=[tok] + ([tok] if first else []) + [hid, hid],
        out_shape=[tok_out] + ([h_out] if first else []) + [hid_out, hid_out],
        scratch_shapes=[pltpu.VMEM((ts, fc), LOW)],
        args=(x, g if first else h, wgt, wut, wd),
        exchange=exchange,
    )


def _slabs(width, slab=256):
    return [(c0, min(c0 + slab, width)) for c0 in range(0, width, slab)]


def _ffn_bwdw(df, a, b, h, wd, name, exchange=None):
    s_len, d = df.shape
    f_len = wd.shape[0]
    tm = _pick(f_len, WGRAD_ROW_CANDIDATES)
    tk = min(WGRAD_TOK_TILE, s_len)
    n_k = s_len // tk

    def body(df_ref, a_ref, b_ref, h_ref, wd_ref, da_ref, db_ref, dwd_ref, dwg_ref, dwu_ref,
             s_ref, acc_d, acc_g, acc_u):
        k = pl.program_id(1)

        @pl.when(k == 0)
        def _():
            acc_d[...] = jnp.zeros_like(acc_d)
            acc_g[...] = jnp.zeros_like(acc_g)
            acc_u[...] = jnp.zeros_like(acc_u)

        wdv = wd_ref[...]
        for r0, r1 in _slabs(tk, BWD_ROW_SLAB):
            ds = _dot_nt(df_ref[r0:r1, :], wdv)
            av = a_ref[r0:r1, :].astype(F32)
            bv = b_ref[r0:r1, :].astype(F32)
            sig = jax.nn.sigmoid(av)
            silu = av * sig
            s_ref[r0:r1, :] = (silu * bv).astype(LOW)
            da_ref[r0:r1, :] = (ds * bv * (sig * (1.0 + av * (1.0 - sig)))).astype(LOW)
            db_ref[r0:r1, :] = (ds * silu).astype(LOW)
            hv = h_ref[r0:r1, :]
            acc_d[...] += _dot_tn(s_ref[r0:r1, :], df_ref[r0:r1, :])
            acc_g[...] += _dot_tn(da_ref[r0:r1, :], hv)
            acc_u[...] += _dot_tn(db_ref[r0:r1, :], hv)

        @pl.when(k == n_k - 1)
        def _():
            dwd_ref[...] = acc_d[...].astype(LOW)
            dwg_ref[...] = acc_g[...].astype(LOW)
            dwu_ref[...] = acc_u[...].astype(LOW)

    hid = pl.BlockSpec((tk, tm), lambda i, k: (k, i))
    tok = pl.BlockSpec((tk, d), lambda i, k: (k, 0))
    wrow = pl.BlockSpec((tm, d), lambda i, k: (i, 0))
    return _call(
        body,
        name=name,
        grid=(f_len // tm, n_k),
        in_specs=[tok, hid, hid, tok, wrow],
        out_specs=[hid, hid, wrow, wrow, wrow],
        out_shape=[jax.ShapeDtypeStruct((s_len, f_len), LOW)] * 2 + [jax.ShapeDtypeStruct((f_len, d), LOW)] * 3,
        scratch_shapes=[pltpu.VMEM((tk, tm), LOW)] + [pltpu.VMEM((tm, d), F32)] * 3,
        args=(df, a, b, h, wd),
        exchange=exchange,
    )


def _ffn_dx(dxo, x, g, parts, name, exchange=None):
    s_len, d = x.shape
    ts = min(TOK_TILE, s_len)
    n_p = len(parts)

    def body(dxo_ref, x_ref, g_ref, *refs):
        dxi_ref, dg_ref = refs[4 * n_p:]

        @pl.when(pl.program_id(0) == 0)
        def _():
            dg_ref[...] = jnp.zeros_like(dg_ref)

        dh = None
        for p in range(n_p):
            da_ref, db_ref, wg_ref, wu_ref = refs[4 * p:4 * p + 4]
            part = _dot(da_ref[...], wg_ref[...]) + _dot(db_ref[...], wu_ref[...])
            dh = part if dh is None else dh + part
        dx, dgp = _rms_bwd(dh, x_ref[...], g_ref[...])
        dxi_ref[...] = dxo_ref[...] + dx
        dg_ref[...] += dgp

    tok = pl.BlockSpec((ts, d), lambda t: (t, 0))
    vec = pl.BlockSpec((1, d), lambda t: (0, 0))
    part_specs, part_args = [], []
    for da, db, wgt, wut in parts:
        hid = pl.BlockSpec((ts, da.shape[1]), lambda t: (t, 0))
        part_specs += [hid, hid, _resident(wgt), _resident(wut)]
        part_args += [da, db, wgt, wut]
    return _call(
        body,
        name=name,
        grid=(s_len // ts,),
        in_specs=[tok, tok, vec] + part_specs,
        out_specs=[tok, vec],
        out_shape=[jax.ShapeDtypeStruct((s_len, d), F32), jax.ShapeDtypeStruct((1, d), F32)],
        args=(dxo, x, g, *part_args),
        exchange=exchange,
    )


def _wgrad_tn(xm, ym, tn, stacked, name, exchange=None):
    s_len, m = xm.shape
    n = ym.shape[1]
    tk = min(WGRAD_TOK_TILE, s_len)
    n_k = s_len // tk

    def body(x_ref, y_ref, o_ref, acc):
        k = pl.program_id(1)

        @pl.when(k == 0)
        def _():
            acc[...] = jnp.zeros_like(acc)

        acc[...] += _dot_tn(x_ref[...].astype(LOW), y_ref[...].astype(LOW))

        @pl.when(k == n_k - 1)
        def _():
            o_ref[...] = acc[...].astype(LOW)

    if stacked:
        out_spec = pl.BlockSpec((None, m, tn), lambda j, k: (j, 0, 0))
        out_shape = jax.ShapeDtypeStruct((n // tn, m, tn), LOW)
    else:
        out_spec = pl.BlockSpec((m, tn), lambda j, k: (0, j))
        out_shape = jax.ShapeDtypeStruct((m, n), LOW)
    return _call(
        body,
        name=name,
        grid=(n // tn, n_k),
        in_specs=[pl.BlockSpec((tk, m), lambda j, k: (k, 0)), pl.BlockSpec((tk, tn), lambda j, k: (k, j))],
        out_specs=[out_spec],
        out_shape=[out_shape],
        scratch_shapes=[pltpu.VMEM((m, tn), F32)],
        args=(xm, ym),
        exchange=exchange,
    )


def _mix_parts(ext_ref, cw, ts, row0):
    dc = D_CONV

    def back(off, c0, c1):
        return ext_ref[HALO - off:HALO - off + ts, c0:c1]

    v, gate_b, gate_c = back(0, 0, dc), back(0, dc, 2 * dc), back(0, 2 * dc, 3 * dc)
    z0 = gate_c * v
    z1 = back(1, 2 * dc, 3 * dc) * back(1, 0, dc)
    z2 = back(2, 2 * dc, 3 * dc) * back(2, 0, dc)
    conv = cw[2:3, :] * z0 + cw[1:2, :] * z1 + cw[0:1, :] * z2
    rows = row0 + lax.broadcasted_iota(jnp.int32, (ts, 1), 0)
    pooled, inv_count = [], []
    for grp, w in enumerate(POOL_WINDOWS):
        c0 = 3 * dc + POOL_GC * grp
        u = back(0, c0, c0 + POOL_GC)
        acc = u
        for j in range(1, w):
            acc = acc + back(j, c0, c0 + POOL_GC)
        inv = 1.0 / jnp.minimum(rows + 1, w).astype(F32)
        pooled.append(acc * inv - u)
        inv_count.append(inv)
    return v, gate_b, gate_c, z0, z1, z2, conv, pooled, inv_count


def _mix_fwd(x, g, w_in, conv_w, pool_w, pool_scale, w_out, name, exchange=None):
    s_len, d = x.shape
    n_blk, _, wcols = w_in.shape
    p_len = n_blk * wcols
    d_mix = w_out.shape[0]
    ts = min(MIX_TOK_TILE, s_len)
    dc = D_CONV

    def body(x_ref, g_ref, win_ref, cw_ref, pw_ref, ps_ref, wout_ref, x2_ref, h_ref, proj_ref, ext_ref, cat_ref):
        t = pl.program_id(0)

        @pl.when(t == 0)
        def _():
            ext_ref[0:HALO, :] = jnp.zeros((HALO, p_len), F32)

        xv = x_ref[...]
        hb = (xv * _rms_scale(xv) * g_ref[...]).astype(LOW)
        h_ref[...] = hb
        for k in range(n_blk):
            ext_ref[HALO:HALO + ts, wcols * k:wcols * (k + 1)] = _dot(hb, win_ref[k])
        proj_ref[...] = ext_ref[HALO:HALO + ts, :]

        _, gate_b, _, _, _, _, conv, pooled, _ = _mix_parts(ext_ref, cw_ref[...], ts, t * ts)
        cat_ref[:, 0:dc] = (gate_b * conv).astype(LOW)
        for grp in range(len(POOL_WINDOWS)):
            c0 = POOL_GC * grp
            lin = _dot(pooled[grp].astype(LOW), pw_ref[grp])
            cat_ref[:, dc + c0:dc + c0 + POOL_GC] = (lin * ps_ref[:, c0:c0 + POOL_GC]).astype(LOW)
        x2_ref[...] = xv + _dot(cat_ref[...], wout_ref[...])
        ext_ref[0:HALO, :] = ext_ref[ts:ts + HALO, :]

    tok = pl.BlockSpec((ts, d), lambda t: (t, 0))

    def whole(arr):
        return pl.BlockSpec(arr.shape, lambda t: (0,) * arr.ndim)

    return _call(
        body,
        name=name,
        grid=(s_len // ts,),
        in_specs=[tok, whole(g), _resident(w_in), whole(conv_w), whole(pool_w), whole(pool_scale), _resident(w_out)],
        out_specs=[tok, tok, pl.BlockSpec((ts, p_len), lambda t: (t, 0))],
        out_shape=[
            jax.ShapeDtypeStruct((s_len, d), F32),
            jax.ShapeDtypeStruct((s_len, d), LOW),
            jax.ShapeDtypeStruct((s_len, p_len), F32),
        ],
        scratch_shapes=[pltpu.VMEM((ts + HALO, p_len), F32), pltpu.VMEM((ts, d_mix), LOW)],
        args=(x, g, w_in, conv_w, pool_w, pool_scale, w_out),
        exchange=exchange,
    )


def _mix_bwd(dx2, x, proj, g, w_in, conv_w, pool_w, pool_scale, w_out, name, exchange=None):
    s_len, d = x.shape
    n_blk, _, wcols = w_in.shape
    p_len = n_blk * wcols
    d_mix = w_out.shape[0]
    ts = min(MIX_TOK_TILE, s_len)
    n_t = s_len // ts
    dc = D_CONV
    n_grp = len(POOL_WINDOWS)

    def body(dx2_ref, x_ref, proj_ref, halo_ref, g_ref, win_ref, cw_ref, pw_ref, ps_ref, wout_ref,
             dx_ref, dproj_ref, cat_ref, dg_ref, dcw_ref, dpw_ref, dps_ref, df_ref, ext_ref, fut_ref):
        i = pl.program_id(0)
        t = n_t - 1 - i

        @pl.when(i == 0)
        def _():
            dg_ref[...] = jnp.zeros_like(dg_ref)
            dcw_ref[...] = jnp.zeros_like(dcw_ref)
            dpw_ref[...] = jnp.zeros_like(dpw_ref)
            dps_ref[...] = jnp.zeros_like(dps_ref)
            fut_ref[ts:ts + HALO, :] = jnp.zeros((HALO, d_mix), F32)

        ext_ref[HALO:HALO + ts, :] = proj_ref[...]

        @pl.when(t == 0)
        def _():
            ext_ref[0:HALO, :] = jnp.zeros((HALO, p_len), F32)

        @pl.when(t > 0)
        def _():
            ext_ref[0:HALO, :] = halo_ref[...]

        cw = cw_ref[...]
        v, gate_b, gate_c, z0, z1, z2, conv, pooled, inv_count = _mix_parts(ext_ref, cw, ts, t * ts)
        dx2 = dx2_ref[...]
        dcat = _dot_nt(dx2.astype(LOW), wout_ref[...])

        dy_a = dcat[:, 0:dc]
        dconv = dy_a * gate_b
        fut_ref[0:ts, 0:dc] = dconv
        cat_ref[:, 0:dc] = (gate_b * conv).astype(LOW)
        dproj_ref[:, dc:2 * dc] = (dy_a * conv).astype(LOW)
        dcw_ref[2:3, :] += jnp.sum(dconv * z0, axis=0, keepdims=True)
        dcw_ref[1:2, :] += jnp.sum(dconv * z1, axis=0, keepdims=True)
        dcw_ref[0:1, :] += jnp.sum(dconv * z2, axis=0, keepdims=True)

        dpool = []
        for grp in range(n_grp):
            c0 = POOL_GC * grp
            pooled_b = pooled[grp].astype(LOW)
            lin = _dot(pooled_b, pw_ref[grp])
            dy_b = dcat[:, dc + c0:dc + c0 + POOL_GC]
            scale = ps_ref[:, c0:c0 + POOL_GC]
            cat_ref[:, dc + c0:dc + c0 + POOL_GC] = (lin * scale).astype(LOW)
            dps_ref[:, c0:c0 + POOL_GC] += jnp.sum(dy_b * lin, axis=0, keepdims=True)
            dlin = (dy_b * scale).astype(LOW)
            dpw_ref[grp] += _dot_tn(pooled_b, dlin)
            dpool.append(_dot_nt(dlin, pw_ref[grp]))
            fut_ref[0:ts, dc + c0:dc + c0 + POOL_GC] = dpool[grp] * inv_count[grp]

        def ahead(off, c0, c1):
            return fut_ref[off:off + ts, c0:c1]

        dz = cw[2:3, :] * ahead(0, 0, dc) + cw[1:2, :] * ahead(1, 0, dc) + cw[0:1, :] * ahead(2, 0, dc)
        dproj_ref[:, 0:dc] = (dz * gate_c).astype(LOW)
        dproj_ref[:, 2 * dc:3 * dc] = (dz * v).astype(LOW)
        for grp, w in enumerate(POOL_WINDOWS):
            c0 = dc + POOL_GC * grp
            acc = ahead(0, c0, c0 + POOL_GC)
            for j in range(1, w):
                acc = acc + ahead(j, c0, c0 + POOL_GC)
            dproj_ref[:, 2 * dc + c0:2 * dc + c0 + POOL_GC] = (acc - dpool[grp]).astype(LOW)

        dh = _dot_nt(dproj_ref[:, 0:wcols], win_ref[0])
        for k in range(1, n_blk):
            dh += _dot_nt(dproj_ref[:, wcols * k:wcols * (k + 1)], win_ref[k])
        dx, dgp = _rms_bwd(dh, x_ref[...], g_ref[...])
        dx = dx2 + dx
        dx_ref[...] = dx
        df_ref[...] = (0.5 * dx).astype(LOW)
        dg_ref[...] += dgp
        fut_ref[ts:ts + HALO, :] = fut_ref[0:HALO, :]

    tok = pl.BlockSpec((ts, d), lambda i: (n_t - 1 - i, 0))
    halo = pl.BlockSpec((HALO, p_len), lambda i: (jnp.maximum((n_t - 1 - i) * (ts // HALO) - 1, 0), 0))

    def whole(arr):
        return pl.BlockSpec(arr.shape, lambda i: (0,) * arr.ndim)

    return _call(
        body,
        name=name,
        grid=(n_t,),
        in_specs=[tok, tok, pl.BlockSpec((ts, p_len), lambda i: (n_t - 1 - i, 0)), halo,
                  whole(g), _resident(w_in), whole(conv_w), whole(pool_w), whole(pool_scale), _resident(w_out)],
        out_specs=[tok, pl.BlockSpec((ts, p_len), lambda i: (n_t - 1 - i, 0)),
                   pl.BlockSpec((ts, d_mix), lambda i: (n_t - 1 - i, 0)),
                   whole(g), whole(conv_w), whole(pool_w), whole(pool_scale), tok],
        out_shape=[
            jax.ShapeDtypeStruct((s_len, d), F32),
            jax.ShapeDtypeStruct((s_len, p_len), LOW),
            jax.ShapeDtypeStruct((s_len, d_mix), LOW),
            jax.ShapeDtypeStruct(g.shape, F32),
            jax.ShapeDtypeStruct(conv_w.shape, F32),
            jax.ShapeDtypeStruct(pool_w.shape, F32),
            jax.ShapeDtypeStruct(pool_scale.shape, F32),
            jax.ShapeDtypeStruct((s_len, d), LOW),
        ],
        scratch_shapes=[pltpu.VMEM((ts + HALO, p_len), F32), pltpu.VMEM((ts + HALO, d_mix), F32)],
        args=(dx2, x, proj, proj, g, w_in, conv_w, pool_w, pool_scale, w_out),
        exchange=exchange,
    )


def _ffn_fwd_loss(x, h, wgt, wut, wd, g, target, name):
    s_len, d = x.shape
    fc = wd.shape[0]
    ts = min(TOK_TILE, s_len)

    def body(x_ref, h_ref, wg_ref, wu_ref, wd_ref, g_ref, tgt_ref,
             a_ref, b_ref, loss_ref, dx_ref, dg_ref, df_ref, s_ref):
        @pl.when(pl.program_id(0) == 0)
        def _():
            loss_ref[...] = jnp.zeros_like(loss_ref)
            dg_ref[...] = jnp.zeros_like(dg_ref)

        hb = h_ref[...]
        for c0, c1 in _slabs(fc):
            a = _dot_nt(hb, wg_ref[c0:c1, :])
            b = _dot_nt(hb, wu_ref[c0:c1, :])
            s_ref[:, c0:c1] = (a * jax.nn.sigmoid(a) * b).astype(LOW)
            a_ref[:, c0:c1] = a.astype(LOW)
            b_ref[:, c0:c1] = b.astype(LOW)
        xv = x_ref[...] + 0.5 * _dot(s_ref[...], wd_ref[...])
        gv = g_ref[...]
        err = xv * _rms_scale(xv) * gv - tgt_ref[...]
        loss_ref[...] += 0.5 * jnp.sum(jnp.mean(err * err, axis=-1, keepdims=True), axis=0, keepdims=True)
        dx, dgp = _rms_bwd(err * (1.0 / d), xv, gv)
        dx_ref[...] = dx
        df_ref[...] = (0.5 * dx).astype(LOW)
        dg_ref[...] += dgp

    tok = pl.BlockSpec((ts, d), lambda t: (t, 0))
    vec = pl.BlockSpec((1, d), lambda t: (0, 0))
    hid = pl.BlockSpec((ts, fc), lambda t: (t, 0))
    return pl.pallas_call(
        body,
        name=name,
        grid=(s_len // ts,),
        in_specs=[tok, tok, _resident(wgt), _resident(wut), _resident(wd), vec, tok],
        out_specs=[hid, hid, pl.BlockSpec((1, 128), lambda t: (0, 0)), tok, vec, tok],
        out_shape=[
            jax.ShapeDtypeStruct((s_len, fc), LOW),
            jax.ShapeDtypeStruct((s_len, fc), LOW),
            jax.ShapeDtypeStruct((1, 128), F32),
            jax.ShapeDtypeStruct((s_len, d), F32),
            jax.ShapeDtypeStruct((1, d), F32),
            jax.ShapeDtypeStruct((s_len, d), LOW),
        ],
        scratch_shapes=[pltpu.VMEM((ts, fc), LOW)],
        compiler_params=_params("arbitrary"),
    )(x, h, wgt, wut, wd, g, target)


def _row_tile(rows, cols, stack_bytes):
    budget = 20 * 1024 * 1024
    per_row = cols * (4 * 7 + stack_bytes)
    for tr in (rows, 512, 256, 176, 128, 64, 32, 16, 8):
        if rows % tr == 0 and tr % 8 == 0 and tr * per_row * 2 <= budget:
            return tr
    return rows


def _sum_stack(stack, name):
    n, r, c = stack.shape
    tr = _row_tile(r, c, n * stack.dtype.itemsize)

    def body(s_ref, o_ref):
        acc = s_ref[0].astype(F32)
        for k in range(1, n):
            acc = acc + s_ref[k].astype(F32)
        o_ref[...] = acc

    return pl.pallas_call(
        body,
        name=name,
        grid=(r // tr,),
        in_specs=[pl.BlockSpec((n, tr, c), lambda i: (0, i, 0))],
        out_specs=pl.BlockSpec((tr, c), lambda i: (i, 0)),
        out_shape=jax.ShapeDtypeStruct((r, c), F32),
        compiler_params=_params("arbitrary"),
    )(stack)


def _adamw_many(params, name):
    params = [(list(st) if isinstance(st, (list, tuple)) else [st], w, m, v) for st, w, m, v in params]
    stacks0, w0 = params[0][0], params[0][1]
    r, c = w0.shape
    n = stacks0[0].shape[0]
    n_st = len(stacks0)
    part_rows = [st.shape[1] for st in stacks0]
    assert sum(part_rows) == r
    assert all(w.shape == (r, c) and [st.shape for st in sts] == [st.shape for st in stacks0] for sts, w, _, _ in params)
    first_row = [sum(part_rows[:j]) for j in range(n_st)]
    tc = next(t for t in (512, 256, 128) if c % t == 0)
    c1 = 1.0 - ADAM_B1 ** ADAM_STEP
    c2 = 1.0 - ADAM_B2 ** ADAM_STEP
    n_in = n_st + 3

    def body(*refs):
        ins, outs = refs[:n_in * len(params)], refs[n_in * len(params):]
        for p in range(len(params)):
            s_refs = ins[n_in * p:n_in * p + n_st]
            w_ref, m_ref, v_ref = ins[n_in * p + n_st:n_in * (p + 1)]
            g_ref, d_ref, mo_ref, vo_ref = outs[4 * p:4 * p + 4]
            for s_ref, r0, nr in zip(s_refs, first_row, part_rows):
                gv = s_ref[0].astype(F32)
                for k in range(1, n):
                    gv = gv + s_ref[k].astype(F32)
                mn = ADAM_B1 * m_ref[r0:r0 + nr, :] + (1.0 - ADAM_B1) * gv
                vn = ADAM_B2 * v_ref[r0:r0 + nr, :] + (1.0 - ADAM_B2) * (gv * gv)
                g_ref[r0:r0 + nr, :] = gv
                mo_ref[r0:r0 + nr, :] = mn
                vo_ref[r0:r0 + nr, :] = vn
                d_ref[r0:r0 + nr, :] = -ADAM_LR * ((mn / c1) / (jnp.sqrt(vn / c2) + ADAM_EPS)
                                                   + ADAM_WD * w_ref[r0:r0 + nr, :])

    blk = pl.BlockSpec((r, tc), lambda i: (0, i))
    one_in = [pl.BlockSpec((n, nr, tc), lambda i: (0, 0, i)) for nr in part_rows] + [blk, blk, blk]
    res = pl.pallas_call(
        body,
        name=name,
        grid=(c // tc,),
        in_specs=one_in * len(params),
        out_specs=[blk] * (4 * len(params)),
        out_shape=[jax.ShapeDtypeStruct((r, c), F32)] * (4 * len(params)),
        compiler_params=_params("arbitrary"),
    )(*[arr for sts, w, m, v in params for arr in (*sts, w, m, v)])
    return [tuple(res[4 * p:4 * p + 4]) for p in range(len(params))]


def _adamw(stacks, w, m, v, name):
    return _adamw_many([(stacks, w, m, v)], name)[0]


def _to_sheet(parts):
    sheets, spans = [], []
    row = 0
    for p in parts:
        flat = p.reshape(-1).astype(F32)
        rows = -(-flat.shape[0] // 1024) * 8
        flat = jnp.pad(flat, (0, rows * 128 - flat.shape[0]))
        sheets.append(flat.reshape(rows, 128))
        spans.append((row, p.size, p.shape))
        row += rows
    return jnp.concatenate(sheets, axis=0), spans


def _from_sheet(sheet, spans):
    out = []
    for row, size, shape in spans:
        rows = -(-size // 1024) * 8
        out.append(sheet[row:row + rows].reshape(-1)[:size].reshape(shape))
    return out


def kernel(x, norm_ffn1, ffn1_w_gate, ffn1_w_up, ffn1_w_down, norm_mix, w_in, conv_w, pool_w, pool_scale, w_out, norm_ffn2, ffn2_w_gate, ffn2_w_up, ffn2_w_down, norm_final, loss_target, m_norm_ffn1, m_ffn1_w_gate, m_ffn1_w_up, m_ffn1_w_down, m_norm_mix, m_w_in, m_conv_w, m_pool_w, m_pool_scale, m_w_out, m_norm_ffn2, m_ffn2_w_gate, m_ffn2_w_up, m_ffn2_w_down, m_norm_final, v_norm_ffn1, v_ffn1_w_gate, v_ffn1_w_up, v_ffn1_w_down, v_norm_mix, v_w_in, v_conv_w, v_pool_w, v_pool_scale, v_w_out, v_norm_ffn2, v_ffn2_w_gate, v_ffn2_w_up, v_ffn2_w_down, v_norm_final):
    me = 4 * lax.axis_index("x") + 2 * lax.axis_index("y") + lax.axis_index("c")
    xs, tgt = x[0], loss_target[0]
    s_len, d = xs.shape
    f_shard = ffn1_w_down.shape[1]
    conv_shard = conv_w.shape[2]

    def low_t(wt):
        return wt[0].T.astype(LOW)

    def by_dev(gw):
        return gw.reshape(N_DEV, -1, d)

    conv_tile = jnp.zeros((8, 128), F32).at[0:conv_w.shape[1], 0:conv_shard].set(conv_w[0])
    pool_w_low = pool_w[0].astype(LOW)

    rows_a = -(-f_shard // 64) * 32

    def parts_of(w_gate, w_up, w_down):
        shards = [low_t(w_gate), low_t(w_up), w_down[0].astype(LOW)]
        return [(s, 0, rows_a) for s in shards], [(s, rows_a, f_shard - rows_a) for s in shards]

    def rows_flat(stacks):
        return [st.reshape(-1, d) for st in stacks]

    def gather(shards):
        return _Exchange(shards, [False] * len(shards), relay_at=0.6)

    def scatter(pairs):
        return _Exchange(pairs, [True] * len(pairs), among_chips=True)

    w1a_shards, w1b_shards = parts_of(ffn1_w_gate, ffn1_w_up, ffn1_w_down)
    w2a_shards, w2b_shards = parts_of(ffn2_w_gate, ffn2_w_up, ffn2_w_down)

    wg1a, wu1a, wd1a = rows_flat(_exchange(w1a_shards, [False] * 3, "gather_ffn1_a", relay=True))
    xa, h1, a1a, b1a, *w1b = _ffn_fwd(xs, norm_ffn1, None, wg1a, wu1a, wd1a, "ffn1_fwd_a",
                                      exchange=gather(w1b_shards))
    wg1b, wu1b, wd1b = rows_flat(w1b)
    x1, a1b, b1b, w_in_full, w_out_full, conv_tiles = _ffn_fwd(
        xa, None, h1, wg1b, wu1b, wd1b, "ffn1_fwd_b",
        exchange=gather([w_in[0].astype(LOW), w_out[0].astype(LOW), conv_tile]))
    w_out_full = w_out_full.reshape(-1, d)
    conv_full = jnp.concatenate([conv_tiles[k, 0:conv_w.shape[1], 0:conv_shard] for k in range(N_DEV)], axis=1)
    x2, h2, proj, *w2a = _mix_fwd(x1, norm_mix, w_in_full, conv_full, pool_w_low, pool_scale, w_out_full, "mix_fwd",
                                  exchange=gather(w2a_shards))
    wg2a, wu2a, wd2a = rows_flat(w2a)
    xb, h3, a2a, b2a, *w2b = _ffn_fwd(x2, norm_ffn2, None, wg2a, wu2a, wd2a, "ffn2_fwd_a",
                                      exchange=gather(w2b_shards))
    wg2b, wu2b, wd2b = rows_flat(w2b)
    a2b, b2b, loss_row, dx3, dg_final, df3 = _ffn_fwd_loss(
        xb, h3, wg2b, wu2b, wd2b, norm_final.reshape(1, d), tgt, "ffn2_fwd_b_loss")

    da2a, db2a, dwd_a, dwg_a, dwu_a = _ffn_bwdw(df3, a2a, b2a, h3, wd2a, "ffn2_bwdw_a")
    da2b, db2b, dwd_b, dwg_b, dwu_b = _ffn_bwdw(df3, a2b, b2b, h3, wd2b, "ffn2_bwdw_b")
    pairs = _pair_sum([by_dev(dwg_a), by_dev(dwu_a), by_dev(dwd_a), by_dev(dwg_b), by_dev(dwu_b), by_dev(dwd_b)],
                      "pair_sum_ffn2")
    dx2, dg_ffn2, *got_2a = _ffn_dx(dx3, x2, norm_ffn2, [(da2a, db2a, wg2a, wu2a), (da2b, db2b, wg2b, wu2b)],
                                    "ffn2_dx", exchange=scatter(pairs[:3]))
    dx1, dproj, cat, dg_mix, dconv, dpool_w, dpool_scale, df1, *got_2b = _mix_bwd(
        dx2, x1, proj, norm_mix, w_in_full, conv_full, pool_w_low, pool_scale, w_out_full, "mix_bwd",
        exchange=scatter(pairs[3:]))
    small_parts = [dg_mix, dg_ffn2, dg_final, dconv, dpool_w, dpool_scale, loss_row]
    small_sheet, spans = _to_sheet(small_parts)
    dw_in, got_small = _wgrad_tn(h2, dproj, W_IN_SHARD, True, "w_in_wgrad",
                                 exchange=_Exchange([small_sheet], [False], relay_at=0.5))
    (dw_out,) = _wgrad_tn(cat, dx2, d, False, "w_out_wgrad")
    pairs = _pair_sum([dw_in, by_dev(dw_out)], "pair_sum_mix")
    da1a, db1a, dwd, dwg, dwu, got_in, got_out = _ffn_bwdw(
        df1, a1a, b1a, h1, wd1a, "ffn1_bwdw_a", exchange=scatter(pairs))
    pairs = _pair_sum([by_dev(dwg), by_dev(dwu), by_dev(dwd)], "pair_sum_ffn1_a")
    da1b, db1b, dwd, dwg, dwu, *got_1a = _ffn_bwdw(df1, a1b, b1b, h1, wd1b, "ffn1_bwdw_b", exchange=scatter(pairs))
    pairs = _pair_sum([by_dev(dwg), by_dev(dwu), by_dev(dwd)], "pair_sum_ffn1_b")
    dx0, dg_ffn1, *got_1b = _ffn_dx(dx1, xs, norm_ffn1, [(da1a, db1a, wg1a, wu1a), (da1b, db1b, wg1b, wu1b)],
                                    "ffn1_dx", exchange=scatter(pairs))
    (got_n1,) = _exchange([dg_ffn1.reshape(8, 128)], [False], "gather_dnorm_ffn1")

    outs = {}

    def update(name, stacks, w, m, v):
        outs[name] = _adamw(stacks, w[0], m[0], v[0], "adamw_" + name)

    def update_ffn(prefix, got_a, got_b, gate, up, down):
        res = _adamw_many(
            [([got_a[j], got_b[j]], *[(t[0].T if j < 2 else t[0]) for t in wmv]) for j, wmv in enumerate((gate, up, down))],
            "adamw_" + prefix)
        outs[prefix + "_w_gate"] = tuple(r.T for r in res[0])
        outs[prefix + "_w_up"] = tuple(r.T for r in res[1])
        outs[prefix + "_w_down"] = res[2]

    update_ffn("ffn1", got_1a, got_1b, (ffn1_w_gate, m_ffn1_w_gate, v_ffn1_w_gate),
               (ffn1_w_up, m_ffn1_w_up, v_ffn1_w_up), (ffn1_w_down, m_ffn1_w_down, v_ffn1_w_down))
    update_ffn("ffn2", got_2a, got_2b, (ffn2_w_gate, m_ffn2_w_gate, v_ffn2_w_gate),
               (ffn2_w_up, m_ffn2_w_up, v_ffn2_w_up), (ffn2_w_down, m_ffn2_w_down, v_ffn2_w_down))
    update("w_in", got_in, w_in, m_w_in, v_w_in)
    update("w_out", got_out, w_out, m_w_out, v_w_out)

    g_small = _from_sheet(_sum_stack(got_small, "sum_small"), spans)
    g_norm_ffn1 = _sum_stack(got_n1, "sum_dnorm_ffn1").reshape(norm_ffn1.shape)
    g_conv = lax.dynamic_slice_in_dim(g_small[3], me * conv_shard, conv_shard, axis=1)
    small_names = ["norm_ffn1", "norm_mix", "norm_ffn2", "norm_final", "conv_w", "pool_w", "pool_scale"]
    small_g = [g_norm_ffn1, g_small[0], g_small[1], g_small[2].reshape(norm_final.shape), g_conv[None],
               g_small[4][None], g_small[5]]
    small_w = [norm_ffn1, norm_mix, norm_ffn2, norm_final, conv_w, pool_w, pool_scale]
    small_m = [m_norm_ffn1, m_norm_mix, m_norm_ffn2, m_norm_final, m_conv_w, m_pool_w, m_pool_scale]
    small_v = [v_norm_ffn1, v_norm_mix, v_norm_ffn2, v_norm_final, v_conv_w, v_pool_w, v_pool_scale]
    g_sheet, spans_u = _to_sheet(small_g)
    w_sheet, _ = _to_sheet(small_w)
    m_sheet, _ = _to_sheet(small_m)
    v_sheet, _ = _to_sheet(small_v)
    upd = _adamw(g_sheet[None], w_sheet, m_sheet, v_sheet, "adamw_small")
    small_out = [_from_sheet(u, spans_u) for u in upd]
    for k, nm in enumerate(small_names):
        outs[nm] = tuple(small_out[j][k] for j in range(4))

    loss = g_small[6][0, 0]
    order = ["norm_ffn1", "ffn1_w_gate", "ffn1_w_up", "ffn1_w_down", "norm_mix", "w_in", "conv_w", "pool_w",
             "pool_scale", "w_out", "norm_ffn2", "ffn2_w_gate", "ffn2_w_up", "ffn2_w_down", "norm_final"]
    big = {"ffn1_w_gate", "ffn1_w_up", "ffn1_w_down", "w_in", "w_out", "ffn2_w_gate", "ffn2_w_up", "ffn2_w_down"}

    def leaf(nm, j):
        val = outs[nm][j]
        return val[None] if nm in big else val

    return (loss, dx0[None],
            *[leaf(nm, 0) for nm in order], *[leaf(nm, 1) for nm in order],
            *[leaf(nm, 2) for nm in order], *[leaf(nm, 3) for nm in order])
```

```python
import jax
import jax.numpy as jnp
from jax import lax
from jax.experimental import pallas as pl
from jax.experimental.pallas import tpu as pltpu

F32 = jnp.float32
LOW = jnp.bfloat16

N_DEV = 8
EPS = 1e-6
D_CONV = 512
POOL_WINDOWS = (2, 4, 8, 16)
POOL_GC = 128
HALO = 16
W_IN_SHARD = 256

ADAM_LR = 0.001
ADAM_B1 = 0.9
ADAM_B2 = 0.999
ADAM_EPS = 1e-08
ADAM_WD = 0.01
ADAM_STEP = 10

VMEM_LIMIT_BYTES = 56 * 1024 * 1024
TOK_TILE = 512
MIX_TOK_TILE = 256
WGRAD_TOK_TILE = 4096
WGRAD_ROW_CANDIDATES = (256, 128)
RELAY_LAST_LATER = 0.25
BWD_ROW_SLAB = 2048


def _params(*sem):
    return pltpu.CompilerParams(dimension_semantics=sem, vmem_limit_bytes=VMEM_LIMIT_BYTES)


def _resident(arr):
    return pl.BlockSpec(arr.shape, lambda *_: (0,) * arr.ndim, pipeline_mode=pl.Buffered(1))


def _pick(n, candidates):
    for c in candidates:
        if n % c == 0:
            return c
    raise ValueError(f"no tile in {candidates} divides {n}")


def _dot(a, b):
    return lax.dot_general(a, b, (((1,), (0,)), ((), ())), preferred_element_type=F32)


def _dot_nt(a, b):
    return lax.dot_general(a, b, (((1,), (1,)), ((), ())), preferred_element_type=F32)


def _dot_tn(a, b):
    return lax.dot_general(a, b, (((0,), (0,)), ((), ())), preferred_element_type=F32)


def _rms_scale(x):
    return lax.rsqrt(jnp.mean(x * x, axis=-1, keepdims=True) + EPS)


def _rms_bwd(dy, x, g):
    r = _rms_scale(x)
    xhat = x * r
    gdy = dy * g
    dx = r * (gdy - xhat * jnp.mean(gdy * xhat, axis=-1, keepdims=True))
    return dx, jnp.sum(dy * xhat, axis=0, keepdims=True)


COLLECTIVE_PAIR, COLLECTIVE_CHIPS, COLLECTIVE_RELAY = 0, 1, 2


def _handshake(peer_numbers):
    mx, my, mc = lax.axis_index("x"), lax.axis_index("y"), lax.axis_index("c")
    barrier = pltpu.get_barrier_semaphore()
    for m in peer_numbers:
        peer = (lax.rem(mx + ((m >> 2) & 1), 2), lax.rem(my + ((m >> 1) & 1), 2), lax.rem(mc + (m & 1), 2))
        pl.semaphore_signal(barrier, inc=1, device_id=peer, device_id_type=pl.DeviceIdType.MESH)
    pl.semaphore_wait(barrier, len(peer_numbers))


class _Exchange:
    CHIPS = (2, 4, 6)

    def __init__(self, arrays, sliced, relay_at=None, among_chips=False):
        assert relay_at is None or not any(sliced)
        assert not among_chips or (all(sliced) and relay_at is None)
        self.relay_at, self.among_chips = relay_at, among_chips
        self.peers = self.CHIPS if among_chips else ((1, 2, 4) if relay_at is not None else tuple(range(1, N_DEV)))
        self.collective_id = COLLECTIVE_CHIPS if among_chips else (COLLECTIVE_RELAY if relay_at is not None else None)
        self.rows = [arr[1:] if isinstance(arr, tuple) else None for arr in arrays]
        self.arrays = [arr[0] if isinstance(arr, tuple) else arr for arr in arrays]
        self.sliced, self.n = list(sliced), len(arrays)
        assert all(rg is None or not sl for rg, sl in zip(self.rows, sliced))
        self.block_shape = [arr.shape if rg is None else (rg[1],) + arr.shape[1:]
                            for arr, rg in zip(self.arrays, self.rows)]
        self.out_shape = [jax.ShapeDtypeStruct(shape if sl else (N_DEV,) + shape, arr.dtype)
                          for arr, shape, sl in zip(self.arrays, self.block_shape, sliced)]
        self.specs = [pl.BlockSpec(memory_space=pl.ANY)] * self.n
        self.scratch_shapes = [pltpu.SemaphoreType.DMA((self.n, N_DEV)),
                               pltpu.SemaphoreType.DMA((self.n, N_DEV)),
                               pltpu.SemaphoreType.DMA((self.n,))]

    HALF_VIA = ((4, 2, 5), (2, 4, 7))

    def _halves(self, a):
        rows = self.block_shape[a][0]
        if rows % 32:
            return ((0, rows), None)
        return ((0, rows // 2), (rows // 2, rows // 2))

    def _copies(self, ins, outs, sems):
        send_sems, recv_sems, local_sems = sems
        sliced = self.sliced
        mx, my, mc = lax.axis_index("x"), lax.axis_index("y"), lax.axis_index("c")
        me = 2 * mx + my if self.among_chips else 4 * mx + 2 * my + mc

        def peer(m):
            px = lax.rem(mx + ((m >> 2) & 1), 2)
            py = lax.rem(my + ((m >> 1) & 1), 2)
            pc = lax.rem(mc + (m & 1), 2)
            return (px, py, pc), (2 * px + py if self.among_chips else 4 * px + 2 * py + pc)

        def mine(a):
            return ins[a] if self.rows[a] is None else ins[a].at[pl.ds(*self.rows[a])]

        def remote(a, m, arriving):
            pid, pflat = peer(m)
            return pltpu.make_async_remote_copy(
                src_ref=ins[a].at[pflat] if sliced[a] else mine(a),
                dst_ref=outs[a].at[pflat if arriving else me],
                send_sem=send_sems.at[a, m - 1],
                recv_sem=recv_sems.at[a, m - 1],
                device_id=pid,
                device_id_type=pl.DeviceIdType.MESH,
            )

        def local(a):
            return pltpu.make_async_copy(ins[a].at[me] if sliced[a] else mine(a), outs[a].at[me], local_sems.at[a])

        def passed_on(a, m):
            _, origin = peer(m)
            sibling, _ = peer(1)
            return pltpu.make_async_remote_copy(
                src_ref=outs[a].at[origin],
                dst_ref=outs[a].at[origin],
                send_sem=send_sems.at[a, m],
                recv_sem=recv_sems.at[a, m],
                device_id=sibling,
                device_id_type=pl.DeviceIdType.MESH,
            )

        def half_on(a, h, arriving):
            via, to, column = self.HALF_VIA[h]
            r0, nr = self._halves(a)[h]
            _, origin = peer(6 if arriving else via)
            rows = outs[a].at[origin].at[pl.ds(r0, nr)]
            return pltpu.make_async_remote_copy(
                src_ref=rows, dst_ref=rows, send_sem=send_sems.at[a, column], recv_sem=recv_sems.at[a, column],
                device_id=peer(to)[0], device_id_type=pl.DeviceIdType.MESH)

        return remote, local, passed_on, half_on

    def start(self, ins, outs, sems):
        remote, local, _, _ = self._copies(ins, outs, sems)
        if self.collective_id is not None:
            _handshake(self.peers)
        for a in range(self.n):
            local(a).start()
        for m in self.peers:
            for a in range(self.n):
                remote(a, m, False).start()

    def relay(self, ins, outs, sems):
        remote, _, passed_on, half_on = self._copies(ins, outs, sems)
        for h, (via, _, _) in enumerate(self.HALF_VIA):
            for a in range(self.n):
                remote(a, via, True).wait_recv()
                passed_on(a, via).start()
                if self._halves(a)[h] is not None:
                    half_on(a, h, False).start()

    def relay_last(self, ins, outs, sems):
        _, _, passed_on, half_on = self._copies(ins, outs, sems)
        for a in range(self.n):
            for h in range(2):
                if self._halves(a)[h] is not None:
                    half_on(a, h, True).wait_recv()
            passed_on(a, 6).start()

    def wait(self, ins, outs, sems):
        remote, local, passed_on, half_on = self._copies(ins, outs, sems)
        if self.relay_at is None:
            for m in self.peers:
                for a in range(self.n):
                    remote(a, m, True).wait_recv()
            for m in self.peers:
                for a in range(self.n):
                    remote(a, m, False).wait_send()
        else:
            for m in (1, 3, 5, 7):
                for a in range(self.n):
                    remote(a, m, True).wait_recv()
            for m in self.peers:
                for a in range(self.n):
                    remote(a, m, False).wait_send()
            for a in range(self.n):
                for m in self.CHIPS:
                    passed_on(a, m).wait_send()
                for h in range(2):
                    if self._halves(a)[h] is not None:
                        half_on(a, h, False).wait_send()
        for a in range(self.n):
            local(a).wait()


def _pair_sum(stacks, name):
    n = len(stacks)
    n_chip = N_DEV // 2
    half = [(n_chip,) + st.shape[1:] for st in stacks]

    def body(*refs):
        ins, outs, mine, theirs = refs[:n], refs[n:2 * n], refs[2 * n:3 * n], refs[3 * n:4 * n]
        local_sems, send_sems, recv_sems = refs[4 * n:]
        mx, my, mc = lax.axis_index("x"), lax.axis_index("y"), lax.axis_index("c")

        def own(a, k):
            return pltpu.make_async_copy(ins[a].at[2 * k + mc], mine[a].at[k], local_sems.at[a, k])

        def swap(a, k):
            return pltpu.make_async_remote_copy(
                src_ref=ins[a].at[2 * k + (1 - mc)], dst_ref=theirs[a].at[k],
                send_sem=send_sems.at[a, k], recv_sem=recv_sems.at[a, k],
                device_id=(mx, my, 1 - mc), device_id_type=pl.DeviceIdType.MESH)

        _handshake((1,))
        for k in range(n_chip):
            for a in range(n):
                own(a, k).start()
                swap(a, k).start()
        for k in range(n_chip):
            for a in range(n):
                own(a, k).wait()
                swap(a, k).wait()
                outs[a][k] = (mine[a][k].astype(F32) + theirs[a][k].astype(F32)).astype(LOW)

    return pl.pallas_call(
        body, name=name,
        out_shape=[jax.ShapeDtypeStruct(h, LOW) for h in half],
        in_specs=[pl.BlockSpec(memory_space=pl.ANY)] * n,
        out_specs=[pl.BlockSpec(memory_space=pltpu.VMEM)] * n,
        scratch_shapes=([pltpu.VMEM(h, st.dtype) for h, st in zip(half, stacks)] * 2
                        + [pltpu.SemaphoreType.DMA((n, n_chip))] * 3),
        compiler_params=pltpu.CompilerParams(vmem_limit_bytes=VMEM_LIMIT_BYTES, collective_id=COLLECTIVE_PAIR),
    )(*stacks)


def _exchange(arrays, sliced, name, relay=False):
    ex = _Exchange(arrays, sliced, relay_at=0 if relay else None)

    def body(*refs):
        ins, outs, sems = refs[:ex.n], refs[ex.n:2 * ex.n], refs[2 * ex.n:]
        ex.start(ins, outs, sems)
        if relay:
            ex.relay(ins, outs, sems)
            ex.relay_last(ins, outs, sems)
        ex.wait(ins, outs, sems)

    return pl.pallas_call(body, name=name, out_shape=ex.out_shape, in_specs=ex.specs, out_specs=ex.specs,
                          scratch_shapes=ex.scratch_shapes,
                          compiler_params=pltpu.CompilerParams(collective_id=ex.collective_id))(*ex.arrays)


def _call(body, *, name, grid, in_specs, out_specs, out_shape, args, scratch_shapes=(), exchange=None):
    params = _params(*(("arbitrary",) * len(grid)))
    if exchange is None:
        return pl.pallas_call(body, name=name, grid=grid, in_specs=in_specs, out_specs=out_specs, out_shape=out_shape,
                              scratch_shapes=list(scratch_shapes), compiler_params=params)(*args)
    exs = list(exchange) if isinstance(exchange, (list, tuple)) else [exchange]
    assert len(exs) == 1 or all(ex.collective_id is None for ex in exs)
    params = pltpu.CompilerParams(dimension_semantics=("arbitrary",) * len(grid), vmem_limit_bytes=VMEM_LIMIT_BYTES,
                                  collective_id=exs[0].collective_id)
    n_in, n_out, n_scr = len(in_specs), len(out_specs), len(scratch_shapes)
    n_ex = sum(ex.n for ex in exs)
    n_steps = 1
    for g in grid:
        n_steps *= g

    def hosted(*refs):
        ins, refs = refs[:n_in], refs[n_in:]
        ex_ins, refs = refs[:n_ex], refs[n_ex:]
        outs, refs = refs[:n_out], refs[n_out:]
        ex_outs, refs = refs[:n_ex], refs[n_ex:]
        scr, sems = refs[:n_scr], refs[n_scr:]
        parts, at = [], 0
        for j, ex in enumerate(exs):
            parts.append((ex_ins[at:at + ex.n], ex_outs[at:at + ex.n], sems[3 * j:3 * j + 3]))
            at += ex.n
        step = pl.program_id(0)
        for ax in range(1, len(grid)):
            step = step * grid[ax] + pl.program_id(ax)

        @pl.when(step == 0)
        def _():
            for ex, part in zip(exs, parts):
                ex.start(*part)

        body(*ins, *outs, *scr)

        for ex, part in zip(exs, parts):
            if ex.relay_at is not None:
                @pl.when(step == min(int(ex.relay_at * n_steps), n_steps - 1))
                def _():
                    ex.relay(*part)

                @pl.when(step == min(int((ex.relay_at + RELAY_LAST_LATER) * n_steps), n_steps - 1))
                def _():
                    ex.relay_last(*part)

        @pl.when(step == n_steps - 1)
        def _():
            for ex, part in zip(exs, parts):
                ex.wait(*part)

    return pl.pallas_call(
        hosted, name=name, grid=grid,
        in_specs=list(in_specs) + [sp for ex in exs for sp in ex.specs],
        out_specs=list(out_specs) + [sp for ex in exs for sp in ex.specs],
        out_shape=list(out_shape) + [sh for ex in exs for sh in ex.out_shape],
        scratch_shapes=list(scratch_shapes) + [sc for ex in exs for sc in ex.scratch_shapes],
        compiler_params=params)(*args, *[arr for ex in exs for arr in ex.arrays])


def _ffn_fwd(x, g, h, wgt, wut, wd, name, exchange=None):
    s_len, d = x.shape
    fc = wd.shape[0]
    ts = min(TOK_TILE, s_len)
    first = h is None

    def body(*refs):
        x_ref, gh_ref, wg_ref, wu_ref, wd_ref, xo_ref = refs[:6]
        a_ref, b_ref, s_ref = refs[-3:]
        xv = x_ref[...]
        if first:
            hb = (xv * _rms_scale(xv) * gh_ref[...]).astype(LOW)
            refs[6][...] = hb
        else:
            hb = gh_ref[...]
        for c0, c1 in _slabs(fc):
            a = _dot_nt(hb, wg_ref[c0:c1, :])
            b = _dot_nt(hb, wu_ref[c0:c1, :])
            s_ref[:, c0:c1] = (a * jax.nn.sigmoid(a) * b).astype(LOW)
            a_ref[:, c0:c1] = a.astype(LOW)
            b_ref[:, c0:c1] = b.astype(LOW)
        xo_ref[...] = xv + 0.5 * _dot(s_ref[...], wd_ref[...])

    tok = pl.BlockSpec((ts, d), lambda t: (t, 0))
    hid = pl.BlockSpec((ts, fc), lambda t: (t, 0))
    tok_out = jax.ShapeDtypeStruct((s_len, d), F32)
    h_out = jax.ShapeDtypeStruct((s_len, d), LOW)
    hid_out = jax.ShapeDtypeStruct((s_len, fc), LOW)
    return _call(
        body,
        name=name,
        grid=(s_len // ts,),
        in_specs=[tok, pl.BlockSpec((1, d), lambda t: (0, 0)) if first else tok,
                  _resident(wgt), _resident(wut), _resident(wd)],
        out_specs=[tok] + ([tok] if first else []) + [hid, hid],
        out_shape=[tok_out] + ([h_out] if first else []) + [hid_out, hid_out],
        scratch_shapes=[pltpu.VMEM((ts, fc), LOW)],
        args=(x, g if first else h, wgt, wut, wd),
        exchange=exchange,
    )


def _slabs(width, slab=256):
    return [(c0, min(c0 + slab, width)) for c0 in range(0, width, slab)]


def _ffn_bwdw(df, a, b, h, wd, name, exchange=None):
    s_len, d = df.shape
    f_len = wd.shape[0]
    tm = _pick(f_len, WGRAD_ROW_CANDIDATES)
    tk = min(WGRAD_TOK_TILE, s_len)
    n_k = s_len // tk

    def body(df_ref, a_ref, b_ref, h_ref, wd_ref, da_ref, db_ref, dwd_ref, dwg_ref, dwu_ref,
             s_ref, acc_d, acc_g, acc_u):
        k = pl.program_id(1)

        @pl.when(k == 0)
        def _():
            acc_d[...] = jnp.zeros_like(acc_d)
            acc_g[...] = jnp.zeros_like(acc_g)
            acc_u[...] = jnp.zeros_like(acc_u)

        wdv = wd_ref[...]
        for r0, r1 in _slabs(tk, BWD_ROW_SLAB):
            ds = _dot_nt(df_ref[r0:r1, :], wdv)
            av = a_ref[r0:r1, :].astype(F32)
            bv = b_ref[r0:r1, :].astype(F32)
            sig = jax.nn.sigmoid(av)
            silu = av * sig
            s_ref[r0:r1, :] = (silu * bv).astype(LOW)
            da_ref[r0:r1, :] = (ds * bv * (sig * (1.0 + av * (1.0 - sig)))).astype(LOW)
            db_ref[r0:r1, :] = (ds * silu).astype(LOW)
            hv = h_ref[r0:r1, :]
            acc_d[...] += _dot_tn(s_ref[r0:r1, :], df_ref[r0:r1, :])
            acc_g[...] += _dot_tn(da_ref[r0:r1, :], hv)
            acc_u[...] += _dot_tn(db_ref[r0:r1, :], hv)

        @pl.when(k == n_k - 1)
        def _():
            dwd_ref[...] = acc_d[...].astype(LOW)
            dwg_ref[...] = acc_g[...].astype(LOW)
            dwu_ref[...] = acc_u[...].astype(LOW)

    hid = pl.BlockSpec((tk, tm), lambda i, k: (k, i))
    tok = pl.BlockSpec((tk, d), lambda i, k: (k, 0))
    wrow = pl.BlockSpec((tm, d), lambda i, k: (i, 0))
    return _call(
        body,
        name=name,
        grid=(f_len // tm, n_k),
        in_specs=[tok, hid, hid, tok, wrow],
        out_specs=[hid, hid, wrow, wrow, wrow],
        out_shape=[jax.ShapeDtypeStruct((s_len, f_len), LOW)] * 2 + [jax.ShapeDtypeStruct((f_len, d), LOW)] * 3,
        scratch_shapes=[pltpu.VMEM((tk, tm), LOW)] + [pltpu.VMEM((tm, d), F32)] * 3,
        args=(df, a, b, h, wd),
        exchange=exchange,
    )


def _ffn_dx(dxo, x, g, parts, name, exchange=None):
    s_len, d = x.shape
    ts = min(TOK_TILE, s_len)
    n_p = len(parts)

    def body(dxo_ref, x_ref, g_ref, *refs):
        dxi_ref, dg_ref = refs[4 * n_p:]

        @pl.when(pl.program_id(0) == 0)
        def _():
            dg_ref[...] = jnp.zeros_like(dg_ref)

        dh = None
        for p in range(n_p):
            da_ref, db_ref, wg_ref, wu_ref = refs[4 * p:4 * p + 4]
            part = _dot(da_ref[...], wg_ref[...]) + _dot(db_ref[...], wu_ref[...])
            dh = part if dh is None else dh + part
        dx, dgp = _rms_bwd(dh, x_ref[...], g_ref[...])
        dxi_ref[...] = dxo_ref[...] + dx
        dg_ref[...] += dgp

    tok = pl.BlockSpec((ts, d), lambda t: (t, 0))
    vec = pl.BlockSpec((1, d), lambda t: (0, 0))
    part_specs, part_args = [], []
    for da, db, wgt, wut in parts:
        hid = pl.BlockSpec((ts, da.shape[1]), lambda t: (t, 0))
        part_specs += [hid, hid, _resident(wgt), _resident(wut)]
        part_args += [da, db, wgt, wut]
    return _call(
        body,
        name=name,
        grid=(s_len // ts,),
        in_specs=[tok, tok, vec] + part_specs,
        out_specs=[tok, vec],
        out_shape=[jax.ShapeDtypeStruct((s_len, d), F32), jax.ShapeDtypeStruct((1, d), F32)],
        args=(dxo, x, g, *part_args),
        exchange=exchange,
    )


def _wgrad_tn(xm, ym, tn, stacked, name, exchange=None):
    s_len, m = xm.shape
    n = ym.shape[1]
    tk = min(WGRAD_TOK_TILE, s_len)
    n_k = s_len // tk

    def body(x_ref, y_ref, o_ref, acc):
        k = pl.program_id(1)

        @pl.when(k == 0)
        def _():
            acc[...] = jnp.zeros_like(acc)

        acc[...] += _dot_tn(x_ref[...].astype(LOW), y_ref[...].astype(LOW))

        @pl.when(k == n_k - 1)
        def _():
            o_ref[...] = acc[...].astype(LOW)

    if stacked:
        out_spec = pl.BlockSpec((None, m, tn), lambda j, k: (j, 0, 0))
        out_shape = jax.ShapeDtypeStruct((n // tn, m, tn), LOW)
    else:
        out_spec = pl.BlockSpec((m, tn), lambda j, k: (0, j))
        out_shape = jax.ShapeDtypeStruct((m, n), LOW)
    return _call(
        body,
        name=name,
        grid=(n // tn, n_k),
        in_specs=[pl.BlockSpec((tk, m), lambda j, k: (k, 0)), pl.BlockSpec((tk, tn), lambda j, k: (k, j))],
        out_specs=[out_spec],
        out_shape=[out_shape],
        scratch_shapes=[pltpu.VMEM((m, tn), F32)],
        args=(xm, ym),
        exchange=exchange,
    )


def _mix_parts(ext_ref, cw, ts, row0):
    dc = D_CONV

    def back(off, c0, c1):
        return ext_ref[HALO - off:HALO - off + ts, c0:c1]

    v, gate_b, gate_c = back(0, 0, dc), back(0, dc, 2 * dc), back(0, 2 * dc, 3 * dc)
    z0 = gate_c * v
    z1 = back(1, 2 * dc, 3 * dc) * back(1, 0, dc)
    z2 = back(2, 2 * dc, 3 * dc) * back(2, 0, dc)
    conv = cw[2:3, :] * z0 + cw[1:2, :] * z1 + cw[0:1, :] * z2
    rows = row0 + lax.broadcasted_iota(jnp.int32, (ts, 1), 0)
    pooled, inv_count = [], []
    for grp, w in enumerate(POOL_WINDOWS):
        c0 = 3 * dc + POOL_GC * grp
        u = back(0, c0, c0 + POOL_GC)
        acc = u
        for j in range(1, w):
            acc = acc + back(j, c0, c0 + POOL_GC)
        inv = 1.0 / jnp.minimum(rows + 1, w).astype(F32)
        pooled.append(acc * inv - u)
        inv_count.append(inv)
    return v, gate_b, gate_c, z0, z1, z2, conv, pooled, inv_count


def _mix_fwd(x, g, w_in, conv_w, pool_w, pool_scale, w_out, name, exchange=None):
    s_len, d = x.shape
    n_blk, _, wcols = w_in.shape
    p_len = n_blk * wcols
    d_mix = w_out.shape[0]
    ts = min(MIX_TOK_TILE, s_len)
    dc = D_CONV

    def body(x_ref, g_ref, win_ref, cw_ref, pw_ref, ps_ref, wout_ref, x2_ref, h_ref, proj_ref, ext_ref, cat_ref):
        t = pl.program_id(0)

        @pl.when(t == 0)
        def _():
            ext_ref[0:HALO, :] = jnp.zeros((HALO, p_len), F32)

        xv = x_ref[...]
        hb = (xv * _rms_scale(xv) * g_ref[...]).astype(LOW)
        h_ref[...] = hb
        for k in range(n_blk):
            ext_ref[HALO:HALO + ts, wcols * k:wcols * (k + 1)] = _dot(hb, win_ref[k])
        proj_ref[...] = ext_ref[HALO:HALO + ts, :]

        _, gate_b, _, _, _, _, conv, pooled, _ = _mix_parts(ext_ref, cw_ref[...], ts, t * ts)
        cat_ref[:, 0:dc] = (gate_b * conv).astype(LOW)
        for grp in range(len(POOL_WINDOWS)):
            c0 = POOL_GC * grp
            lin = _dot(pooled[grp].astype(LOW), pw_ref[grp])
            cat_ref[:, dc + c0:dc + c0 + POOL_GC] = (lin * ps_ref[:, c0:c0 + POOL_GC]).astype(LOW)
        x2_ref[...] = xv + _dot(cat_ref[...], wout_ref[...])
        ext_ref[0:HALO, :] = ext_ref[ts:ts + HALO, :]

    tok = pl.BlockSpec((ts, d), lambda t: (t, 0))

    def whole(arr):
        return pl.BlockSpec(arr.shape, lambda t: (0,) * arr.ndim)

    return _call(
        body,
        name=name,
        grid=(s_len // ts,),
        in_specs=[tok, whole(g), _resident(w_in), whole(conv_w), whole(pool_w), whole(pool_scale), _resident(w_out)],
        out_specs=[tok, tok, pl.BlockSpec((ts, p_len), lambda t: (t, 0))],
        out_shape=[
            jax.ShapeDtypeStruct((s_len, d), F32),
            jax.ShapeDtypeStruct((s_len, d), LOW),
            jax.ShapeDtypeStruct((s_len, p_len), F32),
        ],
        scratch_shapes=[pltpu.VMEM((ts + HALO, p_len), F32), pltpu.VMEM((ts, d_mix), LOW)],
        args=(x, g, w_in, conv_w, pool_w, pool_scale, w_out),
        exchange=exchange,
    )


def _mix_bwd(dx2, x, proj, g, w_in, conv_w, pool_w, pool_scale, w_out, name, exchange=None):
    s_len, d = x.shape
    n_blk, _, wcols = w_in.shape
    p_len = n_blk * wcols
    d_mix = w_out.shape[0]
    ts = min(MIX_TOK_TILE, s_len)
    n_t = s_len // ts
    dc = D_CONV
    n_grp = len(POOL_WINDOWS)

    def body(dx2_ref, x_ref, proj_ref, halo_ref, g_ref, win_ref, cw_ref, pw_ref, ps_ref, wout_ref,
             dx_ref, dproj_ref, cat_ref, dg_ref, dcw_ref, dpw_ref, dps_ref, df_ref, ext_ref, fut_ref):
        i = pl.program_id(0)
        t = n_t - 1 - i

        @pl.when(i == 0)
        def _():
            dg_ref[...] = jnp.zeros_like(dg_ref)
            dcw_ref[...] = jnp.zeros_like(dcw_ref)
            dpw_ref[...] = jnp.zeros_like(dpw_ref)
            dps_ref[...] = jnp.zeros_like(dps_ref)
            fut_ref[ts:ts + HALO, :] = jnp.zeros((HALO, d_mix), F32)

        ext_ref[HALO:HALO + ts, :] = proj_ref[...]

        @pl.when(t == 0)
        def _():
            ext_ref[0:HALO, :] = jnp.zeros((HALO, p_len), F32)

        @pl.when(t > 0)
        def _():
            ext_ref[0:HALO, :] = halo_ref[...]

        cw = cw_ref[...]
        v, gate_b, gate_c, z0, z1, z2, conv, pooled, inv_count = _mix_parts(ext_ref, cw, ts, t * ts)
        dx2 = dx2_ref[...]
        dcat = _dot_nt(dx2.astype(LOW), wout_ref[...])

        dy_a = dcat[:, 0:dc]
        dconv = dy_a * gate_b
        fut_ref[0:ts, 0:dc] = dconv
        cat_ref[:, 0:dc] = (gate_b * conv).astype(LOW)
        dproj_ref[:, dc:2 * dc] = (dy_a * conv).astype(LOW)
        dcw_ref[2:3, :] += jnp.sum(dconv * z0, axis=0, keepdims=True)
        dcw_ref[1:2, :] += jnp.sum(dconv * z1, axis=0, keepdims=True)
        dcw_ref[0:1, :] += jnp.sum(dconv * z2, axis=0, keepdims=True)

        dpool = []
        for grp in range(n_grp):
            c0 = POOL_GC * grp
            pooled_b = pooled[grp].astype(LOW)
            lin = _dot(pooled_b, pw_ref[grp])
            dy_b = dcat[:, dc + c0:dc + c0 + POOL_GC]
            scale = ps_ref[:, c0:c0 + POOL_GC]
            cat_ref[:, dc + c0:dc + c0 + POOL_GC] = (lin * scale).astype(LOW)
            dps_ref[:, c0:c0 + POOL_GC] += jnp.sum(dy_b * lin, axis=0, keepdims=True)
            dlin = (dy_b * scale).astype(LOW)
            dpw_ref[grp] += _dot_tn(pooled_b, dlin)
            dpool.append(_dot_nt(dlin, pw_ref[grp]))
            fut_ref[0:ts, dc + c0:dc + c0 + POOL_GC] = dpool[grp] * inv_count[grp]

        def ahead(off, c0, c1):
            return fut_ref[off:off + ts, c0:c1]

        dz = cw[2:3, :] * ahead(0, 0, dc) + cw[1:2, :] * ahead(1, 0, dc) + cw[0:1, :] * ahead(2, 0, dc)
        dproj_ref[:, 0:dc] = (dz * gate_c).astype(LOW)
        dproj_ref[:, 2 * dc:3 * dc] = (dz * v).astype(LOW)
        for grp, w in enumerate(POOL_WINDOWS):
            c0 = dc + POOL_GC * grp
            acc = ahead(0, c0, c0 + POOL_GC)
            for j in range(1, w):
                acc = acc + ahead(j, c0, c0 + POOL_GC)
            dproj_ref[:, 2 * dc + c0:2 * dc + c0 + POOL_GC] = (acc - dpool[grp]).astype(LOW)

        dh = _dot_nt(dproj_ref[:, 0:wcols], win_ref[0])
        for k in range(1, n_blk):
            dh += _dot_nt(dproj_ref[:, wcols * k:wcols * (k + 1)], win_ref[k])
        dx, dgp = _rms_bwd(dh, x_ref[...], g_ref[...])
        dx = dx2 + dx
        dx_ref[...] = dx
        df_ref[...] = (0.5 * dx).astype(LOW)
        dg_ref[...] += dgp
        fut_ref[ts:ts + HALO, :] = fut_ref[0:HALO, :]

    tok = pl.BlockSpec((ts, d), lambda i: (n_t - 1 - i, 0))
    halo = pl.BlockSpec((HALO, p_len), lambda i: (jnp.maximum((n_t - 1 - i) * (ts // HALO) - 1, 0), 0))

    def whole(arr):
        return pl.BlockSpec(arr.shape, lambda i: (0,) * arr.ndim)

    return _call(
        body,
        name=name,
        grid=(n_t,),
        in_specs=[tok, tok, pl.BlockSpec((ts, p_len), lambda i: (n_t - 1 - i, 0)), halo,
                  whole(g), _resident(w_in), whole(conv_w), whole(pool_w), whole(pool_scale), _resident(w_out)],
        out_specs=[tok, pl.BlockSpec((ts, p_len), lambda i: (n_t - 1 - i, 0)),
                   pl.BlockSpec((ts, d_mix), lambda i: (n_t - 1 - i, 0)),
                   whole(g), whole(conv_w), whole(pool_w), whole(pool_scale), tok],
        out_shape=[
            jax.ShapeDtypeStruct((s_len, d), F32),
            jax.ShapeDtypeStruct((s_len, p_len), LOW),
            jax.ShapeDtypeStruct((s_len, d_mix), LOW),
            jax.ShapeDtypeStruct(g.shape, F32),
            jax.ShapeDtypeStruct(conv_w.shape, F32),
            jax.ShapeDtypeStruct(pool_w.shape, F32),
            jax.ShapeDtypeStruct(pool_scale.shape, F32),
            jax.ShapeDtypeStruct((s_len, d), LOW),
        ],
        scratch_shapes=[pltpu.VMEM((ts + HALO, p_len), F32), pltpu.VMEM((ts + HALO, d_mix), F32)],
        args=(dx2, x, proj, proj, g, w_in, conv_w, pool_w, pool_scale, w_out),
        exchange=exchange,
    )


def _ffn_fwd_loss(x, h, wgt, wut, wd, g, target, name):
    s_len, d = x.shape
    fc = wd.shape[0]
    ts = min(TOK_TILE, s_len)

    def body(x_ref, h_ref, wg_ref, wu_ref, wd_ref, g_ref, tgt_ref,
             a_ref, b_ref, loss_ref, dx_ref, dg_ref, df_ref, s_ref):
        @pl.when(pl.program_id(0) == 0)
        def _():
            loss_ref[...] = jnp.zeros_like(loss_ref)
            dg_ref[...] = jnp.zeros_like(dg_ref)

        hb = h_ref[...]
        for c0, c1 in _slabs(fc):
            a = _dot_nt(hb, wg_ref[c0:c1, :])
            b = _dot_nt(hb, wu_ref[c0:c1, :])
            s_ref[:, c0:c1] = (a * jax.nn.sigmoid(a) * b).astype(LOW)
            a_ref[:, c0:c1] = a.astype(LOW)
            b_ref[:, c0:c1] = b.astype(LOW)
        gv = g_ref[...]
        for r0, r1 in _slabs(ts, 256):
            xv = x_ref[r0:r1, :] + 0.5 * _dot(s_ref[r0:r1, :], wd_ref[...])
            err = xv * _rms_scale(xv) * gv - tgt_ref[r0:r1, :]
            loss_ref[...] += 0.5 * jnp.sum(jnp.mean(err * err, axis=-1, keepdims=True), axis=0, keepdims=True)
            dx, dgp = _rms_bwd(err * (1.0 / d), xv, gv)
            dx_ref[r0:r1, :] = dx
            df_ref[r0:r1, :] = (0.5 * dx).astype(LOW)
            dg_ref[...] += dgp

    tok = pl.BlockSpec((ts, d), lambda t: (t, 0))
    vec = pl.BlockSpec((1, d), lambda t: (0, 0))
    hid = pl.BlockSpec((ts, fc), lambda t: (t, 0))
    return pl.pallas_call(
        body,
        name=name,
        grid=(s_len // ts,),
        in_specs=[tok, tok, _resident(wgt), _resident(wut), _resident(wd), vec, tok],
        out_specs=[hid, hid, pl.BlockSpec((1, 128), lambda t: (0, 0)), tok, vec, tok],
        out_shape=[
            jax.ShapeDtypeStruct((s_len, fc), LOW),
            jax.ShapeDtypeStruct((s_len, fc), LOW),
            jax.ShapeDtypeStruct((1, 128), F32),
            jax.ShapeDtypeStruct((s_len, d), F32),
            jax.ShapeDtypeStruct((1, d), F32),
            jax.ShapeDtypeStruct((s_len, d), LOW),
        ],
        scratch_shapes=[pltpu.VMEM((ts, fc), LOW)],
        compiler_params=_params("arbitrary"),
    )(x, h, wgt, wut, wd, g, target)


def _row_tile(rows, cols, stack_bytes):
    budget = 20 * 1024 * 1024
    per_row = cols * (4 * 7 + stack_bytes)
    for tr in (rows, 512, 256, 176, 128, 64, 32, 16, 8):
        if rows % tr == 0 and tr % 8 == 0 and tr * per_row * 2 <= budget:
            return tr
    return rows


def _sum_stack(stack, name):
    n, r, c = stack.shape
    tr = _row_tile(r, c, n * stack.dtype.itemsize)

    def body(s_ref, o_ref):
        acc = s_ref[0].astype(F32)
        for k in range(1, n):
            acc = acc + s_ref[k].astype(F32)
        o_ref[...] = acc

    return pl.pallas_call(
        body,
        name=name,
        grid=(r // tr,),
        in_specs=[pl.BlockSpec((n, tr, c), lambda i: (0, i, 0))],
        out_specs=pl.BlockSpec((tr, c), lambda i: (i, 0)),
        out_shape=jax.ShapeDtypeStruct((r, c), F32),
        compiler_params=_params("arbitrary"),
    )(stack)


def _adamw_many(params, name):
    params = [(list(st) if isinstance(st, (list, tuple)) else [st], w, m, v) for st, w, m, v in params]
    stacks0, w0 = params[0][0], params[0][1]
    r, c = w0.shape
    n = stacks0[0].shape[0]
    n_st = len(stacks0)
    part_rows = [st.shape[1] for st in stacks0]
    assert sum(part_rows) == r
    assert all(w.shape == (r, c) and [st.shape for st in sts] == [st.shape for st in stacks0] for sts, w, _, _ in params)
    first_row = [sum(part_rows[:j]) for j in range(n_st)]
    tc = next(t for t in (512, 256, 128) if c % t == 0)
    c1 = 1.0 - ADAM_B1 ** ADAM_STEP
    c2 = 1.0 - ADAM_B2 ** ADAM_STEP
    n_in = n_st + 3

    def body(*refs):
        ins, outs = refs[:n_in * len(params)], refs[n_in * len(params):]
        for p in range(len(params)):
            s_refs = ins[n_in * p:n_in * p + n_st]
            w_ref, m_ref, v_ref = ins[n_in * p + n_st:n_in * (p + 1)]
            g_ref, d_ref, mo_ref, vo_ref = outs[4 * p:4 * p + 4]
            for s_ref, r0, nr in zip(s_refs, first_row, part_rows):
                gv = s_ref[0].astype(F32)
                for k in range(1, n):
                    gv = gv + s_ref[k].astype(F32)
                mn = ADAM_B1 * m_ref[r0:r0 + nr, :] + (1.0 - ADAM_B1) * gv
                vn = ADAM_B2 * v_ref[r0:r0 + nr, :] + (1.0 - ADAM_B2) * (gv * gv)
                g_ref[r0:r0 + nr, :] = gv
                mo_ref[r0:r0 + nr, :] = mn
                vo_ref[r0:r0 + nr, :] = vn
                d_ref[r0:r0 + nr, :] = -ADAM_LR * ((mn / c1) / (jnp.sqrt(vn / c2) + ADAM_EPS)
                                                   + ADAM_WD * w_ref[r0:r0 + nr, :])

    blk = pl.BlockSpec((r, tc), lambda i: (0, i))
    one_in = [pl.BlockSpec((n, nr, tc), lambda i: (0, 0, i)) for nr in part_rows] + [blk, blk, blk]
    res = pl.pallas_call(
        body,
        name=name,
        grid=(c // tc,),
        in_specs=one_in * len(params),
        out_specs=[blk] * (4 * len(params)),
        out_shape=[jax.ShapeDtypeStruct((r, c), F32)] * (4 * len(params)),
        compiler_params=_params("arbitrary"),
    )(*[arr for sts, w, m, v in params for arr in (*sts, w, m, v)])
    return [tuple(res[4 * p:4 * p + 4]) for p in range(len(params))]


def _adamw(stacks, w, m, v, name):
    return _adamw_many([(stacks, w, m, v)], name)[0]


def _to_sheet(parts):
    sheets, spans = [], []
    row = 0
    for p in parts:
        flat = p.reshape(-1).astype(F32)
        rows = -(-flat.shape[0] // 1024) * 8
        flat = jnp.pad(flat, (0, rows * 128 - flat.shape[0]))
        sheets.append(flat.reshape(rows, 128))
        spans.append((row, p.size, p.shape))
        row += rows
    return jnp.concatenate(sheets, axis=0), spans


def _from_sheet(sheet, spans):
    out = []
    for row, size, shape in spans:
        rows = -(-size // 1024) * 8
        out.append(sheet[row:row + rows].reshape(-1)[:size].reshape(shape))
    return out


def kernel(x, norm_ffn1, ffn1_w_gate, ffn1_w_up, ffn1_w_down, norm_mix, w_in, conv_w, pool_w, pool_scale, w_out, norm_ffn2, ffn2_w_gate, ffn2_w_up, ffn2_w_down, norm_final, loss_target, m_norm_ffn1, m_ffn1_w_gate, m_ffn1_w_up, m_ffn1_w_down, m_norm_mix, m_w_in, m_conv_w, m_pool_w, m_pool_scale, m_w_out, m_norm_ffn2, m_ffn2_w_gate, m_ffn2_w_up, m_ffn2_w_down, m_norm_final, v_norm_ffn1, v_ffn1_w_gate, v_ffn1_w_up, v_ffn1_w_down, v_norm_mix, v_w_in, v_conv_w, v_pool_w, v_pool_scale, v_w_out, v_norm_ffn2, v_ffn2_w_gate, v_ffn2_w_up, v_ffn2_w_down, v_norm_final):
    me = 4 * lax.axis_index("x") + 2 * lax.axis_index("y") + lax.axis_index("c")
    xs, tgt = x[0], loss_target[0]
    s_len, d = xs.shape
    f_shard = ffn1_w_down.shape[1]
    conv_shard = conv_w.shape[2]

    def low_t(wt):
        return wt[0].T.astype(LOW)

    def by_dev(gw):
        return gw.reshape(N_DEV, -1, d)

    conv_tile = jnp.zeros((8, 128), F32).at[0:conv_w.shape[1], 0:conv_shard].set(conv_w[0])
    pool_w_low = pool_w[0].astype(LOW)

    rows_a = -(-f_shard // 64) * 32

    def parts_of(w_gate, w_up, w_down):
        shards = [low_t(w_gate), low_t(w_up), w_down[0].astype(LOW)]
        return [(s, 0, rows_a) for s in shards], [(s, rows_a, f_shard - rows_a) for s in shards]

    def rows_flat(stacks):
        return [st.reshape(-1, d) for st in stacks]

    def gather(shards):
        return _Exchange(shards, [False] * len(shards), relay_at=0.45)

    def scatter(pairs):
        return _Exchange(pairs, [True] * len(pairs), among_chips=True)

    w1a_shards, w1b_shards = parts_of(ffn1_w_gate, ffn1_w_up, ffn1_w_down)
    w2a_shards, w2b_shards = parts_of(ffn2_w_gate, ffn2_w_up, ffn2_w_down)

    wg1a, wu1a, wd1a = rows_flat(_exchange(w1a_shards, [False] * 3, "gather_ffn1_a", relay=True))
    xa, h1, a1a, b1a, *w1b = _ffn_fwd(xs, norm_ffn1, None, wg1a, wu1a, wd1a, "ffn1_fwd_a",
                                      exchange=gather(w1b_shards))
    wg1b, wu1b, wd1b = rows_flat(w1b)
    x1, a1b, b1b, w_in_full, w_out_full, conv_tiles = _ffn_fwd(
        xa, None, h1, wg1b, wu1b, wd1b, "ffn1_fwd_b",
        exchange=gather([w_in[0].astype(LOW), w_out[0].astype(LOW), conv_tile]))
    w_out_full = w_out_full.reshape(-1, d)
    conv_full = jnp.concatenate([conv_tiles[k, 0:conv_w.shape[1], 0:conv_shard] for k in range(N_DEV)], axis=1)
    x2, h2, proj, *w2a = _mix_fwd(x1, norm_mix, w_in_full, conv_full, pool_w_low, pool_scale, w_out_full, "mix_fwd",
                                  exchange=gather(w2a_shards))
    wg2a, wu2a, wd2a = rows_flat(w2a)
    xb, h3, a2a, b2a, *w2b = _ffn_fwd(x2, norm_ffn2, None, wg2a, wu2a, wd2a, "ffn2_fwd_a",
                                      exchange=gather(w2b_shards))
    wg2b, wu2b, wd2b = rows_flat(w2b)
    a2b, b2b, loss_row, dx3, dg_final, df3 = _ffn_fwd_loss(
        xb, h3, wg2b, wu2b, wd2b, norm_final.reshape(1, d), tgt, "ffn2_fwd_b_loss")

    da2a, db2a, dwd_a, dwg_a, dwu_a = _ffn_bwdw(df3, a2a, b2a, h3, wd2a, "ffn2_bwdw_a")
    da2b, db2b, dwd_b, dwg_b, dwu_b = _ffn_bwdw(df3, a2b, b2b, h3, wd2b, "ffn2_bwdw_b")
    pairs = _pair_sum([by_dev(dwg_a), by_dev(dwu_a), by_dev(dwd_a), by_dev(dwg_b), by_dev(dwu_b), by_dev(dwd_b)],
                      "pair_sum_ffn2")
    dx2, dg_ffn2, *got_2a = _ffn_dx(dx3, x2, norm_ffn2, [(da2a, db2a, wg2a, wu2a), (da2b, db2b, wg2b, wu2b)],
                                    "ffn2_dx", exchange=scatter(pairs[:3]))
    dx1, dproj, cat, dg_mix, dconv, dpool_w, dpool_scale, df1, *got_2b = _mix_bwd(
        dx2, x1, proj, norm_mix, w_in_full, conv_full, pool_w_low, pool_scale, w_out_full, "mix_bwd",
        exchange=scatter(pairs[3:]))
    small_parts = [dg_mix, dg_ffn2, dg_final, dconv, dpool_w, dpool_scale, loss_row]
    small_sheet, spans = _to_sheet(small_parts)
    dw_in, got_small = _wgrad_tn(h2, dproj, W_IN_SHARD, True, "w_in_wgrad",
                                 exchange=_Exchange([small_sheet], [False], relay_at=0.5))
    (dw_out,) = _wgrad_tn(cat, dx2, d, False, "w_out_wgrad")
    pairs = _pair_sum([dw_in, by_dev(dw_out)], "pair_sum_mix")
    da1a, db1a, dwd, dwg, dwu, got_in, got_out = _ffn_bwdw(
        df1, a1a, b1a, h1, wd1a, "ffn1_bwdw_a", exchange=scatter(pairs))
    pairs = _pair_sum([by_dev(dwg), by_dev(dwu), by_dev(dwd)], "pair_sum_ffn1_a")
    da1b, db1b, dwd, dwg, dwu, *got_1a = _ffn_bwdw(df1, a1b, b1b, h1, wd1b, "ffn1_bwdw_b", exchange=scatter(pairs))
    pairs = _pair_sum([by_dev(dwg), by_dev(dwu), by_dev(dwd)], "pair_sum_ffn1_b")
    dx0, dg_ffn1, *got_1b = _ffn_dx(dx1, xs, norm_ffn1, [(da1a, db1a, wg1a, wu1a), (da1b, db1b, wg1b, wu1b)],
                                    "ffn1_dx", exchange=scatter(pairs))
    (got_n1,) = _exchange([dg_ffn1.reshape(8, 128)], [False], "gather_dnorm_ffn1")

    outs = {}

    def update(name, stacks, w, m, v):
        outs[name] = _adamw(stacks, w[0], m[0], v[0], "adamw_" + name)

    def update_ffn(prefix, got_a, got_b, gate, up, down):
        res = _adamw_many(
            [([got_a[j], got_b[j]], *[(t[0].T if j < 2 else t[0]) for t in wmv]) for j, wmv in enumerate((gate, up, down))],
            "adamw_" + prefix)
        outs[prefix + "_w_gate"] = tuple(r.T for r in res[0])
        outs[prefix + "_w_up"] = tuple(r.T for r in res[1])
        outs[prefix + "_w_down"] = res[2]

    update_ffn("ffn1", got_1a, got_1b, (ffn1_w_gate, m_ffn1_w_gate, v_ffn1_w_gate),
               (ffn1_w_up, m_ffn1_w_up, v_ffn1_w_up), (ffn1_w_down, m_ffn1_w_down, v_ffn1_w_down))
    update_ffn("ffn2", got_2a, got_2b, (ffn2_w_gate, m_ffn2_w_gate, v_ffn2_w_gate),
               (ffn2_w_up, m_ffn2_w_up, v_ffn2_w_up), (ffn2_w_down, m_ffn2_w_down, v_ffn2_w_down))
    update("w_in", got_in, w_in, m_w_in, v_w_in)
    update("w_out", got_out, w_out, m_w_out, v_w_out)

    g_small = _from_sheet(_sum_stack(got_small, "sum_small"), spans)
    g_norm_ffn1 = _sum_stack(got_n1, "sum_dnorm_ffn1").reshape(norm_ffn1.shape)
    g_conv = lax.dynamic_slice_in_dim(g_small[3], me * conv_shard, conv_shard, axis=1)
    small_names = ["norm_ffn1", "norm_mix", "norm_ffn2", "norm_final", "conv_w", "pool_w", "pool_scale"]
    small_g = [g_norm_ffn1, g_small[0], g_small[1], g_small[2].reshape(norm_final.shape), g_conv[None],
               g_small[4][None], g_small[5]]
    small_w = [norm_ffn1, norm_mix, norm_ffn2, norm_final, conv_w, pool_w, pool_scale]
    small_m = [m_norm_ffn1, m_norm_mix, m_norm_ffn2, m_norm_final, m_conv_w, m_pool_w, m_pool_scale]
    small_v = [v_norm_ffn1, v_norm_mix, v_norm_ffn2, v_norm_final, v_conv_w, v_pool_w, v_pool_scale]
    g_sheet, spans_u = _to_sheet(small_g)
    w_sheet, _ = _to_sheet(small_w)
    m_sheet, _ = _to_sheet(small_m)
    v_sheet, _ = _to_sheet(small_v)
    upd = _adamw(g_sheet[None], w_sheet, m_sheet, v_sheet, "adamw_small")
    small_out = [_from_sheet(u, spans_u) for u in upd]
    for k, nm in enumerate(small_names):
        outs[nm] = tuple(small_out[j][k] for j in range(4))

    loss = g_small[6][0, 0]
    order = ["norm_ffn1", "ffn1_w_gate", "ffn1_w_up", "ffn1_w_down", "norm_mix", "w_in", "conv_w", "pool_w",
             "pool_scale", "w_out", "norm_ffn2", "ffn2_w_gate", "ffn2_w_up", "ffn2_w_down", "norm_final"]
    big = {"ffn1_w_gate", "ffn1_w_up", "ffn1_w_down", "w_in", "w_out", "ffn2_w_gate", "ffn2_w_up", "ffn2_w_down"}

    def leaf(nm, j):
        val = outs[nm][j]
        return val[None] if nm in big else val

    return (loss, dx0[None],
            *[leaf(nm, 0) for nm in order], *[leaf(nm, 1) for nm in order],
            *[leaf(nm, 2) for nm in order], *[leaf(nm, 3) for nm in order])
```

```python
import jax
import jax.numpy as jnp
from jax import lax
from jax.experimental import pallas as pl
from jax.experimental.pallas import tpu as pltpu

F32 = jnp.float32
LOW = jnp.bfloat16

N_DEV = 8
EPS = 1e-6
D_CONV = 512
POOL_WINDOWS = (2, 4, 8, 16)
POOL_GC = 128
HALO = 16
W_IN_SHARD = 256

ADAM_LR = 0.001
ADAM_B1 = 0.9
ADAM_B2 = 0.999
ADAM_EPS = 1e-08
ADAM_WD = 0.01
ADAM_STEP = 10

VMEM_LIMIT_BYTES = 56 * 1024 * 1024
TOK_TILE = 512
MIX_TOK_TILE = 256
WGRAD_TOK_TILE = 4096
WGRAD_ROW_CANDIDATES = (256, 128)
RELAY_LAST_LATER = 0.25
BWD_ROW_SLAB = 2048


def _params(*sem):
    return pltpu.CompilerParams(dimension_semantics=sem, vmem_limit_bytes=VMEM_LIMIT_BYTES)


def _resident(arr):
    return pl.BlockSpec(arr.shape, lambda *_: (0,) * arr.ndim, pipeline_mode=pl.Buffered(1))


def _pick(n, candidates):
    for c in candidates:
        if n % c == 0:
            return c
    raise ValueError(f"no tile in {candidates} divides {n}")


def _dot(a, b):
    return lax.dot_general(a, b, (((1,), (0,)), ((), ())), preferred_element_type=F32)


def _dot_nt(a, b):
    return lax.dot_general(a, b, (((1,), (1,)), ((), ())), preferred_element_type=F32)


def _dot_tn(a, b):
    return lax.dot_general(a, b, (((0,), (0,)), ((), ())), preferred_element_type=F32)


def _rms_scale(x):
    return lax.rsqrt(jnp.mean(x * x, axis=-1, keepdims=True) + EPS)


def _rms_bwd(dy, x, g):
    r = _rms_scale(x)
    xhat = x * r
    gdy = dy * g
    dx = r * (gdy - xhat * jnp.mean(gdy * xhat, axis=-1, keepdims=True))
    return dx, jnp.sum(dy * xhat, axis=0, keepdims=True)


COLLECTIVE_PAIR, COLLECTIVE_CHIPS, COLLECTIVE_RELAY = 0, 1, 2


def _handshake(peer_numbers):
    mx, my, mc = lax.axis_index("x"), lax.axis_index("y"), lax.axis_index("c")
    barrier = pltpu.get_barrier_semaphore()
    for m in peer_numbers:
        peer = (lax.rem(mx + ((m >> 2) & 1), 2), lax.rem(my + ((m >> 1) & 1), 2), lax.rem(mc + (m & 1), 2))
        pl.semaphore_signal(barrier, inc=1, device_id=peer, device_id_type=pl.DeviceIdType.MESH)
    pl.semaphore_wait(barrier, len(peer_numbers))


class _Exchange:
    CHIPS = (2, 4, 6)

    def __init__(self, arrays, sliced, relay_at=None, among_chips=False):
        assert relay_at is None or not any(sliced)
        assert not among_chips or (all(sliced) and relay_at is None)
        self.relay_at, self.among_chips = relay_at, among_chips
        self.peers = self.CHIPS if among_chips else ((1, 2, 4) if relay_at is not None else tuple(range(1, N_DEV)))
        self.collective_id = COLLECTIVE_CHIPS if among_chips else (COLLECTIVE_RELAY if relay_at is not None else None)
        self.rows = [arr[1:] if isinstance(arr, tuple) else None for arr in arrays]
        self.arrays = [arr[0] if isinstance(arr, tuple) else arr for arr in arrays]
        self.sliced, self.n = list(sliced), len(arrays)
        assert all(rg is None or not sl for rg, sl in zip(self.rows, sliced))
        self.block_shape = [arr.shape if rg is None else (rg[1],) + arr.shape[1:]
                            for arr, rg in zip(self.arrays, self.rows)]
        self.out_shape = [jax.ShapeDtypeStruct(shape if sl else (N_DEV,) + shape, arr.dtype)
                          for arr, shape, sl in zip(self.arrays, self.block_shape, sliced)]
        self.specs = [pl.BlockSpec(memory_space=pl.ANY)] * self.n
        self.scratch_shapes = [pltpu.SemaphoreType.DMA((self.n, N_DEV)),
                               pltpu.SemaphoreType.DMA((self.n, N_DEV)),
                               pltpu.SemaphoreType.DMA((self.n,))]

    HALF_VIA = ((4, 2, 5), (2, 4, 7))

    def _halves(self, a):
        rows = self.block_shape[a][0]
        if rows % 32:
            return ((0, rows), None)
        return ((0, rows // 2), (rows // 2, rows // 2))

    def _copies(self, ins, outs, sems):
        send_sems, recv_sems, local_sems = sems
        sliced = self.sliced
        mx, my, mc = lax.axis_index("x"), lax.axis_index("y"), lax.axis_index("c")
        me = 2 * mx + my if self.among_chips else 4 * mx + 2 * my + mc

        def peer(m):
            px = lax.rem(mx + ((m >> 2) & 1), 2)
            py = lax.rem(my + ((m >> 1) & 1), 2)
            pc = lax.rem(mc + (m & 1), 2)
            return (px, py, pc), (2 * px + py if self.among_chips else 4 * px + 2 * py + pc)

        def mine(a):
            return ins[a] if self.rows[a] is None else ins[a].at[pl.ds(*self.rows[a])]

        def remote(a, m, arriving):
            pid, pflat = peer(m)
            return pltpu.make_async_remote_copy(
                src_ref=ins[a].at[pflat] if sliced[a] else mine(a),
                dst_ref=outs[a].at[pflat if arriving else me],
                send_sem=send_sems.at[a, m - 1],
                recv_sem=recv_sems.at[a, m - 1],
                device_id=pid,
                device_id_type=pl.DeviceIdType.MESH,
            )

        def local(a):
            return pltpu.make_async_copy(ins[a].at[me] if sliced[a] else mine(a), outs[a].at[me], local_sems.at[a])

        def passed_on(a, m):
            _, origin = peer(m)
            sibling, _ = peer(1)
            return pltpu.make_async_remote_copy(
                src_ref=outs[a].at[origin],
                dst_ref=outs[a].at[origin],
                send_sem=send_sems.at[a, m],
                recv_sem=recv_sems.at[a, m],
                device_id=sibling,
                device_id_type=pl.DeviceIdType.MESH,
            )

        def half_on(a, h, arriving):
            via, to, column = self.HALF_VIA[h]
            r0, nr = self._halves(a)[h]
            _, origin = peer(6 if arriving else via)
            rows = outs[a].at[origin].at[pl.ds(r0, nr)]
            return pltpu.make_async_remote_copy(
                src_ref=rows, dst_ref=rows, send_sem=send_sems.at[a, column], recv_sem=recv_sems.at[a, column],
                device_id=peer(to)[0], device_id_type=pl.DeviceIdType.MESH)

        return remote, local, passed_on, half_on

    def start(self, ins, outs, sems):
        remote, local, _, _ = self._copies(ins, outs, sems)
        if self.collective_id is not None:
            _handshake(self.peers)
        for a in range(self.n):
            local(a).start()
        for m in self.peers:
            for a in range(self.n):
                remote(a, m, False).start()

    def relay(self, ins, outs, sems):
        remote, _, passed_on, half_on = self._copies(ins, outs, sems)
        for h, (via, _, _) in enumerate(self.HALF_VIA):
            for a in range(self.n):
                remote(a, via, True).wait_recv()
                passed_on(a, via).start()
                if self._halves(a)[h] is not None:
                    half_on(a, h, False).start()

    def relay_last(self, ins, outs, sems):
        _, _, passed_on, half_on = self._copies(ins, outs, sems)
        for a in range(self.n):
            for h in range(2):
                if self._halves(a)[h] is not None:
                    half_on(a, h, True).wait_recv()
            passed_on(a, 6).start()

    def wait(self, ins, outs, sems):
        remote, local, passed_on, half_on = self._copies(ins, outs, sems)
        if self.relay_at is None:
            for m in self.peers:
                for a in range(self.n):
                    remote(a, m, True).wait_recv()
            for m in self.peers:
                for a in range(self.n):
                    remote(a, m, False).wait_send()
        else:
            for m in (1, 3, 5, 7):
                for a in range(self.n):
                    remote(a, m, True).wait_recv()
            for m in self.peers:
                for a in range(self.n):
                    remote(a, m, False).wait_send()
            for a in range(self.n):
                for m in self.CHIPS:
                    passed_on(a, m).wait_send()
                for h in range(2):
                    if self._halves(a)[h] is not None:
                        half_on(a, h, False).wait_send()
        for a in range(self.n):
            local(a).wait()


def _pair_sum(stacks, name):
    n = len(stacks)
    n_chip = N_DEV // 2
    half = [(n_chip,) + st.shape[1:] for st in stacks]

    def body(*refs):
        ins, outs, mine, theirs = refs[:n], refs[n:2 * n], refs[2 * n:3 * n], refs[3 * n:4 * n]
        local_sems, send_sems, recv_sems = refs[4 * n:]
        mx, my, mc = lax.axis_index("x"), lax.axis_index("y"), lax.axis_index("c")

        def own(a, k):
            return pltpu.make_async_copy(ins[a].at[2 * k + mc], mine[a].at[k], local_sems.at[a, k])

        def swap(a, k):
            return pltpu.make_async_remote_copy(
                src_ref=ins[a].at[2 * k + (1 - mc)], dst_ref=theirs[a].at[k],
                send_sem=send_sems.at[a, k], recv_sem=recv_sems.at[a, k],
                device_id=(mx, my, 1 - mc), device_id_type=pl.DeviceIdType.MESH)

        _handshake((1,))
        for k in range(n_chip):
            for a in range(n):
                own(a, k).start()
                swap(a, k).start()
        for k in range(n_chip):
            for a in range(n):
                own(a, k).wait()
                swap(a, k).wait()
                outs[a][k] = (mine[a][k].astype(F32) + theirs[a][k].astype(F32)).astype(LOW)

    return pl.pallas_call(
        body, name=name,
        out_shape=[jax.ShapeDtypeStruct(h, LOW) for h in half],
        in_specs=[pl.BlockSpec(memory_space=pl.ANY)] * n,
        out_specs=[pl.BlockSpec(memory_space=pltpu.VMEM)] * n,
        scratch_shapes=([pltpu.VMEM(h, st.dtype) for h, st in zip(half, stacks)] * 2
                        + [pltpu.SemaphoreType.DMA((n, n_chip))] * 3),
        compiler_params=pltpu.CompilerParams(vmem_limit_bytes=VMEM_LIMIT_BYTES, collective_id=COLLECTIVE_PAIR),
    )(*stacks)


def _exchange(arrays, sliced, name, relay=False):
    ex = _Exchange(arrays, sliced, relay_at=0 if relay else None)

    def body(*refs):
        ins, outs, sems = refs[:ex.n], refs[ex.n:2 * ex.n], refs[2 * ex.n:]
        ex.start(ins, outs, sems)
        if relay:
            ex.relay(ins, outs, sems)
            ex.relay_last(ins, outs, sems)
        ex.wait(ins, outs, sems)

    return pl.pallas_call(body, name=name, out_shape=ex.out_shape, in_specs=ex.specs, out_specs=ex.specs,
                          scratch_shapes=ex.scratch_shapes,
                          compiler_params=pltpu.CompilerParams(collective_id=ex.collective_id))(*ex.arrays)


def _call(body, *, name, grid, in_specs, out_specs, out_shape, args, scratch_shapes=(), exchange=None):
    params = _params(*(("arbitrary",) * len(grid)))
    if exchange is None:
        return pl.pallas_call(body, name=name, grid=grid, in_specs=in_specs, out_specs=out_specs, out_shape=out_shape,
                              scratch_shapes=list(scratch_shapes), compiler_params=params)(*args)
    exs = list(exchange) if isinstance(exchange, (list, tuple)) else [exchange]
    assert len(exs) == 1 or all(ex.collective_id is None for ex in exs)
    params = pltpu.CompilerParams(dimension_semantics=("arbitrary",) * len(grid), vmem_limit_bytes=VMEM_LIMIT_BYTES,
                                  collective_id=exs[0].collective_id)
    n_in, n_out, n_scr = len(in_specs), len(out_specs), len(scratch_shapes)
    n_ex = sum(ex.n for ex in exs)
    n_steps = 1
    for g in grid:
        n_steps *= g

    def hosted(*refs):
        ins, refs = refs[:n_in], refs[n_in:]
        ex_ins, refs = refs[:n_ex], refs[n_ex:]
        outs, refs = refs[:n_out], refs[n_out:]
        ex_outs, refs = refs[:n_ex], refs[n_ex:]
        scr, sems = refs[:n_scr], refs[n_scr:]
        parts, at = [], 0
        for j, ex in enumerate(exs):
            parts.append((ex_ins[at:at + ex.n], ex_outs[at:at + ex.n], sems[3 * j:3 * j + 3]))
            at += ex.n
        step = pl.program_id(0)
        for ax in range(1, len(grid)):
            step = step * grid[ax] + pl.program_id(ax)

        @pl.when(step == 0)
        def _():
            for ex, part in zip(exs, parts):
                ex.start(*part)

        body(*ins, *outs, *scr)

        for ex, part in zip(exs, parts):
            if ex.relay_at is not None:
                @pl.when(step == min(int(ex.relay_at * n_steps), n_steps - 1))
                def _():
                    ex.relay(*part)

                @pl.when(step == min(int((ex.relay_at + RELAY_LAST_LATER) * n_steps), n_steps - 1))
                def _():
                    ex.relay_last(*part)

        @pl.when(step == n_steps - 1)
        def _():
            for ex, part in zip(exs, parts):
                ex.wait(*part)

    return pl.pallas_call(
        hosted, name=name, grid=grid,
        in_specs=list(in_specs) + [sp for ex in exs for sp in ex.specs],
        out_specs=list(out_specs) + [sp for ex in exs for sp in ex.specs],
        out_shape=list(out_shape) + [sh for ex in exs for sh in ex.out_shape],
        scratch_shapes=list(scratch_shapes) + [sc for ex in exs for sc in ex.scratch_shapes],
        compiler_params=params)(*args, *[arr for ex in exs for arr in ex.arrays])


def _ffn_fwd(x, g, h, wgt, wut, wd, name, exchange=None):
    s_len, d = x.shape
    fc = wd.shape[0]
    ts = min(TOK_TILE, s_len)
    first = h is None

    def body(*refs):
        x_ref, gh_ref, wg_ref, wu_ref, wd_ref, xo_ref = refs[:6]
        a_ref, b_ref, s_ref = refs[-3:]
        xv = x_ref[...]
        if first:
            hb = (xv * _rms_scale(xv) * gh_ref[...]).astype(LOW)
            refs[6][...] = hb
        else:
            hb = gh_ref[...]
        for c0, c1 in _slabs(fc):
            a = _dot_nt(hb, wg_ref[c0:c1, :])
            b = _dot_nt(hb, wu_ref[c0:c1, :])
            s_ref[:, c0:c1] = (a * jax.nn.sigmoid(a) * b).astype(LOW)
            a_ref[:, c0:c1] = a.astype(LOW)
            b_ref[:, c0:c1] = b.astype(LOW)
        xo_ref[...] = xv + 0.5 * _dot(s_ref[...], wd_ref[...])

    tok = pl.BlockSpec((ts, d), lambda t: (t, 0))
    hid = pl.BlockSpec((ts, fc), lambda t: (t, 0))
    tok_out = jax.ShapeDtypeStruct((s_len, d), F32)
    h_out = jax.ShapeDtypeStruct((s_len, d), LOW)
    hid_out = jax.ShapeDtypeStruct((s_len, fc), LOW)
    return _call(
        body,
        name=name,
        grid=(s_len // ts,),
        in_specs=[tok, pl.BlockSpec((1, d), lambda t: (0, 0)) if first else tok,
                  _resident(wgt), _resident(wut), _resident(wd)],
        out_specs=[tok] + ([tok] if first else []) + [hid, hid],
        out_shape=[tok_out] + ([h_out] if first else []) + [hid_out, hid_out],
        scratch_shapes=[pltpu.VMEM((ts, fc), LOW)],
        args=(x, g if first else h, wgt, wut, wd),
        exchange=exchange,
    )


def _slabs(width, slab=256):
    return [(c0, min(c0 + slab, width)) for c0 in range(0, width, slab)]


def _ffn_bwdw(df, a, b, h, wd, name, exchange=None):
    s_len, d = df.shape
    f_len = wd.shape[0]
    tm = _pick(f_len, WGRAD_ROW_CANDIDATES)
    tk = min(WGRAD_TOK_TILE, s_len)
    n_k = s_len // tk

    def body(df_ref, a_ref, b_ref, h_ref, wd_ref, da_ref, db_ref, dwd_ref, dwg_ref, dwu_ref,
             s_ref, acc_d, acc_g, acc_u):
        k = pl.program_id(1)

        @pl.when(k == 0)
        def _():
            acc_d[...] = jnp.zeros_like(acc_d)
            acc_g[...] = jnp.zeros_like(acc_g)
            acc_u[...] = jnp.zeros_like(acc_u)

        wdv = wd_ref[...]
        for r0, r1 in _slabs(tk, BWD_ROW_SLAB):
            ds = _dot_nt(df_ref[r0:r1, :], wdv)
            av = a_ref[r0:r1, :].astype(F32)
            bv = b_ref[r0:r1, :].astype(F32)
            sig = jax.nn.sigmoid(av)
            silu = av * sig
            s_ref[r0:r1, :] = (silu * bv).astype(LOW)
            da_ref[r0:r1, :] = (ds * bv * (sig * (1.0 + av * (1.0 - sig)))).astype(LOW)
            db_ref[r0:r1, :] = (ds * silu).astype(LOW)
            hv = h_ref[r0:r1, :]
            acc_d[...] += _dot_tn(s_ref[r0:r1, :], df_ref[r0:r1, :])
            acc_g[...] += _dot_tn(da_ref[r0:r1, :], hv)
            acc_u[...] += _dot_tn(db_ref[r0:r1, :], hv)

        @pl.when(k == n_k - 1)
        def _():
            dwd_ref[...] = acc_d[...].astype(LOW)
            dwg_ref[...] = acc_g[...].astype(LOW)
            dwu_ref[...] = acc_u[...].astype(LOW)

    hid = pl.BlockSpec((tk, tm), lambda i, k: (k, i))
    tok = pl.BlockSpec((tk, d), lambda i, k: (k, 0))
    wrow = pl.BlockSpec((tm, d), lambda i, k: (i, 0))
    return _call(
        body,
        name=name,
        grid=(f_len // tm, n_k),
        in_specs=[tok, hid, hid, tok, wrow],
        out_specs=[hid, hid, wrow, wrow, wrow],
        out_shape=[jax.ShapeDtypeStruct((s_len, f_len), LOW)] * 2 + [jax.ShapeDtypeStruct((f_len, d), LOW)] * 3,
        scratch_shapes=[pltpu.VMEM((tk, tm), LOW)] + [pltpu.VMEM((tm, d), F32)] * 3,
        args=(df, a, b, h, wd),
        exchange=exchange,
    )


def _ffn_dx(dxo, x, g, parts, name, exchange=None):
    s_len, d = x.shape
    ts = min(TOK_TILE, s_len)
    n_p = len(parts)

    def body(dxo_ref, x_ref, g_ref, *refs):
        dxi_ref, dg_ref = refs[4 * n_p:]

        @pl.when(pl.program_id(0) == 0)
        def _():
            dg_ref[...] = jnp.zeros_like(dg_ref)

        dh = None
        for p in range(n_p):
            da_ref, db_ref, wg_ref, wu_ref = refs[4 * p:4 * p + 4]
            part = _dot(da_ref[...], wg_ref[...]) + _dot(db_ref[...], wu_ref[...])
            dh = part if dh is None else dh + part
        dx, dgp = _rms_bwd(dh, x_ref[...], g_ref[...])
        dxi_ref[...] = dxo_ref[...] + dx
        dg_ref[...] += dgp

    tok = pl.BlockSpec((ts, d), lambda t: (t, 0))
    vec = pl.BlockSpec((1, d), lambda t: (0, 0))
    part_specs, part_args = [], []
    for da, db, wgt, wut in parts:
        hid = pl.BlockSpec((ts, da.shape[1]), lambda t: (t, 0))
        part_specs += [hid, hid, _resident(wgt), _resident(wut)]
        part_args += [da, db, wgt, wut]
    return _call(
        body,
        name=name,
        grid=(s_len // ts,),
        in_specs=[tok, tok, vec] + part_specs,
        out_specs=[tok, vec],
        out_shape=[jax.ShapeDtypeStruct((s_len, d), F32), jax.ShapeDtypeStruct((1, d), F32)],
        args=(dxo, x, g, *part_args),
        exchange=exchange,
    )


def _wgrad_tn(xm, ym, tn, stacked, name, exchange=None):
    s_len, m = xm.shape
    n = ym.shape[1]
    tk = min(WGRAD_TOK_TILE, s_len)
    n_k = s_len // tk

    def body(x_ref, y_ref, o_ref, acc):
        k = pl.program_id(1)

        @pl.when(k == 0)
        def _():
            acc[...] = jnp.zeros_like(acc)

        acc[...] += _dot_tn(x_ref[...].astype(LOW), y_ref[...].astype(LOW))

        @pl.when(k == n_k - 1)
        def _():
            o_ref[...] = acc[...].astype(LOW)

    if stacked:
        out_spec = pl.BlockSpec((None, m, tn), lambda j, k: (j, 0, 0))
        out_shape = jax.ShapeDtypeStruct((n // tn, m, tn), LOW)
    else:
        out_spec = pl.BlockSpec((m, tn), lambda j, k: (0, j))
        out_shape = jax.ShapeDtypeStruct((m, n), LOW)
    return _call(
        body,
        name=name,
        grid=(n // tn, n_k),
        in_specs=[pl.BlockSpec((tk, m), lambda j, k: (k, 0)), pl.BlockSpec((tk, tn), lambda j, k: (k, j))],
        out_specs=[out_spec],
        out_shape=[out_shape],
        scratch_shapes=[pltpu.VMEM((m, tn), F32)],
        args=(xm, ym),
        exchange=exchange,
    )


def _mix_parts(ext_ref, cw, ts, row0, with_pooled=True):
    dc = D_CONV

    def back(off, c0, c1):
        return ext_ref[HALO - off:HALO - off + ts, c0:c1]

    v, gate_b, gate_c = back(0, 0, dc), back(0, dc, 2 * dc), back(0, 2 * dc, 3 * dc)
    z0 = gate_c * v
    z1 = back(1, 2 * dc, 3 * dc) * back(1, 0, dc)
    z2 = back(2, 2 * dc, 3 * dc) * back(2, 0, dc)
    conv = cw[2:3, :] * z0 + cw[1:2, :] * z1 + cw[0:1, :] * z2
    rows = row0 + lax.broadcasted_iota(jnp.int32, (ts, 1), 0)
    pooled, inv_count = [], []
    for grp, w in enumerate(POOL_WINDOWS):
        inv = 1.0 / jnp.minimum(rows + 1, w).astype(F32)
        inv_count.append(inv)
        if with_pooled:
            c0 = 3 * dc + POOL_GC * grp
            u = back(0, c0, c0 + POOL_GC)
            acc = u
            for j in range(1, w):
                acc = acc + back(j, c0, c0 + POOL_GC)
            pooled.append(acc * inv - u)
    return v, gate_b, gate_c, z0, z1, z2, conv, pooled, inv_count


def _mix_fwd(x, g, w_in, conv_w, pool_w, pool_scale, w_out, name, exchange=None):
    s_len, d = x.shape
    n_blk, _, wcols = w_in.shape
    p_len = n_blk * wcols
    d_mix = w_out.shape[0]
    ts = min(MIX_TOK_TILE, s_len)
    dc = D_CONV

    def body(x_ref, g_ref, win_ref, cw_ref, pw_ref, ps_ref, wout_ref, x2_ref, h_ref, proj_ref, pooled_ref,
             ext_ref, cat_ref):
        t = pl.program_id(0)

        @pl.when(t == 0)
        def _():
            ext_ref[0:HALO, :] = jnp.zeros((HALO, p_len), F32)

        xv = x_ref[...]
        hb = (xv * _rms_scale(xv) * g_ref[...]).astype(LOW)
        h_ref[...] = hb
        for k in range(n_blk):
            ext_ref[HALO:HALO + ts, wcols * k:wcols * (k + 1)] = _dot(hb, win_ref[k])
        proj_ref[...] = ext_ref[HALO:HALO + ts, :]

        _, gate_b, _, _, _, _, conv, pooled, _ = _mix_parts(ext_ref, cw_ref[...], ts, t * ts)
        cat_ref[:, 0:dc] = (gate_b * conv).astype(LOW)
        for grp in range(len(POOL_WINDOWS)):
            c0 = POOL_GC * grp
            pooled_b = pooled[grp].astype(LOW)
            pooled_ref[:, c0:c0 + POOL_GC] = pooled_b
            lin = _dot(pooled_b, pw_ref[grp])
            cat_ref[:, dc + c0:dc + c0 + POOL_GC] = (lin * ps_ref[:, c0:c0 + POOL_GC]).astype(LOW)
        x2_ref[...] = xv + _dot(cat_ref[...], wout_ref[...])
        ext_ref[0:HALO, :] = ext_ref[ts:ts + HALO, :]

    tok = pl.BlockSpec((ts, d), lambda t: (t, 0))

    def whole(arr):
        return pl.BlockSpec(arr.shape, lambda t: (0,) * arr.ndim)

    return _call(
        body,
        name=name,
        grid=(s_len // ts,),
        in_specs=[tok, whole(g), _resident(w_in), whole(conv_w), whole(pool_w), whole(pool_scale), _resident(w_out)],
        out_specs=[tok, tok, pl.BlockSpec((ts, p_len), lambda t: (t, 0)),
                   pl.BlockSpec((ts, d_mix - dc), lambda t: (t, 0))],
        out_shape=[
            jax.ShapeDtypeStruct((s_len, d), F32),
            jax.ShapeDtypeStruct((s_len, d), LOW),
            jax.ShapeDtypeStruct((s_len, p_len), F32),
            jax.ShapeDtypeStruct((s_len, d_mix - dc), LOW),
        ],
        scratch_shapes=[pltpu.VMEM((ts + HALO, p_len), F32), pltpu.VMEM((ts, d_mix), LOW)],
        args=(x, g, w_in, conv_w, pool_w, pool_scale, w_out),
        exchange=exchange,
    )


def _mix_bwd(dx2, x, proj, pooled, g, w_in, conv_w, pool_w, pool_scale, w_out, name, exchange=None):
    s_len, d = x.shape
    n_blk, _, wcols = w_in.shape
    p_len = n_blk * wcols
    d_mix = w_out.shape[0]
    ts = min(MIX_TOK_TILE, s_len)
    n_t = s_len // ts
    dc = D_CONV
    n_grp = len(POOL_WINDOWS)

    def body(dx2_ref, x_ref, proj_ref, halo_ref, pooled_ref, g_ref, win_ref, cw_ref, pw_ref, ps_ref, wout_ref,
             dx_ref, dproj_ref, cat_ref, dg_ref, dcw_ref, dpw_ref, dps_ref, df_ref, ext_ref, fut_ref):
        i = pl.program_id(0)
        t = n_t - 1 - i

        @pl.when(i == 0)
        def _():
            dg_ref[...] = jnp.zeros_like(dg_ref)
            dcw_ref[...] = jnp.zeros_like(dcw_ref)
            dpw_ref[...] = jnp.zeros_like(dpw_ref)
            dps_ref[...] = jnp.zeros_like(dps_ref)
            fut_ref[ts:ts + HALO, :] = jnp.zeros((HALO, d_mix), F32)

        ext_ref[HALO:HALO + ts, :] = proj_ref[...]

        @pl.when(t == 0)
        def _():
            ext_ref[0:HALO, :] = jnp.zeros((HALO, p_len), F32)

        @pl.when(t > 0)
        def _():
            ext_ref[0:HALO, :] = halo_ref[...]

        cw = cw_ref[...]
        v, gate_b, gate_c, z0, z1, z2, conv, _, inv_count = _mix_parts(ext_ref, cw, ts, t * ts, with_pooled=False)
        dx2 = dx2_ref[...]
        dcat = _dot_nt(dx2.astype(LOW), wout_ref[...])

        dy_a = dcat[:, 0:dc]
        dconv = dy_a * gate_b
        fut_ref[0:ts, 0:dc] = dconv
        cat_ref[:, 0:dc] = (gate_b * conv).astype(LOW)
        dproj_ref[:, dc:2 * dc] = (dy_a * conv).astype(LOW)
        dcw_ref[2:3, :] += jnp.sum(dconv * z0, axis=0, keepdims=True)
        dcw_ref[1:2, :] += jnp.sum(dconv * z1, axis=0, keepdims=True)
        dcw_ref[0:1, :] += jnp.sum(dconv * z2, axis=0, keepdims=True)

        dpool = []
        for grp in range(n_grp):
            c0 = POOL_GC * grp
            pooled_b = pooled_ref[:, c0:c0 + POOL_GC]
            lin = _dot(pooled_b, pw_ref[grp])
            dy_b = dcat[:, dc + c0:dc + c0 + POOL_GC]
            scale = ps_ref[:, c0:c0 + POOL_GC]
            cat_ref[:, dc + c0:dc + c0 + POOL_GC] = (lin * scale).astype(LOW)
            dps_ref[:, c0:c0 + POOL_GC] += jnp.sum(dy_b * lin, axis=0, keepdims=True)
            dlin = (dy_b * scale).astype(LOW)
            dpw_ref[grp] += _dot_tn(pooled_b, dlin)
            dpool.append(_dot_nt(dlin, pw_ref[grp]))
            fut_ref[0:ts, dc + c0:dc + c0 + POOL_GC] = dpool[grp] * inv_count[grp]

        def ahead(off, c0, c1):
            return fut_ref[off:off + ts, c0:c1]

        dz = cw[2:3, :] * ahead(0, 0, dc) + cw[1:2, :] * ahead(1, 0, dc) + cw[0:1, :] * ahead(2, 0, dc)
        dproj_ref[:, 0:dc] = (dz * gate_c).astype(LOW)
        dproj_ref[:, 2 * dc:3 * dc] = (dz * v).astype(LOW)
        for grp, w in enumerate(POOL_WINDOWS):
            c0 = dc + POOL_GC * grp
            acc = ahead(0, c0, c0 + POOL_GC)
            for j in range(1, w):
                acc = acc + ahead(j, c0, c0 + POOL_GC)
            dproj_ref[:, 2 * dc + c0:2 * dc + c0 + POOL_GC] = (acc - dpool[grp]).astype(LOW)

        dh = _dot_nt(dproj_ref[:, 0:wcols], win_ref[0])
        for k in range(1, n_blk):
            dh += _dot_nt(dproj_ref[:, wcols * k:wcols * (k + 1)], win_ref[k])
        dx, dgp = _rms_bwd(dh, x_ref[...], g_ref[...])
        dx = dx2 + dx
        dx_ref[...] = dx
        df_ref[...] = (0.5 * dx).astype(LOW)
        dg_ref[...] += dgp
        fut_ref[ts:ts + HALO, :] = fut_ref[0:HALO, :]

    tok = pl.BlockSpec((ts, d), lambda i: (n_t - 1 - i, 0))
    halo = pl.BlockSpec((HALO, p_len), lambda i: (jnp.maximum((n_t - 1 - i) * (ts // HALO) - 1, 0), 0))

    def whole(arr):
        return pl.BlockSpec(arr.shape, lambda i: (0,) * arr.ndim)

    return _call(
        body,
        name=name,
        grid=(n_t,),
        in_specs=[tok, tok, pl.BlockSpec((ts, p_len), lambda i: (n_t - 1 - i, 0)), halo,
                  pl.BlockSpec((ts, d_mix - dc), lambda i: (n_t - 1 - i, 0)),
                  whole(g), _resident(w_in), whole(conv_w), whole(pool_w), whole(pool_scale), _resident(w_out)],
        out_specs=[tok, pl.BlockSpec((ts, p_len), lambda i: (n_t - 1 - i, 0)),
                   pl.BlockSpec((ts, d_mix), lambda i: (n_t - 1 - i, 0)),
                   whole(g), whole(conv_w), whole(pool_w), whole(pool_scale), tok],
        out_shape=[
            jax.ShapeDtypeStruct((s_len, d), F32),
            jax.ShapeDtypeStruct((s_len, p_len), LOW),
            jax.ShapeDtypeStruct((s_len, d_mix), LOW),
            jax.ShapeDtypeStruct(g.shape, F32),
            jax.ShapeDtypeStruct(conv_w.shape, F32),
            jax.ShapeDtypeStruct(pool_w.shape, F32),
            jax.ShapeDtypeStruct(pool_scale.shape, F32),
            jax.ShapeDtypeStruct((s_len, d), LOW),
        ],
        scratch_shapes=[pltpu.VMEM((ts + HALO, p_len), F32), pltpu.VMEM((ts + HALO, d_mix), F32)],
        args=(dx2, x, proj, proj, pooled, g, w_in, conv_w, pool_w, pool_scale, w_out),
        exchange=exchange,
    )


def _ffn_fwd_loss(x, h, wgt, wut, wd, g, target, name):
    s_len, d = x.shape
    fc = wd.shape[0]
    ts = min(TOK_TILE, s_len)

    def body(x_ref, h_ref, wg_ref, wu_ref, wd_ref, g_ref, tgt_ref,
             a_ref, b_ref, loss_ref, dx_ref, dg_ref, df_ref, s_ref):
        @pl.when(pl.program_id(0) == 0)
        def _():
            loss_ref[...] = jnp.zeros_like(loss_ref)
            dg_ref[...] = jnp.zeros_like(dg_ref)

        hb = h_ref[...]
        for c0, c1 in _slabs(fc):
            a = _dot_nt(hb, wg_ref[c0:c1, :])
            b = _dot_nt(hb, wu_ref[c0:c1, :])
            s_ref[:, c0:c1] = (a * jax.nn.sigmoid(a) * b).astype(LOW)
            a_ref[:, c0:c1] = a.astype(LOW)
            b_ref[:, c0:c1] = b.astype(LOW)
        xv = x_ref[...] + 0.5 * _dot(s_ref[...], wd_ref[...])
        gv = g_ref[...]
        err = xv * _rms_scale(xv) * gv - tgt_ref[...]
        loss_ref[...] += 0.5 * jnp.sum(jnp.mean(err * err, axis=-1, keepdims=True), axis=0, keepdims=True)
        dx, dgp = _rms_bwd(err * (1.0 / d), xv, gv)
        dx_ref[...] = dx
        df_ref[...] = (0.5 * dx).astype(LOW)
        dg_ref[...] += dgp

    tok = pl.BlockSpec((ts, d), lambda t: (t, 0))
    vec = pl.BlockSpec((1, d), lambda t: (0, 0))
    hid = pl.BlockSpec((ts, fc), lambda t: (t, 0))
    return pl.pallas_call(
        body,
        name=name,
        grid=(s_len // ts,),
        in_specs=[tok, tok, _resident(wgt), _resident(wut), _resident(wd), vec, tok],
        out_specs=[hid, hid, pl.BlockSpec((1, 128), lambda t: (0, 0)), tok, vec, tok],
        out_shape=[
            jax.ShapeDtypeStruct((s_len, fc), LOW),
            jax.ShapeDtypeStruct((s_len, fc), LOW),
            jax.ShapeDtypeStruct((1, 128), F32),
            jax.ShapeDtypeStruct((s_len, d), F32),
            jax.ShapeDtypeStruct((1, d), F32),
            jax.ShapeDtypeStruct((s_len, d), LOW),
        ],
        scratch_shapes=[pltpu.VMEM((ts, fc), LOW)],
        compiler_params=_params("arbitrary"),
    )(x, h, wgt, wut, wd, g, target)


def _row_tile(rows, cols, stack_bytes):
    budget = 20 * 1024 * 1024
    per_row = cols * (4 * 7 + stack_bytes)
    for tr in (rows, 512, 256, 176, 128, 64, 32, 16, 8):
        if rows % tr == 0 and tr % 8 == 0 and tr * per_row * 2 <= budget:
            return tr
    return rows


def _sum_stack(stack, name):
    n, r, c = stack.shape
    tr = _row_tile(r, c, n * stack.dtype.itemsize)

    def body(s_ref, o_ref):
        acc = s_ref[0].astype(F32)
        for k in range(1, n):
            acc = acc + s_ref[k].astype(F32)
        o_ref[...] = acc

    return pl.pallas_call(
        body,
        name=name,
        grid=(r // tr,),
        in_specs=[pl.BlockSpec((n, tr, c), lambda i: (0, i, 0))],
        out_specs=pl.BlockSpec((tr, c), lambda i: (i, 0)),
        out_shape=jax.ShapeDtypeStruct((r, c), F32),
        compiler_params=_params("arbitrary"),
    )(stack)


def _adamw_many(params, name):
    params = [(list(st) if isinstance(st, (list, tuple)) else [st], w, m, v) for st, w, m, v in params]
    stacks0, w0 = params[0][0], params[0][1]
    r, c = w0.shape
    n = stacks0[0].shape[0]
    n_st = len(stacks0)
    part_rows = [st.shape[1] for st in stacks0]
    assert sum(part_rows) == r
    assert all(w.shape == (r, c) and [st.shape for st in sts] == [st.shape for st in stacks0] for sts, w, _, _ in params)
    first_row = [sum(part_rows[:j]) for j in range(n_st)]
    tc = next(t for t in (512, 256, 128) if c % t == 0)
    c1 = 1.0 - ADAM_B1 ** ADAM_STEP
    c2 = 1.0 - ADAM_B2 ** ADAM_STEP
    n_in = n_st + 3

    def body(*refs):
        ins, outs = refs[:n_in * len(params)], refs[n_in * len(params):]
        for p in range(len(params)):
            s_refs = ins[n_in * p:n_in * p + n_st]
            w_ref, m_ref, v_ref = ins[n_in * p + n_st:n_in * (p + 1)]
            g_ref, d_ref, mo_ref, vo_ref = outs[4 * p:4 * p + 4]
            for s_ref, r0, nr in zip(s_refs, first_row, part_rows):
                gv = s_ref[0].astype(F32)
                for k in range(1, n):
                    gv = gv + s_ref[k].astype(F32)
                mn = ADAM_B1 * m_ref[r0:r0 + nr, :] + (1.0 - ADAM_B1) * gv
                vn = ADAM_B2 * v_ref[r0:r0 + nr, :] + (1.0 - ADAM_B2) * (gv * gv)
                g_ref[r0:r0 + nr, :] = gv
                mo_ref[r0:r0 + nr, :] = mn
                vo_ref[r0:r0 + nr, :] = vn
                d_ref[r0:r0 + nr, :] = -ADAM_LR * ((mn / c1) / (jnp.sqrt(vn / c2) + ADAM_EPS)
                                                   + ADAM_WD * w_ref[r0:r0 + nr, :])

    blk = pl.BlockSpec((r, tc), lambda i: (0, i))
    one_in = [pl.BlockSpec((n, nr, tc), lambda i: (0, 0, i)) for nr in part_rows] + [blk, blk, blk]
    res = pl.pallas_call(
        body,
        name=name,
        grid=(c // tc,),
        in_specs=one_in * len(params),
        out_specs=[blk] * (4 * len(params)),
        out_shape=[jax.ShapeDtypeStruct((r, c), F32)] * (4 * len(params)),
        compiler_params=_params("arbitrary"),
    )(*[arr for sts, w, m, v in params for arr in (*sts, w, m, v)])
    return [tuple(res[4 * p:4 * p + 4]) for p in range(len(params))]


def _adamw(stacks, w, m, v, name):
    return _adamw_many([(stacks, w, m, v)], name)[0]


def _to_sheet(parts):
    sheets, spans = [], []
    row = 0
    for p in parts:
        flat = p.reshape(-1).astype(F32)
        rows = -(-flat.shape[0] // 1024) * 8
        flat = jnp.pad(flat, (0, rows * 128 - flat.shape[0]))
        sheets.append(flat.reshape(rows, 128))
        spans.append((row, p.size, p.shape))
        row += rows
    return jnp.concatenate(sheets, axis=0), spans


def _from_sheet(sheet, spans):
    out = []
    for row, size, shape in spans:
        rows = -(-size // 1024) * 8
        out.append(sheet[row:row + rows].reshape(-1)[:size].reshape(shape))
    return out


def kernel(x, norm_ffn1, ffn1_w_gate, ffn1_w_up, ffn1_w_down, norm_mix, w_in, conv_w, pool_w, pool_scale, w_out, norm_ffn2, ffn2_w_gate, ffn2_w_up, ffn2_w_down, norm_final, loss_target, m_norm_ffn1, m_ffn1_w_gate, m_ffn1_w_up, m_ffn1_w_down, m_norm_mix, m_w_in, m_conv_w, m_pool_w, m_pool_scale, m_w_out, m_norm_ffn2, m_ffn2_w_gate, m_ffn2_w_up, m_ffn2_w_down, m_norm_final, v_norm_ffn1, v_ffn1_w_gate, v_ffn1_w_up, v_ffn1_w_down, v_norm_mix, v_w_in, v_conv_w, v_pool_w, v_pool_scale, v_w_out, v_norm_ffn2, v_ffn2_w_gate, v_ffn2_w_up, v_ffn2_w_down, v_norm_final):
    me = 4 * lax.axis_index("x") + 2 * lax.axis_index("y") + lax.axis_index("c")
    xs, tgt = x[0], loss_target[0]
    s_len, d = xs.shape
    f_shard = ffn1_w_down.shape[1]
    conv_shard = conv_w.shape[2]

    def low_t(wt):
        return wt[0].T.astype(LOW)

    def by_dev(gw):
        return gw.reshape(N_DEV, -1, d)

    conv_tile = jnp.zeros((8, 128), F32).at[0:conv_w.shape[1], 0:conv_shard].set(conv_w[0])
    pool_w_low = pool_w[0].astype(LOW)

    rows_a = -(-f_shard // 64) * 32

    def parts_of(w_gate, w_up, w_down):
        shards = [low_t(w_gate), low_t(w_up), w_down[0].astype(LOW)]
        return [(s, 0, rows_a) for s in shards], [(s, rows_a, f_shard - rows_a) for s in shards]

    def rows_flat(stacks):
        return [st.reshape(-1, d) for st in stacks]

    def gather(shards):
        return _Exchange(shards, [False] * len(shards), relay_at=0.6)

    def scatter(pairs):
        return _Exchange(pairs, [True] * len(pairs), among_chips=True)

    w1a_shards, w1b_shards = parts_of(ffn1_w_gate, ffn1_w_up, ffn1_w_down)
    w2a_shards, w2b_shards = parts_of(ffn2_w_gate, ffn2_w_up, ffn2_w_down)

    wg1a, wu1a, wd1a = rows_flat(_exchange(w1a_shards, [False] * 3, "gather_ffn1_a", relay=True))
    xa, h1, a1a, b1a, *w1b = _ffn_fwd(xs, norm_ffn1, None, wg1a, wu1a, wd1a, "ffn1_fwd_a",
                                      exchange=gather(w1b_shards))
    wg1b, wu1b, wd1b = rows_flat(w1b)
    x1, a1b, b1b, w_in_full, w_out_full, conv_tiles = _ffn_fwd(
        xa, None, h1, wg1b, wu1b, wd1b, "ffn1_fwd_b",
        exchange=gather([w_in[0].astype(LOW), w_out[0].astype(LOW), conv_tile]))
    w_out_full = w_out_full.reshape(-1, d)
    conv_full = jnp.concatenate([conv_tiles[k, 0:conv_w.shape[1], 0:conv_shard] for k in range(N_DEV)], axis=1)
    x2, h2, proj, pooled, *w2a = _mix_fwd(x1, norm_mix, w_in_full, conv_full, pool_w_low, pool_scale, w_out_full,
                                          "mix_fwd", exchange=gather(w2a_shards))
    wg2a, wu2a, wd2a = rows_flat(w2a)
    xb, h3, a2a, b2a, *w2b = _ffn_fwd(x2, norm_ffn2, None, wg2a, wu2a, wd2a, "ffn2_fwd_a",
                                      exchange=gather(w2b_shards))
    wg2b, wu2b, wd2b = rows_flat(w2b)
    a2b, b2b, loss_row, dx3, dg_final, df3 = _ffn_fwd_loss(
        xb, h3, wg2b, wu2b, wd2b, norm_final.reshape(1, d), tgt, "ffn2_fwd_b_loss")

    da2a, db2a, dwd_a, dwg_a, dwu_a = _ffn_bwdw(df3, a2a, b2a, h3, wd2a, "ffn2_bwdw_a")
    da2b, db2b, dwd_b, dwg_b, dwu_b = _ffn_bwdw(df3, a2b, b2b, h3, wd2b, "ffn2_bwdw_b")
    pairs = _pair_sum([by_dev(dwg_a), by_dev(dwu_a), by_dev(dwd_a), by_dev(dwg_b), by_dev(dwu_b), by_dev(dwd_b)],
                      "pair_sum_ffn2")
    dx2, dg_ffn2, *got_2a = _ffn_dx(dx3, x2, norm_ffn2, [(da2a, db2a, wg2a, wu2a), (da2b, db2b, wg2b, wu2b)],
                                    "ffn2_dx", exchange=scatter(pairs[:3]))
    dx1, dproj, cat, dg_mix, dconv, dpool_w, dpool_scale, df1, *got_2b = _mix_bwd(
        dx2, x1, proj, pooled, norm_mix, w_in_full, conv_full, pool_w_low, pool_scale, w_out_full, "mix_bwd",
        exchange=scatter(pairs[3:]))
    small_parts = [dg_mix, dg_ffn2, dg_final, dconv, dpool_w, dpool_scale, loss_row]
    small_sheet, spans = _to_sheet(small_parts)
    dw_in, got_small = _wgrad_tn(h2, dproj, W_IN_SHARD, True, "w_in_wgrad",
                                 exchange=_Exchange([small_sheet], [False], relay_at=0.5))
    (dw_out,) = _wgrad_tn(cat, dx2, d, False, "w_out_wgrad")
    pairs = _pair_sum([dw_in, by_dev(dw_out)], "pair_sum_mix")
    da1a, db1a, dwd, dwg, dwu, got_in, got_out = _ffn_bwdw(
        df1, a1a, b1a, h1, wd1a, "ffn1_bwdw_a", exchange=scatter(pairs))
    pairs = _pair_sum([by_dev(dwg), by_dev(dwu), by_dev(dwd)], "pair_sum_ffn1_a")
    da1b, db1b, dwd, dwg, dwu, *got_1a = _ffn_bwdw(df1, a1b, b1b, h1, wd1b, "ffn1_bwdw_b", exchange=scatter(pairs))
    pairs = _pair_sum([by_dev(dwg), by_dev(dwu), by_dev(dwd)], "pair_sum_ffn1_b")
    dx0, dg_ffn1, *got_1b = _ffn_dx(dx1, xs, norm_ffn1, [(da1a, db1a, wg1a, wu1a), (da1b, db1b, wg1b, wu1b)],
                                    "ffn1_dx", exchange=scatter(pairs))
    (got_n1,) = _exchange([dg_ffn1.reshape(8, 128)], [False], "gather_dnorm_ffn1")

    outs = {}

    def update(name, stacks, w, m, v):
        outs[name] = _adamw(stacks, w[0], m[0], v[0], "adamw_" + name)

    def update_ffn(prefix, got_a, got_b, gate, up, down):
        res = _adamw_many(
            [([got_a[j], got_b[j]], *[(t[0].T if j < 2 else t[0]) for t in wmv]) for j, wmv in enumerate((gate, up, down))],
            "adamw_" + prefix)
        outs[prefix + "_w_gate"] = tuple(r.T for r in res[0])
        outs[prefix + "_w_up"] = tuple(r.T for r in res[1])
        outs[prefix + "_w_down"] = res[2]

    update_ffn("ffn1", got_1a, got_1b, (ffn1_w_gate, m_ffn1_w_gate, v_ffn1_w_gate),
               (ffn1_w_up, m_ffn1_w_up, v_ffn1_w_up), (ffn1_w_down, m_ffn1_w_down, v_ffn1_w_down))
    update_ffn("ffn2", got_2a, got_2b, (ffn2_w_gate, m_ffn2_w_gate, v_ffn2_w_gate),
               (ffn2_w_up, m_ffn2_w_up, v_ffn2_w_up), (ffn2_w_down, m_ffn2_w_down, v_ffn2_w_down))
    update("w_in", got_in, w_in, m_w_in, v_w_in)
    update("w_out", got_out, w_out, m_w_out, v_w_out)

    g_small = _from_sheet(_sum_stack(got_small, "sum_small"), spans)
    g_norm_ffn1 = _sum_stack(got_n1, "sum_dnorm_ffn1").reshape(norm_ffn1.shape)
    g_conv = lax.dynamic_slice_in_dim(g_small[3], me * conv_shard, conv_shard, axis=1)
    small_names = ["norm_ffn1", "norm_mix", "norm_ffn2", "norm_final", "conv_w", "pool_w", "pool_scale"]
    small_g = [g_norm_ffn1, g_small[0], g_small[1], g_small[2].reshape(norm_final.shape), g_conv[None],
               g_small[4][None], g_small[5]]
    small_w = [norm_ffn1, norm_mix, norm_ffn2, norm_final, conv_w, pool_w, pool_scale]
    small_m = [m_norm_ffn1, m_norm_mix, m_norm_ffn2, m_norm_final, m_conv_w, m_pool_w, m_pool_scale]
    small_v = [v_norm_ffn1, v_norm_mix, v_norm_ffn2, v_norm_final, v_conv_w, v_pool_w, v_pool_scale]
    g_sheet, spans_u = _to_sheet(small_g)
    w_sheet, _ = _to_sheet(small_w)
    m_sheet, _ = _to_sheet(small_m)
    v_sheet, _ = _to_sheet(small_v)
    upd = _adamw(g_sheet[None], w_sheet, m_sheet, v_sheet, "adamw_small")
    small_out = [_from_sheet(u, spans_u) for u in upd]
    for k, nm in enumerate(small_names):
        outs[nm] = tuple(small_out[j][k] for j in range(4))

    loss = g_small[6][0, 0]
    order = ["norm_ffn1", "ffn1_w_gate", "ffn1_w_up", "ffn1_w_down", "norm_mix", "w_in", "conv_w", "pool_w",
             "pool_scale", "w_out", "norm_ffn2", "ffn2_w_gate", "ffn2_w_up", "ffn2_w_down", "norm_final"]
    big = {"ffn1_w_gate", "ffn1_w_up", "ffn1_w_down", "w_in", "w_out", "ffn2_w_gate", "ffn2_w_up", "ffn2_w_down"}

    def leaf(nm, j):
        val = outs[nm][j]
        return val[None] if nm in big else val

    return (loss, dx0[None],
            *[leaf(nm, 0) for nm in order], *[leaf(nm, 1) for nm in order],
            *[leaf(nm, 2) for nm in order], *[leaf(nm, 3) for nm in order])
```

```python
import jax
import jax.numpy as jnp
from jax import lax
from jax.experimental import pallas as pl
from jax.experimental.pallas import tpu as pltpu

F32 = jnp.float32
LOW = jnp.bfloat16

N_DEV = 8
EPS = 1e-6
D_CONV = 512
POOL_WINDOWS = (2, 4, 8, 16)
POOL_GC = 128
HALO = 16
W_IN_SHARD = 256

ADAM_LR = 0.001
ADAM_B1 = 0.9
ADAM_B2 = 0.999
ADAM_EPS = 1e-08
ADAM_WD = 0.01
ADAM_STEP = 10

VMEM_LIMIT_BYTES = 56 * 1024 * 1024
TOK_TILE = 512
MIX_TOK_TILE = 256
WGRAD_TOK_TILE = 4096
WGRAD_ROW_CANDIDATES = (256, 128)
RELAY_LAST_LATER = 0.25
BWD_ROW_SLAB = 2048


def _params(*sem):
    return pltpu.CompilerParams(dimension_semantics=sem, vmem_limit_bytes=VMEM_LIMIT_BYTES)


def _resident(arr):
    return pl.BlockSpec(arr.shape, lambda *_: (0,) * arr.ndim, pipeline_mode=pl.Buffered(1))


def _pick(n, candidates):
    for c in candidates:
        if n % c == 0:
            return c
    raise ValueError(f"no tile in {candidates} divides {n}")


def _dot(a, b):
    return lax.dot_general(a, b, (((1,), (0,)), ((), ())), preferred_element_type=F32)


def _dot_nt(a, b):
    return lax.dot_general(a, b, (((1,), (1,)), ((), ())), preferred_element_type=F32)


def _dot_tn(a, b):
    return lax.dot_general(a, b, (((0,), (0,)), ((), ())), preferred_element_type=F32)


def _rms_scale(x):
    return lax.rsqrt(jnp.mean(x * x, axis=-1, keepdims=True) + EPS)


def _rms_bwd(dy, x, g):
    r = _rms_scale(x)
    xhat = x * r
    gdy = dy * g
    dx = r * (gdy - xhat * jnp.mean(gdy * xhat, axis=-1, keepdims=True))
    return dx, jnp.sum(dy * xhat, axis=0, keepdims=True)


COLLECTIVE_PAIR, COLLECTIVE_CHIPS, COLLECTIVE_RELAY = 0, 1, 2


def _handshake(peer_numbers):
    mx, my, mc = lax.axis_index("x"), lax.axis_index("y"), lax.axis_index("c")
    barrier = pltpu.get_barrier_semaphore()
    for m in peer_numbers:
        peer = (lax.rem(mx + ((m >> 2) & 1), 2), lax.rem(my + ((m >> 1) & 1), 2), lax.rem(mc + (m & 1), 2))
        pl.semaphore_signal(barrier, inc=1, device_id=peer, device_id_type=pl.DeviceIdType.MESH)
    pl.semaphore_wait(barrier, len(peer_numbers))


class _Exchange:
    CHIPS = (2, 4, 6)

    def __init__(self, arrays, sliced, relay_at=None, among_chips=False):
        assert relay_at is None or not any(sliced)
        assert not among_chips or (all(sliced) and relay_at is None)
        self.relay_at, self.among_chips = relay_at, among_chips
        self.peers = self.CHIPS if among_chips else ((1, 2, 4) if relay_at is not None else tuple(range(1, N_DEV)))
        self.collective_id = COLLECTIVE_CHIPS if among_chips else (COLLECTIVE_RELAY if relay_at is not None else None)
        self.rows = [arr[1:] if isinstance(arr, tuple) else None for arr in arrays]
        self.arrays = [arr[0] if isinstance(arr, tuple) else arr for arr in arrays]
        self.sliced, self.n = list(sliced), len(arrays)
        assert all(rg is None or not sl for rg, sl in zip(self.rows, sliced))
        self.block_shape = [arr.shape if rg is None else (rg[1],) + arr.shape[1:]
                            for arr, rg in zip(self.arrays, self.rows)]
        self.out_shape = [jax.ShapeDtypeStruct(shape if sl else (N_DEV,) + shape, arr.dtype)
                          for arr, shape, sl in zip(self.arrays, self.block_shape, sliced)]
        self.specs = [pl.BlockSpec(memory_space=pl.ANY)] * self.n
        self.scratch_shapes = [pltpu.SemaphoreType.DMA((self.n, N_DEV)),
                               pltpu.SemaphoreType.DMA((self.n, N_DEV)),
                               pltpu.SemaphoreType.DMA((self.n,))]

    HALF_VIA = ((4, 2, 5), (2, 4, 7))

    def _halves(self, a):
        rows = self.block_shape[a][0]
        if rows % 32:
            return ((0, rows), None)
        return ((0, rows // 2), (rows // 2, rows // 2))

    def _copies(self, ins, outs, sems):
        send_sems, recv_sems, local_sems = sems
        sliced = self.sliced
        mx, my, mc = lax.axis_index("x"), lax.axis_index("y"), lax.axis_index("c")
        me = 2 * mx + my if self.among_chips else 4 * mx + 2 * my + mc

        def peer(m):
            px = lax.rem(mx + ((m >> 2) & 1), 2)
            py = lax.rem(my + ((m >> 1) & 1), 2)
            pc = lax.rem(mc + (m & 1), 2)
            return (px, py, pc), (2 * px + py if self.among_chips else 4 * px + 2 * py + pc)

        def mine(a):
            return ins[a] if self.rows[a] is None else ins[a].at[pl.ds(*self.rows[a])]

        def remote(a, m, arriving):
            pid, pflat = peer(m)
            return pltpu.make_async_remote_copy(
                src_ref=ins[a].at[pflat] if sliced[a] else mine(a),
                dst_ref=outs[a].at[pflat if arriving else me],
                send_sem=send_sems.at[a, m - 1],
                recv_sem=recv_sems.at[a, m - 1],
                device_id=pid,
                device_id_type=pl.DeviceIdType.MESH,
            )

        def local(a):
            return pltpu.make_async_copy(ins[a].at[me] if sliced[a] else mine(a), outs[a].at[me], local_sems.at[a])

        def passed_on(a, m):
            _, origin = peer(m)
            sibling, _ = peer(1)
            return pltpu.make_async_remote_copy(
                src_ref=outs[a].at[origin],
                dst_ref=outs[a].at[origin],
                send_sem=send_sems.at[a, m],
                recv_sem=recv_sems.at[a, m],
                device_id=sibling,
                device_id_type=pl.DeviceIdType.MESH,
            )

        def half_on(a, h, arriving):
            via, to, column = self.HALF_VIA[h]
            r0, nr = self._halves(a)[h]
            _, origin = peer(6 if arriving else via)
            rows = outs[a].at[origin].at[pl.ds(r0, nr)]
            return pltpu.make_async_remote_copy(
                src_ref=rows, dst_ref=rows, send_sem=send_sems.at[a, column], recv_sem=recv_sems.at[a, column],
                device_id=peer(to)[0], device_id_type=pl.DeviceIdType.MESH)

        return remote, local, passed_on, half_on

    def start(self, ins, outs, sems):
        remote, local, _, _ = self._copies(ins, outs, sems)
        if self.collective_id is not None:
            _handshake(self.peers)
        for a in range(self.n):
            local(a).start()
        for m in self.peers:
            for a in range(self.n):
                remote(a, m, False).start()

    def relay(self, ins, outs, sems):
        remote, _, passed_on, half_on = self._copies(ins, outs, sems)
        for h, (via, _, _) in enumerate(self.HALF_VIA):
            for a in range(self.n):
                remote(a, via, True).wait_recv()
                passed_on(a, via).start()
                if self._halves(a)[h] is not None:
                    half_on(a, h, False).start()

    def relay_last(self, ins, outs, sems):
        _, _, passed_on, half_on = self._copies(ins, outs, sems)
        for a in range(self.n):
            for h in range(2):
                if self._halves(a)[h] is not None:
                    half_on(a, h, True).wait_recv()
            passed_on(a, 6).start()

    def wait(self, ins, outs, sems):
        remote, local, passed_on, half_on = self._copies(ins, outs, sems)
        if self.relay_at is None:
            for m in self.peers:
                for a in range(self.n):
                    remote(a, m, True).wait_recv()
            for m in self.peers:
                for a in range(self.n):
                    remote(a, m, False).wait_send()
        else:
            for m in (1, 3, 5, 7):
                for a in range(self.n):
                    remote(a, m, True).wait_recv()
            for m in self.peers:
                for a in range(self.n):
                    remote(a, m, False).wait_send()
            for a in range(self.n):
                for m in self.CHIPS:
                    passed_on(a, m).wait_send()
                for h in range(2):
                    if self._halves(a)[h] is not None:
                        half_on(a, h, False).wait_send()
        for a in range(self.n):
            local(a).wait()


def _pair_sum(stacks, name):
    n = len(stacks)
    n_chip = N_DEV // 2
    half = [(n_chip,) + st.shape[1:] for st in stacks]

    def body(*refs):
        ins, outs, mine, theirs = refs[:n], refs[n:2 * n], refs[2 * n:3 * n], refs[3 * n:4 * n]
        local_sems, send_sems, recv_sems = refs[4 * n:]
        mx, my, mc = lax.axis_index("x"), lax.axis_index("y"), lax.axis_index("c")

        def own(a, k):
            return pltpu.make_async_copy(ins[a].at[2 * k + mc], mine[a].at[k], local_sems.at[a, k])

        def swap(a, k):
            return pltpu.make_async_remote_copy(
                src_ref=ins[a].at[2 * k + (1 - mc)], dst_ref=theirs[a].at[k],
                send_sem=send_sems.at[a, k], recv_sem=recv_sems.at[a, k],
                device_id=(mx, my, 1 - mc), device_id_type=pl.DeviceIdType.MESH)

        _handshake((1,))
        for k in range(n_chip):
            for a in range(n):
                own(a, k).start()
                swap(a, k).start()
        for k in range(n_chip):
            for a in range(n):
                own(a, k).wait()
                swap(a, k).wait()
                outs[a][k] = (mine[a][k].astype(F32) + theirs[a][k].astype(F32)).astype(LOW)

    return pl.pallas_call(
        body, name=name,
        out_shape=[jax.ShapeDtypeStruct(h, LOW) for h in half],
        in_specs=[pl.BlockSpec(memory_space=pl.ANY)] * n,
        out_specs=[pl.BlockSpec(memory_space=pltpu.VMEM)] * n,
        scratch_shapes=([pltpu.VMEM(h, st.dtype) for h, st in zip(half, stacks)] * 2
                        + [pltpu.SemaphoreType.DMA((n, n_chip))] * 3),
        compiler_params=pltpu.CompilerParams(vmem_limit_bytes=VMEM_LIMIT_BYTES, collective_id=COLLECTIVE_PAIR),
    )(*stacks)


def _exchange(arrays, sliced, name, relay=False):
    ex = _Exchange(arrays, sliced, relay_at=0 if relay else None)

    def body(*refs):
        ins, outs, sems = refs[:ex.n], refs[ex.n:2 * ex.n], refs[2 * ex.n:]
        ex.start(ins, outs, sems)
        if relay:
            ex.relay(ins, outs, sems)
            ex.relay_last(ins, outs, sems)
        ex.wait(ins, outs, sems)

    return pl.pallas_call(body, name=name, out_shape=ex.out_shape, in_specs=ex.specs, out_specs=ex.specs,
                          scratch_shapes=ex.scratch_shapes,
                          compiler_params=pltpu.CompilerParams(collective_id=ex.collective_id))(*ex.arrays)


def _call(body, *, name, grid, in_specs, out_specs, out_shape, args, scratch_shapes=(), exchange=None):
    params = _params(*(("arbitrary",) * len(grid)))
    if exchange is None:
        return pl.pallas_call(body, name=name, grid=grid, in_specs=in_specs, out_specs=out_specs, out_shape=out_shape,
                              scratch_shapes=list(scratch_shapes), compiler_params=params)(*args)
    exs = list(exchange) if isinstance(exchange, (list, tuple)) else [exchange]
    assert len(exs) == 1 or all(ex.collective_id is None for ex in exs)
    params = pltpu.CompilerParams(dimension_semantics=("arbitrary",) * len(grid), vmem_limit_bytes=VMEM_LIMIT_BYTES,
                                  collective_id=exs[0].collective_id)
    n_in, n_out, n_scr = len(in_specs), len(out_specs), len(scratch_shapes)
    n_ex = sum(ex.n for ex in exs)
    n_steps = 1
    for g in grid:
        n_steps *= g

    def hosted(*refs):
        ins, refs = refs[:n_in], refs[n_in:]
        ex_ins, refs = refs[:n_ex], refs[n_ex:]
        outs, refs = refs[:n_out], refs[n_out:]
        ex_outs, refs = refs[:n_ex], refs[n_ex:]
        scr, sems = refs[:n_scr], refs[n_scr:]
        parts, at = [], 0
        for j, ex in enumerate(exs):
            parts.append((ex_ins[at:at + ex.n], ex_outs[at:at + ex.n], sems[3 * j:3 * j + 3]))
            at += ex.n
        step = pl.program_id(0)
        for ax in range(1, len(grid)):
            step = step * grid[ax] + pl.program_id(ax)

        @pl.when(step == 0)
        def _():
            for ex, part in zip(exs, parts):
                ex.start(*part)

        body(*ins, *outs, *scr)

        for ex, part in zip(exs, parts):
            if ex.relay_at is not None:
                @pl.when(step == min(int(ex.relay_at * n_steps), n_steps - 1))
                def _():
                    ex.relay(*part)

                @pl.when(step == min(int((ex.relay_at + RELAY_LAST_LATER) * n_steps), n_steps - 1))
                def _():
                    ex.relay_last(*part)

        @pl.when(step == n_steps - 1)
        def _():
            for ex, part in zip(exs, parts):
                ex.wait(*part)

    return pl.pallas_call(
        hosted, name=name, grid=grid,
        in_specs=list(in_specs) + [sp for ex in exs for sp in ex.specs],
        out_specs=list(out_specs) + [sp for ex in exs for sp in ex.specs],
        out_shape=list(out_shape) + [sh for ex in exs for sh in ex.out_shape],
        scratch_shapes=list(scratch_shapes) + [sc for ex in exs for sc in ex.scratch_shapes],
        compiler_params=params)(*args, *[arr for ex in exs for arr in ex.arrays])


def _ffn_fwd(x, g, h, wgt, wut, wd, name, exchange=None):
    s_len, d = x.shape
    fc = wd.shape[0]
    ts = min(TOK_TILE, s_len)
    first = h is None

    def body(*refs):
        x_ref, gh_ref, wg_ref, wu_ref, wd_ref, xo_ref = refs[:6]
        a_ref, b_ref, s_ref = refs[-3:]
        xv = x_ref[...]
        if first:
            hb = (xv * _rms_scale(xv) * gh_ref[...]).astype(LOW)
            refs[6][...] = hb
        else:
            hb = gh_ref[...]
        for c0, c1 in _slabs(fc):
            a = _dot_nt(hb, wg_ref[c0:c1, :])
            b = _dot_nt(hb, wu_ref[c0:c1, :])
            s_ref[:, c0:c1] = (a * jax.nn.sigmoid(a) * b).astype(LOW)
            a_ref[:, c0:c1] = a.astype(LOW)
            b_ref[:, c0:c1] = b.astype(LOW)
        xo_ref[...] = xv + 0.5 * _dot(s_ref[...], wd_ref[...])

    tok = pl.BlockSpec((ts, d), lambda t: (t, 0))
    hid = pl.BlockSpec((ts, fc), lambda t: (t, 0))
    tok_out = jax.ShapeDtypeStruct((s_len, d), F32)
    h_out = jax.ShapeDtypeStruct((s_len, d), LOW)
    hid_out = jax.ShapeDtypeStruct((s_len, fc), LOW)
    return _call(
        body,
        name=name,
        grid=(s_len // ts,),
        in_specs=[tok, pl.BlockSpec((1, d), lambda t: (0, 0)) if first else tok,
                  _resident(wgt), _resident(wut), _resident(wd)],
        out_specs=[tok] + ([tok] if first else []) + [hid, hid],
        out_shape=[tok_out] + ([h_out] if first else []) + [hid_out, hid_out],
        scratch_shapes=[pltpu.VMEM((ts, fc), LOW)],
        args=(x, g if first else h, wgt, wut, wd),
        exchange=exchange,
    )


def _slabs(width, slab=256):
    return [(c0, min(c0 + slab, width)) for c0 in range(0, width, slab)]


def _ffn_bwdw(df, a, b, h, wd, name, exchange=None):
    s_len, d = df.shape
    f_len = wd.shape[0]
    tm = _pick(f_len, WGRAD_ROW_CANDIDATES)
    tk = min(WGRAD_TOK_TILE, s_len)
    n_k = s_len // tk

    def body(df_ref, a_ref, b_ref, h_ref, wd_ref, da_ref, db_ref, dwd_ref, dwg_ref, dwu_ref,
             s_ref, acc_d, acc_g, acc_u):
        k = pl.program_id(1)

        @pl.when(k == 0)
        def _():
            acc_d[...] = jnp.zeros_like(acc_d)
            acc_g[...] = jnp.zeros_like(acc_g)
            acc_u[...] = jnp.zeros_like(acc_u)

        wdv = wd_ref[...]
        for r0, r1 in _slabs(tk, BWD_ROW_SLAB):
            ds = _dot_nt(df_ref[r0:r1, :], wdv)
            av = a_ref[r0:r1, :].astype(F32)
            bv = b_ref[r0:r1, :].astype(F32)
            sig = jax.nn.sigmoid(av)
            silu = av * sig
            s_ref[r0:r1, :] = (silu * bv).astype(LOW)
            da_ref[r0:r1, :] = (ds * bv * (sig * (1.0 + av * (1.0 - sig)))).astype(LOW)
            db_ref[r0:r1, :] = (ds * silu).astype(LOW)
            hv = h_ref[r0:r1, :]
            acc_d[...] += _dot_tn(s_ref[r0:r1, :], df_ref[r0:r1, :])
            acc_g[...] += _dot_tn(da_ref[r0:r1, :], hv)
            acc_u[...] += _dot_tn(db_ref[r0:r1, :], hv)

        @pl.when(k == n_k - 1)
        def _():
            dwd_ref[...] = acc_d[...].astype(LOW)
            dwg_ref[...] = acc_g[...].astype(LOW)
            dwu_ref[...] = acc_u[...].astype(LOW)

    hid = pl.BlockSpec((tk, tm), lambda i, k: (k, i))
    tok = pl.BlockSpec((tk, d), lambda i, k: (k, 0))
    wrow = pl.BlockSpec((tm, d), lambda i, k: (i, 0))
    return _call(
        body,
        name=name,
        grid=(f_len // tm, n_k),
        in_specs=[tok, hid, hid, tok, wrow],
        out_specs=[hid, hid, wrow, wrow, wrow],
        out_shape=[jax.ShapeDtypeStruct((s_len, f_len), LOW)] * 2 + [jax.ShapeDtypeStruct((f_len, d), LOW)] * 3,
        scratch_shapes=[pltpu.VMEM((tk, tm), LOW)] + [pltpu.VMEM((tm, d), F32)] * 3,
        args=(df, a, b, h, wd),
        exchange=exchange,
    )


def _ffn_dx(dxo, x, g, parts, name, exchange=None):
    s_len, d = x.shape
    ts = min(TOK_TILE, s_len)
    n_p = len(parts)

    def body(dxo_ref, x_ref, g_ref, *refs):
        dxi_ref, dg_ref = refs[4 * n_p:]

        @pl.when(pl.program_id(0) == 0)
        def _():
            dg_ref[...] = jnp.zeros_like(dg_ref)

        dh = None
        for p in range(n_p):
            da_ref, db_ref, wg_ref, wu_ref = refs[4 * p:4 * p + 4]
            part = _dot(da_ref[...], wg_ref[...]) + _dot(db_ref[...], wu_ref[...])
            dh = part if dh is None else dh + part
        dx, dgp = _rms_bwd(dh, x_ref[...], g_ref[...])
        dxi_ref[...] = dxo_ref[...] + dx
        dg_ref[...] += dgp

    tok = pl.BlockSpec((ts, d), lambda t: (t, 0))
    vec = pl.BlockSpec((1, d), lambda t: (0, 0))
    part_specs, part_args = [], []
    for da, db, wgt, wut in parts:
        hid = pl.BlockSpec((ts, da.shape[1]), lambda t: (t, 0))
        part_specs += [hid, hid, _resident(wgt), _resident(wut)]
        part_args += [da, db, wgt, wut]
    return _call(
        body,
        name=name,
        grid=(s_len // ts,),
        in_specs=[tok, tok, vec] + part_specs,
        out_specs=[tok, vec],
        out_shape=[jax.ShapeDtypeStruct((s_len, d), F32), jax.ShapeDtypeStruct((1, d), F32)],
        args=(dxo, x, g, *part_args),
        exchange=exchange,
    )


def _wgrad_tn(xm, ym, tn, stacked, name, exchange=None):
    s_len, m = xm.shape
    n = ym.shape[1]
    tk = min(WGRAD_TOK_TILE, s_len)
    n_k = s_len // tk

    def body(x_ref, y_ref, o_ref, acc):
        k = pl.program_id(1)

        @pl.when(k == 0)
        def _():
            acc[...] = jnp.zeros_like(acc)

        acc[...] += _dot_tn(x_ref[...].astype(LOW), y_ref[...].astype(LOW))

        @pl.when(k == n_k - 1)
        def _():
            o_ref[...] = acc[...].astype(LOW)

    if stacked:
        out_spec = pl.BlockSpec((None, m, tn), lambda j, k: (j, 0, 0))
        out_shape = jax.ShapeDtypeStruct((n // tn, m, tn), LOW)
    else:
        out_spec = pl.BlockSpec((m, tn), lambda j, k: (0, j))
        out_shape = jax.ShapeDtypeStruct((m, n), LOW)
    return _call(
        body,
        name=name,
        grid=(n // tn, n_k),
        in_specs=[pl.BlockSpec((tk, m), lambda j, k: (k, 0)), pl.BlockSpec((tk, tn), lambda j, k: (k, j))],
        out_specs=[out_spec],
        out_shape=[out_shape],
        scratch_shapes=[pltpu.VMEM((m, tn), F32)],
        args=(xm, ym),
        exchange=exchange,
    )


def _mix_parts(ext_ref, cw, ts, row0, with_pooled=True):
    dc = D_CONV

    def back(off, c0, c1):
        return ext_ref[HALO - off:HALO - off + ts, c0:c1]

    v, gate_b, gate_c = back(0, 0, dc), back(0, dc, 2 * dc), back(0, 2 * dc, 3 * dc)
    z0 = gate_c * v
    z1 = back(1, 2 * dc, 3 * dc) * back(1, 0, dc)
    z2 = back(2, 2 * dc, 3 * dc) * back(2, 0, dc)
    conv = cw[2:3, :] * z0 + cw[1:2, :] * z1 + cw[0:1, :] * z2
    rows = row0 + lax.broadcasted_iota(jnp.int32, (ts, 1), 0)
    pooled, inv_count = [], []
    for grp, w in enumerate(POOL_WINDOWS):
        inv = 1.0 / jnp.minimum(rows + 1, w).astype(F32)
        inv_count.append(inv)
        if with_pooled:
            c0 = 3 * dc + POOL_GC * grp
            u = back(0, c0, c0 + POOL_GC)
            acc = u
            for j in range(1, w):
                acc = acc + back(j, c0, c0 + POOL_GC)
            pooled.append(acc * inv - u)
    return v, gate_b, gate_c, z0, z1, z2, conv, pooled, inv_count


def _mix_fwd(x, g, w_in, conv_w, pool_w, pool_scale, w_out, name, exchange=None):
    s_len, d = x.shape
    n_blk, _, wcols = w_in.shape
    p_len = n_blk * wcols
    d_mix = w_out.shape[0]
    ts = min(MIX_TOK_TILE, s_len)
    dc = D_CONV

    def body(x_ref, g_ref, win_ref, cw_ref, pw_ref, ps_ref, wout_ref, x2_ref, h_ref, proj_ref, pooled_ref,
             ext_ref, cat_ref):
        t = pl.program_id(0)

        @pl.when(t == 0)
        def _():
            ext_ref[0:HALO, :] = jnp.zeros((HALO, p_len), F32)

        xv = x_ref[...]
        hb = (xv * _rms_scale(xv) * g_ref[...]).astype(LOW)
        h_ref[...] = hb
        for k in range(n_blk):
            ext_ref[HALO:HALO + ts, wcols * k:wcols * (k + 1)] = _dot(hb, win_ref[k])
        proj_ref[...] = ext_ref[HALO:HALO + ts, :]

        _, gate_b, _, _, _, _, conv, pooled, _ = _mix_parts(ext_ref, cw_ref[...], ts, t * ts)
        cat_ref[:, 0:dc] = (gate_b * conv).astype(LOW)
        for grp in range(len(POOL_WINDOWS)):
            c0 = POOL_GC * grp
            pooled_b = pooled[grp].astype(LOW)
            pooled_ref[:, c0:c0 + POOL_GC] = pooled_b
            lin = _dot(pooled_b, pw_ref[grp])
            cat_ref[:, dc + c0:dc + c0 + POOL_GC] = (lin * ps_ref[:, c0:c0 + POOL_GC]).astype(LOW)
        x2_ref[...] = xv + _dot(cat_ref[...], wout_ref[...])
        ext_ref[0:HALO, :] = ext_ref[ts:ts + HALO, :]

    tok = pl.BlockSpec((ts, d), lambda t: (t, 0))

    def whole(arr):
        return pl.BlockSpec(arr.shape, lambda t: (0,) * arr.ndim)

    return _call(
        body,
        name=name,
        grid=(s_len // ts,),
        in_specs=[tok, whole(g), _resident(w_in), whole(conv_w), whole(pool_w), whole(pool_scale), _resident(w_out)],
        out_specs=[tok, tok, pl.BlockSpec((ts, p_len), lambda t: (t, 0)),
                   pl.BlockSpec((ts, d_mix - dc), lambda t: (t, 0))],
        out_shape=[
            jax.ShapeDtypeStruct((s_len, d), F32),
            jax.ShapeDtypeStruct((s_len, d), LOW),
            jax.ShapeDtypeStruct((s_len, p_len), F32),
            jax.ShapeDtypeStruct((s_len, d_mix - dc), LOW),
        ],
        scratch_shapes=[pltpu.VMEM((ts + HALO, p_len), F32), pltpu.VMEM((ts, d_mix), LOW)],
        args=(x, g, w_in, conv_w, pool_w, pool_scale, w_out),
        exchange=exchange,
    )


def _mix_bwd(dx2, x, proj, pooled, g, w_in, conv_w, pool_w, pool_scale, w_out, name, exchange=None):
    s_len, d = x.shape
    n_blk, _, wcols = w_in.shape
    p_len = n_blk * wcols
    d_mix = w_out.shape[0]
    ts = min(MIX_TOK_TILE, s_len)
    n_t = s_len // ts
    dc = D_CONV
    n_grp = len(POOL_WINDOWS)

    def body(dx2_ref, x_ref, proj_ref, halo_ref, pooled_ref, g_ref, win_ref, cw_ref, pw_ref, ps_ref, wout_ref,
             dx_ref, dproj_ref, dwout_ref, dg_ref, dcw_ref, dpw_ref, dps_ref, df_ref,
             ext_ref, fut_ref, cat_ref, dwout_acc):
        i = pl.program_id(0)
        t = n_t - 1 - i

        @pl.when(i == 0)
        def _():
            dwout_acc[...] = jnp.zeros_like(dwout_acc)
            dg_ref[...] = jnp.zeros_like(dg_ref)
            dcw_ref[...] = jnp.zeros_like(dcw_ref)
            dpw_ref[...] = jnp.zeros_like(dpw_ref)
            dps_ref[...] = jnp.zeros_like(dps_ref)
            fut_ref[ts:ts + HALO, :] = jnp.zeros((HALO, d_mix), F32)

        ext_ref[HALO:HALO + ts, :] = proj_ref[...]

        @pl.when(t == 0)
        def _():
            ext_ref[0:HALO, :] = jnp.zeros((HALO, p_len), F32)

        @pl.when(t > 0)
        def _():
            ext_ref[0:HALO, :] = halo_ref[...]

        cw = cw_ref[...]
        v, gate_b, gate_c, z0, z1, z2, conv, _, inv_count = _mix_parts(ext_ref, cw, ts, t * ts, with_pooled=False)
        dx2 = dx2_ref[...]
        dcat = _dot_nt(dx2.astype(LOW), wout_ref[...])

        dy_a = dcat[:, 0:dc]
        dconv = dy_a * gate_b
        fut_ref[0:ts, 0:dc] = dconv
        cat_ref[:, 0:dc] = (gate_b * conv).astype(LOW)
        dproj_ref[:, dc:2 * dc] = (dy_a * conv).astype(LOW)
        dcw_ref[2:3, :] += jnp.sum(dconv * z0, axis=0, keepdims=True)
        dcw_ref[1:2, :] += jnp.sum(dconv * z1, axis=0, keepdims=True)
        dcw_ref[0:1, :] += jnp.sum(dconv * z2, axis=0, keepdims=True)

        dpool = []
        for grp in range(n_grp):
            c0 = POOL_GC * grp
            pooled_b = pooled_ref[:, c0:c0 + POOL_GC]
            lin = _dot(pooled_b, pw_ref[grp])
            dy_b = dcat[:, dc + c0:dc + c0 + POOL_GC]
            scale = ps_ref[:, c0:c0 + POOL_GC]
            cat_ref[:, dc + c0:dc + c0 + POOL_GC] = (lin * scale).astype(LOW)
            dps_ref[:, c0:c0 + POOL_GC] += jnp.sum(dy_b * lin, axis=0, keepdims=True)
            dlin = (dy_b * scale).astype(LOW)
            dpw_ref[grp] += _dot_tn(pooled_b, dlin)
            dpool.append(_dot_nt(dlin, pw_ref[grp]))
            fut_ref[0:ts, dc + c0:dc + c0 + POOL_GC] = dpool[grp] * inv_count[grp]
        dwout_acc[...] += _dot_tn(cat_ref[...], dx2.astype(LOW))

        def ahead(off, c0, c1):
            return fut_ref[off:off + ts, c0:c1]

        dz = cw[2:3, :] * ahead(0, 0, dc) + cw[1:2, :] * ahead(1, 0, dc) + cw[0:1, :] * ahead(2, 0, dc)
        dproj_ref[:, 0:dc] = (dz * gate_c).astype(LOW)
        dproj_ref[:, 2 * dc:3 * dc] = (dz * v).astype(LOW)
        for grp, w in enumerate(POOL_WINDOWS):
            c0 = dc + POOL_GC * grp
            acc = ahead(0, c0, c0 + POOL_GC)
            for j in range(1, w):
                acc = acc + ahead(j, c0, c0 + POOL_GC)
            dproj_ref[:, 2 * dc + c0:2 * dc + c0 + POOL_GC] = (acc - dpool[grp]).astype(LOW)

        dh = _dot_nt(dproj_ref[:, 0:wcols], win_ref[0])
        for k in range(1, n_blk):
            dh += _dot_nt(dproj_ref[:, wcols * k:wcols * (k + 1)], win_ref[k])
        dx, dgp = _rms_bwd(dh, x_ref[...], g_ref[...])
        dx = dx2 + dx
        dx_ref[...] = dx
        df_ref[...] = (0.5 * dx).astype(LOW)
        dg_ref[...] += dgp
        fut_ref[ts:ts + HALO, :] = fut_ref[0:HALO, :]

        @pl.when(i == n_t - 1)
        def _():
            dwout_ref[...] = dwout_acc[...].astype(LOW)

    tok = pl.BlockSpec((ts, d), lambda i: (n_t - 1 - i, 0))
    halo = pl.BlockSpec((HALO, p_len), lambda i: (jnp.maximum((n_t - 1 - i) * (ts // HALO) - 1, 0), 0))

    def whole(arr):
        return pl.BlockSpec(arr.shape, lambda i: (0,) * arr.ndim)

    return _call(
        body,
        name=name,
        grid=(n_t,),
        in_specs=[tok, tok, pl.BlockSpec((ts, p_len), lambda i: (n_t - 1 - i, 0)), halo,
                  pl.BlockSpec((ts, d_mix - dc), lambda i: (n_t - 1 - i, 0)),
                  whole(g), _resident(w_in), whole(conv_w), whole(pool_w), whole(pool_scale), _resident(w_out)],
        out_specs=[tok, pl.BlockSpec((ts, p_len), lambda i: (n_t - 1 - i, 0)),
                   whole(w_out),
                   whole(g), whole(conv_w), whole(pool_w), whole(pool_scale), tok],
        out_shape=[
            jax.ShapeDtypeStruct((s_len, d), F32),
            jax.ShapeDtypeStruct((s_len, p_len), LOW),
            jax.ShapeDtypeStruct((d_mix, d), LOW),
            jax.ShapeDtypeStruct(g.shape, F32),
            jax.ShapeDtypeStruct(conv_w.shape, F32),
            jax.ShapeDtypeStruct(pool_w.shape, F32),
            jax.ShapeDtypeStruct(pool_scale.shape, F32),
            jax.ShapeDtypeStruct((s_len, d), LOW),
        ],
        scratch_shapes=[pltpu.VMEM((ts + HALO, p_len), F32), pltpu.VMEM((ts + HALO, d_mix), F32),
                        pltpu.VMEM((ts, d_mix), LOW), pltpu.VMEM((d_mix, d), F32)],
        args=(dx2, x, proj, proj, pooled, g, w_in, conv_w, pool_w, pool_scale, w_out),
        exchange=exchange,
    )


def _ffn_fwd_loss(x, h, wgt, wut, wd, g, target, name):
    s_len, d = x.shape
    fc = wd.shape[0]
    ts = min(TOK_TILE, s_len)

    def body(x_ref, h_ref, wg_ref, wu_ref, wd_ref, g_ref, tgt_ref,
             a_ref, b_ref, loss_ref, dx_ref, dg_ref, df_ref, s_ref):
        @pl.when(pl.program_id(0) == 0)
        def _():
            loss_ref[...] = jnp.zeros_like(loss_ref)
            dg_ref[...] = jnp.zeros_like(dg_ref)

        hb = h_ref[...]
        for c0, c1 in _slabs(fc):
            a = _dot_nt(hb, wg_ref[c0:c1, :])
            b = _dot_nt(hb, wu_ref[c0:c1, :])
            s_ref[:, c0:c1] = (a * jax.nn.sigmoid(a) * b).astype(LOW)
            a_ref[:, c0:c1] = a.astype(LOW)
            b_ref[:, c0:c1] = b.astype(LOW)
        xv = x_ref[...] + 0.5 * _dot(s_ref[...], wd_ref[...])
        gv = g_ref[...]
        err = xv * _rms_scale(xv) * gv - tgt_ref[...]
        loss_ref[...] += 0.5 * jnp.sum(jnp.mean(err * err, axis=-1, keepdims=True), axis=0, keepdims=True)
        dx, dgp = _rms_bwd(err * (1.0 / d), xv, gv)
        dx_ref[...] = dx
        df_ref[...] = (0.5 * dx).astype(LOW)
        dg_ref[...] += dgp

    tok = pl.BlockSpec((ts, d), lambda t: (t, 0))
    vec = pl.BlockSpec((1, d), lambda t: (0, 0))
    hid = pl.BlockSpec((ts, fc), lambda t: (t, 0))
    return pl.pallas_call(
        body,
        name=name,
        grid=(s_len // ts,),
        in_specs=[tok, tok, _resident(wgt), _resident(wut), _resident(wd), vec, tok],
        out_specs=[hid, hid, pl.BlockSpec((1, 128), lambda t: (0, 0)), tok, vec, tok],
        out_shape=[
            jax.ShapeDtypeStruct((s_len, fc), LOW),
            jax.ShapeDtypeStruct((s_len, fc), LOW),
            jax.ShapeDtypeStruct((1, 128), F32),
            jax.ShapeDtypeStruct((s_len, d), F32),
            jax.ShapeDtypeStruct((1, d), F32),
            jax.ShapeDtypeStruct((s_len, d), LOW),
        ],
        scratch_shapes=[pltpu.VMEM((ts, fc), LOW)],
        compiler_params=_params("arbitrary"),
    )(x, h, wgt, wut, wd, g, target)


def _row_tile(rows, cols, stack_bytes):
    budget = 20 * 1024 * 1024
    per_row = cols * (4 * 7 + stack_bytes)
    for tr in (rows, 512, 256, 176, 128, 64, 32, 16, 8):
        if rows % tr == 0 and tr % 8 == 0 and tr * per_row * 2 <= budget:
            return tr
    return rows


def _sum_stack(stack, name):
    n, r, c = stack.shape
    tr = _row_tile(r, c, n * stack.dtype.itemsize)

    def body(s_ref, o_ref):
        acc = s_ref[0].astype(F32)
        for k in range(1, n):
            acc = acc + s_ref[k].astype(F32)
        o_ref[...] = acc

    return pl.pallas_call(
        body,
        name=name,
        grid=(r // tr,),
        in_specs=[pl.BlockSpec((n, tr, c), lambda i: (0, i, 0))],
        out_specs=pl.BlockSpec((tr, c), lambda i: (i, 0)),
        out_shape=jax.ShapeDtypeStruct((r, c), F32),
        compiler_params=_params("arbitrary"),
    )(stack)


def _adamw_many(params, name):
    params = [(list(st) if isinstance(st, (list, tuple)) else [st], w, m, v) for st, w, m, v in params]
    stacks0, w0 = params[0][0], params[0][1]
    r, c = w0.shape
    n = stacks0[0].shape[0]
    n_st = len(stacks0)
    part_rows = [st.shape[1] for st in stacks0]
    assert sum(part_rows) == r
    assert all(w.shape == (r, c) and [st.shape for st in sts] == [st.shape for st in stacks0] for sts, w, _, _ in params)
    first_row = [sum(part_rows[:j]) for j in range(n_st)]
    tc = next(t for t in (512, 256, 128) if c % t == 0)
    c1 = 1.0 - ADAM_B1 ** ADAM_STEP
    c2 = 1.0 - ADAM_B2 ** ADAM_STEP
    n_in = n_st + 3

    def body(*refs):
        ins, outs = refs[:n_in * len(params)], refs[n_in * len(params):]
        for p in range(len(params)):
            s_refs = ins[n_in * p:n_in * p + n_st]
            w_ref, m_ref, v_ref = ins[n_in * p + n_st:n_in * (p + 1)]
            g_ref, d_ref, mo_ref, vo_ref = outs[4 * p:4 * p + 4]
            for s_ref, r0, nr in zip(s_refs, first_row, part_rows):
                gv = s_ref[0].astype(F32)
                for k in range(1, n):
                    gv = gv + s_ref[k].astype(F32)
                mn = ADAM_B1 * m_ref[r0:r0 + nr, :] + (1.0 - ADAM_B1) * gv
                vn = ADAM_B2 * v_ref[r0:r0 + nr, :] + (1.0 - ADAM_B2) * (gv * gv)
                g_ref[r0:r0 + nr, :] = gv
                mo_ref[r0:r0 + nr, :] = mn
                vo_ref[r0:r0 + nr, :] = vn
                d_ref[r0:r0 + nr, :] = -ADAM_LR * ((mn / c1) / (jnp.sqrt(vn / c2) + ADAM_EPS)
                                                   + ADAM_WD * w_ref[r0:r0 + nr, :])

    blk = pl.BlockSpec((r, tc), lambda i: (0, i))
    one_in = [pl.BlockSpec((n, nr, tc), lambda i: (0, 0, i)) for nr in part_rows] + [blk, blk, blk]
    res = pl.pallas_call(
        body,
        name=name,
        grid=(c // tc,),
        in_specs=one_in * len(params),
        out_specs=[blk] * (4 * len(params)),
        out_shape=[jax.ShapeDtypeStruct((r, c), F32)] * (4 * len(params)),
        compiler_params=_params("arbitrary"),
    )(*[arr for sts, w, m, v in params for arr in (*sts, w, m, v)])
    return [tuple(res[4 * p:4 * p + 4]) for p in range(len(params))]


def _adamw(stacks, w, m, v, name):
    return _adamw_many([(stacks, w, m, v)], name)[0]


def _to_sheet(parts):
    sheets, spans = [], []
    row = 0
    for p in parts:
        flat = p.reshape(-1).astype(F32)
        rows = -(-flat.shape[0] // 1024) * 8
        flat = jnp.pad(flat, (0, rows * 128 - flat.shape[0]))
        sheets.append(flat.reshape(rows, 128))
        spans.append((row, p.size, p.shape))
        row += rows
    return jnp.concatenate(sheets, axis=0), spans


def _from_sheet(sheet, spans):
    out = []
    for row, size, shape in spans:
        rows = -(-size // 1024) * 8
        out.append(sheet[row:row + rows].reshape(-1)[:size].reshape(shape))
    return out


def kernel(x, norm_ffn1, ffn1_w_gate, ffn1_w_up, ffn1_w_down, norm_mix, w_in, conv_w, pool_w, pool_scale, w_out, norm_ffn2, ffn2_w_gate, ffn2_w_up, ffn2_w_down, norm_final, loss_target, m_norm_ffn1, m_ffn1_w_gate, m_ffn1_w_up, m_ffn1_w_down, m_norm_mix, m_w_in, m_conv_w, m_pool_w, m_pool_scale, m_w_out, m_norm_ffn2, m_ffn2_w_gate, m_ffn2_w_up, m_ffn2_w_down, m_norm_final, v_norm_ffn1, v_ffn1_w_gate, v_ffn1_w_up, v_ffn1_w_down, v_norm_mix, v_w_in, v_conv_w, v_pool_w, v_pool_scale, v_w_out, v_norm_ffn2, v_ffn2_w_gate, v_ffn2_w_up, v_ffn2_w_down, v_norm_final):
    me = 4 * lax.axis_index("x") + 2 * lax.axis_index("y") + lax.axis_index("c")
    xs, tgt = x[0], loss_target[0]
    s_len, d = xs.shape
    f_shard = ffn1_w_down.shape[1]
    conv_shard = conv_w.shape[2]

    def low_t(wt):
        return wt[0].T.astype(LOW)

    def by_dev(gw):
        return gw.reshape(N_DEV, -1, d)

    conv_tile = jnp.zeros((8, 128), F32).at[0:conv_w.shape[1], 0:conv_shard].set(conv_w[0])
    pool_w_low = pool_w[0].astype(LOW)

    rows_a = -(-f_shard // 64) * 32

    def parts_of(w_gate, w_up, w_down):
        shards = [low_t(w_gate), low_t(w_up), w_down[0].astype(LOW)]
        return [(s, 0, rows_a) for s in shards], [(s, rows_a, f_shard - rows_a) for s in shards]

    def rows_flat(stacks):
        return [st.reshape(-1, d) for st in stacks]

    def gather(shards):
        return _Exchange(shards, [False] * len(shards), relay_at=0.6)

    def scatter(pairs):
        return _Exchange(pairs, [True] * len(pairs), among_chips=True)

    w1a_shards, w1b_shards = parts_of(ffn1_w_gate, ffn1_w_up, ffn1_w_down)
    w2a_shards, w2b_shards = parts_of(ffn2_w_gate, ffn2_w_up, ffn2_w_down)

    wg1a, wu1a, wd1a = rows_flat(_exchange(w1a_shards, [False] * 3, "gather_ffn1_a", relay=True))
    xa, h1, a1a, b1a, *w1b = _ffn_fwd(xs, norm_ffn1, None, wg1a, wu1a, wd1a, "ffn1_fwd_a",
                                      exchange=gather(w1b_shards))
    wg1b, wu1b, wd1b = rows_flat(w1b)
    x1, a1b, b1b, w_in_full, w_out_full, conv_tiles = _ffn_fwd(
        xa, None, h1, wg1b, wu1b, wd1b, "ffn1_fwd_b",
        exchange=gather([w_in[0].astype(LOW), w_out[0].astype(LOW), conv_tile]))
    w_out_full = w_out_full.reshape(-1, d)
    conv_full = jnp.concatenate([conv_tiles[k, 0:conv_w.shape[1], 0:conv_shard] for k in range(N_DEV)], axis=1)
    x2, h2, proj, pooled, *w2a = _mix_fwd(x1, norm_mix, w_in_full, conv_full, pool_w_low, pool_scale, w_out_full,
                                          "mix_fwd", exchange=gather(w2a_shards))
    wg2a, wu2a, wd2a = rows_flat(w2a)
    xb, h3, a2a, b2a, *w2b = _ffn_fwd(x2, norm_ffn2, None, wg2a, wu2a, wd2a, "ffn2_fwd_a",
                                      exchange=gather(w2b_shards))
    wg2b, wu2b, wd2b = rows_flat(w2b)
    a2b, b2b, loss_row, dx3, dg_final, df3 = _ffn_fwd_loss(
        xb, h3, wg2b, wu2b, wd2b, norm_final.reshape(1, d), tgt, "ffn2_fwd_b_loss")

    da2a, db2a, dwd_a, dwg_a, dwu_a = _ffn_bwdw(df3, a2a, b2a, h3, wd2a, "ffn2_bwdw_a")
    da2b, db2b, dwd_b, dwg_b, dwu_b = _ffn_bwdw(df3, a2b, b2b, h3, wd2b, "ffn2_bwdw_b")
    pairs = _pair_sum([by_dev(dwg_a), by_dev(dwu_a), by_dev(dwd_a), by_dev(dwg_b), by_dev(dwu_b), by_dev(dwd_b)],
                      "pair_sum_ffn2")
    dx2, dg_ffn2, *got_2a = _ffn_dx(dx3, x2, norm_ffn2, [(da2a, db2a, wg2a, wu2a), (da2b, db2b, wg2b, wu2b)],
                                    "ffn2_dx", exchange=scatter(pairs[:3]))
    dx1, dproj, dw_out, dg_mix, dconv, dpool_w, dpool_scale, df1, *got_2b = _mix_bwd(
        dx2, x1, proj, pooled, norm_mix, w_in_full, conv_full, pool_w_low, pool_scale, w_out_full, "mix_bwd",
        exchange=scatter(pairs[3:]))
    small_parts = [dg_mix, dg_ffn2, dg_final, dconv, dpool_w, dpool_scale, loss_row]
    small_sheet, spans = _to_sheet(small_parts)
    dw_in, got_small = _wgrad_tn(h2, dproj, W_IN_SHARD, True, "w_in_wgrad",
                                 exchange=_Exchange([small_sheet], [False], relay_at=0.5))
    pairs = _pair_sum([dw_in, by_dev(dw_out)], "pair_sum_mix")
    da1a, db1a, dwd, dwg, dwu, got_in, got_out = _ffn_bwdw(
        df1, a1a, b1a, h1, wd1a, "ffn1_bwdw_a", exchange=scatter(pairs))
    pairs = _pair_sum([by_dev(dwg), by_dev(dwu), by_dev(dwd)], "pair_sum_ffn1_a")
    da1b, db1b, dwd, dwg, dwu, *got_1a = _ffn_bwdw(df1, a1b, b1b, h1, wd1b, "ffn1_bwdw_b", exchange=scatter(pairs))
    pairs = _pair_sum([by_dev(dwg), by_dev(dwu), by_dev(dwd)], "pair_sum_ffn1_b")
    dx0, dg_ffn1, *got_1b = _ffn_dx(dx1, xs, norm_ffn1, [(da1a, db1a, wg1a, wu1a), (da1b, db1b, wg1b, wu1b)],
                                    "ffn1_dx", exchange=scatter(pairs))
    (got_n1,) = _exchange([dg_ffn1.reshape(8, 128)], [False], "gather_dnorm_ffn1")

    outs = {}

    def update(name, stacks, w, m, v):
        outs[name] = _adamw(stacks, w[0], m[0], v[0], "adamw_" + name)

    def update_ffn(prefix, got_a, got_b, gate, up, down):
        res = _adamw_many(
            [([got_a[j], got_b[j]], *[(t[0].T if j < 2 else t[0]) for t in wmv]) for j, wmv in enumerate((gate, up, down))],
            "adamw_" + prefix)
        outs[prefix + "_w_gate"] = tuple(r.T for r in res[0])
        outs[prefix + "_w_up"] = tuple(r.T for r in res[1])
        outs[prefix + "_w_down"] = res[2]

    update_ffn("ffn1", got_1a, got_1b, (ffn1_w_gate, m_ffn1_w_gate, v_ffn1_w_gate),
               (ffn1_w_up, m_ffn1_w_up, v_ffn1_w_up), (ffn1_w_down, m_ffn1_w_down, v_ffn1_w_down))
    update_ffn("ffn2", got_2a, got_2b, (ffn2_w_gate, m_ffn2_w_gate, v_ffn2_w_gate),
               (ffn2_w_up, m_ffn2_w_up, v_ffn2_w_up), (ffn2_w_down, m_ffn2_w_down, v_ffn2_w_down))
    update("w_in", got_in, w_in, m_w_in, v_w_in)
    update("w_out", got_out, w_out, m_w_out, v_w_out)

    g_small = _from_sheet(_sum_stack(got_small, "sum_small"), spans)
    g_norm_ffn1 = _sum_stack(got_n1, "sum_dnorm_ffn1").reshape(norm_ffn1.shape)
    g_conv = lax.dynamic_slice_in_dim(g_small[3], me * conv_shard, conv_shard, axis=1)
    small_names = ["norm_ffn1", "norm_mix", "norm_ffn2", "norm_final", "conv_w", "pool_w", "pool_scale"]
    small_g = [g_norm_ffn1, g_small[0], g_small[1], g_small[2].reshape(norm_final.shape), g_conv[None],
               g_small[4][None], g_small[5]]
    small_w = [norm_ffn1, norm_mix, norm_ffn2, norm_final, conv_w, pool_w, pool_scale]
    small_m = [m_norm_ffn1, m_norm_mix, m_norm_ffn2, m_norm_final, m_conv_w, m_pool_w, m_pool_scale]
    small_v = [v_norm_ffn1, v_norm_mix, v_norm_ffn2, v_norm_final, v_conv_w, v_pool_w, v_pool_scale]
    g_sheet, spans_u = _to_sheet(small_g)
    w_sheet, _ = _to_sheet(small_w)
    m_sheet, _ = _to_sheet(small_m)
    v_sheet, _ = _to_sheet(small_v)
    upd = _adamw(g_sheet[None], w_sheet, m_sheet, v_sheet, "adamw_small")
    small_out = [_from_sheet(u, spans_u) for u in upd]
    for k, nm in enumerate(small_names):
        outs[nm] = tuple(small_out[j][k] for j in range(4))

    loss = g_small[6][0, 0]
    order = ["norm_ffn1", "ffn1_w_gate", "ffn1_w_up", "ffn1_w_down", "norm_mix", "w_in", "conv_w", "pool_w",
             "pool_scale", "w_out", "norm_ffn2", "ffn2_w_gate", "ffn2_w_up", "ffn2_w_down", "norm_final"]
    big = {"ffn1_w_gate", "ffn1_w_up", "ffn1_w_down", "w_in", "w_out", "ffn2_w_gate", "ffn2_w_up", "ffn2_w_down"}

    def leaf(nm, j):
        val = outs[nm][j]
        return val[None] if nm in big else val

    return (loss, dx0[None],
            *[leaf(nm, 0) for nm in order], *[leaf(nm, 1) for nm in order],
            *[leaf(nm, 2) for nm in order], *[leaf(nm, 3) for nm in order])
```

```python
import jax
import jax.numpy as jnp
from jax import lax
from jax.experimental import pallas as pl
from jax.experimental.pallas import tpu as pltpu

F32 = jnp.float32
LOW = jnp.bfloat16

N_DEV = 8
EPS = 1e-6
D_CONV = 512
POOL_WINDOWS = (2, 4, 8, 16)
POOL_GC = 128
HALO = 16
W_IN_SHARD = 256

ADAM_LR = 0.001
ADAM_B1 = 0.9
ADAM_B2 = 0.999
ADAM_EPS = 1e-08
ADAM_WD = 0.01
ADAM_STEP = 10

VMEM_LIMIT_BYTES = 56 * 1024 * 1024
TOK_TILE = 512
MIX_TOK_TILE = 512
WGRAD_TOK_TILE = 4096
WGRAD_ROW_CANDIDATES = (256, 128)
RELAY_LAST_LATER = 0.25
BWD_ROW_SLAB = 2048


def _params(*sem):
    return pltpu.CompilerParams(dimension_semantics=sem, vmem_limit_bytes=VMEM_LIMIT_BYTES)


def _resident(arr):
    return pl.BlockSpec(arr.shape, lambda *_: (0,) * arr.ndim, pipeline_mode=pl.Buffered(1))


def _pick(n, candidates):
    for c in candidates:
        if n % c == 0:
            return c
    raise ValueError(f"no tile in {candidates} divides {n}")


def _dot(a, b):
    return lax.dot_general(a, b, (((1,), (0,)), ((), ())), preferred_element_type=F32)


def _dot_nt(a, b):
    return lax.dot_general(a, b, (((1,), (1,)), ((), ())), preferred_element_type=F32)


def _dot_tn(a, b):
    return lax.dot_general(a, b, (((0,), (0,)), ((), ())), preferred_element_type=F32)


def _rms_scale(x):
    return lax.rsqrt(jnp.mean(x * x, axis=-1, keepdims=True) + EPS)


def _rms_bwd(dy, x, g):
    r = _rms_scale(x)
    xhat = x * r
    gdy = dy * g
    dx = r * (gdy - xhat * jnp.mean(gdy * xhat, axis=-1, keepdims=True))
    return dx, jnp.sum(dy * xhat, axis=0, keepdims=True)


COLLECTIVE_PAIR, COLLECTIVE_CHIPS, COLLECTIVE_RELAY = 0, 1, 2


def _handshake(peer_numbers):
    mx, my, mc = lax.axis_index("x"), lax.axis_index("y"), lax.axis_index("c")
    barrier = pltpu.get_barrier_semaphore()
    for m in peer_numbers:
        peer = (lax.rem(mx + ((m >> 2) & 1), 2), lax.rem(my + ((m >> 1) & 1), 2), lax.rem(mc + (m & 1), 2))
        pl.semaphore_signal(barrier, inc=1, device_id=peer, device_id_type=pl.DeviceIdType.MESH)
    pl.semaphore_wait(barrier, len(peer_numbers))


class _Exchange:
    CHIPS = (2, 4, 6)

    def __init__(self, arrays, sliced, relay_at=None, among_chips=False):
        assert relay_at is None or not any(sliced)
        assert not among_chips or (all(sliced) and relay_at is None)
        self.relay_at, self.among_chips = relay_at, among_chips
        self.peers = self.CHIPS if among_chips else ((1, 2, 4) if relay_at is not None else tuple(range(1, N_DEV)))
        self.collective_id = COLLECTIVE_CHIPS if among_chips else (COLLECTIVE_RELAY if relay_at is not None else None)
        self.rows = [arr[1:] if isinstance(arr, tuple) else None for arr in arrays]
        self.arrays = [arr[0] if isinstance(arr, tuple) else arr for arr in arrays]
        self.sliced, self.n = list(sliced), len(arrays)
        assert all(rg is None or not sl for rg, sl in zip(self.rows, sliced))
        self.block_shape = [arr.shape if rg is None else (rg[1],) + arr.shape[1:]
                            for arr, rg in zip(self.arrays, self.rows)]
        self.out_shape = [jax.ShapeDtypeStruct(shape if sl else (N_DEV,) + shape, arr.dtype)
                          for arr, shape, sl in zip(self.arrays, self.block_shape, sliced)]
        self.specs = [pl.BlockSpec(memory_space=pl.ANY)] * self.n
        self.scratch_shapes = [pltpu.SemaphoreType.DMA((self.n, N_DEV)),
                               pltpu.SemaphoreType.DMA((self.n, N_DEV)),
                               pltpu.SemaphoreType.DMA((self.n,))]

    HALF_VIA = ((4, 2, 5), (2, 4, 7))

    def _halves(self, a):
        rows = self.block_shape[a][0]
        if rows % 32:
            return ((0, rows), None)
        return ((0, rows // 2), (rows // 2, rows // 2))

    def _copies(self, ins, outs, sems):
        send_sems, recv_sems, local_sems = sems
        sliced = self.sliced
        mx, my, mc = lax.axis_index("x"), lax.axis_index("y"), lax.axis_index("c")
        me = 2 * mx + my if self.among_chips else 4 * mx + 2 * my + mc

        def peer(m):
            px = lax.rem(mx + ((m >> 2) & 1), 2)
            py = lax.rem(my + ((m >> 1) & 1), 2)
            pc = lax.rem(mc + (m & 1), 2)
            return (px, py, pc), (2 * px + py if self.among_chips else 4 * px + 2 * py + pc)

        def mine(a):
            return ins[a] if self.rows[a] is None else ins[a].at[pl.ds(*self.rows[a])]

        def remote(a, m, arriving):
            pid, pflat = peer(m)
            return pltpu.make_async_remote_copy(
                src_ref=ins[a].at[pflat] if sliced[a] else mine(a),
                dst_ref=outs[a].at[pflat if arriving else me],
                send_sem=send_sems.at[a, m - 1],
                recv_sem=recv_sems.at[a, m - 1],
                device_id=pid,
                device_id_type=pl.DeviceIdType.MESH,
            )

        def local(a):
            return pltpu.make_async_copy(ins[a].at[me] if sliced[a] else mine(a), outs[a].at[me], local_sems.at[a])

        def passed_on(a, m):
            _, origin = peer(m)
            sibling, _ = peer(1)
            return pltpu.make_async_remote_copy(
                src_ref=outs[a].at[origin],
                dst_ref=outs[a].at[origin],
                send_sem=send_sems.at[a, m],
                recv_sem=recv_sems.at[a, m],
                device_id=sibling,
                device_id_type=pl.DeviceIdType.MESH,
            )

        def half_on(a, h, arriving):
            via, to, column = self.HALF_VIA[h]
            r0, nr = self._halves(a)[h]
            _, origin = peer(6 if arriving else via)
            rows = outs[a].at[origin].at[pl.ds(r0, nr)]
            return pltpu.make_async_remote_copy(
                src_ref=rows, dst_ref=rows, send_sem=send_sems.at[a, column], recv_sem=recv_sems.at[a, column],
                device_id=peer(to)[0], device_id_type=pl.DeviceIdType.MESH)

        return remote, local, passed_on, half_on

    def start(self, ins, outs, sems):
        remote, local, _, _ = self._copies(ins, outs, sems)
        if self.collective_id is not None:
            _handshake(self.peers)
        for a in range(self.n):
            local(a).start()
        for m in self.peers:
            for a in range(self.n):
                remote(a, m, False).start()

    def relay(self, ins, outs, sems):
        remote, _, passed_on, half_on = self._copies(ins, outs, sems)
        for h, (via, _, _) in enumerate(self.HALF_VIA):
            for a in range(self.n):
                remote(a, via, True).wait_recv()
                passed_on(a, via).start()
                if self._halves(a)[h] is not None:
                    half_on(a, h, False).start()

    def relay_last(self, ins, outs, sems):
        _, _, passed_on, half_on = self._copies(ins, outs, sems)
        for a in range(self.n):
            for h in range(2):
                if self._halves(a)[h] is not None:
                    half_on(a, h, True).wait_recv()
            passed_on(a, 6).start()

    def wait(self, ins, outs, sems):
        remote, local, passed_on, half_on = self._copies(ins, outs, sems)
        if self.relay_at is None:
            for m in self.peers:
                for a in range(self.n):
                    remote(a, m, True).wait_recv()
            for m in self.peers:
                for a in range(self.n):
                    remote(a, m, False).wait_send()
        else:
            for m in (1, 3, 5, 7):
                for a in range(self.n):
                    remote(a, m, True).wait_recv()
            for m in self.peers:
                for a in range(self.n):
                    remote(a, m, False).wait_send()
            for a in range(self.n):
                for m in self.CHIPS:
                    passed_on(a, m).wait_send()
                for h in range(2):
                    if self._halves(a)[h] is not None:
                        half_on(a, h, False).wait_send()
        for a in range(self.n):
            local(a).wait()


def _pair_sum(stacks, name):
    n = len(stacks)
    n_chip = N_DEV // 2
    half = [(n_chip,) + st.shape[1:] for st in stacks]

    def body(*refs):
        ins, outs, mine, theirs = refs[:n], refs[n:2 * n], refs[2 * n:3 * n], refs[3 * n:4 * n]
        local_sems, send_sems, recv_sems = refs[4 * n:]
        mx, my, mc = lax.axis_index("x"), lax.axis_index("y"), lax.axis_index("c")

        def own(a, k):
            return pltpu.make_async_copy(ins[a].at[2 * k + mc], mine[a].at[k], local_sems.at[a, k])

        def swap(a, k):
            return pltpu.make_async_remote_copy(
                src_ref=ins[a].at[2 * k + (1 - mc)], dst_ref=theirs[a].at[k],
                send_sem=send_sems.at[a, k], recv_sem=recv_sems.at[a, k],
                device_id=(mx, my, 1 - mc), device_id_type=pl.DeviceIdType.MESH)

        _handshake((1,))
        for k in range(n_chip):
            for a in range(n):
                own(a, k).start()
                swap(a, k).start()
        for k in range(n_chip):
            for a in range(n):
                own(a, k).wait()
                swap(a, k).wait()
                outs[a][k] = (mine[a][k].astype(F32) + theirs[a][k].astype(F32)).astype(LOW)

    return pl.pallas_call(
        body, name=name,
        out_shape=[jax.ShapeDtypeStruct(h, LOW) for h in half],
        in_specs=[pl.BlockSpec(memory_space=pl.ANY)] * n,
        out_specs=[pl.BlockSpec(memory_space=pltpu.VMEM)] * n,
        scratch_shapes=([pltpu.VMEM(h, st.dtype) for h, st in zip(half, stacks)] * 2
                        + [pltpu.SemaphoreType.DMA((n, n_chip))] * 3),
        compiler_params=pltpu.CompilerParams(vmem_limit_bytes=VMEM_LIMIT_BYTES, collective_id=COLLECTIVE_PAIR),
    )(*stacks)


def _exchange(arrays, sliced, name, relay=False):
    ex = _Exchange(arrays, sliced, relay_at=0 if relay else None)

    def body(*refs):
        ins, outs, sems = refs[:ex.n], refs[ex.n:2 * ex.n], refs[2 * ex.n:]
        ex.start(ins, outs, sems)
        if relay:
            ex.relay(ins, outs, sems)
            ex.relay_last(ins, outs, sems)
        ex.wait(ins, outs, sems)

    return pl.pallas_call(body, name=name, out_shape=ex.out_shape, in_specs=ex.specs, out_specs=ex.specs,
                          scratch_shapes=ex.scratch_shapes,
                          compiler_params=pltpu.CompilerParams(collective_id=ex.collective_id))(*ex.arrays)


def _call(body, *, name, grid, in_specs, out_specs, out_shape, args, scratch_shapes=(), exchange=None):
    params = _params(*(("arbitrary",) * len(grid)))
    if exchange is None:
        return pl.pallas_call(body, name=name, grid=grid, in_specs=in_specs, out_specs=out_specs, out_shape=out_shape,
                              scratch_shapes=list(scratch_shapes), compiler_params=params)(*args)
    exs = list(exchange) if isinstance(exchange, (list, tuple)) else [exchange]
    assert len(exs) == 1 or all(ex.collective_id is None for ex in exs)
    params = pltpu.CompilerParams(dimension_semantics=("arbitrary",) * len(grid), vmem_limit_bytes=VMEM_LIMIT_BYTES,
                                  collective_id=exs[0].collective_id)
    n_in, n_out, n_scr = len(in_specs), len(out_specs), len(scratch_shapes)
    n_ex = sum(ex.n for ex in exs)
    n_steps = 1
    for g in grid:
        n_steps *= g

    def hosted(*refs):
        ins, refs = refs[:n_in], refs[n_in:]
        ex_ins, refs = refs[:n_ex], refs[n_ex:]
        outs, refs = refs[:n_out], refs[n_out:]
        ex_outs, refs = refs[:n_ex], refs[n_ex:]
        scr, sems = refs[:n_scr], refs[n_scr:]
        parts, at = [], 0
        for j, ex in enumerate(exs):
            parts.append((ex_ins[at:at + ex.n], ex_outs[at:at + ex.n], sems[3 * j:3 * j + 3]))
            at += ex.n
        step = pl.program_id(0)
        for ax in range(1, len(grid)):
            step = step * grid[ax] + pl.program_id(ax)

        @pl.when(step == 0)
        def _():
            for ex, part in zip(exs, parts):
                ex.start(*part)

        body(*ins, *outs, *scr)

        for ex, part in zip(exs, parts):
            if ex.relay_at is not None:
                @pl.when(step == min(int(ex.relay_at * n_steps), n_steps - 1))
                def _():
                    ex.relay(*part)

                @pl.when(step == min(int((ex.relay_at + RELAY_LAST_LATER) * n_steps), n_steps - 1))
                def _():
                    ex.relay_last(*part)

        @pl.when(step == n_steps - 1)
        def _():
            for ex, part in zip(exs, parts):
                ex.wait(*part)

    return pl.pallas_call(
        hosted, name=name, grid=grid,
        in_specs=list(in_specs) + [sp for ex in exs for sp in ex.specs],
        out_specs=list(out_specs) + [sp for ex in exs for sp in ex.specs],
        out_shape=list(out_shape) + [sh for ex in exs for sh in ex.out_shape],
        scratch_shapes=list(scratch_shapes) + [sc for ex in exs for sc in ex.scratch_shapes],
        compiler_params=params)(*args, *[arr for ex in exs for arr in ex.arrays])


def _ffn_fwd(x, g, h, wgt, wut, wd, name, exchange=None):
    s_len, d = x.shape
    fc = wd.shape[0]
    ts = min(TOK_TILE, s_len)
    first = h is None

    def body(*refs):
        x_ref, gh_ref, wg_ref, wu_ref, wd_ref, xo_ref = refs[:6]
        a_ref, b_ref, s_ref = refs[-3:]
        xv = x_ref[...]
        if first:
            hb = (xv * _rms_scale(xv) * gh_ref[...]).astype(LOW)
            refs[6][...] = hb
        else:
            hb = gh_ref[...]
        for c0, c1 in _slabs(fc):
            a = _dot_nt(hb, wg_ref[c0:c1, :])
            b = _dot_nt(hb, wu_ref[c0:c1, :])
            s_ref[:, c0:c1] = (a * jax.nn.sigmoid(a) * b).astype(LOW)
            a_ref[:, c0:c1] = a.astype(LOW)
            b_ref[:, c0:c1] = b.astype(LOW)
        xo_ref[...] = xv + 0.5 * _dot(s_ref[...], wd_ref[...])

    tok = pl.BlockSpec((ts, d), lambda t: (t, 0))
    hid = pl.BlockSpec((ts, fc), lambda t: (t, 0))
    tok_out = jax.ShapeDtypeStruct((s_len, d), F32)
    h_out = jax.ShapeDtypeStruct((s_len, d), LOW)
    hid_out = jax.ShapeDtypeStruct((s_len, fc), LOW)
    return _call(
        body,
        name=name,
        grid=(s_len // ts,),
        in_specs=[tok, pl.BlockSpec((1, d), lambda t: (0, 0)) if first else tok,
                  _resident(wgt), _resident(wut), _resident(wd)],
        out_specs=[tok] + ([tok] if first else []) + [hid, hid],
        out_shape=[tok_out] + ([h_out] if first else []) + [hid_out, hid_out],
        scratch_shapes=[pltpu.VMEM((ts, fc), LOW)],
        args=(x, g if first else h, wgt, wut, wd),
        exchange=exchange,
    )


def _slabs(width, slab=256):
    return [(c0, min(c0 + slab, width)) for c0 in range(0, width, slab)]


def _ffn_bwdw(df, a, b, h, wd, name, exchange=None):
    s_len, d = df.shape
    f_len = wd.shape[0]
    tm = _pick(f_len, WGRAD_ROW_CANDIDATES)
    tk = min(WGRAD_TOK_TILE, s_len)
    n_k = s_len // tk

    def body(df_ref, a_ref, b_ref, h_ref, wd_ref, da_ref, db_ref, dwd_ref, dwg_ref, dwu_ref,
             s_ref, acc_d, acc_g, acc_u):
        k = pl.program_id(1)

        @pl.when(k == 0)
        def _():
            acc_d[...] = jnp.zeros_like(acc_d)
            acc_g[...] = jnp.zeros_like(acc_g)
            acc_u[...] = jnp.zeros_like(acc_u)

        wdv = wd_ref[...]
        for r0, r1 in _slabs(tk, BWD_ROW_SLAB):
            ds = _dot_nt(df_ref[r0:r1, :], wdv)
            av = a_ref[r0:r1, :].astype(F32)
            bv = b_ref[r0:r1, :].astype(F32)
            sig = jax.nn.sigmoid(av)
            silu = av * sig
            s_ref[r0:r1, :] = (silu * bv).astype(LOW)
            da_ref[r0:r1, :] = (ds * bv * (sig * (1.0 + av * (1.0 - sig)))).astype(LOW)
            db_ref[r0:r1, :] = (ds * silu).astype(LOW)
            hv = h_ref[r0:r1, :]
            acc_d[...] += _dot_tn(s_ref[r0:r1, :], df_ref[r0:r1, :])
            acc_g[...] += _dot_tn(da_ref[r0:r1, :], hv)
            acc_u[...] += _dot_tn(db_ref[r0:r1, :], hv)

        @pl.when(k == n_k - 1)
        def _():
            dwd_ref[...] = acc_d[...].astype(LOW)
            dwg_ref[...] = acc_g[...].astype(LOW)
            dwu_ref[...] = acc_u[...].astype(LOW)

    hid = pl.BlockSpec((tk, tm), lambda i, k: (k, i))
    tok = pl.BlockSpec((tk, d), lambda i, k: (k, 0))
    wrow = pl.BlockSpec((tm, d), lambda i, k: (i, 0))
    return _call(
        body,
        name=name,
        grid=(f_len // tm, n_k),
        in_specs=[tok, hid, hid, tok, wrow],
        out_specs=[hid, hid, wrow, wrow, wrow],
        out_shape=[jax.ShapeDtypeStruct((s_len, f_len), LOW)] * 2 + [jax.ShapeDtypeStruct((f_len, d), LOW)] * 3,
        scratch_shapes=[pltpu.VMEM((tk, tm), LOW)] + [pltpu.VMEM((tm, d), F32)] * 3,
        args=(df, a, b, h, wd),
        exchange=exchange,
    )


def _ffn_dx(dxo, x, g, parts, name, exchange=None):
    s_len, d = x.shape
    ts = min(TOK_TILE, s_len)
    n_p = len(parts)

    def body(dxo_ref, x_ref, g_ref, *refs):
        dxi_ref, dg_ref = refs[4 * n_p:]

        @pl.when(pl.program_id(0) == 0)
        def _():
            dg_ref[...] = jnp.zeros_like(dg_ref)

        dh = None
        for p in range(n_p):
            da_ref, db_ref, wg_ref, wu_ref = refs[4 * p:4 * p + 4]
            part = _dot(da_ref[...], wg_ref[...]) + _dot(db_ref[...], wu_ref[...])
            dh = part if dh is None else dh + part
        dx, dgp = _rms_bwd(dh, x_ref[...], g_ref[...])
        dxi_ref[...] = dxo_ref[...] + dx
        dg_ref[...] += dgp

    tok = pl.BlockSpec((ts, d), lambda t: (t, 0))
    vec = pl.BlockSpec((1, d), lambda t: (0, 0))
    part_specs, part_args = [], []
    for da, db, wgt, wut in parts:
        hid = pl.BlockSpec((ts, da.shape[1]), lambda t: (t, 0))
        part_specs += [hid, hid, _resident(wgt), _resident(wut)]
        part_args += [da, db, wgt, wut]
    return _call(
        body,
        name=name,
        grid=(s_len // ts,),
        in_specs=[tok, tok, vec] + part_specs,
        out_specs=[tok, vec],
        out_shape=[jax.ShapeDtypeStruct((s_len, d), F32), jax.ShapeDtypeStruct((1, d), F32)],
        args=(dxo, x, g, *part_args),
        exchange=exchange,
    )


def _wgrad_tn(xm, ym, tn, stacked, name, exchange=None):
    s_len, m = xm.shape
    n = ym.shape[1]
    tk = min(WGRAD_TOK_TILE, s_len)
    n_k = s_len // tk

    def body(x_ref, y_ref, o_ref, acc):
        k = pl.program_id(1)

        @pl.when(k == 0)
        def _():
            acc[...] = jnp.zeros_like(acc)

        acc[...] += _dot_tn(x_ref[...].astype(LOW), y_ref[...].astype(LOW))

        @pl.when(k == n_k - 1)
        def _():
            o_ref[...] = acc[...].astype(LOW)

    if stacked:
        out_spec = pl.BlockSpec((None, m, tn), lambda j, k: (j, 0, 0))
        out_shape = jax.ShapeDtypeStruct((n // tn, m, tn), LOW)
    else:
        out_spec = pl.BlockSpec((m, tn), lambda j, k: (0, j))
        out_shape = jax.ShapeDtypeStruct((m, n), LOW)
    return _call(
        body,
        name=name,
        grid=(n // tn, n_k),
        in_specs=[pl.BlockSpec((tk, m), lambda j, k: (k, 0)), pl.BlockSpec((tk, tn), lambda j, k: (k, j))],
        out_specs=[out_spec],
        out_shape=[out_shape],
        scratch_shapes=[pltpu.VMEM((m, tn), F32)],
        args=(xm, ym),
        exchange=exchange,
    )


def _mix_parts(ext_ref, cw, ts, row0, with_pooled=True):
    dc = D_CONV

    def back(off, c0, c1):
        return ext_ref[HALO - off:HALO - off + ts, c0:c1]

    v, gate_b, gate_c = back(0, 0, dc), back(0, dc, 2 * dc), back(0, 2 * dc, 3 * dc)
    z0 = gate_c * v
    z1 = back(1, 2 * dc, 3 * dc) * back(1, 0, dc)
    z2 = back(2, 2 * dc, 3 * dc) * back(2, 0, dc)
    conv = cw[2:3, :] * z0 + cw[1:2, :] * z1 + cw[0:1, :] * z2
    rows = row0 + lax.broadcasted_iota(jnp.int32, (ts, 1), 0)
    pooled, inv_count = [], []
    for grp, w in enumerate(POOL_WINDOWS):
        inv = 1.0 / jnp.minimum(rows + 1, w).astype(F32)
        inv_count.append(inv)
        if with_pooled:
            c0 = 3 * dc + POOL_GC * grp
            u = back(0, c0, c0 + POOL_GC)
            acc = u
            for j in range(1, w):
                acc = acc + back(j, c0, c0 + POOL_GC)
            pooled.append(acc * inv - u)
    return v, gate_b, gate_c, z0, z1, z2, conv, pooled, inv_count


def _mix_fwd(x, g, w_in, conv_w, pool_w, pool_scale, w_out, name, exchange=None):
    s_len, d = x.shape
    n_blk, _, wcols = w_in.shape
    p_len = n_blk * wcols
    d_mix = w_out.shape[0]
    ts = min(MIX_TOK_TILE, s_len)
    dc = D_CONV

    def body(x_ref, g_ref, win_ref, cw_ref, pw_ref, ps_ref, wout_ref, x2_ref, h_ref, proj_ref, pooled_ref,
             ext_ref, cat_ref):
        t = pl.program_id(0)

        @pl.when(t == 0)
        def _():
            ext_ref[0:HALO, :] = jnp.zeros((HALO, p_len), F32)

        xv = x_ref[...]
        hb = (xv * _rms_scale(xv) * g_ref[...]).astype(LOW)
        h_ref[...] = hb
        for k in range(n_blk):
            ext_ref[HALO:HALO + ts, wcols * k:wcols * (k + 1)] = _dot(hb, win_ref[k])
        proj_ref[...] = ext_ref[HALO:HALO + ts, :]

        _, gate_b, _, _, _, _, conv, pooled, _ = _mix_parts(ext_ref, cw_ref[...], ts, t * ts)
        cat_ref[:, 0:dc] = (gate_b * conv).astype(LOW)
        for grp in range(len(POOL_WINDOWS)):
            c0 = POOL_GC * grp
            pooled_b = pooled[grp].astype(LOW)
            pooled_ref[:, c0:c0 + POOL_GC] = pooled_b
            lin = _dot(pooled_b, pw_ref[grp])
            cat_ref[:, dc + c0:dc + c0 + POOL_GC] = (lin * ps_ref[:, c0:c0 + POOL_GC]).astype(LOW)
        x2_ref[...] = xv + _dot(cat_ref[...], wout_ref[...])
        ext_ref[0:HALO, :] = ext_ref[ts:ts + HALO, :]

    tok = pl.BlockSpec((ts, d), lambda t: (t, 0))

    def whole(arr):
        return pl.BlockSpec(arr.shape, lambda t: (0,) * arr.ndim)

    return _call(
        body,
        name=name,
        grid=(s_len // ts,),
        in_specs=[tok, whole(g), _resident(w_in), whole(conv_w), whole(pool_w), whole(pool_scale), _resident(w_out)],
        out_specs=[tok, tok, pl.BlockSpec((ts, p_len), lambda t: (t, 0)),
                   pl.BlockSpec((ts, d_mix - dc), lambda t: (t, 0))],
        out_shape=[
            jax.ShapeDtypeStruct((s_len, d), F32),
            jax.ShapeDtypeStruct((s_len, d), LOW),
            jax.ShapeDtypeStruct((s_len, p_len), F32),
            jax.ShapeDtypeStruct((s_len, d_mix - dc), LOW),
        ],
        scratch_shapes=[pltpu.VMEM((ts + HALO, p_len), F32), pltpu.VMEM((ts, d_mix), LOW)],
        args=(x, g, w_in, conv_w, pool_w, pool_scale, w_out),
        exchange=exchange,
    )


def _mix_bwd(dx2, x, proj, pooled, g, w_in, conv_w, pool_w, pool_scale, w_out, name, exchange=None):
    s_len, d = x.shape
    n_blk, _, wcols = w_in.shape
    p_len = n_blk * wcols
    d_mix = w_out.shape[0]
    ts = min(MIX_TOK_TILE, s_len)
    n_t = s_len // ts
    dc = D_CONV
    n_grp = len(POOL_WINDOWS)

    def body(dx2_ref, x_ref, proj_ref, halo_ref, pooled_ref, g_ref, win_ref, cw_ref, pw_ref, ps_ref, wout_ref,
             dx_ref, dproj_ref, dwout_ref, dg_ref, dcw_ref, dpw_ref, dps_ref, df_ref,
             ext_ref, fut_ref, cat_ref, dwout_acc):
        i = pl.program_id(0)
        t = n_t - 1 - i

        @pl.when(i == 0)
        def _():
            dwout_acc[...] = jnp.zeros_like(dwout_acc)
            dg_ref[...] = jnp.zeros_like(dg_ref)
            dcw_ref[...] = jnp.zeros_like(dcw_ref)
            dpw_ref[...] = jnp.zeros_like(dpw_ref)
            dps_ref[...] = jnp.zeros_like(dps_ref)
            fut_ref[ts:ts + HALO, :] = jnp.zeros((HALO, d_mix), F32)

        ext_ref[HALO:HALO + ts, :] = proj_ref[...]

        @pl.when(t == 0)
        def _():
            ext_ref[0:HALO, :] = jnp.zeros((HALO, p_len), F32)

        @pl.when(t > 0)
        def _():
            ext_ref[0:HALO, :] = halo_ref[...]

        cw = cw_ref[...]
        v, gate_b, gate_c, z0, z1, z2, conv, _, inv_count = _mix_parts(ext_ref, cw, ts, t * ts, with_pooled=False)
        dx2 = dx2_ref[...]
        dcat = _dot_nt(dx2.astype(LOW), wout_ref[...])

        dy_a = dcat[:, 0:dc]
        dconv = dy_a * gate_b
        fut_ref[0:ts, 0:dc] = dconv
        cat_ref[:, 0:dc] = (gate_b * conv).astype(LOW)
        dproj_ref[:, dc:2 * dc] = (dy_a * conv).astype(LOW)
        dcw_ref[2:3, :] += jnp.sum(dconv * z0, axis=0, keepdims=True)
        dcw_ref[1:2, :] += jnp.sum(dconv * z1, axis=0, keepdims=True)
        dcw_ref[0:1, :] += jnp.sum(dconv * z2, axis=0, keepdims=True)

        dpool = []
        for grp in range(n_grp):
            c0 = POOL_GC * grp
            pooled_b = pooled_ref[:, c0:c0 + POOL_GC]
            lin = _dot(pooled_b, pw_ref[grp])
            dy_b = dcat[:, dc + c0:dc + c0 + POOL_GC]
            scale = ps_ref[:, c0:c0 + POOL_GC]
            cat_ref[:, dc + c0:dc + c0 + POOL_GC] = (lin * scale).astype(LOW)
            dps_ref[:, c0:c0 + POOL_GC] += jnp.sum(dy_b * lin, axis=0, keepdims=True)
            dlin = (dy_b * scale).astype(LOW)
            dpw_ref[grp] += _dot_tn(pooled_b, dlin)
            dpool.append(_dot_nt(dlin, pw_ref[grp]))
            fut_ref[0:ts, dc + c0:dc + c0 + POOL_GC] = dpool[grp] * inv_count[grp]
        dwout_acc[...] += _dot_tn(cat_ref[...], dx2.astype(LOW))

        def ahead(off, c0, c1):
            return fut_ref[off:off + ts, c0:c1]

        dz = cw[2:3, :] * ahead(0, 0, dc) + cw[1:2, :] * ahead(1, 0, dc) + cw[0:1, :] * ahead(2, 0, dc)
        dproj_ref[:, 0:dc] = (dz * gate_c).astype(LOW)
        dproj_ref[:, 2 * dc:3 * dc] = (dz * v).astype(LOW)
        for grp, w in enumerate(POOL_WINDOWS):
            c0 = dc + POOL_GC * grp
            acc = ahead(0, c0, c0 + POOL_GC)
            for j in range(1, w):
                acc = acc + ahead(j, c0, c0 + POOL_GC)
            dproj_ref[:, 2 * dc + c0:2 * dc + c0 + POOL_GC] = (acc - dpool[grp]).astype(LOW)

        dh = _dot_nt(dproj_ref[:, 0:wcols], win_ref[0])
        for k in range(1, n_blk):
            dh += _dot_nt(dproj_ref[:, wcols * k:wcols * (k + 1)], win_ref[k])
        dx, dgp = _rms_bwd(dh, x_ref[...], g_ref[...])
        dx = dx2 + dx
        dx_ref[...] = dx
        df_ref[...] = (0.5 * dx).astype(LOW)
        dg_ref[...] += dgp
        fut_ref[ts:ts + HALO, :] = fut_ref[0:HALO, :]

        @pl.when(i == n_t - 1)
        def _():
            dwout_ref[...] = dwout_acc[...].astype(LOW)

    tok = pl.BlockSpec((ts, d), lambda i: (n_t - 1 - i, 0))
    halo = pl.BlockSpec((HALO, p_len), lambda i: (jnp.maximum((n_t - 1 - i) * (ts // HALO) - 1, 0), 0))

    def whole(arr):
        return pl.BlockSpec(arr.shape, lambda i: (0,) * arr.ndim)

    return _call(
        body,
        name=name,
        grid=(n_t,),
        in_specs=[tok, tok, pl.BlockSpec((ts, p_len), lambda i: (n_t - 1 - i, 0)), halo,
                  pl.BlockSpec((ts, d_mix - dc), lambda i: (n_t - 1 - i, 0)),
                  whole(g), _resident(w_in), whole(conv_w), whole(pool_w), whole(pool_scale), _resident(w_out)],
        out_specs=[tok, pl.BlockSpec((ts, p_len), lambda i: (n_t - 1 - i, 0)),
                   whole(w_out),
                   whole(g), whole(conv_w), whole(pool_w), whole(pool_scale), tok],
        out_shape=[
            jax.ShapeDtypeStruct((s_len, d), F32),
            jax.ShapeDtypeStruct((s_len, p_len), LOW),
            jax.ShapeDtypeStruct((d_mix, d), LOW),
            jax.ShapeDtypeStruct(g.shape, F32),
            jax.ShapeDtypeStruct(conv_w.shape, F32),
            jax.ShapeDtypeStruct(pool_w.shape, F32),
            jax.ShapeDtypeStruct(pool_scale.shape, F32),
            jax.ShapeDtypeStruct((s_len, d), LOW),
        ],
        scratch_shapes=[pltpu.VMEM((ts + HALO, p_len), F32), pltpu.VMEM((ts + HALO, d_mix), F32),
                        pltpu.VMEM((ts, d_mix), LOW), pltpu.VMEM((d_mix, d), F32)],
        args=(dx2, x, proj, proj, pooled, g, w_in, conv_w, pool_w, pool_scale, w_out),
        exchange=exchange,
    )


def _ffn_fwd_loss(x, h, wgt, wut, wd, g, target, name):
    s_len, d = x.shape
    fc = wd.shape[0]
    ts = min(TOK_TILE, s_len)

    def body(x_ref, h_ref, wg_ref, wu_ref, wd_ref, g_ref, tgt_ref,
             a_ref, b_ref, loss_ref, dx_ref, dg_ref, df_ref, s_ref):
        @pl.when(pl.program_id(0) == 0)
        def _():
            loss_ref[...] = jnp.zeros_like(loss_ref)
            dg_ref[...] = jnp.zeros_like(dg_ref)

        hb = h_ref[...]
        for c0, c1 in _slabs(fc):
            a = _dot_nt(hb, wg_ref[c0:c1, :])
            b = _dot_nt(hb, wu_ref[c0:c1, :])
            s_ref[:, c0:c1] = (a * jax.nn.sigmoid(a) * b).astype(LOW)
            a_ref[:, c0:c1] = a.astype(LOW)
            b_ref[:, c0:c1] = b.astype(LOW)
        xv = x_ref[...] + 0.5 * _dot(s_ref[...], wd_ref[...])
        gv = g_ref[...]
        err = xv * _rms_scale(xv) * gv - tgt_ref[...]
        loss_ref[...] += 0.5 * jnp.sum(jnp.mean(err * err, axis=-1, keepdims=True), axis=0, keepdims=True)
        dx, dgp = _rms_bwd(err * (1.0 / d), xv, gv)
        dx_ref[...] = dx
        df_ref[...] = (0.5 * dx).astype(LOW)
        dg_ref[...] += dgp

    tok = pl.BlockSpec((ts, d), lambda t: (t, 0))
    vec = pl.BlockSpec((1, d), lambda t: (0, 0))
    hid = pl.BlockSpec((ts, fc), lambda t: (t, 0))
    return pl.pallas_call(
        body,
        name=name,
        grid=(s_len // ts,),
        in_specs=[tok, tok, _resident(wgt), _resident(wut), _resident(wd), vec, tok],
        out_specs=[hid, hid, pl.BlockSpec((1, 128), lambda t: (0, 0)), tok, vec, tok],
        out_shape=[
            jax.ShapeDtypeStruct((s_len, fc), LOW),
            jax.ShapeDtypeStruct((s_len, fc), LOW),
            jax.ShapeDtypeStruct((1, 128), F32),
            jax.ShapeDtypeStruct((s_len, d), F32),
            jax.ShapeDtypeStruct((1, d), F32),
            jax.ShapeDtypeStruct((s_len, d), LOW),
        ],
        scratch_shapes=[pltpu.VMEM((ts, fc), LOW)],
        compiler_params=_params("arbitrary"),
    )(x, h, wgt, wut, wd, g, target)


def _row_tile(rows, cols, stack_bytes):
    budget = 20 * 1024 * 1024
    per_row = cols * (4 * 7 + stack_bytes)
    for tr in (rows, 512, 256, 176, 128, 64, 32, 16, 8):
        if rows % tr == 0 and tr % 8 == 0 and tr * per_row * 2 <= budget:
            return tr
    return rows


def _sum_stack(stack, name):
    n, r, c = stack.shape
    tr = _row_tile(r, c, n * stack.dtype.itemsize)

    def body(s_ref, o_ref):
        acc = s_ref[0].astype(F32)
        for k in range(1, n):
            acc = acc + s_ref[k].astype(F32)
        o_ref[...] = acc

    return pl.pallas_call(
        body,
        name=name,
        grid=(r // tr,),
        in_specs=[pl.BlockSpec((n, tr, c), lambda i: (0, i, 0))],
        out_specs=pl.BlockSpec((tr, c), lambda i: (i, 0)),
        out_shape=jax.ShapeDtypeStruct((r, c), F32),
        compiler_params=_params("arbitrary"),
    )(stack)


def _adamw_many(params, name):
    params = [(list(st) if isinstance(st, (list, tuple)) else [st], w, m, v) for st, w, m, v in params]
    stacks0, w0 = params[0][0], params[0][1]
    r, c = w0.shape
    n = stacks0[0].shape[0]
    n_st = len(stacks0)
    part_rows = [st.shape[1] for st in stacks0]
    assert sum(part_rows) == r
    assert all(w.shape == (r, c) and [st.shape for st in sts] == [st.shape for st in stacks0] for sts, w, _, _ in params)
    first_row = [sum(part_rows[:j]) for j in range(n_st)]
    tc = next(t for t in (512, 256, 128) if c % t == 0)
    c1 = 1.0 - ADAM_B1 ** ADAM_STEP
    c2 = 1.0 - ADAM_B2 ** ADAM_STEP
    n_in = n_st + 3

    def body(*refs):
        ins, outs = refs[:n_in * len(params)], refs[n_in * len(params):]
        for p in range(len(params)):
            s_refs = ins[n_in * p:n_in * p + n_st]
            w_ref, m_ref, v_ref = ins[n_in * p + n_st:n_in * (p + 1)]
            g_ref, d_ref, mo_ref, vo_ref = outs[4 * p:4 * p + 4]
            for s_ref, r0, nr in zip(s_refs, first_row, part_rows):
                gv = s_ref[0].astype(F32)
                for k in range(1, n):
                    gv = gv + s_ref[k].astype(F32)
                mn = ADAM_B1 * m_ref[r0:r0 + nr, :] + (1.0 - ADAM_B1) * gv
                vn = ADAM_B2 * v_ref[r0:r0 + nr, :] + (1.0 - ADAM_B2) * (gv * gv)
                g_ref[r0:r0 + nr, :] = gv
                mo_ref[r0:r0 + nr, :] = mn
                vo_ref[r0:r0 + nr, :] = vn
                d_ref[r0:r0 + nr, :] = -ADAM_LR * ((mn / c1) / (jnp.sqrt(vn / c2) + ADAM_EPS)
                                                   + ADAM_WD * w_ref[r0:r0 + nr, :])

    blk = pl.BlockSpec((r, tc), lambda i: (0, i))
    one_in = [pl.BlockSpec((n, nr, tc), lambda i: (0, 0, i)) for nr in part_rows] + [blk, blk, blk]
    res = pl.pallas_call(
        body,
        name=name,
        grid=(c // tc,),
        in_specs=one_in * len(params),
        out_specs=[blk] * (4 * len(params)),
        out_shape=[jax.ShapeDtypeStruct((r, c), F32)] * (4 * len(params)),
        compiler_params=_params("arbitrary"),
    )(*[arr for sts, w, m, v in params for arr in (*sts, w, m, v)])
    return [tuple(res[4 * p:4 * p + 4]) for p in range(len(params))]


def _adamw(stacks, w, m, v, name):
    return _adamw_many([(stacks, w, m, v)], name)[0]


def _to_sheet(parts):
    sheets, spans = [], []
    row = 0
    for p in parts:
        flat = p.reshape(-1).astype(F32)
        rows = -(-flat.shape[0] // 1024) * 8
        flat = jnp.pad(flat, (0, rows * 128 - flat.shape[0]))
        sheets.append(flat.reshape(rows, 128))
        spans.append((row, p.size, p.shape))
        row += rows
    return jnp.concatenate(sheets, axis=0), spans


def _from_sheet(sheet, spans):
    out = []
    for row, size, shape in spans:
        rows = -(-size // 1024) * 8
        out.append(sheet[row:row + rows].reshape(-1)[:size].reshape(shape))
    return out


def kernel(x, norm_ffn1, ffn1_w_gate, ffn1_w_up, ffn1_w_down, norm_mix, w_in, conv_w, pool_w, pool_scale, w_out, norm_ffn2, ffn2_w_gate, ffn2_w_up, ffn2_w_down, norm_final, loss_target, m_norm_ffn1, m_ffn1_w_gate, m_ffn1_w_up, m_ffn1_w_down, m_norm_mix, m_w_in, m_conv_w, m_pool_w, m_pool_scale, m_w_out, m_norm_ffn2, m_ffn2_w_gate, m_ffn2_w_up, m_ffn2_w_down, m_norm_final, v_norm_ffn1, v_ffn1_w_gate, v_ffn1_w_up, v_ffn1_w_down, v_norm_mix, v_w_in, v_conv_w, v_pool_w, v_pool_scale, v_w_out, v_norm_ffn2, v_ffn2_w_gate, v_ffn2_w_up, v_ffn2_w_down, v_norm_final):
    me = 4 * lax.axis_index("x") + 2 * lax.axis_index("y") + lax.axis_index("c")
    xs, tgt = x[0], loss_target[0]
    s_len, d = xs.shape
    f_shard = ffn1_w_down.shape[1]
    conv_shard = conv_w.shape[2]

    def low_t(wt):
        return wt[0].T.astype(LOW)

    def by_dev(gw):
        return gw.reshape(N_DEV, -1, d)

    conv_tile = jnp.zeros((8, 128), F32).at[0:conv_w.shape[1], 0:conv_shard].set(conv_w[0])
    pool_w_low = pool_w[0].astype(LOW)

    rows_a = -(-f_shard // 64) * 32

    def parts_of(w_gate, w_up, w_down):
        shards = [low_t(w_gate), low_t(w_up), w_down[0].astype(LOW)]
        return [(s, 0, rows_a) for s in shards], [(s, rows_a, f_shard - rows_a) for s in shards]

    def rows_flat(stacks):
        return [st.reshape(-1, d) for st in stacks]

    def gather(shards):
        return _Exchange(shards, [False] * len(shards), relay_at=0.6)

    def scatter(pairs):
        return _Exchange(pairs, [True] * len(pairs), among_chips=True)

    w1a_shards, w1b_shards = parts_of(ffn1_w_gate, ffn1_w_up, ffn1_w_down)
    w2a_shards, w2b_shards = parts_of(ffn2_w_gate, ffn2_w_up, ffn2_w_down)

    wg1a, wu1a, wd1a = rows_flat(_exchange(w1a_shards, [False] * 3, "gather_ffn1_a", relay=True))
    xa, h1, a1a, b1a, *w1b = _ffn_fwd(xs, norm_ffn1, None, wg1a, wu1a, wd1a, "ffn1_fwd_a",
                                      exchange=gather(w1b_shards))
    wg1b, wu1b, wd1b = rows_flat(w1b)
    x1, a1b, b1b, w_in_full, w_out_full, conv_tiles = _ffn_fwd(
        xa, None, h1, wg1b, wu1b, wd1b, "ffn1_fwd_b",
        exchange=gather([w_in[0].astype(LOW), w_out[0].astype(LOW), conv_tile]))
    w_out_full = w_out_full.reshape(-1, d)
    conv_full = jnp.concatenate([conv_tiles[k, 0:conv_w.shape[1], 0:conv_shard] for k in range(N_DEV)], axis=1)
    x2, h2, proj, pooled, *w2a = _mix_fwd(x1, norm_mix, w_in_full, conv_full, pool_w_low, pool_scale, w_out_full,
                                          "mix_fwd", exchange=gather(w2a_shards))
    wg2a, wu2a, wd2a = rows_flat(w2a)
    xb, h3, a2a, b2a, *w2b = _ffn_fwd(x2, norm_ffn2, None, wg2a, wu2a, wd2a, "ffn2_fwd_a",
                                      exchange=gather(w2b_shards))
    wg2b, wu2b, wd2b = rows_flat(w2b)
    a2b, b2b, loss_row, dx3, dg_final, df3 = _ffn_fwd_loss(
        xb, h3, wg2b, wu2b, wd2b, norm_final.reshape(1, d), tgt, "ffn2_fwd_b_loss")

    da2a, db2a, dwd_a, dwg_a, dwu_a = _ffn_bwdw(df3, a2a, b2a, h3, wd2a, "ffn2_bwdw_a")
    da2b, db2b, dwd_b, dwg_b, dwu_b = _ffn_bwdw(df3, a2b, b2b, h3, wd2b, "ffn2_bwdw_b")
    pairs = _pair_sum([by_dev(dwg_a), by_dev(dwu_a), by_dev(dwd_a), by_dev(dwg_b), by_dev(dwu_b), by_dev(dwd_b)],
                      "pair_sum_ffn2")
    dx2, dg_ffn2, *got_2a = _ffn_dx(dx3, x2, norm_ffn2, [(da2a, db2a, wg2a, wu2a), (da2b, db2b, wg2b, wu2b)],
                                    "ffn2_dx", exchange=scatter(pairs[:3]))
    dx1, dproj, dw_out, dg_mix, dconv, dpool_w, dpool_scale, df1, *got_2b = _mix_bwd(
        dx2, x1, proj, pooled, norm_mix, w_in_full, conv_full, pool_w_low, pool_scale, w_out_full, "mix_bwd",
        exchange=scatter(pairs[3:]))
    small_parts = [dg_mix, dg_ffn2, dg_final, dconv, dpool_w, dpool_scale, loss_row]
    small_sheet, spans = _to_sheet(small_parts)
    dw_in, got_small = _wgrad_tn(h2, dproj, W_IN_SHARD, True, "w_in_wgrad",
                                 exchange=_Exchange([small_sheet], [False], relay_at=0.5))
    pairs = _pair_sum([dw_in, by_dev(dw_out)], "pair_sum_mix")
    da1a, db1a, dwd, dwg, dwu, got_in, got_out = _ffn_bwdw(
        df1, a1a, b1a, h1, wd1a, "ffn1_bwdw_a", exchange=scatter(pairs))
    pairs = _pair_sum([by_dev(dwg), by_dev(dwu), by_dev(dwd)], "pair_sum_ffn1_a")
    da1b, db1b, dwd, dwg, dwu, *got_1a = _ffn_bwdw(df1, a1b, b1b, h1, wd1b, "ffn1_bwdw_b", exchange=scatter(pairs))
    pairs = _pair_sum([by_dev(dwg), by_dev(dwu), by_dev(dwd)], "pair_sum_ffn1_b")
    dx0, dg_ffn1, *got_1b = _ffn_dx(dx1, xs, norm_ffn1, [(da1a, db1a, wg1a, wu1a), (da1b, db1b, wg1b, wu1b)],
                                    "ffn1_dx", exchange=scatter(pairs))
    (got_n1,) = _exchange([dg_ffn1.reshape(8, 128)], [False], "gather_dnorm_ffn1")

    outs = {}

    def update(name, stacks, w, m, v):
        outs[name] = _adamw(stacks, w[0], m[0], v[0], "adamw_" + name)

    def update_ffn(prefix, got_a, got_b, gate, up, down):
        res = _adamw_many(
            [([got_a[j], got_b[j]], *[(t[0].T if j < 2 else t[0]) for t in wmv]) for j, wmv in enumerate((gate, up, down))],
            "adamw_" + prefix)
        outs[prefix + "_w_gate"] = tuple(r.T for r in res[0])
        outs[prefix + "_w_up"] = tuple(r.T for r in res[1])
        outs[prefix + "_w_down"] = res[2]

    update_ffn("ffn1", got_1a, got_1b, (ffn1_w_gate, m_ffn1_w_gate, v_ffn1_w_gate),
               (ffn1_w_up, m_ffn1_w_up, v_ffn1_w_up), (ffn1_w_down, m_ffn1_w_down, v_ffn1_w_down))
    update_ffn("ffn2", got_2a, got_2b, (ffn2_w_gate, m_ffn2_w_gate, v_ffn2_w_gate),
               (ffn2_w_up, m_ffn2_w_up, v_ffn2_w_up), (ffn2_w_down, m_ffn2_w_down, v_ffn2_w_down))
    update("w_in", got_in, w_in, m_w_in, v_w_in)
    update("w_out", got_out, w_out, m_w_out, v_w_out)

    g_small = _from_sheet(_sum_stack(got_small, "sum_small"), spans)
    g_norm_ffn1 = _sum_stack(got_n1, "sum_dnorm_ffn1").reshape(norm_ffn1.shape)
    g_conv = lax.dynamic_slice_in_dim(g_small[3], me * conv_shard, conv_shard, axis=1)
    small_names = ["norm_ffn1", "norm_mix", "norm_ffn2", "norm_final", "conv_w", "pool_w", "pool_scale"]
    small_g = [g_norm_ffn1, g_small[0], g_small[1], g_small[2].reshape(norm_final.shape), g_conv[None],
               g_small[4][None], g_small[5]]
    small_w = [norm_ffn1, norm_mix, norm_ffn2, norm_final, conv_w, pool_w, pool_scale]
    small_m = [m_norm_ffn1, m_norm_mix, m_norm_ffn2, m_norm_final, m_conv_w, m_pool_w, m_pool_scale]
    small_v = [v_norm_ffn1, v_norm_mix, v_norm_ffn2, v_norm_final, v_conv_w, v_pool_w, v_pool_scale]
    g_sheet, spans_u = _to_sheet(small_g)
    w_sheet, _ = _to_sheet(small_w)
    m_sheet, _ = _to_sheet(small_m)
    v_sheet, _ = _to_sheet(small_v)
    upd = _adamw(g_sheet[None], w_sheet, m_sheet, v_sheet, "adamw_small")
    small_out = [_from_sheet(u, spans_u) for u in upd]
    for k, nm in enumerate(small_names):
        outs[nm] = tuple(small_out[j][k] for j in range(4))

    loss = g_small[6][0, 0]
    order = ["norm_ffn1", "ffn1_w_gate", "ffn1_w_up", "ffn1_w_down", "norm_mix", "w_in", "conv_w", "pool_w",
             "pool_scale", "w_out", "norm_ffn2", "ffn2_w_gate", "ffn2_w_up", "ffn2_w_down", "norm_final"]
    big = {"ffn1_w_gate", "ffn1_w_up", "ffn1_w_down", "w_in", "w_out", "ffn2_w_gate", "ffn2_w_up", "ffn2_w_down"}

    def leaf(nm, j):
        val = outs[nm][j]
        return val[None] if nm in big else val

    return (loss, dx0[None],
            *[leaf(nm, 0) for nm in order], *[leaf(nm, 1) for nm in order],
            *[leaf(nm, 2) for nm in order], *[leaf(nm, 3) for nm in order])
```

```python
import jax
import jax.numpy as jnp
from jax import lax
from jax.experimental import pallas as pl
from jax.experimental.pallas import tpu as pltpu

F32 = jnp.float32
LOW = jnp.bfloat16

N_DEV = 8
EPS = 1e-6
D_CONV = 512
POOL_WINDOWS = (2, 4, 8, 16)
POOL_GC = 128
HALO = 16
W_IN_SHARD = 256

ADAM_LR = 0.001
ADAM_B1 = 0.9
ADAM_B2 = 0.999
ADAM_EPS = 1e-08
ADAM_WD = 0.01
ADAM_STEP = 10

VMEM_LIMIT_BYTES = 56 * 1024 * 1024
TOK_TILE = 512
MIX_TOK_TILE = 512
WGRAD_TOK_TILE = 4096
WGRAD_ROW_CANDIDATES = (256, 128)
RELAY_LAST_LATER = 0.25
BWD_ROW_SLAB = 2048


def _params(*sem):
    return pltpu.CompilerParams(dimension_semantics=sem, vmem_limit_bytes=VMEM_LIMIT_BYTES)


def _resident(arr):
    return pl.BlockSpec(arr.shape, lambda *_: (0,) * arr.ndim, pipeline_mode=pl.Buffered(1))


def _pick(n, candidates):
    for c in candidates:
        if n % c == 0:
            return c
    raise ValueError(f"no tile in {candidates} divides {n}")


def _dot(a, b):
    return lax.dot_general(a, b, (((1,), (0,)), ((), ())), preferred_element_type=F32)


def _dot_nt(a, b):
    return lax.dot_general(a, b, (((1,), (1,)), ((), ())), preferred_element_type=F32)


def _dot_tn(a, b):
    return lax.dot_general(a, b, (((0,), (0,)), ((), ())), preferred_element_type=F32)


def _rms_scale(x):
    return lax.rsqrt(jnp.mean(x * x, axis=-1, keepdims=True) + EPS)


def _rms_bwd(dy, x, g):
    r = _rms_scale(x)
    xhat = x * r
    gdy = dy * g
    dx = r * (gdy - xhat * jnp.mean(gdy * xhat, axis=-1, keepdims=True))
    return dx, jnp.sum(dy * xhat, axis=0, keepdims=True)


COLLECTIVE_PAIR, COLLECTIVE_CHIPS, COLLECTIVE_RELAY = 0, 1, 2


def _handshake(peer_numbers):
    mx, my, mc = lax.axis_index("x"), lax.axis_index("y"), lax.axis_index("c")
    barrier = pltpu.get_barrier_semaphore()
    for m in peer_numbers:
        peer = (lax.rem(mx + ((m >> 2) & 1), 2), lax.rem(my + ((m >> 1) & 1), 2), lax.rem(mc + (m & 1), 2))
        pl.semaphore_signal(barrier, inc=1, device_id=peer, device_id_type=pl.DeviceIdType.MESH)
    pl.semaphore_wait(barrier, len(peer_numbers))


class _Exchange:
    CHIPS = (2, 4, 6)

    def __init__(self, arrays, sliced, relay_at=None, among_chips=False):
        assert relay_at is None or not any(sliced)
        assert not among_chips or (all(sliced) and relay_at is None)
        self.relay_at, self.among_chips = relay_at, among_chips
        self.peers = self.CHIPS if among_chips else ((1, 2, 4) if relay_at is not None else tuple(range(1, N_DEV)))
        self.collective_id = COLLECTIVE_CHIPS if among_chips else (COLLECTIVE_RELAY if relay_at is not None else None)
        self.rows = [arr[1:] if isinstance(arr, tuple) else None for arr in arrays]
        self.arrays = [arr[0] if isinstance(arr, tuple) else arr for arr in arrays]
        self.sliced, self.n = list(sliced), len(arrays)
        assert all(rg is None or not sl for rg, sl in zip(self.rows, sliced))
        self.block_shape = [arr.shape if rg is None else (rg[1],) + arr.shape[1:]
                            for arr, rg in zip(self.arrays, self.rows)]
        self.out_shape = [jax.ShapeDtypeStruct(shape if sl else (N_DEV,) + shape, arr.dtype)
                          for arr, shape, sl in zip(self.arrays, self.block_shape, sliced)]
        self.specs = [pl.BlockSpec(memory_space=pl.ANY)] * self.n
        self.scratch_shapes = [pltpu.SemaphoreType.DMA((self.n, N_DEV)),
                               pltpu.SemaphoreType.DMA((self.n, N_DEV)),
                               pltpu.SemaphoreType.DMA((self.n,))]

    HALF_VIA = ((4, 2, 5), (2, 4, 7))

    def _halves(self, a):
        rows = self.block_shape[a][0]
        if rows % 32:
            return ((0, rows), None)
        return ((0, rows // 2), (rows // 2, rows // 2))

    def _copies(self, ins, outs, sems):
        send_sems, recv_sems, local_sems = sems
        sliced = self.sliced
        mx, my, mc = lax.axis_index("x"), lax.axis_index("y"), lax.axis_index("c")
        me = 2 * mx + my if self.among_chips else 4 * mx + 2 * my + mc

        def peer(m):
            px = lax.rem(mx + ((m >> 2) & 1), 2)
            py = lax.rem(my + ((m >> 1) & 1), 2)
            pc = lax.rem(mc + (m & 1), 2)
            return (px, py, pc), (2 * px + py if self.among_chips else 4 * px + 2 * py + pc)

        def mine(a):
            return ins[a] if self.rows[a] is None else ins[a].at[pl.ds(*self.rows[a])]

        def remote(a, m, arriving):
            pid, pflat = peer(m)
            return pltpu.make_async_remote_copy(
                src_ref=ins[a].at[pflat] if sliced[a] else mine(a),
                dst_ref=outs[a].at[pflat if arriving else me],
                send_sem=send_sems.at[a, m - 1],
                recv_sem=recv_sems.at[a, m - 1],
                device_id=pid,
                device_id_type=pl.DeviceIdType.MESH,
            )

        def local(a):
            return pltpu.make_async_copy(ins[a].at[me] if sliced[a] else mine(a), outs[a].at[me], local_sems.at[a])

        def passed_on(a, m):
            _, origin = peer(m)
            sibling, _ = peer(1)
            return pltpu.make_async_remote_copy(
                src_ref=outs[a].at[origin],
                dst_ref=outs[a].at[origin],
                send_sem=send_sems.at[a, m],
                recv_sem=recv_sems.at[a, m],
                device_id=sibling,
                device_id_type=pl.DeviceIdType.MESH,
            )

        def half_on(a, h, arriving):
            via, to, column = self.HALF_VIA[h]
            r0, nr = self._halves(a)[h]
            _, origin = peer(6 if arriving else via)
            rows = outs[a].at[origin].at[pl.ds(r0, nr)]
            return pltpu.make_async_remote_copy(
                src_ref=rows, dst_ref=rows, send_sem=send_sems.at[a, column], recv_sem=recv_sems.at[a, column],
                device_id=peer(to)[0], device_id_type=pl.DeviceIdType.MESH)

        return remote, local, passed_on, half_on

    def start(self, ins, outs, sems):
        remote, local, _, _ = self._copies(ins, outs, sems)
        if self.collective_id is not None:
            _handshake(self.peers)
        for a in range(self.n):
            local(a).start()
        for m in self.peers:
            for a in range(self.n):
                remote(a, m, False).start()

    def relay(self, ins, outs, sems):
        remote, _, passed_on, half_on = self._copies(ins, outs, sems)
        for h, (via, _, _) in enumerate(self.HALF_VIA):
            for a in range(self.n):
                remote(a, via, True).wait_recv()
                passed_on(a, via).start()
                if self._halves(a)[h] is not None:
                    half_on(a, h, False).start()

    def relay_last(self, ins, outs, sems):
        _, _, passed_on, half_on = self._copies(ins, outs, sems)
        for a in range(self.n):
            for h in range(2):
                if self._halves(a)[h] is not None:
                    half_on(a, h, True).wait_recv()
            passed_on(a, 6).start()

    def wait(self, ins, outs, sems):
        remote, local, passed_on, half_on = self._copies(ins, outs, sems)
        if self.relay_at is None:
            for m in self.peers:
                for a in range(self.n):
                    remote(a, m, True).wait_recv()
            for m in self.peers:
                for a in range(self.n):
                    remote(a, m, False).wait_send()
        else:
            for m in (1, 3, 5, 7):
                for a in range(self.n):
                    remote(a, m, True).wait_recv()
            for m in self.peers:
                for a in range(self.n):
                    remote(a, m, False).wait_send()
            for a in range(self.n):
                for m in self.CHIPS:
                    passed_on(a, m).wait_send()
                for h in range(2):
                    if self._halves(a)[h] is not None:
                        half_on(a, h, False).wait_send()
        for a in range(self.n):
            local(a).wait()


def _pair_sum(stacks, name):
    n = len(stacks)
    n_chip = N_DEV // 2
    half = [(n_chip,) + st.shape[1:] for st in stacks]

    def body(*refs):
        ins, outs, mine, theirs = refs[:n], refs[n:2 * n], refs[2 * n:3 * n], refs[3 * n:4 * n]
        local_sems, send_sems, recv_sems = refs[4 * n:]
        mx, my, mc = lax.axis_index("x"), lax.axis_index("y"), lax.axis_index("c")

        def own(a, k):
            return pltpu.make_async_copy(ins[a].at[2 * k + mc], mine[a].at[k], local_sems.at[a, k])

        def swap(a, k):
            return pltpu.make_async_remote_copy(
                src_ref=ins[a].at[2 * k + (1 - mc)], dst_ref=theirs[a].at[k],
                send_sem=send_sems.at[a, k], recv_sem=recv_sems.at[a, k],
                device_id=(mx, my, 1 - mc), device_id_type=pl.DeviceIdType.MESH)

        _handshake((1,))
        for k in range(n_chip):
            for a in range(n):
                own(a, k).start()
                swap(a, k).start()
        for k in range(n_chip):
            for a in range(n):
                own(a, k).wait()
                swap(a, k).wait()
                outs[a][k] = (mine[a][k].astype(F32) + theirs[a][k].astype(F32)).astype(LOW)

    return pl.pallas_call(
        body, name=name,
        out_shape=[jax.ShapeDtypeStruct(h, LOW) for h in half],
        in_specs=[pl.BlockSpec(memory_space=pl.ANY)] * n,
        out_specs=[pl.BlockSpec(memory_space=pltpu.VMEM)] * n,
        scratch_shapes=([pltpu.VMEM(h, st.dtype) for h, st in zip(half, stacks)] * 2
                        + [pltpu.SemaphoreType.DMA((n, n_chip))] * 3),
        compiler_params=pltpu.CompilerParams(vmem_limit_bytes=VMEM_LIMIT_BYTES, collective_id=COLLECTIVE_PAIR),
    )(*stacks)


def _exchange(arrays, sliced, name, relay=False):
    ex = _Exchange(arrays, sliced, relay_at=0 if relay else None)

    def body(*refs):
        ins, outs, sems = refs[:ex.n], refs[ex.n:2 * ex.n], refs[2 * ex.n:]
        ex.start(ins, outs, sems)
        if relay:
            ex.relay(ins, outs, sems)
            ex.relay_last(ins, outs, sems)
        ex.wait(ins, outs, sems)

    return pl.pallas_call(body, name=name, out_shape=ex.out_shape, in_specs=ex.specs, out_specs=ex.specs,
                          scratch_shapes=ex.scratch_shapes,
                          compiler_params=pltpu.CompilerParams(collective_id=ex.collective_id))(*ex.arrays)


def _call(body, *, name, grid, in_specs, out_specs, out_shape, args, scratch_shapes=(), exchange=None):
    params = _params(*(("arbitrary",) * len(grid)))
    if exchange is None:
        return pl.pallas_call(body, name=name, grid=grid, in_specs=in_specs, out_specs=out_specs, out_shape=out_shape,
                              scratch_shapes=list(scratch_shapes), compiler_params=params)(*args)
    exs = list(exchange) if isinstance(exchange, (list, tuple)) else [exchange]
    assert len(exs) == 1 or all(ex.collective_id is None for ex in exs)
    params = pltpu.CompilerParams(dimension_semantics=("arbitrary",) * len(grid), vmem_limit_bytes=VMEM_LIMIT_BYTES,
                                  collective_id=exs[0].collective_id)
    n_in, n_out, n_scr = len(in_specs), len(out_specs), len(scratch_shapes)
    n_ex = sum(ex.n for ex in exs)
    n_steps = 1
    for g in grid:
        n_steps *= g

    def hosted(*refs):
        ins, refs = refs[:n_in], refs[n_in:]
        ex_ins, refs = refs[:n_ex], refs[n_ex:]
        outs, refs = refs[:n_out], refs[n_out:]
        ex_outs, refs = refs[:n_ex], refs[n_ex:]
        scr, sems = refs[:n_scr], refs[n_scr:]
        parts, at = [], 0
        for j, ex in enumerate(exs):
            parts.append((ex_ins[at:at + ex.n], ex_outs[at:at + ex.n], sems[3 * j:3 * j + 3]))
            at += ex.n
        step = pl.program_id(0)
        for ax in range(1, len(grid)):
            step = step * grid[ax] + pl.program_id(ax)

        @pl.when(step == 0)
        def _():
            for ex, part in zip(exs, parts):
                ex.start(*part)

        body(*ins, *outs, *scr)

        for ex, part in zip(exs, parts):
            if ex.relay_at is not None:
                @pl.when(step == min(int(ex.relay_at * n_steps), n_steps - 1))
                def _():
                    ex.relay(*part)

                @pl.when(step == min(int((ex.relay_at + RELAY_LAST_LATER) * n_steps), n_steps - 1))
                def _():
                    ex.relay_last(*part)

        @pl.when(step == n_steps - 1)
        def _():
            for ex, part in zip(exs, parts):
                ex.wait(*part)

    return pl.pallas_call(
        hosted, name=name, grid=grid,
        in_specs=list(in_specs) + [sp for ex in exs for sp in ex.specs],
        out_specs=list(out_specs) + [sp for ex in exs for sp in ex.specs],
        out_shape=list(out_shape) + [sh for ex in exs for sh in ex.out_shape],
        scratch_shapes=list(scratch_shapes) + [sc for ex in exs for sc in ex.scratch_shapes],
        compiler_params=params)(*args, *[arr for ex in exs for arr in ex.arrays])


def _ffn_fwd(x, g, h, wgt, wut, wd, name, exchange=None):
    s_len, d = x.shape
    fc = wd.shape[0]
    ts = min(TOK_TILE, s_len)
    first = h is None

    def body(*refs):
        x_ref, gh_ref, wg_ref, wu_ref, wd_ref, xo_ref = refs[:6]
        a_ref, b_ref, s_ref = refs[-3:]
        xv = x_ref[...]
        if first:
            hb = (xv * _rms_scale(xv) * gh_ref[...]).astype(LOW)
            refs[6][...] = hb
        else:
            hb = gh_ref[...]
        for c0, c1 in _slabs(fc):
            a = _dot_nt(hb, wg_ref[c0:c1, :])
            b = _dot_nt(hb, wu_ref[c0:c1, :])
            s_ref[:, c0:c1] = (a * jax.nn.sigmoid(a) * b).astype(LOW)
            a_ref[:, c0:c1] = a.astype(LOW)
            b_ref[:, c0:c1] = b.astype(LOW)
        xo_ref[...] = xv + 0.5 * _dot(s_ref[...], wd_ref[...])

    tok = pl.BlockSpec((ts, d), lambda t: (t, 0))
    hid = pl.BlockSpec((ts, fc), lambda t: (t, 0))
    tok_out = jax.ShapeDtypeStruct((s_len, d), F32)
    h_out = jax.ShapeDtypeStruct((s_len, d), LOW)
    hid_out = jax.ShapeDtypeStruct((s_len, fc), LOW)
    return _call(
        body,
        name=name,
        grid=(s_len // ts,),
        in_specs=[tok, pl.BlockSpec((1, d), lambda t: (0, 0)) if first else tok,
                  _resident(wgt), _resident(wut), _resident(wd)],
        out_specs=[tok] + ([tok] if first else []) + [hid, hid],
        out_shape=[tok_out] + ([h_out] if first else []) + [hid_out, hid_out],
        scratch_shapes=[pltpu.VMEM((ts, fc), LOW)],
        args=(x, g if first else h, wgt, wut, wd),
        exchange=exchange,
    )


def _slabs(width, slab=256):
    return [(c0, min(c0 + slab, width)) for c0 in range(0, width, slab)]


def _ffn_bwdw(df, a, b, h, wd, name, exchange=None):
    s_len, d = df.shape
    f_len = wd.shape[0]
    tm = _pick(f_len, WGRAD_ROW_CANDIDATES)
    tk = min(WGRAD_TOK_TILE, s_len)
    n_k = s_len // tk

    def body(df_ref, a_ref, b_ref, h_ref, wd_ref, da_ref, db_ref, dwd_ref, dwg_ref, dwu_ref,
             s_ref, acc_d, acc_g, acc_u):
        k = pl.program_id(1)

        @pl.when(k == 0)
        def _():
            acc_d[...] = jnp.zeros_like(acc_d)
            acc_g[...] = jnp.zeros_like(acc_g)
            acc_u[...] = jnp.zeros_like(acc_u)

        wdv = wd_ref[...]
        for r0, r1 in _slabs(tk, BWD_ROW_SLAB):
            ds = _dot_nt(df_ref[r0:r1, :], wdv)
            av = a_ref[r0:r1, :].astype(F32)
            bv = b_ref[r0:r1, :].astype(F32)
            sig = jax.nn.sigmoid(av)
            silu = av * sig
            s_ref[r0:r1, :] = (silu * bv).astype(LOW)
            da_ref[r0:r1, :] = (ds * bv * (sig * (1.0 + av * (1.0 - sig)))).astype(LOW)
            db_ref[r0:r1, :] = (ds * silu).astype(LOW)
            hv = h_ref[r0:r1, :]
            acc_d[...] += _dot_tn(s_ref[r0:r1, :], df_ref[r0:r1, :])
            acc_g[...] += _dot_tn(da_ref[r0:r1, :], hv)
            acc_u[...] += _dot_tn(db_ref[r0:r1, :], hv)

        @pl.when(k == n_k - 1)
        def _():
            dwd_ref[...] = acc_d[...].astype(LOW)
            dwg_ref[...] = acc_g[...].astype(LOW)
            dwu_ref[...] = acc_u[...].astype(LOW)

    hid = pl.BlockSpec((tk, tm), lambda i, k: (k, i))
    tok = pl.BlockSpec((tk, d), lambda i, k: (k, 0))
    wrow = pl.BlockSpec((tm, d), lambda i, k: (i, 0))
    return _call(
        body,
        name=name,
        grid=(f_len // tm, n_k),
        in_specs=[tok, hid, hid, tok, wrow],
        out_specs=[hid, hid, wrow, wrow, wrow],
        out_shape=[jax.ShapeDtypeStruct((s_len, f_len), LOW)] * 2 + [jax.ShapeDtypeStruct((f_len, d), LOW)] * 3,
        scratch_shapes=[pltpu.VMEM((tk, tm), LOW)] + [pltpu.VMEM((tm, d), F32)] * 3,
        args=(df, a, b, h, wd),
        exchange=exchange,
    )


def _ffn_dx(dxo, x, g, parts, name, exchange=None):
    s_len, d = x.shape
    ts = min(TOK_TILE, s_len)
    n_p = len(parts)

    def body(dxo_ref, x_ref, g_ref, *refs):
        dxi_ref, dg_ref = refs[4 * n_p:]

        @pl.when(pl.program_id(0) == 0)
        def _():
            dg_ref[...] = jnp.zeros_like(dg_ref)

        dh = None
        for p in range(n_p):
            da_ref, db_ref, wg_ref, wu_ref = refs[4 * p:4 * p + 4]
            part = _dot(da_ref[...], wg_ref[...]) + _dot(db_ref[...], wu_ref[...])
            dh = part if dh is None else dh + part
        dx, dgp = _rms_bwd(dh, x_ref[...], g_ref[...])
        dxi_ref[...] = dxo_ref[...] + dx
        dg_ref[...] += dgp

    tok = pl.BlockSpec((ts, d), lambda t: (t, 0))
    vec = pl.BlockSpec((1, d), lambda t: (0, 0))
    part_specs, part_args = [], []
    for da, db, wgt, wut in parts:
        hid = pl.BlockSpec((ts, da.shape[1]), lambda t: (t, 0))
        part_specs += [hid, hid, _resident(wgt), _resident(wut)]
        part_args += [da, db, wgt, wut]
    return _call(
        body,
        name=name,
        grid=(s_len // ts,),
        in_specs=[tok, tok, vec] + part_specs,
        out_specs=[tok, vec],
        out_shape=[jax.ShapeDtypeStruct((s_len, d), F32), jax.ShapeDtypeStruct((1, d), F32)],
        args=(dxo, x, g, *part_args),
        exchange=exchange,
    )


def _wgrad_tn(xm, ym, tn, stacked, name, exchange=None):
    s_len, m = xm.shape
    n = ym.shape[1]
    tk = min(WGRAD_TOK_TILE, s_len)
    n_k = s_len // tk

    def body(x_ref, y_ref, o_ref, acc):
        k = pl.program_id(1)

        @pl.when(k == 0)
        def _():
            acc[...] = jnp.zeros_like(acc)

        acc[...] += _dot_tn(x_ref[...].astype(LOW), y_ref[...].astype(LOW))

        @pl.when(k == n_k - 1)
        def _():
            o_ref[...] = acc[...].astype(LOW)

    if stacked:
        out_spec = pl.BlockSpec((None, m, tn), lambda j, k: (j, 0, 0))
        out_shape = jax.ShapeDtypeStruct((n // tn, m, tn), LOW)
    else:
        out_spec = pl.BlockSpec((m, tn), lambda j, k: (0, j))
        out_shape = jax.ShapeDtypeStruct((m, n), LOW)
    return _call(
        body,
        name=name,
        grid=(n // tn, n_k),
        in_specs=[pl.BlockSpec((tk, m), lambda j, k: (k, 0)), pl.BlockSpec((tk, tn), lambda j, k: (k, j))],
        out_specs=[out_spec],
        out_shape=[out_shape],
        scratch_shapes=[pltpu.VMEM((m, tn), F32)],
        args=(xm, ym),
        exchange=exchange,
    )


def _mix_parts(ext_ref, cw, ts, row0, with_pooled=True):
    dc = D_CONV

    def back(off, c0, c1):
        return ext_ref[HALO - off:HALO - off + ts, c0:c1]

    v, gate_b, gate_c = back(0, 0, dc), back(0, dc, 2 * dc), back(0, 2 * dc, 3 * dc)
    z0 = gate_c * v
    z1 = back(1, 2 * dc, 3 * dc) * back(1, 0, dc)
    z2 = back(2, 2 * dc, 3 * dc) * back(2, 0, dc)
    conv = cw[2:3, :] * z0 + cw[1:2, :] * z1 + cw[0:1, :] * z2
    rows = row0 + lax.broadcasted_iota(jnp.int32, (ts, 1), 0)
    pooled, inv_count = [], []
    for grp, w in enumerate(POOL_WINDOWS):
        inv = 1.0 / jnp.minimum(rows + 1, w).astype(F32)
        inv_count.append(inv)
        if with_pooled:
            c0 = 3 * dc + POOL_GC * grp
            u = back(0, c0, c0 + POOL_GC)
            acc = u
            for j in range(1, w):
                acc = acc + back(j, c0, c0 + POOL_GC)
            pooled.append(acc * inv - u)
    return v, gate_b, gate_c, z0, z1, z2, conv, pooled, inv_count


def _mix_fwd(x, g, w_in, conv_w, pool_w, pool_scale, w_out, name, exchange=None):
    s_len, d = x.shape
    n_blk, _, wcols = w_in.shape
    p_len = n_blk * wcols
    d_mix = w_out.shape[0]
    ts = min(MIX_TOK_TILE, s_len)
    dc = D_CONV

    def body(x_ref, g_ref, win_ref, cw_ref, pw_ref, ps_ref, wout_ref, x2_ref, h_ref, proj_ref, pooled_ref,
             ext_ref, cat_ref):
        t = pl.program_id(0)

        @pl.when(t == 0)
        def _():
            ext_ref[0:HALO, :] = jnp.zeros((HALO, p_len), F32)

        xv = x_ref[...]
        hb = (xv * _rms_scale(xv) * g_ref[...]).astype(LOW)
        h_ref[...] = hb
        for k in range(n_blk):
            ext_ref[HALO:HALO + ts, wcols * k:wcols * (k + 1)] = _dot(hb, win_ref[k])
        proj_ref[...] = ext_ref[HALO:HALO + ts, :]

        _, gate_b, _, _, _, _, conv, pooled, _ = _mix_parts(ext_ref, cw_ref[...], ts, t * ts)
        cat_ref[:, 0:dc] = (gate_b * conv).astype(LOW)
        for grp in range(len(POOL_WINDOWS)):
            c0 = POOL_GC * grp
            pooled_b = pooled[grp].astype(LOW)
            pooled_ref[:, c0:c0 + POOL_GC] = pooled_b
            lin = _dot(pooled_b, pw_ref[grp])
            cat_ref[:, dc + c0:dc + c0 + POOL_GC] = (lin * ps_ref[:, c0:c0 + POOL_GC]).astype(LOW)
        x2_ref[...] = xv + _dot(cat_ref[...], wout_ref[...])
        ext_ref[0:HALO, :] = ext_ref[ts:ts + HALO, :]

    tok = pl.BlockSpec((ts, d), lambda t: (t, 0))

    def whole(arr):
        return pl.BlockSpec(arr.shape, lambda t: (0,) * arr.ndim)

    return _call(
        body,
        name=name,
        grid=(s_len // ts,),
        in_specs=[tok, whole(g), _resident(w_in), whole(conv_w), whole(pool_w), whole(pool_scale), _resident(w_out)],
        out_specs=[tok, tok, pl.BlockSpec((ts, p_len), lambda t: (t, 0)),
                   pl.BlockSpec((ts, d_mix - dc), lambda t: (t, 0))],
        out_shape=[
            jax.ShapeDtypeStruct((s_len, d), F32),
            jax.ShapeDtypeStruct((s_len, d), LOW),
            jax.ShapeDtypeStruct((s_len, p_len), F32),
            jax.ShapeDtypeStruct((s_len, d_mix - dc), LOW),
        ],
        scratch_shapes=[pltpu.VMEM((ts + HALO, p_len), F32), pltpu.VMEM((ts, d_mix), LOW)],
        args=(x, g, w_in, conv_w, pool_w, pool_scale, w_out),
        exchange=exchange,
    )


def _mix_bwd(dx2, x, proj, pooled, g, w_in, conv_w, pool_w, pool_scale, w_out, name, exchange=None):
    s_len, d = x.shape
    n_blk, _, wcols = w_in.shape
    p_len = n_blk * wcols
    d_mix = w_out.shape[0]
    ts = min(MIX_TOK_TILE, s_len)
    n_t = s_len // ts
    dc = D_CONV
    n_grp = len(POOL_WINDOWS)

    def body(dx2_ref, x_ref, proj_ref, halo_ref, pooled_ref, g_ref, win_ref, cw_ref, pw_ref, ps_ref, wout_ref,
             dx_ref, dproj_ref, dwout_ref, dg_ref, dcw_ref, dpw_ref, dps_ref, df_ref,
             ext_ref, fut_ref, cat_ref, dwout_acc, win_a, win_b):
        i = pl.program_id(0)
        t = n_t - 1 - i

        @pl.when(i == 0)
        def _():
            win_a[...] = jnp.zeros_like(win_a)
            win_b[...] = jnp.zeros_like(win_b)
            dwout_acc[...] = jnp.zeros_like(dwout_acc)
            dg_ref[...] = jnp.zeros_like(dg_ref)
            dcw_ref[...] = jnp.zeros_like(dcw_ref)
            dpw_ref[...] = jnp.zeros_like(dpw_ref)
            dps_ref[...] = jnp.zeros_like(dps_ref)
            fut_ref[ts:ts + 2 * HALO, :] = jnp.zeros((2 * HALO, d_mix), F32)

        ext_ref[HALO:HALO + ts, :] = proj_ref[...]

        @pl.when(t == 0)
        def _():
            ext_ref[0:HALO, :] = jnp.zeros((HALO, p_len), F32)

        @pl.when(t > 0)
        def _():
            ext_ref[0:HALO, :] = halo_ref[...]

        cw = cw_ref[...]
        v, gate_b, gate_c, z0, z1, z2, conv, _, inv_count = _mix_parts(ext_ref, cw, ts, t * ts, with_pooled=False)
        dx2 = dx2_ref[...]
        dcat = _dot_nt(dx2.astype(LOW), wout_ref[...])

        dy_a = dcat[:, 0:dc]
        dconv = dy_a * gate_b
        fut_ref[0:ts, 0:dc] = dconv
        cat_ref[:, 0:dc] = (gate_b * conv).astype(LOW)
        dproj_ref[:, dc:2 * dc] = (dy_a * conv).astype(LOW)
        dcw_ref[2:3, :] += jnp.sum(dconv * z0, axis=0, keepdims=True)
        dcw_ref[1:2, :] += jnp.sum(dconv * z1, axis=0, keepdims=True)
        dcw_ref[0:1, :] += jnp.sum(dconv * z2, axis=0, keepdims=True)

        dpool = []
        for grp in range(n_grp):
            c0 = POOL_GC * grp
            pooled_b = pooled_ref[:, c0:c0 + POOL_GC]
            lin = _dot(pooled_b, pw_ref[grp])
            dy_b = dcat[:, dc + c0:dc + c0 + POOL_GC]
            scale = ps_ref[:, c0:c0 + POOL_GC]
            cat_ref[:, dc + c0:dc + c0 + POOL_GC] = (lin * scale).astype(LOW)
            dps_ref[:, c0:c0 + POOL_GC] += jnp.sum(dy_b * lin, axis=0, keepdims=True)
            dlin = (dy_b * scale).astype(LOW)
            dpw_ref[grp] += _dot_tn(pooled_b, dlin)
            dpool.append(_dot_nt(dlin, pw_ref[grp]))
            fut_ref[0:ts, dc + c0:dc + c0 + POOL_GC] = dpool[grp] * inv_count[grp]
        dwout_acc[...] += _dot_tn(cat_ref[...], dx2.astype(LOW))

        def ahead(off, c0, c1):
            return fut_ref[off:off + ts, c0:c1]

        dz = cw[2:3, :] * ahead(0, 0, dc) + cw[1:2, :] * ahead(1, 0, dc) + cw[0:1, :] * ahead(2, 0, dc)
        dproj_ref[:, 0:dc] = (dz * gate_c).astype(LOW)
        dproj_ref[:, 2 * dc:3 * dc] = (dz * v).astype(LOW)
        rows_ext = ts + HALO
        for grp, w in enumerate(POOL_WINDOWS):
            c0 = dc + POOL_GC * grp
            src, dst, other = fut_ref, win_a, win_b
            cols = slice(c0, c0 + POOL_GC)
            span = 1
            while span < w:
                dst[0:rows_ext, :] = src[0:rows_ext, cols] + src[span:span + rows_ext, cols]
                src, dst, other, cols = dst, other, dst, slice(0, POOL_GC)
                span *= 2
            dproj_ref[:, 2 * dc + c0:2 * dc + c0 + POOL_GC] = (src[0:ts, cols] - dpool[grp]).astype(LOW)

        dh = _dot_nt(dproj_ref[:, 0:wcols], win_ref[0])
        for k in range(1, n_blk):
            dh += _dot_nt(dproj_ref[:, wcols * k:wcols * (k + 1)], win_ref[k])
        dx, dgp = _rms_bwd(dh, x_ref[...], g_ref[...])
        dx = dx2 + dx
        dx_ref[...] = dx
        df_ref[...] = (0.5 * dx).astype(LOW)
        dg_ref[...] += dgp
        fut_ref[ts:ts + HALO, :] = fut_ref[0:HALO, :]

        @pl.when(i == n_t - 1)
        def _():
            dwout_ref[...] = dwout_acc[...].astype(LOW)

    tok = pl.BlockSpec((ts, d), lambda i: (n_t - 1 - i, 0))
    halo = pl.BlockSpec((HALO, p_len), lambda i: (jnp.maximum((n_t - 1 - i) * (ts // HALO) - 1, 0), 0))

    def whole(arr):
        return pl.BlockSpec(arr.shape, lambda i: (0,) * arr.ndim)

    return _call(
        body,
        name=name,
        grid=(n_t,),
        in_specs=[tok, tok, pl.BlockSpec((ts, p_len), lambda i: (n_t - 1 - i, 0)), halo,
                  pl.BlockSpec((ts, d_mix - dc), lambda i: (n_t - 1 - i, 0)),
                  whole(g), _resident(w_in), whole(conv_w), whole(pool_w), whole(pool_scale), _resident(w_out)],
        out_specs=[tok, pl.BlockSpec((ts, p_len), lambda i: (n_t - 1 - i, 0)),
                   whole(w_out),
                   whole(g), whole(conv_w), whole(pool_w), whole(pool_scale), tok],
        out_shape=[
            jax.ShapeDtypeStruct((s_len, d), F32),
            jax.ShapeDtypeStruct((s_len, p_len), LOW),
            jax.ShapeDtypeStruct((d_mix, d), LOW),
            jax.ShapeDtypeStruct(g.shape, F32),
            jax.ShapeDtypeStruct(conv_w.shape, F32),
            jax.ShapeDtypeStruct(pool_w.shape, F32),
            jax.ShapeDtypeStruct(pool_scale.shape, F32),
            jax.ShapeDtypeStruct((s_len, d), LOW),
        ],
        scratch_shapes=[pltpu.VMEM((ts + HALO, p_len), F32), pltpu.VMEM((ts + 2 * HALO, d_mix), F32),
                        pltpu.VMEM((ts, d_mix), LOW), pltpu.VMEM((d_mix, d), F32),
                        pltpu.VMEM((ts + 2 * HALO, POOL_GC), F32), pltpu.VMEM((ts + 2 * HALO, POOL_GC), F32)],
        args=(dx2, x, proj, proj, pooled, g, w_in, conv_w, pool_w, pool_scale, w_out),
        exchange=exchange,
    )


def _ffn_fwd_loss(x, h, wgt, wut, wd, g, target, name):
    s_len, d = x.shape
    fc = wd.shape[0]
    ts = min(TOK_TILE, s_len)

    def body(x_ref, h_ref, wg_ref, wu_ref, wd_ref, g_ref, tgt_ref,
             a_ref, b_ref, loss_ref, dx_ref, dg_ref, df_ref, s_ref):
        @pl.when(pl.program_id(0) == 0)
        def _():
            loss_ref[...] = jnp.zeros_like(loss_ref)
            dg_ref[...] = jnp.zeros_like(dg_ref)

        hb = h_ref[...]
        for c0, c1 in _slabs(fc):
            a = _dot_nt(hb, wg_ref[c0:c1, :])
            b = _dot_nt(hb, wu_ref[c0:c1, :])
            s_ref[:, c0:c1] = (a * jax.nn.sigmoid(a) * b).astype(LOW)
            a_ref[:, c0:c1] = a.astype(LOW)
            b_ref[:, c0:c1] = b.astype(LOW)
        xv = x_ref[...] + 0.5 * _dot(s_ref[...], wd_ref[...])
        gv = g_ref[...]
        err = xv * _rms_scale(xv) * gv - tgt_ref[...]
        loss_ref[...] += 0.5 * jnp.sum(jnp.mean(err * err, axis=-1, keepdims=True), axis=0, keepdims=True)
        dx, dgp = _rms_bwd(err * (1.0 / d), xv, gv)
        dx_ref[...] = dx
        df_ref[...] = (0.5 * dx).astype(LOW)
        dg_ref[...] += dgp

    tok = pl.BlockSpec((ts, d), lambda t: (t, 0))
    vec = pl.BlockSpec((1, d), lambda t: (0, 0))
    hid = pl.BlockSpec((ts, fc), lambda t: (t, 0))
    return pl.pallas_call(
        body,
        name=name,
        grid=(s_len // ts,),
        in_specs=[tok, tok, _resident(wgt), _resident(wut), _resident(wd), vec, tok],
        out_specs=[hid, hid, pl.BlockSpec((1, 128), lambda t: (0, 0)), tok, vec, tok],
        out_shape=[
            jax.ShapeDtypeStruct((s_len, fc), LOW),
            jax.ShapeDtypeStruct((s_len, fc), LOW),
            jax.ShapeDtypeStruct((1, 128), F32),
            jax.ShapeDtypeStruct((s_len, d), F32),
            jax.ShapeDtypeStruct((1, d), F32),
            jax.ShapeDtypeStruct((s_len, d), LOW),
        ],
        scratch_shapes=[pltpu.VMEM((ts, fc), LOW)],
        compiler_params=_params("arbitrary"),
    )(x, h, wgt, wut, wd, g, target)


def _row_tile(rows, cols, stack_bytes):
    budget = 20 * 1024 * 1024
    per_row = cols * (4 * 7 + stack_bytes)
    for tr in (rows, 512, 256, 176, 128, 64, 32, 16, 8):
        if rows % tr == 0 and tr % 8 == 0 and tr * per_row * 2 <= budget:
            return tr
    return rows


def _sum_stack(stack, name):
    n, r, c = stack.shape
    tr = _row_tile(r, c, n * stack.dtype.itemsize)

    def body(s_ref, o_ref):
        acc = s_ref[0].astype(F32)
        for k in range(1, n):
            acc = acc + s_ref[k].astype(F32)
        o_ref[...] = acc

    return pl.pallas_call(
        body,
        name=name,
        grid=(r // tr,),
        in_specs=[pl.BlockSpec((n, tr, c), lambda i: (0, i, 0))],
        out_specs=pl.BlockSpec((tr, c), lambda i: (i, 0)),
        out_shape=jax.ShapeDtypeStruct((r, c), F32),
        compiler_params=_params("arbitrary"),
    )(stack)


def _adamw_many(params, name):
    params = [(list(st) if isinstance(st, (list, tuple)) else [st], w, m, v) for st, w, m, v in params]
    stacks0, w0 = params[0][0], params[0][1]
    r, c = w0.shape
    n = stacks0[0].shape[0]
    n_st = len(stacks0)
    part_rows = [st.shape[1] for st in stacks0]
    assert sum(part_rows) == r
    assert all(w.shape == (r, c) and [st.shape for st in sts] == [st.shape for st in stacks0] for sts, w, _, _ in params)
    first_row = [sum(part_rows[:j]) for j in range(n_st)]
    tc = next(t for t in (512, 256, 128) if c % t == 0)
    c1 = 1.0 - ADAM_B1 ** ADAM_STEP
    c2 = 1.0 - ADAM_B2 ** ADAM_STEP
    n_in = n_st + 3

    def body(*refs):
        ins, outs = refs[:n_in * len(params)], refs[n_in * len(params):]
        for p in range(len(params)):
            s_refs = ins[n_in * p:n_in * p + n_st]
            w_ref, m_ref, v_ref = ins[n_in * p + n_st:n_in * (p + 1)]
            g_ref, d_ref, mo_ref, vo_ref = outs[4 * p:4 * p + 4]
            for s_ref, r0, nr in zip(s_refs, first_row, part_rows):
                gv = s_ref[0].astype(F32)
                for k in range(1, n):
                    gv = gv + s_ref[k].astype(F32)
                mn = ADAM_B1 * m_ref[r0:r0 + nr, :] + (1.0 - ADAM_B1) * gv
                vn = ADAM_B2 * v_ref[r0:r0 + nr, :] + (1.0 - ADAM_B2) * (gv * gv)
                g_ref[r0:r0 + nr, :] = gv
                mo_ref[r0:r0 + nr, :] = mn
                vo_ref[r0:r0 + nr, :] = vn
                d_ref[r0:r0 + nr, :] = -ADAM_LR * ((mn / c1) / (jnp.sqrt(vn / c2) + ADAM_EPS)
                                                   + ADAM_WD * w_ref[r0:r0 + nr, :])

    blk = pl.BlockSpec((r, tc), lambda i: (0, i))
    one_in = [pl.BlockSpec((n, nr, tc), lambda i: (0, 0, i)) for nr in part_rows] + [blk, blk, blk]
    res = pl.pallas_call(
        body,
        name=name,
        grid=(c // tc,),
        in_specs=one_in * len(params),
        out_specs=[blk] * (4 * len(params)),
        out_shape=[jax.ShapeDtypeStruct((r, c), F32)] * (4 * len(params)),
        compiler_params=_params("arbitrary"),
    )(*[arr for sts, w, m, v in params for arr in (*sts, w, m, v)])
    return [tuple(res[4 * p:4 * p + 4]) for p in range(len(params))]


def _adamw(stacks, w, m, v, name):
    return _adamw_many([(stacks, w, m, v)], name)[0]


def _to_sheet(parts):
    sheets, spans = [], []
    row = 0
    for p in parts:
        flat = p.reshape(-1).astype(F32)
        rows = -(-flat.shape[0] // 1024) * 8
        flat = jnp.pad(flat, (0, rows * 128 - flat.shape[0]))
        sheets.append(flat.reshape(rows, 128))
        spans.append((row, p.size, p.shape))
        row += rows
    return jnp.concatenate(sheets, axis=0), spans


def _from_sheet(sheet, spans):
    out = []
    for row, size, shape in spans:
        rows = -(-size // 1024) * 8
        out.append(sheet[row:row + rows].reshape(-1)[:size].reshape(shape))
    return out


def kernel(x, norm_ffn1, ffn1_w_gate, ffn1_w_up, ffn1_w_down, norm_mix, w_in, conv_w, pool_w, pool_scale, w_out, norm_ffn2, ffn2_w_gate, ffn2_w_up, ffn2_w_down, norm_final, loss_target, m_norm_ffn1, m_ffn1_w_gate, m_ffn1_w_up, m_ffn1_w_down, m_norm_mix, m_w_in, m_conv_w, m_pool_w, m_pool_scale, m_w_out, m_norm_ffn2, m_ffn2_w_gate, m_ffn2_w_up, m_ffn2_w_down, m_norm_final, v_norm_ffn1, v_ffn1_w_gate, v_ffn1_w_up, v_ffn1_w_down, v_norm_mix, v_w_in, v_conv_w, v_pool_w, v_pool_scale, v_w_out, v_norm_ffn2, v_ffn2_w_gate, v_ffn2_w_up, v_ffn2_w_down, v_norm_final):
    me = 4 * lax.axis_index("x") + 2 * lax.axis_index("y") + lax.axis_index("c")
    xs, tgt = x[0], loss_target[0]
    s_len, d = xs.shape
    f_shard = ffn1_w_down.shape[1]
    conv_shard = conv_w.shape[2]

    def low_t(wt):
        return wt[0].T.astype(LOW)

    def by_dev(gw):
        return gw.reshape(N_DEV, -1, d)

    conv_tile = jnp.zeros((8, 128), F32).at[0:conv_w.shape[1], 0:conv_shard].set(conv_w[0])
    pool_w_low = pool_w[0].astype(LOW)

    rows_a = -(-f_shard // 64) * 32

    def parts_of(w_gate, w_up, w_down):
        shards = [low_t(w_gate), low_t(w_up), w_down[0].astype(LOW)]
        return [(s, 0, rows_a) for s in shards], [(s, rows_a, f_shard - rows_a) for s in shards]

    def rows_flat(stacks):
        return [st.reshape(-1, d) for st in stacks]

    def gather(shards):
        return _Exchange(shards, [False] * len(shards), relay_at=0.6)

    def scatter(pairs):
        return _Exchange(pairs, [True] * len(pairs), among_chips=True)

    w1a_shards, w1b_shards = parts_of(ffn1_w_gate, ffn1_w_up, ffn1_w_down)
    w2a_shards, w2b_shards = parts_of(ffn2_w_gate, ffn2_w_up, ffn2_w_down)

    wg1a, wu1a, wd1a = rows_flat(_exchange(w1a_shards, [False] * 3, "gather_ffn1_a", relay=True))
    xa, h1, a1a, b1a, *w1b = _ffn_fwd(xs, norm_ffn1, None, wg1a, wu1a, wd1a, "ffn1_fwd_a",
                                      exchange=gather(w1b_shards))
    wg1b, wu1b, wd1b = rows_flat(w1b)
    x1, a1b, b1b, w_in_full, w_out_full, conv_tiles = _ffn_fwd(
        xa, None, h1, wg1b, wu1b, wd1b, "ffn1_fwd_b",
        exchange=gather([w_in[0].astype(LOW), w_out[0].astype(LOW), conv_tile]))
    w_out_full = w_out_full.reshape(-1, d)
    conv_full = jnp.concatenate([conv_tiles[k, 0:conv_w.shape[1], 0:conv_shard] for k in range(N_DEV)], axis=1)
    x2, h2, proj, pooled, *w2a = _mix_fwd(x1, norm_mix, w_in_full, conv_full, pool_w_low, pool_scale, w_out_full,
                                          "mix_fwd", exchange=gather(w2a_shards))
    wg2a, wu2a, wd2a = rows_flat(w2a)
    xb, h3, a2a, b2a, *w2b = _ffn_fwd(x2, norm_ffn2, None, wg2a, wu2a, wd2a, "ffn2_fwd_a",
                                      exchange=gather(w2b_shards))
    wg2b, wu2b, wd2b = rows_flat(w2b)
    a2b, b2b, loss_row, dx3, dg_final, df3 = _ffn_fwd_loss(
        xb, h3, wg2b, wu2b, wd2b, norm_final.reshape(1, d), tgt, "ffn2_fwd_b_loss")

    da2a, db2a, dwd_a, dwg_a, dwu_a = _ffn_bwdw(df3, a2a, b2a, h3, wd2a, "ffn2_bwdw_a")
    da2b, db2b, dwd_b, dwg_b, dwu_b = _ffn_bwdw(df3, a2b, b2b, h3, wd2b, "ffn2_bwdw_b")
    pairs = _pair_sum([by_dev(dwg_a), by_dev(dwu_a), by_dev(dwd_a), by_dev(dwg_b), by_dev(dwu_b), by_dev(dwd_b)],
                      "pair_sum_ffn2")
    dx2, dg_ffn2, *got_2a = _ffn_dx(dx3, x2, norm_ffn2, [(da2a, db2a, wg2a, wu2a), (da2b, db2b, wg2b, wu2b)],
                                    "ffn2_dx", exchange=scatter(pairs[:3]))
    dx1, dproj, dw_out, dg_mix, dconv, dpool_w, dpool_scale, df1, *got_2b = _mix_bwd(
        dx2, x1, proj, pooled, norm_mix, w_in_full, conv_full, pool_w_low, pool_scale, w_out_full, "mix_bwd",
        exchange=scatter(pairs[3:]))
    small_parts = [dg_mix, dg_ffn2, dg_final, dconv, dpool_w, dpool_scale, loss_row]
    small_sheet, spans = _to_sheet(small_parts)
    dw_in, got_small = _wgrad_tn(h2, dproj, W_IN_SHARD, True, "w_in_wgrad",
                                 exchange=_Exchange([small_sheet], [False], relay_at=0.5))
    pairs = _pair_sum([dw_in, by_dev(dw_out)], "pair_sum_mix")
    da1a, db1a, dwd, dwg, dwu, got_in, got_out = _ffn_bwdw(
        df1, a1a, b1a, h1, wd1a, "ffn1_bwdw_a", exchange=scatter(pairs))
    pairs = _pair_sum([by_dev(dwg), by_dev(dwu), by_dev(dwd)], "pair_sum_ffn1_a")
    da1b, db1b, dwd, dwg, dwu, *got_1a = _ffn_bwdw(df1, a1b, b1b, h1, wd1b, "ffn1_bwdw_b", exchange=scatter(pairs))
    pairs = _pair_sum([by_dev(dwg), by_dev(dwu), by_dev(dwd)], "pair_sum_ffn1_b")
    dx0, dg_ffn1, *got_1b = _ffn_dx(dx1, xs, norm_ffn1, [(da1a, db1a, wg1a, wu1a), (da1b, db1b, wg1b, wu1b)],
                                    "ffn1_dx", exchange=scatter(pairs))
    (got_n1,) = _exchange([dg_ffn1.reshape(8, 128)], [False], "gather_dnorm_ffn1")

    outs = {}

    def update(name, stacks, w, m, v):
        outs[name] = _adamw(stacks, w[0], m[0], v[0], "adamw_" + name)

    def update_ffn(prefix, got_a, got_b, gate, up, down):
        res = _adamw_many(
            [([got_a[j], got_b[j]], *[(t[0].T if j < 2 else t[0]) for t in wmv]) for j, wmv in enumerate((gate, up, down))],
            "adamw_" + prefix)
        outs[prefix + "_w_gate"] = tuple(r.T for r in res[0])
        outs[prefix + "_w_up"] = tuple(r.T for r in res[1])
        outs[prefix + "_w_down"] = res[2]

    update_ffn("ffn1", got_1a, got_1b, (ffn1_w_gate, m_ffn1_w_gate, v_ffn1_w_gate),
               (ffn1_w_up, m_ffn1_w_up, v_ffn1_w_up), (ffn1_w_down, m_ffn1_w_down, v_ffn1_w_down))
    update_ffn("ffn2", got_2a, got_2b, (ffn2_w_gate, m_ffn2_w_gate, v_ffn2_w_gate),
               (ffn2_w_up, m_ffn2_w_up, v_ffn2_w_up), (ffn2_w_down, m_ffn2_w_down, v_ffn2_w_down))
    update("w_in", got_in, w_in, m_w_in, v_w_in)
    update("w_out", got_out, w_out, m_w_out, v_w_out)

    g_small = _from_sheet(_sum_stack(got_small, "sum_small"), spans)
    g_norm_ffn1 = _sum_stack(got_n1, "sum_dnorm_ffn1").reshape(norm_ffn1.shape)
    g_conv = lax.dynamic_slice_in_dim(g_small[3], me * conv_shard, conv_shard, axis=1)
    small_names = ["norm_ffn1", "norm_mix", "norm_ffn2", "norm_final", "conv_w", "pool_w", "pool_scale"]
    small_g = [g_norm_ffn1, g_small[0], g_small[1], g_small[2].reshape(norm_final.shape), g_conv[None],
               g_small[4][None], g_small[5]]
    small_w = [norm_ffn1, norm_mix, norm_ffn2, norm_final, conv_w, pool_w, pool_scale]
    small_m = [m_norm_ffn1, m_norm_mix, m_norm_ffn2, m_norm_final, m_conv_w, m_pool_w, m_pool_scale]
    small_v = [v_norm_ffn1, v_norm_mix, v_norm_ffn2, v_norm_final, v_conv_w, v_pool_w, v_pool_scale]
    g_sheet, spans_u = _to_sheet(small_g)
    w_sheet, _ = _to_sheet(small_w)
    m_sheet, _ = _to_sheet(small_m)
    v_sheet, _ = _to_sheet(small_v)
    upd = _adamw(g_sheet[None], w_sheet, m_sheet, v_sheet, "adamw_small")
    small_out = [_from_sheet(u, spans_u) for u in upd]
    for k, nm in enumerate(small_names):
        outs[nm] = tuple(small_out[j][k] for j in range(4))

    loss = g_small[6][0, 0]
    order = ["norm_ffn1", "ffn1_w_gate", "ffn1_w_up", "ffn1_w_down", "norm_mix", "w_in", "conv_w", "pool_w",
             "pool_scale", "w_out", "norm_ffn2", "ffn2_w_gate", "ffn2_w_up", "ffn2_w_down", "norm_final"]
    big = {"ffn1_w_gate", "ffn1_w_up", "ffn1_w_down", "w_in", "w_out", "ffn2_w_gate", "ffn2_w_up", "ffn2_w_down"}

    def leaf(nm, j):
        val = outs[nm][j]
        return val[None] if nm in big else val

    return (loss, dx0[None],
            *[leaf(nm, 0) for nm in order], *[leaf(nm, 1) for nm in order],
            *[leaf(nm, 2) for nm in order], *[leaf(nm, 3) for nm in order])
```

```python
import jax
import jax.numpy as jnp
from jax import lax
from jax.experimental import pallas as pl
from jax.experimental.pallas import tpu as pltpu

F32 = jnp.float32
LOW = jnp.bfloat16

N_DEV = 8
EPS = 1e-6
D_CONV = 512
POOL_WINDOWS = (2, 4, 8, 16)
POOL_GC = 128
HALO = 16
W_IN_SHARD = 256

ADAM_LR = 0.001
ADAM_B1 = 0.9
ADAM_B2 = 0.999
ADAM_EPS = 1e-08
ADAM_WD = 0.01
ADAM_STEP = 10

VMEM_LIMIT_BYTES = 56 * 1024 * 1024
TOK_TILE = 512
MIX_TOK_TILE = 512
WGRAD_TOK_TILE = 4096
WGRAD_ROW_CANDIDATES = (256, 128)
RELAY_LAST_LATER = 0.25
BWD_ROW_SLAB = 2048


def _params(*sem):
    return pltpu.CompilerParams(dimension_semantics=sem, vmem_limit_bytes=VMEM_LIMIT_BYTES)


def _resident(arr):
    return pl.BlockSpec(arr.shape, lambda *_: (0,) * arr.ndim, pipeline_mode=pl.Buffered(1))


def _pick(n, candidates):
    for c in candidates:
        if n % c == 0:
            return c
    raise ValueError(f"no tile in {candidates} divides {n}")


def _dot(a, b):
    return lax.dot_general(a, b, (((1,), (0,)), ((), ())), preferred_element_type=F32)


def _dot_nt(a, b):
    return lax.dot_general(a, b, (((1,), (1,)), ((), ())), preferred_element_type=F32)


def _dot_tn(a, b):
    return lax.dot_general(a, b, (((0,), (0,)), ((), ())), preferred_element_type=F32)


def _rms_scale(x):
    return lax.rsqrt(jnp.mean(x * x, axis=-1, keepdims=True) + EPS)


def _rms_bwd(dy, x, g):
    r = _rms_scale(x)
    xhat = x * r
    gdy = dy * g
    dx = r * (gdy - xhat * jnp.mean(gdy * xhat, axis=-1, keepdims=True))
    return dx, jnp.sum(dy * xhat, axis=0, keepdims=True)


COLLECTIVE_PAIR, COLLECTIVE_CHIPS, COLLECTIVE_RELAY = 0, 1, 2


def _handshake(peer_numbers):
    mx, my, mc = lax.axis_index("x"), lax.axis_index("y"), lax.axis_index("c")
    barrier = pltpu.get_barrier_semaphore()
    for m in peer_numbers:
        peer = (lax.rem(mx + ((m >> 2) & 1), 2), lax.rem(my + ((m >> 1) & 1), 2), lax.rem(mc + (m & 1), 2))
        pl.semaphore_signal(barrier, inc=1, device_id=peer, device_id_type=pl.DeviceIdType.MESH)
    pl.semaphore_wait(barrier, len(peer_numbers))


class _Exchange:
    CHIPS = (2, 4, 6)

    def __init__(self, arrays, sliced, relay_at=None, among_chips=False):
        assert relay_at is None or not any(sliced)
        assert not among_chips or (all(sliced) and relay_at is None)
        self.relay_at, self.among_chips = relay_at, among_chips
        self.peers = self.CHIPS if among_chips else ((1, 2, 4) if relay_at is not None else tuple(range(1, N_DEV)))
        self.collective_id = COLLECTIVE_CHIPS if among_chips else (COLLECTIVE_RELAY if relay_at is not None else None)
        self.rows = [arr[1:] if isinstance(arr, tuple) else None for arr in arrays]
        self.arrays = [arr[0] if isinstance(arr, tuple) else arr for arr in arrays]
        self.sliced, self.n = list(sliced), len(arrays)
        assert all(rg is None or not sl for rg, sl in zip(self.rows, sliced))
        self.block_shape = [arr.shape if rg is None else (rg[1],) + arr.shape[1:]
                            for arr, rg in zip(self.arrays, self.rows)]
        self.out_shape = [jax.ShapeDtypeStruct(shape if sl else (N_DEV,) + shape, arr.dtype)
                          for arr, shape, sl in zip(self.arrays, self.block_shape, sliced)]
        self.specs = [pl.BlockSpec(memory_space=pl.ANY)] * self.n
        self.scratch_shapes = [pltpu.SemaphoreType.DMA((self.n, N_DEV)),
                               pltpu.SemaphoreType.DMA((self.n, N_DEV)),
                               pltpu.SemaphoreType.DMA((self.n,))]

    HALF_VIA = ((4, 2, 5), (2, 4, 7))

    def _halves(self, a):
        rows = self.block_shape[a][0]
        if rows % 32:
            return ((0, rows), None)
        return ((0, rows // 2), (rows // 2, rows // 2))

    def _copies(self, ins, outs, sems):
        send_sems, recv_sems, local_sems = sems
        sliced = self.sliced
        mx, my, mc = lax.axis_index("x"), lax.axis_index("y"), lax.axis_index("c")
        me = 2 * mx + my if self.among_chips else 4 * mx + 2 * my + mc

        def peer(m):
            px = lax.rem(mx + ((m >> 2) & 1), 2)
            py = lax.rem(my + ((m >> 1) & 1), 2)
            pc = lax.rem(mc + (m & 1), 2)
            return (px, py, pc), (2 * px + py if self.among_chips else 4 * px + 2 * py + pc)

        def mine(a):
            return ins[a] if self.rows[a] is None else ins[a].at[pl.ds(*self.rows[a])]

        def remote(a, m, arriving):
            pid, pflat = peer(m)
            return pltpu.make_async_remote_copy(
                src_ref=ins[a].at[pflat] if sliced[a] else mine(a),
                dst_ref=outs[a].at[pflat if arriving else me],
                send_sem=send_sems.at[a, m - 1],
                recv_sem=recv_sems.at[a, m - 1],
                device_id=pid,
                device_id_type=pl.DeviceIdType.MESH,
            )

        def local(a):
            return pltpu.make_async_copy(ins[a].at[me] if sliced[a] else mine(a), outs[a].at[me], local_sems.at[a])

        def passed_on(a, m):
            _, origin = peer(m)
            sibling, _ = peer(1)
            return pltpu.make_async_remote_copy(
                src_ref=outs[a].at[origin],
                dst_ref=outs[a].at[origin],
                send_sem=send_sems.at[a, m],
                recv_sem=recv_sems.at[a, m],
                device_id=sibling,
                device_id_type=pl.DeviceIdType.MESH,
            )

        def half_on(a, h, arriving):
            via, to, column = self.HALF_VIA[h]
            r0, nr = self._halves(a)[h]
            _, origin = peer(6 if arriving else via)
            rows = outs[a].at[origin].at[pl.ds(r0, nr)]
            return pltpu.make_async_remote_copy(
                src_ref=rows, dst_ref=rows, send_sem=send_sems.at[a, column], recv_sem=recv_sems.at[a, column],
                device_id=peer(to)[0], device_id_type=pl.DeviceIdType.MESH)

        return remote, local, passed_on, half_on

    def start(self, ins, outs, sems):
        remote, local, _, _ = self._copies(ins, outs, sems)
        if self.collective_id is not None:
            _handshake(self.peers)
        for a in range(self.n):
            local(a).start()
        for m in self.peers:
            for a in range(self.n):
                remote(a, m, False).start()

    def relay(self, ins, outs, sems):
        remote, _, passed_on, half_on = self._copies(ins, outs, sems)
        for h, (via, _, _) in enumerate(self.HALF_VIA):
            for a in range(self.n):
                remote(a, via, True).wait_recv()
                passed_on(a, via).start()
                if self._halves(a)[h] is not None:
                    half_on(a, h, False).start()

    def relay_last(self, ins, outs, sems):
        _, _, passed_on, half_on = self._copies(ins, outs, sems)
        for a in range(self.n):
            for h in range(2):
                if self._halves(a)[h] is not None:
                    half_on(a, h, True).wait_recv()
            passed_on(a, 6).start()

    def wait(self, ins, outs, sems):
        remote, local, passed_on, half_on = self._copies(ins, outs, sems)
        if self.relay_at is None:
            for m in self.peers:
                for a in range(self.n):
                    remote(a, m, True).wait_recv()
            for m in self.peers:
                for a in range(self.n):
                    remote(a, m, False).wait_send()
        else:
            for m in (1, 3, 5, 7):
                for a in range(self.n):
                    remote(a, m, True).wait_recv()
            for m in self.peers:
                for a in range(self.n):
                    remote(a, m, False).wait_send()
            for a in range(self.n):
                for m in self.CHIPS:
                    passed_on(a, m).wait_send()
                for h in range(2):
                    if self._halves(a)[h] is not None:
                        half_on(a, h, False).wait_send()
        for a in range(self.n):
            local(a).wait()


def _pair_sum(stacks, name):
    n = len(stacks)
    n_chip = N_DEV // 2
    half = [(n_chip,) + st.shape[1:] for st in stacks]

    def body(*refs):
        ins, outs, mine, theirs = refs[:n], refs[n:2 * n], refs[2 * n:3 * n], refs[3 * n:4 * n]
        local_sems, send_sems, recv_sems = refs[4 * n:]
        mx, my, mc = lax.axis_index("x"), lax.axis_index("y"), lax.axis_index("c")

        def own(a, k):
            return pltpu.make_async_copy(ins[a].at[2 * k + mc], mine[a].at[k], local_sems.at[a, k])

        def swap(a, k):
            return pltpu.make_async_remote_copy(
                src_ref=ins[a].at[2 * k + (1 - mc)], dst_ref=theirs[a].at[k],
                send_sem=send_sems.at[a, k], recv_sem=recv_sems.at[a, k],
                device_id=(mx, my, 1 - mc), device_id_type=pl.DeviceIdType.MESH)

        _handshake((1,))
        for k in range(n_chip):
            for a in range(n):
                own(a, k).start()
                swap(a, k).start()
        for k in range(n_chip):
            for a in range(n):
                own(a, k).wait()
                swap(a, k).wait()
                outs[a][k] = (mine[a][k].astype(F32) + theirs[a][k].astype(F32)).astype(LOW)

    return pl.pallas_call(
        body, name=name,
        out_shape=[jax.ShapeDtypeStruct(h, LOW) for h in half],
        in_specs=[pl.BlockSpec(memory_space=pl.ANY)] * n,
        out_specs=[pl.BlockSpec(memory_space=pltpu.VMEM)] * n,
        scratch_shapes=([pltpu.VMEM(h, st.dtype) for h, st in zip(half, stacks)] * 2
                        + [pltpu.SemaphoreType.DMA((n, n_chip))] * 3),
        compiler_params=pltpu.CompilerParams(vmem_limit_bytes=VMEM_LIMIT_BYTES, collective_id=COLLECTIVE_PAIR),
    )(*stacks)


def _exchange(arrays, sliced, name, relay=False):
    ex = _Exchange(arrays, sliced, relay_at=0 if relay else None)

    def body(*refs):
        ins, outs, sems = refs[:ex.n], refs[ex.n:2 * ex.n], refs[2 * ex.n:]
        ex.start(ins, outs, sems)
        if relay:
            ex.relay(ins, outs, sems)
            ex.relay_last(ins, outs, sems)
        ex.wait(ins, outs, sems)

    return pl.pallas_call(body, name=name, out_shape=ex.out_shape, in_specs=ex.specs, out_specs=ex.specs,
                          scratch_shapes=ex.scratch_shapes,
                          compiler_params=pltpu.CompilerParams(collective_id=ex.collective_id))(*ex.arrays)


def _call(body, *, name, grid, in_specs, out_specs, out_shape, args, scratch_shapes=(), exchange=None):
    params = _params(*(("arbitrary",) * len(grid)))
    if exchange is None:
        return pl.pallas_call(body, name=name, grid=grid, in_specs=in_specs, out_specs=out_specs, out_shape=out_shape,
                              scratch_shapes=list(scratch_shapes), compiler_params=params)(*args)
    exs = list(exchange) if isinstance(exchange, (list, tuple)) else [exchange]
    assert len(exs) == 1 or all(ex.collective_id is None for ex in exs)
    params = pltpu.CompilerParams(dimension_semantics=("arbitrary",) * len(grid), vmem_limit_bytes=VMEM_LIMIT_BYTES,
                                  collective_id=exs[0].collective_id)
    n_in, n_out, n_scr = len(in_specs), len(out_specs), len(scratch_shapes)
    n_ex = sum(ex.n for ex in exs)
    n_steps = 1
    for g in grid:
        n_steps *= g

    def hosted(*refs):
        ins, refs = refs[:n_in], refs[n_in:]
        ex_ins, refs = refs[:n_ex], refs[n_ex:]
        outs, refs = refs[:n_out], refs[n_out:]
        ex_outs, refs = refs[:n_ex], refs[n_ex:]
        scr, sems = refs[:n_scr], refs[n_scr:]
        parts, at = [], 0
        for j, ex in enumerate(exs):
            parts.append((ex_ins[at:at + ex.n], ex_outs[at:at + ex.n], sems[3 * j:3 * j + 3]))
            at += ex.n
        step = pl.program_id(0)
        for ax in range(1, len(grid)):
            step = step * grid[ax] + pl.program_id(ax)

        @pl.when(step == 0)
        def _():
            for ex, part in zip(exs, parts):
                ex.start(*part)

        body(*ins, *outs, *scr)

        for ex, part in zip(exs, parts):
            if ex.relay_at is not None:
                @pl.when(step == min(int(ex.relay_at * n_steps), n_steps - 1))
                def _():
                    ex.relay(*part)

                @pl.when(step == min(int((ex.relay_at + RELAY_LAST_LATER) * n_steps), n_steps - 1))
                def _():
                    ex.relay_last(*part)

        @pl.when(step == n_steps - 1)
        def _():
            for ex, part in zip(exs, parts):
                ex.wait(*part)

    return pl.pallas_call(
        hosted, name=name, grid=grid,
        in_specs=list(in_specs) + [sp for ex in exs for sp in ex.specs],
        out_specs=list(out_specs) + [sp for ex in exs for sp in ex.specs],
        out_shape=list(out_shape) + [sh for ex in exs for sh in ex.out_shape],
        scratch_shapes=list(scratch_shapes) + [sc for ex in exs for sc in ex.scratch_shapes],
        compiler_params=params)(*args, *[arr for ex in exs for arr in ex.arrays])


def _ffn_fwd(x, g, h, wgt, wut, wd, name, exchange=None):
    s_len, d = x.shape
    fc = wd.shape[0]
    ts = min(TOK_TILE, s_len)
    first = h is None

    def body(*refs):
        x_ref, gh_ref, wg_ref, wu_ref, wd_ref, xo_ref = refs[:6]
        a_ref, b_ref, s_ref = refs[-3:]
        xv = x_ref[...]
        if first:
            hb = (xv * _rms_scale(xv) * gh_ref[...]).astype(LOW)
            refs[6][...] = hb
        else:
            hb = gh_ref[...]
        for c0, c1 in _slabs(fc):
            a = _dot_nt(hb, wg_ref[c0:c1, :])
            b = _dot_nt(hb, wu_ref[c0:c1, :])
            s_ref[:, c0:c1] = (a * jax.nn.sigmoid(a) * b).astype(LOW)
            a_ref[:, c0:c1] = a.astype(LOW)
            b_ref[:, c0:c1] = b.astype(LOW)
        xo_ref[...] = xv + 0.5 * _dot(s_ref[...], wd_ref[...])

    tok = pl.BlockSpec((ts, d), lambda t: (t, 0))
    hid = pl.BlockSpec((ts, fc), lambda t: (t, 0))
    tok_out = jax.ShapeDtypeStruct((s_len, d), F32)
    h_out = jax.ShapeDtypeStruct((s_len, d), LOW)
    hid_out = jax.ShapeDtypeStruct((s_len, fc), LOW)
    return _call(
        body,
        name=name,
        grid=(s_len // ts,),
        in_specs=[tok, pl.BlockSpec((1, d), lambda t: (0, 0)) if first else tok,
                  _resident(wgt), _resident(wut), _resident(wd)],
        out_specs=[tok] + ([tok] if first else []) + [hid, hid],
        out_shape=[tok_out] + ([h_out] if first else []) + [hid_out, hid_out],
        scratch_shapes=[pltpu.VMEM((ts, fc), LOW)],
        args=(x, g if first else h, wgt, wut, wd),
        exchange=exchange,
    )


def _slabs(width, slab=256):
    return [(c0, min(c0 + slab, width)) for c0 in range(0, width, slab)]


def _ffn_bwdw(df, a, b, h, wd, name, exchange=None):
    s_len, d = df.shape
    f_len = wd.shape[0]
    tm = _pick(f_len, WGRAD_ROW_CANDIDATES)
    tk = min(WGRAD_TOK_TILE, s_len)
    n_k = s_len // tk

    def body(df_ref, a_ref, b_ref, h_ref, wd_ref, da_ref, db_ref, dwd_ref, dwg_ref, dwu_ref,
             s_ref, acc_d, acc_g, acc_u):
        k = pl.program_id(1)

        @pl.when(k == 0)
        def _():
            acc_d[...] = jnp.zeros_like(acc_d)
            acc_g[...] = jnp.zeros_like(acc_g)
            acc_u[...] = jnp.zeros_like(acc_u)

        wdv = wd_ref[...]
        for r0, r1 in _slabs(tk, BWD_ROW_SLAB):
            ds = _dot_nt(df_ref[r0:r1, :], wdv)
            av = a_ref[r0:r1, :].astype(F32)
            bv = b_ref[r0:r1, :].astype(F32)
            sig = jax.nn.sigmoid(av)
            silu = av * sig
            s_ref[r0:r1, :] = (silu * bv).astype(LOW)
            da_ref[r0:r1, :] = (ds * bv * (sig * (1.0 + av * (1.0 - sig)))).astype(LOW)
            db_ref[r0:r1, :] = (ds * silu).astype(LOW)
            hv = h_ref[r0:r1, :]
            acc_d[...] += _dot_tn(s_ref[r0:r1, :], df_ref[r0:r1, :])
            acc_g[...] += _dot_tn(da_ref[r0:r1, :], hv)
            acc_u[...] += _dot_tn(db_ref[r0:r1, :], hv)

        @pl.when(k == n_k - 1)
        def _():
            dwd_ref[...] = acc_d[...].astype(LOW)
            dwg_ref[...] = acc_g[...].astype(LOW)
            dwu_ref[...] = acc_u[...].astype(LOW)

    hid = pl.BlockSpec((tk, tm), lambda i, k: (k, i))
    tok = pl.BlockSpec((tk, d), lambda i, k: (k, 0))
    wrow = pl.BlockSpec((tm, d), lambda i, k: (i, 0))
    return _call(
        body,
        name=name,
        grid=(f_len // tm, n_k),
        in_specs=[tok, hid, hid, tok, wrow],
        out_specs=[hid, hid, wrow, wrow, wrow],
        out_shape=[jax.ShapeDtypeStruct((s_len, f_len), LOW)] * 2 + [jax.ShapeDtypeStruct((f_len, d), LOW)] * 3,
        scratch_shapes=[pltpu.VMEM((tk, tm), LOW)] + [pltpu.VMEM((tm, d), F32)] * 3,
        args=(df, a, b, h, wd),
        exchange=exchange,
    )


def _ffn_dx(dxo, x, g, parts, name, exchange=None):
    s_len, d = x.shape
    ts = min(TOK_TILE, s_len)
    n_p = len(parts)

    def body(dxo_ref, x_ref, g_ref, *refs):
        dxi_ref, dg_ref = refs[4 * n_p:]

        @pl.when(pl.program_id(0) == 0)
        def _():
            dg_ref[...] = jnp.zeros_like(dg_ref)

        dh = None
        for p in range(n_p):
            da_ref, db_ref, wg_ref, wu_ref = refs[4 * p:4 * p + 4]
            part = _dot(da_ref[...], wg_ref[...]) + _dot(db_ref[...], wu_ref[...])
            dh = part if dh is None else dh + part
        dx, dgp = _rms_bwd(dh, x_ref[...], g_ref[...])
        dxi_ref[...] = dxo_ref[...] + dx
        dg_ref[...] += dgp

    tok = pl.BlockSpec((ts, d), lambda t: (t, 0))
    vec = pl.BlockSpec((1, d), lambda t: (0, 0))
    part_specs, part_args = [], []
    for da, db, wgt, wut in parts:
        hid = pl.BlockSpec((ts, da.shape[1]), lambda t: (t, 0))
        part_specs += [hid, hid, _resident(wgt), _resident(wut)]
        part_args += [da, db, wgt, wut]
    return _call(
        body,
        name=name,
        grid=(s_len // ts,),
        in_specs=[tok, tok, vec] + part_specs,
        out_specs=[tok, vec],
        out_shape=[jax.ShapeDtypeStruct((s_len, d), F32), jax.ShapeDtypeStruct((1, d), F32)],
        args=(dxo, x, g, *part_args),
        exchange=exchange,
    )


def _wgrad_tn(xm, ym, tn, stacked, name, exchange=None):
    s_len, m = xm.shape
    n = ym.shape[1]
    tk = min(WGRAD_TOK_TILE, s_len)
    n_k = s_len // tk

    def body(x_ref, y_ref, o_ref, acc):
        k = pl.program_id(1)

        @pl.when(k == 0)
        def _():
            acc[...] = jnp.zeros_like(acc)

        acc[...] += _dot_tn(x_ref[...].astype(LOW), y_ref[...].astype(LOW))

        @pl.when(k == n_k - 1)
        def _():
            o_ref[...] = acc[...].astype(LOW)

    if stacked:
        out_spec = pl.BlockSpec((None, m, tn), lambda j, k: (j, 0, 0))
        out_shape = jax.ShapeDtypeStruct((n // tn, m, tn), LOW)
    else:
        out_spec = pl.BlockSpec((m, tn), lambda j, k: (0, j))
        out_shape = jax.ShapeDtypeStruct((m, n), LOW)
    return _call(
        body,
        name=name,
        grid=(n // tn, n_k),
        in_specs=[pl.BlockSpec((tk, m), lambda j, k: (k, 0)), pl.BlockSpec((tk, tn), lambda j, k: (k, j))],
        out_specs=[out_spec],
        out_shape=[out_shape],
        scratch_shapes=[pltpu.VMEM((m, tn), F32)],
        args=(xm, ym),
        exchange=exchange,
    )


def _mix_parts(ext_ref, z_ref, cw, ts, row0, with_pooled=True):
    dc = D_CONV

    def back(off, c0, c1):
        return ext_ref[HALO - off:HALO - off + ts, c0:c1]

    v, gate_b, gate_c = back(0, 0, dc), back(0, dc, 2 * dc), back(0, 2 * dc, 3 * dc)
    z_ref[...] = ext_ref[:, 2 * dc:3 * dc] * ext_ref[:, 0:dc]
    z0, z1, z2 = (z_ref[HALO - off:HALO - off + ts, :] for off in range(3))
    conv = cw[2:3, :] * z0 + cw[1:2, :] * z1 + cw[0:1, :] * z2
    rows = row0 + lax.broadcasted_iota(jnp.int32, (ts, 1), 0)
    pooled, inv_count = [], []
    for grp, w in enumerate(POOL_WINDOWS):
        inv = 1.0 / jnp.minimum(rows + 1, w).astype(F32)
        inv_count.append(inv)
        if with_pooled:
            c0 = 3 * dc + POOL_GC * grp
            u = back(0, c0, c0 + POOL_GC)
            acc = u
            for j in range(1, w):
                acc = acc + back(j, c0, c0 + POOL_GC)
            pooled.append(acc * inv - u)
    return v, gate_b, gate_c, z0, z1, z2, conv, pooled, inv_count


def _mix_fwd(x, g, w_in, conv_w, pool_w, pool_scale, w_out, name, exchange=None):
    s_len, d = x.shape
    n_blk, _, wcols = w_in.shape
    p_len = n_blk * wcols
    d_mix = w_out.shape[0]
    ts = min(MIX_TOK_TILE, s_len)
    dc = D_CONV

    def body(x_ref, g_ref, win_ref, cw_ref, pw_ref, ps_ref, wout_ref, x2_ref, h_ref, proj_ref, pooled_ref,
             ext_ref, cat_ref, z_ref):
        t = pl.program_id(0)

        @pl.when(t == 0)
        def _():
            ext_ref[0:HALO, :] = jnp.zeros((HALO, p_len), F32)

        xv = x_ref[...]
        hb = (xv * _rms_scale(xv) * g_ref[...]).astype(LOW)
        h_ref[...] = hb
        for k in range(n_blk):
            ext_ref[HALO:HALO + ts, wcols * k:wcols * (k + 1)] = _dot(hb, win_ref[k])
        proj_ref[...] = ext_ref[HALO:HALO + ts, :]

        _, gate_b, _, _, _, _, conv, pooled, _ = _mix_parts(ext_ref, z_ref, cw_ref[...], ts, t * ts)
        cat_ref[:, 0:dc] = (gate_b * conv).astype(LOW)
        for grp in range(len(POOL_WINDOWS)):
            c0 = POOL_GC * grp
            pooled_b = pooled[grp].astype(LOW)
            pooled_ref[:, c0:c0 + POOL_GC] = pooled_b
            lin = _dot(pooled_b, pw_ref[grp])
            cat_ref[:, dc + c0:dc + c0 + POOL_GC] = (lin * ps_ref[:, c0:c0 + POOL_GC]).astype(LOW)
        x2_ref[...] = xv + _dot(cat_ref[...], wout_ref[...])
        ext_ref[0:HALO, :] = ext_ref[ts:ts + HALO, :]

    tok = pl.BlockSpec((ts, d), lambda t: (t, 0))

    def whole(arr):
        return pl.BlockSpec(arr.shape, lambda t: (0,) * arr.ndim)

    return _call(
        body,
        name=name,
        grid=(s_len // ts,),
        in_specs=[tok, whole(g), _resident(w_in), whole(conv_w), whole(pool_w), whole(pool_scale), _resident(w_out)],
        out_specs=[tok, tok, pl.BlockSpec((ts, p_len), lambda t: (t, 0)),
                   pl.BlockSpec((ts, d_mix - dc), lambda t: (t, 0))],
        out_shape=[
            jax.ShapeDtypeStruct((s_len, d), F32),
            jax.ShapeDtypeStruct((s_len, d), LOW),
            jax.ShapeDtypeStruct((s_len, p_len), F32),
            jax.ShapeDtypeStruct((s_len, d_mix - dc), LOW),
        ],
        scratch_shapes=[pltpu.VMEM((ts + HALO, p_len), F32), pltpu.VMEM((ts, d_mix), LOW),
                        pltpu.VMEM((ts + HALO, dc), F32)],
        args=(x, g, w_in, conv_w, pool_w, pool_scale, w_out),
        exchange=exchange,
    )


def _mix_bwd(dx2, x, proj, pooled, g, w_in, conv_w, pool_w, pool_scale, w_out, name, exchange=None):
    s_len, d = x.shape
    n_blk, _, wcols = w_in.shape
    p_len = n_blk * wcols
    d_mix = w_out.shape[0]
    ts = min(MIX_TOK_TILE, s_len)
    n_t = s_len // ts
    dc = D_CONV
    n_grp = len(POOL_WINDOWS)

    def body(dx2_ref, x_ref, proj_ref, halo_ref, pooled_ref, g_ref, win_ref, cw_ref, pw_ref, ps_ref, wout_ref,
             dx_ref, dproj_ref, dwout_ref, dg_ref, dcw_ref, dpw_ref, dps_ref, df_ref,
             ext_ref, fut_ref, cat_ref, dwout_acc, win_a, win_b, z_ref):
        i = pl.program_id(0)
        t = n_t - 1 - i

        @pl.when(i == 0)
        def _():
            win_a[...] = jnp.zeros_like(win_a)
            win_b[...] = jnp.zeros_like(win_b)
            dwout_acc[...] = jnp.zeros_like(dwout_acc)
            dg_ref[...] = jnp.zeros_like(dg_ref)
            dcw_ref[...] = jnp.zeros_like(dcw_ref)
            dpw_ref[...] = jnp.zeros_like(dpw_ref)
            dps_ref[...] = jnp.zeros_like(dps_ref)
            fut_ref[ts:ts + 2 * HALO, :] = jnp.zeros((2 * HALO, d_mix), F32)

        ext_ref[HALO:HALO + ts, :] = proj_ref[...]

        @pl.when(t == 0)
        def _():
            ext_ref[0:HALO, :] = jnp.zeros((HALO, p_len), F32)

        @pl.when(t > 0)
        def _():
            ext_ref[0:HALO, :] = halo_ref[...]

        cw = cw_ref[...]
        v, gate_b, gate_c, z0, z1, z2, conv, _, inv_count = _mix_parts(ext_ref, z_ref, cw, ts, t * ts,
                                                                        with_pooled=False)
        dx2 = dx2_ref[...]
        dcat = _dot_nt(dx2.astype(LOW), wout_ref[...])

        dy_a = dcat[:, 0:dc]
        dconv = dy_a * gate_b
        fut_ref[0:ts, 0:dc] = dconv
        cat_ref[:, 0:dc] = (gate_b * conv).astype(LOW)
        dproj_ref[:, dc:2 * dc] = (dy_a * conv).astype(LOW)
        dcw_ref[2:3, :] += jnp.sum(dconv * z0, axis=0, keepdims=True)
        dcw_ref[1:2, :] += jnp.sum(dconv * z1, axis=0, keepdims=True)
        dcw_ref[0:1, :] += jnp.sum(dconv * z2, axis=0, keepdims=True)

        dpool = []
        for grp in range(n_grp):
            c0 = POOL_GC * grp
            pooled_b = pooled_ref[:, c0:c0 + POOL_GC]
            lin = _dot(pooled_b, pw_ref[grp])
            dy_b = dcat[:, dc + c0:dc + c0 + POOL_GC]
            scale = ps_ref[:, c0:c0 + POOL_GC]
            cat_ref[:, dc + c0:dc + c0 + POOL_GC] = (lin * scale).astype(LOW)
            dps_ref[:, c0:c0 + POOL_GC] += jnp.sum(dy_b * lin, axis=0, keepdims=True)
            dlin = (dy_b * scale).astype(LOW)
            dpw_ref[grp] += _dot_tn(pooled_b, dlin)
            dpool.append(_dot_nt(dlin, pw_ref[grp]))
            fut_ref[0:ts, dc + c0:dc + c0 + POOL_GC] = dpool[grp] * inv_count[grp]
        dwout_acc[...] += _dot_tn(cat_ref[...], dx2.astype(LOW))

        def ahead(off, c0, c1):
            return fut_ref[off:off + ts, c0:c1]

        dz = cw[2:3, :] * ahead(0, 0, dc) + cw[1:2, :] * ahead(1, 0, dc) + cw[0:1, :] * ahead(2, 0, dc)
        dproj_ref[:, 0:dc] = (dz * gate_c).astype(LOW)
        dproj_ref[:, 2 * dc:3 * dc] = (dz * v).astype(LOW)
        rows_ext = ts + HALO
        for grp, w in enumerate(POOL_WINDOWS):
            c0 = dc + POOL_GC * grp
            src, dst, other = fut_ref, win_a, win_b
            cols = slice(c0, c0 + POOL_GC)
            span = 1
            while span < w:
                dst[0:rows_ext, :] = src[0:rows_ext, cols] + src[span:span + rows_ext, cols]
                src, dst, other, cols = dst, other, dst, slice(0, POOL_GC)
                span *= 2
            dproj_ref[:, 2 * dc + c0:2 * dc + c0 + POOL_GC] = (src[0:ts, cols] - dpool[grp]).astype(LOW)

        dh = _dot_nt(dproj_ref[:, 0:wcols], win_ref[0])
        for k in range(1, n_blk):
            dh += _dot_nt(dproj_ref[:, wcols * k:wcols * (k + 1)], win_ref[k])
        dx, dgp = _rms_bwd(dh, x_ref[...], g_ref[...])
        dx = dx2 + dx
        dx_ref[...] = dx
        df_ref[...] = (0.5 * dx).astype(LOW)
        dg_ref[...] += dgp
        fut_ref[ts:ts + HALO, :] = fut_ref[0:HALO, :]

        @pl.when(i == n_t - 1)
        def _():
            dwout_ref[...] = dwout_acc[...].astype(LOW)

    tok = pl.BlockSpec((ts, d), lambda i: (n_t - 1 - i, 0))
    halo = pl.BlockSpec((HALO, p_len), lambda i: (jnp.maximum((n_t - 1 - i) * (ts // HALO) - 1, 0), 0))

    def whole(arr):
        return pl.BlockSpec(arr.shape, lambda i: (0,) * arr.ndim)

    return _call(
        body,
        name=name,
        grid=(n_t,),
        in_specs=[tok, tok, pl.BlockSpec((ts, p_len), lambda i: (n_t - 1 - i, 0)), halo,
                  pl.BlockSpec((ts, d_mix - dc), lambda i: (n_t - 1 - i, 0)),
                  whole(g), _resident(w_in), whole(conv_w), whole(pool_w), whole(pool_scale), _resident(w_out)],
        out_specs=[tok, pl.BlockSpec((ts, p_len), lambda i: (n_t - 1 - i, 0)),
                   whole(w_out),
                   whole(g), whole(conv_w), whole(pool_w), whole(pool_scale), tok],
        out_shape=[
            jax.ShapeDtypeStruct((s_len, d), F32),
            jax.ShapeDtypeStruct((s_len, p_len), LOW),
            jax.ShapeDtypeStruct((d_mix, d), LOW),
            jax.ShapeDtypeStruct(g.shape, F32),
            jax.ShapeDtypeStruct(conv_w.shape, F32),
            jax.ShapeDtypeStruct(pool_w.shape, F32),
            jax.ShapeDtypeStruct(pool_scale.shape, F32),
            jax.ShapeDtypeStruct((s_len, d), LOW),
        ],
        scratch_shapes=[pltpu.VMEM((ts + HALO, p_len), F32), pltpu.VMEM((ts + 2 * HALO, d_mix), F32),
                        pltpu.VMEM((ts, d_mix), LOW), pltpu.VMEM((d_mix, d), F32),
                        pltpu.VMEM((ts + 2 * HALO, POOL_GC), F32), pltpu.VMEM((ts + 2 * HALO, POOL_GC), F32),
                        pltpu.VMEM((ts + HALO, dc), F32)],
        args=(dx2, x, proj, proj, pooled, g, w_in, conv_w, pool_w, pool_scale, w_out),
        exchange=exchange,
    )


def _ffn_fwd_loss(x, h, wgt, wut, wd, g, target, name):
    s_len, d = x.shape
    fc = wd.shape[0]
    ts = min(TOK_TILE, s_len)

    def body(x_ref, h_ref, wg_ref, wu_ref, wd_ref, g_ref, tgt_ref,
             a_ref, b_ref, loss_ref, dx_ref, dg_ref, df_ref, s_ref):
        @pl.when(pl.program_id(0) == 0)
        def _():
            loss_ref[...] = jnp.zeros_like(loss_ref)
            dg_ref[...] = jnp.zeros_like(dg_ref)

        hb = h_ref[...]
        for c0, c1 in _slabs(fc):
            a = _dot_nt(hb, wg_ref[c0:c1, :])
            b = _dot_nt(hb, wu_ref[c0:c1, :])
            s_ref[:, c0:c1] = (a * jax.nn.sigmoid(a) * b).astype(LOW)
            a_ref[:, c0:c1] = a.astype(LOW)
            b_ref[:, c0:c1] = b.astype(LOW)
        xv = x_ref[...] + 0.5 * _dot(s_ref[...], wd_ref[...])
        gv = g_ref[...]
        err = xv * _rms_scale(xv) * gv - tgt_ref[...]
        loss_ref[...] += 0.5 * jnp.sum(jnp.mean(err * err, axis=-1, keepdims=True), axis=0, keepdims=True)
        dx, dgp = _rms_bwd(err * (1.0 / d), xv, gv)
        dx_ref[...] = dx
        df_ref[...] = (0.5 * dx).astype(LOW)
        dg_ref[...] += dgp

    tok = pl.BlockSpec((ts, d), lambda t: (t, 0))
    vec = pl.BlockSpec((1, d), lambda t: (0, 0))
    hid = pl.BlockSpec((ts, fc), lambda t: (t, 0))
    return pl.pallas_call(
        body,
        name=name,
        grid=(s_len // ts,),
        in_specs=[tok, tok, _resident(wgt), _resident(wut), _resident(wd), vec, tok],
        out_specs=[hid, hid, pl.BlockSpec((1, 128), lambda t: (0, 0)), tok, vec, tok],
        out_shape=[
            jax.ShapeDtypeStruct((s_len, fc), LOW),
            jax.ShapeDtypeStruct((s_len, fc), LOW),
            jax.ShapeDtypeStruct((1, 128), F32),
            jax.ShapeDtypeStruct((s_len, d), F32),
            jax.ShapeDtypeStruct((1, d), F32),
            jax.ShapeDtypeStruct((s_len, d), LOW),
        ],
        scratch_shapes=[pltpu.VMEM((ts, fc), LOW)],
        compiler_params=_params("arbitrary"),
    )(x, h, wgt, wut, wd, g, target)


def _row_tile(rows, cols, stack_bytes):
    budget = 20 * 1024 * 1024
    per_row = cols * (4 * 7 + stack_bytes)
    for tr in (rows, 512, 256, 176, 128, 64, 32, 16, 8):
        if rows % tr == 0 and tr % 8 == 0 and tr * per_row * 2 <= budget:
            return tr
    return rows


def _sum_stack(stack, name):
    n, r, c = stack.shape
    tr = _row_tile(r, c, n * stack.dtype.itemsize)

    def body(s_ref, o_ref):
        acc = s_ref[0].astype(F32)
        for k in range(1, n):
            acc = acc + s_ref[k].astype(F32)
        o_ref[...] = acc

    return pl.pallas_call(
        body,
        name=name,
        grid=(r // tr,),
        in_specs=[pl.BlockSpec((n, tr, c), lambda i: (0, i, 0))],
        out_specs=pl.BlockSpec((tr, c), lambda i: (i, 0)),
        out_shape=jax.ShapeDtypeStruct((r, c), F32),
        compiler_params=_params("arbitrary"),
    )(stack)


def _adamw_many(params, name):
    params = [(list(st) if isinstance(st, (list, tuple)) else [st], w, m, v) for st, w, m, v in params]
    stacks0, w0 = params[0][0], params[0][1]
    r, c = w0.shape
    n = stacks0[0].shape[0]
    n_st = len(stacks0)
    part_rows = [st.shape[1] for st in stacks0]
    assert sum(part_rows) == r
    assert all(w.shape == (r, c) and [st.shape for st in sts] == [st.shape for st in stacks0] for sts, w, _, _ in params)
    first_row = [sum(part_rows[:j]) for j in range(n_st)]
    tc = next(t for t in (512, 256, 128) if c % t == 0)
    c1 = 1.0 - ADAM_B1 ** ADAM_STEP
    c2 = 1.0 - ADAM_B2 ** ADAM_STEP
    n_in = n_st + 3

    def body(*refs):
        ins, outs = refs[:n_in * len(params)], refs[n_in * len(params):]
        for p in range(len(params)):
            s_refs = ins[n_in * p:n_in * p + n_st]
            w_ref, m_ref, v_ref = ins[n_in * p + n_st:n_in * (p + 1)]
            g_ref, d_ref, mo_ref, vo_ref = outs[4 * p:4 * p + 4]
            for s_ref, r0, nr in zip(s_refs, first_row, part_rows):
                gv = s_ref[0].astype(F32)
                for k in range(1, n):
                    gv = gv + s_ref[k].astype(F32)
                mn = ADAM_B1 * m_ref[r0:r0 + nr, :] + (1.0 - ADAM_B1) * gv
                vn = ADAM_B2 * v_ref[r0:r0 + nr, :] + (1.0 - ADAM_B2) * (gv * gv)
                g_ref[r0:r0 + nr, :] = gv
                mo_ref[r0:r0 + nr, :] = mn
                vo_ref[r0:r0 + nr, :] = vn
                d_ref[r0:r0 + nr, :] = -ADAM_LR * ((mn / c1) / (jnp.sqrt(vn / c2) + ADAM_EPS)
                                                   + ADAM_WD * w_ref[r0:r0 + nr, :])

    blk = pl.BlockSpec((r, tc), lambda i: (0, i))
    one_in = [pl.BlockSpec((n, nr, tc), lambda i: (0, 0, i)) for nr in part_rows] + [blk, blk, blk]
    res = pl.pallas_call(
        body,
        name=name,
        grid=(c // tc,),
        in_specs=one_in * len(params),
        out_specs=[blk] * (4 * len(params)),
        out_shape=[jax.ShapeDtypeStruct((r, c), F32)] * (4 * len(params)),
        compiler_params=_params("arbitrary"),
    )(*[arr for sts, w, m, v in params for arr in (*sts, w, m, v)])
    return [tuple(res[4 * p:4 * p + 4]) for p in range(len(params))]


def _adamw(stacks, w, m, v, name):
    return _adamw_many([(stacks, w, m, v)], name)[0]


def _to_sheet(parts):
    sheets, spans = [], []
    row = 0
    for p in parts:
        flat = p.reshape(-1).astype(F32)
        rows = -(-flat.shape[0] // 1024) * 8
        flat = jnp.pad(flat, (0, rows * 128 - flat.shape[0]))
        sheets.append(flat.reshape(rows, 128))
        spans.append((row, p.size, p.shape))
        row += rows
    return jnp.concatenate(sheets, axis=0), spans


def _from_sheet(sheet, spans):
    out = []
    for row, size, shape in spans:
        rows = -(-size // 1024) * 8
        out.append(sheet[row:row + rows].reshape(-1)[:size].reshape(shape))
    return out


def kernel(x, norm_ffn1, ffn1_w_gate, ffn1_w_up, ffn1_w_down, norm_mix, w_in, conv_w, pool_w, pool_scale, w_out, norm_ffn2, ffn2_w_gate, ffn2_w_up, ffn2_w_down, norm_final, loss_target, m_norm_ffn1, m_ffn1_w_gate, m_ffn1_w_up, m_ffn1_w_down, m_norm_mix, m_w_in, m_conv_w, m_pool_w, m_pool_scale, m_w_out, m_norm_ffn2, m_ffn2_w_gate, m_ffn2_w_up, m_ffn2_w_down, m_norm_final, v_norm_ffn1, v_ffn1_w_gate, v_ffn1_w_up, v_ffn1_w_down, v_norm_mix, v_w_in, v_conv_w, v_pool_w, v_pool_scale, v_w_out, v_norm_ffn2, v_ffn2_w_gate, v_ffn2_w_up, v_ffn2_w_down, v_norm_final):
    me = 4 * lax.axis_index("x") + 2 * lax.axis_index("y") + lax.axis_index("c")
    xs, tgt = x[0], loss_target[0]
    s_len, d = xs.shape
    f_shard = ffn1_w_down.shape[1]
    conv_shard = conv_w.shape[2]

    def low_t(wt):
        return wt[0].T.astype(LOW)

    def by_dev(gw):
        return gw.reshape(N_DEV, -1, d)

    conv_tile = jnp.zeros((8, 128), F32).at[0:conv_w.shape[1], 0:conv_shard].set(conv_w[0])
    pool_w_low = pool_w[0].astype(LOW)

    rows_a = -(-f_shard // 64) * 32

    def parts_of(w_gate, w_up, w_down):
        shards = [low_t(w_gate), low_t(w_up), w_down[0].astype(LOW)]
        return [(s, 0, rows_a) for s in shards], [(s, rows_a, f_shard - rows_a) for s in shards]

    def rows_flat(stacks):
        return [st.reshape(-1, d) for st in stacks]

    def gather(shards):
        return _Exchange(shards, [False] * len(shards), relay_at=0.6)

    def scatter(pairs):
        return _Exchange(pairs, [True] * len(pairs), among_chips=True)

    w1a_shards, w1b_shards = parts_of(ffn1_w_gate, ffn1_w_up, ffn1_w_down)
    w2a_shards, w2b_shards = parts_of(ffn2_w_gate, ffn2_w_up, ffn2_w_down)

    wg1a, wu1a, wd1a = rows_flat(_exchange(w1a_shards, [False] * 3, "gather_ffn1_a", relay=True))
    xa, h1, a1a, b1a, *w1b = _ffn_fwd(xs, norm_ffn1, None, wg1a, wu1a, wd1a, "ffn1_fwd_a",
                                      exchange=gather(w1b_shards))
    wg1b, wu1b, wd1b = rows_flat(w1b)
    x1, a1b, b1b, w_in_full, w_out_full, conv_tiles = _ffn_fwd(
        xa, None, h1, wg1b, wu1b, wd1b, "ffn1_fwd_b",
        exchange=gather([w_in[0].astype(LOW), w_out[0].astype(LOW), conv_tile]))
    w_out_full = w_out_full.reshape(-1, d)
    conv_full = jnp.concatenate([conv_tiles[k, 0:conv_w.shape[1], 0:conv_shard] for k in range(N_DEV)], axis=1)
    x2, h2, proj, pooled, *w2a = _mix_fwd(x1, norm_mix, w_in_full, conv_full, pool_w_low, pool_scale, w_out_full,
                                          "mix_fwd", exchange=gather(w2a_shards))
    wg2a, wu2a, wd2a = rows_flat(w2a)
    xb, h3, a2a, b2a, *w2b = _ffn_fwd(x2, norm_ffn2, None, wg2a, wu2a, wd2a, "ffn2_fwd_a",
                                      exchange=gather(w2b_shards))
    wg2b, wu2b, wd2b = rows_flat(w2b)
    a2b, b2b, loss_row, dx3, dg_final, df3 = _ffn_fwd_loss(
        xb, h3, wg2b, wu2b, wd2b, norm_final.reshape(1, d), tgt, "ffn2_fwd_b_loss")

    da2a, db2a, dwd_a, dwg_a, dwu_a = _ffn_bwdw(df3, a2a, b2a, h3, wd2a, "ffn2_bwdw_a")
    da2b, db2b, dwd_b, dwg_b, dwu_b = _ffn_bwdw(df3, a2b, b2b, h3, wd2b, "ffn2_bwdw_b")
    pairs = _pair_sum([by_dev(dwg_a), by_dev(dwu_a), by_dev(dwd_a), by_dev(dwg_b), by_dev(dwu_b), by_dev(dwd_b)],
                      "pair_sum_ffn2")
    dx2, dg_ffn2, *got_2a = _ffn_dx(dx3, x2, norm_ffn2, [(da2a, db2a, wg2a, wu2a), (da2b, db2b, wg2b, wu2b)],
                                    "ffn2_dx", exchange=scatter(pairs[:3]))
    dx1, dproj, dw_out, dg_mix, dconv, dpool_w, dpool_scale, df1, *got_2b = _mix_bwd(
        dx2, x1, proj, pooled, norm_mix, w_in_full, conv_full, pool_w_low, pool_scale, w_out_full, "mix_bwd",
        exchange=scatter(pairs[3:]))
    small_parts = [dg_mix, dg_ffn2, dg_final, dconv, dpool_w, dpool_scale, loss_row]
    small_sheet, spans = _to_sheet(small_parts)
    dw_in, got_small = _wgrad_tn(h2, dproj, W_IN_SHARD, True, "w_in_wgrad",
                                 exchange=_Exchange([small_sheet], [False], relay_at=0.5))
    pairs = _pair_sum([dw_in, by_dev(dw_out)], "pair_sum_mix")
    da1a, db1a, dwd, dwg, dwu, got_in, got_out = _ffn_bwdw(
        df1, a1a, b1a, h1, wd1a, "ffn1_bwdw_a", exchange=scatter(pairs))
    pairs = _pair_sum([by_dev(dwg), by_dev(dwu), by_dev(dwd)], "pair_sum_ffn1_a")
    da1b, db1b, dwd, dwg, dwu, *got_1a = _ffn_bwdw(df1, a1b, b1b, h1, wd1b, "ffn1_bwdw_b", exchange=scatter(pairs))
    pairs = _pair_sum([by_dev(dwg), by_dev(dwu), by_dev(dwd)], "pair_sum_ffn1_b")
    dx0, dg_ffn1, *got_1b = _ffn_dx(dx1, xs, norm_ffn1, [(da1a, db1a, wg1a, wu1a), (da1b, db1b, wg1b, wu1b)],
                                    "ffn1_dx", exchange=scatter(pairs))
    (got_n1,) = _exchange([dg_ffn1.reshape(8, 128)], [False], "gather_dnorm_ffn1")

    outs = {}

    def update(name, stacks, w, m, v):
        outs[name] = _adamw(stacks, w[0], m[0], v[0], "adamw_" + name)

    def update_ffn(prefix, got_a, got_b, gate, up, down):
        res = _adamw_many(
            [([got_a[j], got_b[j]], *[(t[0].T if j < 2 else t[0]) for t in wmv]) for j, wmv in enumerate((gate, up, down))],
            "adamw_" + prefix)
        outs[prefix + "_w_gate"] = tuple(r.T for r in res[0])
        outs[prefix + "_w_up"] = tuple(r.T for r in res[1])
        outs[prefix + "_w_down"] = res[2]

    update_ffn("ffn1", got_1a, got_1b, (ffn1_w_gate, m_ffn1_w_gate, v_ffn1_w_gate),
               (ffn1_w_up, m_ffn1_w_up, v_ffn1_w_up), (ffn1_w_down, m_ffn1_w_down, v_ffn1_w_down))
    update_ffn("ffn2", got_2a, got_2b, (ffn2_w_gate, m_ffn2_w_gate, v_ffn2_w_gate),
               (ffn2_w_up, m_ffn2_w_up, v_ffn2_w_up), (ffn2_w_down, m_ffn2_w_down, v_ffn2_w_down))
    update("w_in", got_in, w_in, m_w_in, v_w_in)
    update("w_out", got_out, w_out, m_w_out, v_w_out)

    g_small = _from_sheet(_sum_stack(got_small, "sum_small"), spans)
    g_norm_ffn1 = _sum_stack(got_n1, "sum_dnorm_ffn1").reshape(norm_ffn1.shape)
    g_conv = lax.dynamic_slice_in_dim(g_small[3], me * conv_shard, conv_shard, axis=1)
    small_names = ["norm_ffn1", "norm_mix", "norm_ffn2", "norm_final", "conv_w", "pool_w", "pool_scale"]
    small_g = [g_norm_ffn1, g_small[0], g_small[1], g_small[2].reshape(norm_final.shape), g_conv[None],
               g_small[4][None], g_small[5]]
    small_w = [norm_ffn1, norm_mix, norm_ffn2, norm_final, conv_w, pool_w, pool_scale]
    small_m = [m_norm_ffn1, m_norm_mix, m_norm_ffn2, m_norm_final, m_conv_w, m_pool_w, m_pool_scale]
    small_v = [v_norm_ffn1, v_norm_mix, v_norm_ffn2, v_norm_final, v_conv_w, v_pool_w, v_pool_scale]
    g_sheet, spans_u = _to_sheet(small_g)
    w_sheet, _ = _to_sheet(small_w)
    m_sheet, _ = _to_sheet(small_m)
    v_sheet, _ = _to_sheet(small_v)
    upd = _adamw(g_sheet[None], w_sheet, m_sheet, v_sheet, "adamw_small")
    small_out = [_from_sheet(u, spans_u) for u in upd]
    for k, nm in enumerate(small_names):
        outs[nm] = tuple(small_out[j][k] for j in range(4))

    loss = g_small[6][0, 0]
    order = ["norm_ffn1", "ffn1_w_gate", "ffn1_w_up", "ffn1_w_down", "norm_mix", "w_in", "conv_w", "pool_w",
             "pool_scale", "w_out", "norm_ffn2", "ffn2_w_gate", "ffn2_w_up", "ffn2_w_down", "norm_final"]
    big = {"ffn1_w_gate", "ffn1_w_up", "ffn1_w_down", "w_in", "w_out", "ffn2_w_gate", "ffn2_w_up", "ffn2_w_down"}

    def leaf(nm, j):
        val = outs[nm][j]
        return val[None] if nm in big else val

    return (loss, dx0[None],
            *[leaf(nm, 0) for nm in order], *[leaf(nm, 1) for nm in order],
            *[leaf(nm, 2) for nm in order], *[leaf(nm, 3) for nm in order])
```

```python
import jax
import jax.numpy as jnp
from jax import lax
from jax.experimental import pallas as pl
from jax.experimental.pallas import tpu as pltpu

F32 = jnp.float32
LOW = jnp.bfloat16

N_DEV = 8
EPS = 1e-6
D_CONV = 512
POOL_WINDOWS = (2, 4, 8, 16)
POOL_GC = 128
HALO = 16
W_IN_SHARD = 256

ADAM_LR = 0.001
ADAM_B1 = 0.9
ADAM_B2 = 0.999
ADAM_EPS = 1e-08
ADAM_WD = 0.01
ADAM_STEP = 10

VMEM_LIMIT_BYTES = 56 * 1024 * 1024
TOK_TILE = 512
MIX_TOK_TILE = 512
WGRAD_TOK_TILE = 4096
WGRAD_ROW_CANDIDATES = (256, 128)
RELAY_LAST_LATER = 0.25
BWD_ROW_SLAB = 2048


def _params(*sem):
    return pltpu.CompilerParams(dimension_semantics=sem, vmem_limit_bytes=VMEM_LIMIT_BYTES)


def _resident(arr):
    return pl.BlockSpec(arr.shape, lambda *_: (0,) * arr.ndim, pipeline_mode=pl.Buffered(1))


def _pick(n, candidates):
    for c in candidates:
        if n % c == 0:
            return c
    raise ValueError(f"no tile in {candidates} divides {n}")


def _dot(a, b):
    return lax.dot_general(a, b, (((1,), (0,)), ((), ())), preferred_element_type=F32)


def _dot_nt(a, b):
    return lax.dot_general(a, b, (((1,), (1,)), ((), ())), preferred_element_type=F32)


def _dot_tn(a, b):
    return lax.dot_general(a, b, (((0,), (0,)), ((), ())), preferred_element_type=F32)


def _rms_scale(x):
    return lax.rsqrt(jnp.mean(x * x, axis=-1, keepdims=True) + EPS)


def _rms_bwd(dy, x, g):
    r = _rms_scale(x)
    xhat = x * r
    gdy = dy * g
    dx = r * (gdy - xhat * jnp.mean(gdy * xhat, axis=-1, keepdims=True))
    return dx, jnp.sum(dy * xhat, axis=0, keepdims=True)


COLLECTIVE_PAIR, COLLECTIVE_CHIPS, COLLECTIVE_RELAY = 0, 1, 2


def _handshake(peer_numbers):
    mx, my, mc = lax.axis_index("x"), lax.axis_index("y"), lax.axis_index("c")
    barrier = pltpu.get_barrier_semaphore()
    for m in peer_numbers:
        peer = (lax.rem(mx + ((m >> 2) & 1), 2), lax.rem(my + ((m >> 1) & 1), 2), lax.rem(mc + (m & 1), 2))
        pl.semaphore_signal(barrier, inc=1, device_id=peer, device_id_type=pl.DeviceIdType.MESH)
    pl.semaphore_wait(barrier, len(peer_numbers))


class _Exchange:
    CHIPS = (2, 4, 6)

    def __init__(self, arrays, sliced, relay_at=None, among_chips=False):
        assert relay_at is None or not any(sliced)
        assert not among_chips or (all(sliced) and relay_at is None)
        self.relay_at, self.among_chips = relay_at, among_chips
        self.peers = self.CHIPS if among_chips else ((1, 2, 4) if relay_at is not None else tuple(range(1, N_DEV)))
        self.collective_id = COLLECTIVE_CHIPS if among_chips else (COLLECTIVE_RELAY if relay_at is not None else None)
        self.rows = [arr[1:] if isinstance(arr, tuple) else None for arr in arrays]
        self.arrays = [arr[0] if isinstance(arr, tuple) else arr for arr in arrays]
        self.sliced, self.n = list(sliced), len(arrays)
        assert all(rg is None or not sl for rg, sl in zip(self.rows, sliced))
        self.block_shape = [arr.shape if rg is None else (rg[1],) + arr.shape[1:]
                            for arr, rg in zip(self.arrays, self.rows)]
        self.out_shape = [jax.ShapeDtypeStruct(shape if sl else (N_DEV,) + shape, arr.dtype)
                          for arr, shape, sl in zip(self.arrays, self.block_shape, sliced)]
        self.specs = [pl.BlockSpec(memory_space=pl.ANY)] * self.n
        self.scratch_shapes = [pltpu.SemaphoreType.DMA((self.n, N_DEV)),
                               pltpu.SemaphoreType.DMA((self.n, N_DEV)),
                               pltpu.SemaphoreType.DMA((self.n,))]

    HALF_VIA = ((4, 2, 5), (2, 4, 7))

    def _halves(self, a):
        rows = self.block_shape[a][0]
        if rows % 32:
            return ((0, rows), None)
        return ((0, rows // 2), (rows // 2, rows // 2))

    def _copies(self, ins, outs, sems):
        send_sems, recv_sems, local_sems = sems
        sliced = self.sliced
        mx, my, mc = lax.axis_index("x"), lax.axis_index("y"), lax.axis_index("c")
        me = 2 * mx + my if self.among_chips else 4 * mx + 2 * my + mc

        def peer(m):
            px = lax.rem(mx + ((m >> 2) & 1), 2)
            py = lax.rem(my + ((m >> 1) & 1), 2)
            pc = lax.rem(mc + (m & 1), 2)
            return (px, py, pc), (2 * px + py if self.among_chips else 4 * px + 2 * py + pc)

        def mine(a):
            return ins[a] if self.rows[a] is None else ins[a].at[pl.ds(*self.rows[a])]

        def remote(a, m, arriving):
            pid, pflat = peer(m)
            return pltpu.make_async_remote_copy(
                src_ref=ins[a].at[pflat] if sliced[a] else mine(a),
                dst_ref=outs[a].at[pflat if arriving else me],
                send_sem=send_sems.at[a, m - 1],
                recv_sem=recv_sems.at[a, m - 1],
                device_id=pid,
                device_id_type=pl.DeviceIdType.MESH,
            )

        def local(a):
            return pltpu.make_async_copy(ins[a].at[me] if sliced[a] else mine(a), outs[a].at[me], local_sems.at[a])

        def passed_on(a, m):
            _, origin = peer(m)
            sibling, _ = peer(1)
            return pltpu.make_async_remote_copy(
                src_ref=outs[a].at[origin],
                dst_ref=outs[a].at[origin],
                send_sem=send_sems.at[a, m],
                recv_sem=recv_sems.at[a, m],
                device_id=sibling,
                device_id_type=pl.DeviceIdType.MESH,
            )

        def half_on(a, h, arriving):
            via, to, column = self.HALF_VIA[h]
            r0, nr = self._halves(a)[h]
            _, origin = peer(6 if arriving else via)
            rows = outs[a].at[origin].at[pl.ds(r0, nr)]
            return pltpu.make_async_remote_copy(
                src_ref=rows, dst_ref=rows, send_sem=send_sems.at[a, column], recv_sem=recv_sems.at[a, column],
                device_id=peer(to)[0], device_id_type=pl.DeviceIdType.MESH)

        return remote, local, passed_on, half_on

    def start(self, ins, outs, sems):
        remote, local, _, _ = self._copies(ins, outs, sems)
        if self.collective_id is not None:
            _handshake(self.peers)
        for a in range(self.n):
            local(a).start()
        for m in self.peers:
            for a in range(self.n):
                remote(a, m, False).start()

    def relay(self, ins, outs, sems):
        remote, _, passed_on, half_on = self._copies(ins, outs, sems)
        for h, (via, _, _) in enumerate(self.HALF_VIA):
            for a in range(self.n):
                remote(a, via, True).wait_recv()
                passed_on(a, via).start()
                if self._halves(a)[h] is not None:
                    half_on(a, h, False).start()

    def relay_last(self, ins, outs, sems):
        _, _, passed_on, half_on = self._copies(ins, outs, sems)
        for a in range(self.n):
            for h in range(2):
                if self._halves(a)[h] is not None:
                    half_on(a, h, True).wait_recv()
            passed_on(a, 6).start()

    def wait(self, ins, outs, sems):
        remote, local, passed_on, half_on = self._copies(ins, outs, sems)
        if self.relay_at is None:
            for m in self.peers:
                for a in range(self.n):
                    remote(a, m, True).wait_recv()
            for m in self.peers:
                for a in range(self.n):
                    remote(a, m, False).wait_send()
        else:
            for m in (1, 3, 5, 7):
                for a in range(self.n):
                    remote(a, m, True).wait_recv()
            for m in self.peers:
                for a in range(self.n):
                    remote(a, m, False).wait_send()
            for a in range(self.n):
                for m in self.CHIPS:
                    passed_on(a, m).wait_send()
                for h in range(2):
                    if self._halves(a)[h] is not None:
                        half_on(a, h, False).wait_send()
        for a in range(self.n):
            local(a).wait()


def _pair_sum(stacks, name):
    n = len(stacks)
    n_chip = N_DEV // 2
    half = [(n_chip,) + st.shape[1:] for st in stacks]

    def body(*refs):
        ins, outs, mine, theirs = refs[:n], refs[n:2 * n], refs[2 * n:3 * n], refs[3 * n:4 * n]
        local_sems, send_sems, recv_sems = refs[4 * n:]
        mx, my, mc = lax.axis_index("x"), lax.axis_index("y"), lax.axis_index("c")

        def own(a, k):
            return pltpu.make_async_copy(ins[a].at[2 * k + mc], mine[a].at[k], local_sems.at[a, k])

        def swap(a, k):
            return pltpu.make_async_remote_copy(
                src_ref=ins[a].at[2 * k + (1 - mc)], dst_ref=theirs[a].at[k],
                send_sem=send_sems.at[a, k], recv_sem=recv_sems.at[a, k],
                device_id=(mx, my, 1 - mc), device_id_type=pl.DeviceIdType.MESH)

        _handshake((1,))
        for k in range(n_chip):
            for a in range(n):
                own(a, k).start()
                swap(a, k).start()
        for k in range(n_chip):
            for a in range(n):
                own(a, k).wait()
                swap(a, k).wait()
                outs[a][k] = (mine[a][k].astype(F32) + theirs[a][k].astype(F32)).astype(LOW)

    return pl.pallas_call(
        body, name=name,
        out_shape=[jax.ShapeDtypeStruct(h, LOW) for h in half],
        in_specs=[pl.BlockSpec(memory_space=pl.ANY)] * n,
        out_specs=[pl.BlockSpec(memory_space=pltpu.VMEM)] * n,
        scratch_shapes=([pltpu.VMEM(h, st.dtype) for h, st in zip(half, stacks)] * 2
                        + [pltpu.SemaphoreType.DMA((n, n_chip))] * 3),
        compiler_params=pltpu.CompilerParams(vmem_limit_bytes=VMEM_LIMIT_BYTES, collective_id=COLLECTIVE_PAIR),
    )(*stacks)


def _exchange(arrays, sliced, name, relay=False):
    ex = _Exchange(arrays, sliced, relay_at=0 if relay else None)

    def body(*refs):
        ins, outs, sems = refs[:ex.n], refs[ex.n:2 * ex.n], refs[2 * ex.n:]
        ex.start(ins, outs, sems)
        if relay:
            ex.relay(ins, outs, sems)
            ex.relay_last(ins, outs, sems)
        ex.wait(ins, outs, sems)

    return pl.pallas_call(body, name=name, out_shape=ex.out_shape, in_specs=ex.specs, out_specs=ex.specs,
                          scratch_shapes=ex.scratch_shapes,
                          compiler_params=pltpu.CompilerParams(collective_id=ex.collective_id))(*ex.arrays)


def _call(body, *, name, grid, in_specs, out_specs, out_shape, args, scratch_shapes=(), exchange=None):
    params = _params(*(("arbitrary",) * len(grid)))
    if exchange is None:
        return pl.pallas_call(body, name=name, grid=grid, in_specs=in_specs, out_specs=out_specs, out_shape=out_shape,
                              scratch_shapes=list(scratch_shapes), compiler_params=params)(*args)
    exs = list(exchange) if isinstance(exchange, (list, tuple)) else [exchange]
    assert len(exs) == 1 or all(ex.collective_id is None for ex in exs)
    params = pltpu.CompilerParams(dimension_semantics=("arbitrary",) * len(grid), vmem_limit_bytes=VMEM_LIMIT_BYTES,
                                  collective_id=exs[0].collective_id)
    n_in, n_out, n_scr = len(in_specs), len(out_specs), len(scratch_shapes)
    n_ex = sum(ex.n for ex in exs)
    n_steps = 1
    for g in grid:
        n_steps *= g

    def hosted(*refs):
        ins, refs = refs[:n_in], refs[n_in:]
        ex_ins, refs = refs[:n_ex], refs[n_ex:]
        outs, refs = refs[:n_out], refs[n_out:]
        ex_outs, refs = refs[:n_ex], refs[n_ex:]
        scr, sems = refs[:n_scr], refs[n_scr:]
        parts, at = [], 0
        for j, ex in enumerate(exs):
            parts.append((ex_ins[at:at + ex.n], ex_outs[at:at + ex.n], sems[3 * j:3 * j + 3]))
            at += ex.n
        step = pl.program_id(0)
        for ax in range(1, len(grid)):
            step = step * grid[ax] + pl.program_id(ax)

        @pl.when(step == 0)
        def _():
            for ex, part in zip(exs, parts):
                ex.start(*part)

        body(*ins, *outs, *scr)

        for ex, part in zip(exs, parts):
            if ex.relay_at is not None:
                @pl.when(step == min(int(ex.relay_at * n_steps), n_steps - 1))
                def _():
                    ex.relay(*part)

                @pl.when(step == min(int((ex.relay_at + RELAY_LAST_LATER) * n_steps), n_steps - 1))
                def _():
                    ex.relay_last(*part)

        @pl.when(step == n_steps - 1)
        def _():
            for ex, part in zip(exs, parts):
                ex.wait(*part)

    return pl.pallas_call(
        hosted, name=name, grid=grid,
        in_specs=list(in_specs) + [sp for ex in exs for sp in ex.specs],
        out_specs=list(out_specs) + [sp for ex in exs for sp in ex.specs],
        out_shape=list(out_shape) + [sh for ex in exs for sh in ex.out_shape],
        scratch_shapes=list(scratch_shapes) + [sc for ex in exs for sc in ex.scratch_shapes],
        compiler_params=params)(*args, *[arr for ex in exs for arr in ex.arrays])


def _ffn_fwd(x, g, h, wgt, wut, wd, name, exchange=None):
    s_len, d = x.shape
    fc = wd.shape[0]
    ts = min(TOK_TILE, s_len)
    first = h is None

    def body(*refs):
        x_ref, gh_ref, wg_ref, wu_ref, wd_ref, xo_ref = refs[:6]
        a_ref, b_ref, s_ref = refs[-3:]
        xv = x_ref[...]
        if first:
            hb = (xv * _rms_scale(xv) * gh_ref[...]).astype(LOW)
            refs[6][...] = hb
        else:
            hb = gh_ref[...]
        for c0, c1 in _slabs(fc):
            a = _dot_nt(hb, wg_ref[c0:c1, :])
            b = _dot_nt(hb, wu_ref[c0:c1, :])
            s_ref[:, c0:c1] = (a * jax.nn.sigmoid(a) * b).astype(LOW)
            a_ref[:, c0:c1] = a.astype(LOW)
            b_ref[:, c0:c1] = b.astype(LOW)
        xo_ref[...] = xv + 0.5 * _dot(s_ref[...], wd_ref[...])

    tok = pl.BlockSpec((ts, d), lambda t: (t, 0))
    hid = pl.BlockSpec((ts, fc), lambda t: (t, 0))
    tok_out = jax.ShapeDtypeStruct((s_len, d), F32)
    h_out = jax.ShapeDtypeStruct((s_len, d), LOW)
    hid_out = jax.ShapeDtypeStruct((s_len, fc), LOW)
    return _call(
        body,
        name=name,
        grid=(s_len // ts,),
        in_specs=[tok, pl.BlockSpec((1, d), lambda t: (0, 0)) if first else tok,
                  _resident(wgt), _resident(wut), _resident(wd)],
        out_specs=[tok] + ([tok] if first else []) + [hid, hid],
        out_shape=[tok_out] + ([h_out] if first else []) + [hid_out, hid_out],
        scratch_shapes=[pltpu.VMEM((ts, fc), LOW)],
        args=(x, g if first else h, wgt, wut, wd),
        exchange=exchange,
    )


def _slabs(width, slab=256):
    return [(c0, min(c0 + slab, width)) for c0 in range(0, width, slab)]


def _ffn_bwdw(df, a, b, h, wd, name, exchange=None):
    s_len, d = df.shape
    f_len = wd.shape[0]
    tm = _pick(f_len, WGRAD_ROW_CANDIDATES)
    tk = min(WGRAD_TOK_TILE, s_len)
    n_k = s_len // tk

    def body(df_ref, a_ref, b_ref, h_ref, wd_ref, da_ref, db_ref, dwd_ref, dwg_ref, dwu_ref,
             s_ref, acc_d, acc_g, acc_u):
        k = pl.program_id(1)

        @pl.when(k == 0)
        def _():
            acc_d[...] = jnp.zeros_like(acc_d)
            acc_g[...] = jnp.zeros_like(acc_g)
            acc_u[...] = jnp.zeros_like(acc_u)

        wdv = wd_ref[...]
        for r0, r1 in _slabs(tk, BWD_ROW_SLAB):
            ds = _dot_nt(df_ref[r0:r1, :], wdv)
            av = a_ref[r0:r1, :].astype(F32)
            bv = b_ref[r0:r1, :].astype(F32)
            sig = jax.nn.sigmoid(av)
            silu = av * sig
            s_ref[r0:r1, :] = (silu * bv).astype(LOW)
            da_ref[r0:r1, :] = (ds * bv * (sig * (1.0 + av * (1.0 - sig)))).astype(LOW)
            db_ref[r0:r1, :] = (ds * silu).astype(LOW)
            hv = h_ref[r0:r1, :]
            acc_d[...] += _dot_tn(s_ref[r0:r1, :], df_ref[r0:r1, :])
            acc_g[...] += _dot_tn(da_ref[r0:r1, :], hv)
            acc_u[...] += _dot_tn(db_ref[r0:r1, :], hv)

        @pl.when(k == n_k - 1)
        def _():
            dwd_ref[...] = acc_d[...].astype(LOW)
            dwg_ref[...] = acc_g[...].astype(LOW)
            dwu_ref[...] = acc_u[...].astype(LOW)

    hid = pl.BlockSpec((tk, tm), lambda i, k: (k, i))
    tok = pl.BlockSpec((tk, d), lambda i, k: (k, 0))
    wrow = pl.BlockSpec((tm, d), lambda i, k: (i, 0))
    return _call(
        body,
        name=name,
        grid=(f_len // tm, n_k),
        in_specs=[tok, hid, hid, tok, wrow],
        out_specs=[hid, hid, wrow, wrow, wrow],
        out_shape=[jax.ShapeDtypeStruct((s_len, f_len), LOW)] * 2 + [jax.ShapeDtypeStruct((f_len, d), LOW)] * 3,
        scratch_shapes=[pltpu.VMEM((tk, tm), LOW)] + [pltpu.VMEM((tm, d), F32)] * 3,
        args=(df, a, b, h, wd),
        exchange=exchange,
    )


def _ffn_dx(dxo, x, g, parts, name, exchange=None):
    s_len, d = x.shape
    ts = min(TOK_TILE, s_len)
    n_p = len(parts)

    def body(dxo_ref, x_ref, g_ref, *refs):
        dxi_ref, dg_ref = refs[4 * n_p:]

        @pl.when(pl.program_id(0) == 0)
        def _():
            dg_ref[...] = jnp.zeros_like(dg_ref)

        dh = None
        for p in range(n_p):
            da_ref, db_ref, wg_ref, wu_ref = refs[4 * p:4 * p + 4]
            part = _dot(da_ref[...], wg_ref[...]) + _dot(db_ref[...], wu_ref[...])
            dh = part if dh is None else dh + part
        dx, dgp = _rms_bwd(dh, x_ref[...], g_ref[...])
        dxi_ref[...] = dxo_ref[...] + dx
        dg_ref[...] += dgp

    tok = pl.BlockSpec((ts, d), lambda t: (t, 0))
    vec = pl.BlockSpec((1, d), lambda t: (0, 0))
    part_specs, part_args = [], []
    for da, db, wgt, wut in parts:
        hid = pl.BlockSpec((ts, da.shape[1]), lambda t: (t, 0))
        part_specs += [hid, hid, _resident(wgt), _resident(wut)]
        part_args += [da, db, wgt, wut]
    return _call(
        body,
        name=name,
        grid=(s_len // ts,),
        in_specs=[tok, tok, vec] + part_specs,
        out_specs=[tok, vec],
        out_shape=[jax.ShapeDtypeStruct((s_len, d), F32), jax.ShapeDtypeStruct((1, d), F32)],
        args=(dxo, x, g, *part_args),
        exchange=exchange,
    )


def _wgrad_tn(xm, ym, tn, stacked, name, exchange=None):
    s_len, m = xm.shape
    n = ym.shape[1]
    tk = min(WGRAD_TOK_TILE, s_len)
    n_k = s_len // tk

    def body(x_ref, y_ref, o_ref, acc):
        k = pl.program_id(1)

        @pl.when(k == 0)
        def _():
            acc[...] = jnp.zeros_like(acc)

        acc[...] += _dot_tn(x_ref[...].astype(LOW), y_ref[...].astype(LOW))

        @pl.when(k == n_k - 1)
        def _():
            o_ref[...] = acc[...].astype(LOW)

    if stacked:
        out_spec = pl.BlockSpec((None, m, tn), lambda j, k: (j, 0, 0))
        out_shape = jax.ShapeDtypeStruct((n // tn, m, tn), LOW)
    else:
        out_spec = pl.BlockSpec((m, tn), lambda j, k: (0, j))
        out_shape = jax.ShapeDtypeStruct((m, n), LOW)
    return _call(
        body,
        name=name,
        grid=(n // tn, n_k),
        in_specs=[pl.BlockSpec((tk, m), lambda j, k: (k, 0)), pl.BlockSpec((tk, tn), lambda j, k: (k, j))],
        out_specs=[out_spec],
        out_shape=[out_shape],
        scratch_shapes=[pltpu.VMEM((m, tn), F32)],
        args=(xm, ym),
        exchange=exchange,
    )


def _mix_parts(ext_ref, cw, ts, row0, with_pooled=True):
    dc = D_CONV

    def back(off, c0, c1):
        return ext_ref[HALO - off:HALO - off + ts, c0:c1]

    v, gate_b, gate_c = back(0, 0, dc), back(0, dc, 2 * dc), back(0, 2 * dc, 3 * dc)
    z0 = gate_c * v
    z1 = back(1, 2 * dc, 3 * dc) * back(1, 0, dc)
    z2 = back(2, 2 * dc, 3 * dc) * back(2, 0, dc)
    conv = cw[2:3, :] * z0 + cw[1:2, :] * z1 + cw[0:1, :] * z2
    rows = row0 + lax.broadcasted_iota(jnp.int32, (ts, 1), 0)
    pooled, inv_count = [], []
    for grp, w in enumerate(POOL_WINDOWS):
        inv = 1.0 / jnp.minimum(rows + 1, w).astype(F32)
        inv_count.append(inv)
        if with_pooled:
            c0 = 3 * dc + POOL_GC * grp
            u = back(0, c0, c0 + POOL_GC)
            acc = u
            for j in range(1, w):
                acc = acc + back(j, c0, c0 + POOL_GC)
            pooled.append(acc * inv - u)
    return v, gate_b, gate_c, z0, z1, z2, conv, pooled, inv_count


def _mix_fwd(x, g, w_in, conv_w, pool_w, pool_scale, w_out, name, exchange=None):
    s_len, d = x.shape
    n_blk, _, wcols = w_in.shape
    p_len = n_blk * wcols
    d_mix = w_out.shape[0]
    ts = min(MIX_TOK_TILE, s_len)
    dc = D_CONV

    def body(x_ref, g_ref, win_ref, cw_ref, pw_ref, ps_ref, wout_ref, x2_ref, h_ref, proj_ref, pooled_ref,
             ext_ref, cat_ref):
        t = pl.program_id(0)

        @pl.when(t == 0)
        def _():
            ext_ref[0:HALO, :] = jnp.zeros((HALO, p_len), F32)

        xv = x_ref[...]
        hb = (xv * _rms_scale(xv) * g_ref[...]).astype(LOW)
        h_ref[...] = hb
        for k in range(n_blk):
            ext_ref[HALO:HALO + ts, wcols * k:wcols * (k + 1)] = _dot(hb, win_ref[k])
        proj_ref[...] = ext_ref[HALO:HALO + ts, :]

        _, gate_b, _, _, _, _, conv, pooled, _ = _mix_parts(ext_ref, cw_ref[...], ts, t * ts)
        cat_ref[:, 0:dc] = (gate_b * conv).astype(LOW)
        for grp in range(len(POOL_WINDOWS)):
            c0 = POOL_GC * grp
            pooled_b = pooled[grp].astype(LOW)
            pooled_ref[:, c0:c0 + POOL_GC] = pooled_b
            lin = _dot(pooled_b, pw_ref[grp])
            cat_ref[:, dc + c0:dc + c0 + POOL_GC] = (lin * ps_ref[:, c0:c0 + POOL_GC]).astype(LOW)
        x2_ref[...] = xv + _dot(cat_ref[...], wout_ref[...])
        ext_ref[0:HALO, :] = ext_ref[ts:ts + HALO, :]

    tok = pl.BlockSpec((ts, d), lambda t: (t, 0))

    def whole(arr):
        return pl.BlockSpec(arr.shape, lambda t: (0,) * arr.ndim)

    return _call(
        body,
        name=name,
        grid=(s_len // ts,),
        in_specs=[tok, whole(g), _resident(w_in), whole(conv_w), whole(pool_w), whole(pool_scale), _resident(w_out)],
        out_specs=[tok, tok, pl.BlockSpec((ts, p_len), lambda t: (t, 0)),
                   pl.BlockSpec((ts, d_mix - dc), lambda t: (t, 0))],
        out_shape=[
            jax.ShapeDtypeStruct((s_len, d), F32),
            jax.ShapeDtypeStruct((s_len, d), LOW),
            jax.ShapeDtypeStruct((s_len, p_len), F32),
            jax.ShapeDtypeStruct((s_len, d_mix - dc), LOW),
        ],
        scratch_shapes=[pltpu.VMEM((ts + HALO, p_len), F32), pltpu.VMEM((ts, d_mix), LOW)],
        args=(x, g, w_in, conv_w, pool_w, pool_scale, w_out),
        exchange=exchange,
    )


def _mix_bwd(dx2, x, proj, pooled, g, w_in, conv_w, pool_w, pool_scale, w_out, name, exchange=None):
    s_len, d = x.shape
    n_blk, _, wcols = w_in.shape
    p_len = n_blk * wcols
    d_mix = w_out.shape[0]
    ts = min(MIX_TOK_TILE, s_len)
    n_t = s_len // ts
    dc = D_CONV
    n_grp = len(POOL_WINDOWS)

    def body(dx2_ref, x_ref, proj_ref, halo_ref, pooled_ref, g_ref, win_ref, cw_ref, pw_ref, ps_ref, wout_ref,
             dx_ref, dproj_ref, dwout_ref, dg_ref, dcw_ref, dpw_ref, dps_ref, df_ref,
             ext_ref, fut_ref, cat_ref, dwout_acc, win_a, win_b):
        i = pl.program_id(0)
        t = n_t - 1 - i

        @pl.when(i == 0)
        def _():
            win_a[...] = jnp.zeros_like(win_a)
            win_b[...] = jnp.zeros_like(win_b)
            dwout_acc[...] = jnp.zeros_like(dwout_acc)
            dg_ref[...] = jnp.zeros_like(dg_ref)
            dcw_ref[...] = jnp.zeros_like(dcw_ref)
            dpw_ref[...] = jnp.zeros_like(dpw_ref)
            dps_ref[...] = jnp.zeros_like(dps_ref)
            fut_ref[ts:ts + 2 * HALO, :] = jnp.zeros((2 * HALO, d_mix), F32)

        ext_ref[HALO:HALO + ts, :] = proj_ref[...]

        @pl.when(t == 0)
        def _():
            ext_ref[0:HALO, :] = jnp.zeros((HALO, p_len), F32)

        @pl.when(t > 0)
        def _():
            ext_ref[0:HALO, :] = halo_ref[...]

        cw = cw_ref[...]
        v, gate_b, gate_c, z0, z1, z2, conv, _, inv_count = _mix_parts(ext_ref, cw, ts, t * ts, with_pooled=False)
        dx2 = dx2_ref[...]
        dcat = _dot_nt(dx2.astype(LOW), wout_ref[...])

        dy_a = dcat[:, 0:dc]
        dconv = dy_a * gate_b
        fut_ref[0:ts, 0:dc] = dconv
        cat_ref[:, 0:dc] = (gate_b * conv).astype(LOW)
        dproj_ref[:, dc:2 * dc] = (dy_a * conv).astype(LOW)
        dcw_ref[2:3, :] += jnp.sum(dconv * z0, axis=0, keepdims=True)
        dcw_ref[1:2, :] += jnp.sum(dconv * z1, axis=0, keepdims=True)
        dcw_ref[0:1, :] += jnp.sum(dconv * z2, axis=0, keepdims=True)

        dpool = []
        for grp in range(n_grp):
            c0 = POOL_GC * grp
            pooled_b = pooled_ref[:, c0:c0 + POOL_GC]
            lin = _dot(pooled_b, pw_ref[grp])
            dy_b = dcat[:, dc + c0:dc + c0 + POOL_GC]
            scale = ps_ref[:, c0:c0 + POOL_GC]
            cat_ref[:, dc + c0:dc + c0 + POOL_GC] = (lin * scale).astype(LOW)
            dps_ref[:, c0:c0 + POOL_GC] += jnp.sum(dy_b * lin, axis=0, keepdims=True)
            dlin = (dy_b * scale).astype(LOW)
            dpw_ref[grp] += _dot_tn(pooled_b, dlin)
            dpool.append(_dot_nt(dlin, pw_ref[grp]))
            fut_ref[0:ts, dc + c0:dc + c0 + POOL_GC] = dpool[grp] * inv_count[grp]
        dwout_acc[...] += _dot_tn(cat_ref[...], dx2.astype(LOW))

        def ahead(off, c0, c1):
            return fut_ref[off:off + ts, c0:c1]

        dz = cw[2:3, :] * ahead(0, 0, dc) + cw[1:2, :] * ahead(1, 0, dc) + cw[0:1, :] * ahead(2, 0, dc)
        dproj_ref[:, 0:dc] = (dz * gate_c).astype(LOW)
        dproj_ref[:, 2 * dc:3 * dc] = (dz * v).astype(LOW)
        rows_ext = ts + HALO
        for grp, w in enumerate(POOL_WINDOWS):
            c0 = dc + POOL_GC * grp
            src, dst, other = fut_ref, win_a, win_b
            cols = slice(c0, c0 + POOL_GC)
            span = 1
            while span < w:
                dst[0:rows_ext, :] = src[0:rows_ext, cols] + src[span:span + rows_ext, cols]
                src, dst, other, cols = dst, other, dst, slice(0, POOL_GC)
                span *= 2
            dproj_ref[:, 2 * dc + c0:2 * dc + c0 + POOL_GC] = (src[0:ts, cols] - dpool[grp]).astype(LOW)

        dh = _dot_nt(dproj_ref[:, 0:wcols], win_ref[0])
        for k in range(1, n_blk):
            dh += _dot_nt(dproj_ref[:, wcols * k:wcols * (k + 1)], win_ref[k])
        dx, dgp = _rms_bwd(dh, x_ref[...], g_ref[...])
        dx = dx2 + dx
        dx_ref[...] = dx
        df_ref[...] = (0.5 * dx).astype(LOW)
        dg_ref[...] += dgp
        fut_ref[ts:ts + HALO, :] = fut_ref[0:HALO, :]

        @pl.when(i == n_t - 1)
        def _():
            dwout_ref[...] = dwout_acc[...].astype(LOW)

    tok = pl.BlockSpec((ts, d), lambda i: (n_t - 1 - i, 0))
    halo = pl.BlockSpec((HALO, p_len), lambda i: (jnp.maximum((n_t - 1 - i) * (ts // HALO) - 1, 0), 0))

    def whole(arr):
        return pl.BlockSpec(arr.shape, lambda i: (0,) * arr.ndim)

    return _call(
        body,
        name=name,
        grid=(n_t,),
        in_specs=[tok, tok, pl.BlockSpec((ts, p_len), lambda i: (n_t - 1 - i, 0)), halo,
                  pl.BlockSpec((ts, d_mix - dc), lambda i: (n_t - 1 - i, 0)),
                  whole(g), _resident(w_in), whole(conv_w), whole(pool_w), whole(pool_scale), _resident(w_out)],
        out_specs=[tok, pl.BlockSpec((ts, p_len), lambda i: (n_t - 1 - i, 0)),
                   whole(w_out),
                   whole(g), whole(conv_w), whole(pool_w), whole(pool_scale), tok],
        out_shape=[
            jax.ShapeDtypeStruct((s_len, d), F32),
            jax.ShapeDtypeStruct((s_len, p_len), LOW),
            jax.ShapeDtypeStruct((d_mix, d), LOW),
            jax.ShapeDtypeStruct(g.shape, F32),
            jax.ShapeDtypeStruct(conv_w.shape, F32),
            jax.ShapeDtypeStruct(pool_w.shape, F32),
            jax.ShapeDtypeStruct(pool_scale.shape, F32),
            jax.ShapeDtypeStruct((s_len, d), LOW),
        ],
        scratch_shapes=[pltpu.VMEM((ts + HALO, p_len), F32), pltpu.VMEM((ts + 2 * HALO, d_mix), F32),
                        pltpu.VMEM((ts, d_mix), LOW), pltpu.VMEM((d_mix, d), F32),
                        pltpu.VMEM((ts + 2 * HALO, POOL_GC), F32), pltpu.VMEM((ts + 2 * HALO, POOL_GC), F32)],
        args=(dx2, x, proj, proj, pooled, g, w_in, conv_w, pool_w, pool_scale, w_out),
        exchange=exchange,
    )


def _ffn_fwd_loss(x, h, wgt, wut, wd, g, target, name):
    s_len, d = x.shape
    fc = wd.shape[0]
    ts = min(TOK_TILE, s_len)

    def body(x_ref, h_ref, wg_ref, wu_ref, wd_ref, g_ref, tgt_ref,
             a_ref, b_ref, loss_ref, dx_ref, dg_ref, df_ref, s_ref):
        @pl.when(pl.program_id(0) == 0)
        def _():
            loss_ref[...] = jnp.zeros_like(loss_ref)
            dg_ref[...] = jnp.zeros_like(dg_ref)

        hb = h_ref[...]
        for c0, c1 in _slabs(fc):
            a = _dot_nt(hb, wg_ref[c0:c1, :])
            b = _dot_nt(hb, wu_ref[c0:c1, :])
            s_ref[:, c0:c1] = (a * jax.nn.sigmoid(a) * b).astype(LOW)
            a_ref[:, c0:c1] = a.astype(LOW)
            b_ref[:, c0:c1] = b.astype(LOW)
        xv = x_ref[...] + 0.5 * _dot(s_ref[...], wd_ref[...])
        gv = g_ref[...]
        err = xv * _rms_scale(xv) * gv - tgt_ref[...]
        loss_ref[...] += 0.5 * jnp.sum(jnp.mean(err * err, axis=-1, keepdims=True), axis=0, keepdims=True)
        dx, dgp = _rms_bwd(err * (1.0 / d), xv, gv)
        dx_ref[...] = dx
        df_ref[...] = (0.5 * dx).astype(LOW)
        dg_ref[...] += dgp

    tok = pl.BlockSpec((ts, d), lambda t: (t, 0))
    vec = pl.BlockSpec((1, d), lambda t: (0, 0))
    hid = pl.BlockSpec((ts, fc), lambda t: (t, 0))
    return pl.pallas_call(
        body,
        name=name,
        grid=(s_len // ts,),
        in_specs=[tok, tok, _resident(wgt), _resident(wut), _resident(wd), vec, tok],
        out_specs=[hid, hid, pl.BlockSpec((1, 128), lambda t: (0, 0)), tok, vec, tok],
        out_shape=[
            jax.ShapeDtypeStruct((s_len, fc), LOW),
            jax.ShapeDtypeStruct((s_len, fc), LOW),
            jax.ShapeDtypeStruct((1, 128), F32),
            jax.ShapeDtypeStruct((s_len, d), F32),
            jax.ShapeDtypeStruct((1, d), F32),
            jax.ShapeDtypeStruct((s_len, d), LOW),
        ],
        scratch_shapes=[pltpu.VMEM((ts, fc), LOW)],
        compiler_params=_params("arbitrary"),
    )(x, h, wgt, wut, wd, g, target)


def _row_tile(rows, cols, stack_bytes):
    budget = 20 * 1024 * 1024
    per_row = cols * (4 * 7 + stack_bytes)
    for tr in (rows, 512, 256, 176, 128, 64, 32, 16, 8):
        if rows % tr == 0 and tr % 8 == 0 and tr * per_row * 2 <= budget:
            return tr
    return rows


def _sum_stack(stack, name):
    n, r, c = stack.shape
    tr = _row_tile(r, c, n * stack.dtype.itemsize)

    def body(s_ref, o_ref):
        acc = s_ref[0].astype(F32)
        for k in range(1, n):
            acc = acc + s_ref[k].astype(F32)
        o_ref[...] = acc

    return pl.pallas_call(
        body,
        name=name,
        grid=(r // tr,),
        in_specs=[pl.BlockSpec((n, tr, c), lambda i: (0, i, 0))],
        out_specs=pl.BlockSpec((tr, c), lambda i: (i, 0)),
        out_shape=jax.ShapeDtypeStruct((r, c), F32),
        compiler_params=_params("arbitrary"),
    )(stack)


def _adamw_many(params, name):
    params = [(list(st) if isinstance(st, (list, tuple)) else [st], w, m, v) for st, w, m, v in params]
    stacks0, w0 = params[0][0], params[0][1]
    r, c = w0.shape
    n = stacks0[0].shape[0]
    n_st = len(stacks0)
    part_rows = [st.shape[1] for st in stacks0]
    assert sum(part_rows) == r
    assert all(w.shape == (r, c) and [st.shape for st in sts] == [st.shape for st in stacks0] for sts, w, _, _ in params)
    first_row = [sum(part_rows[:j]) for j in range(n_st)]
    tc = next(t for t in (512, 256, 128) if c % t == 0)
    c1 = 1.0 - ADAM_B1 ** ADAM_STEP
    c2 = 1.0 - ADAM_B2 ** ADAM_STEP
    n_in = n_st + 3

    def body(*refs):
        ins, outs = refs[:n_in * len(params)], refs[n_in * len(params):]
        for p in range(len(params)):
            s_refs = ins[n_in * p:n_in * p + n_st]
            w_ref, m_ref, v_ref = ins[n_in * p + n_st:n_in * (p + 1)]
            g_ref, d_ref, mo_ref, vo_ref = outs[4 * p:4 * p + 4]
            for s_ref, r0, nr in zip(s_refs, first_row, part_rows):
                gv = s_ref[0].astype(F32)
                for k in range(1, n):
                    gv = gv + s_ref[k].astype(F32)
                mn = ADAM_B1 * m_ref[r0:r0 + nr, :] + (1.0 - ADAM_B1) * gv
                vn = ADAM_B2 * v_ref[r0:r0 + nr, :] + (1.0 - ADAM_B2) * (gv * gv)
                g_ref[r0:r0 + nr, :] = gv
                mo_ref[r0:r0 + nr, :] = mn
                vo_ref[r0:r0 + nr, :] = vn
                d_ref[r0:r0 + nr, :] = -ADAM_LR * ((mn / c1) / (jnp.sqrt(vn / c2) + ADAM_EPS)
                                                   + ADAM_WD * w_ref[r0:r0 + nr, :])

    blk = pl.BlockSpec((r, tc), lambda i: (0, i))
    one_in = [pl.BlockSpec((n, nr, tc), lambda i: (0, 0, i)) for nr in part_rows] + [blk, blk, blk]
    res = pl.pallas_call(
        body,
        name=name,
        grid=(c // tc,),
        in_specs=one_in * len(params),
        out_specs=[blk] * (4 * len(params)),
        out_shape=[jax.ShapeDtypeStruct((r, c), F32)] * (4 * len(params)),
        compiler_params=_params("arbitrary"),
    )(*[arr for sts, w, m, v in params for arr in (*sts, w, m, v)])
    return [tuple(res[4 * p:4 * p + 4]) for p in range(len(params))]


def _adamw(stacks, w, m, v, name):
    return _adamw_many([(stacks, w, m, v)], name)[0]


def _to_sheet(parts):
    sheets, spans = [], []
    row = 0
    for p in parts:
        flat = p.reshape(-1).astype(F32)
        rows = -(-flat.shape[0] // 1024) * 8
        flat = jnp.pad(flat, (0, rows * 128 - flat.shape[0]))
        sheets.append(flat.reshape(rows, 128))
        spans.append((row, p.size, p.shape))
        row += rows
    return jnp.concatenate(sheets, axis=0), spans


def _from_sheet(sheet, spans):
    out = []
    for row, size, shape in spans:
        rows = -(-size // 1024) * 8
        out.append(sheet[row:row + rows].reshape(-1)[:size].reshape(shape))
    return out


def kernel(x, norm_ffn1, ffn1_w_gate, ffn1_w_up, ffn1_w_down, norm_mix, w_in, conv_w, pool_w, pool_scale, w_out, norm_ffn2, ffn2_w_gate, ffn2_w_up, ffn2_w_down, norm_final, loss_target, m_norm_ffn1, m_ffn1_w_gate, m_ffn1_w_up, m_ffn1_w_down, m_norm_mix, m_w_in, m_conv_w, m_pool_w, m_pool_scale, m_w_out, m_norm_ffn2, m_ffn2_w_gate, m_ffn2_w_up, m_ffn2_w_down, m_norm_final, v_norm_ffn1, v_ffn1_w_gate, v_ffn1_w_up, v_ffn1_w_down, v_norm_mix, v_w_in, v_conv_w, v_pool_w, v_pool_scale, v_w_out, v_norm_ffn2, v_ffn2_w_gate, v_ffn2_w_up, v_ffn2_w_down, v_norm_final):
    me = 4 * lax.axis_index("x") + 2 * lax.axis_index("y") + lax.axis_index("c")
    xs, tgt = x[0], loss_target[0]
    s_len, d = xs.shape
    f_shard = ffn1_w_down.shape[1]
    conv_shard = conv_w.shape[2]

    def low_t(wt):
        return wt[0].T.astype(LOW)

    def by_dev(gw):
        return gw.reshape(N_DEV, -1, d)

    conv_tile = jnp.zeros((8, 128), F32).at[0:conv_w.shape[1], 0:conv_shard].set(conv_w[0])
    pool_w_low = pool_w[0].astype(LOW)

    rows_a = -(-f_shard // 64) * 32

    def parts_of(w_gate, w_up, w_down):
        shards = [low_t(w_gate), low_t(w_up), w_down[0].astype(LOW)]
        return [(s, 0, rows_a) for s in shards], [(s, rows_a, f_shard - rows_a) for s in shards]

    def rows_flat(stacks):
        return [st.reshape(-1, d) for st in stacks]

    def gather(shards):
        return _Exchange(shards, [False] * len(shards), relay_at=0.6)

    def scatter(pairs):
        return _Exchange(pairs, [True] * len(pairs), among_chips=True)

    w1a_shards, w1b_shards = parts_of(ffn1_w_gate, ffn1_w_up, ffn1_w_down)
    w2a_shards, w2b_shards = parts_of(ffn2_w_gate, ffn2_w_up, ffn2_w_down)

    wg1a, wu1a, wd1a = rows_flat(_exchange(w1a_shards, [False] * 3, "gather_ffn1_a", relay=True))
    xa, h1, a1a, b1a, *w1b = _ffn_fwd(xs, norm_ffn1, None, wg1a, wu1a, wd1a, "ffn1_fwd_a",
                                      exchange=gather(w1b_shards))
    wg1b, wu1b, wd1b = rows_flat(w1b)
    x1, a1b, b1b, w_in_full, w_out_full, conv_tiles = _ffn_fwd(
        xa, None, h1, wg1b, wu1b, wd1b, "ffn1_fwd_b",
        exchange=gather([w_in[0].astype(LOW), w_out[0].astype(LOW), conv_tile]))
    w_out_full = w_out_full.reshape(-1, d)
    conv_full = jnp.concatenate([conv_tiles[k, 0:conv_w.shape[1], 0:conv_shard] for k in range(N_DEV)], axis=1)
    x2, h2, proj, pooled, *w2a = _mix_fwd(x1, norm_mix, w_in_full, conv_full, pool_w_low, pool_scale, w_out_full,
                                          "mix_fwd", exchange=gather(w2a_shards))
    wg2a, wu2a, wd2a = rows_flat(w2a)
    xb, h3, a2a, b2a, *w2b = _ffn_fwd(x2, norm_ffn2, None, wg2a, wu2a, wd2a, "ffn2_fwd_a",
                                      exchange=gather(w2b_shards))
    wg2b, wu2b, wd2b = rows_flat(w2b)
    a2b, b2b, loss_row, dx3, dg_final, df3 = _ffn_fwd_loss(
        xb, h3, wg2b, wu2b, wd2b, norm_final.reshape(1, d), tgt, "ffn2_fwd_b_loss")

    da2a, db2a, dwd_a, dwg_a, dwu_a = _ffn_bwdw(df3, a2a, b2a, h3, wd2a, "ffn2_bwdw_a")
    da2b, db2b, dwd_b, dwg_b, dwu_b = _ffn_bwdw(df3, a2b, b2b, h3, wd2b, "ffn2_bwdw_b")
    pairs = _pair_sum([by_dev(dwg_a), by_dev(dwu_a), by_dev(dwd_a), by_dev(dwg_b), by_dev(dwu_b), by_dev(dwd_b)],
                      "pair_sum_ffn2")
    dx2, dg_ffn2, *got_2a = _ffn_dx(dx3, x2, norm_ffn2, [(da2a, db2a, wg2a, wu2a), (da2b, db2b, wg2b, wu2b)],
                                    "ffn2_dx", exchange=scatter(pairs[:3]))
    dx1, dproj, dw_out, dg_mix, dconv, dpool_w, dpool_scale, df1, *got_2b = _mix_bwd(
        dx2, x1, proj, pooled, norm_mix, w_in_full, conv_full, pool_w_low, pool_scale, w_out_full, "mix_bwd",
        exchange=scatter(pairs[3:]))
    small_parts = [dg_mix, dg_ffn2, dg_final, dconv, dpool_w, dpool_scale, loss_row]
    small_sheet, spans = _to_sheet(small_parts)
    dw_in, got_small = _wgrad_tn(h2, dproj, W_IN_SHARD, True, "w_in_wgrad",
                                 exchange=_Exchange([small_sheet], [False], relay_at=0.5))
    da1a, db1a, dwd, dwg, dwu, got_in, got_out = _ffn_bwdw(
        df1, a1a, b1a, h1, wd1a, "ffn1_bwdw_a", exchange=_Exchange([dw_in, by_dev(dw_out)], [True, True]))
    pairs = _pair_sum([by_dev(dwg), by_dev(dwu), by_dev(dwd)], "pair_sum_ffn1_a")
    da1b, db1b, dwd, dwg, dwu, *got_1a = _ffn_bwdw(df1, a1b, b1b, h1, wd1b, "ffn1_bwdw_b", exchange=scatter(pairs))
    pairs = _pair_sum([by_dev(dwg), by_dev(dwu), by_dev(dwd)], "pair_sum_ffn1_b")
    dx0, dg_ffn1, *got_1b = _ffn_dx(dx1, xs, norm_ffn1, [(da1a, db1a, wg1a, wu1a), (da1b, db1b, wg1b, wu1b)],
                                    "ffn1_dx", exchange=scatter(pairs))
    (got_n1,) = _exchange([dg_ffn1.reshape(8, 128)], [False], "gather_dnorm_ffn1")

    outs = {}

    def update(name, stacks, w, m, v):
        outs[name] = _adamw(stacks, w[0], m[0], v[0], "adamw_" + name)

    def update_ffn(prefix, got_a, got_b, gate, up, down):
        res = _adamw_many(
            [([got_a[j], got_b[j]], *[(t[0].T if j < 2 else t[0]) for t in wmv]) for j, wmv in enumerate((gate, up, down))],
            "adamw_" + prefix)
        outs[prefix + "_w_gate"] = tuple(r.T for r in res[0])
        outs[prefix + "_w_up"] = tuple(r.T for r in res[1])
        outs[prefix + "_w_down"] = res[2]

    update_ffn("ffn1", got_1a, got_1b, (ffn1_w_gate, m_ffn1_w_gate, v_ffn1_w_gate),
               (ffn1_w_up, m_ffn1_w_up, v_ffn1_w_up), (ffn1_w_down, m_ffn1_w_down, v_ffn1_w_down))
    update_ffn("ffn2", got_2a, got_2b, (ffn2_w_gate, m_ffn2_w_gate, v_ffn2_w_gate),
               (ffn2_w_up, m_ffn2_w_up, v_ffn2_w_up), (ffn2_w_down, m_ffn2_w_down, v_ffn2_w_down))
    update("w_in", got_in, w_in, m_w_in, v_w_in)
    update("w_out", got_out, w_out, m_w_out, v_w_out)

    g_small = _from_sheet(_sum_stack(got_small, "sum_small"), spans)
    g_norm_ffn1 = _sum_stack(got_n1, "sum_dnorm_ffn1").reshape(norm_ffn1.shape)
    g_conv = lax.dynamic_slice_in_dim(g_small[3], me * conv_shard, conv_shard, axis=1)
    small_names = ["norm_ffn1", "norm_mix", "norm_ffn2", "norm_final", "conv_w", "pool_w", "pool_scale"]
    small_g = [g_norm_ffn1, g_small[0], g_small[1], g_small[2].reshape(norm_final.shape), g_conv[None],
               g_small[4][None], g_small[5]]
    small_w = [norm_ffn1, norm_mix, norm_ffn2, norm_final, conv_w, pool_w, pool_scale]
    small_m = [m_norm_ffn1, m_norm_mix, m_norm_ffn2, m_norm_final, m_conv_w, m_pool_w, m_pool_scale]
    small_v = [v_norm_ffn1, v_norm_mix, v_norm_ffn2, v_norm_final, v_conv_w, v_pool_w, v_pool_scale]
    g_sheet, spans_u = _to_sheet(small_g)
    w_sheet, _ = _to_sheet(small_w)
    m_sheet, _ = _to_sheet(small_m)
    v_sheet, _ = _to_sheet(small_v)
    upd = _adamw(g_sheet[None], w_sheet, m_sheet, v_sheet, "adamw_small")
    small_out = [_from_sheet(u, spans_u) for u in upd]
    for k, nm in enumerate(small_names):
        outs[nm] = tuple(small_out[j][k] for j in range(4))

    loss = g_small[6][0, 0]
    order = ["norm_ffn1", "ffn1_w_gate", "ffn1_w_up", "ffn1_w_down", "norm_mix", "w_in", "conv_w", "pool_w",
             "pool_scale", "w_out", "norm_ffn2", "ffn2_w_gate", "ffn2_w_up", "ffn2_w_down", "norm_final"]
    big = {"ffn1_w_gate", "ffn1_w_up", "ffn1_w_down", "w_in", "w_out", "ffn2_w_gate", "ffn2_w_up", "ffn2_w_down"}

    def leaf(nm, j):
        val = outs[nm][j]
        return val[None] if nm in big else val

    return (loss, dx0[None],
            *[leaf(nm, 0) for nm in order], *[leaf(nm, 1) for nm in order],
            *[leaf(nm, 2) for nm in order], *[leaf(nm, 3) for nm in order])
```

```python
import jax
import jax.numpy as jnp
from jax import lax
from jax.experimental import pallas as pl
from jax.experimental.pallas import tpu as pltpu

F32 = jnp.float32
LOW = jnp.bfloat16

N_DEV = 8
EPS = 1e-6
D_CONV = 512
POOL_WINDOWS = (2, 4, 8, 16)
POOL_GC = 128
HALO = 16
W_IN_SHARD = 256

ADAM_LR = 0.001
ADAM_B1 = 0.9
ADAM_B2 = 0.999
ADAM_EPS = 1e-08
ADAM_WD = 0.01
ADAM_STEP = 10

VMEM_LIMIT_BYTES = 56 * 1024 * 1024
TOK_TILE = 512
MIX_TOK_TILE = 512
WGRAD_TOK_TILE = 4096
WGRAD_ROW_CANDIDATES = (256, 128)
RELAY_LAST_LATER = 0.25
BWD_ROW_SLAB = 2048


def _params(*sem):
    return pltpu.CompilerParams(dimension_semantics=sem, vmem_limit_bytes=VMEM_LIMIT_BYTES)


def _resident(arr):
    return pl.BlockSpec(arr.shape, lambda *_: (0,) * arr.ndim, pipeline_mode=pl.Buffered(1))


def _pick(n, candidates):
    for c in candidates:
        if n % c == 0:
            return c
    raise ValueError(f"no tile in {candidates} divides {n}")


def _dot(a, b):
    return lax.dot_general(a, b, (((1,), (0,)), ((), ())), preferred_element_type=F32)


def _dot_nt(a, b):
    return lax.dot_general(a, b, (((1,), (1,)), ((), ())), preferred_element_type=F32)


def _dot_tn(a, b):
    return lax.dot_general(a, b, (((0,), (0,)), ((), ())), preferred_element_type=F32)


def _rms_scale(x):
    return lax.rsqrt(jnp.mean(x * x, axis=-1, keepdims=True) + EPS)


def _rms_bwd(dy, x, g):
    r = _rms_scale(x)
    xhat = x * r
    gdy = dy * g
    dx = r * (gdy - xhat * jnp.mean(gdy * xhat, axis=-1, keepdims=True))
    return dx, jnp.sum(dy * xhat, axis=0, keepdims=True)


COLLECTIVE_PAIR, COLLECTIVE_CHIPS, COLLECTIVE_RELAY = 0, 1, 2


def _handshake(peer_numbers):
    mx, my, mc = lax.axis_index("x"), lax.axis_index("y"), lax.axis_index("c")
    barrier = pltpu.get_barrier_semaphore()
    for m in peer_numbers:
        peer = (lax.rem(mx + ((m >> 2) & 1), 2), lax.rem(my + ((m >> 1) & 1), 2), lax.rem(mc + (m & 1), 2))
        pl.semaphore_signal(barrier, inc=1, device_id=peer, device_id_type=pl.DeviceIdType.MESH)
    pl.semaphore_wait(barrier, len(peer_numbers))


class _Exchange:
    CHIPS = (2, 4, 6)

    def __init__(self, arrays, sliced, relay_at=None, among_chips=False):
        assert relay_at is None or not any(sliced)
        assert not among_chips or (all(sliced) and relay_at is None)
        self.relay_at, self.among_chips = relay_at, among_chips
        self.peers = self.CHIPS if among_chips else ((1, 2, 4) if relay_at is not None else tuple(range(1, N_DEV)))
        self.collective_id = COLLECTIVE_CHIPS if among_chips else (COLLECTIVE_RELAY if relay_at is not None else None)
        self.rows = [arr[1:] if isinstance(arr, tuple) else None for arr in arrays]
        self.arrays = [arr[0] if isinstance(arr, tuple) else arr for arr in arrays]
        self.sliced, self.n = list(sliced), len(arrays)
        assert all(rg is None or not sl for rg, sl in zip(self.rows, sliced))
        self.block_shape = [arr.shape if rg is None else (rg[1],) + arr.shape[1:]
                            for arr, rg in zip(self.arrays, self.rows)]
        self.out_shape = [jax.ShapeDtypeStruct(shape if sl else (N_DEV,) + shape, arr.dtype)
                          for arr, shape, sl in zip(self.arrays, self.block_shape, sliced)]
        self.specs = [pl.BlockSpec(memory_space=pl.ANY)] * self.n
        self.scratch_shapes = [pltpu.SemaphoreType.DMA((self.n, N_DEV)),
                               pltpu.SemaphoreType.DMA((self.n, N_DEV)),
                               pltpu.SemaphoreType.DMA((self.n,))]

    HALF_VIA = ((4, 2, 5), (2, 4, 7))

    def _halves(self, a):
        rows = self.block_shape[a][0]
        if rows % 32:
            return ((0, rows), None)
        return ((0, rows // 2), (rows // 2, rows // 2))

    def _copies(self, ins, outs, sems):
        send_sems, recv_sems, local_sems = sems
        sliced = self.sliced
        mx, my, mc = lax.axis_index("x"), lax.axis_index("y"), lax.axis_index("c")
        me = 2 * mx + my if self.among_chips else 4 * mx + 2 * my + mc

        def peer(m):
            px = lax.rem(mx + ((m >> 2) & 1), 2)
            py = lax.rem(my + ((m >> 1) & 1), 2)
            pc = lax.rem(mc + (m & 1), 2)
            return (px, py, pc), (2 * px + py if self.among_chips else 4 * px + 2 * py + pc)

        def mine(a):
            return ins[a] if self.rows[a] is None else ins[a].at[pl.ds(*self.rows[a])]

        def remote(a, m, arriving):
            pid, pflat = peer(m)
            return pltpu.make_async_remote_copy(
                src_ref=ins[a].at[pflat] if sliced[a] else mine(a),
                dst_ref=outs[a].at[pflat if arriving else me],
                send_sem=send_sems.at[a, m - 1],
                recv_sem=recv_sems.at[a, m - 1],
                device_id=pid,
                device_id_type=pl.DeviceIdType.MESH,
            )

        def local(a):
            return pltpu.make_async_copy(ins[a].at[me] if sliced[a] else mine(a), outs[a].at[me], local_sems.at[a])

        def passed_on(a, m):
            _, origin = peer(m)
            sibling, _ = peer(1)
            return pltpu.make_async_remote_copy(
                src_ref=outs[a].at[origin],
                dst_ref=outs[a].at[origin],
                send_sem=send_sems.at[a, m],
                recv_sem=recv_sems.at[a, m],
                device_id=sibling,
                device_id_type=pl.DeviceIdType.MESH,
            )

        def half_on(a, h, arriving):
            via, to, column = self.HALF_VIA[h]
            r0, nr = self._halves(a)[h]
            _, origin = peer(6 if arriving else via)
            rows = outs[a].at[origin].at[pl.ds(r0, nr)]
            return pltpu.make_async_remote_copy(
                src_ref=rows, dst_ref=rows, send_sem=send_sems.at[a, column], recv_sem=recv_sems.at[a, column],
                device_id=peer(to)[0], device_id_type=pl.DeviceIdType.MESH)

        return remote, local, passed_on, half_on

    def start(self, ins, outs, sems):
        remote, local, _, _ = self._copies(ins, outs, sems)
        if self.collective_id is not None:
            _handshake(self.peers)
        for a in range(self.n):
            local(a).start()
        for m in self.peers:
            for a in range(self.n):
                remote(a, m, False).start()

    def relay(self, ins, outs, sems):
        remote, _, passed_on, half_on = self._copies(ins, outs, sems)
        for h, (via, _, _) in enumerate(self.HALF_VIA):
            for a in range(self.n):
                remote(a, via, True).wait_recv()
                passed_on(a, via).start()
                if self._halves(a)[h] is not None:
                    half_on(a, h, False).start()

    def relay_last(self, ins, outs, sems):
        _, _, passed_on, half_on = self._copies(ins, outs, sems)
        for a in range(self.n):
            for h in range(2):
                if self._halves(a)[h] is not None:
                    half_on(a, h, True).wait_recv()
            passed_on(a, 6).start()

    def wait(self, ins, outs, sems):
        remote, local, passed_on, half_on = self._copies(ins, outs, sems)
        if self.relay_at is None:
            for m in self.peers:
                for a in range(self.n):
                    remote(a, m, True).wait_recv()
            for m in self.peers:
                for a in range(self.n):
                    remote(a, m, False).wait_send()
        else:
            for m in (1, 3, 5, 7):
                for a in range(self.n):
                    remote(a, m, True).wait_recv()
            for m in self.peers:
                for a in range(self.n):
                    remote(a, m, False).wait_send()
            for a in range(self.n):
                for m in self.CHIPS:
                    passed_on(a, m).wait_send()
                for h in range(2):
                    if self._halves(a)[h] is not None:
                        half_on(a, h, False).wait_send()
        for a in range(self.n):
            local(a).wait()


def _pair_sum(stacks, name):
    n = len(stacks)
    n_chip = N_DEV // 2
    half = [(n_chip,) + st.shape[1:] for st in stacks]

    def body(*refs):
        ins, outs, mine, theirs = refs[:n], refs[n:2 * n], refs[2 * n:3 * n], refs[3 * n:4 * n]
        local_sems, send_sems, recv_sems = refs[4 * n:]
        mx, my, mc = lax.axis_index("x"), lax.axis_index("y"), lax.axis_index("c")

        def own(a, k):
            return pltpu.make_async_copy(ins[a].at[2 * k + mc], mine[a].at[k], local_sems.at[a, k])

        def swap(a, k):
            return pltpu.make_async_remote_copy(
                src_ref=ins[a].at[2 * k + (1 - mc)], dst_ref=theirs[a].at[k],
                send_sem=send_sems.at[a, k], recv_sem=recv_sems.at[a, k],
                device_id=(mx, my, 1 - mc), device_id_type=pl.DeviceIdType.MESH)

        _handshake((1,))
        for k in range(n_chip):
            for a in range(n):
                own(a, k).start()
                swap(a, k).start()
        for k in range(n_chip):
            for a in range(n):
                own(a, k).wait()
                swap(a, k).wait()
                outs[a][k] = (mine[a][k].astype(F32) + theirs[a][k].astype(F32)).astype(LOW)

    return pl.pallas_call(
        body, name=name,
        out_shape=[jax.ShapeDtypeStruct(h, LOW) for h in half],
        in_specs=[pl.BlockSpec(memory_space=pl.ANY)] * n,
        out_specs=[pl.BlockSpec(memory_space=pltpu.VMEM)] * n,
        scratch_shapes=([pltpu.VMEM(h, st.dtype) for h, st in zip(half, stacks)] * 2
                        + [pltpu.SemaphoreType.DMA((n, n_chip))] * 3),
        compiler_params=pltpu.CompilerParams(vmem_limit_bytes=VMEM_LIMIT_BYTES, collective_id=COLLECTIVE_PAIR),
    )(*stacks)


def _exchange(arrays, sliced, name, relay=False):
    ex = _Exchange(arrays, sliced, relay_at=0 if relay else None)

    def body(*refs):
        ins, outs, sems = refs[:ex.n], refs[ex.n:2 * ex.n], refs[2 * ex.n:]
        ex.start(ins, outs, sems)
        if relay:
            ex.relay(ins, outs, sems)
            ex.relay_last(ins, outs, sems)
        ex.wait(ins, outs, sems)

    return pl.pallas_call(body, name=name, out_shape=ex.out_shape, in_specs=ex.specs, out_specs=ex.specs,
                          scratch_shapes=ex.scratch_shapes,
                          compiler_params=pltpu.CompilerParams(collective_id=ex.collective_id))(*ex.arrays)


def _call(body, *, name, grid, in_specs, out_specs, out_shape, args, scratch_shapes=(), exchange=None):
    params = _params(*(("arbitrary",) * len(grid)))
    if exchange is None:
        return pl.pallas_call(body, name=name, grid=grid, in_specs=in_specs, out_specs=out_specs, out_shape=out_shape,
                              scratch_shapes=list(scratch_shapes), compiler_params=params)(*args)
    exs = list(exchange) if isinstance(exchange, (list, tuple)) else [exchange]
    assert len(exs) == 1 or all(ex.collective_id is None for ex in exs)
    params = pltpu.CompilerParams(dimension_semantics=("arbitrary",) * len(grid), vmem_limit_bytes=VMEM_LIMIT_BYTES,
                                  collective_id=exs[0].collective_id)
    n_in, n_out, n_scr = len(in_specs), len(out_specs), len(scratch_shapes)
    n_ex = sum(ex.n for ex in exs)
    n_steps = 1
    for g in grid:
        n_steps *= g

    def hosted(*refs):
        ins, refs = refs[:n_in], refs[n_in:]
        ex_ins, refs = refs[:n_ex], refs[n_ex:]
        outs, refs = refs[:n_out], refs[n_out:]
        ex_outs, refs = refs[:n_ex], refs[n_ex:]
        scr, sems = refs[:n_scr], refs[n_scr:]
        parts, at = [], 0
        for j, ex in enumerate(exs):
            parts.append((ex_ins[at:at + ex.n], ex_outs[at:at + ex.n], sems[3 * j:3 * j + 3]))
            at += ex.n
        step = pl.program_id(0)
        for ax in range(1, len(grid)):
            step = step * grid[ax] + pl.program_id(ax)

        @pl.when(step == 0)
        def _():
            for ex, part in zip(exs, parts):
                ex.start(*part)

        body(*ins, *outs, *scr)

        for ex, part in zip(exs, parts):
            if ex.relay_at is not None:
                @pl.when(step == min(int(ex.relay_at * n_steps), n_steps - 1))
                def _():
                    ex.relay(*part)

                @pl.when(step == min(int((ex.relay_at + RELAY_LAST_LATER) * n_steps), n_steps - 1))
                def _():
                    ex.relay_last(*part)

        @pl.when(step == n_steps - 1)
        def _():
            for ex, part in zip(exs, parts):
                ex.wait(*part)

    return pl.pallas_call(
        hosted, name=name, grid=grid,
        in_specs=list(in_specs) + [sp for ex in exs for sp in ex.specs],
        out_specs=list(out_specs) + [sp for ex in exs for sp in ex.specs],
        out_shape=list(out_shape) + [sh for ex in exs for sh in ex.out_shape],
        scratch_shapes=list(scratch_shapes) + [sc for ex in exs for sc in ex.scratch_shapes],
        compiler_params=params)(*args, *[arr for ex in exs for arr in ex.arrays])


def _ffn_fwd(x, g, h, wgt, wut, wd, name, exchange=None):
    s_len, d = x.shape
    fc = wd.shape[0]
    ts = min(TOK_TILE, s_len)
    first = h is None

    def body(*refs):
        x_ref, gh_ref, wg_ref, wu_ref, wd_ref, xo_ref = refs[:6]
        a_ref, b_ref, s_ref = refs[-3:]
        xv = x_ref[...]
        if first:
            hb = (xv * _rms_scale(xv) * gh_ref[...]).astype(LOW)
            refs[6][...] = hb
        else:
            hb = gh_ref[...]
        for c0, c1 in _slabs(fc):
            a = _dot_nt(hb, wg_ref[c0:c1, :])
            b = _dot_nt(hb, wu_ref[c0:c1, :])
            s_ref[:, c0:c1] = (a * jax.nn.sigmoid(a) * b).astype(LOW)
            a_ref[:, c0:c1] = a.astype(LOW)
            b_ref[:, c0:c1] = b.astype(LOW)
        xo_ref[...] = xv + 0.5 * _dot(s_ref[...], wd_ref[...])

    tok = pl.BlockSpec((ts, d), lambda t: (t, 0))
    hid = pl.BlockSpec((ts, fc), lambda t: (t, 0))
    tok_out = jax.ShapeDtypeStruct((s_len, d), F32)
    h_out = jax.ShapeDtypeStruct((s_len, d), LOW)
    hid_out = jax.ShapeDtypeStruct((s_len, fc), LOW)
    return _call(
        body,
        name=name,
        grid=(s_len // ts,),
        in_specs=[tok, pl.BlockSpec((1, d), lambda t: (0, 0)) if first else tok,
                  _resident(wgt), _resident(wut), _resident(wd)],
        out_specs=[tok] + ([tok] if first else []) + [hid, hid],
        out_shape=[tok_out] + ([h_out] if first else []) + [hid_out, hid_out],
        scratch_shapes=[pltpu.VMEM((ts, fc), LOW)],
        args=(x, g if first else h, wgt, wut, wd),
        exchange=exchange,
    )


def _slabs(width, slab=256):
    return [(c0, min(c0 + slab, width)) for c0 in range(0, width, slab)]


def _ffn_bwdw(df, a, b, h, wd, name, exchange=None):
    s_len, d = df.shape
    f_len = wd.shape[0]
    tm = _pick(f_len, WGRAD_ROW_CANDIDATES)
    tk = min(WGRAD_TOK_TILE, s_len)
    n_k = s_len // tk

    def body(df_ref, a_ref, b_ref, h_ref, wd_ref, da_ref, db_ref, dwd_ref, dwg_ref, dwu_ref,
             s_ref, acc_d, acc_g, acc_u):
        k = pl.program_id(1)

        @pl.when(k == 0)
        def _():
            acc_d[...] = jnp.zeros_like(acc_d)
            acc_g[...] = jnp.zeros_like(acc_g)
            acc_u[...] = jnp.zeros_like(acc_u)

        wdv = wd_ref[...]
        for r0, r1 in _slabs(tk, BWD_ROW_SLAB):
            ds = _dot_nt(df_ref[r0:r1, :], wdv)
            av = a_ref[r0:r1, :].astype(F32)
            bv = b_ref[r0:r1, :].astype(F32)
            sig = jax.nn.sigmoid(av)
            silu = av * sig
            s_ref[r0:r1, :] = (silu * bv).astype(LOW)
            da_ref[r0:r1, :] = (ds * bv * (sig * (1.0 + av * (1.0 - sig)))).astype(LOW)
            db_ref[r0:r1, :] = (ds * silu).astype(LOW)
            hv = h_ref[r0:r1, :]
            acc_d[...] += _dot_tn(s_ref[r0:r1, :], df_ref[r0:r1, :])
            acc_g[...] += _dot_tn(da_ref[r0:r1, :], hv)
            acc_u[...] += _dot_tn(db_ref[r0:r1, :], hv)

        @pl.when(k == n_k - 1)
        def _():
            dwd_ref[...] = acc_d[...].astype(LOW)
            dwg_ref[...] = acc_g[...].astype(LOW)
            dwu_ref[...] = acc_u[...].astype(LOW)

    hid = pl.BlockSpec((tk, tm), lambda i, k: (k, i))
    tok = pl.BlockSpec((tk, d), lambda i, k: (k, 0))
    wrow = pl.BlockSpec((tm, d), lambda i, k: (i, 0))
    return _call(
        body,
        name=name,
        grid=(f_len // tm, n_k),
        in_specs=[tok, hid, hid, tok, wrow],
        out_specs=[hid, hid, wrow, wrow, wrow],
        out_shape=[jax.ShapeDtypeStruct((s_len, f_len), LOW)] * 2 + [jax.ShapeDtypeStruct((f_len, d), LOW)] * 3,
        scratch_shapes=[pltpu.VMEM((tk, tm), LOW)] + [pltpu.VMEM((tm, d), F32)] * 3,
        args=(df, a, b, h, wd),
        exchange=exchange,
    )


def _ffn_dx(dxo, x, g, parts, name, exchange=None):
    s_len, d = x.shape
    ts = min(TOK_TILE, s_len)
    n_p = len(parts)

    def body(dxo_ref, x_ref, g_ref, *refs):
        dxi_ref, dg_ref = refs[4 * n_p:]

        @pl.when(pl.program_id(0) == 0)
        def _():
            dg_ref[...] = jnp.zeros_like(dg_ref)

        dh = None
        for p in range(n_p):
            da_ref, db_ref, wg_ref, wu_ref = refs[4 * p:4 * p + 4]
            part = _dot(da_ref[...], wg_ref[...]) + _dot(db_ref[...], wu_ref[...])
            dh = part if dh is None else dh + part
        dx, dgp = _rms_bwd(dh, x_ref[...], g_ref[...])
        dxi_ref[...] = dxo_ref[...] + dx
        dg_ref[...] += dgp

    tok = pl.BlockSpec((ts, d), lambda t: (t, 0))
    vec = pl.BlockSpec((1, d), lambda t: (0, 0))
    part_specs, part_args = [], []
    for da, db, wgt, wut in parts:
        hid = pl.BlockSpec((ts, da.shape[1]), lambda t: (t, 0))
        part_specs += [hid, hid, _resident(wgt), _resident(wut)]
        part_args += [da, db, wgt, wut]
    return _call(
        body,
        name=name,
        grid=(s_len // ts,),
        in_specs=[tok, tok, vec] + part_specs,
        out_specs=[tok, vec],
        out_shape=[jax.ShapeDtypeStruct((s_len, d), F32), jax.ShapeDtypeStruct((1, d), F32)],
        args=(dxo, x, g, *part_args),
        exchange=exchange,
    )


def _wgrad_tn(xm, ym, tn, stacked, name, exchange=None):
    s_len, m = xm.shape
    n = ym.shape[1]
    tk = min(WGRAD_TOK_TILE, s_len)
    n_k = s_len // tk

    def body(x_ref, y_ref, o_ref, acc):
        k = pl.program_id(1)

        @pl.when(k == 0)
        def _():
            acc[...] = jnp.zeros_like(acc)

        acc[...] += _dot_tn(x_ref[...].astype(LOW), y_ref[...].astype(LOW))

        @pl.when(k == n_k - 1)
        def _():
            o_ref[...] = acc[...].astype(LOW)

    if stacked:
        out_spec = pl.BlockSpec((None, m, tn), lambda j, k: (j, 0, 0))
        out_shape = jax.ShapeDtypeStruct((n // tn, m, tn), LOW)
    else:
        out_spec = pl.BlockSpec((m, tn), lambda j, k: (0, j))
        out_shape = jax.ShapeDtypeStruct((m, n), LOW)
    return _call(
        body,
        name=name,
        grid=(n // tn, n_k),
        in_specs=[pl.BlockSpec((tk, m), lambda j, k: (k, 0)), pl.BlockSpec((tk, tn), lambda j, k: (k, j))],
        out_specs=[out_spec],
        out_shape=[out_shape],
        scratch_shapes=[pltpu.VMEM((m, tn), F32)],
        args=(xm, ym),
        exchange=exchange,
    )


def _mix_parts(ext_ref, cw, ts, row0, with_pooled=True):
    dc = D_CONV

    def back(off, c0, c1):
        return ext_ref[HALO - off:HALO - off + ts, c0:c1]

    v, gate_b, gate_c = back(0, 0, dc), back(0, dc, 2 * dc), back(0, 2 * dc, 3 * dc)
    z0 = gate_c * v
    z1 = back(1, 2 * dc, 3 * dc) * back(1, 0, dc)
    z2 = back(2, 2 * dc, 3 * dc) * back(2, 0, dc)
    conv = cw[2:3, :] * z0 + cw[1:2, :] * z1 + cw[0:1, :] * z2
    rows = row0 + lax.broadcasted_iota(jnp.int32, (ts, 1), 0)
    pooled, inv_count = [], []
    for grp, w in enumerate(POOL_WINDOWS):
        inv = 1.0 / jnp.minimum(rows + 1, w).astype(F32)
        inv_count.append(inv)
        if with_pooled:
            c0 = 3 * dc + POOL_GC * grp
            u = back(0, c0, c0 + POOL_GC)
            acc = u
            for j in range(1, w):
                acc = acc + back(j, c0, c0 + POOL_GC)
            pooled.append(acc * inv - u)
    return v, gate_b, gate_c, z0, z1, z2, conv, pooled, inv_count


def _mix_fwd(x, g, w_in, conv_w, pool_w, pool_scale, w_out, name, exchange=None):
    s_len, d = x.shape
    n_blk, _, wcols = w_in.shape
    p_len = n_blk * wcols
    d_mix = w_out.shape[0]
    ts = min(MIX_TOK_TILE, s_len)
    dc = D_CONV

    def body(x_ref, g_ref, win_ref, cw_ref, pw_ref, ps_ref, wout_ref, x2_ref, h_ref, proj_ref, pooled_ref,
             ext_ref, cat_ref):
        t = pl.program_id(0)

        @pl.when(t == 0)
        def _():
            ext_ref[0:HALO, :] = jnp.zeros((HALO, p_len), F32)

        xv = x_ref[...]
        hb = (xv * _rms_scale(xv) * g_ref[...]).astype(LOW)
        h_ref[...] = hb
        for k in range(n_blk):
            ext_ref[HALO:HALO + ts, wcols * k:wcols * (k + 1)] = _dot(hb, win_ref[k])
        proj_ref[...] = ext_ref[HALO:HALO + ts, :]

        _, gate_b, _, _, _, _, conv, pooled, _ = _mix_parts(ext_ref, cw_ref[...], ts, t * ts)
        cat_ref[:, 0:dc] = (gate_b * conv).astype(LOW)
        for grp in range(len(POOL_WINDOWS)):
            c0 = POOL_GC * grp
            pooled_b = pooled[grp].astype(LOW)
            pooled_ref[:, c0:c0 + POOL_GC] = pooled_b
            lin = _dot(pooled_b, pw_ref[grp])
            cat_ref[:, dc + c0:dc + c0 + POOL_GC] = (lin * ps_ref[:, c0:c0 + POOL_GC]).astype(LOW)
        x2_ref[...] = xv + _dot(cat_ref[...], wout_ref[...])
        ext_ref[0:HALO, :] = ext_ref[ts:ts + HALO, :]

    tok = pl.BlockSpec((ts, d), lambda t: (t, 0))

    def whole(arr):
        return pl.BlockSpec(arr.shape, lambda t: (0,) * arr.ndim)

    return _call(
        body,
        name=name,
        grid=(s_len // ts,),
        in_specs=[tok, whole(g), _resident(w_in), whole(conv_w), whole(pool_w), whole(pool_scale), _resident(w_out)],
        out_specs=[tok, tok, pl.BlockSpec((ts, p_len), lambda t: (t, 0)),
                   pl.BlockSpec((ts, d_mix - dc), lambda t: (t, 0))],
        out_shape=[
            jax.ShapeDtypeStruct((s_len, d), F32),
            jax.ShapeDtypeStruct((s_len, d), LOW),
            jax.ShapeDtypeStruct((s_len, p_len), F32),
            jax.ShapeDtypeStruct((s_len, d_mix - dc), LOW),
        ],
        scratch_shapes=[pltpu.VMEM((ts + HALO, p_len), F32), pltpu.VMEM((ts, d_mix), LOW)],
        args=(x, g, w_in, conv_w, pool_w, pool_scale, w_out),
        exchange=exchange,
    )


def _mix_bwd(dx2, x, proj, pooled, g, w_in, conv_w, pool_w, pool_scale, w_out, name, exchange=None):
    s_len, d = x.shape
    n_blk, _, wcols = w_in.shape
    p_len = n_blk * wcols
    d_mix = w_out.shape[0]
    ts = min(MIX_TOK_TILE, s_len)
    n_t = s_len // ts
    dc = D_CONV
    n_grp = len(POOL_WINDOWS)

    def body(dx2_ref, x_ref, proj_ref, halo_ref, pooled_ref, g_ref, win_ref, cw_ref, pw_ref, ps_ref, wout_ref,
             dx_ref, dproj_ref, dwout_ref, dg_ref, dcw_ref, dpw_ref, dps_ref, df_ref,
             ext_ref, fut_ref, cat_ref, dwout_acc, win_a, win_b):
        i = pl.program_id(0)
        t = n_t - 1 - i

        @pl.when(i == 0)
        def _():
            win_a[...] = jnp.zeros_like(win_a)
            win_b[...] = jnp.zeros_like(win_b)
            dwout_acc[...] = jnp.zeros_like(dwout_acc)
            dg_ref[...] = jnp.zeros_like(dg_ref)
            dcw_ref[...] = jnp.zeros_like(dcw_ref)
            dpw_ref[...] = jnp.zeros_like(dpw_ref)
            dps_ref[...] = jnp.zeros_like(dps_ref)
            fut_ref[ts:ts + 2 * HALO, :] = jnp.zeros((2 * HALO, d_mix), F32)

        ext_ref[HALO:HALO + ts, :] = proj_ref[...]

        @pl.when(t == 0)
        def _():
            ext_ref[0:HALO, :] = jnp.zeros((HALO, p_len), F32)

        @pl.when(t > 0)
        def _():
            ext_ref[0:HALO, :] = halo_ref[...]

        cw = cw_ref[...]
        v, gate_b, gate_c, z0, z1, z2, conv, _, inv_count = _mix_parts(ext_ref, cw, ts, t * ts, with_pooled=False)
        dx2 = dx2_ref[...]
        dcat = _dot_nt(dx2.astype(LOW), wout_ref[...])

        dy_a = dcat[:, 0:dc]
        dconv = dy_a * gate_b
        fut_ref[0:ts, 0:dc] = dconv
        cat_ref[:, 0:dc] = (gate_b * conv).astype(LOW)
        dproj_ref[:, dc:2 * dc] = (dy_a * conv).astype(LOW)
        dcw_ref[2:3, :] += jnp.sum(dconv * z0, axis=0, keepdims=True)
        dcw_ref[1:2, :] += jnp.sum(dconv * z1, axis=0, keepdims=True)
        dcw_ref[0:1, :] += jnp.sum(dconv * z2, axis=0, keepdims=True)

        dpool = []
        for grp in range(n_grp):
            c0 = POOL_GC * grp
            pooled_b = pooled_ref[:, c0:c0 + POOL_GC]
            lin = _dot(pooled_b, pw_ref[grp])
            dy_b = dcat[:, dc + c0:dc + c0 + POOL_GC]
            scale = ps_ref[:, c0:c0 + POOL_GC]
            cat_ref[:, dc + c0:dc + c0 + POOL_GC] = (lin * scale).astype(LOW)
            dps_ref[:, c0:c0 + POOL_GC] += jnp.sum(dy_b * lin, axis=0, keepdims=True)
            dlin = (dy_b * scale).astype(LOW)
            dpw_ref[grp] += _dot_tn(pooled_b, dlin)
            dpool.append(_dot_nt(dlin, pw_ref[grp]))
            fut_ref[0:ts, dc + c0:dc + c0 + POOL_GC] = dpool[grp] * inv_count[grp]
        dwout_acc[...] += _dot_tn(cat_ref[...], dx2.astype(LOW))

        def ahead(off, c0, c1):
            return fut_ref[off:off + ts, c0:c1]

        dz = cw[2:3, :] * ahead(0, 0, dc) + cw[1:2, :] * ahead(1, 0, dc) + cw[0:1, :] * ahead(2, 0, dc)
        dproj_ref[:, 0:dc] = (dz * gate_c).astype(LOW)
        dproj_ref[:, 2 * dc:3 * dc] = (dz * v).astype(LOW)
        rows_ext = ts + HALO
        for grp, w in enumerate(POOL_WINDOWS):
            c0 = dc + POOL_GC * grp
            src, dst, other = fut_ref, win_a, win_b
            cols = slice(c0, c0 + POOL_GC)
            span = 1
            while span < w:
                dst[0:rows_ext, :] = src[0:rows_ext, cols] + src[span:span + rows_ext, cols]
                src, dst, other, cols = dst, other, dst, slice(0, POOL_GC)
                span *= 2
            dproj_ref[:, 2 * dc + c0:2 * dc + c0 + POOL_GC] = (src[0:ts, cols] - dpool[grp]).astype(LOW)

        dh = _dot_nt(dproj_ref[:, 0:wcols], win_ref[0])
        for k in range(1, n_blk):
            dh += _dot_nt(dproj_ref[:, wcols * k:wcols * (k + 1)], win_ref[k])
        dx, dgp = _rms_bwd(dh, x_ref[...], g_ref[...])
        dx = dx2 + dx
        dx_ref[...] = dx
        df_ref[...] = (0.5 * dx).astype(LOW)
        dg_ref[...] += dgp
        fut_ref[ts:ts + HALO, :] = fut_ref[0:HALO, :]

        @pl.when(i == n_t - 1)
        def _():
            dwout_ref[...] = dwout_acc[...].astype(LOW)

    tok = pl.BlockSpec((ts, d), lambda i: (n_t - 1 - i, 0))
    halo = pl.BlockSpec((HALO, p_len), lambda i: (jnp.maximum((n_t - 1 - i) * (ts // HALO) - 1, 0), 0))

    def whole(arr):
        return pl.BlockSpec(arr.shape, lambda i: (0,) * arr.ndim)

    return _call(
        body,
        name=name,
        grid=(n_t,),
        in_specs=[tok, tok, pl.BlockSpec((ts, p_len), lambda i: (n_t - 1 - i, 0)), halo,
                  pl.BlockSpec((ts, d_mix - dc), lambda i: (n_t - 1 - i, 0)),
                  whole(g), _resident(w_in), whole(conv_w), whole(pool_w), whole(pool_scale), _resident(w_out)],
        out_specs=[tok, pl.BlockSpec((ts, p_len), lambda i: (n_t - 1 - i, 0)),
                   whole(w_out),
                   whole(g), whole(conv_w), whole(pool_w), whole(pool_scale), tok],
        out_shape=[
            jax.ShapeDtypeStruct((s_len, d), F32),
            jax.ShapeDtypeStruct((s_len, p_len), LOW),
            jax.ShapeDtypeStruct((d_mix, d), LOW),
            jax.ShapeDtypeStruct(g.shape, F32),
            jax.ShapeDtypeStruct(conv_w.shape, F32),
            jax.ShapeDtypeStruct(pool_w.shape, F32),
            jax.ShapeDtypeStruct(pool_scale.shape, F32),
            jax.ShapeDtypeStruct((s_len, d), LOW),
        ],
        scratch_shapes=[pltpu.VMEM((ts + HALO, p_len), F32), pltpu.VMEM((ts + 2 * HALO, d_mix), F32),
                        pltpu.VMEM((ts, d_mix), LOW), pltpu.VMEM((d_mix, d), F32),
                        pltpu.VMEM((ts + 2 * HALO, POOL_GC), F32), pltpu.VMEM((ts + 2 * HALO, POOL_GC), F32)],
        args=(dx2, x, proj, proj, pooled, g, w_in, conv_w, pool_w, pool_scale, w_out),
        exchange=exchange,
    )


def _ffn_fwd_loss(x, h, wgt, wut, wd, g, target, name):
    s_len, d = x.shape
    fc = wd.shape[0]
    ts = min(TOK_TILE, s_len)

    def body(x_ref, h_ref, wg_ref, wu_ref, wd_ref, g_ref, tgt_ref,
             a_ref, b_ref, loss_ref, dx_ref, dg_ref, df_ref, s_ref):
        @pl.when(pl.program_id(0) == 0)
        def _():
            loss_ref[...] = jnp.zeros_like(loss_ref)
            dg_ref[...] = jnp.zeros_like(dg_ref)

        hb = h_ref[...]
        for c0, c1 in _slabs(fc):
            a = _dot_nt(hb, wg_ref[c0:c1, :])
            b = _dot_nt(hb, wu_ref[c0:c1, :])
            s_ref[:, c0:c1] = (a * jax.nn.sigmoid(a) * b).astype(LOW)
            a_ref[:, c0:c1] = a.astype(LOW)
            b_ref[:, c0:c1] = b.astype(LOW)
        xv = x_ref[...] + 0.5 * _dot(s_ref[...], wd_ref[...])
        gv = g_ref[...]
        err = xv * _rms_scale(xv) * gv - tgt_ref[...]
        loss_ref[...] += 0.5 * jnp.sum(jnp.mean(err * err, axis=-1, keepdims=True), axis=0, keepdims=True)
        dx, dgp = _rms_bwd(err * (1.0 / d), xv, gv)
        dx_ref[...] = dx
        df_ref[...] = (0.5 * dx).astype(LOW)
        dg_ref[...] += dgp

    tok = pl.BlockSpec((ts, d), lambda t: (t, 0))
    vec = pl.BlockSpec((1, d), lambda t: (0, 0))
    hid = pl.BlockSpec((ts, fc), lambda t: (t, 0))
    return pl.pallas_call(
        body,
        name=name,
        grid=(s_len // ts,),
        in_specs=[tok, tok, _resident(wgt), _resident(wut), _resident(wd), vec, tok],
        out_specs=[hid, hid, pl.BlockSpec((1, 128), lambda t: (0, 0)), tok, vec, tok],
        out_shape=[
            jax.ShapeDtypeStruct((s_len, fc), LOW),
            jax.ShapeDtypeStruct((s_len, fc), LOW),
            jax.ShapeDtypeStruct((1, 128), F32),
            jax.ShapeDtypeStruct((s_len, d), F32),
            jax.ShapeDtypeStruct((1, d), F32),
            jax.ShapeDtypeStruct((s_len, d), LOW),
        ],
        scratch_shapes=[pltpu.VMEM((ts, fc), LOW)],
        compiler_params=_params("arbitrary"),
    )(x, h, wgt, wut, wd, g, target)


def _row_tile(rows, cols, stack_bytes):
    budget = 20 * 1024 * 1024
    per_row = cols * (4 * 7 + stack_bytes)
    for tr in (rows, 512, 256, 176, 128, 64, 32, 16, 8):
        if rows % tr == 0 and tr % 8 == 0 and tr * per_row * 2 <= budget:
            return tr
    return rows


def _sum_stack(stack, name):
    n, r, c = stack.shape
    tr = _row_tile(r, c, n * stack.dtype.itemsize)

    def body(s_ref, o_ref):
        acc = s_ref[0].astype(F32)
        for k in range(1, n):
            acc = acc + s_ref[k].astype(F32)
        o_ref[...] = acc

    return pl.pallas_call(
        body,
        name=name,
        grid=(r // tr,),
        in_specs=[pl.BlockSpec((n, tr, c), lambda i: (0, i, 0))],
        out_specs=pl.BlockSpec((tr, c), lambda i: (i, 0)),
        out_shape=jax.ShapeDtypeStruct((r, c), F32),
        compiler_params=_params("arbitrary"),
    )(stack)


def _adamw_many(params, name):
    params = [(list(st) if isinstance(st, (list, tuple)) else [st], w, m, v) for st, w, m, v in params]
    stacks0, w0 = params[0][0], params[0][1]
    r, c = w0.shape
    n = stacks0[0].shape[0]
    n_st = len(stacks0)
    part_rows = [st.shape[1] for st in stacks0]
    assert sum(part_rows) == r
    assert all(w.shape == (r, c) and [st.shape for st in sts] == [st.shape for st in stacks0] for sts, w, _, _ in params)
    first_row = [sum(part_rows[:j]) for j in range(n_st)]
    tc = next(t for t in (512, 256, 128) if c % t == 0)
    c1 = 1.0 - ADAM_B1 ** ADAM_STEP
    c2 = 1.0 - ADAM_B2 ** ADAM_STEP
    n_in = n_st + 3

    def body(*refs):
        ins, outs = refs[:n_in * len(params)], refs[n_in * len(params):]
        for p in range(len(params)):
            s_refs = ins[n_in * p:n_in * p + n_st]
            w_ref, m_ref, v_ref = ins[n_in * p + n_st:n_in * (p + 1)]
            g_ref, d_ref, mo_ref, vo_ref = outs[4 * p:4 * p + 4]
            for s_ref, r0, nr in zip(s_refs, first_row, part_rows):
                gv = s_ref[0].astype(F32)
                for k in range(1, n):
                    gv = gv + s_ref[k].astype(F32)
                mn = ADAM_B1 * m_ref[r0:r0 + nr, :] + (1.0 - ADAM_B1) * gv
                vn = ADAM_B2 * v_ref[r0:r0 + nr, :] + (1.0 - ADAM_B2) * (gv * gv)
                g_ref[r0:r0 + nr, :] = gv
                mo_ref[r0:r0 + nr, :] = mn
                vo_ref[r0:r0 + nr, :] = vn
                d_ref[r0:r0 + nr, :] = -ADAM_LR * ((mn / c1) / (jnp.sqrt(vn / c2) + ADAM_EPS)
                                                   + ADAM_WD * w_ref[r0:r0 + nr, :])

    blk = pl.BlockSpec((r, tc), lambda i: (0, i))
    one_in = [pl.BlockSpec((n, nr, tc), lambda i: (0, 0, i)) for nr in part_rows] + [blk, blk, blk]
    res = pl.pallas_call(
        body,
        name=name,
        grid=(c // tc,),
        in_specs=one_in * len(params),
        out_specs=[blk] * (4 * len(params)),
        out_shape=[jax.ShapeDtypeStruct((r, c), F32)] * (4 * len(params)),
        compiler_params=_params("arbitrary"),
    )(*[arr for sts, w, m, v in params for arr in (*sts, w, m, v)])
    return [tuple(res[4 * p:4 * p + 4]) for p in range(len(params))]


def _adamw(stacks, w, m, v, name):
    return _adamw_many([(stacks, w, m, v)], name)[0]


def _to_sheet(parts):
    sheets, spans = [], []
    row = 0
    for p in parts:
        flat = p.reshape(-1).astype(F32)
        rows = -(-flat.shape[0] // 1024) * 8
        flat = jnp.pad(flat, (0, rows * 128 - flat.shape[0]))
        sheets.append(flat.reshape(rows, 128))
        spans.append((row, p.size, p.shape))
        row += rows
    return jnp.concatenate(sheets, axis=0), spans


def _from_sheet(sheet, spans):
    out = []
    for row, size, shape in spans:
        rows = -(-size // 1024) * 8
        out.append(sheet[row:row + rows].reshape(-1)[:size].reshape(shape))
    return out


def kernel(x, norm_ffn1, ffn1_w_gate, ffn1_w_up, ffn1_w_down, norm_mix, w_in, conv_w, pool_w, pool_scale, w_out, norm_ffn2, ffn2_w_gate, ffn2_w_up, ffn2_w_down, norm_final, loss_target, m_norm_ffn1, m_ffn1_w_gate, m_ffn1_w_up, m_ffn1_w_down, m_norm_mix, m_w_in, m_conv_w, m_pool_w, m_pool_scale, m_w_out, m_norm_ffn2, m_ffn2_w_gate, m_ffn2_w_up, m_ffn2_w_down, m_norm_final, v_norm_ffn1, v_ffn1_w_gate, v_ffn1_w_up, v_ffn1_w_down, v_norm_mix, v_w_in, v_conv_w, v_pool_w, v_pool_scale, v_w_out, v_norm_ffn2, v_ffn2_w_gate, v_ffn2_w_up, v_ffn2_w_down, v_norm_final):
    me = 4 * lax.axis_index("x") + 2 * lax.axis_index("y") + lax.axis_index("c")
    xs, tgt = x[0], loss_target[0]
    s_len, d = xs.shape
    f_shard = ffn1_w_down.shape[1]
    conv_shard = conv_w.shape[2]

    def low_t(wt):
        return wt[0].T.astype(LOW)

    def by_dev(gw):
        return gw.reshape(N_DEV, -1, d)

    conv_tile = jnp.zeros((8, 128), F32).at[0:conv_w.shape[1], 0:conv_shard].set(conv_w[0])
    pool_w_low = pool_w[0].astype(LOW)

    rows_a = -(-f_shard // 64) * 32

    def parts_of(w_gate, w_up, w_down):
        shards = [low_t(w_gate), low_t(w_up), w_down[0].astype(LOW)]
        return [(s, 0, rows_a) for s in shards], [(s, rows_a, f_shard - rows_a) for s in shards]

    def rows_flat(stacks):
        return [st.reshape(-1, d) for st in stacks]

    def gather(shards):
        return _Exchange(shards, [False] * len(shards), relay_at=0.6)

    def scatter(pairs):
        return _Exchange(pairs, [True] * len(pairs), among_chips=True)

    w1a_shards, w1b_shards = parts_of(ffn1_w_gate, ffn1_w_up, ffn1_w_down)
    w2a_shards, w2b_shards = parts_of(ffn2_w_gate, ffn2_w_up, ffn2_w_down)

    wg1a, wu1a, wd1a = rows_flat(_exchange(w1a_shards, [False] * 3, "gather_ffn1_a", relay=True))
    xa, h1, a1a, b1a, *w1b = _ffn_fwd(xs, norm_ffn1, None, wg1a, wu1a, wd1a, "ffn1_fwd_a",
                                      exchange=gather(w1b_shards))
    wg1b, wu1b, wd1b = rows_flat(w1b)
    x1, a1b, b1b, w_in_full, w_out_full, conv_tiles = _ffn_fwd(
        xa, None, h1, wg1b, wu1b, wd1b, "ffn1_fwd_b",
        exchange=gather([w_in[0].astype(LOW), w_out[0].astype(LOW), conv_tile]))
    w_out_full = w_out_full.reshape(-1, d)
    conv_full = jnp.concatenate([conv_tiles[k, 0:conv_w.shape[1], 0:conv_shard] for k in range(N_DEV)], axis=1)
    x2, h2, proj, pooled, *w2a = _mix_fwd(x1, norm_mix, w_in_full, conv_full, pool_w_low, pool_scale, w_out_full,
                                          "mix_fwd", exchange=gather(w2a_shards))
    wg2a, wu2a, wd2a = rows_flat(w2a)
    xb, h3, a2a, b2a, *w2b = _ffn_fwd(x2, norm_ffn2, None, wg2a, wu2a, wd2a, "ffn2_fwd_a",
                                      exchange=gather(w2b_shards))
    wg2b, wu2b, wd2b = rows_flat(w2b)
    a2b, b2b, loss_row, dx3, dg_final, df3 = _ffn_fwd_loss(
        xb, h3, wg2b, wu2b, wd2b, norm_final.reshape(1, d), tgt, "ffn2_fwd_b_loss")

    da2a, db2a, dwd_a, dwg_a, dwu_a = _ffn_bwdw(df3, a2a, b2a, h3, wd2a, "ffn2_bwdw_a")
    da2b, db2b, dwd_b, dwg_b, dwu_b = _ffn_bwdw(df3, a2b, b2b, h3, wd2b, "ffn2_bwdw_b")
    pairs = _pair_sum([by_dev(dwg_a), by_dev(dwu_a), by_dev(dwd_a), by_dev(dwg_b), by_dev(dwu_b), by_dev(dwd_b)],
                      "pair_sum_ffn2")
    dx2, dg_ffn2, *got_2b = _ffn_dx(dx3, x2, norm_ffn2, [(da2a, db2a, wg2a, wu2a), (da2b, db2b, wg2b, wu2b)],
                                    "ffn2_dx", exchange=scatter(pairs[3:]))
    dx1, dproj, dw_out, dg_mix, dconv, dpool_w, dpool_scale, df1, *got_2a = _mix_bwd(
        dx2, x1, proj, pooled, norm_mix, w_in_full, conv_full, pool_w_low, pool_scale, w_out_full, "mix_bwd",
        exchange=scatter(pairs[:3]))
    small_parts = [dg_mix, dg_ffn2, dg_final, dconv, dpool_w, dpool_scale, loss_row]
    small_sheet, spans = _to_sheet(small_parts)
    dw_in, got_small = _wgrad_tn(h2, dproj, W_IN_SHARD, True, "w_in_wgrad",
                                 exchange=_Exchange([small_sheet], [False], relay_at=0.5))
    pairs = _pair_sum([dw_in, by_dev(dw_out)], "pair_sum_mix")
    da1a, db1a, dwd, dwg, dwu, got_in, got_out = _ffn_bwdw(
        df1, a1a, b1a, h1, wd1a, "ffn1_bwdw_a", exchange=scatter(pairs))
    pairs = _pair_sum([by_dev(dwg), by_dev(dwu), by_dev(dwd)], "pair_sum_ffn1_a")
    da1b, db1b, dwd, dwg, dwu, *got_1a = _ffn_bwdw(df1, a1b, b1b, h1, wd1b, "ffn1_bwdw_b", exchange=scatter(pairs))
    pairs = _pair_sum([by_dev(dwg), by_dev(dwu), by_dev(dwd)], "pair_sum_ffn1_b")
    dx0, dg_ffn1, *got_1b = _ffn_dx(dx1, xs, norm_ffn1, [(da1a, db1a, wg1a, wu1a), (da1b, db1b, wg1b, wu1b)],
                                    "ffn1_dx", exchange=scatter(pairs))
    (got_n1,) = _exchange([dg_ffn1.reshape(8, 128)], [False], "gather_dnorm_ffn1")

    outs = {}

    def update(name, stacks, w, m, v):
        outs[name] = _adamw(stacks, w[0], m[0], v[0], "adamw_" + name)

    def update_ffn(prefix, got_a, got_b, gate, up, down):
        res = _adamw_many(
            [([got_a[j], got_b[j]], *[(t[0].T if j < 2 else t[0]) for t in wmv]) for j, wmv in enumerate((gate, up, down))],
            "adamw_" + prefix)
        outs[prefix + "_w_gate"] = tuple(r.T for r in res[0])
        outs[prefix + "_w_up"] = tuple(r.T for r in res[1])
        outs[prefix + "_w_down"] = res[2]

    update_ffn("ffn1", got_1a, got_1b, (ffn1_w_gate, m_ffn1_w_gate, v_ffn1_w_gate),
               (ffn1_w_up, m_ffn1_w_up, v_ffn1_w_up), (ffn1_w_down, m_ffn1_w_down, v_ffn1_w_down))
    update_ffn("ffn2", got_2a, got_2b, (ffn2_w_gate, m_ffn2_w_gate, v_ffn2_w_gate),
               (ffn2_w_up, m_ffn2_w_up, v_ffn2_w_up), (ffn2_w_down, m_ffn2_w_down, v_ffn2_w_down))
    update("w_in", got_in, w_in, m_w_in, v_w_in)
    update("w_out", got_out, w_out, m_w_out, v_w_out)

    g_small = _from_sheet(_sum_stack(got_small, "sum_small"), spans)
    g_norm_ffn1 = _sum_stack(got_n1, "sum_dnorm_ffn1").reshape(norm_ffn1.shape)
    g_conv = lax.dynamic_slice_in_dim(g_small[3], me * conv_shard, conv_shard, axis=1)
    small_names = ["norm_ffn1", "norm_mix", "norm_ffn2", "norm_final", "conv_w", "pool_w", "pool_scale"]
    small_g = [g_norm_ffn1, g_small[0], g_small[1], g_small[2].reshape(norm_final.shape), g_conv[None],
               g_small[4][None], g_small[5]]
    small_w = [norm_ffn1, norm_mix, norm_ffn2, norm_final, conv_w, pool_w, pool_scale]
    small_m = [m_norm_ffn1, m_norm_mix, m_norm_ffn2, m_norm_final, m_conv_w, m_pool_w, m_pool_scale]
    small_v = [v_norm_ffn1, v_norm_mix, v_norm_ffn2, v_norm_final, v_conv_w, v_pool_w, v_pool_scale]
    g_sheet, spans_u = _to_sheet(small_g)
    w_sheet, _ = _to_sheet(small_w)
    m_sheet, _ = _to_sheet(small_m)
    v_sheet, _ = _to_sheet(small_v)
    upd = _adamw(g_sheet[None], w_sheet, m_sheet, v_sheet, "adamw_small")
    small_out = [_from_sheet(u, spans_u) for u in upd]
    for k, nm in enumerate(small_names):
        outs[nm] = tuple(small_out[j][k] for j in range(4))

    loss = g_small[6][0, 0]
    order = ["norm_ffn1", "ffn1_w_gate", "ffn1_w_up", "ffn1_w_down", "norm_mix", "w_in", "conv_w", "pool_w",
             "pool_scale", "w_out", "norm_ffn2", "ffn2_w_gate", "ffn2_w_up", "ffn2_w_down", "norm_final"]
    big = {"ffn1_w_gate", "ffn1_w_up", "ffn1_w_down", "w_in", "w_out", "ffn2_w_gate", "ffn2_w_up", "ffn2_w_down"}

    def leaf(nm, j):
        val = outs[nm][j]
        return val[None] if nm in big else val

    return (loss, dx0[None],
            *[leaf(nm, 0) for nm in order], *[leaf(nm, 1) for nm in order],
            *[leaf(nm, 2) for nm in order], *[leaf(nm, 3) for nm in order])
```

```python
import jax
import jax.numpy as jnp
from jax import lax
from jax.experimental import pallas as pl
from jax.experimental.pallas import tpu as pltpu

F32 = jnp.float32
LOW = jnp.bfloat16

N_DEV = 8
EPS = 1e-6
D_CONV = 512
POOL_WINDOWS = (2, 4, 8, 16)
POOL_GC = 128
HALO = 16
W_IN_SHARD = 256

ADAM_LR = 0.001
ADAM_B1 = 0.9
ADAM_B2 = 0.999
ADAM_EPS = 1e-08
ADAM_WD = 0.01
ADAM_STEP = 10

VMEM_LIMIT_BYTES = 56 * 1024 * 1024
TOK_TILE = 512
MIX_TOK_TILE = 512
WGRAD_TOK_TILE = 4096
WGRAD_ROW_CANDIDATES = (256, 128)
RELAY_LAST_LATER = 0.25
BWD_ROW_SLAB = 2048


def _params(*sem):
    return pltpu.CompilerParams(dimension_semantics=sem, vmem_limit_bytes=VMEM_LIMIT_BYTES)


def _resident(arr):
    return pl.BlockSpec(arr.shape, lambda *_: (0,) * arr.ndim, pipeline_mode=pl.Buffered(1))


def _pick(n, candidates):
    for c in candidates:
        if n % c == 0:
            return c
    raise ValueError(f"no tile in {candidates} divides {n}")


def _dot(a, b):
    return lax.dot_general(a, b, (((1,), (0,)), ((), ())), preferred_element_type=F32)


def _dot_nt(a, b):
    return lax.dot_general(a, b, (((1,), (1,)), ((), ())), preferred_element_type=F32)


def _dot_tn(a, b):
    return lax.dot_general(a, b, (((0,), (0,)), ((), ())), preferred_element_type=F32)


def _rms_scale(x):
    return lax.rsqrt(jnp.mean(x * x, axis=-1, keepdims=True) + EPS)


def _rms_bwd(dy, x, g):
    r = _rms_scale(x)
    xhat = x * r
    gdy = dy * g
    dx = r * (gdy - xhat * jnp.mean(gdy * xhat, axis=-1, keepdims=True))
    return dx, jnp.sum(dy * xhat, axis=0, keepdims=True)


COLLECTIVE_PAIR, COLLECTIVE_CHIPS, COLLECTIVE_RELAY = 0, 1, 2


def _handshake(peer_numbers):
    mx, my, mc = lax.axis_index("x"), lax.axis_index("y"), lax.axis_index("c")
    barrier = pltpu.get_barrier_semaphore()
    for m in peer_numbers:
        peer = (lax.rem(mx + ((m >> 2) & 1), 2), lax.rem(my + ((m >> 1) & 1), 2), lax.rem(mc + (m & 1), 2))
        pl.semaphore_signal(barrier, inc=1, device_id=peer, device_id_type=pl.DeviceIdType.MESH)
    pl.semaphore_wait(barrier, len(peer_numbers))


class _Exchange:
    CHIPS = (2, 4, 6)

    def __init__(self, arrays, sliced, relay_at=None, among_chips=False):
        assert relay_at is None or not any(sliced)
        assert not among_chips or (all(sliced) and relay_at is None)
        self.relay_at, self.among_chips = relay_at, among_chips
        self.peers = self.CHIPS if among_chips else ((1, 2, 4) if relay_at is not None else tuple(range(1, N_DEV)))
        self.collective_id = COLLECTIVE_CHIPS if among_chips else (COLLECTIVE_RELAY if relay_at is not None else None)
        self.rows = [arr[1:] if isinstance(arr, tuple) else None for arr in arrays]
        self.arrays = [arr[0] if isinstance(arr, tuple) else arr for arr in arrays]
        self.sliced, self.n = list(sliced), len(arrays)
        assert all(rg is None or not sl for rg, sl in zip(self.rows, sliced))
        self.block_shape = [arr.shape if rg is None else (rg[1],) + arr.shape[1:]
                            for arr, rg in zip(self.arrays, self.rows)]
        self.out_shape = [jax.ShapeDtypeStruct(shape if sl else (N_DEV,) + shape, arr.dtype)
                          for arr, shape, sl in zip(self.arrays, self.block_shape, sliced)]
        self.specs = [pl.BlockSpec(memory_space=pl.ANY)] * self.n
        self.scratch_shapes = [pltpu.SemaphoreType.DMA((self.n, N_DEV)),
                               pltpu.SemaphoreType.DMA((self.n, N_DEV)),
                               pltpu.SemaphoreType.DMA((self.n,))]

    HALF_VIA = ((4, 2, 5), (2, 4, 7))

    def _halves(self, a):
        rows = self.block_shape[a][0]
        if rows % 32:
            return ((0, rows), None)
        return ((0, rows // 2), (rows // 2, rows // 2))

    def _copies(self, ins, outs, sems):
        send_sems, recv_sems, local_sems = sems
        sliced = self.sliced
        mx, my, mc = lax.axis_index("x"), lax.axis_index("y"), lax.axis_index("c")
        me = 2 * mx + my if self.among_chips else 4 * mx + 2 * my + mc

        def peer(m):
            px = lax.rem(mx + ((m >> 2) & 1), 2)
            py = lax.rem(my + ((m >> 1) & 1), 2)
            pc = lax.rem(mc + (m & 1), 2)
            return (px, py, pc), (2 * px + py if self.among_chips else 4 * px + 2 * py + pc)

        def mine(a):
            return ins[a] if self.rows[a] is None else ins[a].at[pl.ds(*self.rows[a])]

        def remote(a, m, arriving):
            pid, pflat = peer(m)
            return pltpu.make_async_remote_copy(
                src_ref=ins[a].at[pflat] if sliced[a] else mine(a),
                dst_ref=outs[a].at[pflat if arriving else me],
                send_sem=send_sems.at[a, m - 1],
                recv_sem=recv_sems.at[a, m - 1],
                device_id=pid,
                device_id_type=pl.DeviceIdType.MESH,
            )

        def local(a):
            return pltpu.make_async_copy(ins[a].at[me] if sliced[a] else mine(a), outs[a].at[me], local_sems.at[a])

        def passed_on(a, m):
            _, origin = peer(m)
            sibling, _ = peer(1)
            return pltpu.make_async_remote_copy(
                src_ref=outs[a].at[origin],
                dst_ref=outs[a].at[origin],
                send_sem=send_sems.at[a, m],
                recv_sem=recv_sems.at[a, m],
                device_id=sibling,
                device_id_type=pl.DeviceIdType.MESH,
            )

        def half_on(a, h, arriving):
            via, to, column = self.HALF_VIA[h]
            r0, nr = self._halves(a)[h]
            _, origin = peer(6 if arriving else via)
            rows = outs[a].at[origin].at[pl.ds(r0, nr)]
            return pltpu.make_async_remote_copy(
                src_ref=rows, dst_ref=rows, send_sem=send_sems.at[a, column], recv_sem=recv_sems.at[a, column],
                device_id=peer(to)[0], device_id_type=pl.DeviceIdType.MESH)

        return remote, local, passed_on, half_on

    def start(self, ins, outs, sems):
        remote, local, _, _ = self._copies(ins, outs, sems)
        if self.collective_id is not None:
            _handshake(self.peers)
        for a in range(self.n):
            local(a).start()
        for m in self.peers:
            for a in range(self.n):
                remote(a, m, False).start()

    def relay(self, ins, outs, sems):
        remote, _, passed_on, half_on = self._copies(ins, outs, sems)
        for h, (via, _, _) in enumerate(self.HALF_VIA):
            for a in range(self.n):
                remote(a, via, True).wait_recv()
                passed_on(a, via).start()
                if self._halves(a)[h] is not None:
                    half_on(a, h, False).start()

    def relay_last(self, ins, outs, sems):
        _, _, passed_on, half_on = self._copies(ins, outs, sems)
        for a in range(self.n):
            for h in range(2):
                if self._halves(a)[h] is not None:
                    half_on(a, h, True).wait_recv()
            passed_on(a, 6).start()

    def wait(self, ins, outs, sems):
        remote, local, passed_on, half_on = self._copies(ins, outs, sems)
        if self.relay_at is None:
            for m in self.peers:
                for a in range(self.n):
                    remote(a, m, True).wait_recv()
            for m in self.peers:
                for a in range(self.n):
                    remote(a, m, False).wait_send()
        else:
            for m in (1, 3, 5, 7):
                for a in range(self.n):
                    remote(a, m, True).wait_recv()
            for m in self.peers:
                for a in range(self.n):
                    remote(a, m, False).wait_send()
            for a in range(self.n):
                for m in self.CHIPS:
                    passed_on(a, m).wait_send()
                for h in range(2):
                    if self._halves(a)[h] is not None:
                        half_on(a, h, False).wait_send()
        for a in range(self.n):
            local(a).wait()


def _pair_sum(stacks, name):
    n = len(stacks)
    n_chip = N_DEV // 2
    half = [(n_chip,) + st.shape[1:] for st in stacks]

    def body(*refs):
        ins, outs, mine, theirs = refs[:n], refs[n:2 * n], refs[2 * n:3 * n], refs[3 * n:4 * n]
        local_sems, send_sems, recv_sems = refs[4 * n:]
        mx, my, mc = lax.axis_index("x"), lax.axis_index("y"), lax.axis_index("c")

        def own(a, k):
            return pltpu.make_async_copy(ins[a].at[2 * k + mc], mine[a].at[k], local_sems.at[a, k])

        def swap(a, k):
            return pltpu.make_async_remote_copy(
                src_ref=ins[a].at[2 * k + (1 - mc)], dst_ref=theirs[a].at[k],
                send_sem=send_sems.at[a, k], recv_sem=recv_sems.at[a, k],
                device_id=(mx, my, 1 - mc), device_id_type=pl.DeviceIdType.MESH)

        _handshake((1,))
        for k in range(n_chip):
            for a in range(n):
                own(a, k).start()
                swap(a, k).start()
        for k in range(n_chip):
            for a in range(n):
                own(a, k).wait()
                swap(a, k).wait()
                outs[a][k] = (mine[a][k].astype(F32) + theirs[a][k].astype(F32)).astype(LOW)

    return pl.pallas_call(
        body, name=name,
        out_shape=[jax.ShapeDtypeStruct(h, LOW) for h in half],
        in_specs=[pl.BlockSpec(memory_space=pl.ANY)] * n,
        out_specs=[pl.BlockSpec(memory_space=pltpu.VMEM)] * n,
        scratch_shapes=([pltpu.VMEM(h, st.dtype) for h, st in zip(half, stacks)] * 2
                        + [pltpu.SemaphoreType.DMA((n, n_chip))] * 3),
        compiler_params=pltpu.CompilerParams(vmem_limit_bytes=VMEM_LIMIT_BYTES, collective_id=COLLECTIVE_PAIR),
    )(*stacks)


def _exchange(arrays, sliced, name, relay=False):
    ex = _Exchange(arrays, sliced, relay_at=0 if relay else None)

    def body(*refs):
        ins, outs, sems = refs[:ex.n], refs[ex.n:2 * ex.n], refs[2 * ex.n:]
        ex.start(ins, outs, sems)
        if relay:
            ex.relay(ins, outs, sems)
            ex.relay_last(ins, outs, sems)
        ex.wait(ins, outs, sems)

    return pl.pallas_call(body, name=name, out_shape=ex.out_shape, in_specs=ex.specs, out_specs=ex.specs,
                          scratch_shapes=ex.scratch_shapes,
                          compiler_params=pltpu.CompilerParams(collective_id=ex.collective_id))(*ex.arrays)


def _call(body, *, name, grid, in_specs, out_specs, out_shape, args, scratch_shapes=(), exchange=None):
    params = _params(*(("arbitrary",) * len(grid)))
    if exchange is None:
        return pl.pallas_call(body, name=name, grid=grid, in_specs=in_specs, out_specs=out_specs, out_shape=out_shape,
                              scratch_shapes=list(scratch_shapes), compiler_params=params)(*args)
    exs = list(exchange) if isinstance(exchange, (list, tuple)) else [exchange]
    assert len(exs) == 1 or all(ex.collective_id is None for ex in exs)
    params = pltpu.CompilerParams(dimension_semantics=("arbitrary",) * len(grid), vmem_limit_bytes=VMEM_LIMIT_BYTES,
                                  collective_id=exs[0].collective_id)
    n_in, n_out, n_scr = len(in_specs), len(out_specs), len(scratch_shapes)
    n_ex = sum(ex.n for ex in exs)
    n_steps = 1
    for g in grid:
        n_steps *= g

    def hosted(*refs):
        ins, refs = refs[:n_in], refs[n_in:]
        ex_ins, refs = refs[:n_ex], refs[n_ex:]
        outs, refs = refs[:n_out], refs[n_out:]
        ex_outs, refs = refs[:n_ex], refs[n_ex:]
        scr, sems = refs[:n_scr], refs[n_scr:]
        parts, at = [], 0
        for j, ex in enumerate(exs):
            parts.append((ex_ins[at:at + ex.n], ex_outs[at:at + ex.n], sems[3 * j:3 * j + 3]))
            at += ex.n
        step = pl.program_id(0)
        for ax in range(1, len(grid)):
            step = step * grid[ax] + pl.program_id(ax)

        @pl.when(step == 0)
        def _():
            for ex, part in zip(exs, parts):
                ex.start(*part)

        body(*ins, *outs, *scr)

        for ex, part in zip(exs, parts):
            if ex.relay_at is not None:
                @pl.when(step == min(int(ex.relay_at * n_steps), n_steps - 1))
                def _():
                    ex.relay(*part)

                @pl.when(step == min(int((ex.relay_at + RELAY_LAST_LATER) * n_steps), n_steps - 1))
                def _():
                    ex.relay_last(*part)

        @pl.when(step == n_steps - 1)
        def _():
            for ex, part in zip(exs, parts):
                ex.wait(*part)

    return pl.pallas_call(
        hosted, name=name, grid=grid,
        in_specs=list(in_specs) + [sp for ex in exs for sp in ex.specs],
        out_specs=list(out_specs) + [sp for ex in exs for sp in ex.specs],
        out_shape=list(out_shape) + [sh for ex in exs for sh in ex.out_shape],
        scratch_shapes=list(scratch_shapes) + [sc for ex in exs for sc in ex.scratch_shapes],
        compiler_params=params)(*args, *[arr for ex in exs for arr in ex.arrays])


def _ffn_fwd(x, g, h, wgt, wut, wd, name, exchange=None):
    s_len, d = x.shape
    fc = wd.shape[0]
    ts = min(TOK_TILE, s_len)
    first = h is None

    def body(*refs):
        x_ref, gh_ref, wg_ref, wu_ref, wd_ref, xo_ref = refs[:6]
        a_ref, b_ref, s_ref = refs[-3:]
        xv = x_ref[...]
        if first:
            hb = (xv * _rms_scale(xv) * gh_ref[...]).astype(LOW)
            refs[6][...] = hb
        else:
            hb = gh_ref[...]
        for c0, c1 in _slabs(fc):
            a = _dot_nt(hb, wg_ref[c0:c1, :])
            b = _dot_nt(hb, wu_ref[c0:c1, :])
            s_ref[:, c0:c1] = (a * jax.nn.sigmoid(a) * b).astype(LOW)
            a_ref[:, c0:c1] = a.astype(LOW)
            b_ref[:, c0:c1] = b.astype(LOW)
        xo_ref[...] = xv + 0.5 * _dot(s_ref[...], wd_ref[...])

    tok = pl.BlockSpec((ts, d), lambda t: (t, 0))
    hid = pl.BlockSpec((ts, fc), lambda t: (t, 0))
    tok_out = jax.ShapeDtypeStruct((s_len, d), F32)
    h_out = jax.ShapeDtypeStruct((s_len, d), LOW)
    hid_out = jax.ShapeDtypeStruct((s_len, fc), LOW)
    return _call(
        body,
        name=name,
        grid=(s_len // ts,),
        in_specs=[tok, pl.BlockSpec((1, d), lambda t: (0, 0)) if first else tok,
                  _resident(wgt), _resident(wut), _resident(wd)],
        out_specs=[tok] + ([tok] if first else []) + [hid, hid],
        out_shape=[tok_out] + ([h_out] if first else []) + [hid_out, hid_out],
        scratch_shapes=[pltpu.VMEM((ts, fc), LOW)],
        args=(x, g if first else h, wgt, wut, wd),
        exchange=exchange,
    )


def _slabs(width, slab=256):
    return [(c0, min(c0 + slab, width)) for c0 in range(0, width, slab)]


def _ffn_bwdw(df, a, b, h, wd, name, exchange=None):
    s_len, d = df.shape
    f_len = wd.shape[0]
    tm = _pick(f_len, WGRAD_ROW_CANDIDATES)
    tk = min(WGRAD_TOK_TILE, s_len)
    n_k = s_len // tk

    def body(df_ref, a_ref, b_ref, h_ref, wd_ref, da_ref, db_ref, dwd_ref, dwg_ref, dwu_ref,
             s_ref, acc_d, acc_g, acc_u):
        k = pl.program_id(1)

        @pl.when(k == 0)
        def _():
            acc_d[...] = jnp.zeros_like(acc_d)
            acc_g[...] = jnp.zeros_like(acc_g)
            acc_u[...] = jnp.zeros_like(acc_u)

        wdv = wd_ref[...]
        for r0, r1 in _slabs(tk, BWD_ROW_SLAB):
            ds = _dot_nt(df_ref[r0:r1, :], wdv)
            av = a_ref[r0:r1, :].astype(F32)
            bv = b_ref[r0:r1, :].astype(F32)
            sig = jax.nn.sigmoid(av)
            silu = av * sig
            s_ref[r0:r1, :] = (silu * bv).astype(LOW)
            da_ref[r0:r1, :] = (ds * bv * (sig * (1.0 + av * (1.0 - sig)))).astype(LOW)
            db_ref[r0:r1, :] = (ds * silu).astype(LOW)
            hv = h_ref[r0:r1, :]
            acc_d[...] += _dot_tn(s_ref[r0:r1, :], df_ref[r0:r1, :])
            acc_g[...] += _dot_tn(da_ref[r0:r1, :], hv)
            acc_u[...] += _dot_tn(db_ref[r0:r1, :], hv)

        @pl.when(k == n_k - 1)
        def _():
            dwd_ref[...] = acc_d[...].astype(LOW)
            dwg_ref[...] = acc_g[...].astype(LOW)
            dwu_ref[...] = acc_u[...].astype(LOW)

    hid = pl.BlockSpec((tk, tm), lambda i, k: (k, i))
    tok = pl.BlockSpec((tk, d), lambda i, k: (k, 0))
    wrow = pl.BlockSpec((tm, d), lambda i, k: (i, 0))
    return _call(
        body,
        name=name,
        grid=(f_len // tm, n_k),
        in_specs=[tok, hid, hid, tok, wrow],
        out_specs=[hid, hid, wrow, wrow, wrow],
        out_shape=[jax.ShapeDtypeStruct((s_len, f_len), LOW)] * 2 + [jax.ShapeDtypeStruct((f_len, d), LOW)] * 3,
        scratch_shapes=[pltpu.VMEM((tk, tm), LOW)] + [pltpu.VMEM((tm, d), F32)] * 3,
        args=(df, a, b, h, wd),
        exchange=exchange,
    )


def _ffn_dx(dxo, x, g, parts, name, exchange=None):
    s_len, d = x.shape
    ts = min(TOK_TILE, s_len)
    n_p = len(parts)

    def body(dxo_ref, x_ref, g_ref, *refs):
        dxi_ref, dg_ref = refs[4 * n_p:]

        @pl.when(pl.program_id(0) == 0)
        def _():
            dg_ref[...] = jnp.zeros_like(dg_ref)

        dh = None
        for p in range(n_p):
            da_ref, db_ref, wg_ref, wu_ref = refs[4 * p:4 * p + 4]
            part = _dot(da_ref[...], wg_ref[...]) + _dot(db_ref[...], wu_ref[...])
            dh = part if dh is None else dh + part
        dx, dgp = _rms_bwd(dh, x_ref[...], g_ref[...])
        dxi_ref[...] = dxo_ref[...] + dx
        dg_ref[...] += dgp

    tok = pl.BlockSpec((ts, d), lambda t: (t, 0))
    vec = pl.BlockSpec((1, d), lambda t: (0, 0))
    part_specs, part_args = [], []
    for da, db, wgt, wut in parts:
        hid = pl.BlockSpec((ts, da.shape[1]), lambda t: (t, 0))
        part_specs += [hid, hid, _resident(wgt), _resident(wut)]
        part_args += [da, db, wgt, wut]
    return _call(
        body,
        name=name,
        grid=(s_len // ts,),
        in_specs=[tok, tok, vec] + part_specs,
        out_specs=[tok, vec],
        out_shape=[jax.ShapeDtypeStruct((s_len, d), F32), jax.ShapeDtypeStruct((1, d), F32)],
        args=(dxo, x, g, *part_args),
        exchange=exchange,
    )


def _wgrad_tn(xm, ym, tn, stacked, name, exchange=None):
    s_len, m = xm.shape
    n = ym.shape[1]
    tk = min(WGRAD_TOK_TILE, s_len)
    n_k = s_len // tk

    def body(x_ref, y_ref, o_ref, acc):
        k = pl.program_id(1)

        @pl.when(k == 0)
        def _():
            acc[...] = jnp.zeros_like(acc)

        acc[...] += _dot_tn(x_ref[...].astype(LOW), y_ref[...].astype(LOW))

        @pl.when(k == n_k - 1)
        def _():
            o_ref[...] = acc[...].astype(LOW)

    if stacked:
        out_spec = pl.BlockSpec((None, m, tn), lambda j, k: (j, 0, 0))
        out_shape = jax.ShapeDtypeStruct((n // tn, m, tn), LOW)
    else:
        out_spec = pl.BlockSpec((m, tn), lambda j, k: (0, j))
        out_shape = jax.ShapeDtypeStruct((m, n), LOW)
    return _call(
        body,
        name=name,
        grid=(n // tn, n_k),
        in_specs=[pl.BlockSpec((tk, m), lambda j, k: (k, 0)), pl.BlockSpec((tk, tn), lambda j, k: (k, j))],
        out_specs=[out_spec],
        out_shape=[out_shape],
        scratch_shapes=[pltpu.VMEM((m, tn), F32)],
        args=(xm, ym),
        exchange=exchange,
    )


def _mix_parts(ext_ref, cw, ts, row0, with_pooled=True):
    dc = D_CONV

    def back(off, c0, c1):
        return ext_ref[HALO - off:HALO - off + ts, c0:c1]

    v, gate_b, gate_c = back(0, 0, dc), back(0, dc, 2 * dc), back(0, 2 * dc, 3 * dc)
    z0 = gate_c * v
    z1 = back(1, 2 * dc, 3 * dc) * back(1, 0, dc)
    z2 = back(2, 2 * dc, 3 * dc) * back(2, 0, dc)
    conv = cw[2:3, :] * z0 + cw[1:2, :] * z1 + cw[0:1, :] * z2
    rows = row0 + lax.broadcasted_iota(jnp.int32, (ts, 1), 0)
    pooled, inv_count = [], []
    for grp, w in enumerate(POOL_WINDOWS):
        inv = 1.0 / jnp.minimum(rows + 1, w).astype(F32)
        inv_count.append(inv)
        if with_pooled:
            c0 = 3 * dc + POOL_GC * grp
            u = back(0, c0, c0 + POOL_GC)
            acc = u
            for j in range(1, w):
                acc = acc + back(j, c0, c0 + POOL_GC)
            pooled.append(acc * inv - u)
    return v, gate_b, gate_c, z0, z1, z2, conv, pooled, inv_count


def _mix_fwd(x, g, w_in, conv_w, pool_w, pool_scale, w_out, name, exchange=None):
    s_len, d = x.shape
    n_blk, _, wcols = w_in.shape
    p_len = n_blk * wcols
    d_mix = w_out.shape[0]
    ts = min(MIX_TOK_TILE, s_len)
    dc = D_CONV

    def body(x_ref, g_ref, win_ref, cw_ref, pw_ref, ps_ref, wout_ref, x2_ref, h_ref, proj_ref, pooled_ref,
             ext_ref, cat_ref):
        t = pl.program_id(0)

        @pl.when(t == 0)
        def _():
            ext_ref[0:HALO, :] = jnp.zeros((HALO, p_len), F32)

        xv = x_ref[...]
        hb = (xv * _rms_scale(xv) * g_ref[...]).astype(LOW)
        h_ref[...] = hb
        for k in range(n_blk):
            ext_ref[HALO:HALO + ts, wcols * k:wcols * (k + 1)] = _dot(hb, win_ref[k])
        proj_ref[...] = ext_ref[HALO:HALO + ts, :]

        _, gate_b, _, _, _, _, conv, pooled, _ = _mix_parts(ext_ref, cw_ref[...], ts, t * ts)
        cat_ref[:, 0:dc] = (gate_b * conv).astype(LOW)
        for grp in range(len(POOL_WINDOWS)):
            c0 = POOL_GC * grp
            pooled_b = pooled[grp].astype(LOW)
            pooled_ref[:, c0:c0 + POOL_GC] = pooled_b
            lin = _dot(pooled_b, pw_ref[grp])
            cat_ref[:, dc + c0:dc + c0 + POOL_GC] = (lin * ps_ref[:, c0:c0 + POOL_GC]).astype(LOW)
        x2_ref[...] = xv + _dot(cat_ref[...], wout_ref[...])
        ext_ref[0:HALO, :] = ext_ref[ts:ts + HALO, :]

    tok = pl.BlockSpec((ts, d), lambda t: (t, 0))

    def whole(arr):
        return pl.BlockSpec(arr.shape, lambda t: (0,) * arr.ndim)

    return _call(
        body,
        name=name,
        grid=(s_len // ts,),
        in_specs=[tok, whole(g), _resident(w_in), whole(conv_w), whole(pool_w), whole(pool_scale), _resident(w_out)],
        out_specs=[tok, tok, pl.BlockSpec((ts, p_len), lambda t: (t, 0)),
                   pl.BlockSpec((ts, d_mix - dc), lambda t: (t, 0))],
        out_shape=[
            jax.ShapeDtypeStruct((s_len, d), F32),
            jax.ShapeDtypeStruct((s_len, d), LOW),
            jax.ShapeDtypeStruct((s_len, p_len), F32),
            jax.ShapeDtypeStruct((s_len, d_mix - dc), LOW),
        ],
        scratch_shapes=[pltpu.VMEM((ts + HALO, p_len), F32), pltpu.VMEM((ts, d_mix), LOW)],
        args=(x, g, w_in, conv_w, pool_w, pool_scale, w_out),
        exchange=exchange,
    )


def _mix_bwd(dx2, x, proj, pooled, g, w_in, conv_w, pool_w, pool_scale, w_out, name, exchange=None):
    s_len, d = x.shape
    n_blk, _, wcols = w_in.shape
    p_len = n_blk * wcols
    d_mix = w_out.shape[0]
    ts = min(MIX_TOK_TILE, s_len)
    n_t = s_len // ts
    dc = D_CONV
    n_grp = len(POOL_WINDOWS)

    def body(dx2_ref, x_ref, proj_ref, halo_ref, pooled_ref, g_ref, win_ref, cw_ref, pw_ref, ps_ref, wout_ref,
             dx_ref, dproj_ref, dwout_ref, dg_ref, dcw_ref, dpw_ref, dps_ref, df_ref,
             ext_ref, fut_ref, cat_ref, dwout_acc, win_a, win_b):
        i = pl.program_id(0)
        t = n_t - 1 - i

        @pl.when(i == 0)
        def _():
            win_a[...] = jnp.zeros_like(win_a)
            win_b[...] = jnp.zeros_like(win_b)
            dwout_acc[...] = jnp.zeros_like(dwout_acc)
            dg_ref[...] = jnp.zeros_like(dg_ref)
            dcw_ref[...] = jnp.zeros_like(dcw_ref)
            dpw_ref[...] = jnp.zeros_like(dpw_ref)
            dps_ref[...] = jnp.zeros_like(dps_ref)
            fut_ref[ts:ts + 2 * HALO, :] = jnp.zeros((2 * HALO, d_mix), F32)

        ext_ref[HALO:HALO + ts, :] = proj_ref[...]

        @pl.when(t == 0)
        def _():
            ext_ref[0:HALO, :] = jnp.zeros((HALO, p_len), F32)

        @pl.when(t > 0)
        def _():
            ext_ref[0:HALO, :] = halo_ref[...]

        cw = cw_ref[...]
        v, gate_b, gate_c, z0, z1, z2, conv, _, inv_count = _mix_parts(ext_ref, cw, ts, t * ts, with_pooled=False)
        dx2 = dx2_ref[...]
        dcat = _dot_nt(dx2.astype(LOW), wout_ref[...])

        dy_a = dcat[:, 0:dc]
        dconv = dy_a * gate_b
        fut_ref[0:ts, 0:dc] = dconv
        cat_ref[:, 0:dc] = (gate_b * conv).astype(LOW)
        dproj_ref[:, dc:2 * dc] = (dy_a * conv).astype(LOW)
        dcw_ref[2:3, :] += jnp.sum(dconv * z0, axis=0, keepdims=True)
        dcw_ref[1:2, :] += jnp.sum(dconv * z1, axis=0, keepdims=True)
        dcw_ref[0:1, :] += jnp.sum(dconv * z2, axis=0, keepdims=True)

        dpool = []
        for grp in range(n_grp):
            c0 = POOL_GC * grp
            pooled_b = pooled_ref[:, c0:c0 + POOL_GC]
            lin = _dot(pooled_b, pw_ref[grp])
            dy_b = dcat[:, dc + c0:dc + c0 + POOL_GC]
            scale = ps_ref[:, c0:c0 + POOL_GC]
            cat_ref[:, dc + c0:dc + c0 + POOL_GC] = (lin * scale).astype(LOW)
            dps_ref[:, c0:c0 + POOL_GC] += jnp.sum(dy_b * lin, axis=0, keepdims=True)
            dlin = (dy_b * scale).astype(LOW)
            dpw_ref[grp] += _dot_tn(pooled_b, dlin)
            dpool.append(_dot_nt(dlin, pw_ref[grp]))
            fut_ref[0:ts, dc + c0:dc + c0 + POOL_GC] = dpool[grp] * inv_count[grp]
        dwout_acc[...] += _dot_tn(cat_ref[...], dx2.astype(LOW))

        def ahead(off, c0, c1):
            return fut_ref[off:off + ts, c0:c1]

        dz = cw[2:3, :] * ahead(0, 0, dc) + cw[1:2, :] * ahead(1, 0, dc) + cw[0:1, :] * ahead(2, 0, dc)
        dproj_ref[:, 0:dc] = (dz * gate_c).astype(LOW)
        dproj_ref[:, 2 * dc:3 * dc] = (dz * v).astype(LOW)
        rows_ext = ts + HALO
        for grp, w in enumerate(POOL_WINDOWS):
            c0 = dc + POOL_GC * grp
            src, dst, other = fut_ref, win_a, win_b
            cols = slice(c0, c0 + POOL_GC)
            span = 1
            while span < w:
                dst[0:rows_ext, :] = src[0:rows_ext, cols] + src[span:span + rows_ext, cols]
                src, dst, other, cols = dst, other, dst, slice(0, POOL_GC)
                span *= 2
            dproj_ref[:, 2 * dc + c0:2 * dc + c0 + POOL_GC] = (src[0:ts, cols] - dpool[grp]).astype(LOW)

        dh = _dot_nt(dproj_ref[:, 0:wcols], win_ref[0])
        for k in range(1, n_blk):
            dh += _dot_nt(dproj_ref[:, wcols * k:wcols * (k + 1)], win_ref[k])
        dx, dgp = _rms_bwd(dh, x_ref[...], g_ref[...])
        dx = dx2 + dx
        dx_ref[...] = dx
        df_ref[...] = (0.5 * dx).astype(LOW)
        dg_ref[...] += dgp
        fut_ref[ts:ts + HALO, :] = fut_ref[0:HALO, :]

        @pl.when(i == n_t - 1)
        def _():
            dwout_ref[...] = dwout_acc[...].astype(LOW)

    tok = pl.BlockSpec((ts, d), lambda i: (n_t - 1 - i, 0))
    halo = pl.BlockSpec((HALO, p_len), lambda i: (jnp.maximum((n_t - 1 - i) * (ts // HALO) - 1, 0), 0))

    def whole(arr):
        return pl.BlockSpec(arr.shape, lambda i: (0,) * arr.ndim)

    return _call(
        body,
        name=name,
        grid=(n_t,),
        in_specs=[tok, tok, pl.BlockSpec((ts, p_len), lambda i: (n_t - 1 - i, 0)), halo,
                  pl.BlockSpec((ts, d_mix - dc), lambda i: (n_t - 1 - i, 0)),
                  whole(g), _resident(w_in), whole(conv_w), whole(pool_w), whole(pool_scale), _resident(w_out)],
        out_specs=[tok, pl.BlockSpec((ts, p_len), lambda i: (n_t - 1 - i, 0)),
                   whole(w_out),
                   whole(g), whole(conv_w), whole(pool_w), whole(pool_scale), tok],
        out_shape=[
            jax.ShapeDtypeStruct((s_len, d), F32),
            jax.ShapeDtypeStruct((s_len, p_len), LOW),
            jax.ShapeDtypeStruct((d_mix, d), LOW),
            jax.ShapeDtypeStruct(g.shape, F32),
            jax.ShapeDtypeStruct(conv_w.shape, F32),
            jax.ShapeDtypeStruct(pool_w.shape, F32),
            jax.ShapeDtypeStruct(pool_scale.shape, F32),
            jax.ShapeDtypeStruct((s_len, d), LOW),
        ],
        scratch_shapes=[pltpu.VMEM((ts + HALO, p_len), F32), pltpu.VMEM((ts + 2 * HALO, d_mix), F32),
                        pltpu.VMEM((ts, d_mix), LOW), pltpu.VMEM((d_mix, d), F32),
                        pltpu.VMEM((ts + 2 * HALO, POOL_GC), F32), pltpu.VMEM((ts + 2 * HALO, POOL_GC), F32)],
        args=(dx2, x, proj, proj, pooled, g, w_in, conv_w, pool_w, pool_scale, w_out),
        exchange=exchange,
    )


def _ffn_fwd_loss(x, h, wgt, wut, wd, g, target, name):
    s_len, d = x.shape
    fc = wd.shape[0]
    ts = min(TOK_TILE, s_len)

    def body(x_ref, h_ref, wg_ref, wu_ref, wd_ref, g_ref, tgt_ref,
             a_ref, b_ref, loss_ref, dx_ref, dg_ref, df_ref, s_ref):
        @pl.when(pl.program_id(0) == 0)
        def _():
            loss_ref[...] = jnp.zeros_like(loss_ref)
            dg_ref[...] = jnp.zeros_like(dg_ref)

        hb = h_ref[...]
        for c0, c1 in _slabs(fc):
            a = _dot_nt(hb, wg_ref[c0:c1, :])
            b = _dot_nt(hb, wu_ref[c0:c1, :])
            s_ref[:, c0:c1] = (a * jax.nn.sigmoid(a) * b).astype(LOW)
            a_ref[:, c0:c1] = a.astype(LOW)
            b_ref[:, c0:c1] = b.astype(LOW)
        xv = x_ref[...] + 0.5 * _dot(s_ref[...], wd_ref[...])
        gv = g_ref[...]
        err = xv * _rms_scale(xv) * gv - tgt_ref[...]
        loss_ref[...] += 0.5 * jnp.sum(jnp.mean(err * err, axis=-1, keepdims=True), axis=0, keepdims=True)
        dx, dgp = _rms_bwd(err * (1.0 / d), xv, gv)
        dx_ref[...] = dx
        df_ref[...] = (0.5 * dx).astype(LOW)
        dg_ref[...] += dgp

    tok = pl.BlockSpec((ts, d), lambda t: (t, 0))
    vec = pl.BlockSpec((1, d), lambda t: (0, 0))
    hid = pl.BlockSpec((ts, fc), lambda t: (t, 0))
    return pl.pallas_call(
        body,
        name=name,
        grid=(s_len // ts,),
        in_specs=[tok, tok, _resident(wgt), _resident(wut), _resident(wd), vec, tok],
        out_specs=[hid, hid, pl.BlockSpec((1, 128), lambda t: (0, 0)), tok, vec, tok],
        out_shape=[
            jax.ShapeDtypeStruct((s_len, fc), LOW),
            jax.ShapeDtypeStruct((s_len, fc), LOW),
            jax.ShapeDtypeStruct((1, 128), F32),
            jax.ShapeDtypeStruct((s_len, d), F32),
            jax.ShapeDtypeStruct((1, d), F32),
            jax.ShapeDtypeStruct((s_len, d), LOW),
        ],
        scratch_shapes=[pltpu.VMEM((ts, fc), LOW)],
        compiler_params=_params("arbitrary"),
    )(x, h, wgt, wut, wd, g, target)


def _row_tile(rows, cols, stack_bytes):
    budget = 20 * 1024 * 1024
    per_row = cols * (4 * 7 + stack_bytes)
    for tr in (rows, 512, 256, 176, 128, 64, 32, 16, 8):
        if rows % tr == 0 and tr % 8 == 0 and tr * per_row * 2 <= budget:
            return tr
    return rows


def _sum_stack(stack, name):
    n, r, c = stack.shape
    tr = _row_tile(r, c, n * stack.dtype.itemsize)

    def body(s_ref, o_ref):
        acc = s_ref[0].astype(F32)
        for k in range(1, n):
            acc = acc + s_ref[k].astype(F32)
        o_ref[...] = acc

    return pl.pallas_call(
        body,
        name=name,
        grid=(r // tr,),
        in_specs=[pl.BlockSpec((n, tr, c), lambda i: (0, i, 0))],
        out_specs=pl.BlockSpec((tr, c), lambda i: (i, 0)),
        out_shape=jax.ShapeDtypeStruct((r, c), F32),
        compiler_params=_params("arbitrary"),
    )(stack)


def _adamw_many(params, name):
    params = [(list(st) if isinstance(st, (list, tuple)) else [st], w, m, v) for st, w, m, v in params]
    stacks0, w0 = params[0][0], params[0][1]
    r, c = w0.shape
    n = stacks0[0].shape[0]
    n_st = len(stacks0)
    part_rows = [st.shape[1] for st in stacks0]
    assert sum(part_rows) == r
    assert all(w.shape == (r, c) and [st.shape for st in sts] == [st.shape for st in stacks0] for sts, w, _, _ in params)
    first_row = [sum(part_rows[:j]) for j in range(n_st)]
    tc = next((t for t in (512, 256, 128) if c % t == 0), c)
    c1 = 1.0 - ADAM_B1 ** ADAM_STEP
    c2 = 1.0 - ADAM_B2 ** ADAM_STEP
    n_in = n_st + 3

    def body(*refs):
        ins, outs = refs[:n_in * len(params)], refs[n_in * len(params):]
        for p in range(len(params)):
            s_refs = ins[n_in * p:n_in * p + n_st]
            w_ref, m_ref, v_ref = ins[n_in * p + n_st:n_in * (p + 1)]
            g_ref, d_ref, mo_ref, vo_ref = outs[4 * p:4 * p + 4]
            for s_ref, r0, nr in zip(s_refs, first_row, part_rows):
                gv = s_ref[0].astype(F32)
                for k in range(1, n):
                    gv = gv + s_ref[k].astype(F32)
                mn = ADAM_B1 * m_ref[r0:r0 + nr, :] + (1.0 - ADAM_B1) * gv
                vn = ADAM_B2 * v_ref[r0:r0 + nr, :] + (1.0 - ADAM_B2) * (gv * gv)
                g_ref[r0:r0 + nr, :] = gv
                mo_ref[r0:r0 + nr, :] = mn
                vo_ref[r0:r0 + nr, :] = vn
                d_ref[r0:r0 + nr, :] = -ADAM_LR * ((mn / c1) / (jnp.sqrt(vn / c2) + ADAM_EPS)
                                                   + ADAM_WD * w_ref[r0:r0 + nr, :])

    blk = pl.BlockSpec((r, tc), lambda i: (0, i))
    one_in = [pl.BlockSpec((n, nr, tc), lambda i: (0, 0, i)) for nr in part_rows] + [blk, blk, blk]
    res = pl.pallas_call(
        body,
        name=name,
        grid=(c // tc,),
        in_specs=one_in * len(params),
        out_specs=[blk] * (4 * len(params)),
        out_shape=[jax.ShapeDtypeStruct((r, c), F32)] * (4 * len(params)),
        compiler_params=_params("arbitrary"),
    )(*[arr for sts, w, m, v in params for arr in (*sts, w, m, v)])
    return [tuple(res[4 * p:4 * p + 4]) for p in range(len(params))]


def _adamw(stacks, w, m, v, name):
    return _adamw_many([(stacks, w, m, v)], name)[0]


def _to_sheet(parts):
    sheets, spans = [], []
    row = 0
    for p in parts:
        flat = p.reshape(-1).astype(F32)
        rows = -(-flat.shape[0] // 1024) * 8
        flat = jnp.pad(flat, (0, rows * 128 - flat.shape[0]))
        sheets.append(flat.reshape(rows, 128))
        spans.append((row, p.size, p.shape))
        row += rows
    return jnp.concatenate(sheets, axis=0), spans


def _from_sheet(sheet, spans):
    out = []
    for row, size, shape in spans:
        rows = -(-size // 1024) * 8
        out.append(sheet[row:row + rows].reshape(-1)[:size].reshape(shape))
    return out


def kernel(x, norm_ffn1, ffn1_w_gate, ffn1_w_up, ffn1_w_down, norm_mix, w_in, conv_w, pool_w, pool_scale, w_out, norm_ffn2, ffn2_w_gate, ffn2_w_up, ffn2_w_down, norm_final, loss_target, m_norm_ffn1, m_ffn1_w_gate, m_ffn1_w_up, m_ffn1_w_down, m_norm_mix, m_w_in, m_conv_w, m_pool_w, m_pool_scale, m_w_out, m_norm_ffn2, m_ffn2_w_gate, m_ffn2_w_up, m_ffn2_w_down, m_norm_final, v_norm_ffn1, v_ffn1_w_gate, v_ffn1_w_up, v_ffn1_w_down, v_norm_mix, v_w_in, v_conv_w, v_pool_w, v_pool_scale, v_w_out, v_norm_ffn2, v_ffn2_w_gate, v_ffn2_w_up, v_ffn2_w_down, v_norm_final):
    me = 4 * lax.axis_index("x") + 2 * lax.axis_index("y") + lax.axis_index("c")
    xs, tgt = x[0], loss_target[0]
    s_len, d = xs.shape
    f_shard = ffn1_w_down.shape[1]
    conv_shard = conv_w.shape[2]

    def low_t(wt):
        return wt[0].T.astype(LOW)

    def by_dev(gw):
        return gw.reshape(N_DEV, -1, d)

    conv_tile = jnp.zeros((8, 128), F32).at[0:conv_w.shape[1], 0:conv_shard].set(conv_w[0])
    pool_w_low = pool_w[0].astype(LOW)

    rows_a = -(-f_shard // 64) * 32

    def parts_of(w_gate, w_up, w_down):
        shards = [low_t(w_gate), low_t(w_up), w_down[0].astype(LOW)]
        return [(s, 0, rows_a) for s in shards], [(s, rows_a, f_shard - rows_a) for s in shards]

    def rows_flat(stacks):
        return [st.reshape(-1, d) for st in stacks]

    def gather(shards):
        return _Exchange(shards, [False] * len(shards), relay_at=0.6)

    def scatter(pairs):
        return _Exchange(pairs, [True] * len(pairs), among_chips=True)

    w1a_shards, w1b_shards = parts_of(ffn1_w_gate, ffn1_w_up, ffn1_w_down)
    w2a_shards, w2b_shards = parts_of(ffn2_w_gate, ffn2_w_up, ffn2_w_down)

    wg1a, wu1a, wd1a = rows_flat(_exchange(w1a_shards, [False] * 3, "gather_ffn1_a", relay=True))
    xa, h1, a1a, b1a, *w1b = _ffn_fwd(xs, norm_ffn1, None, wg1a, wu1a, wd1a, "ffn1_fwd_a",
                                      exchange=gather(w1b_shards))
    wg1b, wu1b, wd1b = rows_flat(w1b)
    x1, a1b, b1b, w_in_full, w_out_full, conv_tiles = _ffn_fwd(
        xa, None, h1, wg1b, wu1b, wd1b, "ffn1_fwd_b",
        exchange=gather([w_in[0].astype(LOW), w_out[0].astype(LOW), conv_tile]))
    w_out_full = w_out_full.reshape(-1, d)
    conv_full = jnp.concatenate([conv_tiles[k, 0:conv_w.shape[1], 0:conv_shard] for k in range(N_DEV)], axis=1)
    x2, h2, proj, pooled, *w2a = _mix_fwd(x1, norm_mix, w_in_full, conv_full, pool_w_low, pool_scale, w_out_full,
                                          "mix_fwd", exchange=gather(w2a_shards))
    wg2a, wu2a, wd2a = rows_flat(w2a)
    xb, h3, a2a, b2a, *w2b = _ffn_fwd(x2, norm_ffn2, None, wg2a, wu2a, wd2a, "ffn2_fwd_a",
                                      exchange=gather(w2b_shards))
    wg2b, wu2b, wd2b = rows_flat(w2b)
    a2b, b2b, loss_row, dx3, dg_final, df3 = _ffn_fwd_loss(
        xb, h3, wg2b, wu2b, wd2b, norm_final.reshape(1, d), tgt, "ffn2_fwd_b_loss")

    da2a, db2a, dwd_a, dwg_a, dwu_a = _ffn_bwdw(df3, a2a, b2a, h3, wd2a, "ffn2_bwdw_a")
    da2b, db2b, dwd_b, dwg_b, dwu_b = _ffn_bwdw(df3, a2b, b2b, h3, wd2b, "ffn2_bwdw_b")
    pairs = _pair_sum([by_dev(dwg_a), by_dev(dwu_a), by_dev(dwd_a), by_dev(dwg_b), by_dev(dwu_b), by_dev(dwd_b)],
                      "pair_sum_ffn2")
    dx2, dg_ffn2, *got_2b = _ffn_dx(dx3, x2, norm_ffn2, [(da2a, db2a, wg2a, wu2a), (da2b, db2b, wg2b, wu2b)],
                                    "ffn2_dx", exchange=scatter(pairs[3:]))
    dx1, dproj, dw_out, dg_mix, dconv, dpool_w, dpool_scale, df1, *got_2a = _mix_bwd(
        dx2, x1, proj, pooled, norm_mix, w_in_full, conv_full, pool_w_low, pool_scale, w_out_full, "mix_bwd",
        exchange=scatter(pairs[:3]))
    small_parts = [dg_mix, dg_ffn2, dg_final, dconv, dpool_w, dpool_scale, loss_row]
    small_sheet, spans = _to_sheet(small_parts)
    dw_in, got_small = _wgrad_tn(h2, dproj, W_IN_SHARD, True, "w_in_wgrad",
                                 exchange=_Exchange([small_sheet], [False], relay_at=0.5))
    pairs = _pair_sum([dw_in, by_dev(dw_out)], "pair_sum_mix")
    da1a, db1a, dwd, dwg, dwu, got_in, got_out = _ffn_bwdw(
        df1, a1a, b1a, h1, wd1a, "ffn1_bwdw_a", exchange=scatter(pairs))
    pairs = _pair_sum([by_dev(dwg), by_dev(dwu), by_dev(dwd)], "pair_sum_ffn1_a")
    da1b, db1b, dwd, dwg, dwu, *got_1a = _ffn_bwdw(df1, a1b, b1b, h1, wd1b, "ffn1_bwdw_b", exchange=scatter(pairs))
    pairs = _pair_sum([by_dev(dwg), by_dev(dwu), by_dev(dwd)], "pair_sum_ffn1_b")
    dx0, dg_ffn1, *got_1b = _ffn_dx(dx1, xs, norm_ffn1, [(da1a, db1a, wg1a, wu1a), (da1b, db1b, wg1b, wu1b)],
                                    "ffn1_dx", exchange=scatter(pairs))
    (got_n1,) = _exchange([dg_ffn1.reshape(8, 128)], [False], "gather_dnorm_ffn1")

    outs = {}

    def update(name, stacks, w, m, v):
        outs[name] = _adamw(stacks, w[0], m[0], v[0], "adamw_" + name)

    def update_ffn(prefix, got_a, got_b, gate, up, down):
        res = _adamw_many(
            [([got_a[j], got_b[j]], *[(t[0].T if j < 2 else t[0]) for t in wmv]) for j, wmv in enumerate((gate, up, down))],
            "adamw_" + prefix)
        outs[prefix + "_w_gate"] = tuple(r.T for r in res[0])
        outs[prefix + "_w_up"] = tuple(r.T for r in res[1])
        outs[prefix + "_w_down"] = res[2]

    update_ffn("ffn1", got_1a, got_1b, (ffn1_w_gate, m_ffn1_w_gate, v_ffn1_w_gate),
               (ffn1_w_up, m_ffn1_w_up, v_ffn1_w_up), (ffn1_w_down, m_ffn1_w_down, v_ffn1_w_down))
    update_ffn("ffn2", got_2a, got_2b, (ffn2_w_gate, m_ffn2_w_gate, v_ffn2_w_gate),
               (ffn2_w_up, m_ffn2_w_up, v_ffn2_w_up), (ffn2_w_down, m_ffn2_w_down, v_ffn2_w_down))
    update("w_in", got_in, w_in, m_w_in, v_w_in)
    update("w_out", got_out, w_out, m_w_out, v_w_out)

    g_small = _from_sheet(_sum_stack(got_small, "sum_small"), spans)
    g_norm_ffn1 = _sum_stack(got_n1, "sum_dnorm_ffn1").reshape(norm_ffn1.shape)
    g_conv = lax.dynamic_slice_in_dim(g_small[3], me * conv_shard, conv_shard, axis=1)
    row = (1, d)
    norms = [("norm_ffn1", g_norm_ffn1, norm_ffn1, m_norm_ffn1, v_norm_ffn1),
             ("norm_mix", g_small[0], norm_mix, m_norm_mix, v_norm_mix),
             ("norm_ffn2", g_small[1], norm_ffn2, m_norm_ffn2, v_norm_ffn2),
             ("norm_final", g_small[2], norm_final, m_norm_final, v_norm_final)]
    res = _adamw_many([(gn.reshape((1,) + row), wn.reshape(row), mn.reshape(row), vn.reshape(row))
                       for _, gn, wn, mn, vn in norms], "adamw_norms")
    for (nm, _, wn, _, _), r4 in zip(norms, res):
        outs[nm] = tuple(r.reshape(wn.shape) for r in r4)
    for nm, gn, wn, mn, vn in (("conv_w", g_conv, conv_w, m_conv_w, v_conv_w),
                               ("pool_w", g_small[4], pool_w, m_pool_w, v_pool_w),
                               ("pool_scale", g_small[5], pool_scale, m_pool_scale, v_pool_scale)):
        flat = (-1, wn.shape[-1])
        r4 = _adamw(gn.reshape(flat)[None], wn.reshape(flat), mn.reshape(flat), vn.reshape(flat), "adamw_" + nm)
        outs[nm] = tuple(r.reshape(wn.shape) for r in r4)

    loss = g_small[6][0, 0]
    order = ["norm_ffn1", "ffn1_w_gate", "ffn1_w_up", "ffn1_w_down", "norm_mix", "w_in", "conv_w", "pool_w",
             "pool_scale", "w_out", "norm_ffn2", "ffn2_w_gate", "ffn2_w_up", "ffn2_w_down", "norm_final"]
    big = {"ffn1_w_gate", "ffn1_w_up", "ffn1_w_down", "w_in", "w_out", "ffn2_w_gate", "ffn2_w_up", "ffn2_w_down"}

    def leaf(nm, j):
        val = outs[nm][j]
        return val[None] if nm in big else val

    return (loss, dx0[None],
            *[leaf(nm, 0) for nm in order], *[leaf(nm, 1) for nm in order],
            *[leaf(nm, 2) for nm in order], *[leaf(nm, 3) for nm in order])
```

```python
import jax
import jax.numpy as jnp
from jax import lax
from jax.experimental import pallas as pl
from jax.experimental.pallas import tpu as pltpu

F32 = jnp.float32
LOW = jnp.bfloat16

N_DEV = 8
EPS = 1e-6
D_CONV = 512
POOL_WINDOWS = (2, 4, 8, 16)
POOL_GC = 128
HALO = 16
W_IN_SHARD = 256

ADAM_LR = 0.001
ADAM_B1 = 0.9
ADAM_B2 = 0.999
ADAM_EPS = 1e-08
ADAM_WD = 0.01
ADAM_STEP = 10

VMEM_LIMIT_BYTES = 56 * 1024 * 1024
TOK_TILE = 512
MIX_TOK_TILE = 512
WGRAD_TOK_TILE = 4096
WGRAD_ROW_CANDIDATES = (256, 128)
RELAY_LAST_LATER = 0.25
BWD_ROW_SLAB = 2048


def _params(*sem):
    return pltpu.CompilerParams(dimension_semantics=sem, vmem_limit_bytes=VMEM_LIMIT_BYTES)


def _resident(arr):
    return pl.BlockSpec(arr.shape, lambda *_: (0,) * arr.ndim, pipeline_mode=pl.Buffered(1))


def _pick(n, candidates):
    for c in candidates:
        if n % c == 0:
            return c
    raise ValueError(f"no tile in {candidates} divides {n}")


def _dot(a, b):
    return lax.dot_general(a, b, (((1,), (0,)), ((), ())), preferred_element_type=F32)


def _dot_nt(a, b):
    return lax.dot_general(a, b, (((1,), (1,)), ((), ())), preferred_element_type=F32)


def _dot_tn(a, b):
    return lax.dot_general(a, b, (((0,), (0,)), ((), ())), preferred_element_type=F32)


def _rms_scale(x):
    return lax.rsqrt(jnp.mean(x * x, axis=-1, keepdims=True) + EPS)


def _rms_bwd(dy, x, g):
    r = _rms_scale(x)
    xhat = x * r
    gdy = dy * g
    dx = r * (gdy - xhat * jnp.mean(gdy * xhat, axis=-1, keepdims=True))
    return dx, jnp.sum(dy * xhat, axis=0, keepdims=True)


COLLECTIVE_PAIR, COLLECTIVE_CHIPS, COLLECTIVE_RELAY = 0, 1, 2


def _handshake(peer_numbers):
    mx, my, mc = lax.axis_index("x"), lax.axis_index("y"), lax.axis_index("c")
    barrier = pltpu.get_barrier_semaphore()
    for m in peer_numbers:
        peer = (lax.rem(mx + ((m >> 2) & 1), 2), lax.rem(my + ((m >> 1) & 1), 2), lax.rem(mc + (m & 1), 2))
        pl.semaphore_signal(barrier, inc=1, device_id=peer, device_id_type=pl.DeviceIdType.MESH)
    pl.semaphore_wait(barrier, len(peer_numbers))


class _Exchange:
    CHIPS = (2, 4, 6)

    def __init__(self, arrays, sliced, relay_at=None, among_chips=False):
        assert relay_at is None or not any(sliced)
        assert not among_chips or (all(sliced) and relay_at is None)
        self.relay_at, self.among_chips = relay_at, among_chips
        self.peers = self.CHIPS if among_chips else ((1, 2, 4) if relay_at is not None else tuple(range(1, N_DEV)))
        self.collective_id = COLLECTIVE_CHIPS if among_chips else (COLLECTIVE_RELAY if relay_at is not None else None)
        self.rows = [arr[1:] if isinstance(arr, tuple) else None for arr in arrays]
        self.arrays = [arr[0] if isinstance(arr, tuple) else arr for arr in arrays]
        self.sliced, self.n = list(sliced), len(arrays)
        assert all(rg is None or not sl for rg, sl in zip(self.rows, sliced))
        self.block_shape = [arr.shape if rg is None else (rg[1],) + arr.shape[1:]
                            for arr, rg in zip(self.arrays, self.rows)]
        self.out_shape = [jax.ShapeDtypeStruct(shape if sl else (N_DEV,) + shape, arr.dtype)
                          for arr, shape, sl in zip(self.arrays, self.block_shape, sliced)]
        self.specs = [pl.BlockSpec(memory_space=pl.ANY)] * self.n
        self.scratch_shapes = [pltpu.SemaphoreType.DMA((self.n, N_DEV)),
                               pltpu.SemaphoreType.DMA((self.n, N_DEV)),
                               pltpu.SemaphoreType.DMA((self.n,))]

    HALF_VIA = ((4, 2, 5), (2, 4, 7))

    def _halves(self, a):
        rows = self.block_shape[a][0]
        if rows % 32:
            return ((0, rows), None)
        return ((0, rows // 2), (rows // 2, rows // 2))

    def _copies(self, ins, outs, sems):
        send_sems, recv_sems, local_sems = sems
        sliced = self.sliced
        mx, my, mc = lax.axis_index("x"), lax.axis_index("y"), lax.axis_index("c")
        me = 2 * mx + my if self.among_chips else 4 * mx + 2 * my + mc

        def peer(m):
            px = lax.rem(mx + ((m >> 2) & 1), 2)
            py = lax.rem(my + ((m >> 1) & 1), 2)
            pc = lax.rem(mc + (m & 1), 2)
            return (px, py, pc), (2 * px + py if self.among_chips else 4 * px + 2 * py + pc)

        def mine(a):
            return ins[a] if self.rows[a] is None else ins[a].at[pl.ds(*self.rows[a])]

        def remote(a, m, arriving):
            pid, pflat = peer(m)
            return pltpu.make_async_remote_copy(
                src_ref=ins[a].at[pflat] if sliced[a] else mine(a),
                dst_ref=outs[a].at[pflat if arriving else me],
                send_sem=send_sems.at[a, m - 1],
                recv_sem=recv_sems.at[a, m - 1],
                device_id=pid,
                device_id_type=pl.DeviceIdType.MESH,
            )

        def local(a):
            return pltpu.make_async_copy(ins[a].at[me] if sliced[a] else mine(a), outs[a].at[me], local_sems.at[a])

        def passed_on(a, m):
            _, origin = peer(m)
            sibling, _ = peer(1)
            return pltpu.make_async_remote_copy(
                src_ref=outs[a].at[origin],
                dst_ref=outs[a].at[origin],
                send_sem=send_sems.at[a, m],
                recv_sem=recv_sems.at[a, m],
                device_id=sibling,
                device_id_type=pl.DeviceIdType.MESH,
            )

        def half_on(a, h, arriving):
            via, to, column = self.HALF_VIA[h]
            r0, nr = self._halves(a)[h]
            _, origin = peer(6 if arriving else via)
            rows = outs[a].at[origin].at[pl.ds(r0, nr)]
            return pltpu.make_async_remote_copy(
                src_ref=rows, dst_ref=rows, send_sem=send_sems.at[a, column], recv_sem=recv_sems.at[a, column],
                device_id=peer(to)[0], device_id_type=pl.DeviceIdType.MESH)

        return remote, local, passed_on, half_on

    def start(self, ins, outs, sems):
        remote, local, _, _ = self._copies(ins, outs, sems)
        if self.collective_id is not None:
            _handshake(self.peers)
        for a in range(self.n):
            local(a).start()
        for m in self.peers:
            for a in range(self.n):
                remote(a, m, False).start()

    def relay(self, ins, outs, sems):
        remote, _, passed_on, half_on = self._copies(ins, outs, sems)
        for h, (via, _, _) in enumerate(self.HALF_VIA):
            for a in range(self.n):
                remote(a, via, True).wait_recv()
                passed_on(a, via).start()
                if self._halves(a)[h] is not None:
                    half_on(a, h, False).start()

    def relay_last(self, ins, outs, sems):
        _, _, passed_on, half_on = self._copies(ins, outs, sems)
        for a in range(self.n):
            for h in range(2):
                if self._halves(a)[h] is not None:
                    half_on(a, h, True).wait_recv()
            passed_on(a, 6).start()

    def wait(self, ins, outs, sems):
        remote, local, passed_on, half_on = self._copies(ins, outs, sems)
        if self.relay_at is None:
            for m in self.peers:
                for a in range(self.n):
                    remote(a, m, True).wait_recv()
            for m in self.peers:
                for a in range(self.n):
                    remote(a, m, False).wait_send()
        else:
            for m in (1, 3, 5, 7):
                for a in range(self.n):
                    remote(a, m, True).wait_recv()
            for m in self.peers:
                for a in range(self.n):
                    remote(a, m, False).wait_send()
            for a in range(self.n):
                for m in self.CHIPS:
                    passed_on(a, m).wait_send()
                for h in range(2):
                    if self._halves(a)[h] is not None:
                        half_on(a, h, False).wait_send()
        for a in range(self.n):
            local(a).wait()


def _pair_sum(stacks, name):
    n = len(stacks)
    n_chip = N_DEV // 2
    half = [(n_chip,) + st.shape[1:] for st in stacks]

    def body(*refs):
        ins, outs, mine, theirs = refs[:n], refs[n:2 * n], refs[2 * n:3 * n], refs[3 * n:4 * n]
        local_sems, send_sems, recv_sems = refs[4 * n:]
        mx, my, mc = lax.axis_index("x"), lax.axis_index("y"), lax.axis_index("c")

        def own(a, k):
            return pltpu.make_async_copy(ins[a].at[2 * k + mc], mine[a].at[k], local_sems.at[a, k])

        def swap(a, k):
            return pltpu.make_async_remote_copy(
                src_ref=ins[a].at[2 * k + (1 - mc)], dst_ref=theirs[a].at[k],
                send_sem=send_sems.at[a, k], recv_sem=recv_sems.at[a, k],
                device_id=(mx, my, 1 - mc), device_id_type=pl.DeviceIdType.MESH)

        _handshake((1,))
        for k in range(n_chip):
            for a in range(n):
                own(a, k).start()
                swap(a, k).start()
        for k in range(n_chip):
            for a in range(n):
                own(a, k).wait()
                swap(a, k).wait()
                outs[a][k] = (mine[a][k].astype(F32) + theirs[a][k].astype(F32)).astype(LOW)

    return pl.pallas_call(
        body, name=name,
        out_shape=[jax.ShapeDtypeStruct(h, LOW) for h in half],
        in_specs=[pl.BlockSpec(memory_space=pl.ANY)] * n,
        out_specs=[pl.BlockSpec(memory_space=pltpu.VMEM)] * n,
        scratch_shapes=([pltpu.VMEM(h, st.dtype) for h, st in zip(half, stacks)] * 2
                        + [pltpu.SemaphoreType.DMA((n, n_chip))] * 3),
        compiler_params=pltpu.CompilerParams(vmem_limit_bytes=VMEM_LIMIT_BYTES, collective_id=COLLECTIVE_PAIR),
    )(*stacks)


def _exchange(arrays, sliced, name, relay=False):
    ex = _Exchange(arrays, sliced, relay_at=0 if relay else None)

    def body(*refs):
        ins, outs, sems = refs[:ex.n], refs[ex.n:2 * ex.n], refs[2 * ex.n:]
        ex.start(ins, outs, sems)
        if relay:
            ex.relay(ins, outs, sems)
            ex.relay_last(ins, outs, sems)
        ex.wait(ins, outs, sems)

    return pl.pallas_call(body, name=name, out_shape=ex.out_shape, in_specs=ex.specs, out_specs=ex.specs,
                          scratch_shapes=ex.scratch_shapes,
                          compiler_params=pltpu.CompilerParams(collective_id=ex.collective_id))(*ex.arrays)


def _call(body, *, name, grid, in_specs, out_specs, out_shape, args, scratch_shapes=(), exchange=None):
    params = _params(*(("arbitrary",) * len(grid)))
    if exchange is None:
        return pl.pallas_call(body, name=name, grid=grid, in_specs=in_specs, out_specs=out_specs, out_shape=out_shape,
                              scratch_shapes=list(scratch_shapes), compiler_params=params)(*args)
    exs = list(exchange) if isinstance(exchange, (list, tuple)) else [exchange]
    assert len(exs) == 1 or all(ex.collective_id is None for ex in exs)
    params = pltpu.CompilerParams(dimension_semantics=("arbitrary",) * len(grid), vmem_limit_bytes=VMEM_LIMIT_BYTES,
                                  collective_id=exs[0].collective_id)
    n_in, n_out, n_scr = len(in_specs), len(out_specs), len(scratch_shapes)
    n_ex = sum(ex.n for ex in exs)
    n_steps = 1
    for g in grid:
        n_steps *= g

    def hosted(*refs):
        ins, refs = refs[:n_in], refs[n_in:]
        ex_ins, refs = refs[:n_ex], refs[n_ex:]
        outs, refs = refs[:n_out], refs[n_out:]
        ex_outs, refs = refs[:n_ex], refs[n_ex:]
        scr, sems = refs[:n_scr], refs[n_scr:]
        parts, at = [], 0
        for j, ex in enumerate(exs):
            parts.append((ex_ins[at:at + ex.n], ex_outs[at:at + ex.n], sems[3 * j:3 * j + 3]))
            at += ex.n
        step = pl.program_id(0)
        for ax in range(1, len(grid)):
            step = step * grid[ax] + pl.program_id(ax)

        @pl.when(step == 0)
        def _():
            for ex, part in zip(exs, parts):
                ex.start(*part)

        body(*ins, *outs, *scr)

        for ex, part in zip(exs, parts):
            if ex.relay_at is not None:
                @pl.when(step == min(int(ex.relay_at * n_steps), n_steps - 1))
                def _():
                    ex.relay(*part)

                @pl.when(step == min(int((ex.relay_at + RELAY_LAST_LATER) * n_steps), n_steps - 1))
                def _():
                    ex.relay_last(*part)

        @pl.when(step == n_steps - 1)
        def _():
            for ex, part in zip(exs, parts):
                ex.wait(*part)

    return pl.pallas_call(
        hosted, name=name, grid=grid,
        in_specs=list(in_specs) + [sp for ex in exs for sp in ex.specs],
        out_specs=list(out_specs) + [sp for ex in exs for sp in ex.specs],
        out_shape=list(out_shape) + [sh for ex in exs for sh in ex.out_shape],
        scratch_shapes=list(scratch_shapes) + [sc for ex in exs for sc in ex.scratch_shapes],
        compiler_params=params)(*args, *[arr for ex in exs for arr in ex.arrays])


def _ffn_fwd(x, g, h, wgt, wut, wd, name, exchange=None):
    s_len, d = x.shape
    fc = wd.shape[0]
    ts = min(TOK_TILE, s_len)
    first = h is None

    def body(*refs):
        x_ref, gh_ref, wg_ref, wu_ref, wd_ref, xo_ref = refs[:6]
        a_ref, b_ref, s_ref = refs[-3:]
        xv = x_ref[...]
        if first:
            hb = (xv * _rms_scale(xv) * gh_ref[...]).astype(LOW)
            refs[6][...] = hb
        else:
            hb = gh_ref[...]
        for c0, c1 in _slabs(fc):
            a = _dot_nt(hb, wg_ref[c0:c1, :])
            b = _dot_nt(hb, wu_ref[c0:c1, :])
            s_ref[:, c0:c1] = (a * jax.nn.sigmoid(a) * b).astype(LOW)
            a_ref[:, c0:c1] = a.astype(LOW)
            b_ref[:, c0:c1] = b.astype(LOW)
        xo_ref[...] = xv + 0.5 * _dot(s_ref[...], wd_ref[...])

    tok = pl.BlockSpec((ts, d), lambda t: (t, 0))
    hid = pl.BlockSpec((ts, fc), lambda t: (t, 0))
    tok_out = jax.ShapeDtypeStruct((s_len, d), F32)
    h_out = jax.ShapeDtypeStruct((s_len, d), LOW)
    hid_out = jax.ShapeDtypeStruct((s_len, fc), LOW)
    return _call(
        body,
        name=name,
        grid=(s_len // ts,),
        in_specs=[tok, pl.BlockSpec((1, d), lambda t: (0, 0)) if first else tok,
                  _resident(wgt), _resident(wut), _resident(wd)],
        out_specs=[tok] + ([tok] if first else []) + [hid, hid],
        out_shape=[tok_out] + ([h_out] if first else []) + [hid_out, hid_out],
        scratch_shapes=[pltpu.VMEM((ts, fc), LOW)],
        args=(x, g if first else h, wgt, wut, wd),
        exchange=exchange,
    )


def _slabs(width, slab=256):
    return [(c0, min(c0 + slab, width)) for c0 in range(0, width, slab)]


def _ffn_bwdw(df, a, b, h, wd, name, exchange=None):
    s_len, d = df.shape
    f_len = wd.shape[0]
    tm = _pick(f_len, WGRAD_ROW_CANDIDATES)
    tk = min(WGRAD_TOK_TILE, s_len)
    n_k = s_len // tk
    slabs = _slabs(tk, BWD_ROW_SLAB)
    chunked = n_k == 1

    def body(df_in, a_ref, b_ref, h_in, wd_ref, da_ref, db_ref, dwd_ref, dwg_ref, dwu_ref,
             s_ref, acc_d, acc_g, acc_u, *resident):
        k = pl.program_id(1)
        first = pl.program_id(0) == 0
        if chunked:
            df_ref, h_ref, load_sems = resident

            def load(j, which):
                r0, r1 = slabs[j]
                src, dst = ((df_in, df_ref), (h_in, h_ref))[which]
                return pltpu.make_async_copy(src.at[pl.ds(r0, r1 - r0)], dst.at[pl.ds(r0, r1 - r0)],
                                             load_sems.at[which, j])

            @pl.when(first)
            def _():
                for j in range(len(slabs)):
                    load(j, 0).start()
                    load(j, 1).start()
        else:
            df_ref, h_ref = df_in, h_in

        @pl.when(k == 0)
        def _():
            acc_d[...] = jnp.zeros_like(acc_d)
            acc_g[...] = jnp.zeros_like(acc_g)
            acc_u[...] = jnp.zeros_like(acc_u)

        wdv = wd_ref[...]
        for j, (r0, r1) in enumerate(slabs):
            if chunked:
                @pl.when(first)
                def _():
                    load(j, 0).wait()
                    load(j, 1).wait()
            ds = _dot_nt(df_ref[r0:r1, :], wdv)
            av = a_ref[r0:r1, :].astype(F32)
            bv = b_ref[r0:r1, :].astype(F32)
            sig = jax.nn.sigmoid(av)
            silu = av * sig
            s_ref[r0:r1, :] = (silu * bv).astype(LOW)
            da_ref[r0:r1, :] = (ds * bv * (sig * (1.0 + av * (1.0 - sig)))).astype(LOW)
            db_ref[r0:r1, :] = (ds * silu).astype(LOW)
            hv = h_ref[r0:r1, :]
            acc_d[...] += _dot_tn(s_ref[r0:r1, :], df_ref[r0:r1, :])
            acc_g[...] += _dot_tn(da_ref[r0:r1, :], hv)
            acc_u[...] += _dot_tn(db_ref[r0:r1, :], hv)

        @pl.when(k == n_k - 1)
        def _():
            dwd_ref[...] = acc_d[...].astype(LOW)
            dwg_ref[...] = acc_g[...].astype(LOW)
            dwu_ref[...] = acc_u[...].astype(LOW)

    hid = pl.BlockSpec((tk, tm), lambda i, k: (k, i))
    tok = pl.BlockSpec(memory_space=pl.ANY) if chunked else pl.BlockSpec((tk, d), lambda i, k: (k, 0))
    wrow = pl.BlockSpec((tm, d), lambda i, k: (i, 0))
    resident = [pltpu.VMEM((tk, d), LOW)] * 2 + [pltpu.SemaphoreType.DMA((2, len(slabs)))] if chunked else []
    return _call(
        body,
        name=name,
        grid=(f_len // tm, n_k),
        in_specs=[tok, hid, hid, tok, wrow],
        out_specs=[hid, hid, wrow, wrow, wrow],
        out_shape=[jax.ShapeDtypeStruct((s_len, f_len), LOW)] * 2 + [jax.ShapeDtypeStruct((f_len, d), LOW)] * 3,
        scratch_shapes=[pltpu.VMEM((tk, tm), LOW)] + [pltpu.VMEM((tm, d), F32)] * 3 + resident,
        args=(df, a, b, h, wd),
        exchange=exchange,
    )


def _ffn_dx(dxo, x, g, parts, name, exchange=None):
    s_len, d = x.shape
    ts = min(TOK_TILE, s_len)
    n_p = len(parts)

    def body(dxo_ref, x_ref, g_ref, *refs):
        dxi_ref, dg_ref = refs[4 * n_p:]

        @pl.when(pl.program_id(0) == 0)
        def _():
            dg_ref[...] = jnp.zeros_like(dg_ref)

        dh = None
        for p in range(n_p):
            da_ref, db_ref, wg_ref, wu_ref = refs[4 * p:4 * p + 4]
            part = _dot(da_ref[...], wg_ref[...]) + _dot(db_ref[...], wu_ref[...])
            dh = part if dh is None else dh + part
        dx, dgp = _rms_bwd(dh, x_ref[...], g_ref[...])
        dxi_ref[...] = dxo_ref[...] + dx
        dg_ref[...] += dgp

    tok = pl.BlockSpec((ts, d), lambda t: (t, 0))
    vec = pl.BlockSpec((1, d), lambda t: (0, 0))
    part_specs, part_args = [], []
    for da, db, wgt, wut in parts:
        hid = pl.BlockSpec((ts, da.shape[1]), lambda t: (t, 0))
        part_specs += [hid, hid, _resident(wgt), _resident(wut)]
        part_args += [da, db, wgt, wut]
    return _call(
        body,
        name=name,
        grid=(s_len // ts,),
        in_specs=[tok, tok, vec] + part_specs,
        out_specs=[tok, vec],
        out_shape=[jax.ShapeDtypeStruct((s_len, d), F32), jax.ShapeDtypeStruct((1, d), F32)],
        args=(dxo, x, g, *part_args),
        exchange=exchange,
    )


def _wgrad_tn(xm, ym, tn, stacked, name, exchange=None):
    s_len, m = xm.shape
    n = ym.shape[1]
    tk = min(WGRAD_TOK_TILE, s_len)
    n_k = s_len // tk

    def body(x_ref, y_ref, o_ref, acc):
        k = pl.program_id(1)

        @pl.when(k == 0)
        def _():
            acc[...] = jnp.zeros_like(acc)

        acc[...] += _dot_tn(x_ref[...].astype(LOW), y_ref[...].astype(LOW))

        @pl.when(k == n_k - 1)
        def _():
            o_ref[...] = acc[...].astype(LOW)

    if stacked:
        out_spec = pl.BlockSpec((None, m, tn), lambda j, k: (j, 0, 0))
        out_shape = jax.ShapeDtypeStruct((n // tn, m, tn), LOW)
    else:
        out_spec = pl.BlockSpec((m, tn), lambda j, k: (0, j))
        out_shape = jax.ShapeDtypeStruct((m, n), LOW)
    return _call(
        body,
        name=name,
        grid=(n // tn, n_k),
        in_specs=[pl.BlockSpec((tk, m), lambda j, k: (k, 0)), pl.BlockSpec((tk, tn), lambda j, k: (k, j))],
        out_specs=[out_spec],
        out_shape=[out_shape],
        scratch_shapes=[pltpu.VMEM((m, tn), F32)],
        args=(xm, ym),
        exchange=exchange,
    )


def _mix_parts(ext_ref, cw, ts, row0, with_pooled=True):
    dc = D_CONV

    def back(off, c0, c1):
        return ext_ref[HALO - off:HALO - off + ts, c0:c1]

    v, gate_b, gate_c = back(0, 0, dc), back(0, dc, 2 * dc), back(0, 2 * dc, 3 * dc)
    z0 = gate_c * v
    z1 = back(1, 2 * dc, 3 * dc) * back(1, 0, dc)
    z2 = back(2, 2 * dc, 3 * dc) * back(2, 0, dc)
    conv = cw[2:3, :] * z0 + cw[1:2, :] * z1 + cw[0:1, :] * z2
    rows = row0 + lax.broadcasted_iota(jnp.int32, (ts, 1), 0)
    pooled, inv_count = [], []
    for grp, w in enumerate(POOL_WINDOWS):
        inv = 1.0 / jnp.minimum(rows + 1, w).astype(F32)
        inv_count.append(inv)
        if with_pooled:
            c0 = 3 * dc + POOL_GC * grp
            u = back(0, c0, c0 + POOL_GC)
            acc = u
            for j in range(1, w):
                acc = acc + back(j, c0, c0 + POOL_GC)
            pooled.append(acc * inv - u)
    return v, gate_b, gate_c, z0, z1, z2, conv, pooled, inv_count


def _mix_fwd(x, g, w_in, conv_w, pool_w, pool_scale, w_out, name, exchange=None):
    s_len, d = x.shape
    n_blk, _, wcols = w_in.shape
    p_len = n_blk * wcols
    d_mix = w_out.shape[0]
    ts = min(MIX_TOK_TILE, s_len)
    dc = D_CONV

    def body(x_ref, g_ref, win_ref, cw_ref, pw_ref, ps_ref, wout_ref, x2_ref, h_ref, proj_ref, pooled_ref,
             ext_ref, cat_ref):
        t = pl.program_id(0)

        @pl.when(t == 0)
        def _():
            ext_ref[0:HALO, :] = jnp.zeros((HALO, p_len), F32)

        xv = x_ref[...]
        hb = (xv * _rms_scale(xv) * g_ref[...]).astype(LOW)
        h_ref[...] = hb
        for k in range(n_blk):
            ext_ref[HALO:HALO + ts, wcols * k:wcols * (k + 1)] = _dot(hb, win_ref[k])
        proj_ref[...] = ext_ref[HALO:HALO + ts, :]

        _, gate_b, _, _, _, _, conv, pooled, _ = _mix_parts(ext_ref, cw_ref[...], ts, t * ts)
        cat_ref[:, 0:dc] = (gate_b * conv).astype(LOW)
        for grp in range(len(POOL_WINDOWS)):
            c0 = POOL_GC * grp
            pooled_b = pooled[grp].astype(LOW)
            pooled_ref[:, c0:c0 + POOL_GC] = pooled_b
            lin = _dot(pooled_b, pw_ref[grp])
            cat_ref[:, dc + c0:dc + c0 + POOL_GC] = (lin * ps_ref[:, c0:c0 + POOL_GC]).astype(LOW)
        x2_ref[...] = xv + _dot(cat_ref[...], wout_ref[...])
        ext_ref[0:HALO, :] = ext_ref[ts:ts + HALO, :]

    tok = pl.BlockSpec((ts, d), lambda t: (t, 0))

    def whole(arr):
        return pl.BlockSpec(arr.shape, lambda t: (0,) * arr.ndim)

    return _call(
        body,
        name=name,
        grid=(s_len // ts,),
        in_specs=[tok, whole(g), _resident(w_in), whole(conv_w), whole(pool_w), whole(pool_scale), _resident(w_out)],
        out_specs=[tok, tok, pl.BlockSpec((ts, p_len), lambda t: (t, 0)),
                   pl.BlockSpec((ts, d_mix - dc), lambda t: (t, 0))],
        out_shape=[
            jax.ShapeDtypeStruct((s_len, d), F32),
            jax.ShapeDtypeStruct((s_len, d), LOW),
            jax.ShapeDtypeStruct((s_len, p_len), F32),
            jax.ShapeDtypeStruct((s_len, d_mix - dc), LOW),
        ],
        scratch_shapes=[pltpu.VMEM((ts + HALO, p_len), F32), pltpu.VMEM((ts, d_mix), LOW)],
        args=(x, g, w_in, conv_w, pool_w, pool_scale, w_out),
        exchange=exchange,
    )


def _mix_bwd(dx2, x, proj, pooled, g, w_in, conv_w, pool_w, pool_scale, w_out, name, exchange=None):
    s_len, d = x.shape
    n_blk, _, wcols = w_in.shape
    p_len = n_blk * wcols
    d_mix = w_out.shape[0]
    ts = min(MIX_TOK_TILE, s_len)
    n_t = s_len // ts
    dc = D_CONV
    n_grp = len(POOL_WINDOWS)

    def body(dx2_ref, x_ref, proj_ref, halo_ref, pooled_ref, g_ref, win_ref, cw_ref, pw_ref, ps_ref, wout_ref,
             dx_ref, dproj_ref, dwout_ref, dg_ref, dcw_ref, dpw_ref, dps_ref, df_ref,
             ext_ref, fut_ref, cat_ref, dwout_acc, win_a, win_b):
        i = pl.program_id(0)
        t = n_t - 1 - i

        @pl.when(i == 0)
        def _():
            win_a[...] = jnp.zeros_like(win_a)
            win_b[...] = jnp.zeros_like(win_b)
            dwout_acc[...] = jnp.zeros_like(dwout_acc)
            dg_ref[...] = jnp.zeros_like(dg_ref)
            dcw_ref[...] = jnp.zeros_like(dcw_ref)
            dpw_ref[...] = jnp.zeros_like(dpw_ref)
            dps_ref[...] = jnp.zeros_like(dps_ref)
            fut_ref[ts:ts + 2 * HALO, :] = jnp.zeros((2 * HALO, d_mix), F32)

        ext_ref[HALO:HALO + ts, :] = proj_ref[...]

        @pl.when(t == 0)
        def _():
            ext_ref[0:HALO, :] = jnp.zeros((HALO, p_len), F32)

        @pl.when(t > 0)
        def _():
            ext_ref[0:HALO, :] = halo_ref[...]

        cw = cw_ref[...]
        v, gate_b, gate_c, z0, z1, z2, conv, _, inv_count = _mix_parts(ext_ref, cw, ts, t * ts, with_pooled=False)
        dx2 = dx2_ref[...]
        dcat = _dot_nt(dx2.astype(LOW), wout_ref[...])

        dy_a = dcat[:, 0:dc]
        dconv = dy_a * gate_b
        fut_ref[0:ts, 0:dc] = dconv
        cat_ref[:, 0:dc] = (gate_b * conv).astype(LOW)
        dproj_ref[:, dc:2 * dc] = (dy_a * conv).astype(LOW)
        dcw_ref[2:3, :] += jnp.sum(dconv * z0, axis=0, keepdims=True)
        dcw_ref[1:2, :] += jnp.sum(dconv * z1, axis=0, keepdims=True)
        dcw_ref[0:1, :] += jnp.sum(dconv * z2, axis=0, keepdims=True)

        dpool = []
        for grp in range(n_grp):
            c0 = POOL_GC * grp
            pooled_b = pooled_ref[:, c0:c0 + POOL_GC]
            lin = _dot(pooled_b, pw_ref[grp])
            dy_b = dcat[:, dc + c0:dc + c0 + POOL_GC]
            scale = ps_ref[:, c0:c0 + POOL_GC]
            cat_ref[:, dc + c0:dc + c0 + POOL_GC] = (lin * scale).astype(LOW)
            dps_ref[:, c0:c0 + POOL_GC] += jnp.sum(dy_b * lin, axis=0, keepdims=True)
            dlin = (dy_b * scale).astype(LOW)
            dpw_ref[grp] += _dot_tn(pooled_b, dlin)
            dpool.append(_dot_nt(dlin, pw_ref[grp]))
            fut_ref[0:ts, dc + c0:dc + c0 + POOL_GC] = dpool[grp] * inv_count[grp]
        dwout_acc[...] += _dot_tn(cat_ref[...], dx2.astype(LOW))

        def ahead(off, c0, c1):
            return fut_ref[off:off + ts, c0:c1]

        dz = cw[2:3, :] * ahead(0, 0, dc) + cw[1:2, :] * ahead(1, 0, dc) + cw[0:1, :] * ahead(2, 0, dc)
        dproj_ref[:, 0:dc] = (dz * gate_c).astype(LOW)
        dproj_ref[:, 2 * dc:3 * dc] = (dz * v).astype(LOW)
        rows_ext = ts + HALO
        for grp, w in enumerate(POOL_WINDOWS):
            c0 = dc + POOL_GC * grp
            src, dst, other = fut_ref, win_a, win_b
            cols = slice(c0, c0 + POOL_GC)
            span = 1
            while span < w:
                dst[0:rows_ext, :] = src[0:rows_ext, cols] + src[span:span + rows_ext, cols]
                src, dst, other, cols = dst, other, dst, slice(0, POOL_GC)
                span *= 2
            dproj_ref[:, 2 * dc + c0:2 * dc + c0 + POOL_GC] = (src[0:ts, cols] - dpool[grp]).astype(LOW)

        dh = _dot_nt(dproj_ref[:, 0:wcols], win_ref[0])
        for k in range(1, n_blk):
            dh += _dot_nt(dproj_ref[:, wcols * k:wcols * (k + 1)], win_ref[k])
        dx, dgp = _rms_bwd(dh, x_ref[...], g_ref[...])
        dx = dx2 + dx
        dx_ref[...] = dx
        df_ref[...] = (0.5 * dx).astype(LOW)
        dg_ref[...] += dgp
        fut_ref[ts:ts + HALO, :] = fut_ref[0:HALO, :]

        @pl.when(i == n_t - 1)
        def _():
            dwout_ref[...] = dwout_acc[...].astype(LOW)

    tok = pl.BlockSpec((ts, d), lambda i: (n_t - 1 - i, 0))
    halo = pl.BlockSpec((HALO, p_len), lambda i: (jnp.maximum((n_t - 1 - i) * (ts // HALO) - 1, 0), 0))

    def whole(arr):
        return pl.BlockSpec(arr.shape, lambda i: (0,) * arr.ndim)

    return _call(
        body,
        name=name,
        grid=(n_t,),
        in_specs=[tok, tok, pl.BlockSpec((ts, p_len), lambda i: (n_t - 1 - i, 0)), halo,
                  pl.BlockSpec((ts, d_mix - dc), lambda i: (n_t - 1 - i, 0)),
                  whole(g), _resident(w_in), whole(conv_w), whole(pool_w), whole(pool_scale), _resident(w_out)],
        out_specs=[tok, pl.BlockSpec((ts, p_len), lambda i: (n_t - 1 - i, 0)),
                   whole(w_out),
                   whole(g), whole(conv_w), whole(pool_w), whole(pool_scale), tok],
        out_shape=[
            jax.ShapeDtypeStruct((s_len, d), F32),
            jax.ShapeDtypeStruct((s_len, p_len), LOW),
            jax.ShapeDtypeStruct((d_mix, d), LOW),
            jax.ShapeDtypeStruct(g.shape, F32),
            jax.ShapeDtypeStruct(conv_w.shape, F32),
            jax.ShapeDtypeStruct(pool_w.shape, F32),
            jax.ShapeDtypeStruct(pool_scale.shape, F32),
            jax.ShapeDtypeStruct((s_len, d), LOW),
        ],
        scratch_shapes=[pltpu.VMEM((ts + HALO, p_len), F32), pltpu.VMEM((ts + 2 * HALO, d_mix), F32),
                        pltpu.VMEM((ts, d_mix), LOW), pltpu.VMEM((d_mix, d), F32),
                        pltpu.VMEM((ts + 2 * HALO, POOL_GC), F32), pltpu.VMEM((ts + 2 * HALO, POOL_GC), F32)],
        args=(dx2, x, proj, proj, pooled, g, w_in, conv_w, pool_w, pool_scale, w_out),
        exchange=exchange,
    )


def _ffn_fwd_loss(x, h, wgt, wut, wd, g, target, name):
    s_len, d = x.shape
    fc = wd.shape[0]
    ts = min(TOK_TILE, s_len)

    def body(x_ref, h_ref, wg_ref, wu_ref, wd_ref, g_ref, tgt_ref,
             a_ref, b_ref, loss_ref, dx_ref, dg_ref, df_ref, s_ref):
        @pl.when(pl.program_id(0) == 0)
        def _():
            loss_ref[...] = jnp.zeros_like(loss_ref)
            dg_ref[...] = jnp.zeros_like(dg_ref)

        hb = h_ref[...]
        for c0, c1 in _slabs(fc):
            a = _dot_nt(hb, wg_ref[c0:c1, :])
            b = _dot_nt(hb, wu_ref[c0:c1, :])
            s_ref[:, c0:c1] = (a * jax.nn.sigmoid(a) * b).astype(LOW)
            a_ref[:, c0:c1] = a.astype(LOW)
            b_ref[:, c0:c1] = b.astype(LOW)
        xv = x_ref[...] + 0.5 * _dot(s_ref[...], wd_ref[...])
        gv = g_ref[...]
        err = xv * _rms_scale(xv) * gv - tgt_ref[...]
        loss_ref[...] += 0.5 * jnp.sum(jnp.mean(err * err, axis=-1, keepdims=True), axis=0, keepdims=True)
        dx, dgp = _rms_bwd(err * (1.0 / d), xv, gv)
        dx_ref[...] = dx
        df_ref[...] = (0.5 * dx).astype(LOW)
        dg_ref[...] += dgp

    tok = pl.BlockSpec((ts, d), lambda t: (t, 0))
    vec = pl.BlockSpec((1, d), lambda t: (0, 0))
    hid = pl.BlockSpec((ts, fc), lambda t: (t, 0))
    return pl.pallas_call(
        body,
        name=name,
        grid=(s_len // ts,),
        in_specs=[tok, tok, _resident(wgt), _resident(wut), _resident(wd), vec, tok],
        out_specs=[hid, hid, pl.BlockSpec((1, 128), lambda t: (0, 0)), tok, vec, tok],
        out_shape=[
            jax.ShapeDtypeStruct((s_len, fc), LOW),
            jax.ShapeDtypeStruct((s_len, fc), LOW),
            jax.ShapeDtypeStruct((1, 128), F32),
            jax.ShapeDtypeStruct((s_len, d), F32),
            jax.ShapeDtypeStruct((1, d), F32),
            jax.ShapeDtypeStruct((s_len, d), LOW),
        ],
        scratch_shapes=[pltpu.VMEM((ts, fc), LOW)],
        compiler_params=_params("arbitrary"),
    )(x, h, wgt, wut, wd, g, target)


def _row_tile(rows, cols, stack_bytes):
    budget = 20 * 1024 * 1024
    per_row = cols * (4 * 7 + stack_bytes)
    for tr in (rows, 512, 256, 176, 128, 64, 32, 16, 8):
        if rows % tr == 0 and tr % 8 == 0 and tr * per_row * 2 <= budget:
            return tr
    return rows


def _sum_stack(stack, name):
    n, r, c = stack.shape
    tr = _row_tile(r, c, n * stack.dtype.itemsize)

    def body(s_ref, o_ref):
        acc = s_ref[0].astype(F32)
        for k in range(1, n):
            acc = acc + s_ref[k].astype(F32)
        o_ref[...] = acc

    return pl.pallas_call(
        body,
        name=name,
        grid=(r // tr,),
        in_specs=[pl.BlockSpec((n, tr, c), lambda i: (0, i, 0))],
        out_specs=pl.BlockSpec((tr, c), lambda i: (i, 0)),
        out_shape=jax.ShapeDtypeStruct((r, c), F32),
        compiler_params=_params("arbitrary"),
    )(stack)


def _adamw_many(params, name):
    params = [(list(st) if isinstance(st, (list, tuple)) else [st], w, m, v) for st, w, m, v in params]
    stacks0, w0 = params[0][0], params[0][1]
    r, c = w0.shape
    n = stacks0[0].shape[0]
    n_st = len(stacks0)
    part_rows = [st.shape[1] for st in stacks0]
    assert sum(part_rows) == r
    assert all(w.shape == (r, c) and [st.shape for st in sts] == [st.shape for st in stacks0] for sts, w, _, _ in params)
    first_row = [sum(part_rows[:j]) for j in range(n_st)]
    tc = next((t for t in (512, 256, 128) if c % t == 0), c)
    c1 = 1.0 - ADAM_B1 ** ADAM_STEP
    c2 = 1.0 - ADAM_B2 ** ADAM_STEP
    n_in = n_st + 3

    def body(*refs):
        ins, outs = refs[:n_in * len(params)], refs[n_in * len(params):]
        for p in range(len(params)):
            s_refs = ins[n_in * p:n_in * p + n_st]
            w_ref, m_ref, v_ref = ins[n_in * p + n_st:n_in * (p + 1)]
            g_ref, d_ref, mo_ref, vo_ref = outs[4 * p:4 * p + 4]
            for s_ref, r0, nr in zip(s_refs, first_row, part_rows):
                gv = s_ref[0].astype(F32)
                for k in range(1, n):
                    gv = gv + s_ref[k].astype(F32)
                mn = ADAM_B1 * m_ref[r0:r0 + nr, :] + (1.0 - ADAM_B1) * gv
                vn = ADAM_B2 * v_ref[r0:r0 + nr, :] + (1.0 - ADAM_B2) * (gv * gv)
                g_ref[r0:r0 + nr, :] = gv
                mo_ref[r0:r0 + nr, :] = mn
                vo_ref[r0:r0 + nr, :] = vn
                d_ref[r0:r0 + nr, :] = -ADAM_LR * ((mn / c1) / (jnp.sqrt(vn / c2) + ADAM_EPS)
                                                   + ADAM_WD * w_ref[r0:r0 + nr, :])

    blk = pl.BlockSpec((r, tc), lambda i: (0, i))
    one_in = [pl.BlockSpec((n, nr, tc), lambda i: (0, 0, i)) for nr in part_rows] + [blk, blk, blk]
    res = pl.pallas_call(
        body,
        name=name,
        grid=(c // tc,),
        in_specs=one_in * len(params),
        out_specs=[blk] * (4 * len(params)),
        out_shape=[jax.ShapeDtypeStruct((r, c), F32)] * (4 * len(params)),
        compiler_params=_params("arbitrary"),
    )(*[arr for sts, w, m, v in params for arr in (*sts, w, m, v)])
    return [tuple(res[4 * p:4 * p + 4]) for p in range(len(params))]


def _adamw(stacks, w, m, v, name):
    return _adamw_many([(stacks, w, m, v)], name)[0]


def _to_sheet(parts):
    sheets, spans = [], []
    row = 0
    for p in parts:
        flat = p.reshape(-1).astype(F32)
        rows = -(-flat.shape[0] // 1024) * 8
        flat = jnp.pad(flat, (0, rows * 128 - flat.shape[0]))
        sheets.append(flat.reshape(rows, 128))
        spans.append((row, p.size, p.shape))
        row += rows
    return jnp.concatenate(sheets, axis=0), spans


def _from_sheet(sheet, spans):
    out = []
    for row, size, shape in spans:
        rows = -(-size // 1024) * 8
        out.append(sheet[row:row + rows].reshape(-1)[:size].reshape(shape))
    return out


def kernel(x, norm_ffn1, ffn1_w_gate, ffn1_w_up, ffn1_w_down, norm_mix, w_in, conv_w, pool_w, pool_scale, w_out, norm_ffn2, ffn2_w_gate, ffn2_w_up, ffn2_w_down, norm_final, loss_target, m_norm_ffn1, m_ffn1_w_gate, m_ffn1_w_up, m_ffn1_w_down, m_norm_mix, m_w_in, m_conv_w, m_pool_w, m_pool_scale, m_w_out, m_norm_ffn2, m_ffn2_w_gate, m_ffn2_w_up, m_ffn2_w_down, m_norm_final, v_norm_ffn1, v_ffn1_w_gate, v_ffn1_w_up, v_ffn1_w_down, v_norm_mix, v_w_in, v_conv_w, v_pool_w, v_pool_scale, v_w_out, v_norm_ffn2, v_ffn2_w_gate, v_ffn2_w_up, v_ffn2_w_down, v_norm_final):
    me = 4 * lax.axis_index("x") + 2 * lax.axis_index("y") + lax.axis_index("c")
    xs, tgt = x[0], loss_target[0]
    s_len, d = xs.shape
    f_shard = ffn1_w_down.shape[1]
    conv_shard = conv_w.shape[2]

    def low_t(wt):
        return wt[0].T.astype(LOW)

    def by_dev(gw):
        return gw.reshape(N_DEV, -1, d)

    conv_tile = jnp.zeros((8, 128), F32).at[0:conv_w.shape[1], 0:conv_shard].set(conv_w[0])
    pool_w_low = pool_w[0].astype(LOW)

    rows_a = -(-f_shard // 64) * 32

    def parts_of(w_gate, w_up, w_down):
        shards = [low_t(w_gate), low_t(w_up), w_down[0].astype(LOW)]
        return [(s, 0, rows_a) for s in shards], [(s, rows_a, f_shard - rows_a) for s in shards]

    def rows_flat(stacks):
        return [st.reshape(-1, d) for st in stacks]

    def gather(shards):
        return _Exchange(shards, [False] * len(shards), relay_at=0.6)

    def scatter(pairs):
        return _Exchange(pairs, [True] * len(pairs), among_chips=True)

    w1a_shards, w1b_shards = parts_of(ffn1_w_gate, ffn1_w_up, ffn1_w_down)
    w2a_shards, w2b_shards = parts_of(ffn2_w_gate, ffn2_w_up, ffn2_w_down)

    wg1a, wu1a, wd1a = rows_flat(_exchange(w1a_shards, [False] * 3, "gather_ffn1_a", relay=True))
    xa, h1, a1a, b1a, *w1b = _ffn_fwd(xs, norm_ffn1, None, wg1a, wu1a, wd1a, "ffn1_fwd_a",
                                      exchange=gather(w1b_shards))
    wg1b, wu1b, wd1b = rows_flat(w1b)
    x1, a1b, b1b, w_in_full, w_out_full, conv_tiles = _ffn_fwd(
        xa, None, h1, wg1b, wu1b, wd1b, "ffn1_fwd_b",
        exchange=gather([w_in[0].astype(LOW), w_out[0].astype(LOW), conv_tile]))
    w_out_full = w_out_full.reshape(-1, d)
    conv_full = jnp.concatenate([conv_tiles[k, 0:conv_w.shape[1], 0:conv_shard] for k in range(N_DEV)], axis=1)
    x2, h2, proj, pooled, *w2a = _mix_fwd(x1, norm_mix, w_in_full, conv_full, pool_w_low, pool_scale, w_out_full,
                                          "mix_fwd", exchange=gather(w2a_shards))
    wg2a, wu2a, wd2a = rows_flat(w2a)
    xb, h3, a2a, b2a, *w2b = _ffn_fwd(x2, norm_ffn2, None, wg2a, wu2a, wd2a, "ffn2_fwd_a",
                                      exchange=gather(w2b_shards))
    wg2b, wu2b, wd2b = rows_flat(w2b)
    a2b, b2b, loss_row, dx3, dg_final, df3 = _ffn_fwd_loss(
        xb, h3, wg2b, wu2b, wd2b, norm_final.reshape(1, d), tgt, "ffn2_fwd_b_loss")

    da2a, db2a, dwd_a, dwg_a, dwu_a = _ffn_bwdw(df3, a2a, b2a, h3, wd2a, "ffn2_bwdw_a")
    da2b, db2b, dwd_b, dwg_b, dwu_b = _ffn_bwdw(df3, a2b, b2b, h3, wd2b, "ffn2_bwdw_b")
    pairs = _pair_sum([by_dev(dwg_a), by_dev(dwu_a), by_dev(dwd_a), by_dev(dwg_b), by_dev(dwu_b), by_dev(dwd_b)],
                      "pair_sum_ffn2")
    dx2, dg_ffn2, *got_2b = _ffn_dx(dx3, x2, norm_ffn2, [(da2a, db2a, wg2a, wu2a), (da2b, db2b, wg2b, wu2b)],
                                    "ffn2_dx", exchange=scatter(pairs[3:]))
    dx1, dproj, dw_out, dg_mix, dconv, dpool_w, dpool_scale, df1, *got_2a = _mix_bwd(
        dx2, x1, proj, pooled, norm_mix, w_in_full, conv_full, pool_w_low, pool_scale, w_out_full, "mix_bwd",
        exchange=scatter(pairs[:3]))
    small_parts = [dg_mix, dg_ffn2, dg_final, dconv, dpool_w, dpool_scale, loss_row]
    small_sheet, spans = _to_sheet(small_parts)
    dw_in, got_small = _wgrad_tn(h2, dproj, W_IN_SHARD, True, "w_in_wgrad",
                                 exchange=_Exchange([small_sheet], [False], relay_at=0.5))
    pairs = _pair_sum([dw_in, by_dev(dw_out)], "pair_sum_mix")
    da1a, db1a, dwd, dwg, dwu, got_in, got_out = _ffn_bwdw(
        df1, a1a, b1a, h1, wd1a, "ffn1_bwdw_a", exchange=scatter(pairs))
    pairs = _pair_sum([by_dev(dwg), by_dev(dwu), by_dev(dwd)], "pair_sum_ffn1_a")
    da1b, db1b, dwd, dwg, dwu, *got_1a = _ffn_bwdw(df1, a1b, b1b, h1, wd1b, "ffn1_bwdw_b", exchange=scatter(pairs))
    pairs = _pair_sum([by_dev(dwg), by_dev(dwu), by_dev(dwd)], "pair_sum_ffn1_b")
    dx0, dg_ffn1, *got_1b = _ffn_dx(dx1, xs, norm_ffn1, [(da1a, db1a, wg1a, wu1a), (da1b, db1b, wg1b, wu1b)],
                                    "ffn1_dx", exchange=scatter(pairs))
    (got_n1,) = _exchange([dg_ffn1.reshape(8, 128)], [False], "gather_dnorm_ffn1")

    outs = {}

    def update(name, stacks, w, m, v):
        outs[name] = _adamw(stacks, w[0], m[0], v[0], "adamw_" + name)

    def update_ffn(prefix, got_a, got_b, gate, up, down):
        res = _adamw_many(
            [([got_a[j], got_b[j]], *[(t[0].T if j < 2 else t[0]) for t in wmv]) for j, wmv in enumerate((gate, up, down))],
            "adamw_" + prefix)
        outs[prefix + "_w_gate"] = tuple(r.T for r in res[0])
        outs[prefix + "_w_up"] = tuple(r.T for r in res[1])
        outs[prefix + "_w_down"] = res[2]

    update_ffn("ffn1", got_1a, got_1b, (ffn1_w_gate, m_ffn1_w_gate, v_ffn1_w_gate),
               (ffn1_w_up, m_ffn1_w_up, v_ffn1_w_up), (ffn1_w_down, m_ffn1_w_down, v_ffn1_w_down))
    update_ffn("ffn2", got_2a, got_2b, (ffn2_w_gate, m_ffn2_w_gate, v_ffn2_w_gate),
               (ffn2_w_up, m_ffn2_w_up, v_ffn2_w_up), (ffn2_w_down, m_ffn2_w_down, v_ffn2_w_down))
    update("w_in", got_in, w_in, m_w_in, v_w_in)
    update("w_out", got_out, w_out, m_w_out, v_w_out)

    g_small = _from_sheet(_sum_stack(got_small, "sum_small"), spans)
    g_norm_ffn1 = _sum_stack(got_n1, "sum_dnorm_ffn1").reshape(norm_ffn1.shape)
    g_conv = lax.dynamic_slice_in_dim(g_small[3], me * conv_shard, conv_shard, axis=1)
    row = (1, d)
    norms = [("norm_ffn1", g_norm_ffn1, norm_ffn1, m_norm_ffn1, v_norm_ffn1),
             ("norm_mix", g_small[0], norm_mix, m_norm_mix, v_norm_mix),
             ("norm_ffn2", g_small[1], norm_ffn2, m_norm_ffn2, v_norm_ffn2),
             ("norm_final", g_small[2], norm_final, m_norm_final, v_norm_final)]
    res = _adamw_many([(gn.reshape((1,) + row), wn.reshape(row), mn.reshape(row), vn.reshape(row))
                       for _, gn, wn, mn, vn in norms], "adamw_norms")
    for (nm, _, wn, _, _), r4 in zip(norms, res):
        outs[nm] = tuple(r.reshape(wn.shape) for r in r4)
    for nm, gn, wn, mn, vn in (("conv_w", g_conv, conv_w, m_conv_w, v_conv_w),
                               ("pool_w", g_small[4], pool_w, m_pool_w, v_pool_w),
                               ("pool_scale", g_small[5], pool_scale, m_pool_scale, v_pool_scale)):
        flat = (-1, wn.shape[-1])
        r4 = _adamw(gn.reshape(flat)[None], wn.reshape(flat), mn.reshape(flat), vn.reshape(flat), "adamw_" + nm)
        outs[nm] = tuple(r.reshape(wn.shape) for r in r4)

    loss = g_small[6][0, 0]
    order = ["norm_ffn1", "ffn1_w_gate", "ffn1_w_up", "ffn1_w_down", "norm_mix", "w_in", "conv_w", "pool_w",
             "pool_scale", "w_out", "norm_ffn2", "ffn2_w_gate", "ffn2_w_up", "ffn2_w_down", "norm_final"]
    big = {"ffn1_w_gate", "ffn1_w_up", "ffn1_w_down", "w_in", "w_out", "ffn2_w_gate", "ffn2_w_up", "ffn2_w_down"}

    def leaf(nm, j):
        val = outs[nm][j]
        return val[None] if nm in big else val

    return (loss, dx0[None],
            *[leaf(nm, 0) for nm in order], *[leaf(nm, 1) for nm in order],
            *[leaf(nm, 2) for nm in order], *[leaf(nm, 3) for nm in order])
```

```python
import jax
import jax.numpy as jnp
from jax import lax
from jax.experimental import pallas as pl
from jax.experimental.pallas import tpu as pltpu

F32 = jnp.float32
LOW = jnp.bfloat16

N_DEV = 8
EPS = 1e-6
D_CONV = 512
POOL_WINDOWS = (2, 4, 8, 16)
POOL_GC = 128
HALO = 16
W_IN_SHARD = 256

ADAM_LR = 0.001
ADAM_B1 = 0.9
ADAM_B2 = 0.999
ADAM_EPS = 1e-08
ADAM_WD = 0.01
ADAM_STEP = 10

VMEM_LIMIT_BYTES = 56 * 1024 * 1024
TOK_TILE = 512
MIX_TOK_TILE = 512
WGRAD_TOK_TILE = 4096
WGRAD_ROW_CANDIDATES = (256, 128)
RELAY_LAST_LATER = 0.25
BWD_ROW_SLAB = 2048


def _params(*sem):
    return pltpu.CompilerParams(dimension_semantics=sem, vmem_limit_bytes=VMEM_LIMIT_BYTES)


def _resident(arr):
    return pl.BlockSpec(arr.shape, lambda *_: (0,) * arr.ndim, pipeline_mode=pl.Buffered(1))


def _pick(n, candidates):
    for c in candidates:
        if n % c == 0:
            return c
    raise ValueError(f"no tile in {candidates} divides {n}")


def _dot(a, b):
    return lax.dot_general(a, b, (((1,), (0,)), ((), ())), preferred_element_type=F32)


def _dot_nt(a, b):
    return lax.dot_general(a, b, (((1,), (1,)), ((), ())), preferred_element_type=F32)


def _dot_tn(a, b):
    return lax.dot_general(a, b, (((0,), (0,)), ((), ())), preferred_element_type=F32)


def _rms_scale(x):
    return lax.rsqrt(jnp.mean(x * x, axis=-1, keepdims=True) + EPS)


def _rms_bwd(dy, x, g):
    r = _rms_scale(x)
    xhat = x * r
    gdy = dy * g
    dx = r * (gdy - xhat * jnp.mean(gdy * xhat, axis=-1, keepdims=True))
    return dx, jnp.sum(dy * xhat, axis=0, keepdims=True)


COLLECTIVE_PAIR, COLLECTIVE_CHIPS, COLLECTIVE_RELAY = 0, 1, 2


def _handshake(peer_numbers):
    mx, my, mc = lax.axis_index("x"), lax.axis_index("y"), lax.axis_index("c")
    barrier = pltpu.get_barrier_semaphore()
    for m in peer_numbers:
        peer = (lax.rem(mx + ((m >> 2) & 1), 2), lax.rem(my + ((m >> 1) & 1), 2), lax.rem(mc + (m & 1), 2))
        pl.semaphore_signal(barrier, inc=1, device_id=peer, device_id_type=pl.DeviceIdType.MESH)
    pl.semaphore_wait(barrier, len(peer_numbers))


class _Exchange:
    CHIPS = (2, 4, 6)

    def __init__(self, arrays, sliced, relay_at=None, among_chips=False):
        assert relay_at is None or not any(sliced)
        assert not among_chips or (all(sliced) and relay_at is None)
        self.relay_at, self.among_chips = relay_at, among_chips
        self.peers = self.CHIPS if among_chips else ((1, 2, 4) if relay_at is not None else tuple(range(1, N_DEV)))
        self.collective_id = COLLECTIVE_CHIPS if among_chips else (COLLECTIVE_RELAY if relay_at is not None else None)
        self.rows = [arr[1:] if isinstance(arr, tuple) else None for arr in arrays]
        self.arrays = [arr[0] if isinstance(arr, tuple) else arr for arr in arrays]
        self.sliced, self.n = list(sliced), len(arrays)
        assert all(rg is None or not sl for rg, sl in zip(self.rows, sliced))
        self.block_shape = [arr.shape if rg is None else (rg[1],) + arr.shape[1:]
                            for arr, rg in zip(self.arrays, self.rows)]
        self.out_shape = [jax.ShapeDtypeStruct(shape if sl else (N_DEV,) + shape, arr.dtype)
                          for arr, shape, sl in zip(self.arrays, self.block_shape, sliced)]
        self.specs = [pl.BlockSpec(memory_space=pl.ANY)] * self.n
        self.scratch_shapes = [pltpu.SemaphoreType.DMA((self.n, N_DEV)),
                               pltpu.SemaphoreType.DMA((self.n, N_DEV)),
                               pltpu.SemaphoreType.DMA((self.n,))]

    HALF_VIA = ((4, 2, 5), (2, 4, 7))

    def _halves(self, a):
        rows = self.block_shape[a][0]
        if rows % 32:
            return ((0, rows), None)
        return ((0, rows // 2), (rows // 2, rows // 2))

    def _copies(self, ins, outs, sems):
        send_sems, recv_sems, local_sems = sems
        sliced = self.sliced
        mx, my, mc = lax.axis_index("x"), lax.axis_index("y"), lax.axis_index("c")
        me = 2 * mx + my if self.among_chips else 4 * mx + 2 * my + mc

        def peer(m):
            px = lax.rem(mx + ((m >> 2) & 1), 2)
            py = lax.rem(my + ((m >> 1) & 1), 2)
            pc = lax.rem(mc + (m & 1), 2)
            return (px, py, pc), (2 * px + py if self.among_chips else 4 * px + 2 * py + pc)

        def mine(a):
            return ins[a] if self.rows[a] is None else ins[a].at[pl.ds(*self.rows[a])]

        def remote(a, m, arriving):
            pid, pflat = peer(m)
            return pltpu.make_async_remote_copy(
                src_ref=ins[a].at[pflat] if sliced[a] else mine(a),
                dst_ref=outs[a].at[pflat if arriving else me],
                send_sem=send_sems.at[a, m - 1],
                recv_sem=recv_sems.at[a, m - 1],
                device_id=pid,
                device_id_type=pl.DeviceIdType.MESH,
            )

        def local(a):
            return pltpu.make_async_copy(ins[a].at[me] if sliced[a] else mine(a), outs[a].at[me], local_sems.at[a])

        def passed_on(a, m):
            _, origin = peer(m)
            sibling, _ = peer(1)
            return pltpu.make_async_remote_copy(
                src_ref=outs[a].at[origin],
                dst_ref=outs[a].at[origin],
                send_sem=send_sems.at[a, m],
                recv_sem=recv_sems.at[a, m],
                device_id=sibling,
                device_id_type=pl.DeviceIdType.MESH,
            )

        def half_on(a, h, arriving):
            via, to, column = self.HALF_VIA[h]
            r0, nr = self._halves(a)[h]
            _, origin = peer(6 if arriving else via)
            rows = outs[a].at[origin].at[pl.ds(r0, nr)]
            return pltpu.make_async_remote_copy(
                src_ref=rows, dst_ref=rows, send_sem=send_sems.at[a, column], recv_sem=recv_sems.at[a, column],
                device_id=peer(to)[0], device_id_type=pl.DeviceIdType.MESH)

        return remote, local, passed_on, half_on

    def start(self, ins, outs, sems):
        remote, local, _, _ = self._copies(ins, outs, sems)
        if self.collective_id is not None:
            _handshake(self.peers)
        for a in range(self.n):
            local(a).start()
        for m in self.peers:
            for a in range(self.n):
                remote(a, m, False).start()

    def relay(self, ins, outs, sems):
        remote, _, passed_on, half_on = self._copies(ins, outs, sems)
        for h, (via, _, _) in enumerate(self.HALF_VIA):
            for a in range(self.n):
                remote(a, via, True).wait_recv()
                passed_on(a, via).start()
                if self._halves(a)[h] is not None:
                    half_on(a, h, False).start()

    def relay_last(self, ins, outs, sems):
        _, _, passed_on, half_on = self._copies(ins, outs, sems)
        for a in range(self.n):
            for h in range(2):
                if self._halves(a)[h] is not None:
                    half_on(a, h, True).wait_recv()
            passed_on(a, 6).start()

    def wait(self, ins, outs, sems):
        remote, local, passed_on, half_on = self._copies(ins, outs, sems)
        if self.relay_at is None:
            for m in self.peers:
                for a in range(self.n):
                    remote(a, m, True).wait_recv()
            for m in self.peers:
                for a in range(self.n):
                    remote(a, m, False).wait_send()
        else:
            for m in (1, 3, 5, 7):
                for a in range(self.n):
                    remote(a, m, True).wait_recv()
            for m in self.peers:
                for a in range(self.n):
                    remote(a, m, False).wait_send()
            for a in range(self.n):
                for m in self.CHIPS:
                    passed_on(a, m).wait_send()
                for h in range(2):
                    if self._halves(a)[h] is not None:
                        half_on(a, h, False).wait_send()
        for a in range(self.n):
            local(a).wait()


def _pair_sum(stacks, name):
    n = len(stacks)
    n_chip = N_DEV // 2
    half = [(n_chip,) + st.shape[1:] for st in stacks]

    def body(*refs):
        ins, outs, mine, theirs = refs[:n], refs[n:2 * n], refs[2 * n:3 * n], refs[3 * n:4 * n]
        local_sems, send_sems, recv_sems = refs[4 * n:]
        mx, my, mc = lax.axis_index("x"), lax.axis_index("y"), lax.axis_index("c")

        def own(a, k):
            return pltpu.make_async_copy(ins[a].at[2 * k + mc], mine[a].at[k], local_sems.at[a, k])

        def swap(a, k):
            return pltpu.make_async_remote_copy(
                src_ref=ins[a].at[2 * k + (1 - mc)], dst_ref=theirs[a].at[k],
                send_sem=send_sems.at[a, k], recv_sem=recv_sems.at[a, k],
                device_id=(mx, my, 1 - mc), device_id_type=pl.DeviceIdType.MESH)

        _handshake((1,))
        for k in range(n_chip):
            for a in range(n):
                own(a, k).start()
                swap(a, k).start()
        for k in range(n_chip):
            for a in range(n):
                own(a, k).wait()
                swap(a, k).wait()
                outs[a][k] = (mine[a][k].astype(F32) + theirs[a][k].astype(F32)).astype(LOW)

    return pl.pallas_call(
        body, name=name,
        out_shape=[jax.ShapeDtypeStruct(h, LOW) for h in half],
        in_specs=[pl.BlockSpec(memory_space=pl.ANY)] * n,
        out_specs=[pl.BlockSpec(memory_space=pltpu.VMEM)] * n,
        scratch_shapes=([pltpu.VMEM(h, st.dtype) for h, st in zip(half, stacks)] * 2
                        + [pltpu.SemaphoreType.DMA((n, n_chip))] * 3),
        compiler_params=pltpu.CompilerParams(vmem_limit_bytes=VMEM_LIMIT_BYTES, collective_id=COLLECTIVE_PAIR),
    )(*stacks)


def _exchange(arrays, sliced, name, relay=False):
    ex = _Exchange(arrays, sliced, relay_at=0 if relay else None)

    def body(*refs):
        ins, outs, sems = refs[:ex.n], refs[ex.n:2 * ex.n], refs[2 * ex.n:]
        ex.start(ins, outs, sems)
        if relay:
            ex.relay(ins, outs, sems)
            ex.relay_last(ins, outs, sems)
        ex.wait(ins, outs, sems)

    return pl.pallas_call(body, name=name, out_shape=ex.out_shape, in_specs=ex.specs, out_specs=ex.specs,
                          scratch_shapes=ex.scratch_shapes,
                          compiler_params=pltpu.CompilerParams(collective_id=ex.collective_id))(*ex.arrays)


def _call(body, *, name, grid, in_specs, out_specs, out_shape, args, scratch_shapes=(), exchange=None):
    params = _params(*(("arbitrary",) * len(grid)))
    if exchange is None:
        return pl.pallas_call(body, name=name, grid=grid, in_specs=in_specs, out_specs=out_specs, out_shape=out_shape,
                              scratch_shapes=list(scratch_shapes), compiler_params=params)(*args)
    exs = list(exchange) if isinstance(exchange, (list, tuple)) else [exchange]
    assert len(exs) == 1 or all(ex.collective_id is None for ex in exs)
    params = pltpu.CompilerParams(dimension_semantics=("arbitrary",) * len(grid), vmem_limit_bytes=VMEM_LIMIT_BYTES,
                                  collective_id=exs[0].collective_id)
    n_in, n_out, n_scr = len(in_specs), len(out_specs), len(scratch_shapes)
    n_ex = sum(ex.n for ex in exs)
    n_steps = 1
    for g in grid:
        n_steps *= g

    def hosted(*refs):
        ins, refs = refs[:n_in], refs[n_in:]
        ex_ins, refs = refs[:n_ex], refs[n_ex:]
        outs, refs = refs[:n_out], refs[n_out:]
        ex_outs, refs = refs[:n_ex], refs[n_ex:]
        scr, sems = refs[:n_scr], refs[n_scr:]
        parts, at = [], 0
        for j, ex in enumerate(exs):
            parts.append((ex_ins[at:at + ex.n], ex_outs[at:at + ex.n], sems[3 * j:3 * j + 3]))
            at += ex.n
        step = pl.program_id(0)
        for ax in range(1, len(grid)):
            step = step * grid[ax] + pl.program_id(ax)

        @pl.when(step == 0)
        def _():
            for ex, part in zip(exs, parts):
                ex.start(*part)

        body(*ins, *outs, *scr)

        for ex, part in zip(exs, parts):
            if ex.relay_at is not None:
                @pl.when(step == min(int(ex.relay_at * n_steps), n_steps - 1))
                def _():
                    ex.relay(*part)

                @pl.when(step == min(int((ex.relay_at + RELAY_LAST_LATER) * n_steps), n_steps - 1))
                def _():
                    ex.relay_last(*part)

        @pl.when(step == n_steps - 1)
        def _():
            for ex, part in zip(exs, parts):
                ex.wait(*part)

    return pl.pallas_call(
        hosted, name=name, grid=grid,
        in_specs=list(in_specs) + [sp for ex in exs for sp in ex.specs],
        out_specs=list(out_specs) + [sp for ex in exs for sp in ex.specs],
        out_shape=list(out_shape) + [sh for ex in exs for sh in ex.out_shape],
        scratch_shapes=list(scratch_shapes) + [sc for ex in exs for sc in ex.scratch_shapes],
        compiler_params=params)(*args, *[arr for ex in exs for arr in ex.arrays])


def _ffn_fwd(x, g, h, wgt, wut, wd, name, exchange=None):
    s_len, d = x.shape
    fc = wd.shape[0]
    ts = min(TOK_TILE, s_len)
    first = h is None

    def body(*refs):
        x_ref, gh_ref, wg_in, wu_in, wd_in, xo_ref = refs[:6]
        a_ref, b_ref, s_ref, wg_ref, wu_ref, wd_ref, w_sems = refs[-7:]
        wait_slab, wait_down = _weight_loads(wg_in, wu_in, wd_in, wg_ref, wu_ref, wd_ref, w_sems, _slabs(fc))
        xv = x_ref[...]
        if first:
            hb = (xv * _rms_scale(xv) * gh_ref[...]).astype(LOW)
            refs[6][...] = hb
        else:
            hb = gh_ref[...]
        for j, (c0, c1) in enumerate(_slabs(fc)):
            wait_slab(j)
            a = _dot_nt(hb, wg_ref[c0:c1, :])
            b = _dot_nt(hb, wu_ref[c0:c1, :])
            s_ref[:, c0:c1] = (a * jax.nn.sigmoid(a) * b).astype(LOW)
            a_ref[:, c0:c1] = a.astype(LOW)
            b_ref[:, c0:c1] = b.astype(LOW)
        wait_down()
        xo_ref[...] = xv + 0.5 * _dot(s_ref[...], wd_ref[...])

    tok = pl.BlockSpec((ts, d), lambda t: (t, 0))
    hid = pl.BlockSpec((ts, fc), lambda t: (t, 0))
    anywhere = pl.BlockSpec(memory_space=pl.ANY)
    tok_out = jax.ShapeDtypeStruct((s_len, d), F32)
    h_out = jax.ShapeDtypeStruct((s_len, d), LOW)
    hid_out = jax.ShapeDtypeStruct((s_len, fc), LOW)
    return _call(
        body,
        name=name,
        grid=(s_len // ts,),
        in_specs=[tok, pl.BlockSpec((1, d), lambda t: (0, 0)) if first else tok, anywhere, anywhere, anywhere],
        out_specs=[tok] + ([tok] if first else []) + [hid, hid],
        out_shape=[tok_out] + ([h_out] if first else []) + [hid_out, hid_out],
        scratch_shapes=[pltpu.VMEM((ts, fc), LOW)] + _weight_scratch(wgt, wut, wd),
        args=(x, g if first else h, wgt, wut, wd),
        exchange=exchange,
    )


def _weight_scratch(wgt, wut, wd):
    return [pltpu.VMEM(w.shape, w.dtype) for w in (wgt, wut, wd)] + [pltpu.SemaphoreType.DMA((3, len(_slabs(wd.shape[0]))))]


def _weight_loads(wg_in, wu_in, wd_in, wg_ref, wu_ref, wd_ref, sems, slabs):
    at_first = pl.program_id(0) == 0

    def slab_copies(j):
        c0, c1 = slabs[j]
        return [pltpu.make_async_copy(src.at[pl.ds(c0, c1 - c0)], dst.at[pl.ds(c0, c1 - c0)], sems.at[which, j])
                for which, (src, dst) in enumerate(((wg_in, wg_ref), (wu_in, wu_ref)))]

    down = pltpu.make_async_copy(wd_in, wd_ref, sems.at[2, 0])

    @pl.when(at_first)
    def _():
        for j in range(len(slabs)):
            for copy in slab_copies(j):
                copy.start()
        down.start()

    def wait_slab(j):
        @pl.when(at_first)
        def _():
            for copy in slab_copies(j):
                copy.wait()

    def wait_down():
        @pl.when(at_first)
        def _():
            down.wait()

    return wait_slab, wait_down


def _slabs(width, slab=256):
    return [(c0, min(c0 + slab, width)) for c0 in range(0, width, slab)]


def _ffn_bwdw(df, a, b, h, wd, name, exchange=None):
    s_len, d = df.shape
    f_len = wd.shape[0]
    tm = _pick(f_len, WGRAD_ROW_CANDIDATES)
    tk = min(WGRAD_TOK_TILE, s_len)
    n_k = s_len // tk
    slabs = _slabs(tk, BWD_ROW_SLAB)
    chunked = n_k == 1

    def body(df_in, a_ref, b_ref, h_in, wd_ref, da_ref, db_ref, dwd_ref, dwg_ref, dwu_ref,
             s_ref, acc_d, acc_g, acc_u, *resident):
        k = pl.program_id(1)
        first = pl.program_id(0) == 0
        if chunked:
            df_ref, h_ref, load_sems = resident

            def load(j, which):
                r0, r1 = slabs[j]
                src, dst = ((df_in, df_ref), (h_in, h_ref))[which]
                return pltpu.make_async_copy(src.at[pl.ds(r0, r1 - r0)], dst.at[pl.ds(r0, r1 - r0)],
                                             load_sems.at[which, j])

            @pl.when(first)
            def _():
                for j in range(len(slabs)):
                    load(j, 0).start()
                    load(j, 1).start()
        else:
            df_ref, h_ref = df_in, h_in

        @pl.when(k == 0)
        def _():
            acc_d[...] = jnp.zeros_like(acc_d)
            acc_g[...] = jnp.zeros_like(acc_g)
            acc_u[...] = jnp.zeros_like(acc_u)

        wdv = wd_ref[...]
        for j, (r0, r1) in enumerate(slabs):
            if chunked:
                @pl.when(first)
                def _():
                    load(j, 0).wait()
                    load(j, 1).wait()
            ds = _dot_nt(df_ref[r0:r1, :], wdv)
            av = a_ref[r0:r1, :].astype(F32)
            bv = b_ref[r0:r1, :].astype(F32)
            sig = jax.nn.sigmoid(av)
            silu = av * sig
            s_ref[r0:r1, :] = (silu * bv).astype(LOW)
            da_ref[r0:r1, :] = (ds * bv * (sig * (1.0 + av * (1.0 - sig)))).astype(LOW)
            db_ref[r0:r1, :] = (ds * silu).astype(LOW)
            hv = h_ref[r0:r1, :]
            acc_d[...] += _dot_tn(s_ref[r0:r1, :], df_ref[r0:r1, :])
            acc_g[...] += _dot_tn(da_ref[r0:r1, :], hv)
            acc_u[...] += _dot_tn(db_ref[r0:r1, :], hv)

        @pl.when(k == n_k - 1)
        def _():
            dwd_ref[...] = acc_d[...].astype(LOW)
            dwg_ref[...] = acc_g[...].astype(LOW)
            dwu_ref[...] = acc_u[...].astype(LOW)

    hid = pl.BlockSpec((tk, tm), lambda i, k: (k, i))
    tok = pl.BlockSpec(memory_space=pl.ANY) if chunked else pl.BlockSpec((tk, d), lambda i, k: (k, 0))
    wrow = pl.BlockSpec((tm, d), lambda i, k: (i, 0))
    resident = [pltpu.VMEM((tk, d), LOW)] * 2 + [pltpu.SemaphoreType.DMA((2, len(slabs)))] if chunked else []
    return _call(
        body,
        name=name,
        grid=(f_len // tm, n_k),
        in_specs=[tok, hid, hid, tok, wrow],
        out_specs=[hid, hid, wrow, wrow, wrow],
        out_shape=[jax.ShapeDtypeStruct((s_len, f_len), LOW)] * 2 + [jax.ShapeDtypeStruct((f_len, d), LOW)] * 3,
        scratch_shapes=[pltpu.VMEM((tk, tm), LOW)] + [pltpu.VMEM((tm, d), F32)] * 3 + resident,
        args=(df, a, b, h, wd),
        exchange=exchange,
    )


def _ffn_dx(dxo, x, g, parts, name, exchange=None):
    s_len, d = x.shape
    ts = min(TOK_TILE, s_len)
    n_p = len(parts)

    def body(dxo_ref, x_ref, g_ref, *refs):
        dxi_ref, dg_ref = refs[4 * n_p:]

        @pl.when(pl.program_id(0) == 0)
        def _():
            dg_ref[...] = jnp.zeros_like(dg_ref)

        dh = None
        for p in range(n_p):
            da_ref, db_ref, wg_ref, wu_ref = refs[4 * p:4 * p + 4]
            part = _dot(da_ref[...], wg_ref[...]) + _dot(db_ref[...], wu_ref[...])
            dh = part if dh is None else dh + part
        dx, dgp = _rms_bwd(dh, x_ref[...], g_ref[...])
        dxi_ref[...] = dxo_ref[...] + dx
        dg_ref[...] += dgp

    tok = pl.BlockSpec((ts, d), lambda t: (t, 0))
    vec = pl.BlockSpec((1, d), lambda t: (0, 0))
    part_specs, part_args = [], []
    for da, db, wgt, wut in parts:
        hid = pl.BlockSpec((ts, da.shape[1]), lambda t: (t, 0))
        part_specs += [hid, hid, _resident(wgt), _resident(wut)]
        part_args += [da, db, wgt, wut]
    return _call(
        body,
        name=name,
        grid=(s_len // ts,),
        in_specs=[tok, tok, vec] + part_specs,
        out_specs=[tok, vec],
        out_shape=[jax.ShapeDtypeStruct((s_len, d), F32), jax.ShapeDtypeStruct((1, d), F32)],
        args=(dxo, x, g, *part_args),
        exchange=exchange,
    )


def _wgrad_tn(xm, ym, tn, stacked, name, exchange=None):
    s_len, m = xm.shape
    n = ym.shape[1]
    tk = min(WGRAD_TOK_TILE, s_len)
    n_k = s_len // tk

    def body(x_ref, y_ref, o_ref, acc):
        k = pl.program_id(1)

        @pl.when(k == 0)
        def _():
            acc[...] = jnp.zeros_like(acc)

        acc[...] += _dot_tn(x_ref[...].astype(LOW), y_ref[...].astype(LOW))

        @pl.when(k == n_k - 1)
        def _():
            o_ref[...] = acc[...].astype(LOW)

    if stacked:
        out_spec = pl.BlockSpec((None, m, tn), lambda j, k: (j, 0, 0))
        out_shape = jax.ShapeDtypeStruct((n // tn, m, tn), LOW)
    else:
        out_spec = pl.BlockSpec((m, tn), lambda j, k: (0, j))
        out_shape = jax.ShapeDtypeStruct((m, n), LOW)
    return _call(
        body,
        name=name,
        grid=(n // tn, n_k),
        in_specs=[pl.BlockSpec((tk, m), lambda j, k: (k, 0)), pl.BlockSpec((tk, tn), lambda j, k: (k, j))],
        out_specs=[out_spec],
        out_shape=[out_shape],
        scratch_shapes=[pltpu.VMEM((m, tn), F32)],
        args=(xm, ym),
        exchange=exchange,
    )


def _mix_parts(ext_ref, cw, ts, row0, with_pooled=True):
    dc = D_CONV

    def back(off, c0, c1):
        return ext_ref[HALO - off:HALO - off + ts, c0:c1]

    v, gate_b, gate_c = back(0, 0, dc), back(0, dc, 2 * dc), back(0, 2 * dc, 3 * dc)
    z0 = gate_c * v
    z1 = back(1, 2 * dc, 3 * dc) * back(1, 0, dc)
    z2 = back(2, 2 * dc, 3 * dc) * back(2, 0, dc)
    conv = cw[2:3, :] * z0 + cw[1:2, :] * z1 + cw[0:1, :] * z2
    rows = row0 + lax.broadcasted_iota(jnp.int32, (ts, 1), 0)
    pooled, inv_count = [], []
    for grp, w in enumerate(POOL_WINDOWS):
        inv = 1.0 / jnp.minimum(rows + 1, w).astype(F32)
        inv_count.append(inv)
        if with_pooled:
            c0 = 3 * dc + POOL_GC * grp
            u = back(0, c0, c0 + POOL_GC)
            acc = u
            for j in range(1, w):
                acc = acc + back(j, c0, c0 + POOL_GC)
            pooled.append(acc * inv - u)
    return v, gate_b, gate_c, z0, z1, z2, conv, pooled, inv_count


def _mix_fwd(x, g, w_in, conv_w, pool_w, pool_scale, w_out, name, exchange=None):
    s_len, d = x.shape
    n_blk, _, wcols = w_in.shape
    p_len = n_blk * wcols
    d_mix = w_out.shape[0]
    ts = min(MIX_TOK_TILE, s_len)
    dc = D_CONV

    def body(x_ref, g_ref, win_ref, cw_ref, pw_ref, ps_ref, wout_ref, x2_ref, h_ref, proj_ref, pooled_ref,
             ext_ref, cat_ref):
        t = pl.program_id(0)

        @pl.when(t == 0)
        def _():
            ext_ref[0:HALO, :] = jnp.zeros((HALO, p_len), F32)

        xv = x_ref[...]
        hb = (xv * _rms_scale(xv) * g_ref[...]).astype(LOW)
        h_ref[...] = hb
        for k in range(n_blk):
            ext_ref[HALO:HALO + ts, wcols * k:wcols * (k + 1)] = _dot(hb, win_ref[k])
        proj_ref[...] = ext_ref[HALO:HALO + ts, :]

        _, gate_b, _, _, _, _, conv, pooled, _ = _mix_parts(ext_ref, cw_ref[...], ts, t * ts)
        cat_ref[:, 0:dc] = (gate_b * conv).astype(LOW)
        for grp in range(len(POOL_WINDOWS)):
            c0 = POOL_GC * grp
            pooled_b = pooled[grp].astype(LOW)
            pooled_ref[:, c0:c0 + POOL_GC] = pooled_b
            lin = _dot(pooled_b, pw_ref[grp])
            cat_ref[:, dc + c0:dc + c0 + POOL_GC] = (lin * ps_ref[:, c0:c0 + POOL_GC]).astype(LOW)
        x2_ref[...] = xv + _dot(cat_ref[...], wout_ref[...])
        ext_ref[0:HALO, :] = ext_ref[ts:ts + HALO, :]

    tok = pl.BlockSpec((ts, d), lambda t: (t, 0))

    def whole(arr):
        return pl.BlockSpec(arr.shape, lambda t: (0,) * arr.ndim)

    return _call(
        body,
        name=name,
        grid=(s_len // ts,),
        in_specs=[tok, whole(g), _resident(w_in), whole(conv_w), whole(pool_w), whole(pool_scale), _resident(w_out)],
        out_specs=[tok, tok, pl.BlockSpec((ts, p_len), lambda t: (t, 0)),
                   pl.BlockSpec((ts, d_mix - dc), lambda t: (t, 0))],
        out_shape=[
            jax.ShapeDtypeStruct((s_len, d), F32),
            jax.ShapeDtypeStruct((s_len, d), LOW),
            jax.ShapeDtypeStruct((s_len, p_len), F32),
            jax.ShapeDtypeStruct((s_len, d_mix - dc), LOW),
        ],
        scratch_shapes=[pltpu.VMEM((ts + HALO, p_len), F32), pltpu.VMEM((ts, d_mix), LOW)],
        args=(x, g, w_in, conv_w, pool_w, pool_scale, w_out),
        exchange=exchange,
    )


def _mix_bwd(dx2, x, proj, pooled, g, w_in, conv_w, pool_w, pool_scale, w_out, name, exchange=None):
    s_len, d = x.shape
    n_blk, _, wcols = w_in.shape
    p_len = n_blk * wcols
    d_mix = w_out.shape[0]
    ts = min(MIX_TOK_TILE, s_len)
    n_t = s_len // ts
    dc = D_CONV
    n_grp = len(POOL_WINDOWS)

    def body(dx2_ref, x_ref, proj_ref, halo_ref, pooled_ref, g_ref, win_ref, cw_ref, pw_ref, ps_ref, wout_ref,
             dx_ref, dproj_ref, dwout_ref, dg_ref, dcw_ref, dpw_ref, dps_ref, df_ref,
             ext_ref, fut_ref, cat_ref, dwout_acc, win_a, win_b):
        i = pl.program_id(0)
        t = n_t - 1 - i

        @pl.when(i == 0)
        def _():
            win_a[...] = jnp.zeros_like(win_a)
            win_b[...] = jnp.zeros_like(win_b)
            dwout_acc[...] = jnp.zeros_like(dwout_acc)
            dg_ref[...] = jnp.zeros_like(dg_ref)
            dcw_ref[...] = jnp.zeros_like(dcw_ref)
            dpw_ref[...] = jnp.zeros_like(dpw_ref)
            dps_ref[...] = jnp.zeros_like(dps_ref)
            fut_ref[ts:ts + 2 * HALO, :] = jnp.zeros((2 * HALO, d_mix), F32)

        ext_ref[HALO:HALO + ts, :] = proj_ref[...]

        @pl.when(t == 0)
        def _():
            ext_ref[0:HALO, :] = jnp.zeros((HALO, p_len), F32)

        @pl.when(t > 0)
        def _():
            ext_ref[0:HALO, :] = halo_ref[...]

        cw = cw_ref[...]
        v, gate_b, gate_c, z0, z1, z2, conv, _, inv_count = _mix_parts(ext_ref, cw, ts, t * ts, with_pooled=False)
        dx2 = dx2_ref[...]
        dcat = _dot_nt(dx2.astype(LOW), wout_ref[...])

        dy_a = dcat[:, 0:dc]
        dconv = dy_a * gate_b
        fut_ref[0:ts, 0:dc] = dconv
        cat_ref[:, 0:dc] = (gate_b * conv).astype(LOW)
        dproj_ref[:, dc:2 * dc] = (dy_a * conv).astype(LOW)
        dcw_ref[2:3, :] += jnp.sum(dconv * z0, axis=0, keepdims=True)
        dcw_ref[1:2, :] += jnp.sum(dconv * z1, axis=0, keepdims=True)
        dcw_ref[0:1, :] += jnp.sum(dconv * z2, axis=0, keepdims=True)

        dpool = []
        for grp in range(n_grp):
            c0 = POOL_GC * grp
            pooled_b = pooled_ref[:, c0:c0 + POOL_GC]
            lin = _dot(pooled_b, pw_ref[grp])
            dy_b = dcat[:, dc + c0:dc + c0 + POOL_GC]
            scale = ps_ref[:, c0:c0 + POOL_GC]
            cat_ref[:, dc + c0:dc + c0 + POOL_GC] = (lin * scale).astype(LOW)
            dps_ref[:, c0:c0 + POOL_GC] += jnp.sum(dy_b * lin, axis=0, keepdims=True)
            dlin = (dy_b * scale).astype(LOW)
            dpw_ref[grp] += _dot_tn(pooled_b, dlin)
            dpool.append(_dot_nt(dlin, pw_ref[grp]))
            fut_ref[0:ts, dc + c0:dc + c0 + POOL_GC] = dpool[grp] * inv_count[grp]
        dwout_acc[...] += _dot_tn(cat_ref[...], dx2.astype(LOW))

        def ahead(off, c0, c1):
            return fut_ref[off:off + ts, c0:c1]

        dz = cw[2:3, :] * ahead(0, 0, dc) + cw[1:2, :] * ahead(1, 0, dc) + cw[0:1, :] * ahead(2, 0, dc)
        dproj_ref[:, 0:dc] = (dz * gate_c).astype(LOW)
        dproj_ref[:, 2 * dc:3 * dc] = (dz * v).astype(LOW)
        rows_ext = ts + HALO
        for grp, w in enumerate(POOL_WINDOWS):
            c0 = dc + POOL_GC * grp
            src, dst, other = fut_ref, win_a, win_b
            cols = slice(c0, c0 + POOL_GC)
            span = 1
            while span < w:
                dst[0:rows_ext, :] = src[0:rows_ext, cols] + src[span:span + rows_ext, cols]
                src, dst, other, cols = dst, other, dst, slice(0, POOL_GC)
                span *= 2
            dproj_ref[:, 2 * dc + c0:2 * dc + c0 + POOL_GC] = (src[0:ts, cols] - dpool[grp]).astype(LOW)

        dh = _dot_nt(dproj_ref[:, 0:wcols], win_ref[0])
        for k in range(1, n_blk):
            dh += _dot_nt(dproj_ref[:, wcols * k:wcols * (k + 1)], win_ref[k])
        dx, dgp = _rms_bwd(dh, x_ref[...], g_ref[...])
        dx = dx2 + dx
        dx_ref[...] = dx
        df_ref[...] = (0.5 * dx).astype(LOW)
        dg_ref[...] += dgp
        fut_ref[ts:ts + HALO, :] = fut_ref[0:HALO, :]

        @pl.when(i == n_t - 1)
        def _():
            dwout_ref[...] = dwout_acc[...].astype(LOW)

    tok = pl.BlockSpec((ts, d), lambda i: (n_t - 1 - i, 0))
    halo = pl.BlockSpec((HALO, p_len), lambda i: (jnp.maximum((n_t - 1 - i) * (ts // HALO) - 1, 0), 0))

    def whole(arr):
        return pl.BlockSpec(arr.shape, lambda i: (0,) * arr.ndim)

    return _call(
        body,
        name=name,
        grid=(n_t,),
        in_specs=[tok, tok, pl.BlockSpec((ts, p_len), lambda i: (n_t - 1 - i, 0)), halo,
                  pl.BlockSpec((ts, d_mix - dc), lambda i: (n_t - 1 - i, 0)),
                  whole(g), _resident(w_in), whole(conv_w), whole(pool_w), whole(pool_scale), _resident(w_out)],
        out_specs=[tok, pl.BlockSpec((ts, p_len), lambda i: (n_t - 1 - i, 0)),
                   whole(w_out),
                   whole(g), whole(conv_w), whole(pool_w), whole(pool_scale), tok],
        out_shape=[
            jax.ShapeDtypeStruct((s_len, d), F32),
            jax.ShapeDtypeStruct((s_len, p_len), LOW),
            jax.ShapeDtypeStruct((d_mix, d), LOW),
            jax.ShapeDtypeStruct(g.shape, F32),
            jax.ShapeDtypeStruct(conv_w.shape, F32),
            jax.ShapeDtypeStruct(pool_w.shape, F32),
            jax.ShapeDtypeStruct(pool_scale.shape, F32),
            jax.ShapeDtypeStruct((s_len, d), LOW),
        ],
        scratch_shapes=[pltpu.VMEM((ts + HALO, p_len), F32), pltpu.VMEM((ts + 2 * HALO, d_mix), F32),
                        pltpu.VMEM((ts, d_mix), LOW), pltpu.VMEM((d_mix, d), F32),
                        pltpu.VMEM((ts + 2 * HALO, POOL_GC), F32), pltpu.VMEM((ts + 2 * HALO, POOL_GC), F32)],
        args=(dx2, x, proj, proj, pooled, g, w_in, conv_w, pool_w, pool_scale, w_out),
        exchange=exchange,
    )


def _ffn_fwd_loss(x, h, wgt, wut, wd, g, target, name):
    s_len, d = x.shape
    fc = wd.shape[0]
    ts = min(TOK_TILE, s_len)

    def body(x_ref, h_ref, wg_in, wu_in, wd_in, g_ref, tgt_ref,
             a_ref, b_ref, loss_ref, dx_ref, dg_ref, df_ref, s_ref, wg_ref, wu_ref, wd_ref, w_sems):
        wait_slab, wait_down = _weight_loads(wg_in, wu_in, wd_in, wg_ref, wu_ref, wd_ref, w_sems, _slabs(fc))

        @pl.when(pl.program_id(0) == 0)
        def _():
            loss_ref[...] = jnp.zeros_like(loss_ref)
            dg_ref[...] = jnp.zeros_like(dg_ref)

        hb = h_ref[...]
        for j, (c0, c1) in enumerate(_slabs(fc)):
            wait_slab(j)
            a = _dot_nt(hb, wg_ref[c0:c1, :])
            b = _dot_nt(hb, wu_ref[c0:c1, :])
            s_ref[:, c0:c1] = (a * jax.nn.sigmoid(a) * b).astype(LOW)
            a_ref[:, c0:c1] = a.astype(LOW)
            b_ref[:, c0:c1] = b.astype(LOW)
        wait_down()
        xv = x_ref[...] + 0.5 * _dot(s_ref[...], wd_ref[...])
        gv = g_ref[...]
        err = xv * _rms_scale(xv) * gv - tgt_ref[...]
        loss_ref[...] += 0.5 * jnp.sum(jnp.mean(err * err, axis=-1, keepdims=True), axis=0, keepdims=True)
        dx, dgp = _rms_bwd(err * (1.0 / d), xv, gv)
        dx_ref[...] = dx
        df_ref[...] = (0.5 * dx).astype(LOW)
        dg_ref[...] += dgp

    tok = pl.BlockSpec((ts, d), lambda t: (t, 0))
    vec = pl.BlockSpec((1, d), lambda t: (0, 0))
    hid = pl.BlockSpec((ts, fc), lambda t: (t, 0))
    return pl.pallas_call(
        body,
        name=name,
        grid=(s_len // ts,),
        in_specs=[tok, tok] + [pl.BlockSpec(memory_space=pl.ANY)] * 3 + [vec, tok],
        out_specs=[hid, hid, pl.BlockSpec((1, 128), lambda t: (0, 0)), tok, vec, tok],
        out_shape=[
            jax.ShapeDtypeStruct((s_len, fc), LOW),
            jax.ShapeDtypeStruct((s_len, fc), LOW),
            jax.ShapeDtypeStruct((1, 128), F32),
            jax.ShapeDtypeStruct((s_len, d), F32),
            jax.ShapeDtypeStruct((1, d), F32),
            jax.ShapeDtypeStruct((s_len, d), LOW),
        ],
        scratch_shapes=[pltpu.VMEM((ts, fc), LOW)] + _weight_scratch(wgt, wut, wd),
        compiler_params=_params("arbitrary"),
    )(x, h, wgt, wut, wd, g, target)


def _row_tile(rows, cols, stack_bytes):
    budget = 20 * 1024 * 1024
    per_row = cols * (4 * 7 + stack_bytes)
    for tr in (rows, 512, 256, 176, 128, 64, 32, 16, 8):
        if rows % tr == 0 and tr % 8 == 0 and tr * per_row * 2 <= budget:
            return tr
    return rows


def _sum_stack(stack, name):
    n, r, c = stack.shape
    tr = _row_tile(r, c, n * stack.dtype.itemsize)

    def body(s_ref, o_ref):
        acc = s_ref[0].astype(F32)
        for k in range(1, n):
            acc = acc + s_ref[k].astype(F32)
        o_ref[...] = acc

    return pl.pallas_call(
        body,
        name=name,
        grid=(r // tr,),
        in_specs=[pl.BlockSpec((n, tr, c), lambda i: (0, i, 0))],
        out_specs=pl.BlockSpec((tr, c), lambda i: (i, 0)),
        out_shape=jax.ShapeDtypeStruct((r, c), F32),
        compiler_params=_params("arbitrary"),
    )(stack)


def _adamw_many(params, name):
    params = [(list(st) if isinstance(st, (list, tuple)) else [st], w, m, v) for st, w, m, v in params]
    stacks0, w0 = params[0][0], params[0][1]
    r, c = w0.shape
    n = stacks0[0].shape[0]
    n_st = len(stacks0)
    part_rows = [st.shape[1] for st in stacks0]
    assert sum(part_rows) == r
    assert all(w.shape == (r, c) and [st.shape for st in sts] == [st.shape for st in stacks0] for sts, w, _, _ in params)
    first_row = [sum(part_rows[:j]) for j in range(n_st)]
    tc = next((t for t in (512, 256, 128) if c % t == 0), c)
    c1 = 1.0 - ADAM_B1 ** ADAM_STEP
    c2 = 1.0 - ADAM_B2 ** ADAM_STEP
    n_in = n_st + 3

    def body(*refs):
        ins, outs = refs[:n_in * len(params)], refs[n_in * len(params):]
        for p in range(len(params)):
            s_refs = ins[n_in * p:n_in * p + n_st]
            w_ref, m_ref, v_ref = ins[n_in * p + n_st:n_in * (p + 1)]
            g_ref, d_ref, mo_ref, vo_ref = outs[4 * p:4 * p + 4]
            for s_ref, r0, nr in zip(s_refs, first_row, part_rows):
                gv = s_ref[0].astype(F32)
                for k in range(1, n):
                    gv = gv + s_ref[k].astype(F32)
                mn = ADAM_B1 * m_ref[r0:r0 + nr, :] + (1.0 - ADAM_B1) * gv
                vn = ADAM_B2 * v_ref[r0:r0 + nr, :] + (1.0 - ADAM_B2) * (gv * gv)
                g_ref[r0:r0 + nr, :] = gv
                mo_ref[r0:r0 + nr, :] = mn
                vo_ref[r0:r0 + nr, :] = vn
                d_ref[r0:r0 + nr, :] = -ADAM_LR * ((mn / c1) / (jnp.sqrt(vn / c2) + ADAM_EPS)
                                                   + ADAM_WD * w_ref[r0:r0 + nr, :])

    blk = pl.BlockSpec((r, tc), lambda i: (0, i))
    one_in = [pl.BlockSpec((n, nr, tc), lambda i: (0, 0, i)) for nr in part_rows] + [blk, blk, blk]
    res = pl.pallas_call(
        body,
        name=name,
        grid=(c // tc,),
        in_specs=one_in * len(params),
        out_specs=[blk] * (4 * len(params)),
        out_shape=[jax.ShapeDtypeStruct((r, c), F32)] * (4 * len(params)),
        compiler_params=_params("arbitrary"),
    )(*[arr for sts, w, m, v in params for arr in (*sts, w, m, v)])
    return [tuple(res[4 * p:4 * p + 4]) for p in range(len(params))]


def _adamw(stacks, w, m, v, name):
    return _adamw_many([(stacks, w, m, v)], name)[0]


def _to_sheet(parts):
    sheets, spans = [], []
    row = 0
    for p in parts:
        flat = p.reshape(-1).astype(F32)
        rows = -(-flat.shape[0] // 1024) * 8
        flat = jnp.pad(flat, (0, rows * 128 - flat.shape[0]))
        sheets.append(flat.reshape(rows, 128))
        spans.append((row, p.size, p.shape))
        row += rows
    return jnp.concatenate(sheets, axis=0), spans


def _from_sheet(sheet, spans):
    out = []
    for row, size, shape in spans:
        rows = -(-size // 1024) * 8
        out.append(sheet[row:row + rows].reshape(-1)[:size].reshape(shape))
    return out


def kernel(x, norm_ffn1, ffn1_w_gate, ffn1_w_up, ffn1_w_down, norm_mix, w_in, conv_w, pool_w, pool_scale, w_out, norm_ffn2, ffn2_w_gate, ffn2_w_up, ffn2_w_down, norm_final, loss_target, m_norm_ffn1, m_ffn1_w_gate, m_ffn1_w_up, m_ffn1_w_down, m_norm_mix, m_w_in, m_conv_w, m_pool_w, m_pool_scale, m_w_out, m_norm_ffn2, m_ffn2_w_gate, m_ffn2_w_up, m_ffn2_w_down, m_norm_final, v_norm_ffn1, v_ffn1_w_gate, v_ffn1_w_up, v_ffn1_w_down, v_norm_mix, v_w_in, v_conv_w, v_pool_w, v_pool_scale, v_w_out, v_norm_ffn2, v_ffn2_w_gate, v_ffn2_w_up, v_ffn2_w_down, v_norm_final):
    me = 4 * lax.axis_index("x") + 2 * lax.axis_index("y") + lax.axis_index("c")
    xs, tgt = x[0], loss_target[0]
    s_len, d = xs.shape
    f_shard = ffn1_w_down.shape[1]
    conv_shard = conv_w.shape[2]

    def low_t(wt):
        return wt[0].T.astype(LOW)

    def by_dev(gw):
        return gw.reshape(N_DEV, -1, d)

    conv_tile = jnp.zeros((8, 128), F32).at[0:conv_w.shape[1], 0:conv_shard].set(conv_w[0])
    pool_w_low = pool_w[0].astype(LOW)

    rows_a = -(-f_shard // 64) * 32

    def parts_of(w_gate, w_up, w_down):
        shards = [low_t(w_gate), low_t(w_up), w_down[0].astype(LOW)]
        return [(s, 0, rows_a) for s in shards], [(s, rows_a, f_shard - rows_a) for s in shards]

    def rows_flat(stacks):
        return [st.reshape(-1, d) for st in stacks]

    def gather(shards):
        return _Exchange(shards, [False] * len(shards), relay_at=0.6)

    def scatter(pairs):
        return _Exchange(pairs, [True] * len(pairs), among_chips=True)

    w1a_shards, w1b_shards = parts_of(ffn1_w_gate, ffn1_w_up, ffn1_w_down)
    w2a_shards, w2b_shards = parts_of(ffn2_w_gate, ffn2_w_up, ffn2_w_down)

    wg1a, wu1a, wd1a = rows_flat(_exchange(w1a_shards, [False] * 3, "gather_ffn1_a", relay=True))
    xa, h1, a1a, b1a, *w1b = _ffn_fwd(xs, norm_ffn1, None, wg1a, wu1a, wd1a, "ffn1_fwd_a",
                                      exchange=gather(w1b_shards))
    wg1b, wu1b, wd1b = rows_flat(w1b)
    x1, a1b, b1b, w_in_full, w_out_full, conv_tiles = _ffn_fwd(
        xa, None, h1, wg1b, wu1b, wd1b, "ffn1_fwd_b",
        exchange=gather([w_in[0].astype(LOW), w_out[0].astype(LOW), conv_tile]))
    w_out_full = w_out_full.reshape(-1, d)
    conv_full = jnp.concatenate([conv_tiles[k, 0:conv_w.shape[1], 0:conv_shard] for k in range(N_DEV)], axis=1)
    x2, h2, proj, pooled, *w2a = _mix_fwd(x1, norm_mix, w_in_full, conv_full, pool_w_low, pool_scale, w_out_full,
                                          "mix_fwd", exchange=gather(w2a_shards))
    wg2a, wu2a, wd2a = rows_flat(w2a)
    xb, h3, a2a, b2a, *w2b = _ffn_fwd(x2, norm_ffn2, None, wg2a, wu2a, wd2a, "ffn2_fwd_a",
                                      exchange=gather(w2b_shards))
    wg2b, wu2b, wd2b = rows_flat(w2b)
    a2b, b2b, loss_row, dx3, dg_final, df3 = _ffn_fwd_loss(
        xb, h3, wg2b, wu2b, wd2b, norm_final.reshape(1, d), tgt, "ffn2_fwd_b_loss")

    da2a, db2a, dwd_a, dwg_a, dwu_a = _ffn_bwdw(df3, a2a, b2a, h3, wd2a, "ffn2_bwdw_a")
    da2b, db2b, dwd_b, dwg_b, dwu_b = _ffn_bwdw(df3, a2b, b2b, h3, wd2b, "ffn2_bwdw_b")
    pairs = _pair_sum([by_dev(dwg_a), by_dev(dwu_a), by_dev(dwd_a), by_dev(dwg_b), by_dev(dwu_b), by_dev(dwd_b)],
                      "pair_sum_ffn2")
    dx2, dg_ffn2, *got_2b = _ffn_dx(dx3, x2, norm_ffn2, [(da2a, db2a, wg2a, wu2a), (da2b, db2b, wg2b, wu2b)],
                                    "ffn2_dx", exchange=scatter(pairs[3:]))
    dx1, dproj, dw_out, dg_mix, dconv, dpool_w, dpool_scale, df1, *got_2a = _mix_bwd(
        dx2, x1, proj, pooled, norm_mix, w_in_full, conv_full, pool_w_low, pool_scale, w_out_full, "mix_bwd",
        exchange=scatter(pairs[:3]))
    small_parts = [dg_mix, dg_ffn2, dg_final, dconv, dpool_w, dpool_scale, loss_row]
    small_sheet, spans = _to_sheet(small_parts)
    dw_in, got_small = _wgrad_tn(h2, dproj, W_IN_SHARD, True, "w_in_wgrad",
                                 exchange=_Exchange([small_sheet], [False], relay_at=0.5))
    pairs = _pair_sum([dw_in, by_dev(dw_out)], "pair_sum_mix")
    da1a, db1a, dwd, dwg, dwu, got_in, got_out = _ffn_bwdw(
        df1, a1a, b1a, h1, wd1a, "ffn1_bwdw_a", exchange=scatter(pairs))
    pairs = _pair_sum([by_dev(dwg), by_dev(dwu), by_dev(dwd)], "pair_sum_ffn1_a")
    da1b, db1b, dwd, dwg, dwu, *got_1a = _ffn_bwdw(df1, a1b, b1b, h1, wd1b, "ffn1_bwdw_b", exchange=scatter(pairs))
    pairs = _pair_sum([by_dev(dwg), by_dev(dwu), by_dev(dwd)], "pair_sum_ffn1_b")
    dx0, dg_ffn1, *got_1b = _ffn_dx(dx1, xs, norm_ffn1, [(da1a, db1a, wg1a, wu1a), (da1b, db1b, wg1b, wu1b)],
                                    "ffn1_dx", exchange=scatter(pairs))
    (got_n1,) = _exchange([dg_ffn1.reshape(8, 128)], [False], "gather_dnorm_ffn1")

    outs = {}

    def update(name, stacks, w, m, v):
        outs[name] = _adamw(stacks, w[0], m[0], v[0], "adamw_" + name)

    def update_ffn(prefix, got_a, got_b, gate, up, down):
        res = _adamw_many(
            [([got_a[j], got_b[j]], *[(t[0].T if j < 2 else t[0]) for t in wmv]) for j, wmv in enumerate((gate, up, down))],
            "adamw_" + prefix)
        outs[prefix + "_w_gate"] = tuple(r.T for r in res[0])
        outs[prefix + "_w_up"] = tuple(r.T for r in res[1])
        outs[prefix + "_w_down"] = res[2]

    update_ffn("ffn1", got_1a, got_1b, (ffn1_w_gate, m_ffn1_w_gate, v_ffn1_w_gate),
               (ffn1_w_up, m_ffn1_w_up, v_ffn1_w_up), (ffn1_w_down, m_ffn1_w_down, v_ffn1_w_down))
    update_ffn("ffn2", got_2a, got_2b, (ffn2_w_gate, m_ffn2_w_gate, v_ffn2_w_gate),
               (ffn2_w_up, m_ffn2_w_up, v_ffn2_w_up), (ffn2_w_down, m_ffn2_w_down, v_ffn2_w_down))
    update("w_in", got_in, w_in, m_w_in, v_w_in)
    update("w_out", got_out, w_out, m_w_out, v_w_out)

    g_small = _from_sheet(_sum_stack(got_small, "sum_small"), spans)
    g_norm_ffn1 = _sum_stack(got_n1, "sum_dnorm_ffn1").reshape(norm_ffn1.shape)
    g_conv = lax.dynamic_slice_in_dim(g_small[3], me * conv_shard, conv_shard, axis=1)
    row = (1, d)
    norms = [("norm_ffn1", g_norm_ffn1, norm_ffn1, m_norm_ffn1, v_norm_ffn1),
             ("norm_mix", g_small[0], norm_mix, m_norm_mix, v_norm_mix),
             ("norm_ffn2", g_small[1], norm_ffn2, m_norm_ffn2, v_norm_ffn2),
             ("norm_final", g_small[2], norm_final, m_norm_final, v_norm_final)]
    res = _adamw_many([(gn.reshape((1,) + row), wn.reshape(row), mn.reshape(row), vn.reshape(row))
                       for _, gn, wn, mn, vn in norms], "adamw_norms")
    for (nm, _, wn, _, _), r4 in zip(norms, res):
        outs[nm] = tuple(r.reshape(wn.shape) for r in r4)
    for nm, gn, wn, mn, vn in (("conv_w", g_conv, conv_w, m_conv_w, v_conv_w),
                               ("pool_w", g_small[4], pool_w, m_pool_w, v_pool_w),
                               ("pool_scale", g_small[5], pool_scale, m_pool_scale, v_pool_scale)):
        flat = (-1, wn.shape[-1])
        r4 = _adamw(gn.reshape(flat)[None], wn.reshape(flat), mn.reshape(flat), vn.reshape(flat), "adamw_" + nm)
        outs[nm] = tuple(r.reshape(wn.shape) for r in r4)

    loss = g_small[6][0, 0]
    order = ["norm_ffn1", "ffn1_w_gate", "ffn1_w_up", "ffn1_w_down", "norm_mix", "w_in", "conv_w", "pool_w",
             "pool_scale", "w_out", "norm_ffn2", "ffn2_w_gate", "ffn2_w_up", "ffn2_w_down", "norm_final"]
    big = {"ffn1_w_gate", "ffn1_w_up", "ffn1_w_down", "w_in", "w_out", "ffn2_w_gate", "ffn2_w_up", "ffn2_w_down"}

    def leaf(nm, j):
        val = outs[nm][j]
        return val[None] if nm in big else val

    return (loss, dx0[None],
            *[leaf(nm, 0) for nm in order], *[leaf(nm, 1) for nm in order],
            *[leaf(nm, 2) for nm in order], *[leaf(nm, 3) for nm in order])
```

```python
import jax
import jax.numpy as jnp
from jax import lax
from jax.experimental import pallas as pl
from jax.experimental.pallas import tpu as pltpu

F32 = jnp.float32
LOW = jnp.bfloat16

N_DEV = 8
EPS = 1e-6
D_CONV = 512
POOL_WINDOWS = (2, 4, 8, 16)
POOL_GC = 128
HALO = 16
W_IN_SHARD = 256

ADAM_LR = 0.001
ADAM_B1 = 0.9
ADAM_B2 = 0.999
ADAM_EPS = 1e-08
ADAM_WD = 0.01
ADAM_STEP = 10

VMEM_LIMIT_BYTES = 56 * 1024 * 1024
TOK_TILE = 512
MIX_TOK_TILE = 512
WGRAD_TOK_TILE = 4096
WGRAD_ROW_CANDIDATES = (256, 128)
RELAY_LAST_LATER = 0.25
BWD_ROW_SLAB = 2048


def _params(*sem):
    return pltpu.CompilerParams(dimension_semantics=sem, vmem_limit_bytes=VMEM_LIMIT_BYTES)


def _resident(arr):
    return pl.BlockSpec(arr.shape, lambda *_: (0,) * arr.ndim, pipeline_mode=pl.Buffered(1))


def _pick(n, candidates):
    for c in candidates:
        if n % c == 0:
            return c
    raise ValueError(f"no tile in {candidates} divides {n}")


def _dot(a, b):
    return lax.dot_general(a, b, (((1,), (0,)), ((), ())), preferred_element_type=F32)


def _dot_nt(a, b):
    return lax.dot_general(a, b, (((1,), (1,)), ((), ())), preferred_element_type=F32)


def _dot_tn(a, b):
    return lax.dot_general(a, b, (((0,), (0,)), ((), ())), preferred_element_type=F32)


def _rms_scale(x):
    return lax.rsqrt(jnp.mean(x * x, axis=-1, keepdims=True) + EPS)


def _rms_bwd(dy, x, g):
    r = _rms_scale(x)
    xhat = x * r
    gdy = dy * g
    dx = r * (gdy - xhat * jnp.mean(gdy * xhat, axis=-1, keepdims=True))
    return dx, jnp.sum(dy * xhat, axis=0, keepdims=True)


COLLECTIVE_PAIR, COLLECTIVE_CHIPS, COLLECTIVE_RELAY = 0, 1, 2


def _handshake(peer_numbers):
    mx, my, mc = lax.axis_index("x"), lax.axis_index("y"), lax.axis_index("c")
    barrier = pltpu.get_barrier_semaphore()
    for m in peer_numbers:
        peer = (lax.rem(mx + ((m >> 2) & 1), 2), lax.rem(my + ((m >> 1) & 1), 2), lax.rem(mc + (m & 1), 2))
        pl.semaphore_signal(barrier, inc=1, device_id=peer, device_id_type=pl.DeviceIdType.MESH)
    pl.semaphore_wait(barrier, len(peer_numbers))


class _Exchange:
    CHIPS = (2, 4, 6)

    def __init__(self, arrays, sliced, relay_at=None, among_chips=False):
        assert relay_at is None or not any(sliced)
        assert not among_chips or (all(sliced) and relay_at is None)
        self.relay_at, self.among_chips = relay_at, among_chips
        self.peers = self.CHIPS if among_chips else ((1, 2, 4) if relay_at is not None else tuple(range(1, N_DEV)))
        self.collective_id = COLLECTIVE_CHIPS if among_chips else (COLLECTIVE_RELAY if relay_at is not None else None)
        self.rows = [arr[1:] if isinstance(arr, tuple) else None for arr in arrays]
        self.arrays = [arr[0] if isinstance(arr, tuple) else arr for arr in arrays]
        self.sliced, self.n = list(sliced), len(arrays)
        assert all(rg is None or not sl for rg, sl in zip(self.rows, sliced))
        self.block_shape = [arr.shape if rg is None else (rg[1],) + arr.shape[1:]
                            for arr, rg in zip(self.arrays, self.rows)]
        self.out_shape = [jax.ShapeDtypeStruct(shape if sl else (N_DEV,) + shape, arr.dtype)
                          for arr, shape, sl in zip(self.arrays, self.block_shape, sliced)]
        self.specs = [pl.BlockSpec(memory_space=pl.ANY)] * self.n
        self.scratch_shapes = [pltpu.SemaphoreType.DMA((self.n, N_DEV)),
                               pltpu.SemaphoreType.DMA((self.n, N_DEV)),
                               pltpu.SemaphoreType.DMA((self.n,))]

    HALF_VIA = ((4, 2, 5), (2, 4, 7))

    def _halves(self, a):
        rows = self.block_shape[a][0]
        if rows % 32:
            return ((0, rows), None)
        return ((0, rows // 2), (rows // 2, rows // 2))

    def _copies(self, ins, outs, sems):
        send_sems, recv_sems, local_sems = sems
        sliced = self.sliced
        mx, my, mc = lax.axis_index("x"), lax.axis_index("y"), lax.axis_index("c")
        me = 2 * mx + my if self.among_chips else 4 * mx + 2 * my + mc

        def peer(m):
            px = lax.rem(mx + ((m >> 2) & 1), 2)
            py = lax.rem(my + ((m >> 1) & 1), 2)
            pc = lax.rem(mc + (m & 1), 2)
            return (px, py, pc), (2 * px + py if self.among_chips else 4 * px + 2 * py + pc)

        def mine(a):
            return ins[a] if self.rows[a] is None else ins[a].at[pl.ds(*self.rows[a])]

        def remote(a, m, arriving):
            pid, pflat = peer(m)
            return pltpu.make_async_remote_copy(
                src_ref=ins[a].at[pflat] if sliced[a] else mine(a),
                dst_ref=outs[a].at[pflat if arriving else me],
                send_sem=send_sems.at[a, m - 1],
                recv_sem=recv_sems.at[a, m - 1],
                device_id=pid,
                device_id_type=pl.DeviceIdType.MESH,
            )

        def local(a):
            return pltpu.make_async_copy(ins[a].at[me] if sliced[a] else mine(a), outs[a].at[me], local_sems.at[a])

        def passed_on(a, m):
            _, origin = peer(m)
            sibling, _ = peer(1)
            return pltpu.make_async_remote_copy(
                src_ref=outs[a].at[origin],
                dst_ref=outs[a].at[origin],
                send_sem=send_sems.at[a, m],
                recv_sem=recv_sems.at[a, m],
                device_id=sibling,
                device_id_type=pl.DeviceIdType.MESH,
            )

        def half_on(a, h, arriving):
            via, to, column = self.HALF_VIA[h]
            r0, nr = self._halves(a)[h]
            _, origin = peer(6 if arriving else via)
            rows = outs[a].at[origin].at[pl.ds(r0, nr)]
            return pltpu.make_async_remote_copy(
                src_ref=rows, dst_ref=rows, send_sem=send_sems.at[a, column], recv_sem=recv_sems.at[a, column],
                device_id=peer(to)[0], device_id_type=pl.DeviceIdType.MESH)

        return remote, local, passed_on, half_on

    def start(self, ins, outs, sems):
        remote, local, _, _ = self._copies(ins, outs, sems)
        if self.collective_id is not None:
            _handshake(self.peers)
        for a in range(self.n):
            local(a).start()
        for m in self.peers:
            for a in range(self.n):
                remote(a, m, False).start()

    def relay(self, ins, outs, sems):
        remote, _, passed_on, half_on = self._copies(ins, outs, sems)
        for h, (via, _, _) in enumerate(self.HALF_VIA):
            for a in range(self.n):
                remote(a, via, True).wait_recv()
                passed_on(a, via).start()
                if self._halves(a)[h] is not None:
                    half_on(a, h, False).start()

    def relay_last(self, ins, outs, sems):
        _, _, passed_on, half_on = self._copies(ins, outs, sems)
        for a in range(self.n):
            for h in range(2):
                if self._halves(a)[h] is not None:
                    half_on(a, h, True).wait_recv()
            passed_on(a, 6).start()

    def wait(self, ins, outs, sems):
        remote, local, passed_on, half_on = self._copies(ins, outs, sems)
        if self.relay_at is None:
            for m in self.peers:
                for a in range(self.n):
                    remote(a, m, True).wait_recv()
            for m in self.peers:
                for a in range(self.n):
                    remote(a, m, False).wait_send()
        else:
            for m in (1, 3, 5, 7):
                for a in range(self.n):
                    remote(a, m, True).wait_recv()
            for m in self.peers:
                for a in range(self.n):
                    remote(a, m, False).wait_send()
            for a in range(self.n):
                for m in self.CHIPS:
                    passed_on(a, m).wait_send()
                for h in range(2):
                    if self._halves(a)[h] is not None:
                        half_on(a, h, False).wait_send()
        for a in range(self.n):
            local(a).wait()


def _pair_sum(stacks, name):
    n = len(stacks)
    n_chip = N_DEV // 2
    half = [(n_chip,) + st.shape[1:] for st in stacks]

    def body(*refs):
        ins, outs, mine, theirs = refs[:n], refs[n:2 * n], refs[2 * n:3 * n], refs[3 * n:4 * n]
        local_sems, send_sems, recv_sems = refs[4 * n:]
        mx, my, mc = lax.axis_index("x"), lax.axis_index("y"), lax.axis_index("c")

        def own(a, k):
            return pltpu.make_async_copy(ins[a].at[2 * k + mc], mine[a].at[k], local_sems.at[a, k])

        def swap(a, k):
            return pltpu.make_async_remote_copy(
                src_ref=ins[a].at[2 * k + (1 - mc)], dst_ref=theirs[a].at[k],
                send_sem=send_sems.at[a, k], recv_sem=recv_sems.at[a, k],
                device_id=(mx, my, 1 - mc), device_id_type=pl.DeviceIdType.MESH)

        _handshake((1,))
        for k in range(n_chip):
            for a in range(n):
                own(a, k).start()
                swap(a, k).start()
        for k in range(n_chip):
            for a in range(n):
                own(a, k).wait()
                swap(a, k).wait()
                outs[a][k] = (mine[a][k].astype(F32) + theirs[a][k].astype(F32)).astype(LOW)

    return pl.pallas_call(
        body, name=name,
        out_shape=[jax.ShapeDtypeStruct(h, LOW) for h in half],
        in_specs=[pl.BlockSpec(memory_space=pl.ANY)] * n,
        out_specs=[pl.BlockSpec(memory_space=pltpu.VMEM)] * n,
        scratch_shapes=([pltpu.VMEM(h, st.dtype) for h, st in zip(half, stacks)] * 2
                        + [pltpu.SemaphoreType.DMA((n, n_chip))] * 3),
        compiler_params=pltpu.CompilerParams(vmem_limit_bytes=VMEM_LIMIT_BYTES, collective_id=COLLECTIVE_PAIR),
    )(*stacks)


def _exchange(arrays, sliced, name, relay=False):
    ex = _Exchange(arrays, sliced, relay_at=0 if relay else None)

    def body(*refs):
        ins, outs, sems = refs[:ex.n], refs[ex.n:2 * ex.n], refs[2 * ex.n:]
        ex.start(ins, outs, sems)
        if relay:
            ex.relay(ins, outs, sems)
            ex.relay_last(ins, outs, sems)
        ex.wait(ins, outs, sems)

    return pl.pallas_call(body, name=name, out_shape=ex.out_shape, in_specs=ex.specs, out_specs=ex.specs,
                          scratch_shapes=ex.scratch_shapes,
                          compiler_params=pltpu.CompilerParams(collective_id=ex.collective_id))(*ex.arrays)


def _call(body, *, name, grid, in_specs, out_specs, out_shape, args, scratch_shapes=(), exchange=None):
    params = _params(*(("arbitrary",) * len(grid)))
    if exchange is None:
        return pl.pallas_call(body, name=name, grid=grid, in_specs=in_specs, out_specs=out_specs, out_shape=out_shape,
                              scratch_shapes=list(scratch_shapes), compiler_params=params)(*args)
    exs = list(exchange) if isinstance(exchange, (list, tuple)) else [exchange]
    assert len(exs) == 1 or all(ex.collective_id is None for ex in exs)
    params = pltpu.CompilerParams(dimension_semantics=("arbitrary",) * len(grid), vmem_limit_bytes=VMEM_LIMIT_BYTES,
                                  collective_id=exs[0].collective_id)
    n_in, n_out, n_scr = len(in_specs), len(out_specs), len(scratch_shapes)
    n_ex = sum(ex.n for ex in exs)
    n_steps = 1
    for g in grid:
        n_steps *= g

    def hosted(*refs):
        ins, refs = refs[:n_in], refs[n_in:]
        ex_ins, refs = refs[:n_ex], refs[n_ex:]
        outs, refs = refs[:n_out], refs[n_out:]
        ex_outs, refs = refs[:n_ex], refs[n_ex:]
        scr, sems = refs[:n_scr], refs[n_scr:]
        parts, at = [], 0
        for j, ex in enumerate(exs):
            parts.append((ex_ins[at:at + ex.n], ex_outs[at:at + ex.n], sems[3 * j:3 * j + 3]))
            at += ex.n
        step = pl.program_id(0)
        for ax in range(1, len(grid)):
            step = step * grid[ax] + pl.program_id(ax)

        @pl.when(step == 0)
        def _():
            for ex, part in zip(exs, parts):
                ex.start(*part)

        body(*ins, *outs, *scr)

        for ex, part in zip(exs, parts):
            if ex.relay_at is not None:
                @pl.when(step == min(int(ex.relay_at * n_steps), n_steps - 1))
                def _():
                    ex.relay(*part)

                @pl.when(step == min(int((ex.relay_at + RELAY_LAST_LATER) * n_steps), n_steps - 1))
                def _():
                    ex.relay_last(*part)

        @pl.when(step == n_steps - 1)
        def _():
            for ex, part in zip(exs, parts):
                ex.wait(*part)

    return pl.pallas_call(
        hosted, name=name, grid=grid,
        in_specs=list(in_specs) + [sp for ex in exs for sp in ex.specs],
        out_specs=list(out_specs) + [sp for ex in exs for sp in ex.specs],
        out_shape=list(out_shape) + [sh for ex in exs for sh in ex.out_shape],
        scratch_shapes=list(scratch_shapes) + [sc for ex in exs for sc in ex.scratch_shapes],
        compiler_params=params)(*args, *[arr for ex in exs for arr in ex.arrays])


def _ffn_fwd(x, g, h, wgt, wut, wd, name, exchange=None):
    s_len, d = x.shape
    fc = wd.shape[0]
    ts = min(TOK_TILE, s_len)
    first = h is None

    def body(*refs):
        x_ref, gh_ref, wg_ref, wu_ref, wd_ref, xo_ref = refs[:6]
        a_ref, b_ref, s_ref = refs[-3:]
        xv = x_ref[...]
        if first:
            hb = (xv * _rms_scale(xv) * gh_ref[...]).astype(LOW)
            refs[6][...] = hb
        else:
            hb = gh_ref[...]
        for c0, c1 in _slabs(fc):
            a = _dot_nt(hb, wg_ref[c0:c1, :])
            b = _dot_nt(hb, wu_ref[c0:c1, :])
            s_ref[:, c0:c1] = (a * jax.nn.sigmoid(a) * b).astype(LOW)
            a_ref[:, c0:c1] = a.astype(LOW)
            b_ref[:, c0:c1] = b.astype(LOW)
        xo_ref[...] = xv + 0.5 * _dot(s_ref[...], wd_ref[...])

    tok = pl.BlockSpec((ts, d), lambda t: (t, 0))
    hid = pl.BlockSpec((ts, fc), lambda t: (t, 0))
    tok_out = jax.ShapeDtypeStruct((s_len, d), F32)
    h_out = jax.ShapeDtypeStruct((s_len, d), LOW)
    hid_out = jax.ShapeDtypeStruct((s_len, fc), LOW)
    return _call(
        body,
        name=name,
        grid=(s_len // ts,),
        in_specs=[tok, pl.BlockSpec((1, d), lambda t: (0, 0)) if first else tok,
                  _resident(wgt), _resident(wut), _resident(wd)],
        out_specs=[tok] + ([tok] if first else []) + [hid, hid],
        out_shape=[tok_out] + ([h_out] if first else []) + [hid_out, hid_out],
        scratch_shapes=[pltpu.VMEM((ts, fc), LOW)],
        args=(x, g if first else h, wgt, wut, wd),
        exchange=exchange,
    )


def _slabs(width, slab=256):
    return [(c0, min(c0 + slab, width)) for c0 in range(0, width, slab)]


def _ffn_bwdw(df, a, b, h, wd, name, exchange=None):
    s_len, d = df.shape
    f_len = wd.shape[0]
    tm = _pick(f_len, WGRAD_ROW_CANDIDATES)
    tk = min(WGRAD_TOK_TILE, s_len)
    n_k = s_len // tk
    slabs = _slabs(tk, BWD_ROW_SLAB)
    chunked = n_k == 1

    def body(df_in, a_ref, b_ref, h_in, wd_ref, da_ref, db_ref, dwd_ref, dwg_ref, dwu_ref,
             s_ref, acc_d, acc_g, acc_u, *resident):
        k = pl.program_id(1)
        first = pl.program_id(0) == 0
        if chunked:
            df_ref, h_ref, load_sems = resident

            def load(j, which):
                r0, r1 = slabs[j]
                src, dst = ((df_in, df_ref), (h_in, h_ref))[which]
                return pltpu.make_async_copy(src.at[pl.ds(r0, r1 - r0)], dst.at[pl.ds(r0, r1 - r0)],
                                             load_sems.at[which, j])

            @pl.when(first)
            def _():
                for j in range(len(slabs)):
                    load(j, 0).start()
                    load(j, 1).start()
        else:
            df_ref, h_ref = df_in, h_in

        @pl.when(k == 0)
        def _():
            acc_d[...] = jnp.zeros_like(acc_d)
            acc_g[...] = jnp.zeros_like(acc_g)
            acc_u[...] = jnp.zeros_like(acc_u)

        def all_slabs(arriving):
            wdv = wd_ref[...]
            for j, (r0, r1) in enumerate(slabs):
                if arriving:
                    load(j, 0).wait()
                    load(j, 1).wait()
                ds = _dot_nt(df_ref[r0:r1, :], wdv)
                av = a_ref[r0:r1, :].astype(F32)
                bv = b_ref[r0:r1, :].astype(F32)
                sig = jax.nn.sigmoid(av)
                silu = av * sig
                s_ref[r0:r1, :] = (silu * bv).astype(LOW)
                da_ref[r0:r1, :] = (ds * bv * (sig * (1.0 + av * (1.0 - sig)))).astype(LOW)
                db_ref[r0:r1, :] = (ds * silu).astype(LOW)
                hv = h_ref[r0:r1, :]
                acc_d[...] += _dot_tn(s_ref[r0:r1, :], df_ref[r0:r1, :])
                acc_g[...] += _dot_tn(da_ref[r0:r1, :], hv)
                acc_u[...] += _dot_tn(db_ref[r0:r1, :], hv)

        if chunked:
            pl.when(first)(lambda: all_slabs(True))
            pl.when(jnp.logical_not(first))(lambda: all_slabs(False))
        else:
            all_slabs(False)

        @pl.when(k == n_k - 1)
        def _():
            dwd_ref[...] = acc_d[...].astype(LOW)
            dwg_ref[...] = acc_g[...].astype(LOW)
            dwu_ref[...] = acc_u[...].astype(LOW)

    hid = pl.BlockSpec((tk, tm), lambda i, k: (k, i))
    tok = pl.BlockSpec(memory_space=pl.ANY) if chunked else pl.BlockSpec((tk, d), lambda i, k: (k, 0))
    wrow = pl.BlockSpec((tm, d), lambda i, k: (i, 0))
    resident = [pltpu.VMEM((tk, d), LOW)] * 2 + [pltpu.SemaphoreType.DMA((2, len(slabs)))] if chunked else []
    return _call(
        body,
        name=name,
        grid=(f_len // tm, n_k),
        in_specs=[tok, hid, hid, tok, wrow],
        out_specs=[hid, hid, wrow, wrow, wrow],
        out_shape=[jax.ShapeDtypeStruct((s_len, f_len), LOW)] * 2 + [jax.ShapeDtypeStruct((f_len, d), LOW)] * 3,
        scratch_shapes=[pltpu.VMEM((tk, tm), LOW)] + [pltpu.VMEM((tm, d), F32)] * 3 + resident,
        args=(df, a, b, h, wd),
        exchange=exchange,
    )


def _ffn_dx(dxo, x, g, parts, name, exchange=None):
    s_len, d = x.shape
    ts = min(TOK_TILE, s_len)
    n_p = len(parts)

    def body(dxo_ref, x_ref, g_ref, *refs):
        dxi_ref, dg_ref = refs[4 * n_p:]

        @pl.when(pl.program_id(0) == 0)
        def _():
            dg_ref[...] = jnp.zeros_like(dg_ref)

        dh = None
        for p in range(n_p):
            da_ref, db_ref, wg_ref, wu_ref = refs[4 * p:4 * p + 4]
            part = _dot(da_ref[...], wg_ref[...]) + _dot(db_ref[...], wu_ref[...])
            dh = part if dh is None else dh + part
        dx, dgp = _rms_bwd(dh, x_ref[...], g_ref[...])
        dxi_ref[...] = dxo_ref[...] + dx
        dg_ref[...] += dgp

    tok = pl.BlockSpec((ts, d), lambda t: (t, 0))
    vec = pl.BlockSpec((1, d), lambda t: (0, 0))
    part_specs, part_args = [], []
    for da, db, wgt, wut in parts:
        hid = pl.BlockSpec((ts, da.shape[1]), lambda t: (t, 0))
        part_specs += [hid, hid, _resident(wgt), _resident(wut)]
        part_args += [da, db, wgt, wut]
    return _call(
        body,
        name=name,
        grid=(s_len // ts,),
        in_specs=[tok, tok, vec] + part_specs,
        out_specs=[tok, vec],
        out_shape=[jax.ShapeDtypeStruct((s_len, d), F32), jax.ShapeDtypeStruct((1, d), F32)],
        args=(dxo, x, g, *part_args),
        exchange=exchange,
    )


def _wgrad_tn(xm, ym, tn, stacked, name, exchange=None):
    s_len, m = xm.shape
    n = ym.shape[1]
    tk = min(WGRAD_TOK_TILE, s_len)
    n_k = s_len // tk

    def body(x_ref, y_ref, o_ref, acc):
        k = pl.program_id(1)

        @pl.when(k == 0)
        def _():
            acc[...] = jnp.zeros_like(acc)

        acc[...] += _dot_tn(x_ref[...].astype(LOW), y_ref[...].astype(LOW))

        @pl.when(k == n_k - 1)
        def _():
            o_ref[...] = acc[...].astype(LOW)

    if stacked:
        out_spec = pl.BlockSpec((None, m, tn), lambda j, k: (j, 0, 0))
        out_shape = jax.ShapeDtypeStruct((n // tn, m, tn), LOW)
    else:
        out_spec = pl.BlockSpec((m, tn), lambda j, k: (0, j))
        out_shape = jax.ShapeDtypeStruct((m, n), LOW)
    return _call(
        body,
        name=name,
        grid=(n // tn, n_k),
        in_specs=[pl.BlockSpec((tk, m), lambda j, k: (k, 0)), pl.BlockSpec((tk, tn), lambda j, k: (k, j))],
        out_specs=[out_spec],
        out_shape=[out_shape],
        scratch_shapes=[pltpu.VMEM((m, tn), F32)],
        args=(xm, ym),
        exchange=exchange,
    )


def _mix_parts(ext_ref, cw, ts, row0, with_pooled=True):
    dc = D_CONV

    def back(off, c0, c1):
        return ext_ref[HALO - off:HALO - off + ts, c0:c1]

    v, gate_b, gate_c = back(0, 0, dc), back(0, dc, 2 * dc), back(0, 2 * dc, 3 * dc)
    z0 = gate_c * v
    z1 = back(1, 2 * dc, 3 * dc) * back(1, 0, dc)
    z2 = back(2, 2 * dc, 3 * dc) * back(2, 0, dc)
    conv = cw[2:3, :] * z0 + cw[1:2, :] * z1 + cw[0:1, :] * z2
    rows = row0 + lax.broadcasted_iota(jnp.int32, (ts, 1), 0)
    pooled, inv_count = [], []
    for grp, w in enumerate(POOL_WINDOWS):
        inv = 1.0 / jnp.minimum(rows + 1, w).astype(F32)
        inv_count.append(inv)
        if with_pooled:
            c0 = 3 * dc + POOL_GC * grp
            u = back(0, c0, c0 + POOL_GC)
            acc = u
            for j in range(1, w):
                acc = acc + back(j, c0, c0 + POOL_GC)
            pooled.append(acc * inv - u)
    return v, gate_b, gate_c, z0, z1, z2, conv, pooled, inv_count


def _mix_fwd(x, g, w_in, conv_w, pool_w, pool_scale, w_out, name, exchange=None):
    s_len, d = x.shape
    n_blk, _, wcols = w_in.shape
    p_len = n_blk * wcols
    d_mix = w_out.shape[0]
    ts = min(MIX_TOK_TILE, s_len)
    dc = D_CONV

    def body(x_ref, g_ref, win_ref, cw_ref, pw_ref, ps_ref, wout_ref, x2_ref, h_ref, proj_ref, pooled_ref,
             ext_ref, cat_ref):
        t = pl.program_id(0)

        @pl.when(t == 0)
        def _():
            ext_ref[0:HALO, :] = jnp.zeros((HALO, p_len), F32)

        xv = x_ref[...]
        hb = (xv * _rms_scale(xv) * g_ref[...]).astype(LOW)
        h_ref[...] = hb
        for k in range(n_blk):
            ext_ref[HALO:HALO + ts, wcols * k:wcols * (k + 1)] = _dot(hb, win_ref[k])
        proj_ref[...] = ext_ref[HALO:HALO + ts, :]

        _, gate_b, _, _, _, _, conv, pooled, _ = _mix_parts(ext_ref, cw_ref[...], ts, t * ts)
        cat_ref[:, 0:dc] = (gate_b * conv).astype(LOW)
        for grp in range(len(POOL_WINDOWS)):
            c0 = POOL_GC * grp
            pooled_b = pooled[grp].astype(LOW)
            pooled_ref[:, c0:c0 + POOL_GC] = pooled_b
            lin = _dot(pooled_b, pw_ref[grp])
            cat_ref[:, dc + c0:dc + c0 + POOL_GC] = (lin * ps_ref[:, c0:c0 + POOL_GC]).astype(LOW)
        x2_ref[...] = xv + _dot(cat_ref[...], wout_ref[...])
        ext_ref[0:HALO, :] = ext_ref[ts:ts + HALO, :]

    tok = pl.BlockSpec((ts, d), lambda t: (t, 0))

    def whole(arr):
        return pl.BlockSpec(arr.shape, lambda t: (0,) * arr.ndim)

    return _call(
        body,
        name=name,
        grid=(s_len // ts,),
        in_specs=[tok, whole(g), _resident(w_in), whole(conv_w), whole(pool_w), whole(pool_scale), _resident(w_out)],
        out_specs=[tok, tok, pl.BlockSpec((ts, p_len), lambda t: (t, 0)),
                   pl.BlockSpec((ts, d_mix - dc), lambda t: (t, 0))],
        out_shape=[
            jax.ShapeDtypeStruct((s_len, d), F32),
            jax.ShapeDtypeStruct((s_len, d), LOW),
            jax.ShapeDtypeStruct((s_len, p_len), F32),
            jax.ShapeDtypeStruct((s_len, d_mix - dc), LOW),
        ],
        scratch_shapes=[pltpu.VMEM((ts + HALO, p_len), F32), pltpu.VMEM((ts, d_mix), LOW)],
        args=(x, g, w_in, conv_w, pool_w, pool_scale, w_out),
        exchange=exchange,
    )


def _mix_bwd(dx2, x, proj, pooled, g, w_in, conv_w, pool_w, pool_scale, w_out, name, exchange=None):
    s_len, d = x.shape
    n_blk, _, wcols = w_in.shape
    p_len = n_blk * wcols
    d_mix = w_out.shape[0]
    ts = min(MIX_TOK_TILE, s_len)
    n_t = s_len // ts
    dc = D_CONV
    n_grp = len(POOL_WINDOWS)

    def body(dx2_ref, x_ref, proj_ref, halo_ref, pooled_ref, g_ref, win_ref, cw_ref, pw_ref, ps_ref, wout_ref,
             dx_ref, dproj_ref, dwout_ref, dg_ref, dcw_ref, dpw_ref, dps_ref, df_ref,
             ext_ref, fut_ref, cat_ref, dwout_acc, win_a, win_b):
        i = pl.program_id(0)
        t = n_t - 1 - i

        @pl.when(i == 0)
        def _():
            win_a[...] = jnp.zeros_like(win_a)
            win_b[...] = jnp.zeros_like(win_b)
            dwout_acc[...] = jnp.zeros_like(dwout_acc)
            dg_ref[...] = jnp.zeros_like(dg_ref)
            dcw_ref[...] = jnp.zeros_like(dcw_ref)
            dpw_ref[...] = jnp.zeros_like(dpw_ref)
            dps_ref[...] = jnp.zeros_like(dps_ref)
            fut_ref[ts:ts + 2 * HALO, :] = jnp.zeros((2 * HALO, d_mix), F32)

        ext_ref[HALO:HALO + ts, :] = proj_ref[...]

        @pl.when(t == 0)
        def _():
            ext_ref[0:HALO, :] = jnp.zeros((HALO, p_len), F32)

        @pl.when(t > 0)
        def _():
            ext_ref[0:HALO, :] = halo_ref[...]

        cw = cw_ref[...]
        v, gate_b, gate_c, z0, z1, z2, conv, _, inv_count = _mix_parts(ext_ref, cw, ts, t * ts, with_pooled=False)
        dx2 = dx2_ref[...]
        dcat = _dot_nt(dx2.astype(LOW), wout_ref[...])

        dy_a = dcat[:, 0:dc]
        dconv = dy_a * gate_b
        fut_ref[0:ts, 0:dc] = dconv
        cat_ref[:, 0:dc] = (gate_b * conv).astype(LOW)
        dproj_ref[:, dc:2 * dc] = (dy_a * conv).astype(LOW)
        dcw_ref[2:3, :] += jnp.sum(dconv * z0, axis=0, keepdims=True)
        dcw_ref[1:2, :] += jnp.sum(dconv * z1, axis=0, keepdims=True)
        dcw_ref[0:1, :] += jnp.sum(dconv * z2, axis=0, keepdims=True)

        dpool = []
        for grp in range(n_grp):
            c0 = POOL_GC * grp
            pooled_b = pooled_ref[:, c0:c0 + POOL_GC]
            lin = _dot(pooled_b, pw_ref[grp])
            dy_b = dcat[:, dc + c0:dc + c0 + POOL_GC]
            scale = ps_ref[:, c0:c0 + POOL_GC]
            cat_ref[:, dc + c0:dc + c0 + POOL_GC] = (lin * scale).astype(LOW)
            dps_ref[:, c0:c0 + POOL_GC] += jnp.sum(dy_b * lin, axis=0, keepdims=True)
            dlin = (dy_b * scale).astype(LOW)
            dpw_ref[grp] += _dot_tn(pooled_b, dlin)
            dpool.append(_dot_nt(dlin, pw_ref[grp]))
            fut_ref[0:ts, dc + c0:dc + c0 + POOL_GC] = dpool[grp] * inv_count[grp]
        dwout_acc[...] += _dot_tn(cat_ref[...], dx2.astype(LOW))

        def ahead(off, c0, c1):
            return fut_ref[off:off + ts, c0:c1]

        dz = cw[2:3, :] * ahead(0, 0, dc) + cw[1:2, :] * ahead(1, 0, dc) + cw[0:1, :] * ahead(2, 0, dc)
        dproj_ref[:, 0:dc] = (dz * gate_c).astype(LOW)
        dproj_ref[:, 2 * dc:3 * dc] = (dz * v).astype(LOW)
        rows_ext = ts + HALO
        for grp, w in enumerate(POOL_WINDOWS):
            c0 = dc + POOL_GC * grp
            src, dst, other = fut_ref, win_a, win_b
            cols = slice(c0, c0 + POOL_GC)
            span = 1
            while span < w:
                dst[0:rows_ext, :] = src[0:rows_ext, cols] + src[span:span + rows_ext, cols]
                src, dst, other, cols = dst, other, dst, slice(0, POOL_GC)
                span *= 2
            dproj_ref[:, 2 * dc + c0:2 * dc + c0 + POOL_GC] = (src[0:ts, cols] - dpool[grp]).astype(LOW)

        dh = _dot_nt(dproj_ref[:, 0:wcols], win_ref[0])
        for k in range(1, n_blk):
            dh += _dot_nt(dproj_ref[:, wcols * k:wcols * (k + 1)], win_ref[k])
        dx, dgp = _rms_bwd(dh, x_ref[...], g_ref[...])
        dx = dx2 + dx
        dx_ref[...] = dx
        df_ref[...] = (0.5 * dx).astype(LOW)
        dg_ref[...] += dgp
        fut_ref[ts:ts + HALO, :] = fut_ref[0:HALO, :]

        @pl.when(i == n_t - 1)
        def _():
            dwout_ref[...] = dwout_acc[...].astype(LOW)

    tok = pl.BlockSpec((ts, d), lambda i: (n_t - 1 - i, 0))
    halo = pl.BlockSpec((HALO, p_len), lambda i: (jnp.maximum((n_t - 1 - i) * (ts // HALO) - 1, 0), 0))

    def whole(arr):
        return pl.BlockSpec(arr.shape, lambda i: (0,) * arr.ndim)

    return _call(
        body,
        name=name,
        grid=(n_t,),
        in_specs=[tok, tok, pl.BlockSpec((ts, p_len), lambda i: (n_t - 1 - i, 0)), halo,
                  pl.BlockSpec((ts, d_mix - dc), lambda i: (n_t - 1 - i, 0)),
                  whole(g), _resident(w_in), whole(conv_w), whole(pool_w), whole(pool_scale), _resident(w_out)],
        out_specs=[tok, pl.BlockSpec((ts, p_len), lambda i: (n_t - 1 - i, 0)),
                   whole(w_out),
                   whole(g), whole(conv_w), whole(pool_w), whole(pool_scale), tok],
        out_shape=[
            jax.ShapeDtypeStruct((s_len, d), F32),
            jax.ShapeDtypeStruct((s_len, p_len), LOW),
            jax.ShapeDtypeStruct((d_mix, d), LOW),
            jax.ShapeDtypeStruct(g.shape, F32),
            jax.ShapeDtypeStruct(conv_w.shape, F32),
            jax.ShapeDtypeStruct(pool_w.shape, F32),
            jax.ShapeDtypeStruct(pool_scale.shape, F32),
            jax.ShapeDtypeStruct((s_len, d), LOW),
        ],
        scratch_shapes=[pltpu.VMEM((ts + HALO, p_len), F32), pltpu.VMEM((ts + 2 * HALO, d_mix), F32),
                        pltpu.VMEM((ts, d_mix), LOW), pltpu.VMEM((d_mix, d), F32),
                        pltpu.VMEM((ts + 2 * HALO, POOL_GC), F32), pltpu.VMEM((ts + 2 * HALO, POOL_GC), F32)],
        args=(dx2, x, proj, proj, pooled, g, w_in, conv_w, pool_w, pool_scale, w_out),
        exchange=exchange,
    )


def _ffn_fwd_loss(x, h, wgt, wut, wd, g, target, name):
    s_len, d = x.shape
    fc = wd.shape[0]
    ts = min(TOK_TILE, s_len)

    def body(x_ref, h_ref, wg_ref, wu_ref, wd_ref, g_ref, tgt_ref,
             a_ref, b_ref, loss_ref, dx_ref, dg_ref, df_ref, s_ref):
        @pl.when(pl.program_id(0) == 0)
        def _():
            loss_ref[...] = jnp.zeros_like(loss_ref)
            dg_ref[...] = jnp.zeros_like(dg_ref)

        hb = h_ref[...]
        for c0, c1 in _slabs(fc):
            a = _dot_nt(hb, wg_ref[c0:c1, :])
            b = _dot_nt(hb, wu_ref[c0:c1, :])
            s_ref[:, c0:c1] = (a * jax.nn.sigmoid(a) * b).astype(LOW)
            a_ref[:, c0:c1] = a.astype(LOW)
            b_ref[:, c0:c1] = b.astype(LOW)
        xv = x_ref[...] + 0.5 * _dot(s_ref[...], wd_ref[...])
        gv = g_ref[...]
        err = xv * _rms_scale(xv) * gv - tgt_ref[...]
        loss_ref[...] += 0.5 * jnp.sum(jnp.mean(err * err, axis=-1, keepdims=True), axis=0, keepdims=True)
        dx, dgp = _rms_bwd(err * (1.0 / d), xv, gv)
        dx_ref[...] = dx
        df_ref[...] = (0.5 * dx).astype(LOW)
        dg_ref[...] += dgp

    tok = pl.BlockSpec((ts, d), lambda t: (t, 0))
    vec = pl.BlockSpec((1, d), lambda t: (0, 0))
    hid = pl.BlockSpec((ts, fc), lambda t: (t, 0))
    return pl.pallas_call(
        body,
        name=name,
        grid=(s_len // ts,),
        in_specs=[tok, tok, _resident(wgt), _resident(wut), _resident(wd), vec, tok],
        out_specs=[hid, hid, pl.BlockSpec((1, 128), lambda t: (0, 0)), tok, vec, tok],
        out_shape=[
            jax.ShapeDtypeStruct((s_len, fc), LOW),
            jax.ShapeDtypeStruct((s_len, fc), LOW),
            jax.ShapeDtypeStruct((1, 128), F32),
            jax.ShapeDtypeStruct((s_len, d), F32),
            jax.ShapeDtypeStruct((1, d), F32),
            jax.ShapeDtypeStruct((s_len, d), LOW),
        ],
        scratch_shapes=[pltpu.VMEM((ts, fc), LOW)],
        compiler_params=_params("arbitrary"),
    )(x, h, wgt, wut, wd, g, target)


def _row_tile(rows, cols, stack_bytes):
    budget = 20 * 1024 * 1024
    per_row = cols * (4 * 7 + stack_bytes)
    for tr in (rows, 512, 256, 176, 128, 64, 32, 16, 8):
        if rows % tr == 0 and tr % 8 == 0 and tr * per_row * 2 <= budget:
            return tr
    return rows


def _sum_stack(stack, name):
    n, r, c = stack.shape
    tr = _row_tile(r, c, n * stack.dtype.itemsize)

    def body(s_ref, o_ref):
        acc = s_ref[0].astype(F32)
        for k in range(1, n):
            acc = acc + s_ref[k].astype(F32)
        o_ref[...] = acc

    return pl.pallas_call(
        body,
        name=name,
        grid=(r // tr,),
        in_specs=[pl.BlockSpec((n, tr, c), lambda i: (0, i, 0))],
        out_specs=pl.BlockSpec((tr, c), lambda i: (i, 0)),
        out_shape=jax.ShapeDtypeStruct((r, c), F32),
        compiler_params=_params("arbitrary"),
    )(stack)


def _adamw_many(params, name):
    params = [(list(st) if isinstance(st, (list, tuple)) else [st], w, m, v) for st, w, m, v in params]
    stacks0, w0 = params[0][0], params[0][1]
    r, c = w0.shape
    n = stacks0[0].shape[0]
    n_st = len(stacks0)
    part_rows = [st.shape[1] for st in stacks0]
    assert sum(part_rows) == r
    assert all(w.shape == (r, c) and [st.shape for st in sts] == [st.shape for st in stacks0] for sts, w, _, _ in params)
    first_row = [sum(part_rows[:j]) for j in range(n_st)]
    tc = next((t for t in (512, 256, 128) if c % t == 0), c)
    c1 = 1.0 - ADAM_B1 ** ADAM_STEP
    c2 = 1.0 - ADAM_B2 ** ADAM_STEP
    n_in = n_st + 3

    def body(*refs):
        ins, outs = refs[:n_in * len(params)], refs[n_in * len(params):]
        for p in range(len(params)):
            s_refs = ins[n_in * p:n_in * p + n_st]
            w_ref, m_ref, v_ref = ins[n_in * p + n_st:n_in * (p + 1)]
            g_ref, d_ref, mo_ref, vo_ref = outs[4 * p:4 * p + 4]
            for s_ref, r0, nr in zip(s_refs, first_row, part_rows):
                gv = s_ref[0].astype(F32)
                for k in range(1, n):
                    gv = gv + s_ref[k].astype(F32)
                mn = ADAM_B1 * m_ref[r0:r0 + nr, :] + (1.0 - ADAM_B1) * gv
                vn = ADAM_B2 * v_ref[r0:r0 + nr, :] + (1.0 - ADAM_B2) * (gv * gv)
                g_ref[r0:r0 + nr, :] = gv
                mo_ref[r0:r0 + nr, :] = mn
                vo_ref[r0:r0 + nr, :] = vn
                d_ref[r0:r0 + nr, :] = -ADAM_LR * ((mn / c1) / (jnp.sqrt(vn / c2) + ADAM_EPS)
                                                   + ADAM_WD * w_ref[r0:r0 + nr, :])

    blk = pl.BlockSpec((r, tc), lambda i: (0, i))
    one_in = [pl.BlockSpec((n, nr, tc), lambda i: (0, 0, i)) for nr in part_rows] + [blk, blk, blk]
    res = pl.pallas_call(
        body,
        name=name,
        grid=(c // tc,),
        in_specs=one_in * len(params),
        out_specs=[blk] * (4 * len(params)),
        out_shape=[jax.ShapeDtypeStruct((r, c), F32)] * (4 * len(params)),
        compiler_params=_params("arbitrary"),
    )(*[arr for sts, w, m, v in params for arr in (*sts, w, m, v)])
    return [tuple(res[4 * p:4 * p + 4]) for p in range(len(params))]


def _adamw(stacks, w, m, v, name):
    return _adamw_many([(stacks, w, m, v)], name)[0]


def _to_sheet(parts):
    sheets, spans = [], []
    row = 0
    for p in parts:
        flat = p.reshape(-1).astype(F32)
        rows = -(-flat.shape[0] // 1024) * 8
        flat = jnp.pad(flat, (0, rows * 128 - flat.shape[0]))
        sheets.append(flat.reshape(rows, 128))
        spans.append((row, p.size, p.shape))
        row += rows
    return jnp.concatenate(sheets, axis=0), spans


def _from_sheet(sheet, spans):
    out = []
    for row, size, shape in spans:
        rows = -(-size // 1024) * 8
        out.append(sheet[row:row + rows].reshape(-1)[:size].reshape(shape))
    return out


def kernel(x, norm_ffn1, ffn1_w_gate, ffn1_w_up, ffn1_w_down, norm_mix, w_in, conv_w, pool_w, pool_scale, w_out, norm_ffn2, ffn2_w_gate, ffn2_w_up, ffn2_w_down, norm_final, loss_target, m_norm_ffn1, m_ffn1_w_gate, m_ffn1_w_up, m_ffn1_w_down, m_norm_mix, m_w_in, m_conv_w, m_pool_w, m_pool_scale, m_w_out, m_norm_ffn2, m_ffn2_w_gate, m_ffn2_w_up, m_ffn2_w_down, m_norm_final, v_norm_ffn1, v_ffn1_w_gate, v_ffn1_w_up, v_ffn1_w_down, v_norm_mix, v_w_in, v_conv_w, v_pool_w, v_pool_scale, v_w_out, v_norm_ffn2, v_ffn2_w_gate, v_ffn2_w_up, v_ffn2_w_down, v_norm_final):
    me = 4 * lax.axis_index("x") + 2 * lax.axis_index("y") + lax.axis_index("c")
    xs, tgt = x[0], loss_target[0]
    s_len, d = xs.shape
    f_shard = ffn1_w_down.shape[1]
    conv_shard = conv_w.shape[2]

    def low_t(wt):
        return wt[0].T.astype(LOW)

    def by_dev(gw):
        return gw.reshape(N_DEV, -1, d)

    conv_tile = jnp.zeros((8, 128), F32).at[0:conv_w.shape[1], 0:conv_shard].set(conv_w[0])
    pool_w_low = pool_w[0].astype(LOW)

    rows_a = -(-f_shard // 64) * 32

    def parts_of(w_gate, w_up, w_down):
        shards = [low_t(w_gate), low_t(w_up), w_down[0].astype(LOW)]
        return [(s, 0, rows_a) for s in shards], [(s, rows_a, f_shard - rows_a) for s in shards]

    def rows_flat(stacks):
        return [st.reshape(-1, d) for st in stacks]

    def gather(shards):
        return _Exchange(shards, [False] * len(shards), relay_at=0.6)

    def scatter(pairs):
        return _Exchange(pairs, [True] * len(pairs), among_chips=True)

    w1a_shards, w1b_shards = parts_of(ffn1_w_gate, ffn1_w_up, ffn1_w_down)
    w2a_shards, w2b_shards = parts_of(ffn2_w_gate, ffn2_w_up, ffn2_w_down)

    wg1a, wu1a, wd1a = rows_flat(_exchange(w1a_shards, [False] * 3, "gather_ffn1_a", relay=True))
    xa, h1, a1a, b1a, *w1b = _ffn_fwd(xs, norm_ffn1, None, wg1a, wu1a, wd1a, "ffn1_fwd_a",
                                      exchange=gather(w1b_shards))
    wg1b, wu1b, wd1b = rows_flat(w1b)
    x1, a1b, b1b, w_in_full, w_out_full, conv_tiles = _ffn_fwd(
        xa, None, h1, wg1b, wu1b, wd1b, "ffn1_fwd_b",
        exchange=gather([w_in[0].astype(LOW), w_out[0].astype(LOW), conv_tile]))
    w_out_full = w_out_full.reshape(-1, d)
    conv_full = jnp.concatenate([conv_tiles[k, 0:conv_w.shape[1], 0:conv_shard] for k in range(N_DEV)], axis=1)
    x2, h2, proj, pooled, *w2a = _mix_fwd(x1, norm_mix, w_in_full, conv_full, pool_w_low, pool_scale, w_out_full,
                                          "mix_fwd", exchange=gather(w2a_shards))
    wg2a, wu2a, wd2a = rows_flat(w2a)
    xb, h3, a2a, b2a, *w2b = _ffn_fwd(x2, norm_ffn2, None, wg2a, wu2a, wd2a, "ffn2_fwd_a",
                                      exchange=gather(w2b_shards))
    wg2b, wu2b, wd2b = rows_flat(w2b)
    a2b, b2b, loss_row, dx3, dg_final, df3 = _ffn_fwd_loss(
        xb, h3, wg2b, wu2b, wd2b, norm_final.reshape(1, d), tgt, "ffn2_fwd_b_loss")

    da2a, db2a, dwd_a, dwg_a, dwu_a = _ffn_bwdw(df3, a2a, b2a, h3, wd2a, "ffn2_bwdw_a")
    da2b, db2b, dwd_b, dwg_b, dwu_b = _ffn_bwdw(df3, a2b, b2b, h3, wd2b, "ffn2_bwdw_b")
    pairs = _pair_sum([by_dev(dwg_a), by_dev(dwu_a), by_dev(dwd_a), by_dev(dwg_b), by_dev(dwu_b), by_dev(dwd_b)],
                      "pair_sum_ffn2")
    dx2, dg_ffn2, *got_2b = _ffn_dx(dx3, x2, norm_ffn2, [(da2a, db2a, wg2a, wu2a), (da2b, db2b, wg2b, wu2b)],
                                    "ffn2_dx", exchange=scatter(pairs[3:]))
    dx1, dproj, dw_out, dg_mix, dconv, dpool_w, dpool_scale, df1, *got_2a = _mix_bwd(
        dx2, x1, proj, pooled, norm_mix, w_in_full, conv_full, pool_w_low, pool_scale, w_out_full, "mix_bwd",
        exchange=scatter(pairs[:3]))
    small_parts = [dg_mix, dg_ffn2, dg_final, dconv, dpool_w, dpool_scale, loss_row]
    small_sheet, spans = _to_sheet(small_parts)
    dw_in, got_small = _wgrad_tn(h2, dproj, W_IN_SHARD, True, "w_in_wgrad",
                                 exchange=_Exchange([small_sheet], [False], relay_at=0.5))
    pairs = _pair_sum([dw_in, by_dev(dw_out)], "pair_sum_mix")
    da1a, db1a, dwd, dwg, dwu, got_in, got_out = _ffn_bwdw(
        df1, a1a, b1a, h1, wd1a, "ffn1_bwdw_a", exchange=scatter(pairs))
    pairs = _pair_sum([by_dev(dwg), by_dev(dwu), by_dev(dwd)], "pair_sum_ffn1_a")
    da1b, db1b, dwd, dwg, dwu, *got_1a = _ffn_bwdw(df1, a1b, b1b, h1, wd1b, "ffn1_bwdw_b", exchange=scatter(pairs))
    pairs = _pair_sum([by_dev(dwg), by_dev(dwu), by_dev(dwd)], "pair_sum_ffn1_b")
    dx0, dg_ffn1, *got_1b = _ffn_dx(dx1, xs, norm_ffn1, [(da1a, db1a, wg1a, wu1a), (da1b, db1b, wg1b, wu1b)],
                                    "ffn1_dx", exchange=scatter(pairs))
    (got_n1,) = _exchange([dg_ffn1.reshape(8, 128)], [False], "gather_dnorm_ffn1")

    outs = {}

    def update(name, stacks, w, m, v):
        outs[name] = _adamw(stacks, w[0], m[0], v[0], "adamw_" + name)

    def update_ffn(prefix, got_a, got_b, gate, up, down):
        res = _adamw_many(
            [([got_a[j], got_b[j]], *[(t[0].T if j < 2 else t[0]) for t in wmv]) for j, wmv in enumerate((gate, up, down))],
            "adamw_" + prefix)
        outs[prefix + "_w_gate"] = tuple(r.T for r in res[0])
        outs[prefix + "_w_up"] = tuple(r.T for r in res[1])
        outs[prefix + "_w_down"] = res[2]

    update_ffn("ffn1", got_1a, got_1b, (ffn1_w_gate, m_ffn1_w_gate, v_ffn1_w_gate),
               (ffn1_w_up, m_ffn1_w_up, v_ffn1_w_up), (ffn1_w_down, m_ffn1_w_down, v_ffn1_w_down))
    update_ffn("ffn2", got_2a, got_2b, (ffn2_w_gate, m_ffn2_w_gate, v_ffn2_w_gate),
               (ffn2_w_up, m_ffn2_w_up, v_ffn2_w_up), (ffn2_w_down, m_ffn2_w_down, v_ffn2_w_down))
    update("w_in", got_in, w_in, m_w_in, v_w_in)
    update("w_out", got_out, w_out, m_w_out, v_w_out)

    g_small = _from_sheet(_sum_stack(got_small, "sum_small"), spans)
    g_norm_ffn1 = _sum_stack(got_n1, "sum_dnorm_ffn1").reshape(norm_ffn1.shape)
    g_conv = lax.dynamic_slice_in_dim(g_small[3], me * conv_shard, conv_shard, axis=1)
    row = (1, d)
    norms = [("norm_ffn1", g_norm_ffn1, norm_ffn1, m_norm_ffn1, v_norm_ffn1),
             ("norm_mix", g_small[0], norm_mix, m_norm_mix, v_norm_mix),
             ("norm_ffn2", g_small[1], norm_ffn2, m_norm_ffn2, v_norm_ffn2),
             ("norm_final", g_small[2], norm_final, m_norm_final, v_norm_final)]
    res = _adamw_many([(gn.reshape((1,) + row), wn.reshape(row), mn.reshape(row), vn.reshape(row))
                       for _, gn, wn, mn, vn in norms], "adamw_norms")
    for (nm, _, wn, _, _), r4 in zip(norms, res):
        outs[nm] = tuple(r.reshape(wn.shape) for r in r4)
    for nm, gn, wn, mn, vn in (("conv_w", g_conv, conv_w, m_conv_w, v_conv_w),
                               ("pool_w", g_small[4], pool_w, m_pool_w, v_pool_w),
                               ("pool_scale", g_small[5], pool_scale, m_pool_scale, v_pool_scale)):
        flat = (-1, wn.shape[-1])
        r4 = _adamw(gn.reshape(flat)[None], wn.reshape(flat), mn.reshape(flat), vn.reshape(flat), "adamw_" + nm)
        outs[nm] = tuple(r.reshape(wn.shape) for r in r4)

    loss = g_small[6][0, 0]
    order = ["norm_ffn1", "ffn1_w_gate", "ffn1_w_up", "ffn1_w_down", "norm_mix", "w_in", "conv_w", "pool_w",
             "pool_scale", "w_out", "norm_ffn2", "ffn2_w_gate", "ffn2_w_up", "ffn2_w_down", "norm_final"]
    big = {"ffn1_w_gate", "ffn1_w_up", "ffn1_w_down", "w_in", "w_out", "ffn2_w_gate", "ffn2_w_up", "ffn2_w_down"}

    def leaf(nm, j):
        val = outs[nm][j]
        return val[None] if nm in big else val

    return (loss, dx0[None],
            *[leaf(nm, 0) for nm in order], *[leaf(nm, 1) for nm in order],
            *[leaf(nm, 2) for nm in order], *[leaf(nm, 3) for nm in order])
```
